```python
import math
import jax
import jax.numpy as jnp
from jax import lax
import numpy as np

D_MODEL = 1024
BATCH = 2
SEQ = 8192
DEPTH = 2
DEC_BATCH = 128
DEC_SEQ = 1
PAST_LEN = 2048
PAGE_SIZE = 128

N_A_LAYERS = DEPTH // 2
N_B_LAYERS = DEPTH - N_A_LAYERS
RWKV_HEAD = 64
RWKV_HEADS = D_MODEL // RWKV_HEAD
DECAY_LORA = 64
AAA_LORA = 64
GATE_LORA = 160
GN_EPS = 64e-5
HEAD_DIM = 64
Q_HEADS = 16
KV_HEADS = 4
REP = Q_HEADS // KV_HEADS
DILATED_GROUPS = ((128, 1), (512, 4), (2048, 16))
N_GROUPS_B = len(DILATED_GROUPS)
WINDOW_MAX = max(w for w, _ in DILATED_GROUPS)
ROT_DIM = HEAD_DIM // 4
ROPE_THETA = 500000.0
Q_BLOCK = 128
N_EXPERTS = 32
N_EXPERT_GROUPS = 4
EXPERTS_PER_GROUP = N_EXPERTS // N_EXPERT_GROUPS
TOP_K = 2
D_EXPERT = 512
DN_ALPHA = (2 * DEPTH) ** 0.25
DN_BETA = (8 * DEPTH) ** -0.25
LN_EPS = 1e-5

kernel_name = 'yoco_rwkv7_dilated_moe_step'


def layer_norm(x, g, b):
    xf = x.astype(jnp.float32)
    mu = xf.mean(-1, keepdims=True)
    var = jnp.square(xf - mu).mean(-1, keepdims=True)
    return ((xf - mu) * lax.rsqrt(var + LN_EPS) * g + b).astype(x.dtype)


def ada_mod(c, w, b, n):
    m = jax.nn.silu(c) @ w + b
    return [t[:, None, :] for t in jnp.split(m, n, axis=-1)]


def partial_rotary(x, pos):
    half = ROT_DIM // 2
    inv = ROPE_THETA ** (-jnp.arange(half, dtype=jnp.float32) * 2.0 / ROT_DIM)
    ang = pos.astype(jnp.float32)[:, None] * inv[None, :]
    shape = (1, ang.shape[0]) + (1,) * (x.ndim - 3) + (half,)
    cos = jnp.cos(ang).reshape(shape)
    sin = jnp.sin(ang).reshape(shape)
    x1 = x[..., :half].astype(jnp.float32)
    x2 = x[..., half:ROT_DIM].astype(jnp.float32)
    rot = jnp.concatenate([x1 * cos - x2 * sin, x2 * cos + x1 * sin], axis=-1).astype(x.dtype)
    return jnp.concatenate([rot, x[..., ROT_DIM:]], axis=-1)


def wkv7_step(S, inp):
    r, w, k, v, a, b = inp
    sa = jnp.einsum('bhij,bhj->bhi', S, a)
    S = S * w[:, :, None, :] + sa[..., None] * b[:, :, None, :] + v[..., None] * k[:, :, None, :]
    return S, jnp.einsum('bhij,bhj->bhi', S, r)


def rwkv7_time_mix(h, shift0, wkv0, mu, w_rkv, w0, w1, w2, a0, a1, a2, g1, g2,
                   k_k, k_a, r_k, lnx_g, lnx_b, w_o):
    B, T, D = h.shape
    H, N = RWKV_HEADS, RWKV_HEAD
    h_prev = jnp.concatenate([shift0[:, None, :].astype(h.dtype), h[:, :-1]], axis=1)
    xx = h_prev - h
    xr, xw, xk, xv, xa, xg = [h + xx * mu[i] for i in range(6)]
    r = xr @ w_rkv[0]
    k = xk @ w_rkv[1]
    v = xv @ w_rkv[2]
    w_log = -jax.nn.softplus(-(w0 + jnp.tanh(xw @ w1) @ w2).astype(jnp.float32)) - 0.5
    decay = jnp.exp(-jnp.exp(w_log))
    a_lr = jax.nn.sigmoid((a0 + (xa @ a1) @ a2).astype(jnp.float32))
    g = jax.nn.sigmoid(xg @ g1) @ g2

    def heads(t):
        return t.astype(jnp.float32).reshape(B, T, H, N)

    kk = heads(k * k_k)
    kk = kk / jnp.maximum(jnp.sqrt(jnp.sum(kk * kk, axis=-1, keepdims=True)), 1e-12)
    k_h = heads(k.astype(jnp.float32) * (1.0 + (a_lr - 1.0) * k_a))
    r_h, v_h, w_h, alr_h = heads(r), heads(v), heads(decay), heads(a_lr)
    seqs = tuple(jnp.moveaxis(t, 1, 0) for t in (r_h, w_h, k_h, v_h, -kk, kk * alr_h))
    S_T, y = lax.scan(wkv7_step, wkv0.astype(jnp.float32), seqs)
    y = jnp.moveaxis(y, 0, 1)
    ym = y.mean(-1, keepdims=True)
    yv = jnp.square(y - ym).mean(-1, keepdims=True)
    y = ((y - ym) * lax.rsqrt(yv + GN_EPS)).reshape(B, T, D) * lnx_g + lnx_b
    bonus = jnp.sum(r_h * k_h * r_k, axis=-1, keepdims=True) * v_h
    y = y + bonus.reshape(B, T, D)
    out = (y * g).astype(h.dtype) @ w_o
    return out, h[:, -1], S_T


def dilated_core(q, k_all, v_all, q_rows):
    scale = HEAD_DIM ** -0.5
    outs, lses = [], []
    for gi, (win, dil) in enumerate(DILATED_GROUPS):
        offs = jnp.arange(win // dil + 1, dtype=jnp.int32) * dil
        idx = q_rows[:, None] - offs[None, :]
        valid = idx >= 0
        idx = jnp.maximum(idx, 0)
        kg = k_all[:, idx].astype(jnp.float32)
        vg = v_all[:, idx].astype(jnp.float32)
        s = jnp.einsum('bqhrd,bqjhd->bqhrj', q[:, :, gi].astype(jnp.float32), kg) * scale
        s = jnp.where(valid[None, :, None, None, :], s, -jnp.inf)
        m = s.max(-1, keepdims=True)
        p = jnp.exp(s - m)
        l = p.sum(-1, keepdims=True)
        outs.append(jnp.einsum('bqhrj,bqjhd->bqhrd', p, vg) / l)
        lses.append(m[..., 0] + jnp.log(l[..., 0]))
    wgt = jax.nn.softmax(jnp.stack(lses), axis=0)
    return jnp.einsum('gbqhr,gbqhrd->bqhrd', wgt, jnp.stack(outs))


def dilated_attention(q, k_all, v_all, q_rows):
    B, T = q.shape[0], q.shape[1]
    if T % Q_BLOCK == 0 and T > Q_BLOCK:
        nb = T // Q_BLOCK
        qb = jnp.moveaxis(q.reshape((B, nb, Q_BLOCK) + q.shape[2:]), 1, 0)
        rows = q_rows.reshape(nb, Q_BLOCK)
        o = lax.map(lambda a: dilated_core(a[0], k_all, v_all, a[1]), (qb, rows))
        return jnp.moveaxis(o, 0, 1).reshape((B, T) + o.shape[3:])
    return dilated_core(q, k_all, v_all, q_rows)


def moe_ffn(h, router_w, router_b, w_in, w_out):
    B, T, D = h.shape
    n = B * T
    xf = h.reshape(n, D)
    aff = jax.nn.sigmoid((xf @ router_w).astype(jnp.float32))
    sel = (aff + router_b.astype(jnp.float32)).reshape(n, N_EXPERT_GROUPS, EXPERTS_PER_GROUP)
    grp = jnp.argmax(lax.top_k(sel, 2)[0].sum(-1), axis=-1)
    in_grp = jnp.take_along_axis(sel, grp[:, None, None], axis=1)[:, 0]
    loc = lax.top_k(in_grp, TOP_K)[1]
    e_idx = (grp[:, None] * EXPERTS_PER_GROUP + loc).astype(jnp.int32)
    gate = jnp.take_along_axis(aff, e_idx, axis=1)
    gate = gate / gate.sum(-1, keepdims=True)
    n_assign = n * TOP_K
    blk = Q_BLOCK if n_assign >= Q_BLOCK * N_EXPERTS else 16
    n_blocks = -(-n_assign // blk) + N_EXPERTS
    flat_e = e_idx.reshape(-1)
    flat_tok = jnp.arange(n_assign, dtype=jnp.int32) // TOP_K
    order = jnp.argsort(flat_e)
    e_sorted = flat_e[order]
    counts = jnp.bincount(flat_e, length=N_EXPERTS)
    padded = (counts + blk - 1) // blk * blk
    pad_end = jnp.cumsum(padded)
    pad_start = pad_end - padded
    start = jnp.cumsum(counts) - counts
    dest = pad_start[e_sorted] + jnp.arange(n_assign, dtype=jnp.int32) - start[e_sorted]
    slot_tok = jnp.full((n_blocks * blk,), n, jnp.int32).at[dest].set(flat_tok[order])
    slot_gate = jnp.zeros((n_blocks * blk,), jnp.float32).at[dest].set(gate.reshape(-1)[order])
    blk_exp = jnp.minimum(jnp.searchsorted(pad_end, jnp.arange(n_blocks, dtype=jnp.int32) * blk,
                                           side='right'), N_EXPERTS - 1)
    x_pad = jnp.concatenate([xf, jnp.zeros((1, D), xf.dtype)], axis=0)

    def expert_block(args):
        tok, e = args
        gt, up = jnp.split(x_pad[tok] @ w_in[e], 2, axis=-1)
        return (jax.nn.silu(gt) * up) @ w_out[e]

    y_slots = lax.map(expert_block, (slot_tok.reshape(n_blocks, blk), blk_exp))
    y_slots = y_slots.reshape(-1, D).astype(jnp.float32) * slot_gate[:, None]
    y = jnp.zeros((n + 1, D), jnp.float32).at[slot_tok].add(y_slots)[:n]
    return y.reshape(B, T, D).astype(h.dtype)


def trunk(x, c, pos, q_rows, wkv0, shift0, k_buf, v_buf, p):
    B, T, D = x.shape
    new_wkv, new_shift = [], []
    k_new = v_new = k_all = v_all = None
    for layer in range(DEPTH):
        sh, sc, gt = ada_mod(c, p['ada_w'][layer, 0], p['ada_b'][layer, 0], 3)
        hm = x * (1 + sc) + sh
        if layer < N_A_LAYERS:
            i = layer
            mix, last_h, S = rwkv7_time_mix(
                hm, shift0[i], wkv0[i], p['rwkv_mu'][i], p['rwkv_w_rkv'][i], p['rwkv_w0'][i],
                p['rwkv_w1'][i], p['rwkv_w2'][i], p['rwkv_a0'][i], p['rwkv_a1'][i], p['rwkv_a2'][i],
                p['rwkv_g1'][i], p['rwkv_g2'][i], p['rwkv_k_k'][i], p['rwkv_k_a'][i], p['rwkv_r_k'][i],
                p['rwkv_lnx_g'][i], p['rwkv_lnx_b'][i], p['rwkv_w_o'][i])
            new_wkv.append(S)
            new_shift.append(last_h)
        else:
            if layer == N_A_LAYERS:
                kv_sh, kv_sc = ada_mod(c, p['kv_ada_w'], p['kv_ada_b'], 2)
                kv = ((x * (1 + kv_sc) + kv_sh) @ p['w_kv']).reshape(B, T, 2, KV_HEADS, HEAD_DIM)
                k_new = partial_rotary(kv[:, :, 0], pos)
                v_new = kv[:, :, 1]
                k_all = jnp.concatenate([k_buf.astype(k_new.dtype), k_new], axis=1)
                v_all = jnp.concatenate([v_buf.astype(v_new.dtype), v_new], axis=1)
            j = layer - N_A_LAYERS
            q = (hm @ p['w_q'][j]).reshape(B, T, N_GROUPS_B, Q_HEADS, HEAD_DIM)
            q = partial_rotary(q, pos).reshape(B, T, N_GROUPS_B, KV_HEADS, REP, HEAD_DIM)
            o = dilated_attention(q, k_all, v_all, q_rows)
            mix = o.reshape(B, T, Q_HEADS * HEAD_DIM).astype(x.dtype) @ p['w_o_attn'][j]
        x = layer_norm(DN_ALPHA * x + gt * mix, p['ln_g'][layer, 0], p['ln_b'][layer, 0])
        sh2, sc2, gt2 = ada_mod(c, p['ada_w'][layer, 1], p['ada_b'][layer, 1], 3)
        ff = moe_ffn(x * (1 + sc2) + sh2, p['router_w'], p['router_b'],
                     p['moe_w_in'][layer], p['moe_w_out'][layer])
        x = layer_norm(DN_ALPHA * x + gt2 * ff, p['ln_g'][layer, 1], p['ln_b'][layer, 1])
    return x, jnp.stack(new_wkv), jnp.stack(new_shift), k_new, v_new


def setup_inputs(seed: int = 0) -> dict:
    key = jax.random.key(seed)
    kit = iter(list(jax.random.split(key, 48)))

    def nrm(shape, s):
        return jax.random.normal(next(kit), shape, jnp.float32) * s

    D = D_MODEL
    NA, NB = N_A_LAYERS, N_B_LAYERS
    wbuf = min(WINDOW_MAX, PAST_LEN)
    inp = {}
    inp['x_prompt'] = nrm((BATCH, SEQ, D), 1.0)
    inp['x_sample'] = nrm((DEC_BATCH, DEC_SEQ, D), 1.0)
    inp['state_wkv'] = nrm((NA, DEC_BATCH, RWKV_HEADS, RWKV_HEAD, RWKV_HEAD), 0.3)
    inp['state_shift'] = nrm((NA, DEC_BATCH, D), 1.0)
    inp['cache_k'] = nrm((DEC_BATCH, wbuf, KV_HEADS, HEAD_DIM), 1.0)
    inp['cache_v'] = nrm((DEC_BATCH, wbuf, KV_HEADS, HEAD_DIM), 1.0)
    inp['c_prompt'] = nrm((BATCH, D), 1.0)
    inp['c_sample'] = nrm((DEC_BATCH, D), 1.0)
    inp['ada_w'] = nrm((DEPTH, 2, D, 3 * D), 0.5 * D ** -0.5)
    inp['ada_b'] = nrm((DEPTH, 2, 3 * D), 0.01)
    inp['ln_g'] = 1.0 + nrm((DEPTH, 2, D), 0.02)
    inp['ln_b'] = nrm((DEPTH, 2, D), 0.02)
    inp['rwkv_mu'] = jax.random.uniform(next(kit), (NA, 6, D), jnp.float32)
    inp['rwkv_w_rkv'] = nrm((NA, 3, D, D), D ** -0.5)
    inp['rwkv_w0'] = -1.5 + nrm((NA, D), 0.5)
    inp['rwkv_w1'] = nrm((NA, D, DECAY_LORA), D ** -0.5)
    inp['rwkv_w2'] = nrm((NA, DECAY_LORA, D), 0.1 * DECAY_LORA ** -0.5)
    inp['rwkv_a0'] = nrm((NA, D), 0.1)
    inp['rwkv_a1'] = nrm((NA, D, AAA_LORA), D ** -0.5)
    inp['rwkv_a2'] = nrm((NA, AAA_LORA, D), 0.1 * AAA_LORA ** -0.5)
    inp['rwkv_g1'] = nrm((NA, D, GATE_LORA), D ** -0.5)
    inp['rwkv_g2'] = nrm((NA, GATE_LORA, D), GATE_LORA ** -0.5)
    inp['rwkv_k_k'] = 0.85 + nrm((NA, D), 0.02)
    inp['rwkv_k_a'] = 1.0 + nrm((NA, D), 0.02)
    inp['rwkv_r_k'] = nrm((NA, RWKV_HEADS, RWKV_HEAD), 0.1)
    inp['rwkv_lnx_g'] = 1.0 + nrm((NA, D), 0.02)
    inp['rwkv_lnx_b'] = nrm((NA, D), 0.02)
    inp['rwkv_w_o'] = nrm((NA, D, D), DN_BETA * D ** -0.5)
    inp['w_q'] = nrm((NB, D, N_GROUPS_B * Q_HEADS * HEAD_DIM), D ** -0.5)
    inp['w_kv'] = nrm((D, 2 * KV_HEADS * HEAD_DIM), D ** -0.5)
    inp['kv_ada_w'] = nrm((D, 2 * D), 0.5 * D ** -0.5)
    inp['kv_ada_b'] = nrm((2 * D,), 0.01)
    inp['w_o_attn'] = nrm((NB, Q_HEADS * HEAD_DIM, D), DN_BETA * (Q_HEADS * HEAD_DIM) ** -0.5)
    inp['router_w'] = nrm((D, N_EXPERTS), D ** -0.5)
    inp['router_b'] = nrm((N_EXPERTS,), 0.01)
    inp['moe_w_in'] = nrm((DEPTH, N_EXPERTS, D, 2 * D_EXPERT), D ** -0.5)
    inp['moe_w_out'] = nrm((DEPTH, N_EXPERTS, D_EXPERT, D), DN_BETA * D_EXPERT ** -0.5)
    return inp


def reference(x_prompt, x_sample, state_wkv, state_shift, cache_k, cache_v, c_prompt, c_sample,
              ada_w, ada_b, ln_g, ln_b, rwkv_mu, rwkv_w_rkv, rwkv_w0, rwkv_w1, rwkv_w2,
              rwkv_a0, rwkv_a1, rwkv_a2, rwkv_g1, rwkv_g2, rwkv_k_k, rwkv_k_a, rwkv_r_k,
              rwkv_lnx_g, rwkv_lnx_b, rwkv_w_o, w_q, w_kv, kv_ada_w, kv_ada_b, w_o_attn,
              router_w, router_b, moe_w_in, moe_w_out):
    p = {'ada_w': ada_w, 'ada_b': ada_b, 'ln_g': ln_g, 'ln_b': ln_b,
         'rwkv_mu': rwkv_mu, 'rwkv_w_rkv': rwkv_w_rkv, 'rwkv_w0': rwkv_w0, 'rwkv_w1': rwkv_w1,
         'rwkv_w2': rwkv_w2, 'rwkv_a0': rwkv_a0, 'rwkv_a1': rwkv_a1, 'rwkv_a2': rwkv_a2,
         'rwkv_g1': rwkv_g1, 'rwkv_g2': rwkv_g2, 'rwkv_k_k': rwkv_k_k, 'rwkv_k_a': rwkv_k_a,
         'rwkv_r_k': rwkv_r_k, 'rwkv_lnx_g': rwkv_lnx_g, 'rwkv_lnx_b': rwkv_lnx_b,
         'rwkv_w_o': rwkv_w_o, 'w_q': w_q, 'w_kv': w_kv, 'kv_ada_w': kv_ada_w,
         'kv_ada_b': kv_ada_b, 'w_o_attn': w_o_attn, 'router_w': router_w,
         'router_b': router_b, 'moe_w_in': moe_w_in, 'moe_w_out': moe_w_out}
    bp, tp = x_prompt.shape[0], x_prompt.shape[1]
    pos_p = jnp.arange(tp, dtype=jnp.int32)
    wkv_zero = jnp.zeros((N_A_LAYERS, bp, RWKV_HEADS, RWKV_HEAD, RWKV_HEAD), jnp.float32)
    shift_zero = jnp.zeros((N_A_LAYERS, bp, D_MODEL), x_prompt.dtype)
    kv_empty = jnp.zeros((bp, 0, KV_HEADS, HEAD_DIM), x_prompt.dtype)
    y_prompt, wkv_p, shift_p, k_p, v_p = trunk(x_prompt, c_prompt, pos_p, pos_p, wkv_zero,
                                               shift_zero, kv_empty, kv_empty, p)
    keep = min(WINDOW_MAX, tp)
    k_p = k_p[:, tp - keep:]
    v_p = v_p[:, tp - keep:]
    ts = x_sample.shape[1]
    wbuf = cache_k.shape[1]
    pos_s = PAST_LEN + jnp.arange(ts, dtype=jnp.int32)
    rows_s = wbuf + jnp.arange(ts, dtype=jnp.int32)
    y_sample, wkv_s, shift_s, k_s, v_s = trunk(x_sample, c_sample, pos_s, rows_s, state_wkv,
                                               state_shift, cache_k, cache_v, p)
    return (y_prompt, y_sample, wkv_p, shift_p, k_p, v_p, wkv_s, shift_s, k_s, v_s)
```

```python
import functools

import jax
import jax.numpy as jnp
from jax import lax
from jax.experimental import pallas as pl
from jax.experimental.pallas import tpu as pltpu

F32 = jnp.float32
BF16 = jnp.bfloat16
I32 = jnp.int32

D = 1024
HEADS = 16
HD = 64
LANES = 128
PAIRS = D // LANES
CHUNK = 64
N_EXPERTS = 32
EXPERTS_PER_GROUP = 8
N_EXPERT_GROUPS = 4
D_EXPERT = 512
MOE_BLK = 128
KV_HEADS = 4
REP = 4
KVW = KV_HEADS * HD
DILATED_GROUPS = ((128, 1), (512, 4), (2048, 16))
ATT_WIN = 128
PAST_LEN = 2048
ROT_DIM = 16
ROPE_THETA = 500000.0
DEPTH = 2
DN_ALPHA = (2 * DEPTH) ** 0.25
LN_EPS = 1e-5
GN_EPS = 64e-5
VMEM_LIMIT = 56 * 1024 * 1024


def _cp(sem):
    return pltpu.CompilerParams(dimension_semantics=sem, vmem_limit_bytes=VMEM_LIMIT)


def _bdot(a, b):
    return jnp.dot(a.astype(BF16), b.astype(BF16), preferred_element_type=F32)


def _bdot_nt(a, b):
    return lax.dot_general(a.astype(BF16), b.astype(BF16), (((1,), (1,)), ((), ())),
                           preferred_element_type=F32)


def _bdot_tn(a, b):
    return lax.dot_general(a.astype(BF16), b.astype(BF16), (((0,), (0,)), ((), ())),
                           preferred_element_type=F32)


def _split_dot(x, m):
    hi = x.astype(BF16)
    lo = (x - hi.astype(F32)).astype(BF16)
    return (jnp.dot(hi, m, preferred_element_type=F32) + jnp.dot(lo, m, preferred_element_type=F32))


def _head_ones():
    r = lax.broadcasted_iota(I32, (LANES, LANES), 0) // HD
    c = lax.broadcasted_iota(I32, (LANES, LANES), 1) // HD
    return (r == c).astype(BF16)


def _segsum(x, ones_bd):
    cols = [_split_dot(x[:, c * LANES:(c + 1) * LANES], ones_bd) for c in range(x.shape[1] // LANES)]
    return cols[0] if len(cols) == 1 else jnp.concatenate(cols, axis=1)


def _layer_norm(x, g, b):
    mu = jnp.mean(x, axis=-1, keepdims=True)
    xc = x - mu
    var = jnp.mean(xc * xc, axis=-1, keepdims=True)
    return xc * lax.rsqrt(var + LN_EPS) * g + b


def _sigmoid(x):
    return 1.0 / (1.0 + jnp.exp(-x))


def _to_pairs(ref, val):
    for p in range(PAIRS):
        ref[p] = val[:, p * LANES:(p + 1) * LANES]


def _from_pairs(ref):
    return jnp.concatenate([ref[p] for p in range(PAIRS)], axis=1)


def _to_rows(ref, val):
    for s in range(PAIRS):
        ref[:, s, :] = val[:, s * LANES:(s + 1) * LANES]


def _from_rows(ref):
    return jnp.concatenate([ref[:, s, :] for s in range(PAIRS)], axis=1)


def _ada_kernel(c_ref, w_ref, b_ref, o_ref):
    c = c_ref[...]
    o_ref[...] = _bdot(c * _sigmoid(c), w_ref[...]) + b_ref[...]


def _ada_linear(c, w, b, tn=512):
    s, _, n = w.shape
    m = c.shape[0]
    return pl.pallas_call(
        _ada_kernel,
        grid=(s, n // tn),
        in_specs=[pl.BlockSpec((m, D), lambda i, j: (0, 0)),
                  pl.BlockSpec((None, D, tn), lambda i, j: (i, 0, j)),
                  pl.BlockSpec((None, 1, tn), lambda i, j: (i, 0, j))],
        out_specs=pl.BlockSpec((None, m, tn), lambda i, j: (i, 0, j)),
        out_shape=jax.ShapeDtypeStruct((s, m, n), F32),
        compiler_params=_cp(("arbitrary", "arbitrary")),
        name="ada_linear",
    )(c, w, b.reshape(s, 1, n))


def _rwkv_pre_kernel(x_ref, sc_ref, sh_ref, prev_ref, mu_ref, wrkv_ref, w1_ref, w2_ref, a1_ref, a2_ref,
                     g1_ref, g2_ref, vec_ref,
                     r_ref, lw_ref, k_ref, v_ref, a_ref, b_ref, g_ref, hm_ref, carry_ref, *, seq_mode, tm):
    x = x_ref[...]
    hm = x * (1.0 + sc_ref[...]) + sh_ref[...]
    if seq_mode:
        @pl.when(pl.program_id(1) == 0)
        def _():
            carry_ref[...] = prev_ref[...]
        row = lax.broadcasted_iota(I32, hm.shape, 0)
        hprev = jnp.where(row == 0, carry_ref[...], pltpu.roll(hm, 1, axis=0))
        carry_ref[...] = hm[tm - 1:tm, :]
        hm_ref[...] = hm[tm - 1:tm, :]
    else:
        hprev = prev_ref[...]
        hm_ref[...] = hm
    xx = hprev - hm

    def mix(i):
        return hm + xx * mu_ref[i:i + 1, :]

    w0, a0, k_k, k_a = (vec_ref[i:i + 1, :] for i in range(4))
    r = _bdot(mix(0), wrkv_ref[0])
    k = _bdot(mix(2), wrkv_ref[1])
    v = _bdot(mix(3), wrkv_ref[2])
    wl = w0 + _bdot(jnp.tanh(_bdot(mix(1), w1_ref[...])), w2_ref[...])
    z = -wl
    softplus = jnp.maximum(z, 0.0) + jnp.log1p(jnp.exp(-jnp.abs(z)))
    lw = -jnp.exp(-softplus - 0.5)
    a_lr = _sigmoid(a0 + _bdot(_bdot(mix(4), a1_ref[...]), a2_ref[...]))
    g = _bdot(_sigmoid(_bdot(mix(5), g1_ref[...])), g2_ref[...])
    kk = k * k_k
    kk = kk / jnp.maximum(jnp.sqrt(_segsum(kk * kk, _head_ones())), 1e-12)
    kmod = k * (1.0 + (a_lr - 1.0) * k_a)
    outs = ((r_ref, r), (lw_ref, lw), (k_ref, kmod), (v_ref, v), (a_ref, -kk), (b_ref, kk * a_lr), (g_ref, g))
    for ref, val in outs:
        if seq_mode:
            _to_pairs(ref, val)
        else:
            ref[...] = val


def _rwkv_pre(x, sc, sh, prev, w, *, seq_mode, tm):
    bsz, t, _ = x.shape
    grid = (bsz, t // tm)
    row = pl.BlockSpec((None, tm, D), lambda b, i: (b, i, 0))
    per_b = pl.BlockSpec((None, 1, D), lambda b, i: (b, 0, 0))
    mod = per_b if seq_mode else row

    def const(shape):
        return pl.BlockSpec(shape, lambda b, i: (0,) * len(shape))

    if seq_mode:
        out_big = pl.BlockSpec((None, PAIRS, tm, LANES), lambda b, i: (b, 0, i, 0))
        big_shape = jax.ShapeDtypeStruct((bsz, PAIRS, t, LANES), F32)
        hm_spec, hm_shape = per_b, jax.ShapeDtypeStruct((bsz, 1, D), F32)
    else:
        out_big, big_shape = row, jax.ShapeDtypeStruct((bsz, t, D), F32)
        hm_spec, hm_shape = row, jax.ShapeDtypeStruct((bsz, t, D), F32)
    return pl.pallas_call(
        functools.partial(_rwkv_pre_kernel, seq_mode=seq_mode, tm=tm),
        grid=grid,
        in_specs=[row, mod, mod, mod, const((6, D)), const((3, D, D)), const((D, LANES)), const((LANES, D)),
                  const((D, LANES)), const((LANES, D)), const((D, 2 * LANES)), const((2 * LANES, D)),
                  const((4, D))],
        out_specs=[out_big] * 7 + [hm_spec],
        out_shape=[big_shape] * 7 + [hm_shape],
        scratch_shapes=[pltpu.VMEM((1, D), F32)],
        compiler_params=_cp(("arbitrary", "arbitrary")),
        name="rwkv_pre",
    )(x, sc, sh, prev, w["mu"], w["w_rkv"], w["w1"], w["w2"], w["a1"], w["a2"], w["g1"], w["g2"], w["vec"])


def _wkv_chunk_kernel(r_ref, lw_ref, k_ref, v_ref, a_ref, b_ref, y_ref, zf_ref, z_ref, *, n_pairs):
    c = pl.program_id(0)

    @pl.when(c == 0)
    def _():
        z_ref[...] = jnp.zeros_like(z_ref)

    L = CHUNK
    row_l = lax.broadcasted_iota(I32, (L, L), 0)
    col_l = lax.broadcasted_iota(I32, (L, L), 1)
    tri_incl = (row_l >= col_l).astype(BF16)
    lane = lax.broadcasted_iota(I32, (L, LANES), 1)
    head0 = lane < HD
    row = lax.broadcasted_iota(I32, (LANES, LANES), 0)
    col = lax.broadcasted_iota(I32, (LANES, LANES), 1)
    strict = row > col
    incl = row >= col
    eye = (row == col).astype(F32)

    def expand(x):
        return jnp.concatenate([jnp.where(head0, x, 0.0), jnp.where(head0, 0.0, x)], axis=0)

    def pair_body(idx, carry):
        b_i = idx // PAIRS
        p_i = idx % PAIRS
        r = r_ref[b_i, p_i]
        lw = lw_ref[b_i, p_i]
        k = k_ref[b_i, p_i]
        v = v_ref[b_i, p_i]
        a = a_ref[b_i, p_i]
        b = b_ref[b_i, p_i]
        cum = _split_dot_left(tri_incl, lw)
        cum_l = cum[L - 1:L, :]
        inv = jnp.exp(-cum)
        a_e = expand(a * jnp.exp(cum - lw))
        r_e = expand(r * jnp.exp(cum))
        b_e = expand(b * inv)
        k_e = expand(k * inv)
        tail = jnp.exp(cum_l - cum)
        bd_e = expand(b * tail)
        kd_e = expand(k * tail)
        v_e = expand(v)
        gm = _bdot_nt(jnp.concatenate([a_e, r_e], axis=0), jnp.concatenate([b_e, k_e], axis=0))
        m_ab = jnp.where(strict, gm[:LANES, :LANES], 0.0)
        m_ak = jnp.where(strict, gm[:LANES, LANES:], 0.0)
        m_rb = jnp.where(incl, gm[LANES:, :LANES], 0.0)
        m_rk = jnp.where(incl, gm[LANES:, LANES:], 0.0)
        tinv = eye + m_ab
        pw = m_ab
        for _ in range(5):
            pw = _bdot(pw, pw)
            tinv = tinv + _bdot(pw, tinv)
        mv = _bdot(jnp.concatenate([m_ak, m_rk], axis=0), v_e)
        taw = _bdot(tinv, jnp.concatenate([a_e, mv[:LANES]], axis=1))
        qy = _bdot(m_rb, taw)
        q = r_e + qy[:, :LANES]
        y0 = qy[:, LANES:] + mv[LANES:]
        pp = _bdot_tn(bd_e, taw)
        phi = eye * jnp.exp(cum_l) + pp[:, :LANES]
        psi = pp[:, LANES:] + _bdot_tn(kd_e, v_e)
        z0 = z_ref[idx]
        zq = _bdot(jnp.concatenate([phi, q], axis=0), z0)
        z_ref[idx] = zq[:LANES] + psi
        y_e = zq[LANES:] + y0
        y_ref[b_i, p_i] = y_e[:L] + y_e[L:]
        return carry

    lax.fori_loop(0, n_pairs, pair_body, 0)

    @pl.when(c == pl.num_programs(0) - 1)
    def _():
        zf_ref[...] = z_ref[...]


def _split_dot_left(m, x):
    hi = x.astype(BF16)
    lo = (x - hi.astype(F32)).astype(BF16)
    return (jnp.dot(m, hi, preferred_element_type=F32) + jnp.dot(m, lo, preferred_element_type=F32))


def _wkv_chunked(r, lw, k, v, a, b):
    bsz, _, t, _ = r.shape
    n_pairs = bsz * PAIRS
    blk = pl.BlockSpec((bsz, PAIRS, CHUNK, LANES), lambda c: (0, 0, c, 0))
    return pl.pallas_call(
        functools.partial(_wkv_chunk_kernel, n_pairs=n_pairs),
        grid=(t // CHUNK,),
        in_specs=[blk] * 6,
        out_specs=[blk, pl.BlockSpec((n_pairs, LANES, LANES), lambda c: (0, 0, 0))],
        out_shape=[jax.ShapeDtypeStruct(r.shape, F32), jax.ShapeDtypeStruct((n_pairs, LANES, LANES), F32)],
        scratch_shapes=[pltpu.VMEM((n_pairs, LANES, LANES), F32)],
        compiler_params=_cp(("arbitrary",)),
        name="wkv_chunked",
    )(r, lw, k, v, a, b)


def _wkv_step_kernel(s_ref, vb_ref, r_ref, lw_ref, k_ref, a_ref, b_ref, so_ref, y_ref, *, tb):
    lane = lax.broadcasted_iota(I32, (HD, LANES), 1)

    def body(idx, ytile):
        n = idx // HEADS
        h = idx % HEADS
        s = s_ref[n, h]
        sa = jnp.sum(s * a_ref[n, h], axis=-1, keepdims=True)
        s_new = s * jnp.exp(lw_ref[n, h]) + sa * b_ref[n, h] + vb_ref[n, h] * k_ref[n, h]
        so_ref[n, h] = s_new
        y = jnp.sum(s_new * r_ref[n, h], axis=-1, keepdims=True)
        return jnp.where(lane == idx, y, ytile)

    y_ref[...] = lax.fori_loop(0, tb * HEADS, body, jnp.zeros((HD, LANES), F32))


def _wkv_step(state, v, r, lw, k, a, b, tb=8):
    bsz = state.shape[0]
    vb = jnp.broadcast_to(v.reshape(bsz, HEADS, HD, 1), (bsz, HEADS, HD, HD))
    rows = [t.reshape(bsz, HEADS, 1, HD) for t in (r, lw, k, a, b)]
    big = pl.BlockSpec((tb, HEADS, HD, HD), lambda i: (i, 0, 0, 0))
    vec = pl.BlockSpec((tb, HEADS, 1, HD), lambda i: (i, 0, 0, 0))
    s_new, yt = pl.pallas_call(
        functools.partial(_wkv_step_kernel, tb=tb),
        grid=(bsz // tb,),
        in_specs=[big, big] + [vec] * 5,
        out_specs=[big, pl.BlockSpec((None, HD, LANES), lambda i: (i, 0, 0))],
        out_shape=[jax.ShapeDtypeStruct(state.shape, F32), jax.ShapeDtypeStruct((bsz // tb, HD, LANES), F32)],
        compiler_params=_cp(("arbitrary",)),
        name="wkv_step",
    )(state, vb, *rows)
    y = yt.reshape(bsz // tb, HD, tb, HEADS).transpose(0, 2, 3, 1).reshape(bsz, D)
    return s_new, y


def _rwkv_post_kernel(y_ref, r_ref, k_ref, v_ref, g_ref, x_ref, gt_ref, vec_ref, wo_ref, o_ref, *, pairs):
    load = _from_pairs if pairs else (lambda ref: ref[...])
    y, r, k, v, g = (load(ref) for ref in (y_ref, r_ref, k_ref, v_ref, g_ref))
    r_k, lnx_g, lnx_b, ln_g, ln_b = (vec_ref[i:i + 1, :] for i in range(5))
    ones_bd = _head_ones()
    ym = _segsum(y, ones_bd) * (1.0 / HD)
    yc = y - ym
    yv = _segsum(yc * yc, ones_bd) * (1.0 / HD)
    yn = yc * lax.rsqrt(yv + GN_EPS) * lnx_g + lnx_b
    bonus = _segsum(r * k * r_k, ones_bd) * v
    mix = _bdot((yn + bonus) * g, wo_ref[...])
    o_ref[...] = _layer_norm(DN_ALPHA * x_ref[...] + gt_ref[...] * mix, ln_g, ln_b)


def _rwkv_post(y, r, k, v, g, x, gt, vec, w_o, *, pairs, tm):
    bsz, t, _ = x.shape
    row = pl.BlockSpec((None, tm, D), lambda b, i: (b, i, 0))
    big = pl.BlockSpec((None, PAIRS, tm, LANES), lambda b, i: (b, 0, i, 0)) if pairs else row
    mod = pl.BlockSpec((None, 1, D), lambda b, i: (b, 0, 0)) if pairs else row
    return pl.pallas_call(
        functools.partial(_rwkv_post_kernel, pairs=pairs),
        grid=(bsz, t // tm),
        in_specs=[big] * 5 + [row, mod, pl.BlockSpec((5, D), lambda b, i: (0, 0)),
                              pl.BlockSpec((D, D), lambda b, i: (0, 0))],
        out_specs=row,
        out_shape=jax.ShapeDtypeStruct(x.shape, F32),
        compiler_params=_cp(("arbitrary", "arbitrary")),
        name="rwkv_post",
    )(y, r, k, v, g, x, gt, vec, w_o)


def _top2(v):
    io = lax.broadcasted_iota(I32, v.shape, 0)
    m1 = jnp.max(v, axis=0, keepdims=True)
    i1 = jnp.min(jnp.where(v == m1, io, EXPERTS_PER_GROUP), axis=0, keepdims=True)
    v2 = jnp.where(io == i1, -jnp.inf, v)
    m2 = jnp.max(v2, axis=0, keepdims=True)
    i2 = jnp.min(jnp.where(v2 == m2, io, EXPERTS_PER_GROUP), axis=0, keepdims=True)
    return m1 + m2, i1, i2


def _moe_route_kernel(x_ref, sc_ref, sh_ref, rw_ref, rb_ref, h_ref, idx_ref, gate_ref, gmat_ref, cnt_ref,
                      carry_ref, *, tm, rows_out):
    @pl.when((pl.program_id(0) == 0) & (pl.program_id(1) == 0))
    def _():
        carry_ref[...] = jnp.zeros_like(carry_ref)

    h = x_ref[...] * (1.0 + sc_ref[...]) + sh_ref[...]
    if rows_out:
        _to_rows(h_ref, h)
    else:
        h_ref[...] = h
    aff = _sigmoid(_bdot_nt(rw_ref[...], h))
    sel = aff + rb_ref[...]
    best = gi = i1 = i2 = None
    for g in range(N_EXPERT_GROUPS):
        sc, j1, j2 = _top2(sel[g * EXPERTS_PER_GROUP:(g + 1) * EXPERTS_PER_GROUP, :])
        if g == 0:
            best, gi, i1, i2 = sc, jnp.zeros_like(j1), j1, j2
        else:
            upd = sc > best
            best = jnp.where(upd, sc, best)
            gi = jnp.where(upd, g, gi)
            i1 = jnp.where(upd, j1, i1)
            i2 = jnp.where(upd, j2, i2)
    e0 = gi * EXPERTS_PER_GROUP + i1
    e1 = gi * EXPERTS_PER_GROUP + i2
    io = lax.broadcasted_iota(I32, (N_EXPERTS, tm), 0)
    oh0 = io == e0
    oh1 = io == e1
    a0 = jnp.sum(jnp.where(oh0, aff, 0.0), axis=0, keepdims=True)
    a1 = jnp.sum(jnp.where(oh1, aff, 0.0), axis=0, keepdims=True)
    den = a0 + a1
    g0 = a0 / den
    g1 = a1 / den
    oh = jnp.where(oh0 | oh1, 1.0, 0.0)
    tr = lax.broadcasted_iota(I32, (tm, tm), 0)
    tc = lax.broadcasted_iota(I32, (tm, tm), 1)
    before = jnp.dot(oh.astype(BF16), (tr < tc).astype(BF16), preferred_element_type=F32) + carry_ref[...]
    rank0 = jnp.sum(jnp.where(oh0, before, 0.0), axis=0, keepdims=True)
    rank1 = jnp.sum(jnp.where(oh1, before, 0.0), axis=0, keepdims=True)
    carry_ref[...] = carry_ref[...] + jnp.sum(oh, axis=1, keepdims=True)
    zi = jnp.zeros((4, tm), I32)
    idx_ref[...] = jnp.concatenate([e0, e1, rank0.astype(I32), rank1.astype(I32), zi], axis=0)
    gate_ref[...] = jnp.concatenate([g0, g1, jnp.zeros((6, tm), F32)], axis=0)
    gmat_ref[...] = jnp.where(oh0, g0, 0.0) + jnp.where(oh1, g1, 0.0)
    cnt_ref[...] = jnp.broadcast_to(carry_ref[...], cnt_ref.shape)


def _moe_route(x, sc, sh, router_w_t, router_b, *, per_b_mod, rows_out, tm):
    bsz, t, _ = x.shape
    n = bsz * t
    nt = t // tm
    row = pl.BlockSpec((None, tm, D), lambda b, i: (b, i, 0))
    mod = pl.BlockSpec((None, 1, D), lambda b, i: (b, 0, 0)) if per_b_mod else row
    if rows_out:
        h_spec = pl.BlockSpec((tm, PAIRS, LANES), lambda b, i: (b * nt + i, 0, 0))
        h_shape = jax.ShapeDtypeStruct((n, PAIRS, LANES), F32)
    else:
        h_spec, h_shape = row, jax.ShapeDtypeStruct(x.shape, F32)
    tok = lambda rows: pl.BlockSpec((rows, tm), lambda b, i: (0, b * nt + i))
    return pl.pallas_call(
        functools.partial(_moe_route_kernel, tm=tm, rows_out=rows_out),
        grid=(bsz, nt),
        in_specs=[row, mod, mod, pl.BlockSpec((N_EXPERTS, D), lambda b, i: (0, 0)),
                  pl.BlockSpec((N_EXPERTS, 1), lambda b, i: (0, 0))],
        out_specs=[h_spec, tok(8), tok(8), tok(N_EXPERTS), pl.BlockSpec((N_EXPERTS, LANES), lambda b, i: (0, 0))],
        out_shape=[h_shape, jax.ShapeDtypeStruct((8, n), I32), jax.ShapeDtypeStruct((8, n), F32),
                   jax.ShapeDtypeStruct((N_EXPERTS, n), F32), jax.ShapeDtypeStruct((N_EXPERTS, LANES), F32)],
        scratch_shapes=[pltpu.VMEM((N_EXPERTS, 1), F32)],
        compiler_params=_cp(("arbitrary", "arbitrary")),
        name="moe_route",
    )(x, sc, sh, router_w_t, router_b)


def _row_copy(src, dst, sem):
    return pltpu.make_async_copy(src, dst, sem)


def _moe_dispatch_kernel(d0_ref, d1_ref, h_ref, init_ref, xs_ref, sem, *, tm):
    del init_ref
    base = pl.program_id(0) * tm

    def issue(i, c):
        _row_copy(h_ref.at[i], xs_ref.at[d0_ref[base + i]], sem.at[0]).start()
        _row_copy(h_ref.at[i], xs_ref.at[d1_ref[base + i]], sem.at[1]).start()
        return c

    lax.fori_loop(0, tm, issue, 0)

    def drain(i, c):
        _row_copy(h_ref.at[i], xs_ref.at[d0_ref[base + i]], sem.at[0]).wait()
        _row_copy(h_ref.at[i], xs_ref.at[d1_ref[base + i]], sem.at[1]).wait()
        return c

    lax.fori_loop(0, tm, drain, 0)


def _moe_dispatch(h_rows, dest0, dest1, n_slots, tm):
    n = h_rows.shape[0]
    init = jnp.zeros((n_slots, PAIRS, LANES), F32)
    return pl.pallas_call(
        functools.partial(_moe_dispatch_kernel, tm=tm),
        grid_spec=pltpu.PrefetchScalarGridSpec(
            num_scalar_prefetch=2, grid=(n // tm,),
            in_specs=[pl.BlockSpec((tm, PAIRS, LANES), lambda i, d0, d1: (i, 0, 0)),
                      pl.BlockSpec(memory_space=pl.ANY)],
            out_specs=pl.BlockSpec(memory_space=pl.ANY),
            scratch_shapes=[pltpu.SemaphoreType.DMA((2,))]),
        out_shape=jax.ShapeDtypeStruct((n_slots, PAIRS, LANES), F32),
        input_output_aliases={3: 0},
        compiler_params=_cp(("arbitrary",)),
        name="moe_dispatch",
    )(dest0, dest1, h_rows, init)


def _moe_expert_kernel(be_ref, used_ref, xs_ref, win_ref, wout_ref, ys_ref, win_bf, wout_bf):
    i = pl.program_id(0)
    fresh = (i == 0) | (be_ref[i] != be_ref[jnp.maximum(i - 1, 0)])

    @pl.when(fresh)
    def _():
        win_bf[...] = win_ref[...].astype(BF16)
        wout_bf[...] = wout_ref[...].astype(BF16)

    @pl.when(i < used_ref[0])
    def _():
        x = _from_rows(xs_ref).astype(BF16)
        hmid = jnp.dot(x, win_bf[...], preferred_element_type=F32)
        gt = hmid[:, :D_EXPERT]
        up = hmid[:, D_EXPERT:]
        act = gt * _sigmoid(gt) * up
        _to_rows(ys_ref, jnp.dot(act.astype(BF16), wout_bf[...], preferred_element_type=F32))

    @pl.when(i >= used_ref[0])
    def _():
        ys_ref[...] = jnp.zeros_like(ys_ref)


def _moe_experts_sorted(xs, blk_exp, n_used, w_in, w_out):
    n_slots = xs.shape[0]
    n_blocks = n_slots // MOE_BLK
    return pl.pallas_call(
        _moe_expert_kernel,
        grid_spec=pltpu.PrefetchScalarGridSpec(
            num_scalar_prefetch=2, grid=(n_blocks,),
            in_specs=[pl.BlockSpec((MOE_BLK, PAIRS, LANES), lambda i, be, u: (i, 0, 0)),
                      pl.BlockSpec((None, D, 2 * D_EXPERT), lambda i, be, u: (be[i], 0, 0)),
                      pl.BlockSpec((None, D_EXPERT, D), lambda i, be, u: (be[i], 0, 0))],
            out_specs=pl.BlockSpec((MOE_BLK, PAIRS, LANES), lambda i, be, u: (i, 0, 0)),
            scratch_shapes=[pltpu.VMEM((D, 2 * D_EXPERT), BF16), pltpu.VMEM((D_EXPERT, D), BF16)]),
        out_shape=jax.ShapeDtypeStruct((n_slots, PAIRS, LANES), F32),
        compiler_params=_cp(("arbitrary",)),
        name="moe_experts_sorted",
    )(blk_exp, n_used, xs, w_in, w_out)


def _moe_combine_kernel(d0_ref, d1_ref, ys_ref, gate_ref, x_ref, gt_ref, vec_ref, o_ref, ya, yb, sem, *, tm):
    base = pl.program_id(0) * tm

    def issue(i, c):
        _row_copy(ys_ref.at[d0_ref[base + i]], ya.at[i], sem.at[0]).start()
        _row_copy(ys_ref.at[d1_ref[base + i]], yb.at[i], sem.at[1]).start()
        return c

    lax.fori_loop(0, tm, issue, 0)

    def drain(i, c):
        _row_copy(ys_ref.at[d0_ref[base + i]], ya.at[i], sem.at[0]).wait()
        _row_copy(ys_ref.at[d1_ref[base + i]], yb.at[i], sem.at[1]).wait()
        return c

    lax.fori_loop(0, tm, drain, 0)
    g0 = gate_ref[:, 0:1]
    g1 = gate_ref[:, 1:2]
    ff = _from_rows(ya) * g0 + _from_rows(yb) * g1
    o_ref[...] = _layer_norm(DN_ALPHA * x_ref[...] + gt_ref[...] * ff, vec_ref[0:1, :], vec_ref[1:2, :])


def _moe_combine(ys, dest0, dest1, gates, x, gt, vec, tm):
    bsz, t, _ = x.shape
    nt = t // tm
    return pl.pallas_call(
        functools.partial(_moe_combine_kernel, tm=tm),
        grid_spec=pltpu.PrefetchScalarGridSpec(
            num_scalar_prefetch=2, grid=(bsz * nt,),
            in_specs=[pl.BlockSpec(memory_space=pl.ANY),
                      pl.BlockSpec((tm, 2), lambda i, d0, d1: (i, 0)),
                      pl.BlockSpec((None, tm, D), lambda i, d0, d1: (i // nt, i % nt, 0)),
                      pl.BlockSpec((None, 1, D), lambda i, d0, d1: (i // nt, 0, 0)),
                      pl.BlockSpec((2, D), lambda i, d0, d1: (0, 0))],
            out_specs=pl.BlockSpec((None, tm, D), lambda i, d0, d1: (i // nt, i % nt, 0)),
            scratch_shapes=[pltpu.VMEM((tm, PAIRS, LANES), F32), pltpu.VMEM((tm, PAIRS, LANES), F32),
                            pltpu.SemaphoreType.DMA((2,))]),
        out_shape=jax.ShapeDtypeStruct(x.shape, F32),
        compiler_params=_cp(("arbitrary",)),
        name="moe_combine",
    )(dest0, dest1, ys, gates, x, gt, vec)


def _moe_sorted(x, sc, sh, gt, vec, w, layer, tm=256):
    bsz, t, _ = x.shape
    n = bsz * t
    n_blocks = (n * 2) // MOE_BLK + N_EXPERTS
    h_rows, idx, gate, _, cnt = _moe_route(x, sc, sh, w["router_w_t"], w["router_b"], per_b_mod=True,
                                           rows_out=True, tm=tm)
    counts = cnt[:, 0].astype(I32)
    padded = (counts + MOE_BLK - 1) // MOE_BLK * MOE_BLK
    pad_end = jnp.cumsum(padded)
    pad_start = pad_end - padded
    dest0 = pad_start[idx[0]] + idx[2]
    dest1 = pad_start[idx[1]] + idx[3]
    blk_exp = jnp.minimum(jnp.searchsorted(pad_end, jnp.arange(n_blocks, dtype=I32) * MOE_BLK, side="right"),
                          N_EXPERTS - 1).astype(I32)
    n_used = (pad_end[-1:] // MOE_BLK).astype(I32)
    xs = _moe_dispatch(h_rows, dest0, dest1, n_blocks * MOE_BLK, tm)
    ys = _moe_experts_sorted(xs, blk_exp, n_used, w["moe_w_in"][layer], w["moe_w_out"][layer])
    return _moe_combine(ys, dest0, dest1, gate[:2].T, x, gt, vec, tm)


def _moe_dense_kernel(h_ref, gm_ref, win_ref, wout_ref, x_ref, gt_ref, vec_ref, o_ref, acc_ref):
    e = pl.program_id(0)

    @pl.when(e == 0)
    def _():
        acc_ref[...] = jnp.zeros_like(acc_ref)

    lane = lax.broadcasted_iota(I32, gm_ref.shape, 1)
    gcol = jnp.sum(jnp.where(lane == e, gm_ref[...], 0.0), axis=1, keepdims=True)
    hmid = _bdot(h_ref[...], win_ref[...])
    gt = hmid[:, :D_EXPERT]
    up = hmid[:, D_EXPERT:]
    y = _bdot(gt * _sigmoid(gt) * up, wout_ref[...])
    acc_ref[...] = acc_ref[...] + jnp.where(gcol != 0.0, gcol * y, 0.0)

    @pl.when(e == N_EXPERTS - 1)
    def _():
        o_ref[...] = _layer_norm(DN_ALPHA * x_ref[...] + gt_ref[...] * acc_ref[...], vec_ref[0:1, :],
                                 vec_ref[1:2, :])


def _moe_dense(x, sc, sh, gt, vec, w, layer):
    _, m, _ = x.shape
    h, _, _, gmat, _ = _moe_route(x, sc, sh, w["router_w_t"], w["router_b"], per_b_mod=False, rows_out=False,
                                  tm=m)
    full = pl.BlockSpec((m, D), lambda e: (0, 0))
    out = pl.pallas_call(
        _moe_dense_kernel,
        grid=(N_EXPERTS,),
        in_specs=[full, pl.BlockSpec((m, N_EXPERTS), lambda e: (0, 0)),
                  pl.BlockSpec((None, None, D, 2 * D_EXPERT), lambda e: (layer, e, 0, 0)),
                  pl.BlockSpec((None, None, D_EXPERT, D), lambda e: (layer, e, 0, 0)),
                  full, full, pl.BlockSpec((2, D), lambda e: (0, 0))],
        out_specs=full,
        out_shape=jax.ShapeDtypeStruct((m, D), F32),
        scratch_shapes=[pltpu.VMEM((m, D), F32)],
        compiler_params=_cp(("arbitrary",)),
        name="moe_dense",
    )(h[0], gmat.T, w["moe_w_in"], w["moe_w_out"], x[0], gt[0], vec)
    return out[None]


def _rotary(x, cos_t, sa_t, sb_t):
    reps = x.shape[1] // LANES
    tile = lambda t: t if reps == 1 else jnp.concatenate([t] * reps, axis=1)
    n = x.shape[1]
    half = ROT_DIM // 2
    return x * tile(cos_t) + pltpu.roll(x, n - half, axis=1) * tile(sa_t) + pltpu.roll(x, half, axis=1) * tile(sb_t)


def _attn_pre_kernel(x_ref, sc_ref, sh_ref, ksc_ref, ksh_ref, rot_ref, wq_ref, wkv_ref,
                     q_ref, k_ref, v_ref, kb_ref, vb_ref):
    x = x_ref[...]
    cos_t, sa_t, sb_t = rot_ref[0], rot_ref[1], rot_ref[2]
    q = _bdot(x * (1.0 + sc_ref[...]) + sh_ref[...], wq_ref[...])
    q_ref[...] = _rotary(q, cos_t, sa_t, sb_t).astype(BF16)
    kv = _bdot(x * (1.0 + ksc_ref[...]) + ksh_ref[...], wkv_ref[...])
    k = _rotary(kv[:, :KVW], cos_t, sa_t, sb_t)
    v = kv[:, KVW:]
    k_ref[...] = k
    v_ref[...] = v
    kb_ref[...] = k.astype(BF16)
    vb_ref[...] = v.astype(BF16)


def _attn_pre(x, sc, sh, ksc, ksh, rot, w_q, w_kv, *, per_b_mod, tm):
    bsz, t, _ = x.shape
    row = lambda wd: pl.BlockSpec((None, tm, wd), lambda b, i: (b, i, 0))
    mod = pl.BlockSpec((None, 1, D), lambda b, i: (b, 0, 0)) if per_b_mod else row(D)
    nq = w_q.shape[1]
    return pl.pallas_call(
        _attn_pre_kernel,
        grid=(bsz, t // tm),
        in_specs=[row(D), mod, mod, mod, mod, pl.BlockSpec((3, tm, LANES), lambda b, i: (0, i, 0)),
                  pl.BlockSpec((D, nq), lambda b, i: (0, 0)), pl.BlockSpec((D, 2 * KVW), lambda b, i: (0, 0))],
        out_specs=[row(nq), row(KVW), row(KVW), row(KVW), row(KVW)],
        out_shape=[jax.ShapeDtypeStruct((bsz, t, nq), BF16), jax.ShapeDtypeStruct((bsz, t, KVW), F32),
                   jax.ShapeDtypeStruct((bsz, t, KVW), F32), jax.ShapeDtypeStruct((bsz, t, KVW), BF16),
                   jax.ShapeDtypeStruct((bsz, t, KVW), BF16)],
        compiler_params=_cp(("arbitrary", "arbitrary")),
        name="attn_pre",
    )(x, sc, sh, ksc, ksh, rot, w_q, w_kv)


def _attn_band_kernel(q_ref, ka_ref, kb_ref, va_ref, vb_ref, o_ref, lse_ref):
    i = pl.program_id(2)
    bq = ATT_WIN
    kcat = jnp.concatenate([ka_ref[...], kb_ref[...]], axis=0)
    vcat = jnp.concatenate([va_ref[...], vb_ref[...]], axis=0)
    qrow = lax.broadcasted_iota(I32, (bq, 2 * bq), 0)
    kcol = lax.broadcasted_iota(I32, (bq, 2 * bq), 1)
    valid = (kcol >= qrow) & (kcol <= qrow + bq) & ((i > 0) | (kcol >= bq))
    lane = lax.broadcasted_iota(I32, (bq, KVW), 1) // HD
    lse_lane = lax.broadcasted_iota(I32, (bq, LANES), 1)
    lse = jnp.zeros((bq, LANES), F32)
    scale = HD ** -0.5
    for rep in range(REP):
        q = q_ref[:, rep * KVW:(rep + 1) * KVW]
        o = jnp.zeros((bq, KVW), F32)
        for h in range(KV_HEADS):
            qm = jnp.where(lane == h, q, jnp.zeros_like(q))
            s = lax.dot_general(qm, kcat, (((1,), (1,)), ((), ())), preferred_element_type=F32) * scale
            s = jnp.where(valid, s, -jnp.inf)
            m = jnp.max(s, axis=-1, keepdims=True)
            p = jnp.exp(s - m)
            l = jnp.sum(p, axis=-1, keepdims=True)
            pv = jnp.dot(p.astype(BF16), vcat, preferred_element_type=F32)
            o = jnp.where(lane == h, pv / l, o)
            lse = jnp.where(lse_lane == rep * KV_HEADS + h, m + jnp.log(l), lse)
        o_ref[:, rep * KVW:(rep + 1) * KVW] = o
    lse_ref[...] = lse


def _attn_band(q, kb, vb, group, dil):
    bsz, t, nq = q.shape
    tc = t // dil
    nb = tc // ATT_WIN
    qv = q.reshape(bsz, tc, dil * nq)
    kv = kb.reshape(bsz, tc, dil * KVW)
    vv = vb.reshape(bsz, tc, dil * KVW)
    ncol = nq // D
    cur = pl.BlockSpec((None, ATT_WIN, KVW), lambda b, c, i: (b, i, c))
    prev = pl.BlockSpec((None, ATT_WIN, KVW), lambda b, c, i: (b, jnp.maximum(i - 1, 0), c))
    o, lse = pl.pallas_call(
        _attn_band_kernel,
        grid=(bsz, dil, nb),
        in_specs=[pl.BlockSpec((None, ATT_WIN, D), lambda b, c, i: (b, i, c * ncol + group)),
                  prev, cur, prev, cur],
        out_specs=[pl.BlockSpec((None, ATT_WIN, D), lambda b, c, i: (b, i, c)),
                   pl.BlockSpec((None, ATT_WIN, LANES), lambda b, c, i: (b, i, c))],
        out_shape=[jax.ShapeDtypeStruct((bsz, tc, dil * D), F32), jax.ShapeDtypeStruct((bsz, tc, dil * LANES), F32)],
        compiler_params=_cp(("arbitrary", "arbitrary", "arbitrary")),
        name=f"attn_band_d{dil}",
    )(qv, kv, kv, vv, vv)
    return o.reshape(bsz, t, D), lse.reshape(bsz, t, LANES)


def _attn_step_kernel(q_ref, kn_ref, vn_ref, k0_ref, k1_ref, k2_ref, v0_ref, v1_ref, v2_ref, o_ref, lse_ref):
    rows = REP * KV_HEADS
    rrow = lax.broadcasted_iota(I32, (rows, KVW), 0)
    rlane = lax.broadcasted_iota(I32, (rows, KVW), 1) // HD
    own = rlane == rrow % KV_HEADS
    scale = HD ** -0.5
    kn = kn_ref[...]
    vn = vn_ref[...]
    lse_row = lax.broadcasted_iota(I32, (rows, LANES), 0)
    lse_lane = lax.broadcasted_iota(I32, (rows, LANES), 1)
    for g, (kc_ref, vc_ref) in enumerate(((k0_ref, v0_ref), (k1_ref, v1_ref), (k2_ref, v2_ref))):
        q = q_ref[:, g * D:(g + 1) * D]
        qm = jnp.concatenate([jnp.broadcast_to(q[:, r * KVW:(r + 1) * KVW], (KV_HEADS, KVW))
                              for r in range(REP)], axis=0)
        qm = jnp.where(own, qm, 0.0)
        s_c = _bdot_nt(qm, kc_ref[...]) * scale
        s_n = jnp.sum(qm.astype(BF16).astype(F32) * kn.astype(BF16).astype(F32), axis=-1, keepdims=True) * scale
        m = jnp.maximum(jnp.max(s_c, axis=-1, keepdims=True), s_n)
        p_c = jnp.exp(s_c - m)
        p_n = jnp.exp(s_n - m)
        l = jnp.sum(p_c, axis=-1, keepdims=True) + p_n
        pv = _bdot(p_c, vc_ref[...]) + p_n.astype(BF16).astype(F32) * vn.astype(BF16).astype(F32)
        o = jnp.where(own, pv / l, 0.0)
        for r in range(REP):
            o_ref[g:g + 1, r * KVW:(r + 1) * KVW] = jnp.sum(o[r * KV_HEADS:(r + 1) * KV_HEADS], axis=0,
                                                            keepdims=True)
        lse_ref[g:g + 1, :] = jnp.sum(jnp.where(lse_row == lse_lane, m + jnp.log(l), 0.0), axis=0, keepdims=True)


def _attn_step(q, k_new, v_new, cache_k, cache_v):
    bsz = q.shape[0]
    wbuf = cache_k.shape[1]
    views, specs = [], []
    for cache in (cache_k, cache_v):
        for _, dil in DILATED_GROUPS:
            tc = wbuf // dil
            views.append(cache.reshape(bsz, tc, dil * KVW))
            specs.append(pl.BlockSpec((None, ATT_WIN, KVW), lambda b, tc=tc: (b, tc // ATT_WIN - 1, 0)))
    vec = lambda wd: pl.BlockSpec((None, 1, wd), lambda b: (b, 0, 0))
    o, lse = pl.pallas_call(
        _attn_step_kernel,
        grid=(bsz,),
        in_specs=[vec(3 * D), vec(KVW), vec(KVW)] + specs,
        out_specs=[pl.BlockSpec((None, 3, D), lambda b: (b, 0, 0)), pl.BlockSpec((None, 3, LANES), lambda b: (b, 0, 0))],
        out_shape=[jax.ShapeDtypeStruct((bsz, 3, D), F32), jax.ShapeDtypeStruct((bsz, 3, LANES), F32)],
        compiler_params=_cp(("arbitrary",)),
        name="attn_step",
    )(q.reshape(bsz, 1, 3 * D), k_new.reshape(bsz, 1, KVW), v_new.reshape(bsz, 1, KVW), *views)
    return o.transpose(1, 0, 2), lse.transpose(1, 0, 2)


def _attn_post_kernel(o0_ref, o1_ref, o2_ref, l0_ref, l1_ref, l2_ref, x_ref, gt_ref, vec_ref, wo_ref, out_ref):
    l0, l1, l2 = l0_ref[...], l1_ref[...], l2_ref[...]
    m = jnp.maximum(jnp.maximum(l0, l1), l2)
    e0, e1, e2 = jnp.exp(l0 - m), jnp.exp(l1 - m), jnp.exp(l2 - m)
    den = e0 + e1 + e2
    r = lax.broadcasted_iota(I32, (LANES, D), 0)
    c = lax.broadcasted_iota(I32, (LANES, D), 1) // HD
    spread = (r == c).astype(BF16)
    mix = (_split_dot(e0 / den, spread) * o0_ref[...] + _split_dot(e1 / den, spread) * o1_ref[...]
           + _split_dot(e2 / den, spread) * o2_ref[...])
    out = _bdot(mix, wo_ref[...])
    out_ref[...] = _layer_norm(DN_ALPHA * x_ref[...] + gt_ref[...] * out, vec_ref[0:1, :], vec_ref[1:2, :])


def _attn_post(os, lses, x, gt, vec, w_o, *, per_b_mod, tm):
    bsz, t, _ = x.shape
    row = lambda wd: pl.BlockSpec((None, tm, wd), lambda b, i: (b, i, 0))
    mod = pl.BlockSpec((None, 1, D), lambda b, i: (b, 0, 0)) if per_b_mod else row(D)
    return pl.pallas_call(
        _attn_post_kernel,
        grid=(bsz, t // tm),
        in_specs=[row(D)] * 3 + [row(LANES)] * 3 + [row(D), mod, pl.BlockSpec((2, D), lambda b, i: (0, 0)),
                                                   pl.BlockSpec((D, D), lambda b, i: (0, 0))],
        out_specs=row(D),
        out_shape=jax.ShapeDtypeStruct(x.shape, F32),
        compiler_params=_cp(("arbitrary", "arbitrary")),
        name="attn_post",
    )(*os, *lses, x, gt, vec, w_o)


def _rot_tables(pos):
    half = ROT_DIM // 2
    inv = ROPE_THETA ** (-jnp.arange(half, dtype=F32) * 2.0 / ROT_DIM)
    ang = pos.astype(F32)[:, None] * inv[None, :]
    cos, sin = jnp.cos(ang), jnp.sin(ang)
    t = pos.shape[0]
    one = jnp.ones((t, HD - ROT_DIM), F32)
    zero = jnp.zeros((t, HD - ROT_DIM), F32)
    zh = jnp.zeros((t, half), F32)
    cos_t = jnp.concatenate([cos, cos, one], axis=1)
    sa_t = jnp.concatenate([-sin, zh, zero], axis=1)
    sb_t = jnp.concatenate([zh, sin, zero], axis=1)
    return jnp.stack([jnp.tile(x, (1, 2)) for x in (cos_t, sa_t, sb_t)])


def _prep_weights(p):
    pad_c = lambda m, n: jnp.pad(m, ((0, 0), (0, n - m.shape[1])))
    pad_r = lambda m, n: jnp.pad(m, ((0, n - m.shape[0]), (0, 0)))
    w = {}
    w["mu"] = p["rwkv_mu"][0]
    w["w_rkv"] = p["rwkv_w_rkv"][0].astype(BF16)
    w["w1"] = pad_c(p["rwkv_w1"][0], LANES).astype(BF16)
    w["w2"] = pad_r(p["rwkv_w2"][0], LANES).astype(BF16)
    w["a1"] = pad_c(p["rwkv_a1"][0], LANES).astype(BF16)
    w["a2"] = pad_r(p["rwkv_a2"][0], LANES).astype(BF16)
    w["g1"] = pad_c(p["rwkv_g1"][0], 2 * LANES).astype(BF16)
    w["g2"] = pad_r(p["rwkv_g2"][0], 2 * LANES).astype(BF16)
    w["vec"] = jnp.stack([p["rwkv_w0"][0], p["rwkv_a0"][0], p["rwkv_k_k"][0], p["rwkv_k_a"][0]])
    w["post_vec"] = jnp.stack([p["rwkv_r_k"][0].reshape(D), p["rwkv_lnx_g"][0], p["rwkv_lnx_b"][0],
                               p["ln_g"][0, 0], p["ln_b"][0, 0]])
    w["rwkv_w_o"] = p["rwkv_w_o"][0].astype(BF16)
    wq = p["w_q"][0].reshape(D, 3, KV_HEADS, REP, HD).transpose(0, 1, 3, 2, 4).reshape(D, 3 * D)
    w["w_q"] = wq.astype(BF16)
    w["w_kv"] = p["w_kv"].astype(BF16)
    wo = p["w_o_attn"][0].reshape(KV_HEADS, REP, HD, D).transpose(1, 0, 2, 3).reshape(D, D)
    w["w_o_attn"] = wo.astype(BF16)
    w["router_w_t"] = p["router_w"].T.astype(BF16)
    w["router_b"] = p["router_b"].reshape(N_EXPERTS, 1)
    w["moe_w_in"] = p["moe_w_in"]
    w["moe_w_out"] = p["moe_w_out"]
    w["ln"] = [[jnp.stack([p["ln_g"][l, i], p["ln_b"][l, i]]) for i in range(2)] for l in range(DEPTH)]
    return w


def _modulations(c_prompt, c_sample, p):
    nb = c_prompt.shape[0]
    c = jnp.concatenate([c_prompt, c_sample], axis=0)
    pad = (-c.shape[0]) % 8
    c = jnp.pad(c, ((0, pad), (0, 0)))
    m3 = _ada_linear(c, p["ada_w"].reshape(2 * DEPTH, D, 3 * D), p["ada_b"].reshape(2 * DEPTH, 3 * D))
    m2 = _ada_linear(c, p["kv_ada_w"][None], p["kv_ada_b"][None])[0]
    n_all = nb + c_sample.shape[0]

    def split(m, parts, lo, hi):
        return [m[lo:hi, i * D:(i + 1) * D] for i in range(parts)]

    out = {}
    for name, lo, hi in (("prompt", 0, nb), ("sample", nb, n_all)):
        out[name] = {"ada": [[split(m3[2 * l + i], 3, lo, hi) for i in range(2)] for l in range(DEPTH)],
                     "kv": split(m2, 2, lo, hi)}
    return out


def _trunk_prompt(x, mods, w):
    bsz, t, _ = x.shape
    per_b = lambda m: m[:, None, :]
    sh, sc, gt = (per_b(m) for m in mods["ada"][0][0])
    zeros = jnp.zeros((bsz, 1, D), F32)
    r, lw, k, v, a, b, g, last = _rwkv_pre(x, sc, sh, zeros, w, seq_mode=True, tm=256)
    y, zf = _wkv_chunked(r, lw, k, v, a, b)
    x = _rwkv_post(y, r, k, v, g, x, gt, w["post_vec"], w["rwkv_w_o"], pairs=True, tm=256)
    zf = zf.reshape(bsz, PAIRS, 2, HD, 2, HD)
    wkv = jnp.stack([zf[:, :, 0, :, 0, :], zf[:, :, 1, :, 1, :]], axis=2)
    wkv = wkv.reshape(bsz, HEADS, HD, HD).transpose(0, 1, 3, 2)
    sh2, sc2, gt2 = (per_b(m) for m in mods["ada"][0][1])
    x = _moe_sorted(x, sc2, sh2, gt2, w["ln"][0][1], w, 0)

    sh, sc, gt = (per_b(m) for m in mods["ada"][1][0])
    ksh, ksc = (per_b(m) for m in mods["kv"])
    rot = _rot_tables(jnp.arange(t, dtype=I32))
    q, k_new, v_new, kb, vb = _attn_pre(x, sc, sh, ksc, ksh, rot, w["w_q"], w["w_kv"], per_b_mod=True, tm=256)
    os, lses = zip(*[_attn_band(q, kb, vb, gi, dil) for gi, (_, dil) in enumerate(DILATED_GROUPS)])
    x = _attn_post(os, lses, x, gt, w["ln"][1][0], w["w_o_attn"], per_b_mod=True, tm=256)
    sh2, sc2, gt2 = (per_b(m) for m in mods["ada"][1][1])
    x = _moe_sorted(x, sc2, sh2, gt2, w["ln"][1][1], w, 1)
    return x, wkv[None], last.reshape(1, bsz, D), k_new, v_new


def _trunk_sample(x, mods, state_wkv, state_shift, cache_k, cache_v, w):
    bsz = x.shape[0]
    xs = x.reshape(1, bsz, D)
    row = lambda m: m[None]
    sh, sc, gt = (row(m) for m in mods["ada"][0][0])
    r, lw, k, v, a, b, g, hm = _rwkv_pre(xs, sc, sh, state_shift[0][None], w, seq_mode=False, tm=bsz)
    s_new, y = _wkv_step(state_wkv[0], v[0], r[0], lw[0], k[0], a[0], b[0])
    xs = _rwkv_post(y[None], r, k, v, g, xs, gt, w["post_vec"], w["rwkv_w_o"], pairs=False, tm=bsz)
    sh2, sc2, gt2 = (row(m) for m in mods["ada"][0][1])
    xs = _moe_dense(xs, sc2, sh2, gt2, w["ln"][0][1], w, 0)

    sh, sc, gt = (row(m) for m in mods["ada"][1][0])
    ksh, ksc = (row(m) for m in mods["kv"])
    rot = _rot_tables(jnp.full((bsz,), PAST_LEN, I32))
    q, k_new, v_new, _, _ = _attn_pre(xs, sc, sh, ksc, ksh, rot, w["w_q"], w["w_kv"], per_b_mod=False, tm=bsz)
    o, lse = _attn_step(q[0].astype(F32), k_new[0], v_new[0], cache_k, cache_v)
    xs = _attn_post([o[i][None] for i in range(3)], [lse[i][None] for i in range(3)], xs, gt, w["ln"][1][0],
                    w["w_o_attn"], per_b_mod=False, tm=bsz)
    sh2, sc2, gt2 = (row(m) for m in mods["ada"][1][1])
    xs = _moe_dense(xs, sc2, sh2, gt2, w["ln"][1][1], w, 1)
    return (xs.reshape(bsz, 1, D), s_new[None], hm, k_new.reshape(bsz, 1, KV_HEADS, HD),
            v_new.reshape(bsz, 1, KV_HEADS, HD))


def kernel(x_prompt, x_sample, state_wkv, state_shift, cache_k, cache_v, c_prompt, c_sample, ada_w, ada_b, ln_g, ln_b, rwkv_mu, rwkv_w_rkv, rwkv_w0, rwkv_w1, rwkv_w2, rwkv_a0, rwkv_a1, rwkv_a2, rwkv_g1, rwkv_g2, rwkv_k_k, rwkv_k_a, rwkv_r_k, rwkv_lnx_g, rwkv_lnx_b, rwkv_w_o, w_q, w_kv, kv_ada_w, kv_ada_b, w_o_attn, router_w, router_b, moe_w_in, moe_w_out):
    p = {"ada_w": ada_w, "ada_b": ada_b, "ln_g": ln_g, "ln_b": ln_b, "rwkv_mu": rwkv_mu, "rwkv_w_rkv": rwkv_w_rkv,
         "rwkv_w0": rwkv_w0, "rwkv_w1": rwkv_w1, "rwkv_w2": rwkv_w2, "rwkv_a0": rwkv_a0, "rwkv_a1": rwkv_a1,
         "rwkv_a2": rwkv_a2, "rwkv_g1": rwkv_g1, "rwkv_g2": rwkv_g2, "rwkv_k_k": rwkv_k_k, "rwkv_k_a": rwkv_k_a,
         "rwkv_r_k": rwkv_r_k, "rwkv_lnx_g": rwkv_lnx_g, "rwkv_lnx_b": rwkv_lnx_b, "rwkv_w_o": rwkv_w_o,
         "w_q": w_q, "w_kv": w_kv, "kv_ada_w": kv_ada_w, "kv_ada_b": kv_ada_b, "w_o_attn": w_o_attn,
         "router_w": router_w, "router_b": router_b, "moe_w_in": moe_w_in, "moe_w_out": moe_w_out}
    w = _prep_weights(p)
    mods = _modulations(c_prompt, c_sample, p)
    bp, tp, _ = x_prompt.shape
    y_p, wkv_p, shift_p, k_p, v_p = _trunk_prompt(x_prompt, mods["prompt"], w)
    keep = min(PAST_LEN, tp)
    k_p = k_p[:, tp - keep:].reshape(bp, keep, KV_HEADS, HD)
    v_p = v_p[:, tp - keep:].reshape(bp, keep, KV_HEADS, HD)
    y_s, wkv_s, shift_s, k_s, v_s = _trunk_sample(x_sample, mods["sample"], state_wkv, state_shift, cache_k,
                                                  cache_v, w)
    return (y_p, y_s, wkv_p, shift_p, k_p, v_p, wkv_s, shift_s, k_s, v_s)
```

```python
import functools

import jax
import jax.numpy as jnp
from jax import lax
from jax.experimental import pallas as pl
from jax.experimental.pallas import tpu as pltpu

F32 = jnp.float32
BF16 = jnp.bfloat16
I32 = jnp.int32

D = 1024
HEADS = 16
HD = 64
LANES = 128
PAIRS = D // LANES
CHUNK = 64
WKV_GROUP = 8
N_EXPERTS = 32
EXPERTS_PER_GROUP = 8
N_EXPERT_GROUPS = 4
D_EXPERT = 512
MOE_BLK = 256
KV_HEADS = 4
REP = 4
KVW = KV_HEADS * HD
DILATED_GROUPS = ((128, 1), (512, 4), (2048, 16))
ATT_WIN = 128
PAST_LEN = 2048
ROT_DIM = 16
ROPE_THETA = 500000.0
DEPTH = 2
DN_ALPHA = (2 * DEPTH) ** 0.25
LN_EPS = 1e-5
GN_EPS = 64e-5
VMEM_LIMIT = 56 * 1024 * 1024


def _cp(sem):
    return pltpu.CompilerParams(dimension_semantics=sem, vmem_limit_bytes=VMEM_LIMIT)


def _bdot(a, b):
    return jnp.dot(a.astype(BF16), b.astype(BF16), preferred_element_type=F32)


def _bdot_nt(a, b):
    return lax.dot_general(a.astype(BF16), b.astype(BF16), (((1,), (1,)), ((), ())),
                           preferred_element_type=F32)


def _bdot_tn(a, b):
    return lax.dot_general(a.astype(BF16), b.astype(BF16), (((0,), (0,)), ((), ())),
                           preferred_element_type=F32)


def _split_dot(x, m):
    hi = x.astype(BF16)
    lo = (x - hi.astype(F32)).astype(BF16)
    return (jnp.dot(hi, m, preferred_element_type=F32) + jnp.dot(lo, m, preferred_element_type=F32))


def _head_ones():
    r = lax.broadcasted_iota(I32, (LANES, LANES), 0) // HD
    c = lax.broadcasted_iota(I32, (LANES, LANES), 1) // HD
    return (r == c).astype(BF16)


def _segsum(x, ones_bd):
    cols = [_split_dot(x[:, c * LANES:(c + 1) * LANES], ones_bd) for c in range(x.shape[1] // LANES)]
    return cols[0] if len(cols) == 1 else jnp.concatenate(cols, axis=1)


def _layer_norm(x, g, b):
    mu = jnp.mean(x, axis=-1, keepdims=True)
    xc = x - mu
    var = jnp.mean(xc * xc, axis=-1, keepdims=True)
    return xc * lax.rsqrt(var + LN_EPS) * g + b


def _sigmoid(x):
    return 1.0 / (1.0 + jnp.exp(-x))


def _to_pairs(ref, val):
    for p in range(PAIRS):
        ref[p] = val[:, p * LANES:(p + 1) * LANES]


def _from_pairs(ref):
    return jnp.concatenate([ref[p] for p in range(PAIRS)], axis=1)


def _to_rows(ref, val):
    for s in range(PAIRS):
        ref[:, s, :] = val[:, s * LANES:(s + 1) * LANES]


def _from_rows(ref):
    return jnp.concatenate([ref[:, s, :] for s in range(PAIRS)], axis=1)


def _ada_kernel(c_ref, w_ref, b_ref, o_ref):
    c = c_ref[...]
    o_ref[...] = _bdot(c * _sigmoid(c), w_ref[...]) + b_ref[...]


def _ada_linear(c, w, b, tn=512):
    s, _, n = w.shape
    m = c.shape[0]
    return pl.pallas_call(
        _ada_kernel,
        grid=(s, n // tn),
        in_specs=[pl.BlockSpec((m, D), lambda i, j: (0, 0)),
                  pl.BlockSpec((None, D, tn), lambda i, j: (i, 0, j)),
                  pl.BlockSpec((None, 1, tn), lambda i, j: (i, 0, j))],
        out_specs=pl.BlockSpec((None, m, tn), lambda i, j: (i, 0, j)),
        out_shape=jax.ShapeDtypeStruct((s, m, n), F32),
        compiler_params=_cp(("arbitrary", "arbitrary")),
        name="ada_linear",
    )(c, w, b.reshape(s, 1, n))


def _rwkv_pre_kernel(x_ref, sc_ref, sh_ref, prev_ref, mu_ref, wrkv_ref, w1_ref, w2_ref, a1_ref, a2_ref,
                     g1_ref, g2_ref, vec_ref,
                     r_ref, lw_ref, k_ref, v_ref, a_ref, b_ref, g_ref, hm_ref, carry_ref, *, seq_mode, tm):
    x = x_ref[...]
    hm = x * (1.0 + sc_ref[...]) + sh_ref[...]
    if seq_mode:
        @pl.when(pl.program_id(1) == 0)
        def _():
            carry_ref[...] = prev_ref[...]
        row = lax.broadcasted_iota(I32, hm.shape, 0)
        hprev = jnp.where(row == 0, carry_ref[...], pltpu.roll(hm, 1, axis=0))
        carry_ref[...] = hm[tm - 1:tm, :]
        hm_ref[...] = hm[tm - 1:tm, :]
    else:
        hprev = prev_ref[...]
        hm_ref[...] = hm
    xx = hprev - hm

    def mix(i):
        return hm + xx * mu_ref[i:i + 1, :]

    w0, a0, k_k, k_a = (vec_ref[i:i + 1, :] for i in range(4))
    r = _bdot(mix(0), wrkv_ref[0])
    k = _bdot(mix(2), wrkv_ref[1])
    v = _bdot(mix(3), wrkv_ref[2])
    wl = w0 + _bdot(jnp.tanh(_bdot(mix(1), w1_ref[...])), w2_ref[...])
    z = -wl
    softplus = jnp.maximum(z, 0.0) + jnp.log1p(jnp.exp(-jnp.abs(z)))
    lw = -jnp.exp(-softplus - 0.5)
    a_lr = _sigmoid(a0 + _bdot(_bdot(mix(4), a1_ref[...]), a2_ref[...]))
    g = _bdot(_sigmoid(_bdot(mix(5), g1_ref[...])), g2_ref[...])
    kk = k * k_k
    kk = kk / jnp.maximum(jnp.sqrt(_segsum(kk * kk, _head_ones())), 1e-12)
    kmod = k * (1.0 + (a_lr - 1.0) * k_a)
    outs = ((r_ref, r), (lw_ref, lw), (k_ref, kmod), (v_ref, v), (a_ref, -kk), (b_ref, kk * a_lr), (g_ref, g))
    for ref, val in outs:
        if seq_mode:
            _to_pairs(ref, val)
        else:
            ref[...] = val


def _rwkv_pre(x, sc, sh, prev, w, *, seq_mode, tm):
    bsz, t, _ = x.shape
    grid = (bsz, t // tm)
    row = pl.BlockSpec((None, tm, D), lambda b, i: (b, i, 0))
    per_b = pl.BlockSpec((None, 1, D), lambda b, i: (b, 0, 0))
    mod = per_b if seq_mode else row

    def const(shape):
        return pl.BlockSpec(shape, lambda b, i: (0,) * len(shape))

    if seq_mode:
        out_big = pl.BlockSpec((None, PAIRS, tm, LANES), lambda b, i: (b, 0, i, 0))
        big_shape = jax.ShapeDtypeStruct((bsz, PAIRS, t, LANES), F32)
        hm_spec, hm_shape = per_b, jax.ShapeDtypeStruct((bsz, 1, D), F32)
    else:
        out_big, big_shape = row, jax.ShapeDtypeStruct((bsz, t, D), F32)
        hm_spec, hm_shape = row, jax.ShapeDtypeStruct((bsz, t, D), F32)
    return pl.pallas_call(
        functools.partial(_rwkv_pre_kernel, seq_mode=seq_mode, tm=tm),
        grid=grid,
        in_specs=[row, mod, mod, mod, const((6, D)), const((3, D, D)), const((D, LANES)), const((LANES, D)),
                  const((D, LANES)), const((LANES, D)), const((D, 2 * LANES)), const((2 * LANES, D)),
                  const((4, D))],
        out_specs=[out_big] * 7 + [hm_spec],
        out_shape=[big_shape] * 7 + [hm_shape],
        scratch_shapes=[pltpu.VMEM((1, D), F32)],
        compiler_params=_cp(("arbitrary", "arbitrary")),
        name="rwkv_pre",
    )(x, sc, sh, prev, w["mu"], w["w_rkv"], w["w1"], w["w2"], w["a1"], w["a2"], w["g1"], w["g2"], w["vec"])


def _wkv_chunk_kernel(r_ref, lw_ref, k_ref, v_ref, a_ref, b_ref, y_ref, zf_ref, z_ref, *, n_pairs, group):
    c = pl.program_id(0)

    @pl.when(c == 0)
    def _():
        z_ref[...] = jnp.zeros_like(z_ref)

    L = CHUNK
    row_l = lax.broadcasted_iota(I32, (L, L), 0)
    col_l = lax.broadcasted_iota(I32, (L, L), 1)
    tri_incl = (row_l >= col_l).astype(BF16)
    lane = lax.broadcasted_iota(I32, (L, LANES), 1)
    head0 = lane < HD
    row = lax.broadcasted_iota(I32, (LANES, LANES), 0)
    col = lax.broadcasted_iota(I32, (LANES, LANES), 1)
    strict = row > col
    incl = row >= col
    eye = (row == col).astype(F32)

    def expand(x):
        return jnp.concatenate([jnp.where(head0, x, 0.0), jnp.where(head0, 0.0, x)], axis=0)

    def group_body(gi, carry):
        ids = [gi * group + j for j in range(group)]
        bp = [(i // PAIRS, i % PAIRS) for i in ids]
        G = range(group)
        r = [r_ref[b_i, p_i] for b_i, p_i in bp]
        lw = [lw_ref[b_i, p_i] for b_i, p_i in bp]
        k = [k_ref[b_i, p_i] for b_i, p_i in bp]
        v = [v_ref[b_i, p_i] for b_i, p_i in bp]
        a = [a_ref[b_i, p_i] for b_i, p_i in bp]
        b = [b_ref[b_i, p_i] for b_i, p_i in bp]
        z0 = [z_ref[i] for i in ids]
        cum = [_split_dot_left(tri_incl, lw[j]) for j in G]
        cum_l = [cum[j][L - 1:L, :] for j in G]
        inv = [jnp.exp(-cum[j]) for j in G]
        tail = [jnp.exp(cum_l[j] - cum[j]) for j in G]
        a_e = [expand(a[j] * jnp.exp(cum[j] - lw[j])) for j in G]
        r_e = [expand(r[j] * jnp.exp(cum[j])) for j in G]
        b_e = [expand(b[j] * inv[j]) for j in G]
        k_e = [expand(k[j] * inv[j]) for j in G]
        bd_e = [expand(b[j] * tail[j]) for j in G]
        kd_e = [expand(k[j] * tail[j]) for j in G]
        v_e = [expand(v[j]) for j in G]
        gm = [_bdot_nt(jnp.concatenate([a_e[j], r_e[j]], axis=0), jnp.concatenate([b_e[j], k_e[j]], axis=0))
              for j in G]
        m_ab = [jnp.where(strict, gm[j][:LANES, :LANES], 0.0) for j in G]
        m_ak = [jnp.where(strict, gm[j][:LANES, LANES:], 0.0) for j in G]
        m_rb = [jnp.where(incl, gm[j][LANES:, :LANES], 0.0) for j in G]
        m_rk = [jnp.where(incl, gm[j][LANES:, LANES:], 0.0) for j in G]
        tinv = [eye + m_ab[j] for j in G]
        pw = m_ab
        for _ in range(5):
            pw = [_bdot(pw[j], pw[j]) for j in G]
            tinv = [tinv[j] + _bdot(pw[j], tinv[j]) for j in G]
        mv = [_bdot(jnp.concatenate([m_ak[j], m_rk[j]], axis=0), v_e[j]) for j in G]
        taw = [_bdot(tinv[j], jnp.concatenate([a_e[j], mv[j][:LANES]], axis=1)) for j in G]
        qy = [_bdot(m_rb[j], taw[j]) for j in G]
        pp = [_bdot_tn(bd_e[j], taw[j]) for j in G]
        kv = [_bdot_tn(kd_e[j], v_e[j]) for j in G]
        phi = [eye * jnp.exp(cum_l[j]) + pp[j][:, :LANES] for j in G]
        q = [r_e[j] + qy[j][:, :LANES] for j in G]
        zq = [_bdot(jnp.concatenate([phi[j], q[j]], axis=0), z0[j]) for j in G]
        for j in G:
            z_ref[ids[j]] = zq[j][:LANES] + pp[j][:, LANES:] + kv[j]
            y_e = zq[j][LANES:] + qy[j][:, LANES:] + mv[j][LANES:]
            y_ref[bp[j][0], bp[j][1]] = y_e[:L] + y_e[L:]
        return carry

    lax.fori_loop(0, n_pairs // group, group_body, 0)

    @pl.when(c == pl.num_programs(0) - 1)
    def _():
        zf_ref[...] = z_ref[...]


def _split_dot_left(m, x):
    hi = x.astype(BF16)
    lo = (x - hi.astype(F32)).astype(BF16)
    return (jnp.dot(m, hi, preferred_element_type=F32) + jnp.dot(m, lo, preferred_element_type=F32))


def _wkv_chunked(r, lw, k, v, a, b):
    bsz, _, t, _ = r.shape
    n_pairs = bsz * PAIRS
    blk = pl.BlockSpec((bsz, PAIRS, CHUNK, LANES), lambda c: (0, 0, c, 0))
    return pl.pallas_call(
        functools.partial(_wkv_chunk_kernel, n_pairs=n_pairs, group=WKV_GROUP),
        grid=(t // CHUNK,),
        in_specs=[blk] * 6,
        out_specs=[blk, pl.BlockSpec((n_pairs, LANES, LANES), lambda c: (0, 0, 0))],
        out_shape=[jax.ShapeDtypeStruct(r.shape, F32), jax.ShapeDtypeStruct((n_pairs, LANES, LANES), F32)],
        scratch_shapes=[pltpu.VMEM((n_pairs, LANES, LANES), F32)],
        compiler_params=_cp(("arbitrary",)),
        name="wkv_chunked",
    )(r, lw, k, v, a, b)


def _wkv_step_kernel(s_ref, vb_ref, r_ref, lw_ref, k_ref, a_ref, b_ref, so_ref, y_ref, *, tb):
    lane = lax.broadcasted_iota(I32, (HD, LANES), 1)

    def body(n, ytile):
        hs = range(HEADS)
        s = [s_ref[n, h] for h in hs]
        sa = [jnp.sum(s[h] * a_ref[n, h], axis=-1, keepdims=True) for h in hs]
        s_new = [s[h] * jnp.exp(lw_ref[n, h]) + sa[h] * b_ref[n, h] + vb_ref[n, h] * k_ref[n, h] for h in hs]
        y = [jnp.sum(s_new[h] * r_ref[n, h], axis=-1, keepdims=True) for h in hs]
        for h in hs:
            so_ref[n, h] = s_new[h]
            ytile = jnp.where(lane == n * HEADS + h, y[h], ytile)
        return ytile

    y_ref[...] = lax.fori_loop(0, tb, body, jnp.zeros((HD, LANES), F32))


def _wkv_step(state, v, r, lw, k, a, b, tb=8):
    bsz = state.shape[0]
    vb = jnp.broadcast_to(v.reshape(bsz, HEADS, HD, 1), (bsz, HEADS, HD, HD))
    rows = [t.reshape(bsz, HEADS, 1, HD) for t in (r, lw, k, a, b)]
    big = pl.BlockSpec((tb, HEADS, HD, HD), lambda i: (i, 0, 0, 0))
    vec = pl.BlockSpec((tb, HEADS, 1, HD), lambda i: (i, 0, 0, 0))
    s_new, yt = pl.pallas_call(
        functools.partial(_wkv_step_kernel, tb=tb),
        grid=(bsz // tb,),
        in_specs=[big, big] + [vec] * 5,
        out_specs=[big, pl.BlockSpec((None, HD, LANES), lambda i: (i, 0, 0))],
        out_shape=[jax.ShapeDtypeStruct(state.shape, F32), jax.ShapeDtypeStruct((bsz // tb, HD, LANES), F32)],
        compiler_params=_cp(("arbitrary",)),
        name="wkv_step",
    )(state, vb, *rows)
    y = yt.reshape(bsz // tb, HD, tb, HEADS).transpose(0, 2, 3, 1).reshape(bsz, D)
    return s_new, y


def _rwkv_post_kernel(y_ref, r_ref, k_ref, v_ref, g_ref, x_ref, gt_ref, vec_ref, wo_ref, o_ref, *, pairs):
    load = _from_pairs if pairs else (lambda ref: ref[...])
    y, r, k, v, g = (load(ref) for ref in (y_ref, r_ref, k_ref, v_ref, g_ref))
    r_k, lnx_g, lnx_b, ln_g, ln_b = (vec_ref[i:i + 1, :] for i in range(5))
    ones_bd = _head_ones()
    ym = _segsum(y, ones_bd) * (1.0 / HD)
    yc = y - ym
    yv = _segsum(yc * yc, ones_bd) * (1.0 / HD)
    yn = yc * lax.rsqrt(yv + GN_EPS) * lnx_g + lnx_b
    bonus = _segsum(r * k * r_k, ones_bd) * v
    mix = _bdot((yn + bonus) * g, wo_ref[...])
    o_ref[...] = _layer_norm(DN_ALPHA * x_ref[...] + gt_ref[...] * mix, ln_g, ln_b)


def _rwkv_post(y, r, k, v, g, x, gt, vec, w_o, *, pairs, tm):
    bsz, t, _ = x.shape
    row = pl.BlockSpec((None, tm, D), lambda b, i: (b, i, 0))
    big = pl.BlockSpec((None, PAIRS, tm, LANES), lambda b, i: (b, 0, i, 0)) if pairs else row
    mod = pl.BlockSpec((None, 1, D), lambda b, i: (b, 0, 0)) if pairs else row
    return pl.pallas_call(
        functools.partial(_rwkv_post_kernel, pairs=pairs),
        grid=(bsz, t // tm),
        in_specs=[big] * 5 + [row, mod, pl.BlockSpec((5, D), lambda b, i: (0, 0)),
                              pl.BlockSpec((D, D), lambda b, i: (0, 0))],
        out_specs=row,
        out_shape=jax.ShapeDtypeStruct(x.shape, F32),
        compiler_params=_cp(("arbitrary", "arbitrary")),
        name="rwkv_post",
    )(y, r, k, v, g, x, gt, vec, w_o)


def _top2(v):
    io = lax.broadcasted_iota(I32, v.shape, 0)
    m1 = jnp.max(v, axis=0, keepdims=True)
    i1 = jnp.min(jnp.where(v == m1, io, EXPERTS_PER_GROUP), axis=0, keepdims=True)
    v2 = jnp.where(io == i1, -jnp.inf, v)
    m2 = jnp.max(v2, axis=0, keepdims=True)
    i2 = jnp.min(jnp.where(v2 == m2, io, EXPERTS_PER_GROUP), axis=0, keepdims=True)
    return m1 + m2, i1, i2


def _moe_route_kernel(x_ref, sc_ref, sh_ref, rw_ref, rb_ref, h_ref, idx_ref, gate_ref, gmat_ref, cnt_ref,
                      carry_ref, *, tm, rows_out):
    @pl.when((pl.program_id(0) == 0) & (pl.program_id(1) == 0))
    def _():
        carry_ref[...] = jnp.zeros_like(carry_ref)

    h = x_ref[...] * (1.0 + sc_ref[...]) + sh_ref[...]
    if rows_out:
        _to_rows(h_ref, h)
    else:
        h_ref[...] = h
    aff = _sigmoid(_bdot_nt(rw_ref[...], h))
    sel = aff + rb_ref[...]
    best = gi = i1 = i2 = None
    for g in range(N_EXPERT_GROUPS):
        sc, j1, j2 = _top2(sel[g * EXPERTS_PER_GROUP:(g + 1) * EXPERTS_PER_GROUP, :])
        if g == 0:
            best, gi, i1, i2 = sc, jnp.zeros_like(j1), j1, j2
        else:
            upd = sc > best
            best = jnp.where(upd, sc, best)
            gi = jnp.where(upd, g, gi)
            i1 = jnp.where(upd, j1, i1)
            i2 = jnp.where(upd, j2, i2)
    e0 = gi * EXPERTS_PER_GROUP + i1
    e1 = gi * EXPERTS_PER_GROUP + i2
    io = lax.broadcasted_iota(I32, (N_EXPERTS, tm), 0)
    oh0 = io == e0
    oh1 = io == e1
    a0 = jnp.sum(jnp.where(oh0, aff, 0.0), axis=0, keepdims=True)
    a1 = jnp.sum(jnp.where(oh1, aff, 0.0), axis=0, keepdims=True)
    den = a0 + a1
    g0 = a0 / den
    g1 = a1 / den
    oh = jnp.where(oh0 | oh1, 1.0, 0.0)
    tr = lax.broadcasted_iota(I32, (tm, tm), 0)
    tc = lax.broadcasted_iota(I32, (tm, tm), 1)
    before = jnp.dot(oh.astype(BF16), (tr < tc).astype(BF16), preferred_element_type=F32) + carry_ref[...]
    rank0 = jnp.sum(jnp.where(oh0, before, 0.0), axis=0, keepdims=True)
    rank1 = jnp.sum(jnp.where(oh1, before, 0.0), axis=0, keepdims=True)
    carry_ref[...] = carry_ref[...] + jnp.sum(oh, axis=1, keepdims=True)
    zi = jnp.zeros((4, tm), I32)
    idx_ref[...] = jnp.concatenate([e0, e1, rank0.astype(I32), rank1.astype(I32), zi], axis=0)
    gate_ref[...] = jnp.concatenate([g0, g1, jnp.zeros((6, tm), F32)], axis=0)
    gmat_ref[...] = jnp.where(oh0, g0, 0.0) + jnp.where(oh1, g1, 0.0)
    cnt_ref[...] = jnp.broadcast_to(carry_ref[...], cnt_ref.shape)


def _moe_route(x, sc, sh, router_w_t, router_b, *, per_b_mod, rows_out, tm):
    bsz, t, _ = x.shape
    n = bsz * t
    nt = t // tm
    row = pl.BlockSpec((None, tm, D), lambda b, i: (b, i, 0))
    mod = pl.BlockSpec((None, 1, D), lambda b, i: (b, 0, 0)) if per_b_mod else row
    if rows_out:
        h_spec = pl.BlockSpec((tm, PAIRS, LANES), lambda b, i: (b * nt + i, 0, 0))
        h_shape = jax.ShapeDtypeStruct((n, PAIRS, LANES), F32)
    else:
        h_spec, h_shape = row, jax.ShapeDtypeStruct(x.shape, F32)
    tok = lambda rows: pl.BlockSpec((rows, tm), lambda b, i: (0, b * nt + i))
    return pl.pallas_call(
        functools.partial(_moe_route_kernel, tm=tm, rows_out=rows_out),
        grid=(bsz, nt),
        in_specs=[row, mod, mod, pl.BlockSpec((N_EXPERTS, D), lambda b, i: (0, 0)),
                  pl.BlockSpec((N_EXPERTS, 1), lambda b, i: (0, 0))],
        out_specs=[h_spec, tok(8), tok(8), tok(N_EXPERTS), pl.BlockSpec((N_EXPERTS, LANES), lambda b, i: (0, 0))],
        out_shape=[h_shape, jax.ShapeDtypeStruct((8, n), I32), jax.ShapeDtypeStruct((8, n), F32),
                   jax.ShapeDtypeStruct((N_EXPERTS, n), F32), jax.ShapeDtypeStruct((N_EXPERTS, LANES), F32)],
        scratch_shapes=[pltpu.VMEM((N_EXPERTS, 1), F32)],
        compiler_params=_cp(("arbitrary", "arbitrary")),
        name="moe_route",
    )(x, sc, sh, router_w_t, router_b)


def _row_copy(src, dst, sem):
    return pltpu.make_async_copy(src, dst, sem)


def _moe_dispatch_kernel(d0_ref, d1_ref, h_ref, init_ref, xs_ref, sem, *, tm):
    del init_ref
    base = pl.program_id(0) * tm

    def issue(i, c):
        _row_copy(h_ref.at[i], xs_ref.at[d0_ref[base + i]], sem.at[0]).start()
        _row_copy(h_ref.at[i], xs_ref.at[d1_ref[base + i]], sem.at[1]).start()
        return c

    lax.fori_loop(0, tm, issue, 0)

    def drain(i, c):
        _row_copy(h_ref.at[i], xs_ref.at[d0_ref[base + i]], sem.at[0]).wait()
        _row_copy(h_ref.at[i], xs_ref.at[d1_ref[base + i]], sem.at[1]).wait()
        return c

    lax.fori_loop(0, tm, drain, 0)


def _moe_dispatch(h_rows, dest0, dest1, n_slots, tm):
    n = h_rows.shape[0]
    init = jnp.zeros((n_slots, PAIRS, LANES), F32)
    return pl.pallas_call(
        functools.partial(_moe_dispatch_kernel, tm=tm),
        grid_spec=pltpu.PrefetchScalarGridSpec(
            num_scalar_prefetch=2, grid=(n // tm,),
            in_specs=[pl.BlockSpec((tm, PAIRS, LANES), lambda i, d0, d1: (i, 0, 0)),
                      pl.BlockSpec(memory_space=pl.ANY)],
            out_specs=pl.BlockSpec(memory_space=pl.ANY),
            scratch_shapes=[pltpu.SemaphoreType.DMA((2,))]),
        out_shape=jax.ShapeDtypeStruct((n_slots, PAIRS, LANES), F32),
        input_output_aliases={3: 0},
        compiler_params=_cp(("arbitrary",)),
        name="moe_dispatch",
    )(dest0, dest1, h_rows, init)


def _moe_expert_kernel(be_ref, used_ref, xs_ref, win_ref, wout_ref, ys_ref, win_bf, wout_bf):
    i = pl.program_id(0)
    fresh = (i == 0) | (be_ref[i] != be_ref[jnp.maximum(i - 1, 0)])

    @pl.when(fresh)
    def _():
        win_bf[...] = win_ref[...].astype(BF16)
        wout_bf[...] = wout_ref[...].astype(BF16)

    @pl.when(i < used_ref[0])
    def _():
        x = _from_rows(xs_ref).astype(BF16)
        hmid = jnp.dot(x, win_bf[...], preferred_element_type=F32)
        gt = hmid[:, :D_EXPERT]
        up = hmid[:, D_EXPERT:]
        act = gt * _sigmoid(gt) * up
        _to_rows(ys_ref, jnp.dot(act.astype(BF16), wout_bf[...], preferred_element_type=F32))

    @pl.when(i >= used_ref[0])
    def _():
        ys_ref[...] = jnp.zeros_like(ys_ref)


def _moe_experts_sorted(xs, blk_exp, n_used, w_in, w_out, layer):
    n_slots = xs.shape[0]
    n_blocks = n_slots // MOE_BLK
    return pl.pallas_call(
        _moe_expert_kernel,
        grid_spec=pltpu.PrefetchScalarGridSpec(
            num_scalar_prefetch=2, grid=(n_blocks,),
            in_specs=[pl.BlockSpec((MOE_BLK, PAIRS, LANES), lambda i, be, u: (i, 0, 0)),
                      pl.BlockSpec((None, None, D, 2 * D_EXPERT), lambda i, be, u: (layer, be[i], 0, 0)),
                      pl.BlockSpec((None, None, D_EXPERT, D), lambda i, be, u: (layer, be[i], 0, 0))],
            out_specs=pl.BlockSpec((MOE_BLK, PAIRS, LANES), lambda i, be, u: (i, 0, 0)),
            scratch_shapes=[pltpu.VMEM((D, 2 * D_EXPERT), BF16), pltpu.VMEM((D_EXPERT, D), BF16)]),
        out_shape=jax.ShapeDtypeStruct((n_slots, PAIRS, LANES), F32),
        compiler_params=_cp(("arbitrary",)),
        name="moe_experts_sorted",
    )(blk_exp, n_used, xs, w_in, w_out)


def _moe_combine_kernel(d0_ref, d1_ref, ys_ref, gate_ref, x_ref, gt_ref, vec_ref, o_ref, ya, yb, sem, *, tm):
    base = pl.program_id(0) * tm

    def issue(i, c):
        _row_copy(ys_ref.at[d0_ref[base + i]], ya.at[i], sem.at[0]).start()
        _row_copy(ys_ref.at[d1_ref[base + i]], yb.at[i], sem.at[1]).start()
        return c

    lax.fori_loop(0, tm, issue, 0)

    def drain(i, c):
        _row_copy(ys_ref.at[d0_ref[base + i]], ya.at[i], sem.at[0]).wait()
        _row_copy(ys_ref.at[d1_ref[base + i]], yb.at[i], sem.at[1]).wait()
        return c

    lax.fori_loop(0, tm, drain, 0)
    g0 = gate_ref[:, 0:1]
    g1 = gate_ref[:, 1:2]
    ff = _from_rows(ya) * g0 + _from_rows(yb) * g1
    o_ref[...] = _layer_norm(DN_ALPHA * x_ref[...] + gt_ref[...] * ff, vec_ref[0:1, :], vec_ref[1:2, :])


def _moe_combine(ys, dest0, dest1, gates, x, gt, vec, tm):
    bsz, t, _ = x.shape
    nt = t // tm
    return pl.pallas_call(
        functools.partial(_moe_combine_kernel, tm=tm),
        grid_spec=pltpu.PrefetchScalarGridSpec(
            num_scalar_prefetch=2, grid=(bsz * nt,),
            in_specs=[pl.BlockSpec(memory_space=pl.ANY),
                      pl.BlockSpec((tm, 2), lambda i, d0, d1: (i, 0)),
                      pl.BlockSpec((None, tm, D), lambda i, d0, d1: (i // nt, i % nt, 0)),
                      pl.BlockSpec((None, 1, D), lambda i, d0, d1: (i // nt, 0, 0)),
                      pl.BlockSpec((2, D), lambda i, d0, d1: (0, 0))],
            out_specs=pl.BlockSpec((None, tm, D), lambda i, d0, d1: (i // nt, i % nt, 0)),
            scratch_shapes=[pltpu.VMEM((tm, PAIRS, LANES), F32), pltpu.VMEM((tm, PAIRS, LANES), F32),
                            pltpu.SemaphoreType.DMA((2,))]),
        out_shape=jax.ShapeDtypeStruct(x.shape, F32),
        compiler_params=_cp(("arbitrary",)),
        name="moe_combine",
    )(dest0, dest1, ys, gates, x, gt, vec)


def _moe_sorted(x, sc, sh, gt, vec, w, layer, tm=256):
    bsz, t, _ = x.shape
    n = bsz * t
    n_blocks = (n * 2) // MOE_BLK + N_EXPERTS
    h_rows, idx, gate, _, cnt = _moe_route(x, sc, sh, w["router_w_t"], w["router_b"], per_b_mod=True,
                                           rows_out=True, tm=tm)
    counts = cnt[:, 0].astype(I32)
    padded = (counts + MOE_BLK - 1) // MOE_BLK * MOE_BLK
    pad_end = jnp.cumsum(padded)
    pad_start = pad_end - padded
    dest0 = pad_start[idx[0]] + idx[2]
    dest1 = pad_start[idx[1]] + idx[3]
    blk_start = jnp.arange(n_blocks, dtype=I32) * MOE_BLK
    blk_exp = jnp.minimum(jnp.sum((pad_end[None, :] <= blk_start[:, None]).astype(I32), axis=1), N_EXPERTS - 1)
    n_used = (pad_end[-1:] // MOE_BLK).astype(I32)
    xs = _moe_dispatch(h_rows, dest0, dest1, n_blocks * MOE_BLK, tm)
    ys = _moe_experts_sorted(xs, blk_exp, n_used, w["moe_w_in"], w["moe_w_out"], layer)
    return _moe_combine(ys, dest0, dest1, gate[:2].T, x, gt, vec, tm)


def _moe_dense_kernel(h_ref, gm_ref, win_ref, wout_ref, x_ref, gt_ref, vec_ref, o_ref, acc_ref):
    e = pl.program_id(0)

    @pl.when(e == 0)
    def _():
        acc_ref[...] = jnp.zeros_like(acc_ref)

    lane = lax.broadcasted_iota(I32, gm_ref.shape, 1)
    gcol = jnp.sum(jnp.where(lane == e, gm_ref[...], 0.0), axis=1, keepdims=True)
    hmid = _bdot(h_ref[...], win_ref[...])
    gt = hmid[:, :D_EXPERT]
    up = hmid[:, D_EXPERT:]
    y = _bdot(gt * _sigmoid(gt) * up, wout_ref[...])
    acc_ref[...] = acc_ref[...] + jnp.where(gcol != 0.0, gcol * y, 0.0)

    @pl.when(e == N_EXPERTS - 1)
    def _():
        o_ref[...] = _layer_norm(DN_ALPHA * x_ref[...] + gt_ref[...] * acc_ref[...], vec_ref[0:1, :],
                                 vec_ref[1:2, :])


def _moe_dense(x, sc, sh, gt, vec, w, layer):
    _, m, _ = x.shape
    h, _, _, gmat, _ = _moe_route(x, sc, sh, w["router_w_t"], w["router_b"], per_b_mod=False, rows_out=False,
                                  tm=m)
    full = pl.BlockSpec((m, D), lambda e: (0, 0))
    out = pl.pallas_call(
        _moe_dense_kernel,
        grid=(N_EXPERTS,),
        in_specs=[full, pl.BlockSpec((m, N_EXPERTS), lambda e: (0, 0)),
                  pl.BlockSpec((None, None, D, 2 * D_EXPERT), lambda e: (layer, e, 0, 0)),
                  pl.BlockSpec((None, None, D_EXPERT, D), lambda e: (layer, e, 0, 0)),
                  full, full, pl.BlockSpec((2, D), lambda e: (0, 0))],
        out_specs=full,
        out_shape=jax.ShapeDtypeStruct((m, D), F32),
        scratch_shapes=[pltpu.VMEM((m, D), F32)],
        compiler_params=_cp(("arbitrary",)),
        name="moe_dense",
    )(h[0], gmat.T, w["moe_w_in"], w["moe_w_out"], x[0], gt[0], vec)
    return out[None]


def _rotary(x, cos_t, sa_t, sb_t):
    reps = x.shape[1] // LANES
    tile = lambda t: t if reps == 1 else jnp.concatenate([t] * reps, axis=1)
    n = x.shape[1]
    half = ROT_DIM // 2
    return x * tile(cos_t) + pltpu.roll(x, n - half, axis=1) * tile(sa_t) + pltpu.roll(x, half, axis=1) * tile(sb_t)


def _class_rows(c, rows, dil):
    return pl.ds(c, rows, stride=dil) if dil > 1 else pl.ds(0, rows)


def _attn_pre_kernel(x_ref, sc_ref, sh_ref, ksc_ref, ksh_ref, rot_ref, wq_ref, wkv_ref, *refs, dils, tm):
    x = x_ref[...]
    cos_t, sa_t, sb_t = rot_ref[0], rot_ref[1], rot_ref[2]
    q = _rotary(_bdot(x * (1.0 + sc_ref[...]) + sh_ref[...], wq_ref[...]), cos_t, sa_t, sb_t)
    kv = _bdot(x * (1.0 + ksc_ref[...]) + ksh_ref[...], wkv_ref[...])
    k = _rotary(kv[:, :KVW], cos_t, sa_t, sb_t)
    v = kv[:, KVW:]
    k_ref, v_ref = refs[0], refs[1]
    k_ref[...] = k
    v_ref[...] = v
    if dils is None:
        refs[2][...] = q
        return
    ng = len(dils)
    q_refs, kb_refs, vb_refs = refs[2:2 + ng], refs[2 + ng:2 + 2 * ng], refs[2 + 2 * ng:2 + 3 * ng]
    qs, kvs = refs[2 + 3 * ng:]
    for j in range(q.shape[1] // LANES):
        qs[j] = q[:, j * LANES:(j + 1) * LANES]
    for j in range(kv.shape[1] // LANES):
        kvs[j] = (k if j < KVW // LANES else v)[:, (j % (KVW // LANES)) * LANES:(j % (KVW // LANES) + 1) * LANES]

    def gather(ref, tiles, sl):
        return jnp.concatenate([ref[j, sl, :] for j in tiles], axis=1).astype(BF16)

    kt = KVW // LANES
    for g, dil in enumerate(dils):
        rows = tm // dil
        for c in range(dil):
            sl = _class_rows(c, rows, dil)
            q_refs[g][c] = gather(qs, range(g * PAIRS, (g + 1) * PAIRS), sl)
            kb_refs[g][c] = gather(kvs, range(kt), sl)
            vb_refs[g][c] = gather(kvs, range(kt, 2 * kt), sl)


def _attn_pre(x, sc, sh, ksc, ksh, rot, w_q, w_kv, *, per_b_mod, tm, dils=None):
    bsz, t, _ = x.shape
    row = lambda wd: pl.BlockSpec((None, tm, wd), lambda b, i: (b, i, 0))
    mod = pl.BlockSpec((None, 1, D), lambda b, i: (b, 0, 0)) if per_b_mod else row(D)
    nq = w_q.shape[1]
    out_specs = [row(KVW), row(KVW)]
    out_shape = [jax.ShapeDtypeStruct((bsz, t, KVW), F32)] * 2
    scratch = []
    if dils is None:
        out_specs.append(row(nq))
        out_shape.append(jax.ShapeDtypeStruct((bsz, t, nq), F32))
    else:
        for wd in (D, KVW, KVW):
            for dil in dils:
                out_specs.append(pl.BlockSpec((None, dil, tm // dil, wd), lambda b, i: (b, 0, i, 0)))
                out_shape.append(jax.ShapeDtypeStruct((bsz, dil, t // dil, wd), BF16))
        scratch = [pltpu.VMEM((nq // LANES, tm, LANES), F32), pltpu.VMEM((2 * KVW // LANES, tm, LANES), F32)]
    outs = pl.pallas_call(
        functools.partial(_attn_pre_kernel, dils=dils, tm=tm),
        grid=(bsz, t // tm),
        in_specs=[row(D), mod, mod, mod, mod, pl.BlockSpec((3, tm, LANES), lambda b, i: (0, i, 0)),
                  pl.BlockSpec((D, nq), lambda b, i: (0, 0)), pl.BlockSpec((D, 2 * KVW), lambda b, i: (0, 0))],
        out_specs=out_specs,
        out_shape=out_shape,
        scratch_shapes=scratch,
        compiler_params=_cp(("arbitrary", "arbitrary")),
        name="attn_pre",
    )(x, sc, sh, ksc, ksh, rot, w_q, w_kv)
    if dils is None:
        return outs
    ng = len(dils)
    return outs[0], outs[1], outs[2:2 + ng], outs[2 + ng:2 + 2 * ng], outs[2 + 2 * ng:]


def _attn_band_kernel(q_ref, ka_ref, kb_ref, va_ref, vb_ref, o_ref, lse_ref):
    i = pl.program_id(2)
    bq = ATT_WIN
    kcat = jnp.concatenate([ka_ref[...], kb_ref[...]], axis=0)
    vcat = jnp.concatenate([va_ref[...], vb_ref[...]], axis=0)
    qs = jnp.concatenate([q_ref[:, rep * KVW:(rep + 1) * KVW] for rep in range(REP)], axis=0)
    qrow = lax.broadcasted_iota(I32, (REP * bq, 2 * bq), 0) % bq
    kcol = lax.broadcasted_iota(I32, (REP * bq, 2 * bq), 1)
    valid = (kcol >= qrow) & (kcol <= qrow + bq) & ((i > 0) | (kcol >= bq))
    lane = lax.broadcasted_iota(I32, (REP * bq, KVW), 1) // HD
    lse_lane = lax.broadcasted_iota(I32, (bq, LANES), 1)
    lse = jnp.zeros((bq, LANES), F32)
    o = jnp.zeros((REP * bq, KVW), F32)
    linv = jnp.ones((REP * bq, KVW), F32)
    scale = HD ** -0.5
    for h in range(KV_HEADS):
        qm = jnp.where(lane == h, qs, jnp.zeros_like(qs))
        s = lax.dot_general(qm, kcat, (((1,), (1,)), ((), ())), preferred_element_type=F32) * scale
        s = jnp.where(valid, s, -jnp.inf)
        m = jnp.max(s, axis=-1, keepdims=True)
        p = jnp.exp(s - m)
        l = jnp.sum(p, axis=-1, keepdims=True)
        o = jnp.where(lane == h, jnp.dot(p.astype(BF16), vcat, preferred_element_type=F32), o)
        linv = jnp.where(lane == h, 1.0 / l, linv)
        lse_h = m + jnp.log(l)
        for rep in range(REP):
            lse = jnp.where(lse_lane == rep * KV_HEADS + h, lse_h[rep * bq:(rep + 1) * bq], lse)
    o = o * linv
    for rep in range(REP):
        o_ref[:, rep * KVW:(rep + 1) * KVW] = o[rep * bq:(rep + 1) * bq]
    lse_ref[...] = lse


def _attn_band(q, kb, vb, dil):
    bsz, _, tc, _ = q.shape
    nb = tc // ATT_WIN
    blk = lambda wd: pl.BlockSpec((None, None, ATT_WIN, wd), lambda b, c, i: (b, c, i, 0))
    prev = pl.BlockSpec((None, None, ATT_WIN, KVW), lambda b, c, i: (b, c, jnp.maximum(i - 1, 0), 0))
    return pl.pallas_call(
        _attn_band_kernel,
        grid=(bsz, dil, nb),
        in_specs=[blk(D), prev, blk(KVW), prev, blk(KVW)],
        out_specs=[blk(D), blk(LANES)],
        out_shape=[jax.ShapeDtypeStruct((bsz, dil, tc, D), F32), jax.ShapeDtypeStruct((bsz, dil, tc, LANES), F32)],
        compiler_params=_cp(("arbitrary", "arbitrary", "arbitrary")),
        name=f"attn_band_d{dil}",
    )(q, kb, kb, vb, vb)


def _attn_step_kernel(q_ref, kn_ref, vn_ref, k0_ref, k1_ref, k2_ref, v0_ref, v1_ref, v2_ref, o_ref, lse_ref):
    scale = HD ** -0.5
    rnd = lambda t: t.astype(BF16).astype(F32)
    lse_lane = lax.broadcasted_iota(I32, (1, LANES), 1)
    for g, (kc_ref, vc_ref) in enumerate(((k0_ref, v0_ref), (k1_ref, v1_ref), (k2_ref, v2_ref))):
        lse = jnp.zeros((1, LANES), F32)
        for h in range(KV_HEADS):
            hs = slice(h * HD, (h + 1) * HD)
            qh = jnp.concatenate([q_ref[:, g * D + r * KVW + h * HD:g * D + r * KVW + (h + 1) * HD]
                                  for r in range(REP)], axis=0)
            s_c = _bdot_nt(qh, kc_ref[:, h, :]) * scale
            s_n = jnp.sum(rnd(qh) * rnd(kn_ref[:, hs]), axis=-1, keepdims=True) * scale
            m = jnp.maximum(jnp.max(s_c, axis=-1, keepdims=True), s_n)
            p_c = jnp.exp(s_c - m)
            p_n = jnp.exp(s_n - m)
            l = jnp.sum(p_c, axis=-1, keepdims=True) + p_n
            oh = (_bdot(p_c, vc_ref[:, h, :]) + rnd(p_n) * rnd(vn_ref[:, hs])) / l
            lse_h = m + jnp.log(l)
            for r in range(REP):
                o_ref[g:g + 1, r * KVW + h * HD:r * KVW + (h + 1) * HD] = oh[r:r + 1]
                lse = jnp.where(lse_lane == r * KV_HEADS + h, lse_h[r:r + 1], lse)
        lse_ref[g:g + 1, :] = lse


def _attn_step(q, k_new, v_new, cache_k, cache_v):
    bsz = q.shape[0]
    wbuf = cache_k.shape[1]
    views, specs = [], []
    for cache in (cache_k, cache_v):
        for _, dil in DILATED_GROUPS:
            tc = wbuf // dil
            views.append(cache.reshape(bsz, tc, dil, KV_HEADS, HD))
            specs.append(pl.BlockSpec((None, ATT_WIN, None, KV_HEADS, HD),
                                      lambda b, tc=tc: (b, tc // ATT_WIN - 1, 0, 0, 0)))
    vec = lambda wd: pl.BlockSpec((None, 1, wd), lambda b: (b, 0, 0))
    o, lse = pl.pallas_call(
        _attn_step_kernel,
        grid=(bsz,),
        in_specs=[vec(3 * D), vec(KVW), vec(KVW)] + specs,
        out_specs=[pl.BlockSpec((None, 3, D), lambda b: (b, 0, 0)), pl.BlockSpec((None, 3, LANES), lambda b: (b, 0, 0))],
        out_shape=[jax.ShapeDtypeStruct((bsz, 3, D), F32), jax.ShapeDtypeStruct((bsz, 3, LANES), F32)],
        compiler_params=_cp(("arbitrary",)),
        name="attn_step",
    )(q.reshape(bsz, 1, 3 * D), k_new.reshape(bsz, 1, KVW), v_new.reshape(bsz, 1, KVW), *views)
    return o.transpose(1, 0, 2), lse.transpose(1, 0, 2)


def _attn_post_kernel(o0_ref, o1_ref, o2_ref, l0_ref, l1_ref, l2_ref, x_ref, gt_ref, vec_ref, wo_ref, out_ref,
                      *scratch, dils, tm):
    o_refs = [o0_ref, o1_ref, o2_ref]
    l_refs = [l0_ref, l1_ref, l2_ref]
    if dils is not None:
        for g, dil in enumerate(dils):
            if dil == 1:
                o_refs[g], l_refs[g] = o_refs[g].at[0], l_refs[g].at[0]
                continue
            so, sl = scratch[2 * g], scratch[2 * g + 1]
            for c in range(dil):
                rows = _class_rows(c, tm // dil, dil)
                oc = o_refs[g][c]
                for j in range(PAIRS):
                    so[j, rows, :] = oc[:, j * LANES:(j + 1) * LANES]
                sl[rows, :] = l_refs[g][c]
            o_refs[g], l_refs[g] = so, sl
    o0, o1, o2 = (_from_pairs(r) if len(r.shape) == 3 else r[...] for r in o_refs)
    l0, l1, l2 = (r[...] for r in l_refs)
    m = jnp.maximum(jnp.maximum(l0, l1), l2)
    e0, e1, e2 = jnp.exp(l0 - m), jnp.exp(l1 - m), jnp.exp(l2 - m)
    den = e0 + e1 + e2
    r = lax.broadcasted_iota(I32, (LANES, D), 0)
    c = lax.broadcasted_iota(I32, (LANES, D), 1) // HD
    spread = (r == c).astype(BF16)
    mix = (_split_dot(e0 / den, spread) * o0 + _split_dot(e1 / den, spread) * o1
           + _split_dot(e2 / den, spread) * o2)
    out = _bdot(mix, wo_ref[...])
    out_ref[...] = _layer_norm(DN_ALPHA * x_ref[...] + gt_ref[...] * out, vec_ref[0:1, :], vec_ref[1:2, :])


def _attn_post(os, lses, x, gt, vec, w_o, *, per_b_mod, tm, dils=None):
    bsz, t, _ = x.shape
    row = lambda wd: pl.BlockSpec((None, tm, wd), lambda b, i: (b, i, 0))
    mod = pl.BlockSpec((None, 1, D), lambda b, i: (b, 0, 0)) if per_b_mod else row(D)
    if dils is None:
        o_specs, l_specs, scratch = [row(D)] * 3, [row(LANES)] * 3, []
    else:
        cls = lambda dil, wd: pl.BlockSpec((None, dil, tm // dil, wd), lambda b, i: (b, 0, i, 0))
        o_specs = [cls(dil, D) for dil in dils]
        l_specs = [cls(dil, LANES) for dil in dils]
        scratch = [pltpu.VMEM(shape, F32) for _ in dils for shape in ((PAIRS, tm, LANES), (tm, LANES))]
    return pl.pallas_call(
        functools.partial(_attn_post_kernel, dils=dils, tm=tm),
        grid=(bsz, t // tm),
        in_specs=o_specs + l_specs + [row(D), mod, pl.BlockSpec((2, D), lambda b, i: (0, 0)),
                                      pl.BlockSpec((D, D), lambda b, i: (0, 0))],
        out_specs=row(D),
        out_shape=jax.ShapeDtypeStruct(x.shape, F32),
        scratch_shapes=scratch,
        compiler_params=_cp(("arbitrary", "arbitrary")),
        name="attn_post",
    )(*os, *lses, x, gt, vec, w_o)


def _rot_tables(pos):
    half = ROT_DIM // 2
    inv = ROPE_THETA ** (-jnp.arange(half, dtype=F32) * 2.0 / ROT_DIM)
    ang = pos.astype(F32)[:, None] * inv[None, :]
    cos, sin = jnp.cos(ang), jnp.sin(ang)
    t = pos.shape[0]
    one = jnp.ones((t, HD - ROT_DIM), F32)
    zero = jnp.zeros((t, HD - ROT_DIM), F32)
    zh = jnp.zeros((t, half), F32)
    cos_t = jnp.concatenate([cos, cos, one], axis=1)
    sa_t = jnp.concatenate([-sin, zh, zero], axis=1)
    sb_t = jnp.concatenate([zh, sin, zero], axis=1)
    return jnp.stack([jnp.tile(x, (1, 2)) for x in (cos_t, sa_t, sb_t)])


def _prep_weights(p):
    pad_c = lambda m, n: jnp.pad(m, ((0, 0), (0, n - m.shape[1])))
    pad_r = lambda m, n: jnp.pad(m, ((0, n - m.shape[0]), (0, 0)))
    w = {}
    w["mu"] = p["rwkv_mu"][0]
    w["w_rkv"] = p["rwkv_w_rkv"][0].astype(BF16)
    w["w1"] = pad_c(p["rwkv_w1"][0], LANES).astype(BF16)
    w["w2"] = pad_r(p["rwkv_w2"][0], LANES).astype(BF16)
    w["a1"] = pad_c(p["rwkv_a1"][0], LANES).astype(BF16)
    w["a2"] = pad_r(p["rwkv_a2"][0], LANES).astype(BF16)
    w["g1"] = pad_c(p["rwkv_g1"][0], 2 * LANES).astype(BF16)
    w["g2"] = pad_r(p["rwkv_g2"][0], 2 * LANES).astype(BF16)
    w["vec"] = jnp.stack([p["rwkv_w0"][0], p["rwkv_a0"][0], p["rwkv_k_k"][0], p["rwkv_k_a"][0]])
    w["post_vec"] = jnp.stack([p["rwkv_r_k"][0].reshape(D), p["rwkv_lnx_g"][0], p["rwkv_lnx_b"][0],
                               p["ln_g"][0, 0], p["ln_b"][0, 0]])
    w["rwkv_w_o"] = p["rwkv_w_o"][0].astype(BF16)
    wq = p["w_q"][0].reshape(D, 3, KV_HEADS, REP, HD).transpose(0, 1, 3, 2, 4).reshape(D, 3 * D)
    w["w_q"] = wq.astype(BF16)
    w["w_kv"] = p["w_kv"].astype(BF16)
    wo = p["w_o_attn"][0].reshape(KV_HEADS, REP, HD, D).transpose(1, 0, 2, 3).reshape(D, D)
    w["w_o_attn"] = wo.astype(BF16)
    w["router_w_t"] = p["router_w"].T.astype(BF16)
    w["router_b"] = p["router_b"].reshape(N_EXPERTS, 1)
    w["moe_w_in"] = p["moe_w_in"]
    w["moe_w_out"] = p["moe_w_out"]
    w["ln"] = [[jnp.stack([p["ln_g"][l, i], p["ln_b"][l, i]]) for i in range(2)] for l in range(DEPTH)]
    return w


def _modulations(c_prompt, c_sample, p):
    nb = c_prompt.shape[0]
    c = jnp.concatenate([c_prompt, c_sample], axis=0)
    pad = (-c.shape[0]) % 8
    c = jnp.pad(c, ((0, pad), (0, 0)))
    m3 = _ada_linear(c, p["ada_w"].reshape(2 * DEPTH, D, 3 * D), p["ada_b"].reshape(2 * DEPTH, 3 * D))
    m2 = _ada_linear(c, p["kv_ada_w"][None], p["kv_ada_b"][None])[0]
    n_all = nb + c_sample.shape[0]

    def split(m, parts, lo, hi):
        return [m[lo:hi, i * D:(i + 1) * D] for i in range(parts)]

    out = {}
    for name, lo, hi in (("prompt", 0, nb), ("sample", nb, n_all)):
        out[name] = {"ada": [[split(m3[2 * l + i], 3, lo, hi) for i in range(2)] for l in range(DEPTH)],
                     "kv": split(m2, 2, lo, hi)}
    return out


def _trunk_prompt(x, mods, w):
    bsz, t, _ = x.shape
    per_b = lambda m: m[:, None, :]
    sh, sc, gt = (per_b(m) for m in mods["ada"][0][0])
    zeros = jnp.zeros((bsz, 1, D), F32)
    r, lw, k, v, a, b, g, last = _rwkv_pre(x, sc, sh, zeros, w, seq_mode=True, tm=256)
    y, zf = _wkv_chunked(r, lw, k, v, a, b)
    x = _rwkv_post(y, r, k, v, g, x, gt, w["post_vec"], w["rwkv_w_o"], pairs=True, tm=256)
    zf = zf.reshape(bsz, PAIRS, 2, HD, 2, HD)
    wkv = jnp.stack([zf[:, :, 0, :, 0, :], zf[:, :, 1, :, 1, :]], axis=2)
    wkv = wkv.reshape(bsz, HEADS, HD, HD).transpose(0, 1, 3, 2)
    sh2, sc2, gt2 = (per_b(m) for m in mods["ada"][0][1])
    x = _moe_sorted(x, sc2, sh2, gt2, w["ln"][0][1], w, 0)

    sh, sc, gt = (per_b(m) for m in mods["ada"][1][0])
    ksh, ksc = (per_b(m) for m in mods["kv"])
    rot = _rot_tables(jnp.arange(t, dtype=I32))
    dils = tuple(dil for _, dil in DILATED_GROUPS)
    k_new, v_new, qs, kbs, vbs = _attn_pre(x, sc, sh, ksc, ksh, rot, w["w_q"], w["w_kv"], per_b_mod=True,
                                           tm=256, dils=dils)
    os, lses = zip(*[_attn_band(qs[g], kbs[g], vbs[g], dil) for g, dil in enumerate(dils)])
    x = _attn_post(os, lses, x, gt, w["ln"][1][0], w["w_o_attn"], per_b_mod=True, tm=256, dils=dils)
    sh2, sc2, gt2 = (per_b(m) for m in mods["ada"][1][1])
    x = _moe_sorted(x, sc2, sh2, gt2, w["ln"][1][1], w, 1)
    return x, wkv[None], last.reshape(1, bsz, D), k_new, v_new


def _trunk_sample(x, mods, state_wkv, state_shift, cache_k, cache_v, w):
    bsz = x.shape[0]
    xs = x.reshape(1, bsz, D)
    row = lambda m: m[None]
    sh, sc, gt = (row(m) for m in mods["ada"][0][0])
    r, lw, k, v, a, b, g, hm = _rwkv_pre(xs, sc, sh, state_shift[0][None], w, seq_mode=False, tm=bsz)
    s_new, y = _wkv_step(state_wkv[0], v[0], r[0], lw[0], k[0], a[0], b[0])
    xs = _rwkv_post(y[None], r, k, v, g, xs, gt, w["post_vec"], w["rwkv_w_o"], pairs=False, tm=bsz)
    sh2, sc2, gt2 = (row(m) for m in mods["ada"][0][1])
    xs = _moe_dense(xs, sc2, sh2, gt2, w["ln"][0][1], w, 0)

    sh, sc, gt = (row(m) for m in mods["ada"][1][0])
    ksh, ksc = (row(m) for m in mods["kv"])
    rot = _rot_tables(jnp.full((bsz,), PAST_LEN, I32))
    k_new, v_new, q = _attn_pre(xs, sc, sh, ksc, ksh, rot, w["w_q"], w["w_kv"], per_b_mod=False, tm=bsz)
    o, lse = _attn_step(q[0], k_new[0], v_new[0], cache_k, cache_v)
    xs = _attn_post([o[i][None] for i in range(3)], [lse[i][None] for i in range(3)], xs, gt, w["ln"][1][0],
                    w["w_o_attn"], per_b_mod=False, tm=bsz)
    sh2, sc2, gt2 = (row(m) for m in mods["ada"][1][1])
    xs = _moe_dense(xs, sc2, sh2, gt2, w["ln"][1][1], w, 1)
    return (xs.reshape(bsz, 1, D), s_new[None], hm, k_new.reshape(bsz, 1, KV_HEADS, HD),
            v_new.reshape(bsz, 1, KV_HEADS, HD))


def kernel(x_prompt, x_sample, state_wkv, state_shift, cache_k, cache_v, c_prompt, c_sample, ada_w, ada_b, ln_g, ln_b, rwkv_mu, rwkv_w_rkv, rwkv_w0, rwkv_w1, rwkv_w2, rwkv_a0, rwkv_a1, rwkv_a2, rwkv_g1, rwkv_g2, rwkv_k_k, rwkv_k_a, rwkv_r_k, rwkv_lnx_g, rwkv_lnx_b, rwkv_w_o, w_q, w_kv, kv_ada_w, kv_ada_b, w_o_attn, router_w, router_b, moe_w_in, moe_w_out):
    p = {"ada_w": ada_w, "ada_b": ada_b, "ln_g": ln_g, "ln_b": ln_b, "rwkv_mu": rwkv_mu, "rwkv_w_rkv": rwkv_w_rkv,
         "rwkv_w0": rwkv_w0, "rwkv_w1": rwkv_w1, "rwkv_w2": rwkv_w2, "rwkv_a0": rwkv_a0, "rwkv_a1": rwkv_a1,
         "rwkv_a2": rwkv_a2, "rwkv_g1": rwkv_g1, "rwkv_g2": rwkv_g2, "rwkv_k_k": rwkv_k_k, "rwkv_k_a": rwkv_k_a,
         "rwkv_r_k": rwkv_r_k, "rwkv_lnx_g": rwkv_lnx_g, "rwkv_lnx_b": rwkv_lnx_b, "rwkv_w_o": rwkv_w_o,
         "w_q": w_q, "w_kv": w_kv, "kv_ada_w": kv_ada_w, "kv_ada_b": kv_ada_b, "w_o_attn": w_o_attn,
         "router_w": router_w, "router_b": router_b, "moe_w_in": moe_w_in, "moe_w_out": moe_w_out}
    w = _prep_weights(p)
    mods = _modulations(c_prompt, c_sample, p)
    bp, tp, _ = x_prompt.shape
    y_p, wkv_p, shift_p, k_p, v_p = _trunk_prompt(x_prompt, mods["prompt"], w)
    keep = min(PAST_LEN, tp)
    k_p = k_p[:, tp - keep:].reshape(bp, keep, KV_HEADS, HD)
    v_p = v_p[:, tp - keep:].reshape(bp, keep, KV_HEADS, HD)
    y_s, wkv_s, shift_s, k_s, v_s = _trunk_sample(x_sample, mods["sample"], state_wkv, state_shift, cache_k,
                                                  cache_v, w)
    return (y_p, y_s, wkv_p, shift_p, k_p, v_p, wkv_s, shift_s, k_s, v_s)
```

```python
import functools

import jax
import jax.numpy as jnp
from jax import lax
from jax.experimental import pallas as pl
from jax.experimental.pallas import tpu as pltpu

F32 = jnp.float32
BF16 = jnp.bfloat16
I32 = jnp.int32

D = 1024
HEADS = 16
HD = 64
LANES = 128
PAIRS = D // LANES
CHUNK = 64
WKV_GROUP = 8
N_EXPERTS = 32
EXPERTS_PER_GROUP = 8
N_EXPERT_GROUPS = 4
D_EXPERT = 512
MOE_BLK = 256
KV_HEADS = 4
REP = 4
KVW = KV_HEADS * HD
DILATED_GROUPS = ((128, 1), (512, 4), (2048, 16))
ATT_WIN = 128
PAST_LEN = 2048
ROT_DIM = 16
ROPE_THETA = 500000.0
DEPTH = 2
DN_ALPHA = (2 * DEPTH) ** 0.25
LN_EPS = 1e-5
GN_EPS = 64e-5
VMEM_LIMIT = 56 * 1024 * 1024


def _cp(sem):
    return pltpu.CompilerParams(dimension_semantics=sem, vmem_limit_bytes=VMEM_LIMIT)


def _bdot(a, b):
    return jnp.dot(a.astype(BF16), b.astype(BF16), preferred_element_type=F32)


def _bdot_nt(a, b):
    return lax.dot_general(a.astype(BF16), b.astype(BF16), (((1,), (1,)), ((), ())),
                           preferred_element_type=F32)


def _bdot_tn(a, b):
    return lax.dot_general(a.astype(BF16), b.astype(BF16), (((0,), (0,)), ((), ())),
                           preferred_element_type=F32)


def _split_dot(x, m):
    hi = x.astype(BF16)
    lo = (x - hi.astype(F32)).astype(BF16)
    return (jnp.dot(hi, m, preferred_element_type=F32) + jnp.dot(lo, m, preferred_element_type=F32))


def _head_ones():
    r = lax.broadcasted_iota(I32, (LANES, LANES), 0) // HD
    c = lax.broadcasted_iota(I32, (LANES, LANES), 1) // HD
    return (r == c).astype(BF16)


def _segsum(x, ones_bd):
    cols = [_split_dot(x[:, c * LANES:(c + 1) * LANES], ones_bd) for c in range(x.shape[1] // LANES)]
    return cols[0] if len(cols) == 1 else jnp.concatenate(cols, axis=1)


def _layer_norm(x, g, b):
    mu = jnp.mean(x, axis=-1, keepdims=True)
    xc = x - mu
    var = jnp.mean(xc * xc, axis=-1, keepdims=True)
    return xc * lax.rsqrt(var + LN_EPS) * g + b


def _sigmoid(x):
    return 1.0 / (1.0 + jnp.exp(-x))


def _to_pairs(ref, val):
    for p in range(PAIRS):
        ref[p] = val[:, p * LANES:(p + 1) * LANES]


def _from_pairs(ref):
    return jnp.concatenate([ref[p] for p in range(PAIRS)], axis=1)


def _ada_kernel(c_ref, w_ref, b_ref, o_ref):
    c = c_ref[...]
    o_ref[...] = _bdot(c * _sigmoid(c), w_ref[...]) + b_ref[...]


def _ada_linear(c, w, b, tn=512):
    s, _, n = w.shape
    m = c.shape[0]
    return pl.pallas_call(
        _ada_kernel,
        grid=(s, n // tn),
        in_specs=[pl.BlockSpec((m, D), lambda i, j: (0, 0)),
                  pl.BlockSpec((None, D, tn), lambda i, j: (i, 0, j)),
                  pl.BlockSpec((None, 1, tn), lambda i, j: (i, 0, j))],
        out_specs=pl.BlockSpec((None, m, tn), lambda i, j: (i, 0, j)),
        out_shape=jax.ShapeDtypeStruct((s, m, n), F32),
        compiler_params=_cp(("arbitrary", "arbitrary")),
        name="ada_linear",
    )(c, w, b.reshape(s, 1, n))


def _rwkv_pre_kernel(x_ref, sc_ref, sh_ref, prev_ref, mu_ref, wrkv_ref, w1_ref, w2_ref, a1_ref, a2_ref,
                     g1_ref, g2_ref, vec_ref,
                     r_ref, lw_ref, k_ref, v_ref, a_ref, b_ref, g_ref, hm_ref, carry_ref, *, seq_mode, tm):
    x = x_ref[...]
    hm = x * (1.0 + sc_ref[...]) + sh_ref[...]
    if seq_mode:
        @pl.when(pl.program_id(1) == 0)
        def _():
            carry_ref[...] = prev_ref[...]
        row = lax.broadcasted_iota(I32, hm.shape, 0)
        hprev = jnp.where(row == 0, carry_ref[...], pltpu.roll(hm, 1, axis=0))
        carry_ref[...] = hm[tm - 1:tm, :]
        hm_ref[...] = hm[tm - 1:tm, :]
    else:
        hprev = prev_ref[...]
        hm_ref[...] = hm
    xx = hprev - hm

    def mix(i):
        return hm + xx * mu_ref[i:i + 1, :]

    w0, a0, k_k, k_a = (vec_ref[i:i + 1, :] for i in range(4))
    r = _bdot(mix(0), wrkv_ref[0])
    k = _bdot(mix(2), wrkv_ref[1])
    v = _bdot(mix(3), wrkv_ref[2])
    wl = w0 + _bdot(jnp.tanh(_bdot(mix(1), w1_ref[...])), w2_ref[...])
    z = -wl
    softplus = jnp.maximum(z, 0.0) + jnp.log1p(jnp.exp(-jnp.abs(z)))
    lw = -jnp.exp(-softplus - 0.5)
    a_lr = _sigmoid(a0 + _bdot(_bdot(mix(4), a1_ref[...]), a2_ref[...]))
    g = _bdot(_sigmoid(_bdot(mix(5), g1_ref[...])), g2_ref[...])
    kk = k * k_k
    kk = kk / jnp.maximum(jnp.sqrt(_segsum(kk * kk, _head_ones())), 1e-12)
    kmod = k * (1.0 + (a_lr - 1.0) * k_a)
    outs = ((r_ref, r), (lw_ref, lw), (k_ref, kmod), (v_ref, v), (a_ref, -kk), (b_ref, kk * a_lr), (g_ref, g))
    for ref, val in outs:
        if seq_mode:
            _to_pairs(ref, val)
        else:
            ref[...] = val


def _rwkv_pre(x, sc, sh, prev, w, *, seq_mode, tm):
    bsz, t, _ = x.shape
    grid = (bsz, t // tm)
    row = pl.BlockSpec((None, tm, D), lambda b, i: (b, i, 0))
    per_b = pl.BlockSpec((None, 1, D), lambda b, i: (b, 0, 0))
    mod = per_b if seq_mode else row

    def const(shape):
        return pl.BlockSpec(shape, lambda b, i: (0,) * len(shape))

    if seq_mode:
        out_big = pl.BlockSpec((None, PAIRS, tm, LANES), lambda b, i: (b, 0, i, 0))
        big_shape = jax.ShapeDtypeStruct((bsz, PAIRS, t, LANES), F32)
        hm_spec, hm_shape = per_b, jax.ShapeDtypeStruct((bsz, 1, D), F32)
    else:
        out_big, big_shape = row, jax.ShapeDtypeStruct((bsz, t, D), F32)
        hm_spec, hm_shape = row, jax.ShapeDtypeStruct((bsz, t, D), F32)
    return pl.pallas_call(
        functools.partial(_rwkv_pre_kernel, seq_mode=seq_mode, tm=tm),
        grid=grid,
        in_specs=[row, mod, mod, mod, const((6, D)), const((3, D, D)), const((D, LANES)), const((LANES, D)),
                  const((D, LANES)), const((LANES, D)), const((D, 2 * LANES)), const((2 * LANES, D)),
                  const((4, D))],
        out_specs=[out_big] * 7 + [hm_spec],
        out_shape=[big_shape] * 7 + [hm_shape],
        scratch_shapes=[pltpu.VMEM((1, D), F32)],
        compiler_params=_cp(("arbitrary", "arbitrary")),
        name="rwkv_pre",
    )(x, sc, sh, prev, w["mu"], w["w_rkv"], w["w1"], w["w2"], w["a1"], w["a2"], w["g1"], w["g2"], w["vec"])


def _wkv_chunk_kernel(r_ref, lw_ref, k_ref, v_ref, a_ref, b_ref, y_ref, zf_ref, z_ref, *, n_pairs, group):
    c = pl.program_id(0)

    @pl.when(c == 0)
    def _():
        z_ref[...] = jnp.zeros_like(z_ref)

    L = CHUNK
    row_l = lax.broadcasted_iota(I32, (L, L), 0)
    col_l = lax.broadcasted_iota(I32, (L, L), 1)
    tri_incl = (row_l >= col_l).astype(BF16)
    lane = lax.broadcasted_iota(I32, (L, LANES), 1)
    head0 = lane < HD
    row = lax.broadcasted_iota(I32, (LANES, LANES), 0)
    col = lax.broadcasted_iota(I32, (LANES, LANES), 1)
    strict = row > col
    incl = row >= col
    eye = (row == col).astype(F32)

    def expand(x):
        return jnp.concatenate([jnp.where(head0, x, 0.0), jnp.where(head0, 0.0, x)], axis=0)

    def group_body(gi, carry):
        ids = [gi * group + j for j in range(group)]
        bp = [(i // PAIRS, i % PAIRS) for i in ids]
        G = range(group)
        r = [r_ref[b_i, p_i] for b_i, p_i in bp]
        lw = [lw_ref[b_i, p_i] for b_i, p_i in bp]
        k = [k_ref[b_i, p_i] for b_i, p_i in bp]
        v = [v_ref[b_i, p_i] for b_i, p_i in bp]
        a = [a_ref[b_i, p_i] for b_i, p_i in bp]
        b = [b_ref[b_i, p_i] for b_i, p_i in bp]
        z0 = [z_ref[i] for i in ids]
        cum = [_split_dot_left(tri_incl, lw[j]) for j in G]
        cum_l = [cum[j][L - 1:L, :] for j in G]
        inv = [jnp.exp(-cum[j]) for j in G]
        tail = [jnp.exp(cum_l[j] - cum[j]) for j in G]
        a_e = [expand(a[j] * jnp.exp(cum[j] - lw[j])) for j in G]
        r_e = [expand(r[j] * jnp.exp(cum[j])) for j in G]
        b_e = [expand(b[j] * inv[j]) for j in G]
        k_e = [expand(k[j] * inv[j]) for j in G]
        bd_e = [expand(b[j] * tail[j]) for j in G]
        kd_e = [expand(k[j] * tail[j]) for j in G]
        v_e = [expand(v[j]) for j in G]
        gm = [_bdot_nt(jnp.concatenate([a_e[j], r_e[j]], axis=0), jnp.concatenate([b_e[j], k_e[j]], axis=0))
              for j in G]
        m_ab = [jnp.where(strict, gm[j][:LANES, :LANES], 0.0) for j in G]
        m_ak = [jnp.where(strict, gm[j][:LANES, LANES:], 0.0) for j in G]
        m_rb = [jnp.where(incl, gm[j][LANES:, :LANES], 0.0) for j in G]
        m_rk = [jnp.where(incl, gm[j][LANES:, LANES:], 0.0) for j in G]
        tinv = [eye + m_ab[j] for j in G]
        pw = m_ab
        for _ in range(5):
            pw = [_bdot(pw[j], pw[j]) for j in G]
            tinv = [tinv[j] + _bdot(pw[j], tinv[j]) for j in G]
        mv = [_bdot(jnp.concatenate([m_ak[j], m_rk[j]], axis=0), v_e[j]) for j in G]
        taw = [_bdot(tinv[j], jnp.concatenate([a_e[j], mv[j][:LANES]], axis=1)) for j in G]
        qy = [_bdot(m_rb[j], taw[j]) for j in G]
        pp = [_bdot_tn(bd_e[j], taw[j]) for j in G]
        kv = [_bdot_tn(kd_e[j], v_e[j]) for j in G]
        phi = [eye * jnp.exp(cum_l[j]) + pp[j][:, :LANES] for j in G]
        q = [r_e[j] + qy[j][:, :LANES] for j in G]
        zq = [_bdot(jnp.concatenate([phi[j], q[j]], axis=0), z0[j]) for j in G]
        for j in G:
            z_ref[ids[j]] = zq[j][:LANES] + pp[j][:, LANES:] + kv[j]
            y_e = zq[j][LANES:] + qy[j][:, LANES:] + mv[j][LANES:]
            y_ref[bp[j][0], bp[j][1]] = y_e[:L] + y_e[L:]
        return carry

    lax.fori_loop(0, n_pairs // group, group_body, 0)

    @pl.when(c == pl.num_programs(0) - 1)
    def _():
        zf_ref[...] = z_ref[...]


def _split_dot_left(m, x):
    hi = x.astype(BF16)
    lo = (x - hi.astype(F32)).astype(BF16)
    return (jnp.dot(m, hi, preferred_element_type=F32) + jnp.dot(m, lo, preferred_element_type=F32))


def _wkv_chunked(r, lw, k, v, a, b):
    bsz, _, t, _ = r.shape
    n_pairs = bsz * PAIRS
    blk = pl.BlockSpec((bsz, PAIRS, CHUNK, LANES), lambda c: (0, 0, c, 0))
    return pl.pallas_call(
        functools.partial(_wkv_chunk_kernel, n_pairs=n_pairs, group=WKV_GROUP),
        grid=(t // CHUNK,),
        in_specs=[blk] * 6,
        out_specs=[blk, pl.BlockSpec((n_pairs, LANES, LANES), lambda c: (0, 0, 0))],
        out_shape=[jax.ShapeDtypeStruct(r.shape, F32), jax.ShapeDtypeStruct((n_pairs, LANES, LANES), F32)],
        scratch_shapes=[pltpu.VMEM((n_pairs, LANES, LANES), F32)],
        compiler_params=_cp(("arbitrary",)),
        name="wkv_chunked",
    )(r, lw, k, v, a, b)


def _wkv_step_kernel(s_ref, vb_ref, r_ref, lw_ref, k_ref, a_ref, b_ref, so_ref, y_ref, *, tb):
    lane = lax.broadcasted_iota(I32, (HD, LANES), 1)

    def body(n, ytile):
        hs = range(HEADS)
        s = [s_ref[n, h] for h in hs]
        sa = [jnp.sum(s[h] * a_ref[n, h], axis=-1, keepdims=True) for h in hs]
        s_new = [s[h] * jnp.exp(lw_ref[n, h]) + sa[h] * b_ref[n, h] + vb_ref[n, h] * k_ref[n, h] for h in hs]
        y = [jnp.sum(s_new[h] * r_ref[n, h], axis=-1, keepdims=True) for h in hs]
        for h in hs:
            so_ref[n, h] = s_new[h]
            ytile = jnp.where(lane == n * HEADS + h, y[h], ytile)
        return ytile

    y_ref[...] = lax.fori_loop(0, tb, body, jnp.zeros((HD, LANES), F32))


def _wkv_step(state, v, r, lw, k, a, b, tb=8):
    bsz = state.shape[0]
    vb = jnp.broadcast_to(v.reshape(bsz, HEADS, HD, 1), (bsz, HEADS, HD, HD))
    rows = [t.reshape(bsz, HEADS, 1, HD) for t in (r, lw, k, a, b)]
    big = pl.BlockSpec((tb, HEADS, HD, HD), lambda i: (i, 0, 0, 0))
    vec = pl.BlockSpec((tb, HEADS, 1, HD), lambda i: (i, 0, 0, 0))
    s_new, yt = pl.pallas_call(
        functools.partial(_wkv_step_kernel, tb=tb),
        grid=(bsz // tb,),
        in_specs=[big, big] + [vec] * 5,
        out_specs=[big, pl.BlockSpec((None, HD, LANES), lambda i: (i, 0, 0))],
        out_shape=[jax.ShapeDtypeStruct(state.shape, F32), jax.ShapeDtypeStruct((bsz // tb, HD, LANES), F32)],
        compiler_params=_cp(("arbitrary",)),
        name="wkv_step",
    )(state, vb, *rows)
    y = yt.reshape(bsz // tb, HD, tb, HEADS).transpose(0, 2, 3, 1).reshape(bsz, D)
    return s_new, y


def _rwkv_post_kernel(y_ref, r_ref, k_ref, v_ref, g_ref, x_ref, gt_ref, vec_ref, wo_ref, o_ref, *, pairs):
    load = _from_pairs if pairs else (lambda ref: ref[...])
    y, r, k, v, g = (load(ref) for ref in (y_ref, r_ref, k_ref, v_ref, g_ref))
    r_k, lnx_g, lnx_b, ln_g, ln_b = (vec_ref[i:i + 1, :] for i in range(5))
    ones_bd = _head_ones()
    ym = _segsum(y, ones_bd) * (1.0 / HD)
    yc = y - ym
    yv = _segsum(yc * yc, ones_bd) * (1.0 / HD)
    yn = yc * lax.rsqrt(yv + GN_EPS) * lnx_g + lnx_b
    bonus = _segsum(r * k * r_k, ones_bd) * v
    mix = _bdot((yn + bonus) * g, wo_ref[...])
    o_ref[...] = _layer_norm(DN_ALPHA * x_ref[...] + gt_ref[...] * mix, ln_g, ln_b)


def _rwkv_post(y, r, k, v, g, x, gt, vec, w_o, *, pairs, tm):
    bsz, t, _ = x.shape
    row = pl.BlockSpec((None, tm, D), lambda b, i: (b, i, 0))
    big = pl.BlockSpec((None, PAIRS, tm, LANES), lambda b, i: (b, 0, i, 0)) if pairs else row
    mod = pl.BlockSpec((None, 1, D), lambda b, i: (b, 0, 0)) if pairs else row
    return pl.pallas_call(
        functools.partial(_rwkv_post_kernel, pairs=pairs),
        grid=(bsz, t // tm),
        in_specs=[big] * 5 + [row, mod, pl.BlockSpec((5, D), lambda b, i: (0, 0)),
                              pl.BlockSpec((D, D), lambda b, i: (0, 0))],
        out_specs=row,
        out_shape=jax.ShapeDtypeStruct(x.shape, F32),
        compiler_params=_cp(("arbitrary", "arbitrary")),
        name="rwkv_post",
    )(y, r, k, v, g, x, gt, vec, w_o)


def _top2(v):
    io = lax.broadcasted_iota(I32, v.shape, 0)
    m1 = jnp.max(v, axis=0, keepdims=True)
    i1 = jnp.min(jnp.where(v == m1, io, EXPERTS_PER_GROUP), axis=0, keepdims=True)
    v2 = jnp.where(io == i1, -jnp.inf, v)
    m2 = jnp.max(v2, axis=0, keepdims=True)
    i2 = jnp.min(jnp.where(v2 == m2, io, EXPERTS_PER_GROUP), axis=0, keepdims=True)
    return m1 + m2, i1, i2


def _moe_route_kernel(x_ref, sc_ref, sh_ref, rw_ref, rb_ref, h_ref, idx_ref, gate_ref, gmat_ref, cnt_ref,
                      carry_ref, *, tm, rows_out):
    @pl.when((pl.program_id(0) == 0) & (pl.program_id(1) == 0))
    def _():
        carry_ref[...] = jnp.zeros_like(carry_ref)

    h = x_ref[...] * (1.0 + sc_ref[...]) + sh_ref[...]
    h_ref[...] = h
    aff = _sigmoid(_bdot_nt(rw_ref[...], h))
    sel = aff + rb_ref[...]
    best = gi = i1 = i2 = None
    for g in range(N_EXPERT_GROUPS):
        sc, j1, j2 = _top2(sel[g * EXPERTS_PER_GROUP:(g + 1) * EXPERTS_PER_GROUP, :])
        if g == 0:
            best, gi, i1, i2 = sc, jnp.zeros_like(j1), j1, j2
        else:
            upd = sc > best
            best = jnp.where(upd, sc, best)
            gi = jnp.where(upd, g, gi)
            i1 = jnp.where(upd, j1, i1)
            i2 = jnp.where(upd, j2, i2)
    e0 = gi * EXPERTS_PER_GROUP + i1
    e1 = gi * EXPERTS_PER_GROUP + i2
    io = lax.broadcasted_iota(I32, (N_EXPERTS, tm), 0)
    oh0 = io == e0
    oh1 = io == e1
    a0 = jnp.sum(jnp.where(oh0, aff, 0.0), axis=0, keepdims=True)
    a1 = jnp.sum(jnp.where(oh1, aff, 0.0), axis=0, keepdims=True)
    den = a0 + a1
    g0 = a0 / den
    g1 = a1 / den
    oh = jnp.where(oh0 | oh1, 1.0, 0.0)
    tr = lax.broadcasted_iota(I32, (tm, tm), 0)
    tc = lax.broadcasted_iota(I32, (tm, tm), 1)
    before = jnp.dot(oh.astype(BF16), (tr < tc).astype(BF16), preferred_element_type=F32) + carry_ref[...]
    rank0 = jnp.sum(jnp.where(oh0, before, 0.0), axis=0, keepdims=True)
    rank1 = jnp.sum(jnp.where(oh1, before, 0.0), axis=0, keepdims=True)
    carry_ref[...] = carry_ref[...] + jnp.sum(oh, axis=1, keepdims=True)
    zi = jnp.zeros((4, tm), I32)
    idx_ref[...] = jnp.concatenate([e0, e1, rank0.astype(I32), rank1.astype(I32), zi], axis=0)
    gate_ref[...] = jnp.concatenate([g0, g1, jnp.zeros((6, tm), F32)], axis=0)
    gmat_ref[...] = jnp.where(oh0, g0, 0.0) + jnp.where(oh1, g1, 0.0)
    cnt_ref[...] = jnp.broadcast_to(carry_ref[...], cnt_ref.shape)


def _moe_route(x, sc, sh, router_w_t, router_b, *, per_b_mod, rows_out, tm):
    bsz, t, _ = x.shape
    n = bsz * t
    nt = t // tm
    row = pl.BlockSpec((None, tm, D), lambda b, i: (b, i, 0))
    mod = pl.BlockSpec((None, 1, D), lambda b, i: (b, 0, 0)) if per_b_mod else row
    if rows_out:
        h_spec = pl.BlockSpec((tm, D), lambda b, i: (b * nt + i, 0))
        h_shape = jax.ShapeDtypeStruct((n, D), F32)
    else:
        h_spec, h_shape = row, jax.ShapeDtypeStruct(x.shape, F32)
    tok = lambda rows: pl.BlockSpec((rows, tm), lambda b, i: (0, b * nt + i))
    return pl.pallas_call(
        functools.partial(_moe_route_kernel, tm=tm, rows_out=rows_out),
        grid=(bsz, nt),
        in_specs=[row, mod, mod, pl.BlockSpec((N_EXPERTS, D), lambda b, i: (0, 0)),
                  pl.BlockSpec((N_EXPERTS, 1), lambda b, i: (0, 0))],
        out_specs=[h_spec, tok(8), tok(8), tok(N_EXPERTS), pl.BlockSpec((N_EXPERTS, LANES), lambda b, i: (0, 0))],
        out_shape=[h_shape, jax.ShapeDtypeStruct((8, n), I32), jax.ShapeDtypeStruct((8, n), F32),
                   jax.ShapeDtypeStruct((N_EXPERTS, n), F32), jax.ShapeDtypeStruct((N_EXPERTS, LANES), F32)],
        scratch_shapes=[pltpu.VMEM((N_EXPERTS, 1), F32)],
        compiler_params=_cp(("arbitrary", "arbitrary")),
        name="moe_route",
    )(x, sc, sh, router_w_t, router_b)


def _row_copy(src, src_row, dst, dst_row, sem):
    return pltpu.make_async_copy(src.at[pl.ds(src_row, 1), :], dst.at[pl.ds(dst_row, 1), :], sem)


def _wait_rows(src, dst, rows, sem):
    pltpu.make_async_copy(src.at[pl.ds(0, rows), :], dst.at[pl.ds(0, rows), :], sem).wait()


def _moe_dispatch_kernel(d0_ref, d1_ref, h_ref, init_ref, xs_ref, sem, *, tm):
    del init_ref
    base = pl.program_id(0) * tm

    def issue(i, c):
        _row_copy(h_ref, i, xs_ref, d0_ref[base + i], sem.at[0]).start()
        _row_copy(h_ref, i, xs_ref, d1_ref[base + i], sem.at[1]).start()
        return c

    lax.fori_loop(0, tm, issue, 0)
    _wait_rows(h_ref, xs_ref, tm, sem.at[0])
    _wait_rows(h_ref, xs_ref, tm, sem.at[1])


def _moe_dispatch(h_rows, dest0, dest1, n_slots, tm):
    n = h_rows.shape[0]
    init = jnp.zeros((n_slots, D), F32)
    return pl.pallas_call(
        functools.partial(_moe_dispatch_kernel, tm=tm),
        grid_spec=pltpu.PrefetchScalarGridSpec(
            num_scalar_prefetch=2, grid=(n // tm,),
            in_specs=[pl.BlockSpec((tm, D), lambda i, d0, d1: (i, 0)),
                      pl.BlockSpec(memory_space=pl.ANY)],
            out_specs=pl.BlockSpec(memory_space=pl.ANY),
            scratch_shapes=[pltpu.SemaphoreType.DMA((2,))]),
        out_shape=jax.ShapeDtypeStruct((n_slots, D), F32),
        input_output_aliases={3: 0},
        compiler_params=_cp(("arbitrary",)),
        name="moe_dispatch",
    )(dest0, dest1, h_rows, init)


def _moe_expert_kernel(be_ref, used_ref, xs_ref, win_ref, wout_ref, ys_ref, win_bf, wout_bf):
    i = pl.program_id(0)
    fresh = (i == 0) | (be_ref[i] != be_ref[jnp.maximum(i - 1, 0)])

    @pl.when(fresh)
    def _():
        win_bf[...] = win_ref[...].astype(BF16)
        wout_bf[...] = wout_ref[...].astype(BF16)

    @pl.when(i < used_ref[0])
    def _():
        hmid = jnp.dot(xs_ref[...].astype(BF16), win_bf[...], preferred_element_type=F32)
        gt = hmid[:, :D_EXPERT]
        up = hmid[:, D_EXPERT:]
        act = gt * _sigmoid(gt) * up
        ys_ref[...] = jnp.dot(act.astype(BF16), wout_bf[...], preferred_element_type=F32)

    @pl.when(i >= used_ref[0])
    def _():
        ys_ref[...] = jnp.zeros_like(ys_ref)


def _moe_experts_sorted(xs, blk_exp, n_used, w_in, w_out, layer):
    n_slots = xs.shape[0]
    n_blocks = n_slots // MOE_BLK
    return pl.pallas_call(
        _moe_expert_kernel,
        grid_spec=pltpu.PrefetchScalarGridSpec(
            num_scalar_prefetch=2, grid=(n_blocks,),
            in_specs=[pl.BlockSpec((MOE_BLK, D), lambda i, be, u: (i, 0)),
                      pl.BlockSpec((None, None, D, 2 * D_EXPERT), lambda i, be, u: (layer, be[i], 0, 0)),
                      pl.BlockSpec((None, None, D_EXPERT, D), lambda i, be, u: (layer, be[i], 0, 0))],
            out_specs=pl.BlockSpec((MOE_BLK, D), lambda i, be, u: (i, 0)),
            scratch_shapes=[pltpu.VMEM((D, 2 * D_EXPERT), BF16), pltpu.VMEM((D_EXPERT, D), BF16)]),
        out_shape=jax.ShapeDtypeStruct((n_slots, D), F32),
        compiler_params=_cp(("arbitrary",)),
        name="moe_experts_sorted",
    )(blk_exp, n_used, xs, w_in, w_out)


def _moe_combine_kernel(d0_ref, d1_ref, ys_ref, gate_ref, x_ref, gt_ref, vec_ref, o_ref, ya, yb, sem, *, tm):
    step = pl.program_id(0)
    slot = step % 2

    def gather(s, sl):
        def issue(i, c):
            _row_copy(ys_ref, d0_ref[s * tm + i], ya.at[sl], i, sem.at[sl, 0]).start()
            _row_copy(ys_ref, d1_ref[s * tm + i], yb.at[sl], i, sem.at[sl, 1]).start()
            return c
        lax.fori_loop(0, tm, issue, 0)

    @pl.when(step == 0)
    def _():
        gather(0, 0)

    @pl.when(step + 1 < pl.num_programs(0))
    def _():
        gather(step + 1, 1 - slot)

    _wait_rows(ys_ref, ya.at[slot], tm, sem.at[slot, 0])
    _wait_rows(ys_ref, yb.at[slot], tm, sem.at[slot, 1])
    ff = ya[slot] * gate_ref[:, 0:1] + yb[slot] * gate_ref[:, 1:2]
    o_ref[...] = _layer_norm(DN_ALPHA * x_ref[...] + gt_ref[...] * ff, vec_ref[0:1, :], vec_ref[1:2, :])


def _moe_combine(ys, dest0, dest1, gates, x, gt, vec, tm):
    bsz, t, _ = x.shape
    nt = t // tm
    return pl.pallas_call(
        functools.partial(_moe_combine_kernel, tm=tm),
        grid_spec=pltpu.PrefetchScalarGridSpec(
            num_scalar_prefetch=2, grid=(bsz * nt,),
            in_specs=[pl.BlockSpec(memory_space=pl.ANY),
                      pl.BlockSpec((tm, 2), lambda i, d0, d1: (i, 0)),
                      pl.BlockSpec((None, tm, D), lambda i, d0, d1: (i // nt, i % nt, 0)),
                      pl.BlockSpec((None, 1, D), lambda i, d0, d1: (i // nt, 0, 0)),
                      pl.BlockSpec((2, D), lambda i, d0, d1: (0, 0))],
            out_specs=pl.BlockSpec((None, tm, D), lambda i, d0, d1: (i // nt, i % nt, 0)),
            scratch_shapes=[pltpu.VMEM((2, tm, D), F32), pltpu.VMEM((2, tm, D), F32),
                            pltpu.SemaphoreType.DMA((2, 2))]),
        out_shape=jax.ShapeDtypeStruct(x.shape, F32),
        compiler_params=_cp(("arbitrary",)),
        name="moe_combine",
    )(dest0, dest1, ys, gates, x, gt, vec)


def _moe_sorted(x, sc, sh, gt, vec, w, layer, tm=256):
    bsz, t, _ = x.shape
    n = bsz * t
    n_blocks = (n * 2) // MOE_BLK + N_EXPERTS
    h_rows, idx, gate, _, cnt = _moe_route(x, sc, sh, w["router_w_t"], w["router_b"], per_b_mod=True,
                                           rows_out=True, tm=tm)
    counts = cnt[:, 0].astype(I32)
    padded = (counts + MOE_BLK - 1) // MOE_BLK * MOE_BLK
    pad_end = jnp.cumsum(padded)
    pad_start = pad_end - padded
    dest0 = pad_start[idx[0]] + idx[2]
    dest1 = pad_start[idx[1]] + idx[3]
    blk_start = jnp.arange(n_blocks, dtype=I32) * MOE_BLK
    blk_exp = jnp.minimum(jnp.sum((pad_end[None, :] <= blk_start[:, None]).astype(I32), axis=1), N_EXPERTS - 1)
    n_used = (pad_end[-1:] // MOE_BLK).astype(I32)
    xs = _moe_dispatch(h_rows, dest0, dest1, n_blocks * MOE_BLK, tm)
    ys = _moe_experts_sorted(xs, blk_exp, n_used, w["moe_w_in"], w["moe_w_out"], layer)
    return _moe_combine(ys, dest0, dest1, gate[:2].T, x, gt, vec, tm)


def _moe_dense_kernel(h_ref, gm_ref, win_ref, wout_ref, x_ref, gt_ref, vec_ref, o_ref, acc_ref):
    e = pl.program_id(0)

    @pl.when(e == 0)
    def _():
        acc_ref[...] = jnp.zeros_like(acc_ref)

    lane = lax.broadcasted_iota(I32, gm_ref.shape, 1)
    gcol = jnp.sum(jnp.where(lane == e, gm_ref[...], 0.0), axis=1, keepdims=True)
    hmid = _bdot(h_ref[...], win_ref[...])
    gt = hmid[:, :D_EXPERT]
    up = hmid[:, D_EXPERT:]
    y = _bdot(gt * _sigmoid(gt) * up, wout_ref[...])
    acc_ref[...] = acc_ref[...] + jnp.where(gcol != 0.0, gcol * y, 0.0)

    @pl.when(e == N_EXPERTS - 1)
    def _():
        o_ref[...] = _layer_norm(DN_ALPHA * x_ref[...] + gt_ref[...] * acc_ref[...], vec_ref[0:1, :],
                                 vec_ref[1:2, :])


def _moe_dense(x, sc, sh, gt, vec, w, layer):
    _, m, _ = x.shape
    h, _, _, gmat, _ = _moe_route(x, sc, sh, w["router_w_t"], w["router_b"], per_b_mod=False, rows_out=False,
                                  tm=m)
    full = pl.BlockSpec((m, D), lambda e: (0, 0))
    out = pl.pallas_call(
        _moe_dense_kernel,
        grid=(N_EXPERTS,),
        in_specs=[full, pl.BlockSpec((m, N_EXPERTS), lambda e: (0, 0)),
                  pl.BlockSpec((None, None, D, 2 * D_EXPERT), lambda e: (layer, e, 0, 0)),
                  pl.BlockSpec((None, None, D_EXPERT, D), lambda e: (layer, e, 0, 0)),
                  full, full, pl.BlockSpec((2, D), lambda e: (0, 0))],
        out_specs=full,
        out_shape=jax.ShapeDtypeStruct((m, D), F32),
        scratch_shapes=[pltpu.VMEM((m, D), F32)],
        compiler_params=_cp(("arbitrary",)),
        name="moe_dense",
    )(h[0], gmat.T, w["moe_w_in"], w["moe_w_out"], x[0], gt[0], vec)
    return out[None]


def _rotary(x, cos_t, sa_t, sb_t):
    reps = x.shape[1] // LANES
    tile = lambda t: t if reps == 1 else jnp.concatenate([t] * reps, axis=1)
    n = x.shape[1]
    half = ROT_DIM // 2
    return x * tile(cos_t) + pltpu.roll(x, n - half, axis=1) * tile(sa_t) + pltpu.roll(x, half, axis=1) * tile(sb_t)


def _class_rows(c, rows, dil):
    return pl.ds(c, rows, stride=dil) if dil > 1 else pl.ds(0, rows)


def _attn_pre_kernel(x_ref, sc_ref, sh_ref, ksc_ref, ksh_ref, rot_ref, wq_ref, wkv_ref, *refs, dils, tm):
    x = x_ref[...]
    cos_t, sa_t, sb_t = rot_ref[0], rot_ref[1], rot_ref[2]
    q = _rotary(_bdot(x * (1.0 + sc_ref[...]) + sh_ref[...], wq_ref[...]), cos_t, sa_t, sb_t)
    kv = _bdot(x * (1.0 + ksc_ref[...]) + ksh_ref[...], wkv_ref[...])
    k = _rotary(kv[:, :KVW], cos_t, sa_t, sb_t)
    v = kv[:, KVW:]
    k_ref, v_ref = refs[0], refs[1]
    k_ref[...] = k
    v_ref[...] = v
    if dils is None:
        refs[2][...] = q
        return
    ng = len(dils)
    q_refs, kb_refs, vb_refs = refs[2:2 + ng], refs[2 + ng:2 + 2 * ng], refs[2 + 2 * ng:2 + 3 * ng]
    qs, kvs = refs[2 + 3 * ng:]
    for j in range(q.shape[1] // LANES):
        qs[j] = q[:, j * LANES:(j + 1) * LANES]
    for j in range(kv.shape[1] // LANES):
        kvs[j] = (k if j < KVW // LANES else v)[:, (j % (KVW // LANES)) * LANES:(j % (KVW // LANES) + 1) * LANES]

    def gather(ref, tiles, sl):
        return jnp.concatenate([ref[j, sl, :] for j in tiles], axis=1).astype(BF16)

    kt = KVW // LANES
    for g, dil in enumerate(dils):
        rows = tm // dil
        for c in range(dil):
            sl = _class_rows(c, rows, dil)
            q_refs[g][c] = gather(qs, range(g * PAIRS, (g + 1) * PAIRS), sl)
            kb_refs[g][c] = gather(kvs, range(kt), sl)
            vb_refs[g][c] = gather(kvs, range(kt, 2 * kt), sl)


def _attn_pre(x, sc, sh, ksc, ksh, rot, w_q, w_kv, *, per_b_mod, tm, dils=None):
    bsz, t, _ = x.shape
    row = lambda wd: pl.BlockSpec((None, tm, wd), lambda b, i: (b, i, 0))
    mod = pl.BlockSpec((None, 1, D), lambda b, i: (b, 0, 0)) if per_b_mod else row(D)
    nq = w_q.shape[1]
    out_specs = [row(KVW), row(KVW)]
    out_shape = [jax.ShapeDtypeStruct((bsz, t, KVW), F32)] * 2
    scratch = []
    if dils is None:
        out_specs.append(row(nq))
        out_shape.append(jax.ShapeDtypeStruct((bsz, t, nq), F32))
    else:
        for wd in (D, KVW, KVW):
            for dil in dils:
                out_specs.append(pl.BlockSpec((None, dil, tm // dil, wd), lambda b, i: (b, 0, i, 0)))
                out_shape.append(jax.ShapeDtypeStruct((bsz, dil, t // dil, wd), BF16))
        scratch = [pltpu.VMEM((nq // LANES, tm, LANES), F32), pltpu.VMEM((2 * KVW // LANES, tm, LANES), F32)]
    outs = pl.pallas_call(
        functools.partial(_attn_pre_kernel, dils=dils, tm=tm),
        grid=(bsz, t // tm),
        in_specs=[row(D), mod, mod, mod, mod, pl.BlockSpec((3, tm, LANES), lambda b, i: (0, i, 0)),
                  pl.BlockSpec((D, nq), lambda b, i: (0, 0)), pl.BlockSpec((D, 2 * KVW), lambda b, i: (0, 0))],
        out_specs=out_specs,
        out_shape=out_shape,
        scratch_shapes=scratch,
        compiler_params=_cp(("arbitrary", "arbitrary")),
        name="attn_pre",
    )(x, sc, sh, ksc, ksh, rot, w_q, w_kv)
    if dils is None:
        return outs
    ng = len(dils)
    return outs[0], outs[1], outs[2:2 + ng], outs[2 + ng:2 + 2 * ng], outs[2 + 2 * ng:]


def _attn_band_kernel(q_ref, ka_ref, kb_ref, va_ref, vb_ref, o_ref, lse_ref):
    i = pl.program_id(2)
    bq = ATT_WIN
    kcat = jnp.concatenate([ka_ref[...], kb_ref[...]], axis=0)
    vcat = jnp.concatenate([va_ref[...], vb_ref[...]], axis=0)
    qs = jnp.concatenate([q_ref[:, rep * KVW:(rep + 1) * KVW] for rep in range(REP)], axis=0)
    qrow = lax.broadcasted_iota(I32, (REP * bq, 2 * bq), 0) % bq
    kcol = lax.broadcasted_iota(I32, (REP * bq, 2 * bq), 1)
    valid = (kcol >= qrow) & (kcol <= qrow + bq) & ((i > 0) | (kcol >= bq))
    lane = lax.broadcasted_iota(I32, (REP * bq, KVW), 1) // HD
    lse_lane = lax.broadcasted_iota(I32, (bq, LANES), 1)
    lse = jnp.zeros((bq, LANES), F32)
    o = jnp.zeros((REP * bq, KVW), F32)
    linv = jnp.ones((REP * bq, KVW), F32)
    scale = HD ** -0.5
    for h in range(KV_HEADS):
        qm = jnp.where(lane == h, qs, jnp.zeros_like(qs))
        s = lax.dot_general(qm, kcat, (((1,), (1,)), ((), ())), preferred_element_type=F32) * scale
        s = jnp.where(valid, s, -jnp.inf)
        m = jnp.max(s, axis=-1, keepdims=True)
        p = jnp.exp(s - m)
        l = jnp.sum(p, axis=-1, keepdims=True)
        o = jnp.where(lane == h, jnp.dot(p.astype(BF16), vcat, preferred_element_type=F32), o)
        linv = jnp.where(lane == h, 1.0 / l, linv)
        lse_h = m + jnp.log(l)
        for rep in range(REP):
            lse = jnp.where(lse_lane == rep * KV_HEADS + h, lse_h[rep * bq:(rep + 1) * bq], lse)
    o = o * linv
    for rep in range(REP):
        o_ref[:, rep * KVW:(rep + 1) * KVW] = o[rep * bq:(rep + 1) * bq]
    lse_ref[...] = lse


def _attn_band(q, kb, vb, dil):
    bsz, _, tc, _ = q.shape
    nb = tc // ATT_WIN
    blk = lambda wd: pl.BlockSpec((None, None, ATT_WIN, wd), lambda b, c, i: (b, c, i, 0))
    prev = pl.BlockSpec((None, None, ATT_WIN, KVW), lambda b, c, i: (b, c, jnp.maximum(i - 1, 0), 0))
    return pl.pallas_call(
        _attn_band_kernel,
        grid=(bsz, dil, nb),
        in_specs=[blk(D), prev, blk(KVW), prev, blk(KVW)],
        out_specs=[blk(D), blk(LANES)],
        out_shape=[jax.ShapeDtypeStruct((bsz, dil, tc, D), F32), jax.ShapeDtypeStruct((bsz, dil, tc, LANES), F32)],
        compiler_params=_cp(("arbitrary", "arbitrary", "arbitrary")),
        name=f"attn_band_d{dil}",
    )(q, kb, kb, vb, vb)


def _attn_step_kernel(q_ref, kn_ref, vn_ref, *refs):
    k_refs, v_refs = refs[:6], refs[6:12]
    o_ref, lse_ref = refs[12], refs[13]
    nq = len(DILATED_GROUPS) * REP
    nk = (len(DILATED_GROUPS) + 1) * ATT_WIN
    scale = HD ** -0.5
    row = lax.broadcasted_iota(I32, (nq, nk), 0)
    col = lax.broadcasted_iota(I32, (nq, nk), 1)
    valid = (col // ATT_WIN == row // REP) | (col == nk - ATT_WIN)
    first = lax.broadcasted_iota(I32, (ATT_WIN, HD), 0) == 0
    lse_lane = lax.broadcasted_iota(I32, (nq, LANES), 1)
    lse = jnp.zeros((nq, LANES), F32)

    def stacked(c_refs, new_ref, h):
        d1, d4, d16 = c_refs[0], c_refs[1:5], c_refs[5]
        per_grp = HD // KV_HEADS
        pieces = [d1[:, KV_HEADS * t + h, :] for t in range(per_grp)]
        pieces += [r[:, h, :] for r in d4]
        pieces.append(d16[:, h, :])
        pieces.append(jnp.where(first, new_ref[:, h * HD:(h + 1) * HD], 0.0))
        return jnp.concatenate(pieces, axis=0).astype(BF16)

    for h in range(KV_HEADS):
        hs = slice(h * HD, (h + 1) * HD)
        s = _bdot_nt(q_ref[:, hs], stacked(k_refs, kn_ref, h)) * scale
        s = jnp.where(valid, s, -1e30)
        m = jnp.max(s, axis=-1, keepdims=True)
        p = jnp.exp(s - m)
        l = jnp.sum(p, axis=-1, keepdims=True)
        o_ref[:, hs] = _bdot(p, stacked(v_refs, vn_ref, h)) / l
        lse = jnp.where(lse_lane == h, m + jnp.log(l), lse)
    lse_ref[...] = lse


def _attn_step(q, k_new, v_new, cache_k, cache_v):
    bsz = q.shape[0]
    wbuf = cache_k.shape[1]
    ng = len(DILATED_GROUPS)
    max_dil = DILATED_GROUPS[-1][1]
    assert [d for _, d in DILATED_GROUPS] == [1, 4, 16] and wbuf == ATT_WIN * max_dil
    grp_rows = max_dil * KV_HEADS
    n_grp = wbuf // max_dil
    specs = [pl.BlockSpec((None, ATT_WIN // max_dil, grp_rows, HD), lambda b: (b, max_dil - 1, 0, 0))]
    specs += [pl.BlockSpec((None, ATT_WIN // 4, 8, HD), lambda b, j=j: (b, 3, 2 * j, 0)) for j in range(4)]
    specs += [pl.BlockSpec((None, n_grp, 8, HD), lambda b: (b, 0, 0, 0))]
    views = [c.reshape(bsz, n_grp, grp_rows, HD) for c in (cache_k, cache_v)]
    nq = ng * REP
    o, lse = pl.pallas_call(
        _attn_step_kernel,
        grid=(bsz,),
        in_specs=[pl.BlockSpec((None, nq, KVW), lambda b: (b, 0, 0)),
                  pl.BlockSpec((None, 1, KVW), lambda b: (b, 0, 0)),
                  pl.BlockSpec((None, 1, KVW), lambda b: (b, 0, 0))] + specs + specs,
        out_specs=[pl.BlockSpec((None, nq, KVW), lambda b: (b, 0, 0)),
                   pl.BlockSpec((None, nq, LANES), lambda b: (b, 0, 0))],
        out_shape=[jax.ShapeDtypeStruct((bsz, nq, KVW), F32), jax.ShapeDtypeStruct((bsz, nq, LANES), F32)],
        compiler_params=_cp(("arbitrary",)),
        name="attn_step",
    )(q.reshape(bsz, nq, KVW), k_new.reshape(bsz, 1, KVW), v_new.reshape(bsz, 1, KVW),
      *([views[0]] * 6), *([views[1]] * 6))
    o = o.reshape(bsz, ng, D).transpose(1, 0, 2)
    lse = lse[:, :, :KV_HEADS].reshape(bsz, ng, REP * KV_HEADS).transpose(1, 0, 2)
    return o, jnp.pad(lse, ((0, 0), (0, 0), (0, LANES - REP * KV_HEADS)))


def _attn_post_kernel(o0_ref, o1_ref, o2_ref, l0_ref, l1_ref, l2_ref, x_ref, gt_ref, vec_ref, wo_ref, out_ref,
                      *scratch, dils, tm):
    o_refs = [o0_ref, o1_ref, o2_ref]
    l_refs = [l0_ref, l1_ref, l2_ref]
    if dils is not None:
        for g, dil in enumerate(dils):
            if dil == 1:
                o_refs[g], l_refs[g] = o_refs[g].at[0], l_refs[g].at[0]
                continue
            so, sl = scratch[2 * g], scratch[2 * g + 1]
            for c in range(dil):
                rows = _class_rows(c, tm // dil, dil)
                oc = o_refs[g][c]
                for j in range(PAIRS):
                    so[j, rows, :] = oc[:, j * LANES:(j + 1) * LANES]
                sl[rows, :] = l_refs[g][c]
            o_refs[g], l_refs[g] = so, sl
    o0, o1, o2 = (_from_pairs(r) if len(r.shape) == 3 else r[...] for r in o_refs)
    l0, l1, l2 = (r[...] for r in l_refs)
    m = jnp.maximum(jnp.maximum(l0, l1), l2)
    e0, e1, e2 = jnp.exp(l0 - m), jnp.exp(l1 - m), jnp.exp(l2 - m)
    den = e0 + e1 + e2
    r = lax.broadcasted_iota(I32, (LANES, D), 0)
    c = lax.broadcasted_iota(I32, (LANES, D), 1) // HD
    spread = (r == c).astype(BF16)
    mix = (_split_dot(e0 / den, spread) * o0 + _split_dot(e1 / den, spread) * o1
           + _split_dot(e2 / den, spread) * o2)
    out = _bdot(mix, wo_ref[...])
    out_ref[...] = _layer_norm(DN_ALPHA * x_ref[...] + gt_ref[...] * out, vec_ref[0:1, :], vec_ref[1:2, :])


def _attn_post(os, lses, x, gt, vec, w_o, *, per_b_mod, tm, dils=None):
    bsz, t, _ = x.shape
    row = lambda wd: pl.BlockSpec((None, tm, wd), lambda b, i: (b, i, 0))
    mod = pl.BlockSpec((None, 1, D), lambda b, i: (b, 0, 0)) if per_b_mod else row(D)
    if dils is None:
        o_specs, l_specs, scratch = [row(D)] * 3, [row(LANES)] * 3, []
    else:
        cls = lambda dil, wd: pl.BlockSpec((None, dil, tm // dil, wd), lambda b, i: (b, 0, i, 0))
        o_specs = [cls(dil, D) for dil in dils]
        l_specs = [cls(dil, LANES) for dil in dils]
        scratch = [pltpu.VMEM(shape, F32) for _ in dils for shape in ((PAIRS, tm, LANES), (tm, LANES))]
    return pl.pallas_call(
        functools.partial(_attn_post_kernel, dils=dils, tm=tm),
        grid=(bsz, t // tm),
        in_specs=o_specs + l_specs + [row(D), mod, pl.BlockSpec((2, D), lambda b, i: (0, 0)),
                                      pl.BlockSpec((D, D), lambda b, i: (0, 0))],
        out_specs=row(D),
        out_shape=jax.ShapeDtypeStruct(x.shape, F32),
        scratch_shapes=scratch,
        compiler_params=_cp(("arbitrary", "arbitrary")),
        name="attn_post",
    )(*os, *lses, x, gt, vec, w_o)


def _rot_tables(pos):
    half = ROT_DIM // 2
    inv = ROPE_THETA ** (-jnp.arange(half, dtype=F32) * 2.0 / ROT_DIM)
    ang = pos.astype(F32)[:, None] * inv[None, :]
    cos, sin = jnp.cos(ang), jnp.sin(ang)
    t = pos.shape[0]
    one = jnp.ones((t, HD - ROT_DIM), F32)
    zero = jnp.zeros((t, HD - ROT_DIM), F32)
    zh = jnp.zeros((t, half), F32)
    cos_t = jnp.concatenate([cos, cos, one], axis=1)
    sa_t = jnp.concatenate([-sin, zh, zero], axis=1)
    sb_t = jnp.concatenate([zh, sin, zero], axis=1)
    return jnp.stack([jnp.tile(x, (1, 2)) for x in (cos_t, sa_t, sb_t)])


def _prep_weights(p):
    pad_c = lambda m, n: jnp.pad(m, ((0, 0), (0, n - m.shape[1])))
    pad_r = lambda m, n: jnp.pad(m, ((0, n - m.shape[0]), (0, 0)))
    w = {}
    w["mu"] = p["rwkv_mu"][0]
    w["w_rkv"] = p["rwkv_w_rkv"][0].astype(BF16)
    w["w1"] = pad_c(p["rwkv_w1"][0], LANES).astype(BF16)
    w["w2"] = pad_r(p["rwkv_w2"][0], LANES).astype(BF16)
    w["a1"] = pad_c(p["rwkv_a1"][0], LANES).astype(BF16)
    w["a2"] = pad_r(p["rwkv_a2"][0], LANES).astype(BF16)
    w["g1"] = pad_c(p["rwkv_g1"][0], 2 * LANES).astype(BF16)
    w["g2"] = pad_r(p["rwkv_g2"][0], 2 * LANES).astype(BF16)
    w["vec"] = jnp.stack([p["rwkv_w0"][0], p["rwkv_a0"][0], p["rwkv_k_k"][0], p["rwkv_k_a"][0]])
    w["post_vec"] = jnp.stack([p["rwkv_r_k"][0].reshape(D), p["rwkv_lnx_g"][0], p["rwkv_lnx_b"][0],
                               p["ln_g"][0, 0], p["ln_b"][0, 0]])
    w["rwkv_w_o"] = p["rwkv_w_o"][0].astype(BF16)
    wq = p["w_q"][0].reshape(D, 3, KV_HEADS, REP, HD).transpose(0, 1, 3, 2, 4).reshape(D, 3 * D)
    w["w_q"] = wq.astype(BF16)
    w["w_kv"] = p["w_kv"].astype(BF16)
    wo = p["w_o_attn"][0].reshape(KV_HEADS, REP, HD, D).transpose(1, 0, 2, 3).reshape(D, D)
    w["w_o_attn"] = wo.astype(BF16)
    w["router_w_t"] = p["router_w"].T.astype(BF16)
    w["router_b"] = p["router_b"].reshape(N_EXPERTS, 1)
    w["moe_w_in"] = p["moe_w_in"]
    w["moe_w_out"] = p["moe_w_out"]
    w["ln"] = [[jnp.stack([p["ln_g"][l, i], p["ln_b"][l, i]]) for i in range(2)] for l in range(DEPTH)]
    return w


def _modulations(c_prompt, c_sample, p):
    nb = c_prompt.shape[0]
    c = jnp.concatenate([c_prompt, c_sample], axis=0)
    pad = (-c.shape[0]) % 8
    c = jnp.pad(c, ((0, pad), (0, 0)))
    m3 = _ada_linear(c, p["ada_w"].reshape(2 * DEPTH, D, 3 * D), p["ada_b"].reshape(2 * DEPTH, 3 * D))
    m2 = _ada_linear(c, p["kv_ada_w"][None], p["kv_ada_b"][None])[0]
    n_all = nb + c_sample.shape[0]

    def split(m, parts, lo, hi):
        return [m[lo:hi, i * D:(i + 1) * D] for i in range(parts)]

    out = {}
    for name, lo, hi in (("prompt", 0, nb), ("sample", nb, n_all)):
        out[name] = {"ada": [[split(m3[2 * l + i], 3, lo, hi) for i in range(2)] for l in range(DEPTH)],
                     "kv": split(m2, 2, lo, hi)}
    return out


def _trunk_prompt(x, mods, w):
    bsz, t, _ = x.shape
    per_b = lambda m: m[:, None, :]
    sh, sc, gt = (per_b(m) for m in mods["ada"][0][0])
    zeros = jnp.zeros((bsz, 1, D), F32)
    r, lw, k, v, a, b, g, last = _rwkv_pre(x, sc, sh, zeros, w, seq_mode=True, tm=256)
    y, zf = _wkv_chunked(r, lw, k, v, a, b)
    x = _rwkv_post(y, r, k, v, g, x, gt, w["post_vec"], w["rwkv_w_o"], pairs=True, tm=256)
    zf = zf.reshape(bsz, PAIRS, 2, HD, 2, HD)
    wkv = jnp.stack([zf[:, :, 0, :, 0, :], zf[:, :, 1, :, 1, :]], axis=2)
    wkv = wkv.reshape(bsz, HEADS, HD, HD).transpose(0, 1, 3, 2)
    sh2, sc2, gt2 = (per_b(m) for m in mods["ada"][0][1])
    x = _moe_sorted(x, sc2, sh2, gt2, w["ln"][0][1], w, 0)

    sh, sc, gt = (per_b(m) for m in mods["ada"][1][0])
    ksh, ksc = (per_b(m) for m in mods["kv"])
    rot = _rot_tables(jnp.arange(t, dtype=I32))
    dils = tuple(dil for _, dil in DILATED_GROUPS)
    k_new, v_new, qs, kbs, vbs = _attn_pre(x, sc, sh, ksc, ksh, rot, w["w_q"], w["w_kv"], per_b_mod=True,
                                           tm=256, dils=dils)
    os, lses = zip(*[_attn_band(qs[g], kbs[g], vbs[g], dil) for g, dil in enumerate(dils)])
    x = _attn_post(os, lses, x, gt, w["ln"][1][0], w["w_o_attn"], per_b_mod=True, tm=256, dils=dils)
    sh2, sc2, gt2 = (per_b(m) for m in mods["ada"][1][1])
    x = _moe_sorted(x, sc2, sh2, gt2, w["ln"][1][1], w, 1)
    return x, wkv[None], last.reshape(1, bsz, D), k_new, v_new


def _trunk_sample(x, mods, state_wkv, state_shift, cache_k, cache_v, w):
    bsz = x.shape[0]
    xs = x.reshape(1, bsz, D)
    row = lambda m: m[None]
    sh, sc, gt = (row(m) for m in mods["ada"][0][0])
    r, lw, k, v, a, b, g, hm = _rwkv_pre(xs, sc, sh, state_shift[0][None], w, seq_mode=False, tm=bsz)
    s_new, y = _wkv_step(state_wkv[0], v[0], r[0], lw[0], k[0], a[0], b[0])
    xs = _rwkv_post(y[None], r, k, v, g, xs, gt, w["post_vec"], w["rwkv_w_o"], pairs=False, tm=bsz)
    sh2, sc2, gt2 = (row(m) for m in mods["ada"][0][1])
    xs = _moe_dense(xs, sc2, sh2, gt2, w["ln"][0][1], w, 0)

    sh, sc, gt = (row(m) for m in mods["ada"][1][0])
    ksh, ksc = (row(m) for m in mods["kv"])
    rot = _rot_tables(jnp.full((bsz,), PAST_LEN, I32))
    k_new, v_new, q = _attn_pre(xs, sc, sh, ksc, ksh, rot, w["w_q"], w["w_kv"], per_b_mod=False, tm=bsz)
    o, lse = _attn_step(q[0], k_new[0], v_new[0], cache_k, cache_v)
    xs = _attn_post([o[i][None] for i in range(3)], [lse[i][None] for i in range(3)], xs, gt, w["ln"][1][0],
                    w["w_o_attn"], per_b_mod=False, tm=bsz)
    sh2, sc2, gt2 = (row(m) for m in mods["ada"][1][1])
    xs = _moe_dense(xs, sc2, sh2, gt2, w["ln"][1][1], w, 1)
    return (xs.reshape(bsz, 1, D), s_new[None], hm, k_new.reshape(bsz, 1, KV_HEADS, HD),
            v_new.reshape(bsz, 1, KV_HEADS, HD))


def kernel(x_prompt, x_sample, state_wkv, state_shift, cache_k, cache_v, c_prompt, c_sample, ada_w, ada_b, ln_g, ln_b, rwkv_mu, rwkv_w_rkv, rwkv_w0, rwkv_w1, rwkv_w2, rwkv_a0, rwkv_a1, rwkv_a2, rwkv_g1, rwkv_g2, rwkv_k_k, rwkv_k_a, rwkv_r_k, rwkv_lnx_g, rwkv_lnx_b, rwkv_w_o, w_q, w_kv, kv_ada_w, kv_ada_b, w_o_attn, router_w, router_b, moe_w_in, moe_w_out):
    p = {"ada_w": ada_w, "ada_b": ada_b, "ln_g": ln_g, "ln_b": ln_b, "rwkv_mu": rwkv_mu, "rwkv_w_rkv": rwkv_w_rkv,
         "rwkv_w0": rwkv_w0, "rwkv_w1": rwkv_w1, "rwkv_w2": rwkv_w2, "rwkv_a0": rwkv_a0, "rwkv_a1": rwkv_a1,
         "rwkv_a2": rwkv_a2, "rwkv_g1": rwkv_g1, "rwkv_g2": rwkv_g2, "rwkv_k_k": rwkv_k_k, "rwkv_k_a": rwkv_k_a,
         "rwkv_r_k": rwkv_r_k, "rwkv_lnx_g": rwkv_lnx_g, "rwkv_lnx_b": rwkv_lnx_b, "rwkv_w_o": rwkv_w_o,
         "w_q": w_q, "w_kv": w_kv, "kv_ada_w": kv_ada_w, "kv_ada_b": kv_ada_b, "w_o_attn": w_o_attn,
         "router_w": router_w, "router_b": router_b, "moe_w_in": moe_w_in, "moe_w_out": moe_w_out}
    w = _prep_weights(p)
    mods = _modulations(c_prompt, c_sample, p)
    bp, tp, _ = x_prompt.shape
    y_p, wkv_p, shift_p, k_p, v_p = _trunk_prompt(x_prompt, mods["prompt"], w)
    keep = min(PAST_LEN, tp)
    k_p = k_p[:, tp - keep:].reshape(bp, keep, KV_HEADS, HD)
    v_p = v_p[:, tp - keep:].reshape(bp, keep, KV_HEADS, HD)
    y_s, wkv_s, shift_s, k_s, v_s = _trunk_sample(x_sample, mods["sample"], state_wkv, state_shift, cache_k,
                                                  cache_v, w)
    return (y_p, y_s, wkv_p, shift_p, k_p, v_p, wkv_s, shift_s, k_s, v_s)
```

```python
import functools

import jax
import jax.numpy as jnp
from jax import lax
from jax.experimental import pallas as pl
from jax.experimental.pallas import tpu as pltpu

F32 = jnp.float32
BF16 = jnp.bfloat16
I32 = jnp.int32

D = 1024
HEADS = 16
HD = 64
LANES = 128
PAIRS = D // LANES
CHUNK = 64
WKV_GROUP = 8
N_EXPERTS = 32
EXPERTS_PER_GROUP = 8
N_EXPERT_GROUPS = 4
D_EXPERT = 512
MOE_BLK = 256
DMA_UNROLL = 8
KV_HEADS = 4
REP = 4
KVW = KV_HEADS * HD
DILATED_GROUPS = ((128, 1), (512, 4), (2048, 16))
ATT_WIN = 128
PAST_LEN = 2048
ROT_DIM = 16
ROPE_THETA = 500000.0
DEPTH = 2
DN_ALPHA = (2 * DEPTH) ** 0.25
LN_EPS = 1e-5
GN_EPS = 64e-5
VMEM_LIMIT = 56 * 1024 * 1024


def _cp(sem):
    return pltpu.CompilerParams(dimension_semantics=sem, vmem_limit_bytes=VMEM_LIMIT)


def _bdot(a, b):
    return jnp.dot(a.astype(BF16), b.astype(BF16), preferred_element_type=F32)


def _bdot_nt(a, b):
    return lax.dot_general(a.astype(BF16), b.astype(BF16), (((1,), (1,)), ((), ())),
                           preferred_element_type=F32)


def _bdot_tn(a, b):
    return lax.dot_general(a.astype(BF16), b.astype(BF16), (((0,), (0,)), ((), ())),
                           preferred_element_type=F32)


def _split_dot(x, m):
    hi = x.astype(BF16)
    lo = (x - hi.astype(F32)).astype(BF16)
    return (jnp.dot(hi, m, preferred_element_type=F32) + jnp.dot(lo, m, preferred_element_type=F32))


def _head_ones():
    r = lax.broadcasted_iota(I32, (LANES, LANES), 0) // HD
    c = lax.broadcasted_iota(I32, (LANES, LANES), 1) // HD
    return (r == c).astype(BF16)


def _segsum(x, ones_bd):
    cols = [_split_dot(x[:, c * LANES:(c + 1) * LANES], ones_bd) for c in range(x.shape[1] // LANES)]
    return cols[0] if len(cols) == 1 else jnp.concatenate(cols, axis=1)


def _layer_norm(x, g, b):
    mu = jnp.mean(x, axis=-1, keepdims=True)
    xc = x - mu
    var = jnp.mean(xc * xc, axis=-1, keepdims=True)
    return xc * lax.rsqrt(var + LN_EPS) * g + b


def _sigmoid(x):
    return 1.0 / (1.0 + jnp.exp(-x))


def _to_pairs(ref, val):
    for p in range(PAIRS):
        ref[p] = val[:, p * LANES:(p + 1) * LANES]


def _from_pairs(ref):
    return jnp.concatenate([ref[p] for p in range(PAIRS)], axis=1)


def _ada_kernel(c_ref, w_ref, b_ref, o_ref):
    c = c_ref[...]
    o_ref[...] = _bdot(c * _sigmoid(c), w_ref[...]) + b_ref[...]


def _ada_linear(c, w, b, tn=512):
    s, _, n = w.shape
    m = c.shape[0]
    return pl.pallas_call(
        _ada_kernel,
        grid=(s, n // tn),
        in_specs=[pl.BlockSpec((m, D), lambda i, j: (0, 0)),
                  pl.BlockSpec((None, D, tn), lambda i, j: (i, 0, j)),
                  pl.BlockSpec((None, 1, tn), lambda i, j: (i, 0, j))],
        out_specs=pl.BlockSpec((None, m, tn), lambda i, j: (i, 0, j)),
        out_shape=jax.ShapeDtypeStruct((s, m, n), F32),
        compiler_params=_cp(("arbitrary", "arbitrary")),
        name="ada_linear",
    )(c, w, b.reshape(s, 1, n))


def _rwkv_pre_kernel(x_ref, sc_ref, sh_ref, prev_ref, mu_ref, wrkv_ref, w1_ref, w2_ref, a1_ref, a2_ref,
                     g1_ref, g2_ref, vec_ref,
                     r_ref, lw_ref, k_ref, v_ref, a_ref, b_ref, g_ref, hm_ref, carry_ref, *, seq_mode, tm):
    x = x_ref[...]
    hm = x * (1.0 + sc_ref[...]) + sh_ref[...]
    if seq_mode:
        @pl.when(pl.program_id(1) == 0)
        def _():
            carry_ref[...] = prev_ref[...]
        row = lax.broadcasted_iota(I32, hm.shape, 0)
        hprev = jnp.where(row == 0, carry_ref[...], pltpu.roll(hm, 1, axis=0))
        carry_ref[...] = hm[tm - 1:tm, :]
        hm_ref[...] = hm[tm - 1:tm, :]
    else:
        hprev = prev_ref[...]
        hm_ref[...] = hm
    xx = hprev - hm

    def mix(i):
        return hm + xx * mu_ref[i:i + 1, :]

    w0, a0, k_k, k_a = (vec_ref[i:i + 1, :] for i in range(4))
    r = _bdot(mix(0), wrkv_ref[0])
    k = _bdot(mix(2), wrkv_ref[1])
    v = _bdot(mix(3), wrkv_ref[2])
    wl = w0 + _bdot(jnp.tanh(_bdot(mix(1), w1_ref[...])), w2_ref[...])
    z = -wl
    softplus = jnp.maximum(z, 0.0) + jnp.log1p(jnp.exp(-jnp.abs(z)))
    lw = -jnp.exp(-softplus - 0.5)
    a_lr = _sigmoid(a0 + _bdot(_bdot(mix(4), a1_ref[...]), a2_ref[...]))
    g = _bdot(_sigmoid(_bdot(mix(5), g1_ref[...])), g2_ref[...])
    kk = k * k_k
    kk = kk / jnp.maximum(jnp.sqrt(_segsum(kk * kk, _head_ones())), 1e-12)
    kmod = k * (1.0 + (a_lr - 1.0) * k_a)
    outs = ((r_ref, r), (lw_ref, lw), (k_ref, kmod), (v_ref, v), (a_ref, -kk), (b_ref, kk * a_lr), (g_ref, g))
    for ref, val in outs:
        if seq_mode:
            _to_pairs(ref, val)
        else:
            ref[...] = val


def _rwkv_pre(x, sc, sh, prev, w, *, seq_mode, tm):
    bsz, t, _ = x.shape
    grid = (bsz, t // tm)
    row = pl.BlockSpec((None, tm, D), lambda b, i: (b, i, 0))
    per_b = pl.BlockSpec((None, 1, D), lambda b, i: (b, 0, 0))
    mod = per_b if seq_mode else row

    def const(shape):
        return pl.BlockSpec(shape, lambda b, i: (0,) * len(shape))

    if seq_mode:
        out_big = pl.BlockSpec((None, PAIRS, tm, LANES), lambda b, i: (b, 0, i, 0))
        big_shape = jax.ShapeDtypeStruct((bsz, PAIRS, t, LANES), F32)
        hm_spec, hm_shape = per_b, jax.ShapeDtypeStruct((bsz, 1, D), F32)
    else:
        out_big, big_shape = row, jax.ShapeDtypeStruct((bsz, t, D), F32)
        hm_spec, hm_shape = row, jax.ShapeDtypeStruct((bsz, t, D), F32)
    return pl.pallas_call(
        functools.partial(_rwkv_pre_kernel, seq_mode=seq_mode, tm=tm),
        grid=grid,
        in_specs=[row, mod, mod, mod, const((6, D)), const((3, D, D)), const((D, LANES)), const((LANES, D)),
                  const((D, LANES)), const((LANES, D)), const((D, 2 * LANES)), const((2 * LANES, D)),
                  const((4, D))],
        out_specs=[out_big] * 7 + [hm_spec],
        out_shape=[big_shape] * 7 + [hm_shape],
        scratch_shapes=[pltpu.VMEM((1, D), F32)],
        compiler_params=_cp(("arbitrary", "arbitrary")),
        name="rwkv_pre",
    )(x, sc, sh, prev, w["mu"], w["w_rkv"], w["w1"], w["w2"], w["a1"], w["a2"], w["g1"], w["g2"], w["vec"])


def _wkv_chunk_kernel(r_ref, lw_ref, k_ref, v_ref, a_ref, b_ref, y_ref, zf_ref, z_ref, *, n_pairs, group):
    c = pl.program_id(0)

    @pl.when(c == 0)
    def _():
        z_ref[...] = jnp.zeros_like(z_ref)

    L = CHUNK
    row_l = lax.broadcasted_iota(I32, (L, L), 0)
    col_l = lax.broadcasted_iota(I32, (L, L), 1)
    tri_incl = (row_l >= col_l).astype(BF16)
    lane = lax.broadcasted_iota(I32, (L, LANES), 1)
    head0 = lane < HD
    row = lax.broadcasted_iota(I32, (LANES, LANES), 0)
    col = lax.broadcasted_iota(I32, (LANES, LANES), 1)
    strict = row > col
    incl = row >= col
    eye = (row == col).astype(F32)

    def expand(x):
        return jnp.concatenate([jnp.where(head0, x, 0.0), jnp.where(head0, 0.0, x)], axis=0)

    def group_body(gi, carry):
        ids = [gi * group + j for j in range(group)]
        bp = [(i // PAIRS, i % PAIRS) for i in ids]
        G = range(group)
        r = [r_ref[b_i, p_i] for b_i, p_i in bp]
        lw = [lw_ref[b_i, p_i] for b_i, p_i in bp]
        k = [k_ref[b_i, p_i] for b_i, p_i in bp]
        v = [v_ref[b_i, p_i] for b_i, p_i in bp]
        a = [a_ref[b_i, p_i] for b_i, p_i in bp]
        b = [b_ref[b_i, p_i] for b_i, p_i in bp]
        z0 = [z_ref[i] for i in ids]
        cum = [_split_dot_left(tri_incl, lw[j]) for j in G]
        cum_l = [cum[j][L - 1:L, :] for j in G]
        inv = [jnp.exp(-cum[j]) for j in G]
        tail = [jnp.exp(cum_l[j] - cum[j]) for j in G]
        a_e = [expand(a[j] * jnp.exp(cum[j] - lw[j])) for j in G]
        r_e = [expand(r[j] * jnp.exp(cum[j])) for j in G]
        b_e = [expand(b[j] * inv[j]) for j in G]
        k_e = [expand(k[j] * inv[j]) for j in G]
        bd_e = [expand(b[j] * tail[j]) for j in G]
        kd_e = [expand(k[j] * tail[j]) for j in G]
        v_e = [expand(v[j]) for j in G]
        gm = [_bdot_nt(jnp.concatenate([a_e[j], r_e[j]], axis=0), jnp.concatenate([b_e[j], k_e[j]], axis=0))
              for j in G]
        m_ab = [jnp.where(strict, gm[j][:LANES, :LANES], 0.0) for j in G]
        m_ak = [jnp.where(strict, gm[j][:LANES, LANES:], 0.0) for j in G]
        m_rb = [jnp.where(incl, gm[j][LANES:, :LANES], 0.0) for j in G]
        m_rk = [jnp.where(incl, gm[j][LANES:, LANES:], 0.0) for j in G]
        tinv = [eye + m_ab[j] for j in G]
        pw = m_ab
        for _ in range(5):
            pw = [_bdot(pw[j], pw[j]) for j in G]
            tinv = [tinv[j] + _bdot(pw[j], tinv[j]) for j in G]
        mv = [_bdot(jnp.concatenate([m_ak[j], m_rk[j]], axis=0), v_e[j]) for j in G]
        taw = [_bdot(tinv[j], jnp.concatenate([a_e[j], mv[j][:LANES]], axis=1)) for j in G]
        qy = [_bdot(m_rb[j], taw[j]) for j in G]
        pp = [_bdot_tn(bd_e[j], taw[j]) for j in G]
        kv = [_bdot_tn(kd_e[j], v_e[j]) for j in G]
        phi = [eye * jnp.exp(cum_l[j]) + pp[j][:, :LANES] for j in G]
        q = [r_e[j] + qy[j][:, :LANES] for j in G]
        zq = [_bdot(jnp.concatenate([phi[j], q[j]], axis=0), z0[j]) for j in G]
        for j in G:
            z_ref[ids[j]] = zq[j][:LANES] + pp[j][:, LANES:] + kv[j]
            y_e = zq[j][LANES:] + qy[j][:, LANES:] + mv[j][LANES:]
            y_ref[bp[j][0], bp[j][1]] = y_e[:L] + y_e[L:]
        return carry

    lax.fori_loop(0, n_pairs // group, group_body, 0)

    @pl.when(c == pl.num_programs(0) - 1)
    def _():
        zf_ref[...] = z_ref[...]


def _split_dot_left(m, x):
    hi = x.astype(BF16)
    lo = (x - hi.astype(F32)).astype(BF16)
    return (jnp.dot(m, hi, preferred_element_type=F32) + jnp.dot(m, lo, preferred_element_type=F32))


def _wkv_chunked(r, lw, k, v, a, b):
    bsz, _, t, _ = r.shape
    n_pairs = bsz * PAIRS
    blk = pl.BlockSpec((bsz, PAIRS, CHUNK, LANES), lambda c: (0, 0, c, 0))
    return pl.pallas_call(
        functools.partial(_wkv_chunk_kernel, n_pairs=n_pairs, group=WKV_GROUP),
        grid=(t // CHUNK,),
        in_specs=[blk] * 6,
        out_specs=[blk, pl.BlockSpec((n_pairs, LANES, LANES), lambda c: (0, 0, 0))],
        out_shape=[jax.ShapeDtypeStruct(r.shape, F32), jax.ShapeDtypeStruct((n_pairs, LANES, LANES), F32)],
        scratch_shapes=[pltpu.VMEM((n_pairs, LANES, LANES), F32)],
        compiler_params=_cp(("arbitrary",)),
        name="wkv_chunked",
    )(r, lw, k, v, a, b)


def _wkv_step_kernel(s_ref, r_ref, lw_ref, k_ref, v_ref, a_ref, b_ref, so_ref, y_ref):
    w = jnp.exp(lw_ref[...])
    a, b, k, r = a_ref[...], b_ref[...], k_ref[...], r_ref[...]

    def body(i, c):
        s = s_ref[i]
        sa = jnp.sum(s * a, axis=0, keepdims=True)
        s_new = s * w + sa * b + v_ref[pl.ds(i, 1), :] * k
        so_ref[i] = s_new
        y_ref[pl.ds(i, 1), :] = jnp.sum(s_new * r, axis=0, keepdims=True)
        return c

    lax.fori_loop(0, HD, body, 0, unroll=4)


def _wkv_step(state_t, r, lw, k, v, a, b):
    bsz = state_t.shape[-1]
    big = pl.BlockSpec((None, HD, HD, bsz), lambda h: (h, 0, 0, 0))
    vec = pl.BlockSpec((HD, bsz), lambda h: (h, 0))
    return pl.pallas_call(
        _wkv_step_kernel,
        grid=(HEADS,),
        in_specs=[big] + [vec] * 6,
        out_specs=[big, vec],
        out_shape=[jax.ShapeDtypeStruct(state_t.shape, F32), jax.ShapeDtypeStruct((D, bsz), F32)],
        compiler_params=_cp(("arbitrary",)),
        name="wkv_step",
    )(state_t, r, lw, k, v, a, b)


def _rwkv_post_kernel(y_ref, r_ref, k_ref, v_ref, g_ref, x_ref, gt_ref, vec_ref, wo_ref, o_ref, *, pairs):
    load = _from_pairs if pairs else (lambda ref: ref[...])
    y, r, k, v, g = (load(ref) for ref in (y_ref, r_ref, k_ref, v_ref, g_ref))
    r_k, lnx_g, lnx_b, ln_g, ln_b = (vec_ref[i:i + 1, :] for i in range(5))
    ones_bd = _head_ones()
    ym = _segsum(y, ones_bd) * (1.0 / HD)
    yc = y - ym
    yv = _segsum(yc * yc, ones_bd) * (1.0 / HD)
    yn = yc * lax.rsqrt(yv + GN_EPS) * lnx_g + lnx_b
    bonus = _segsum(r * k * r_k, ones_bd) * v
    mix = _bdot((yn + bonus) * g, wo_ref[...])
    o_ref[...] = _layer_norm(DN_ALPHA * x_ref[...] + gt_ref[...] * mix, ln_g, ln_b)


def _rwkv_post(y, r, k, v, g, x, gt, vec, w_o, *, pairs, tm):
    bsz, t, _ = x.shape
    row = pl.BlockSpec((None, tm, D), lambda b, i: (b, i, 0))
    big = pl.BlockSpec((None, PAIRS, tm, LANES), lambda b, i: (b, 0, i, 0)) if pairs else row
    mod = pl.BlockSpec((None, 1, D), lambda b, i: (b, 0, 0)) if pairs else row
    return pl.pallas_call(
        functools.partial(_rwkv_post_kernel, pairs=pairs),
        grid=(bsz, t // tm),
        in_specs=[big] * 5 + [row, mod, pl.BlockSpec((5, D), lambda b, i: (0, 0)),
                              pl.BlockSpec((D, D), lambda b, i: (0, 0))],
        out_specs=row,
        out_shape=jax.ShapeDtypeStruct(x.shape, F32),
        compiler_params=_cp(("arbitrary", "arbitrary")),
        name="rwkv_post",
    )(y, r, k, v, g, x, gt, vec, w_o)


def _top2(v):
    io = lax.broadcasted_iota(I32, v.shape, 0)
    m1 = jnp.max(v, axis=0, keepdims=True)
    i1 = jnp.min(jnp.where(v == m1, io, EXPERTS_PER_GROUP), axis=0, keepdims=True)
    v2 = jnp.where(io == i1, -jnp.inf, v)
    m2 = jnp.max(v2, axis=0, keepdims=True)
    i2 = jnp.min(jnp.where(v2 == m2, io, EXPERTS_PER_GROUP), axis=0, keepdims=True)
    return m1 + m2, i1, i2


def _moe_route_kernel(x_ref, sc_ref, sh_ref, rw_ref, rb_ref, h_ref, idx_ref, gate_ref, gmat_ref, cnt_ref,
                      carry_ref, *, tm, rows_out):
    @pl.when((pl.program_id(0) == 0) & (pl.program_id(1) == 0))
    def _():
        carry_ref[...] = jnp.zeros_like(carry_ref)

    h = x_ref[...] * (1.0 + sc_ref[...]) + sh_ref[...]
    h_ref[...] = h
    aff = _sigmoid(_bdot_nt(rw_ref[...], h))
    sel = aff + rb_ref[...]
    best = gi = i1 = i2 = None
    for g in range(N_EXPERT_GROUPS):
        sc, j1, j2 = _top2(sel[g * EXPERTS_PER_GROUP:(g + 1) * EXPERTS_PER_GROUP, :])
        if g == 0:
            best, gi, i1, i2 = sc, jnp.zeros_like(j1), j1, j2
        else:
            upd = sc > best
            best = jnp.where(upd, sc, best)
            gi = jnp.where(upd, g, gi)
            i1 = jnp.where(upd, j1, i1)
            i2 = jnp.where(upd, j2, i2)
    e0 = gi * EXPERTS_PER_GROUP + i1
    e1 = gi * EXPERTS_PER_GROUP + i2
    io = lax.broadcasted_iota(I32, (N_EXPERTS, tm), 0)
    oh0 = io == e0
    oh1 = io == e1
    a0 = jnp.sum(jnp.where(oh0, aff, 0.0), axis=0, keepdims=True)
    a1 = jnp.sum(jnp.where(oh1, aff, 0.0), axis=0, keepdims=True)
    den = a0 + a1
    g0 = a0 / den
    g1 = a1 / den
    oh = jnp.where(oh0 | oh1, 1.0, 0.0)
    tr = lax.broadcasted_iota(I32, (tm, tm), 0)
    tc = lax.broadcasted_iota(I32, (tm, tm), 1)
    before = jnp.dot(oh.astype(BF16), (tr < tc).astype(BF16), preferred_element_type=F32) + carry_ref[...]
    rank0 = jnp.sum(jnp.where(oh0, before, 0.0), axis=0, keepdims=True)
    rank1 = jnp.sum(jnp.where(oh1, before, 0.0), axis=0, keepdims=True)
    carry_ref[...] = carry_ref[...] + jnp.sum(oh, axis=1, keepdims=True)
    zi = jnp.zeros((4, tm), I32)
    idx_ref[...] = jnp.concatenate([e0, e1, rank0.astype(I32), rank1.astype(I32), zi], axis=0)
    gate_ref[...] = jnp.concatenate([g0, g1, jnp.zeros((6, tm), F32)], axis=0)
    gmat_ref[...] = jnp.where(oh0, g0, 0.0) + jnp.where(oh1, g1, 0.0)
    cnt_ref[...] = jnp.broadcast_to(carry_ref[...], cnt_ref.shape)


def _moe_route(x, sc, sh, router_w_t, router_b, *, per_b_mod, rows_out, tm):
    bsz, t, _ = x.shape
    n = bsz * t
    nt = t // tm
    row = pl.BlockSpec((None, tm, D), lambda b, i: (b, i, 0))
    mod = pl.BlockSpec((None, 1, D), lambda b, i: (b, 0, 0)) if per_b_mod else row
    if rows_out:
        h_spec = pl.BlockSpec((tm, D), lambda b, i: (b * nt + i, 0))
        h_shape = jax.ShapeDtypeStruct((n, D), F32)
    else:
        h_spec, h_shape = row, jax.ShapeDtypeStruct(x.shape, F32)
    tok = lambda rows: pl.BlockSpec((rows, tm), lambda b, i: (0, b * nt + i))
    return pl.pallas_call(
        functools.partial(_moe_route_kernel, tm=tm, rows_out=rows_out),
        grid=(bsz, nt),
        in_specs=[row, mod, mod, pl.BlockSpec((N_EXPERTS, D), lambda b, i: (0, 0)),
                  pl.BlockSpec((N_EXPERTS, 1), lambda b, i: (0, 0))],
        out_specs=[h_spec, tok(8), tok(8), tok(N_EXPERTS), pl.BlockSpec((N_EXPERTS, LANES), lambda b, i: (0, 0))],
        out_shape=[h_shape, jax.ShapeDtypeStruct((8, n), I32), jax.ShapeDtypeStruct((8, n), F32),
                   jax.ShapeDtypeStruct((N_EXPERTS, n), F32), jax.ShapeDtypeStruct((N_EXPERTS, LANES), F32)],
        scratch_shapes=[pltpu.VMEM((N_EXPERTS, 1), F32)],
        compiler_params=_cp(("arbitrary", "arbitrary")),
        name="moe_route",
    )(x, sc, sh, router_w_t, router_b)


def _row_copy(src, src_row, dst, dst_row, sem):
    return pltpu.make_async_copy(src.at[pl.ds(src_row, 1), :], dst.at[pl.ds(dst_row, 1), :], sem)


def _wait_rows(src, dst, rows, sem):
    pltpu.make_async_copy(src.at[pl.ds(0, rows), :], dst.at[pl.ds(0, rows), :], sem).wait()


def _moe_dispatch_kernel(d0_ref, d1_ref, h_ref, init_ref, xs_ref, sem, *, tm):
    del init_ref
    base = pl.program_id(0) * tm

    def issue(i, c):
        _row_copy(h_ref, i, xs_ref, d0_ref[base + i], sem.at[0]).start()
        _row_copy(h_ref, i, xs_ref, d1_ref[base + i], sem.at[1]).start()
        return c

    lax.fori_loop(0, tm, issue, 0, unroll=DMA_UNROLL)
    _wait_rows(h_ref, xs_ref, tm, sem.at[0])
    _wait_rows(h_ref, xs_ref, tm, sem.at[1])


def _moe_dispatch(h_rows, dest0, dest1, n_slots, tm):
    n = h_rows.shape[0]
    init = jnp.zeros((n_slots, D), F32)
    return pl.pallas_call(
        functools.partial(_moe_dispatch_kernel, tm=tm),
        grid_spec=pltpu.PrefetchScalarGridSpec(
            num_scalar_prefetch=2, grid=(n // tm,),
            in_specs=[pl.BlockSpec((tm, D), lambda i, d0, d1: (i, 0)),
                      pl.BlockSpec(memory_space=pl.ANY)],
            out_specs=pl.BlockSpec(memory_space=pl.ANY),
            scratch_shapes=[pltpu.SemaphoreType.DMA((2,))]),
        out_shape=jax.ShapeDtypeStruct((n_slots, D), F32),
        input_output_aliases={3: 0},
        compiler_params=_cp(("arbitrary",)),
        name="moe_dispatch",
    )(dest0, dest1, h_rows, init)


def _moe_expert_kernel(be_ref, used_ref, xs_ref, win_ref, wout_ref, ys_ref, win_bf, wout_bf):
    i = pl.program_id(0)
    fresh = (i == 0) | (be_ref[i] != be_ref[jnp.maximum(i - 1, 0)])

    @pl.when(fresh)
    def _():
        win_bf[...] = win_ref[...].astype(BF16)
        wout_bf[...] = wout_ref[...].astype(BF16)

    @pl.when(i < used_ref[0])
    def _():
        hmid = jnp.dot(xs_ref[...].astype(BF16), win_bf[...], preferred_element_type=F32)
        gt = hmid[:, :D_EXPERT]
        up = hmid[:, D_EXPERT:]
        act = gt * _sigmoid(gt) * up
        ys_ref[...] = jnp.dot(act.astype(BF16), wout_bf[...], preferred_element_type=F32)

    @pl.when(i >= used_ref[0])
    def _():
        ys_ref[...] = jnp.zeros_like(ys_ref)


def _moe_experts_sorted(xs, blk_exp, n_used, w_in, w_out, layer):
    n_slots = xs.shape[0]
    n_blocks = n_slots // MOE_BLK
    return pl.pallas_call(
        _moe_expert_kernel,
        grid_spec=pltpu.PrefetchScalarGridSpec(
            num_scalar_prefetch=2, grid=(n_blocks,),
            in_specs=[pl.BlockSpec((MOE_BLK, D), lambda i, be, u: (i, 0)),
                      pl.BlockSpec((None, None, D, 2 * D_EXPERT), lambda i, be, u: (layer, be[i], 0, 0)),
                      pl.BlockSpec((None, None, D_EXPERT, D), lambda i, be, u: (layer, be[i], 0, 0))],
            out_specs=pl.BlockSpec((MOE_BLK, D), lambda i, be, u: (i, 0)),
            scratch_shapes=[pltpu.VMEM((D, 2 * D_EXPERT), BF16), pltpu.VMEM((D_EXPERT, D), BF16)]),
        out_shape=jax.ShapeDtypeStruct((n_slots, D), F32),
        compiler_params=_cp(("arbitrary",)),
        name="moe_experts_sorted",
    )(blk_exp, n_used, xs, w_in, w_out)


def _moe_combine_kernel(d0_ref, d1_ref, ys_ref, gate_ref, x_ref, gt_ref, vec_ref, o_ref, ya, yb, sem, *, tm):
    step = pl.program_id(0)
    slot = step % 2

    def gather(s, sl):
        def issue(i, c):
            _row_copy(ys_ref, d0_ref[s * tm + i], ya.at[sl], i, sem.at[sl, 0]).start()
            _row_copy(ys_ref, d1_ref[s * tm + i], yb.at[sl], i, sem.at[sl, 1]).start()
            return c
        lax.fori_loop(0, tm, issue, 0, unroll=DMA_UNROLL)

    @pl.when(step == 0)
    def _():
        gather(0, 0)

    @pl.when(step + 1 < pl.num_programs(0))
    def _():
        gather(step + 1, 1 - slot)

    _wait_rows(ys_ref, ya.at[slot], tm, sem.at[slot, 0])
    _wait_rows(ys_ref, yb.at[slot], tm, sem.at[slot, 1])
    ff = ya[slot] * gate_ref[:, 0:1] + yb[slot] * gate_ref[:, 1:2]
    o_ref[...] = _layer_norm(DN_ALPHA * x_ref[...] + gt_ref[...] * ff, vec_ref[0:1, :], vec_ref[1:2, :])


def _moe_combine(ys, dest0, dest1, gates, x, gt, vec, tm):
    bsz, t, _ = x.shape
    nt = t // tm
    return pl.pallas_call(
        functools.partial(_moe_combine_kernel, tm=tm),
        grid_spec=pltpu.PrefetchScalarGridSpec(
            num_scalar_prefetch=2, grid=(bsz * nt,),
            in_specs=[pl.BlockSpec(memory_space=pl.ANY),
                      pl.BlockSpec((tm, 2), lambda i, d0, d1: (i, 0)),
                      pl.BlockSpec((None, tm, D), lambda i, d0, d1: (i // nt, i % nt, 0)),
                      pl.BlockSpec((None, 1, D), lambda i, d0, d1: (i // nt, 0, 0)),
                      pl.BlockSpec((2, D), lambda i, d0, d1: (0, 0))],
            out_specs=pl.BlockSpec((None, tm, D), lambda i, d0, d1: (i // nt, i % nt, 0)),
            scratch_shapes=[pltpu.VMEM((2, tm, D), F32), pltpu.VMEM((2, tm, D), F32),
                            pltpu.SemaphoreType.DMA((2, 2))]),
        out_shape=jax.ShapeDtypeStruct(x.shape, F32),
        compiler_params=_cp(("arbitrary",)),
        name="moe_combine",
    )(dest0, dest1, ys, gates, x, gt, vec)


def _moe_sorted(x, sc, sh, gt, vec, w, layer, tm=256):
    bsz, t, _ = x.shape
    n = bsz * t
    n_blocks = (n * 2) // MOE_BLK + N_EXPERTS
    h_rows, idx, gate, _, cnt = _moe_route(x, sc, sh, w["router_w_t"], w["router_b"], per_b_mod=True,
                                           rows_out=True, tm=tm)
    counts = cnt[:, 0].astype(I32)
    padded = (counts + MOE_BLK - 1) // MOE_BLK * MOE_BLK
    pad_end = jnp.cumsum(padded)
    pad_start = pad_end - padded
    dest0 = pad_start[idx[0]] + idx[2]
    dest1 = pad_start[idx[1]] + idx[3]
    blk_start = jnp.arange(n_blocks, dtype=I32) * MOE_BLK
    blk_exp = jnp.minimum(jnp.sum((pad_end[None, :] <= blk_start[:, None]).astype(I32), axis=1), N_EXPERTS - 1)
    n_used = (pad_end[-1:] // MOE_BLK).astype(I32)
    xs = _moe_dispatch(h_rows, dest0, dest1, n_blocks * MOE_BLK, tm)
    ys = _moe_experts_sorted(xs, blk_exp, n_used, w["moe_w_in"], w["moe_w_out"], layer)
    return _moe_combine(ys, dest0, dest1, gate[:2].T, x, gt, vec, tm)


def _moe_dense_kernel(h_ref, gm_ref, win_ref, wout_ref, x_ref, gt_ref, vec_ref, o_ref, acc_ref):
    e = pl.program_id(0)

    @pl.when(e == 0)
    def _():
        acc_ref[...] = jnp.zeros_like(acc_ref)

    lane = lax.broadcasted_iota(I32, gm_ref.shape, 1)
    gcol = jnp.sum(jnp.where(lane == e, gm_ref[...], 0.0), axis=1, keepdims=True)
    hmid = _bdot(h_ref[...], win_ref[...])
    gt = hmid[:, :D_EXPERT]
    up = hmid[:, D_EXPERT:]
    y = _bdot(gt * _sigmoid(gt) * up, wout_ref[...])
    acc_ref[...] = acc_ref[...] + jnp.where(gcol != 0.0, gcol * y, 0.0)

    @pl.when(e == N_EXPERTS - 1)
    def _():
        o_ref[...] = _layer_norm(DN_ALPHA * x_ref[...] + gt_ref[...] * acc_ref[...], vec_ref[0:1, :],
                                 vec_ref[1:2, :])


def _moe_dense(x, sc, sh, gt, vec, w, layer):
    _, m, _ = x.shape
    h, _, _, gmat, _ = _moe_route(x, sc, sh, w["router_w_t"], w["router_b"], per_b_mod=False, rows_out=False,
                                  tm=m)
    full = pl.BlockSpec((m, D), lambda e: (0, 0))
    out = pl.pallas_call(
        _moe_dense_kernel,
        grid=(N_EXPERTS,),
        in_specs=[full, pl.BlockSpec((m, N_EXPERTS), lambda e: (0, 0)),
                  pl.BlockSpec((None, None, D, 2 * D_EXPERT), lambda e: (layer, e, 0, 0)),
                  pl.BlockSpec((None, None, D_EXPERT, D), lambda e: (layer, e, 0, 0)),
                  full, full, pl.BlockSpec((2, D), lambda e: (0, 0))],
        out_specs=full,
        out_shape=jax.ShapeDtypeStruct((m, D), F32),
        scratch_shapes=[pltpu.VMEM((m, D), F32)],
        compiler_params=_cp(("arbitrary",)),
        name="moe_dense",
    )(h[0], gmat.T, w["moe_w_in"], w["moe_w_out"], x[0], gt[0], vec)
    return out[None]


def _rotary(x, cos_t, sa_t, sb_t):
    reps = x.shape[1] // LANES
    tile = lambda t: t if reps == 1 else jnp.concatenate([t] * reps, axis=1)
    n = x.shape[1]
    half = ROT_DIM // 2
    return x * tile(cos_t) + pltpu.roll(x, n - half, axis=1) * tile(sa_t) + pltpu.roll(x, half, axis=1) * tile(sb_t)


def _class_rows(c, rows, dil):
    return pl.ds(c, rows, stride=dil) if dil > 1 else pl.ds(0, rows)


def _attn_pre_kernel(x_ref, sc_ref, sh_ref, ksc_ref, ksh_ref, rot_ref, wq_ref, wkv_ref, *refs, dils, tm):
    x = x_ref[...]
    cos_t, sa_t, sb_t = rot_ref[0], rot_ref[1], rot_ref[2]
    q = _rotary(_bdot(x * (1.0 + sc_ref[...]) + sh_ref[...], wq_ref[...]), cos_t, sa_t, sb_t)
    kv = _bdot(x * (1.0 + ksc_ref[...]) + ksh_ref[...], wkv_ref[...])
    k = _rotary(kv[:, :KVW], cos_t, sa_t, sb_t)
    v = kv[:, KVW:]
    k_ref, v_ref = refs[0], refs[1]
    k_ref[...] = k
    v_ref[...] = v
    if dils is None:
        refs[2][...] = q
        return
    ng = len(dils)
    q_refs, kb_refs, vb_refs = refs[2:2 + ng], refs[2 + ng:2 + 2 * ng], refs[2 + 2 * ng:2 + 3 * ng]
    qs, kvs = refs[2 + 3 * ng:]
    for j in range(q.shape[1] // LANES):
        qs[j] = q[:, j * LANES:(j + 1) * LANES]
    for j in range(kv.shape[1] // LANES):
        kvs[j] = (k if j < KVW // LANES else v)[:, (j % (KVW // LANES)) * LANES:(j % (KVW // LANES) + 1) * LANES]

    def gather(ref, tiles, sl):
        return jnp.concatenate([ref[j, sl, :] for j in tiles], axis=1).astype(BF16)

    kt = KVW // LANES
    for g, dil in enumerate(dils):
        rows = tm // dil
        for c in range(dil):
            sl = _class_rows(c, rows, dil)
            q_refs[g][c] = gather(qs, range(g * PAIRS, (g + 1) * PAIRS), sl)
            kb_refs[g][c] = gather(kvs, range(kt), sl)
            vb_refs[g][c] = gather(kvs, range(kt, 2 * kt), sl)


def _attn_pre(x, sc, sh, ksc, ksh, rot, w_q, w_kv, *, per_b_mod, tm, dils=None):
    bsz, t, _ = x.shape
    row = lambda wd: pl.BlockSpec((None, tm, wd), lambda b, i: (b, i, 0))
    mod = pl.BlockSpec((None, 1, D), lambda b, i: (b, 0, 0)) if per_b_mod else row(D)
    nq = w_q.shape[1]
    out_specs = [row(KVW), row(KVW)]
    out_shape = [jax.ShapeDtypeStruct((bsz, t, KVW), F32)] * 2
    scratch = []
    if dils is None:
        out_specs.append(row(nq))
        out_shape.append(jax.ShapeDtypeStruct((bsz, t, nq), F32))
    else:
        for wd in (D, KVW, KVW):
            for dil in dils:
                out_specs.append(pl.BlockSpec((None, dil, tm // dil, wd), lambda b, i: (b, 0, i, 0)))
                out_shape.append(jax.ShapeDtypeStruct((bsz, dil, t // dil, wd), BF16))
        scratch = [pltpu.VMEM((nq // LANES, tm, LANES), F32), pltpu.VMEM((2 * KVW // LANES, tm, LANES), F32)]
    outs = pl.pallas_call(
        functools.partial(_attn_pre_kernel, dils=dils, tm=tm),
        grid=(bsz, t // tm),
        in_specs=[row(D), mod, mod, mod, mod, pl.BlockSpec((3, tm, LANES), lambda b, i: (0, i, 0)),
                  pl.BlockSpec((D, nq), lambda b, i: (0, 0)), pl.BlockSpec((D, 2 * KVW), lambda b, i: (0, 0))],
        out_specs=out_specs,
        out_shape=out_shape,
        scratch_shapes=scratch,
        compiler_params=_cp(("arbitrary", "arbitrary")),
        name="attn_pre",
    )(x, sc, sh, ksc, ksh, rot, w_q, w_kv)
    if dils is None:
        return outs
    ng = len(dils)
    return outs[0], outs[1], outs[2:2 + ng], outs[2 + ng:2 + 2 * ng], outs[2 + 2 * ng:]


def _attn_band_kernel(q_ref, ka_ref, kb_ref, va_ref, vb_ref, o_ref, lse_ref):
    i = pl.program_id(2)
    bq = ATT_WIN
    kcat = jnp.concatenate([ka_ref[...], kb_ref[...]], axis=0)
    vcat = jnp.concatenate([va_ref[...], vb_ref[...]], axis=0)
    qs = jnp.concatenate([q_ref[:, rep * KVW:(rep + 1) * KVW] for rep in range(REP)], axis=0)
    qrow = lax.broadcasted_iota(I32, (REP * bq, 2 * bq), 0) % bq
    kcol = lax.broadcasted_iota(I32, (REP * bq, 2 * bq), 1)
    valid = (kcol >= qrow) & (kcol <= qrow + bq) & ((i > 0) | (kcol >= bq))
    lane = lax.broadcasted_iota(I32, (REP * bq, KVW), 1) // HD
    lse_lane = lax.broadcasted_iota(I32, (bq, LANES), 1)
    lse = jnp.zeros((bq, LANES), F32)
    o = jnp.zeros((REP * bq, KVW), F32)
    linv = jnp.ones((REP * bq, KVW), F32)
    scale = HD ** -0.5
    for h in range(KV_HEADS):
        qm = jnp.where(lane == h, qs, jnp.zeros_like(qs))
        s = lax.dot_general(qm, kcat, (((1,), (1,)), ((), ())), preferred_element_type=F32) * scale
        s = jnp.where(valid, s, -jnp.inf)
        m = jnp.max(s, axis=-1, keepdims=True)
        p = jnp.exp(s - m)
        l = jnp.sum(p, axis=-1, keepdims=True)
        o = jnp.where(lane == h, jnp.dot(p.astype(BF16), vcat, preferred_element_type=F32), o)
        linv = jnp.where(lane == h, 1.0 / l, linv)
        lse_h = m + jnp.log(l)
        for rep in range(REP):
            lse = jnp.where(lse_lane == rep * KV_HEADS + h, lse_h[rep * bq:(rep + 1) * bq], lse)
    o = o * linv
    for rep in range(REP):
        o_ref[:, rep * KVW:(rep + 1) * KVW] = o[rep * bq:(rep + 1) * bq]
    lse_ref[...] = lse


def _attn_band(q, kb, vb, dil):
    bsz, _, tc, _ = q.shape
    nb = tc // ATT_WIN
    blk = lambda wd: pl.BlockSpec((None, None, ATT_WIN, wd), lambda b, c, i: (b, c, i, 0))
    prev = pl.BlockSpec((None, None, ATT_WIN, KVW), lambda b, c, i: (b, c, jnp.maximum(i - 1, 0), 0))
    return pl.pallas_call(
        _attn_band_kernel,
        grid=(bsz, dil, nb),
        in_specs=[blk(D), prev, blk(KVW), prev, blk(KVW)],
        out_specs=[blk(D), blk(LANES)],
        out_shape=[jax.ShapeDtypeStruct((bsz, dil, tc, D), F32), jax.ShapeDtypeStruct((bsz, dil, tc, LANES), F32)],
        compiler_params=_cp(("arbitrary", "arbitrary", "arbitrary")),
        name=f"attn_band_d{dil}",
    )(q, kb, kb, vb, vb)


def _attn_step_kernel(q_ref, kn_ref, vn_ref, kt_ref, vt_ref, o_ref, lse_ref, *, wbuf):
    nq = len(DILATED_GROUPS) * REP
    scale = HD ** -0.5
    row = lax.broadcasted_iota(I32, (nq, wbuf), 0)
    pos = lax.broadcasted_iota(I32, (nq, wbuf), 1)
    valid = None
    for g, (_, dil) in enumerate(DILATED_GROUPS):
        ok = (row // REP == g) & (pos >= wbuf - ATT_WIN * dil) & (pos % dil == 0)
        valid = ok if valid is None else valid | ok
    rnd = lambda t: t.astype(BF16).astype(F32)
    lse_lane = lax.broadcasted_iota(I32, (nq, LANES), 1)
    lse = jnp.zeros((nq, LANES), F32)
    for h in range(KV_HEADS):
        hs = slice(h * HD, (h + 1) * HD)
        qh = q_ref[:, hs]
        s = jnp.where(valid, _bdot(qh, kt_ref[h]) * scale, -1e30)
        s_n = jnp.sum(rnd(qh) * rnd(kn_ref[:, hs]), axis=-1, keepdims=True) * scale
        m = jnp.maximum(jnp.max(s, axis=-1, keepdims=True), s_n)
        p = jnp.exp(s - m)
        p_n = jnp.exp(s_n - m)
        l = jnp.sum(p, axis=-1, keepdims=True) + p_n
        o_ref[:, hs] = (_bdot_nt(p, vt_ref[h]) + rnd(p_n) * rnd(vn_ref[:, hs])) / l
        lse = jnp.where(lse_lane == h, m + jnp.log(l), lse)
    lse_ref[...] = lse


def _attn_step(q, k_new, v_new, cache_k, cache_v):
    bsz = q.shape[0]
    wbuf = cache_k.shape[1]
    ng = len(DILATED_GROUPS)
    assert all(wbuf % dil == 0 and wbuf >= ATT_WIN * dil for _, dil in DILATED_GROUPS)
    nq = ng * REP
    cache_spec = pl.BlockSpec((None, KV_HEADS, HD, wbuf), lambda b: (b, 0, 0, 0))
    o, lse = pl.pallas_call(
        functools.partial(_attn_step_kernel, wbuf=wbuf),
        grid=(bsz,),
        in_specs=[pl.BlockSpec((None, nq, KVW), lambda b: (b, 0, 0)),
                  pl.BlockSpec((None, 1, KVW), lambda b: (b, 0, 0)),
                  pl.BlockSpec((None, 1, KVW), lambda b: (b, 0, 0)), cache_spec, cache_spec],
        out_specs=[pl.BlockSpec((None, nq, KVW), lambda b: (b, 0, 0)),
                   pl.BlockSpec((None, nq, LANES), lambda b: (b, 0, 0))],
        out_shape=[jax.ShapeDtypeStruct((bsz, nq, KVW), F32), jax.ShapeDtypeStruct((bsz, nq, LANES), F32)],
        compiler_params=_cp(("arbitrary",)),
        name="attn_step",
    )(q.reshape(bsz, nq, KVW), k_new.reshape(bsz, 1, KVW), v_new.reshape(bsz, 1, KVW),
      cache_k.transpose(0, 2, 3, 1), cache_v.transpose(0, 2, 3, 1))
    o = o.reshape(bsz, ng, D).transpose(1, 0, 2)
    lse = lse[:, :, :KV_HEADS].reshape(bsz, ng, REP * KV_HEADS).transpose(1, 0, 2)
    return o, jnp.pad(lse, ((0, 0), (0, 0), (0, LANES - REP * KV_HEADS)))


def _attn_post_kernel(o0_ref, o1_ref, o2_ref, l0_ref, l1_ref, l2_ref, x_ref, gt_ref, vec_ref, wo_ref, out_ref,
                      *scratch, dils, tm):
    o_refs = [o0_ref, o1_ref, o2_ref]
    l_refs = [l0_ref, l1_ref, l2_ref]
    if dils is not None:
        for g, dil in enumerate(dils):
            if dil == 1:
                o_refs[g], l_refs[g] = o_refs[g].at[0], l_refs[g].at[0]
                continue
            so, sl = scratch[2 * g], scratch[2 * g + 1]
            for c in range(dil):
                rows = _class_rows(c, tm // dil, dil)
                oc = o_refs[g][c]
                for j in range(PAIRS):
                    so[j, rows, :] = oc[:, j * LANES:(j + 1) * LANES]
                sl[rows, :] = l_refs[g][c]
            o_refs[g], l_refs[g] = so, sl
    o0, o1, o2 = (_from_pairs(r) if len(r.shape) == 3 else r[...] for r in o_refs)
    l0, l1, l2 = (r[...] for r in l_refs)
    m = jnp.maximum(jnp.maximum(l0, l1), l2)
    e0, e1, e2 = jnp.exp(l0 - m), jnp.exp(l1 - m), jnp.exp(l2 - m)
    den = e0 + e1 + e2
    r = lax.broadcasted_iota(I32, (LANES, D), 0)
    c = lax.broadcasted_iota(I32, (LANES, D), 1) // HD
    spread = (r == c).astype(BF16)
    mix = (_split_dot(e0 / den, spread) * o0 + _split_dot(e1 / den, spread) * o1
           + _split_dot(e2 / den, spread) * o2)
    out = _bdot(mix, wo_ref[...])
    out_ref[...] = _layer_norm(DN_ALPHA * x_ref[...] + gt_ref[...] * out, vec_ref[0:1, :], vec_ref[1:2, :])


def _attn_post(os, lses, x, gt, vec, w_o, *, per_b_mod, tm, dils=None):
    bsz, t, _ = x.shape
    row = lambda wd: pl.BlockSpec((None, tm, wd), lambda b, i: (b, i, 0))
    mod = pl.BlockSpec((None, 1, D), lambda b, i: (b, 0, 0)) if per_b_mod else row(D)
    if dils is None:
        o_specs, l_specs, scratch = [row(D)] * 3, [row(LANES)] * 3, []
    else:
        cls = lambda dil, wd: pl.BlockSpec((None, dil, tm // dil, wd), lambda b, i: (b, 0, i, 0))
        o_specs = [cls(dil, D) for dil in dils]
        l_specs = [cls(dil, LANES) for dil in dils]
        scratch = [pltpu.VMEM(shape, F32) for _ in dils for shape in ((PAIRS, tm, LANES), (tm, LANES))]
    return pl.pallas_call(
        functools.partial(_attn_post_kernel, dils=dils, tm=tm),
        grid=(bsz, t // tm),
        in_specs=o_specs + l_specs + [row(D), mod, pl.BlockSpec((2, D), lambda b, i: (0, 0)),
                                      pl.BlockSpec((D, D), lambda b, i: (0, 0))],
        out_specs=row(D),
        out_shape=jax.ShapeDtypeStruct(x.shape, F32),
        scratch_shapes=scratch,
        compiler_params=_cp(("arbitrary", "arbitrary")),
        name="attn_post",
    )(*os, *lses, x, gt, vec, w_o)


def _rot_tables(pos):
    half = ROT_DIM // 2
    inv = ROPE_THETA ** (-jnp.arange(half, dtype=F32) * 2.0 / ROT_DIM)
    ang = pos.astype(F32)[:, None] * inv[None, :]
    cos, sin = jnp.cos(ang), jnp.sin(ang)
    t = pos.shape[0]
    one = jnp.ones((t, HD - ROT_DIM), F32)
    zero = jnp.zeros((t, HD - ROT_DIM), F32)
    zh = jnp.zeros((t, half), F32)
    cos_t = jnp.concatenate([cos, cos, one], axis=1)
    sa_t = jnp.concatenate([-sin, zh, zero], axis=1)
    sb_t = jnp.concatenate([zh, sin, zero], axis=1)
    return jnp.stack([jnp.tile(x, (1, 2)) for x in (cos_t, sa_t, sb_t)])


def _prep_weights(p):
    pad_c = lambda m, n: jnp.pad(m, ((0, 0), (0, n - m.shape[1])))
    pad_r = lambda m, n: jnp.pad(m, ((0, n - m.shape[0]), (0, 0)))
    w = {}
    w["mu"] = p["rwkv_mu"][0]
    w["w_rkv"] = p["rwkv_w_rkv"][0].astype(BF16)
    w["w1"] = pad_c(p["rwkv_w1"][0], LANES).astype(BF16)
    w["w2"] = pad_r(p["rwkv_w2"][0], LANES).astype(BF16)
    w["a1"] = pad_c(p["rwkv_a1"][0], LANES).astype(BF16)
    w["a2"] = pad_r(p["rwkv_a2"][0], LANES).astype(BF16)
    w["g1"] = pad_c(p["rwkv_g1"][0], 2 * LANES).astype(BF16)
    w["g2"] = pad_r(p["rwkv_g2"][0], 2 * LANES).astype(BF16)
    w["vec"] = jnp.stack([p["rwkv_w0"][0], p["rwkv_a0"][0], p["rwkv_k_k"][0], p["rwkv_k_a"][0]])
    w["post_vec"] = jnp.stack([p["rwkv_r_k"][0].reshape(D), p["rwkv_lnx_g"][0], p["rwkv_lnx_b"][0],
                               p["ln_g"][0, 0], p["ln_b"][0, 0]])
    w["rwkv_w_o"] = p["rwkv_w_o"][0].astype(BF16)
    wq = p["w_q"][0].reshape(D, 3, KV_HEADS, REP, HD).transpose(0, 1, 3, 2, 4).reshape(D, 3 * D)
    w["w_q"] = wq.astype(BF16)
    w["w_kv"] = p["w_kv"].astype(BF16)
    wo = p["w_o_attn"][0].reshape(KV_HEADS, REP, HD, D).transpose(1, 0, 2, 3).reshape(D, D)
    w["w_o_attn"] = wo.astype(BF16)
    w["router_w_t"] = p["router_w"].T.astype(BF16)
    w["router_b"] = p["router_b"].reshape(N_EXPERTS, 1)
    w["moe_w_in"] = p["moe_w_in"]
    w["moe_w_out"] = p["moe_w_out"]
    w["ln"] = [[jnp.stack([p["ln_g"][l, i], p["ln_b"][l, i]]) for i in range(2)] for l in range(DEPTH)]
    return w


def _modulations(c_prompt, c_sample, p):
    nb = c_prompt.shape[0]
    c = jnp.concatenate([c_prompt, c_sample], axis=0)
    pad = (-c.shape[0]) % 8
    c = jnp.pad(c, ((0, pad), (0, 0)))
    m3 = _ada_linear(c, p["ada_w"].reshape(2 * DEPTH, D, 3 * D), p["ada_b"].reshape(2 * DEPTH, 3 * D))
    m2 = _ada_linear(c, p["kv_ada_w"][None], p["kv_ada_b"][None])[0]
    n_all = nb + c_sample.shape[0]

    def split(m, parts, lo, hi):
        return [m[lo:hi, i * D:(i + 1) * D] for i in range(parts)]

    out = {}
    for name, lo, hi in (("prompt", 0, nb), ("sample", nb, n_all)):
        out[name] = {"ada": [[split(m3[2 * l + i], 3, lo, hi) for i in range(2)] for l in range(DEPTH)],
                     "kv": split(m2, 2, lo, hi)}
    return out


def _trunk_prompt(x, mods, w):
    bsz, t, _ = x.shape
    per_b = lambda m: m[:, None, :]
    sh, sc, gt = (per_b(m) for m in mods["ada"][0][0])
    zeros = jnp.zeros((bsz, 1, D), F32)
    r, lw, k, v, a, b, g, last = _rwkv_pre(x, sc, sh, zeros, w, seq_mode=True, tm=256)
    y, zf = _wkv_chunked(r, lw, k, v, a, b)
    x = _rwkv_post(y, r, k, v, g, x, gt, w["post_vec"], w["rwkv_w_o"], pairs=True, tm=256)
    zf = zf.reshape(bsz, PAIRS, 2, HD, 2, HD)
    wkv = jnp.stack([zf[:, :, 0, :, 0, :], zf[:, :, 1, :, 1, :]], axis=2)
    wkv = wkv.reshape(bsz, HEADS, HD, HD).transpose(0, 1, 3, 2)
    sh2, sc2, gt2 = (per_b(m) for m in mods["ada"][0][1])
    x = _moe_sorted(x, sc2, sh2, gt2, w["ln"][0][1], w, 0)

    sh, sc, gt = (per_b(m) for m in mods["ada"][1][0])
    ksh, ksc = (per_b(m) for m in mods["kv"])
    rot = _rot_tables(jnp.arange(t, dtype=I32))
    dils = tuple(dil for _, dil in DILATED_GROUPS)
    k_new, v_new, qs, kbs, vbs = _attn_pre(x, sc, sh, ksc, ksh, rot, w["w_q"], w["w_kv"], per_b_mod=True,
                                           tm=256, dils=dils)
    os, lses = zip(*[_attn_band(qs[g], kbs[g], vbs[g], dil) for g, dil in enumerate(dils)])
    x = _attn_post(os, lses, x, gt, w["ln"][1][0], w["w_o_attn"], per_b_mod=True, tm=256, dils=dils)
    sh2, sc2, gt2 = (per_b(m) for m in mods["ada"][1][1])
    x = _moe_sorted(x, sc2, sh2, gt2, w["ln"][1][1], w, 1)
    return x, wkv[None], last.reshape(1, bsz, D), k_new, v_new


def _trunk_sample(x, mods, state_wkv, state_shift, cache_k, cache_v, w):
    bsz = x.shape[0]
    xs = x.reshape(1, bsz, D)
    row = lambda m: m[None]
    sh, sc, gt = (row(m) for m in mods["ada"][0][0])
    r, lw, k, v, a, b, g, hm = _rwkv_pre(xs, sc, sh, state_shift[0][None], w, seq_mode=False, tm=bsz)
    s_new, y = _wkv_step(state_wkv[0].transpose(1, 2, 3, 0), *(t[0].T for t in (r, lw, k, v, a, b)))
    s_new = s_new.transpose(3, 0, 1, 2)
    xs = _rwkv_post(y.T[None], r, k, v, g, xs, gt, w["post_vec"], w["rwkv_w_o"], pairs=False, tm=bsz)
    sh2, sc2, gt2 = (row(m) for m in mods["ada"][0][1])
    xs = _moe_dense(xs, sc2, sh2, gt2, w["ln"][0][1], w, 0)

    sh, sc, gt = (row(m) for m in mods["ada"][1][0])
    ksh, ksc = (row(m) for m in mods["kv"])
    rot = _rot_tables(jnp.full((bsz,), PAST_LEN, I32))
    k_new, v_new, q = _attn_pre(xs, sc, sh, ksc, ksh, rot, w["w_q"], w["w_kv"], per_b_mod=False, tm=bsz)
    o, lse = _attn_step(q[0], k_new[0], v_new[0], cache_k, cache_v)
    xs = _attn_post([o[i][None] for i in range(3)], [lse[i][None] for i in range(3)], xs, gt, w["ln"][1][0],
                    w["w_o_attn"], per_b_mod=False, tm=bsz)
    sh2, sc2, gt2 = (row(m) for m in mods["ada"][1][1])
    xs = _moe_dense(xs, sc2, sh2, gt2, w["ln"][1][1], w, 1)
    return (xs.reshape(bsz, 1, D), s_new[None], hm, k_new.reshape(bsz, 1, KV_HEADS, HD),
            v_new.reshape(bsz, 1, KV_HEADS, HD))


def kernel(x_prompt, x_sample, state_wkv, state_shift, cache_k, cache_v, c_prompt, c_sample, ada_w, ada_b, ln_g, ln_b, rwkv_mu, rwkv_w_rkv, rwkv_w0, rwkv_w1, rwkv_w2, rwkv_a0, rwkv_a1, rwkv_a2, rwkv_g1, rwkv_g2, rwkv_k_k, rwkv_k_a, rwkv_r_k, rwkv_lnx_g, rwkv_lnx_b, rwkv_w_o, w_q, w_kv, kv_ada_w, kv_ada_b, w_o_attn, router_w, router_b, moe_w_in, moe_w_out):
    p = {"ada_w": ada_w, "ada_b": ada_b, "ln_g": ln_g, "ln_b": ln_b, "rwkv_mu": rwkv_mu, "rwkv_w_rkv": rwkv_w_rkv,
         "rwkv_w0": rwkv_w0, "rwkv_w1": rwkv_w1, "rwkv_w2": rwkv_w2, "rwkv_a0": rwkv_a0, "rwkv_a1": rwkv_a1,
         "rwkv_a2": rwkv_a2, "rwkv_g1": rwkv_g1, "rwkv_g2": rwkv_g2, "rwkv_k_k": rwkv_k_k, "rwkv_k_a": rwkv_k_a,
         "rwkv_r_k": rwkv_r_k, "rwkv_lnx_g": rwkv_lnx_g, "rwkv_lnx_b": rwkv_lnx_b, "rwkv_w_o": rwkv_w_o,
         "w_q": w_q, "w_kv": w_kv, "kv_ada_w": kv_ada_w, "kv_ada_b": kv_ada_b, "w_o_attn": w_o_attn,
         "router_w": router_w, "router_b": router_b, "moe_w_in": moe_w_in, "moe_w_out": moe_w_out}
    w = _prep_weights(p)
    mods = _modulations(c_prompt, c_sample, p)
    bp, tp, _ = x_prompt.shape
    y_p, wkv_p, shift_p, k_p, v_p = _trunk_prompt(x_prompt, mods["prompt"], w)
    keep = min(PAST_LEN, tp)
    k_p = k_p[:, tp - keep:].reshape(bp, keep, KV_HEADS, HD)
    v_p = v_p[:, tp - keep:].reshape(bp, keep, KV_HEADS, HD)
    y_s, wkv_s, shift_s, k_s, v_s = _trunk_sample(x_sample, mods["sample"], state_wkv, state_shift, cache_k,
                                                  cache_v, w)
    return (y_p, y_s, wkv_p, shift_p, k_p, v_p, wkv_s, shift_s, k_s, v_s)
```

```python
import functools

import jax
import jax.numpy as jnp
from jax import lax
from jax.experimental import pallas as pl
from jax.experimental.pallas import tpu as pltpu

F32 = jnp.float32
BF16 = jnp.bfloat16
I32 = jnp.int32

D = 1024
HEADS = 16
HD = 64
LANES = 128
PAIRS = D // LANES
CHUNK = 64
WKV_GROUP = 8
N_EXPERTS = 32
EXPERTS_PER_GROUP = 8
N_EXPERT_GROUPS = 4
D_EXPERT = 512
MOE_BLK = 256
DMA_UNROLL = 8
KV_HEADS = 4
REP = 4
KVW = KV_HEADS * HD
DILATED_GROUPS = ((128, 1), (512, 4), (2048, 16))
ATT_WIN = 128
PAST_LEN = 2048
ROT_DIM = 16
ROPE_THETA = 500000.0
DEPTH = 2
DN_ALPHA = (2 * DEPTH) ** 0.25
LN_EPS = 1e-5
GN_EPS = 64e-5
VMEM_LIMIT = 56 * 1024 * 1024


def _cp(sem):
    return pltpu.CompilerParams(dimension_semantics=sem, vmem_limit_bytes=VMEM_LIMIT)


def _bdot(a, b):
    return jnp.dot(a.astype(BF16), b.astype(BF16), preferred_element_type=F32)


def _bdot_nt(a, b):
    return lax.dot_general(a.astype(BF16), b.astype(BF16), (((1,), (1,)), ((), ())),
                           preferred_element_type=F32)


def _bdot_tn(a, b):
    return lax.dot_general(a.astype(BF16), b.astype(BF16), (((0,), (0,)), ((), ())),
                           preferred_element_type=F32)


def _split_dot(x, m):
    hi = x.astype(BF16)
    lo = (x - hi.astype(F32)).astype(BF16)
    return (jnp.dot(hi, m, preferred_element_type=F32) + jnp.dot(lo, m, preferred_element_type=F32))


def _head_ones():
    r = lax.broadcasted_iota(I32, (LANES, LANES), 0) // HD
    c = lax.broadcasted_iota(I32, (LANES, LANES), 1) // HD
    return (r == c).astype(BF16)


def _segsum(x, ones_bd):
    cols = [_split_dot(x[:, c * LANES:(c + 1) * LANES], ones_bd) for c in range(x.shape[1] // LANES)]
    return cols[0] if len(cols) == 1 else jnp.concatenate(cols, axis=1)


def _layer_norm(x, g, b):
    mu = jnp.mean(x, axis=-1, keepdims=True)
    xc = x - mu
    var = jnp.mean(xc * xc, axis=-1, keepdims=True)
    return xc * lax.rsqrt(var + LN_EPS) * g + b


def _sigmoid(x):
    return 1.0 / (1.0 + jnp.exp(-x))


def _to_pairs(ref, val):
    for p in range(PAIRS):
        ref[p] = val[:, p * LANES:(p + 1) * LANES]


def _from_pairs(ref):
    return jnp.concatenate([ref[p] for p in range(PAIRS)], axis=1)


def _ada_kernel(c_ref, w_ref, b_ref, o_ref):
    c = c_ref[...]
    o_ref[...] = _bdot(c * _sigmoid(c), w_ref[...]) + b_ref[...]


def _ada_linear(c, w, b, tn=512):
    s, _, n = w.shape
    m = c.shape[0]
    return pl.pallas_call(
        _ada_kernel,
        grid=(s, n // tn),
        in_specs=[pl.BlockSpec((m, D), lambda i, j: (0, 0)),
                  pl.BlockSpec((None, D, tn), lambda i, j: (i, 0, j)),
                  pl.BlockSpec((None, 1, tn), lambda i, j: (i, 0, j))],
        out_specs=pl.BlockSpec((None, m, tn), lambda i, j: (i, 0, j)),
        out_shape=jax.ShapeDtypeStruct((s, m, n), F32),
        compiler_params=_cp(("arbitrary", "arbitrary")),
        name="ada_linear",
    )(c, w, b.reshape(s, 1, n))


def _rwkv_pre_kernel(x_ref, sc_ref, sh_ref, prev_ref, mu_ref, wrkv_ref, w1_ref, w2_ref, a1_ref, a2_ref,
                     g1_ref, g2_ref, vec_ref,
                     r_ref, lw_ref, k_ref, v_ref, a_ref, b_ref, g_ref, hm_ref, carry_ref, *, seq_mode, tm):
    x = x_ref[...]
    hm = x * (1.0 + sc_ref[...]) + sh_ref[...]
    if seq_mode:
        @pl.when(pl.program_id(1) == 0)
        def _():
            carry_ref[...] = prev_ref[...]
        row = lax.broadcasted_iota(I32, hm.shape, 0)
        hprev = jnp.where(row == 0, carry_ref[...], pltpu.roll(hm, 1, axis=0))
        carry_ref[...] = hm[tm - 1:tm, :]
        hm_ref[...] = hm[tm - 1:tm, :]
    else:
        hprev = prev_ref[...]
        hm_ref[...] = hm
    xx = hprev - hm

    def mix(i):
        return hm + xx * mu_ref[i:i + 1, :]

    w0, a0, k_k, k_a = (vec_ref[i:i + 1, :] for i in range(4))
    r = _bdot(mix(0), wrkv_ref[0])
    k = _bdot(mix(2), wrkv_ref[1])
    v = _bdot(mix(3), wrkv_ref[2])
    wl = w0 + _bdot(jnp.tanh(_bdot(mix(1), w1_ref[...])), w2_ref[...])
    z = -wl
    softplus = jnp.maximum(z, 0.0) + jnp.log1p(jnp.exp(-jnp.abs(z)))
    lw = -jnp.exp(-softplus - 0.5)
    a_lr = _sigmoid(a0 + _bdot(_bdot(mix(4), a1_ref[...]), a2_ref[...]))
    g = _bdot(_sigmoid(_bdot(mix(5), g1_ref[...])), g2_ref[...])
    kk = k * k_k
    kk = kk / jnp.maximum(jnp.sqrt(_segsum(kk * kk, _head_ones())), 1e-12)
    kmod = k * (1.0 + (a_lr - 1.0) * k_a)
    outs = ((r_ref, r), (lw_ref, lw), (k_ref, kmod), (v_ref, v), (a_ref, -kk), (b_ref, kk * a_lr), (g_ref, g))
    for ref, val in outs:
        if seq_mode:
            _to_pairs(ref, val)
        else:
            ref[...] = val


def _rwkv_pre(x, sc, sh, prev, w, *, seq_mode, tm):
    bsz, t, _ = x.shape
    grid = (bsz, t // tm)
    row = pl.BlockSpec((None, tm, D), lambda b, i: (b, i, 0))
    per_b = pl.BlockSpec((None, 1, D), lambda b, i: (b, 0, 0))
    mod = per_b if seq_mode else row

    def const(shape):
        return pl.BlockSpec(shape, lambda b, i: (0,) * len(shape))

    if seq_mode:
        out_big = pl.BlockSpec((None, PAIRS, tm, LANES), lambda b, i: (b, 0, i, 0))
        big_shape = jax.ShapeDtypeStruct((bsz, PAIRS, t, LANES), F32)
        hm_spec, hm_shape = per_b, jax.ShapeDtypeStruct((bsz, 1, D), F32)
    else:
        out_big, big_shape = row, jax.ShapeDtypeStruct((bsz, t, D), F32)
        hm_spec, hm_shape = row, jax.ShapeDtypeStruct((bsz, t, D), F32)
    return pl.pallas_call(
        functools.partial(_rwkv_pre_kernel, seq_mode=seq_mode, tm=tm),
        grid=grid,
        in_specs=[row, mod, mod, mod, const((6, D)), const((3, D, D)), const((D, LANES)), const((LANES, D)),
                  const((D, LANES)), const((LANES, D)), const((D, 2 * LANES)), const((2 * LANES, D)),
                  const((4, D))],
        out_specs=[out_big] * 7 + [hm_spec],
        out_shape=[big_shape] * 7 + [hm_shape],
        scratch_shapes=[pltpu.VMEM((1, D), F32)],
        compiler_params=_cp(("arbitrary", "arbitrary")),
        name="rwkv_pre",
    )(x, sc, sh, prev, w["mu"], w["w_rkv"], w["w1"], w["w2"], w["a1"], w["a2"], w["g1"], w["g2"], w["vec"])


def _wkv_chunk_kernel(r_ref, lw_ref, k_ref, v_ref, a_ref, b_ref, y_ref, zf_ref, z_ref, *, n_pairs, group):
    c = pl.program_id(0)

    @pl.when(c == 0)
    def _():
        z_ref[...] = jnp.zeros_like(z_ref)

    L = CHUNK
    row_l = lax.broadcasted_iota(I32, (L, L), 0)
    col_l = lax.broadcasted_iota(I32, (L, L), 1)
    tri_incl = (row_l >= col_l).astype(BF16)
    lane = lax.broadcasted_iota(I32, (L, LANES), 1)
    head0 = lane < HD
    row = lax.broadcasted_iota(I32, (LANES, LANES), 0)
    col = lax.broadcasted_iota(I32, (LANES, LANES), 1)
    strict = row > col
    incl = row >= col
    eye = (row == col).astype(F32)

    def expand(x):
        return jnp.concatenate([jnp.where(head0, x, 0.0), jnp.where(head0, 0.0, x)], axis=0)

    def group_body(gi, carry):
        ids = [gi * group + j for j in range(group)]
        bp = [(i // PAIRS, i % PAIRS) for i in ids]
        G = range(group)
        r = [r_ref[b_i, p_i] for b_i, p_i in bp]
        lw = [lw_ref[b_i, p_i] for b_i, p_i in bp]
        k = [k_ref[b_i, p_i] for b_i, p_i in bp]
        v = [v_ref[b_i, p_i] for b_i, p_i in bp]
        a = [a_ref[b_i, p_i] for b_i, p_i in bp]
        b = [b_ref[b_i, p_i] for b_i, p_i in bp]
        z0 = [z_ref[i] for i in ids]
        cum = [_split_dot_left(tri_incl, lw[j]) for j in G]
        cum_l = [cum[j][L - 1:L, :] for j in G]
        inv = [jnp.exp(-cum[j]) for j in G]
        tail = [jnp.exp(cum_l[j] - cum[j]) for j in G]
        a_e = [expand(a[j] * jnp.exp(cum[j] - lw[j])) for j in G]
        r_e = [expand(r[j] * jnp.exp(cum[j])) for j in G]
        b_e = [expand(b[j] * inv[j]) for j in G]
        k_e = [expand(k[j] * inv[j]) for j in G]
        bd_e = [expand(b[j] * tail[j]) for j in G]
        kd_e = [expand(k[j] * tail[j]) for j in G]
        v_e = [expand(v[j]) for j in G]
        gm = [_bdot_nt(jnp.concatenate([a_e[j], r_e[j]], axis=0), jnp.concatenate([b_e[j], k_e[j]], axis=0))
              for j in G]
        m_ab = [jnp.where(strict, gm[j][:LANES, :LANES], 0.0) for j in G]
        m_ak = [jnp.where(strict, gm[j][:LANES, LANES:], 0.0) for j in G]
        m_rb = [jnp.where(incl, gm[j][LANES:, :LANES], 0.0) for j in G]
        m_rk = [jnp.where(incl, gm[j][LANES:, LANES:], 0.0) for j in G]
        tinv = [eye + m_ab[j] for j in G]
        pw = m_ab
        for _ in range(5):
            pw = [_bdot(pw[j], pw[j]) for j in G]
            tinv = [tinv[j] + _bdot(pw[j], tinv[j]) for j in G]
        mv = [_bdot(jnp.concatenate([m_ak[j], m_rk[j]], axis=0), v_e[j]) for j in G]
        taw = [_bdot(tinv[j], jnp.concatenate([a_e[j], mv[j][:LANES]], axis=1)) for j in G]
        qy = [_bdot(m_rb[j], taw[j]) for j in G]
        pp = [_bdot_tn(bd_e[j], taw[j]) for j in G]
        kv = [_bdot_tn(kd_e[j], v_e[j]) for j in G]
        phi = [eye * jnp.exp(cum_l[j]) + pp[j][:, :LANES] for j in G]
        q = [r_e[j] + qy[j][:, :LANES] for j in G]
        zq = [_bdot(jnp.concatenate([phi[j], q[j]], axis=0), z0[j]) for j in G]
        for j in G:
            z_ref[ids[j]] = zq[j][:LANES] + pp[j][:, LANES:] + kv[j]
            y_e = zq[j][LANES:] + qy[j][:, LANES:] + mv[j][LANES:]
            y_ref[bp[j][0], bp[j][1]] = y_e[:L] + y_e[L:]
        return carry

    lax.fori_loop(0, n_pairs // group, group_body, 0)

    @pl.when(c == pl.num_programs(0) - 1)
    def _():
        zf_ref[...] = z_ref[...]


def _split_dot_left(m, x):
    hi = x.astype(BF16)
    lo = (x - hi.astype(F32)).astype(BF16)
    return (jnp.dot(m, hi, preferred_element_type=F32) + jnp.dot(m, lo, preferred_element_type=F32))


def _wkv_chunked(r, lw, k, v, a, b):
    bsz, _, t, _ = r.shape
    n_pairs = bsz * PAIRS
    blk = pl.BlockSpec((bsz, PAIRS, CHUNK, LANES), lambda c: (0, 0, c, 0))
    return pl.pallas_call(
        functools.partial(_wkv_chunk_kernel, n_pairs=n_pairs, group=WKV_GROUP),
        grid=(t // CHUNK,),
        in_specs=[blk] * 6,
        out_specs=[blk, pl.BlockSpec((n_pairs, LANES, LANES), lambda c: (0, 0, 0))],
        out_shape=[jax.ShapeDtypeStruct(r.shape, F32), jax.ShapeDtypeStruct((n_pairs, LANES, LANES), F32)],
        scratch_shapes=[pltpu.VMEM((n_pairs, LANES, LANES), F32)],
        compiler_params=_cp(("arbitrary",)),
        name="wkv_chunked",
    )(r, lw, k, v, a, b)


def _wkv_step_kernel(s_ref, r_ref, lw_ref, k_ref, v_ref, a_ref, b_ref, so_ref, y_ref):
    w = jnp.exp(lw_ref[...])
    a, b, k, r = a_ref[...], b_ref[...], k_ref[...], r_ref[...]

    def body(i, c):
        s = s_ref[i]
        sa = jnp.sum(s * a, axis=0, keepdims=True)
        s_new = s * w + sa * b + v_ref[pl.ds(i, 1), :] * k
        so_ref[i] = s_new
        y_ref[pl.ds(i, 1), :] = jnp.sum(s_new * r, axis=0, keepdims=True)
        return c

    lax.fori_loop(0, HD, body, 0, unroll=4)


def _wkv_step(state_t, r, lw, k, v, a, b):
    bsz = state_t.shape[-1]
    big = pl.BlockSpec((None, HD, HD, bsz), lambda h: (h, 0, 0, 0))
    vec = pl.BlockSpec((HD, bsz), lambda h: (h, 0))
    return pl.pallas_call(
        _wkv_step_kernel,
        grid=(HEADS,),
        in_specs=[big] + [vec] * 6,
        out_specs=[big, vec],
        out_shape=[jax.ShapeDtypeStruct(state_t.shape, F32), jax.ShapeDtypeStruct((D, bsz), F32)],
        compiler_params=_cp(("arbitrary",)),
        name="wkv_step",
    )(state_t, r, lw, k, v, a, b)


def _rwkv_post_kernel(y_ref, r_ref, k_ref, v_ref, g_ref, x_ref, gt_ref, vec_ref, wo_ref, o_ref, *, pairs):
    load = _from_pairs if pairs else (lambda ref: ref[...])
    y, r, k, v, g = (load(ref) for ref in (y_ref, r_ref, k_ref, v_ref, g_ref))
    r_k, lnx_g, lnx_b, ln_g, ln_b = (vec_ref[i:i + 1, :] for i in range(5))
    ones_bd = _head_ones()
    ym = _segsum(y, ones_bd) * (1.0 / HD)
    yc = y - ym
    yv = _segsum(yc * yc, ones_bd) * (1.0 / HD)
    yn = yc * lax.rsqrt(yv + GN_EPS) * lnx_g + lnx_b
    bonus = _segsum(r * k * r_k, ones_bd) * v
    mix = _bdot((yn + bonus) * g, wo_ref[...])
    o_ref[...] = _layer_norm(DN_ALPHA * x_ref[...] + gt_ref[...] * mix, ln_g, ln_b)


def _rwkv_post(y, r, k, v, g, x, gt, vec, w_o, *, pairs, tm):
    bsz, t, _ = x.shape
    row = pl.BlockSpec((None, tm, D), lambda b, i: (b, i, 0))
    big = pl.BlockSpec((None, PAIRS, tm, LANES), lambda b, i: (b, 0, i, 0)) if pairs else row
    mod = pl.BlockSpec((None, 1, D), lambda b, i: (b, 0, 0)) if pairs else row
    return pl.pallas_call(
        functools.partial(_rwkv_post_kernel, pairs=pairs),
        grid=(bsz, t // tm),
        in_specs=[big] * 5 + [row, mod, pl.BlockSpec((5, D), lambda b, i: (0, 0)),
                              pl.BlockSpec((D, D), lambda b, i: (0, 0))],
        out_specs=row,
        out_shape=jax.ShapeDtypeStruct(x.shape, F32),
        compiler_params=_cp(("arbitrary", "arbitrary")),
        name="rwkv_post",
    )(y, r, k, v, g, x, gt, vec, w_o)


def _top2(v):
    io = lax.broadcasted_iota(I32, v.shape, 0)
    m1 = jnp.max(v, axis=0, keepdims=True)
    i1 = jnp.min(jnp.where(v == m1, io, EXPERTS_PER_GROUP), axis=0, keepdims=True)
    v2 = jnp.where(io == i1, -jnp.inf, v)
    m2 = jnp.max(v2, axis=0, keepdims=True)
    i2 = jnp.min(jnp.where(v2 == m2, io, EXPERTS_PER_GROUP), axis=0, keepdims=True)
    return m1 + m2, i1, i2


def _moe_route_kernel(x_ref, sc_ref, sh_ref, rw_ref, rb_ref, h_ref, idx_ref, gate_ref, gmat_ref, cnt_ref,
                      carry_ref, *, tm, rows_out):
    @pl.when((pl.program_id(0) == 0) & (pl.program_id(1) == 0))
    def _():
        carry_ref[...] = jnp.zeros_like(carry_ref)

    h = x_ref[...] * (1.0 + sc_ref[...]) + sh_ref[...]
    h_ref[...] = h
    aff = _sigmoid(_bdot_nt(rw_ref[...], h))
    sel = aff + rb_ref[...]
    best = gi = i1 = i2 = None
    for g in range(N_EXPERT_GROUPS):
        sc, j1, j2 = _top2(sel[g * EXPERTS_PER_GROUP:(g + 1) * EXPERTS_PER_GROUP, :])
        if g == 0:
            best, gi, i1, i2 = sc, jnp.zeros_like(j1), j1, j2
        else:
            upd = sc > best
            best = jnp.where(upd, sc, best)
            gi = jnp.where(upd, g, gi)
            i1 = jnp.where(upd, j1, i1)
            i2 = jnp.where(upd, j2, i2)
    e0 = gi * EXPERTS_PER_GROUP + i1
    e1 = gi * EXPERTS_PER_GROUP + i2
    io = lax.broadcasted_iota(I32, (N_EXPERTS, tm), 0)
    oh0 = io == e0
    oh1 = io == e1
    a0 = jnp.sum(jnp.where(oh0, aff, 0.0), axis=0, keepdims=True)
    a1 = jnp.sum(jnp.where(oh1, aff, 0.0), axis=0, keepdims=True)
    den = a0 + a1
    g0 = a0 / den
    g1 = a1 / den
    oh = jnp.where(oh0 | oh1, 1.0, 0.0)
    tr = lax.broadcasted_iota(I32, (tm, tm), 0)
    tc = lax.broadcasted_iota(I32, (tm, tm), 1)
    before = jnp.dot(oh.astype(BF16), (tr < tc).astype(BF16), preferred_element_type=F32) + carry_ref[...]
    rank0 = jnp.sum(jnp.where(oh0, before, 0.0), axis=0, keepdims=True)
    rank1 = jnp.sum(jnp.where(oh1, before, 0.0), axis=0, keepdims=True)
    carry_ref[...] = carry_ref[...] + jnp.sum(oh, axis=1, keepdims=True)
    zi = jnp.zeros((4, tm), I32)
    idx_ref[...] = jnp.concatenate([e0, e1, rank0.astype(I32), rank1.astype(I32), zi], axis=0)
    gate_ref[...] = jnp.concatenate([g0, g1, jnp.zeros((6, tm), F32)], axis=0)
    gmat_ref[...] = jnp.where(oh0, g0, 0.0) + jnp.where(oh1, g1, 0.0)
    cnt_ref[...] = jnp.broadcast_to(carry_ref[...], cnt_ref.shape)


def _moe_route(x, sc, sh, router_w_t, router_b, *, per_b_mod, rows_out, tm):
    bsz, t, _ = x.shape
    n = bsz * t
    nt = t // tm
    row = pl.BlockSpec((None, tm, D), lambda b, i: (b, i, 0))
    mod = pl.BlockSpec((None, 1, D), lambda b, i: (b, 0, 0)) if per_b_mod else row
    if rows_out:
        h_spec = pl.BlockSpec((tm, D), lambda b, i: (b * nt + i, 0))
        h_shape = jax.ShapeDtypeStruct((n, D), F32)
    else:
        h_spec, h_shape = row, jax.ShapeDtypeStruct(x.shape, F32)
    tok = lambda rows: pl.BlockSpec((rows, tm), lambda b, i: (0, b * nt + i))
    return pl.pallas_call(
        functools.partial(_moe_route_kernel, tm=tm, rows_out=rows_out),
        grid=(bsz, nt),
        in_specs=[row, mod, mod, pl.BlockSpec((N_EXPERTS, D), lambda b, i: (0, 0)),
                  pl.BlockSpec((N_EXPERTS, 1), lambda b, i: (0, 0))],
        out_specs=[h_spec, tok(8), tok(8), tok(N_EXPERTS), pl.BlockSpec((N_EXPERTS, LANES), lambda b, i: (0, 0))],
        out_shape=[h_shape, jax.ShapeDtypeStruct((8, n), I32), jax.ShapeDtypeStruct((8, n), F32),
                   jax.ShapeDtypeStruct((N_EXPERTS, n), F32), jax.ShapeDtypeStruct((N_EXPERTS, LANES), F32)],
        scratch_shapes=[pltpu.VMEM((N_EXPERTS, 1), F32)],
        compiler_params=_cp(("arbitrary", "arbitrary")),
        name="moe_route",
    )(x, sc, sh, router_w_t, router_b)


def _row_copy(src, src_row, dst, dst_row, sem):
    return pltpu.make_async_copy(src.at[pl.ds(src_row, 1), :], dst.at[pl.ds(dst_row, 1), :], sem)


def _wait_rows(src, dst, rows, sem):
    pltpu.make_async_copy(src.at[pl.ds(0, rows), :], dst.at[pl.ds(0, rows), :], sem).wait()


def _wait_rows_dynamic(src, dst, rows, sem):
    tiles = rows // 8

    @pl.when(tiles > 0)
    def _():
        _wait_rows(src, dst, pl.multiple_of(tiles * 8, 8), sem)

    def single(r, c):
        _wait_rows(src, dst, 1, sem)
        return c

    lax.fori_loop(tiles * 8, rows, single, 0)


def _for_rows(count, fn):
    full = count // DMA_UNROLL

    def group(j, c):
        for u in range(DMA_UNROLL):
            fn(j * DMA_UNROLL + u)
        return c

    lax.fori_loop(0, full, group, 0)

    def single(r, c):
        fn(r)
        return c

    lax.fori_loop(full * DMA_UNROLL, count, single, 0)


def _moe_expert_kernel(be_ref, used_ref, cnt_ref, asg_ref, h_ref, win_ref, wout_ref, y2_ref,
                       win_bf, wout_bf, xbuf, ybuf, gsem, ssem, *, n):
    i = pl.program_id(0)
    used = used_ref[0]
    slot = i % 2

    def gather(blk, sl):
        def start(r):
            tok = jnp.right_shift(asg_ref[blk * MOE_BLK + r], 1)
            _row_copy(h_ref, tok, xbuf.at[sl], r, gsem.at[sl]).start(priority=0)
        _for_rows(cnt_ref[blk], start)

    def scatter_wait(blk, sl):
        _wait_rows_dynamic(ybuf.at[sl], y2_ref, cnt_ref[blk], ssem.at[sl])

    @pl.when(i == 0)
    def _():
        xbuf[...] = jnp.zeros_like(xbuf)
        gather(0, 0)

    fresh = (i == 0) | (be_ref[i] != be_ref[jnp.maximum(i - 1, 0)])

    @pl.when(fresh & (i < used))
    def _():
        win_bf[...] = win_ref[...].astype(BF16)
        wout_bf[...] = wout_ref[...].astype(BF16)

    @pl.when(i < used)
    def _():
        @pl.when(i + 1 < used)
        def _():
            gather(i + 1, 1 - slot)

        _wait_rows_dynamic(h_ref, xbuf.at[slot], cnt_ref[i], gsem.at[slot])

        @pl.when(i >= 2)
        def _():
            scatter_wait(i - 2, slot)

        hmid = jnp.dot(xbuf[slot].astype(BF16), win_bf[...], preferred_element_type=F32)
        gt = hmid[:, :D_EXPERT]
        up = hmid[:, D_EXPERT:]
        act = gt * _sigmoid(gt) * up
        ybuf[slot] = jnp.dot(act.astype(BF16), wout_bf[...], preferred_element_type=F32)

        def start(r):
            a = asg_ref[i * MOE_BLK + r]
            row = jnp.bitwise_and(a, 1) * n + jnp.right_shift(a, 1)
            _row_copy(ybuf.at[slot], r, y2_ref, row, ssem.at[slot]).start(priority=1)
        _for_rows(cnt_ref[i], start)

        @pl.when(i == used - 1)
        def _():
            @pl.when(i >= 1)
            def _():
                scatter_wait(i - 1, 1 - slot)
            scatter_wait(i, slot)


def _moe_experts_fused(h, blk_exp, n_used, blk_cnt, slot_asg, w_in, w_out, layer):
    n = h.shape[0]
    n_blocks = blk_exp.shape[0]
    return pl.pallas_call(
        functools.partial(_moe_expert_kernel, n=n),
        grid_spec=pltpu.PrefetchScalarGridSpec(
            num_scalar_prefetch=4, grid=(n_blocks,),
            in_specs=[pl.BlockSpec(memory_space=pl.ANY),
                      pl.BlockSpec((None, None, D, 2 * D_EXPERT), lambda i, be, u, c, a: (layer, be[i], 0, 0)),
                      pl.BlockSpec((None, None, D_EXPERT, D), lambda i, be, u, c, a: (layer, be[i], 0, 0))],
            out_specs=pl.BlockSpec(memory_space=pl.ANY),
            scratch_shapes=[pltpu.VMEM((D, 2 * D_EXPERT), BF16), pltpu.VMEM((D_EXPERT, D), BF16),
                            pltpu.VMEM((2, MOE_BLK, D), F32), pltpu.VMEM((2, MOE_BLK, D), F32),
                            pltpu.SemaphoreType.DMA((2,)), pltpu.SemaphoreType.DMA((2,))]),
        out_shape=jax.ShapeDtypeStruct((2 * n, D), F32),
        compiler_params=_cp(("arbitrary",)),
        name="moe_experts_fused",
    )(blk_exp, n_used, blk_cnt, slot_asg, h, w_in, w_out)


def _moe_combine_kernel(ya_ref, yb_ref, gate_ref, x_ref, gt_ref, vec_ref, o_ref):
    ff = ya_ref[...] * gate_ref[:, 0:1] + yb_ref[...] * gate_ref[:, 1:2]
    o_ref[...] = _layer_norm(DN_ALPHA * x_ref[...] + gt_ref[...] * ff, vec_ref[0:1, :], vec_ref[1:2, :])


def _moe_combine(y2, gates, x, gt, vec, tm):
    bsz, t, _ = x.shape
    nt = t // tm
    row = pl.BlockSpec((None, tm, D), lambda b, i: (b, i, 0))
    return pl.pallas_call(
        _moe_combine_kernel,
        grid=(bsz, nt),
        in_specs=[pl.BlockSpec((tm, D), lambda b, i: (b * nt + i, 0)),
                  pl.BlockSpec((tm, D), lambda b, i: ((bsz + b) * nt + i, 0)),
                  pl.BlockSpec((tm, 2), lambda b, i: (b * nt + i, 0)),
                  row, pl.BlockSpec((None, 1, D), lambda b, i: (b, 0, 0)),
                  pl.BlockSpec((2, D), lambda b, i: (0, 0))],
        out_specs=row,
        out_shape=jax.ShapeDtypeStruct(x.shape, F32),
        compiler_params=_cp(("arbitrary", "arbitrary")),
        name="moe_combine",
    )(y2, y2, gates, x, gt, vec)


def _moe_sorted(x, sc, sh, gt, vec, w, layer, tm=256):
    bsz, t, _ = x.shape
    n = bsz * t
    n_blocks = (n * 2) // MOE_BLK + N_EXPERTS
    h_rows, idx, gate, _, cnt = _moe_route(x, sc, sh, w["router_w_t"], w["router_b"], per_b_mod=True,
                                           rows_out=True, tm=tm)
    counts = cnt[:, 0].astype(I32)
    padded = (counts + MOE_BLK - 1) // MOE_BLK * MOE_BLK
    pad_end = jnp.cumsum(padded)
    pad_start = pad_end - padded
    dest = jnp.stack([pad_start[idx[0]] + idx[2], pad_start[idx[1]] + idx[3]], axis=1).reshape(-1)
    slot_asg = jnp.zeros((n_blocks * MOE_BLK,), I32).at[dest].set(jnp.arange(2 * n, dtype=I32),
                                                                  unique_indices=True)
    blk_start = jnp.arange(n_blocks, dtype=I32) * MOE_BLK
    blk_exp = jnp.minimum(jnp.sum((pad_end[None, :] <= blk_start[:, None]).astype(I32), axis=1), N_EXPERTS - 1)
    blk_cnt = jnp.clip(pad_start[blk_exp] + counts[blk_exp] - blk_start, 0, MOE_BLK)
    n_used = (pad_end[-1:] // MOE_BLK).astype(I32)
    y2 = _moe_experts_fused(h_rows, blk_exp, n_used, blk_cnt, slot_asg, w["moe_w_in"], w["moe_w_out"], layer)
    return _moe_combine(y2, gate[:2].T, x, gt, vec, tm)


def _moe_dense_kernel(h_ref, gm_ref, win_ref, wout_ref, x_ref, gt_ref, vec_ref, o_ref, acc_ref):
    e = pl.program_id(0)

    @pl.when(e == 0)
    def _():
        acc_ref[...] = jnp.zeros_like(acc_ref)

    lane = lax.broadcasted_iota(I32, gm_ref.shape, 1)
    gcol = jnp.sum(jnp.where(lane == e, gm_ref[...], 0.0), axis=1, keepdims=True)
    hmid = _bdot(h_ref[...], win_ref[...])
    gt = hmid[:, :D_EXPERT]
    up = hmid[:, D_EXPERT:]
    y = _bdot(gt * _sigmoid(gt) * up, wout_ref[...])
    acc_ref[...] = acc_ref[...] + jnp.where(gcol != 0.0, gcol * y, 0.0)

    @pl.when(e == N_EXPERTS - 1)
    def _():
        o_ref[...] = _layer_norm(DN_ALPHA * x_ref[...] + gt_ref[...] * acc_ref[...], vec_ref[0:1, :],
                                 vec_ref[1:2, :])


def _moe_dense(x, sc, sh, gt, vec, w, layer):
    _, m, _ = x.shape
    h, _, _, gmat, _ = _moe_route(x, sc, sh, w["router_w_t"], w["router_b"], per_b_mod=False, rows_out=False,
                                  tm=m)
    full = pl.BlockSpec((m, D), lambda e: (0, 0))
    out = pl.pallas_call(
        _moe_dense_kernel,
        grid=(N_EXPERTS,),
        in_specs=[full, pl.BlockSpec((m, N_EXPERTS), lambda e: (0, 0)),
                  pl.BlockSpec((None, None, D, 2 * D_EXPERT), lambda e: (layer, e, 0, 0)),
                  pl.BlockSpec((None, None, D_EXPERT, D), lambda e: (layer, e, 0, 0)),
                  full, full, pl.BlockSpec((2, D), lambda e: (0, 0))],
        out_specs=full,
        out_shape=jax.ShapeDtypeStruct((m, D), F32),
        scratch_shapes=[pltpu.VMEM((m, D), F32)],
        compiler_params=_cp(("arbitrary",)),
        name="moe_dense",
    )(h[0], gmat.T, w["moe_w_in"], w["moe_w_out"], x[0], gt[0], vec)
    return out[None]


def _rotary(x, cos_t, sa_t, sb_t):
    reps = x.shape[1] // LANES
    tile = lambda t: t if reps == 1 else jnp.concatenate([t] * reps, axis=1)
    n = x.shape[1]
    half = ROT_DIM // 2
    return x * tile(cos_t) + pltpu.roll(x, n - half, axis=1) * tile(sa_t) + pltpu.roll(x, half, axis=1) * tile(sb_t)


def _class_rows(c, rows, dil):
    return pl.ds(c, rows, stride=dil) if dil > 1 else pl.ds(0, rows)


def _attn_pre_kernel(x_ref, sc_ref, sh_ref, ksc_ref, ksh_ref, rot_ref, wq_ref, wkv_ref, *refs, dils, tm):
    x = x_ref[...]
    cos_t, sa_t, sb_t = rot_ref[0], rot_ref[1], rot_ref[2]
    q = _rotary(_bdot(x * (1.0 + sc_ref[...]) + sh_ref[...], wq_ref[...]), cos_t, sa_t, sb_t)
    kv = _bdot(x * (1.0 + ksc_ref[...]) + ksh_ref[...], wkv_ref[...])
    k = _rotary(kv[:, :KVW], cos_t, sa_t, sb_t)
    v = kv[:, KVW:]
    k_ref, v_ref = refs[0], refs[1]
    k_ref[...] = k
    v_ref[...] = v
    if dils is None:
        refs[2][...] = q
        return
    ng = len(dils)
    q_refs, kb_refs, vb_refs = refs[2:2 + ng], refs[2 + ng:2 + 2 * ng], refs[2 + 2 * ng:2 + 3 * ng]
    qs, kvs = refs[2 + 3 * ng:]
    for j in range(q.shape[1] // LANES):
        qs[j] = q[:, j * LANES:(j + 1) * LANES]
    for j in range(kv.shape[1] // LANES):
        kvs[j] = (k if j < KVW // LANES else v)[:, (j % (KVW // LANES)) * LANES:(j % (KVW // LANES) + 1) * LANES]

    def gather(ref, tiles, sl):
        return jnp.concatenate([ref[j, sl, :] for j in tiles], axis=1).astype(BF16)

    kt = KVW // LANES
    for g, dil in enumerate(dils):
        rows = tm // dil
        for c in range(dil):
            sl = _class_rows(c, rows, dil)
            q_refs[g][c] = gather(qs, range(g * PAIRS, (g + 1) * PAIRS), sl)
            kb_refs[g][c] = gather(kvs, range(kt), sl)
            vb_refs[g][c] = gather(kvs, range(kt, 2 * kt), sl)


def _attn_pre(x, sc, sh, ksc, ksh, rot, w_q, w_kv, *, per_b_mod, tm, dils=None):
    bsz, t, _ = x.shape
    row = lambda wd: pl.BlockSpec((None, tm, wd), lambda b, i: (b, i, 0))
    mod = pl.BlockSpec((None, 1, D), lambda b, i: (b, 0, 0)) if per_b_mod else row(D)
    nq = w_q.shape[1]
    out_specs = [row(KVW), row(KVW)]
    out_shape = [jax.ShapeDtypeStruct((bsz, t, KVW), F32)] * 2
    scratch = []
    if dils is None:
        out_specs.append(row(nq))
        out_shape.append(jax.ShapeDtypeStruct((bsz, t, nq), F32))
    else:
        for wd in (D, KVW, KVW):
            for dil in dils:
                out_specs.append(pl.BlockSpec((None, dil, tm // dil, wd), lambda b, i: (b, 0, i, 0)))
                out_shape.append(jax.ShapeDtypeStruct((bsz, dil, t // dil, wd), BF16))
        scratch = [pltpu.VMEM((nq // LANES, tm, LANES), F32), pltpu.VMEM((2 * KVW // LANES, tm, LANES), F32)]
    outs = pl.pallas_call(
        functools.partial(_attn_pre_kernel, dils=dils, tm=tm),
        grid=(bsz, t // tm),
        in_specs=[row(D), mod, mod, mod, mod, pl.BlockSpec((3, tm, LANES), lambda b, i: (0, i, 0)),
                  pl.BlockSpec((D, nq), lambda b, i: (0, 0)), pl.BlockSpec((D, 2 * KVW), lambda b, i: (0, 0))],
        out_specs=out_specs,
        out_shape=out_shape,
        scratch_shapes=scratch,
        compiler_params=_cp(("arbitrary", "arbitrary")),
        name="attn_pre",
    )(x, sc, sh, ksc, ksh, rot, w_q, w_kv)
    if dils is None:
        return outs
    ng = len(dils)
    return outs[0], outs[1], outs[2:2 + ng], outs[2 + ng:2 + 2 * ng], outs[2 + 2 * ng:]


def _attn_band_kernel(q_ref, ka_ref, kb_ref, va_ref, vb_ref, o_ref, lse_ref):
    i = pl.program_id(2)
    bq = ATT_WIN
    kcat = jnp.concatenate([ka_ref[...], kb_ref[...]], axis=0)
    vcat = jnp.concatenate([va_ref[...], vb_ref[...]], axis=0)
    qs = jnp.concatenate([q_ref[:, rep * KVW:(rep + 1) * KVW] for rep in range(REP)], axis=0)
    qrow = lax.broadcasted_iota(I32, (REP * bq, 2 * bq), 0) % bq
    kcol = lax.broadcasted_iota(I32, (REP * bq, 2 * bq), 1)
    valid = (kcol >= qrow) & (kcol <= qrow + bq) & ((i > 0) | (kcol >= bq))
    lane = lax.broadcasted_iota(I32, (REP * bq, KVW), 1) // HD
    lse_lane = lax.broadcasted_iota(I32, (bq, LANES), 1)
    lse = jnp.zeros((bq, LANES), F32)
    o = jnp.zeros((REP * bq, KVW), F32)
    linv = jnp.ones((REP * bq, KVW), F32)
    scale = HD ** -0.5
    for h in range(KV_HEADS):
        qm = jnp.where(lane == h, qs, jnp.zeros_like(qs))
        s = lax.dot_general(qm, kcat, (((1,), (1,)), ((), ())), preferred_element_type=F32) * scale
        s = jnp.where(valid, s, -jnp.inf)
        m = jnp.max(s, axis=-1, keepdims=True)
        p = jnp.exp(s - m)
        l = jnp.sum(p, axis=-1, keepdims=True)
        o = jnp.where(lane == h, jnp.dot(p.astype(BF16), vcat, preferred_element_type=F32), o)
        linv = jnp.where(lane == h, 1.0 / l, linv)
        lse_h = m + jnp.log(l)
        for rep in range(REP):
            lse = jnp.where(lse_lane == rep * KV_HEADS + h, lse_h[rep * bq:(rep + 1) * bq], lse)
    o = o * linv
    for rep in range(REP):
        o_ref[:, rep * KVW:(rep + 1) * KVW] = o[rep * bq:(rep + 1) * bq]
    lse_ref[...] = lse


def _attn_band(q, kb, vb, dil):
    bsz, _, tc, _ = q.shape
    nb = tc // ATT_WIN
    blk = lambda wd: pl.BlockSpec((None, None, ATT_WIN, wd), lambda b, c, i: (b, c, i, 0))
    prev = pl.BlockSpec((None, None, ATT_WIN, KVW), lambda b, c, i: (b, c, jnp.maximum(i - 1, 0), 0))
    return pl.pallas_call(
        _attn_band_kernel,
        grid=(bsz, dil, nb),
        in_specs=[blk(D), prev, blk(KVW), prev, blk(KVW)],
        out_specs=[blk(D), blk(LANES)],
        out_shape=[jax.ShapeDtypeStruct((bsz, dil, tc, D), F32), jax.ShapeDtypeStruct((bsz, dil, tc, LANES), F32)],
        compiler_params=_cp(("arbitrary", "arbitrary", "arbitrary")),
        name=f"attn_band_d{dil}",
    )(q, kb, kb, vb, vb)


def _attn_step_kernel(q_ref, kn_ref, vn_ref, kt_ref, vt_ref, o_ref, lse_ref, *, wbuf):
    nq = len(DILATED_GROUPS) * REP
    scale = HD ** -0.5
    row = lax.broadcasted_iota(I32, (nq, wbuf), 0)
    pos = lax.broadcasted_iota(I32, (nq, wbuf), 1)
    valid = None
    for g, (_, dil) in enumerate(DILATED_GROUPS):
        ok = (row // REP == g) & (pos >= wbuf - ATT_WIN * dil) & (pos % dil == 0)
        valid = ok if valid is None else valid | ok
    rnd = lambda t: t.astype(BF16).astype(F32)
    lse_lane = lax.broadcasted_iota(I32, (nq, LANES), 1)
    lse = jnp.zeros((nq, LANES), F32)
    for h in range(KV_HEADS):
        hs = slice(h * HD, (h + 1) * HD)
        qh = q_ref[:, hs]
        s = jnp.where(valid, _bdot(qh, kt_ref[h]) * scale, -1e30)
        s_n = jnp.sum(rnd(qh) * rnd(kn_ref[:, hs]), axis=-1, keepdims=True) * scale
        m = jnp.maximum(jnp.max(s, axis=-1, keepdims=True), s_n)
        p = jnp.exp(s - m)
        p_n = jnp.exp(s_n - m)
        l = jnp.sum(p, axis=-1, keepdims=True) + p_n
        o_ref[:, hs] = (_bdot_nt(p, vt_ref[h]) + rnd(p_n) * rnd(vn_ref[:, hs])) / l
        lse = jnp.where(lse_lane == h, m + jnp.log(l), lse)
    lse_ref[...] = lse


def _attn_step(q, k_new, v_new, cache_k, cache_v):
    bsz = q.shape[0]
    wbuf = cache_k.shape[1]
    ng = len(DILATED_GROUPS)
    assert all(wbuf % dil == 0 and wbuf >= ATT_WIN * dil for _, dil in DILATED_GROUPS)
    nq = ng * REP
    cache_spec = pl.BlockSpec((None, KV_HEADS, HD, wbuf), lambda b: (b, 0, 0, 0))
    o, lse = pl.pallas_call(
        functools.partial(_attn_step_kernel, wbuf=wbuf),
        grid=(bsz,),
        in_specs=[pl.BlockSpec((None, nq, KVW), lambda b: (b, 0, 0)),
                  pl.BlockSpec((None, 1, KVW), lambda b: (b, 0, 0)),
                  pl.BlockSpec((None, 1, KVW), lambda b: (b, 0, 0)), cache_spec, cache_spec],
        out_specs=[pl.BlockSpec((None, nq, KVW), lambda b: (b, 0, 0)),
                   pl.BlockSpec((None, nq, LANES), lambda b: (b, 0, 0))],
        out_shape=[jax.ShapeDtypeStruct((bsz, nq, KVW), F32), jax.ShapeDtypeStruct((bsz, nq, LANES), F32)],
        compiler_params=_cp(("arbitrary",)),
        name="attn_step",
    )(q.reshape(bsz, nq, KVW), k_new.reshape(bsz, 1, KVW), v_new.reshape(bsz, 1, KVW),
      cache_k.transpose(0, 2, 3, 1), cache_v.transpose(0, 2, 3, 1))
    o = o.reshape(bsz, ng, D).transpose(1, 0, 2)
    lse = lse[:, :, :KV_HEADS].reshape(bsz, ng, REP * KV_HEADS).transpose(1, 0, 2)
    return o, jnp.pad(lse, ((0, 0), (0, 0), (0, LANES - REP * KV_HEADS)))


def _attn_post_kernel(o0_ref, o1_ref, o2_ref, l0_ref, l1_ref, l2_ref, x_ref, gt_ref, vec_ref, wo_ref, out_ref,
                      *scratch, dils, tm):
    o_refs = [o0_ref, o1_ref, o2_ref]
    l_refs = [l0_ref, l1_ref, l2_ref]
    if dils is not None:
        for g, dil in enumerate(dils):
            if dil == 1:
                o_refs[g], l_refs[g] = o_refs[g].at[0], l_refs[g].at[0]
                continue
            so, sl = scratch[2 * g], scratch[2 * g + 1]
            for c in range(dil):
                rows = _class_rows(c, tm // dil, dil)
                oc = o_refs[g][c]
                for j in range(PAIRS):
                    so[j, rows, :] = oc[:, j * LANES:(j + 1) * LANES]
                sl[rows, :] = l_refs[g][c]
            o_refs[g], l_refs[g] = so, sl
    o0, o1, o2 = (_from_pairs(r) if len(r.shape) == 3 else r[...] for r in o_refs)
    l0, l1, l2 = (r[...] for r in l_refs)
    m = jnp.maximum(jnp.maximum(l0, l1), l2)
    e0, e1, e2 = jnp.exp(l0 - m), jnp.exp(l1 - m), jnp.exp(l2 - m)
    den = e0 + e1 + e2
    r = lax.broadcasted_iota(I32, (LANES, D), 0)
    c = lax.broadcasted_iota(I32, (LANES, D), 1) // HD
    spread = (r == c).astype(BF16)
    mix = (_split_dot(e0 / den, spread) * o0 + _split_dot(e1 / den, spread) * o1
           + _split_dot(e2 / den, spread) * o2)
    out = _bdot(mix, wo_ref[...])
    out_ref[...] = _layer_norm(DN_ALPHA * x_ref[...] + gt_ref[...] * out, vec_ref[0:1, :], vec_ref[1:2, :])


def _attn_post(os, lses, x, gt, vec, w_o, *, per_b_mod, tm, dils=None):
    bsz, t, _ = x.shape
    row = lambda wd: pl.BlockSpec((None, tm, wd), lambda b, i: (b, i, 0))
    mod = pl.BlockSpec((None, 1, D), lambda b, i: (b, 0, 0)) if per_b_mod else row(D)
    if dils is None:
        o_specs, l_specs, scratch = [row(D)] * 3, [row(LANES)] * 3, []
    else:
        cls = lambda dil, wd: pl.BlockSpec((None, dil, tm // dil, wd), lambda b, i: (b, 0, i, 0))
        o_specs = [cls(dil, D) for dil in dils]
        l_specs = [cls(dil, LANES) for dil in dils]
        scratch = [pltpu.VMEM(shape, F32) for _ in dils for shape in ((PAIRS, tm, LANES), (tm, LANES))]
    return pl.pallas_call(
        functools.partial(_attn_post_kernel, dils=dils, tm=tm),
        grid=(bsz, t // tm),
        in_specs=o_specs + l_specs + [row(D), mod, pl.BlockSpec((2, D), lambda b, i: (0, 0)),
                                      pl.BlockSpec((D, D), lambda b, i: (0, 0))],
        out_specs=row(D),
        out_shape=jax.ShapeDtypeStruct(x.shape, F32),
        scratch_shapes=scratch,
        compiler_params=_cp(("arbitrary", "arbitrary")),
        name="attn_post",
    )(*os, *lses, x, gt, vec, w_o)


def _rot_tables(pos):
    half = ROT_DIM // 2
    inv = ROPE_THETA ** (-jnp.arange(half, dtype=F32) * 2.0 / ROT_DIM)
    ang = pos.astype(F32)[:, None] * inv[None, :]
    cos, sin = jnp.cos(ang), jnp.sin(ang)
    t = pos.shape[0]
    one = jnp.ones((t, HD - ROT_DIM), F32)
    zero = jnp.zeros((t, HD - ROT_DIM), F32)
    zh = jnp.zeros((t, half), F32)
    cos_t = jnp.concatenate([cos, cos, one], axis=1)
    sa_t = jnp.concatenate([-sin, zh, zero], axis=1)
    sb_t = jnp.concatenate([zh, sin, zero], axis=1)
    return jnp.stack([jnp.tile(x, (1, 2)) for x in (cos_t, sa_t, sb_t)])


def _prep_weights(p):
    pad_c = lambda m, n: jnp.pad(m, ((0, 0), (0, n - m.shape[1])))
    pad_r = lambda m, n: jnp.pad(m, ((0, n - m.shape[0]), (0, 0)))
    w = {}
    w["mu"] = p["rwkv_mu"][0]
    w["w_rkv"] = p["rwkv_w_rkv"][0].astype(BF16)
    w["w1"] = pad_c(p["rwkv_w1"][0], LANES).astype(BF16)
    w["w2"] = pad_r(p["rwkv_w2"][0], LANES).astype(BF16)
    w["a1"] = pad_c(p["rwkv_a1"][0], LANES).astype(BF16)
    w["a2"] = pad_r(p["rwkv_a2"][0], LANES).astype(BF16)
    w["g1"] = pad_c(p["rwkv_g1"][0], 2 * LANES).astype(BF16)
    w["g2"] = pad_r(p["rwkv_g2"][0], 2 * LANES).astype(BF16)
    w["vec"] = jnp.stack([p["rwkv_w0"][0], p["rwkv_a0"][0], p["rwkv_k_k"][0], p["rwkv_k_a"][0]])
    w["post_vec"] = jnp.stack([p["rwkv_r_k"][0].reshape(D), p["rwkv_lnx_g"][0], p["rwkv_lnx_b"][0],
                               p["ln_g"][0, 0], p["ln_b"][0, 0]])
    w["rwkv_w_o"] = p["rwkv_w_o"][0].astype(BF16)
    wq = p["w_q"][0].reshape(D, 3, KV_HEADS, REP, HD).transpose(0, 1, 3, 2, 4).reshape(D, 3 * D)
    w["w_q"] = wq.astype(BF16)
    w["w_kv"] = p["w_kv"].astype(BF16)
    wo = p["w_o_attn"][0].reshape(KV_HEADS, REP, HD, D).transpose(1, 0, 2, 3).reshape(D, D)
    w["w_o_attn"] = wo.astype(BF16)
    w["router_w_t"] = p["router_w"].T.astype(BF16)
    w["router_b"] = p["router_b"].reshape(N_EXPERTS, 1)
    w["moe_w_in"] = p["moe_w_in"]
    w["moe_w_out"] = p["moe_w_out"]
    w["ln"] = [[jnp.stack([p["ln_g"][l, i], p["ln_b"][l, i]]) for i in range(2)] for l in range(DEPTH)]
    return w


def _modulations(c_prompt, c_sample, p):
    nb = c_prompt.shape[0]
    c = jnp.concatenate([c_prompt, c_sample], axis=0)
    pad = (-c.shape[0]) % 8
    c = jnp.pad(c, ((0, pad), (0, 0)))
    m3 = _ada_linear(c, p["ada_w"].reshape(2 * DEPTH, D, 3 * D), p["ada_b"].reshape(2 * DEPTH, 3 * D))
    m2 = _ada_linear(c, p["kv_ada_w"][None], p["kv_ada_b"][None])[0]
    n_all = nb + c_sample.shape[0]

    def split(m, parts, lo, hi):
        return [m[lo:hi, i * D:(i + 1) * D] for i in range(parts)]

    out = {}
    for name, lo, hi in (("prompt", 0, nb), ("sample", nb, n_all)):
        out[name] = {"ada": [[split(m3[2 * l + i], 3, lo, hi) for i in range(2)] for l in range(DEPTH)],
                     "kv": split(m2, 2, lo, hi)}
    return out


def _trunk_prompt(x, mods, w):
    bsz, t, _ = x.shape
    per_b = lambda m: m[:, None, :]
    sh, sc, gt = (per_b(m) for m in mods["ada"][0][0])
    zeros = jnp.zeros((bsz, 1, D), F32)
    r, lw, k, v, a, b, g, last = _rwkv_pre(x, sc, sh, zeros, w, seq_mode=True, tm=256)
    y, zf = _wkv_chunked(r, lw, k, v, a, b)
    x = _rwkv_post(y, r, k, v, g, x, gt, w["post_vec"], w["rwkv_w_o"], pairs=True, tm=256)
    zf = zf.reshape(bsz, PAIRS, 2, HD, 2, HD)
    wkv = jnp.stack([zf[:, :, 0, :, 0, :], zf[:, :, 1, :, 1, :]], axis=2)
    wkv = wkv.reshape(bsz, HEADS, HD, HD).transpose(0, 1, 3, 2)
    sh2, sc2, gt2 = (per_b(m) for m in mods["ada"][0][1])
    x = _moe_sorted(x, sc2, sh2, gt2, w["ln"][0][1], w, 0)

    sh, sc, gt = (per_b(m) for m in mods["ada"][1][0])
    ksh, ksc = (per_b(m) for m in mods["kv"])
    rot = _rot_tables(jnp.arange(t, dtype=I32))
    dils = tuple(dil for _, dil in DILATED_GROUPS)
    k_new, v_new, qs, kbs, vbs = _attn_pre(x, sc, sh, ksc, ksh, rot, w["w_q"], w["w_kv"], per_b_mod=True,
                                           tm=256, dils=dils)
    os, lses = zip(*[_attn_band(qs[g], kbs[g], vbs[g], dil) for g, dil in enumerate(dils)])
    x = _attn_post(os, lses, x, gt, w["ln"][1][0], w["w_o_attn"], per_b_mod=True, tm=256, dils=dils)
    sh2, sc2, gt2 = (per_b(m) for m in mods["ada"][1][1])
    x = _moe_sorted(x, sc2, sh2, gt2, w["ln"][1][1], w, 1)
    return x, wkv[None], last.reshape(1, bsz, D), k_new, v_new


def _trunk_sample(x, mods, state_wkv, state_shift, cache_k, cache_v, w):
    bsz = x.shape[0]
    xs = x.reshape(1, bsz, D)
    row = lambda m: m[None]
    sh, sc, gt = (row(m) for m in mods["ada"][0][0])
    r, lw, k, v, a, b, g, hm = _rwkv_pre(xs, sc, sh, state_shift[0][None], w, seq_mode=False, tm=bsz)
    s_new, y = _wkv_step(state_wkv[0].transpose(1, 2, 3, 0), *(t[0].T for t in (r, lw, k, v, a, b)))
    s_new = s_new.transpose(3, 0, 1, 2)
    xs = _rwkv_post(y.T[None], r, k, v, g, xs, gt, w["post_vec"], w["rwkv_w_o"], pairs=False, tm=bsz)
    sh2, sc2, gt2 = (row(m) for m in mods["ada"][0][1])
    xs = _moe_dense(xs, sc2, sh2, gt2, w["ln"][0][1], w, 0)

    sh, sc, gt = (row(m) for m in mods["ada"][1][0])
    ksh, ksc = (row(m) for m in mods["kv"])
    rot = _rot_tables(jnp.full((bsz,), PAST_LEN, I32))
    k_new, v_new, q = _attn_pre(xs, sc, sh, ksc, ksh, rot, w["w_q"], w["w_kv"], per_b_mod=False, tm=bsz)
    o, lse = _attn_step(q[0], k_new[0], v_new[0], cache_k, cache_v)
    xs = _attn_post([o[i][None] for i in range(3)], [lse[i][None] for i in range(3)], xs, gt, w["ln"][1][0],
                    w["w_o_attn"], per_b_mod=False, tm=bsz)
    sh2, sc2, gt2 = (row(m) for m in mods["ada"][1][1])
    xs = _moe_dense(xs, sc2, sh2, gt2, w["ln"][1][1], w, 1)
    return (xs.reshape(bsz, 1, D), s_new[None], hm, k_new.reshape(bsz, 1, KV_HEADS, HD),
            v_new.reshape(bsz, 1, KV_HEADS, HD))


def kernel(x_prompt, x_sample, state_wkv, state_shift, cache_k, cache_v, c_prompt, c_sample, ada_w, ada_b, ln_g, ln_b, rwkv_mu, rwkv_w_rkv, rwkv_w0, rwkv_w1, rwkv_w2, rwkv_a0, rwkv_a1, rwkv_a2, rwkv_g1, rwkv_g2, rwkv_k_k, rwkv_k_a, rwkv_r_k, rwkv_lnx_g, rwkv_lnx_b, rwkv_w_o, w_q, w_kv, kv_ada_w, kv_ada_b, w_o_attn, router_w, router_b, moe_w_in, moe_w_out):
    p = {"ada_w": ada_w, "ada_b": ada_b, "ln_g": ln_g, "ln_b": ln_b, "rwkv_mu": rwkv_mu, "rwkv_w_rkv": rwkv_w_rkv,
         "rwkv_w0": rwkv_w0, "rwkv_w1": rwkv_w1, "rwkv_w2": rwkv_w2, "rwkv_a0": rwkv_a0, "rwkv_a1": rwkv_a1,
         "rwkv_a2": rwkv_a2, "rwkv_g1": rwkv_g1, "rwkv_g2": rwkv_g2, "rwkv_k_k": rwkv_k_k, "rwkv_k_a": rwkv_k_a,
         "rwkv_r_k": rwkv_r_k, "rwkv_lnx_g": rwkv_lnx_g, "rwkv_lnx_b": rwkv_lnx_b, "rwkv_w_o": rwkv_w_o,
         "w_q": w_q, "w_kv": w_kv, "kv_ada_w": kv_ada_w, "kv_ada_b": kv_ada_b, "w_o_attn": w_o_attn,
         "router_w": router_w, "router_b": router_b, "moe_w_in": moe_w_in, "moe_w_out": moe_w_out}
    w = _prep_weights(p)
    mods = _modulations(c_prompt, c_sample, p)
    bp, tp, _ = x_prompt.shape
    y_p, wkv_p, shift_p, k_p, v_p = _trunk_prompt(x_prompt, mods["prompt"], w)
    keep = min(PAST_LEN, tp)
    k_p = k_p[:, tp - keep:].reshape(bp, keep, KV_HEADS, HD)
    v_p = v_p[:, tp - keep:].reshape(bp, keep, KV_HEADS, HD)
    y_s, wkv_s, shift_s, k_s, v_s = _trunk_sample(x_sample, mods["sample"], state_wkv, state_shift, cache_k,
                                                  cache_v, w)
    return (y_p, y_s, wkv_p, shift_p, k_p, v_p, wkv_s, shift_s, k_s, v_s)
```

```python
import functools

import jax
import jax.numpy as jnp
from jax import lax
from jax.experimental import pallas as pl
from jax.experimental.pallas import tpu as pltpu

F32 = jnp.float32
BF16 = jnp.bfloat16
I32 = jnp.int32

D = 1024
HEADS = 16
HD = 64
LANES = 128
PAIRS = D // LANES
CHUNK = 64
WKV_GROUP = 8
N_EXPERTS = 32
EXPERTS_PER_GROUP = 8
N_EXPERT_GROUPS = 4
D_EXPERT = 512
MOE_BLK = 256
DMA_UNROLL = 8
KV_HEADS = 4
REP = 4
KVW = KV_HEADS * HD
DILATED_GROUPS = ((128, 1), (512, 4), (2048, 16))
ATT_WIN = 128
PAST_LEN = 2048
ROT_DIM = 16
ROPE_THETA = 500000.0
DEPTH = 2
DN_ALPHA = (2 * DEPTH) ** 0.25
LN_EPS = 1e-5
GN_EPS = 64e-5
VMEM_LIMIT = 56 * 1024 * 1024


def _cp(sem):
    return pltpu.CompilerParams(dimension_semantics=sem, vmem_limit_bytes=VMEM_LIMIT)


def _bdot(a, b):
    return jnp.dot(a.astype(BF16), b.astype(BF16), preferred_element_type=F32)


def _bdot_nt(a, b):
    return lax.dot_general(a.astype(BF16), b.astype(BF16), (((1,), (1,)), ((), ())),
                           preferred_element_type=F32)


def _bdot_tn(a, b):
    return lax.dot_general(a.astype(BF16), b.astype(BF16), (((0,), (0,)), ((), ())),
                           preferred_element_type=F32)


def _split_dot(x, m):
    hi = x.astype(BF16)
    lo = (x - hi.astype(F32)).astype(BF16)
    return (jnp.dot(hi, m, preferred_element_type=F32) + jnp.dot(lo, m, preferred_element_type=F32))


def _head_ones():
    r = lax.broadcasted_iota(I32, (LANES, LANES), 0) // HD
    c = lax.broadcasted_iota(I32, (LANES, LANES), 1) // HD
    return (r == c).astype(BF16)


def _segsum(x, ones_bd):
    cols = [_split_dot(x[:, c * LANES:(c + 1) * LANES], ones_bd) for c in range(x.shape[1] // LANES)]
    return cols[0] if len(cols) == 1 else jnp.concatenate(cols, axis=1)


def _layer_norm(x, g, b):
    mu = jnp.mean(x, axis=-1, keepdims=True)
    xc = x - mu
    var = jnp.mean(xc * xc, axis=-1, keepdims=True)
    return xc * lax.rsqrt(var + LN_EPS) * g + b


def _sigmoid(x):
    return 1.0 / (1.0 + jnp.exp(-x))


def _to_pairs(ref, val):
    for p in range(PAIRS):
        ref[p] = val[:, p * LANES:(p + 1) * LANES]


def _from_pairs(ref):
    return jnp.concatenate([ref[p] for p in range(PAIRS)], axis=1)


def _ada_kernel(c_ref, w_ref, b_ref, o_ref):
    c = c_ref[...]
    o_ref[...] = _bdot(c * _sigmoid(c), w_ref[...]) + b_ref[...]


def _ada_linear(c, w, b, tn=512):
    s, _, n = w.shape
    m = c.shape[0]
    return pl.pallas_call(
        _ada_kernel,
        grid=(s, n // tn),
        in_specs=[pl.BlockSpec((m, D), lambda i, j: (0, 0)),
                  pl.BlockSpec((None, D, tn), lambda i, j: (i, 0, j)),
                  pl.BlockSpec((None, 1, tn), lambda i, j: (i, 0, j))],
        out_specs=pl.BlockSpec((None, m, tn), lambda i, j: (i, 0, j)),
        out_shape=jax.ShapeDtypeStruct((s, m, n), F32),
        compiler_params=_cp(("arbitrary", "arbitrary")),
        name="ada_linear",
    )(c, w, b.reshape(s, 1, n))


def _rwkv_pre_kernel(x_ref, sc_ref, sh_ref, prev_ref, mu_ref, wrkv_ref, w1_ref, w2_ref, a1_ref, a2_ref,
                     g1_ref, g2_ref, vec_ref,
                     r_ref, lw_ref, k_ref, v_ref, a_ref, b_ref, g_ref, hm_ref, carry_ref, *, seq_mode, tm):
    x = x_ref[...]
    hm = x * (1.0 + sc_ref[...]) + sh_ref[...]
    if seq_mode:
        @pl.when(pl.program_id(1) == 0)
        def _():
            carry_ref[...] = prev_ref[...]
        row = lax.broadcasted_iota(I32, hm.shape, 0)
        hprev = jnp.where(row == 0, carry_ref[...], pltpu.roll(hm, 1, axis=0))
        carry_ref[...] = hm[tm - 1:tm, :]
        hm_ref[...] = hm[tm - 1:tm, :]
    else:
        hprev = prev_ref[...]
        hm_ref[...] = hm
    xx = hprev - hm

    def mix(i):
        return hm + xx * mu_ref[i:i + 1, :]

    w0, a0, k_k, k_a = (vec_ref[i:i + 1, :] for i in range(4))
    r = _bdot(mix(0), wrkv_ref[0])
    k = _bdot(mix(2), wrkv_ref[1])
    v = _bdot(mix(3), wrkv_ref[2])
    wl = w0 + _bdot(jnp.tanh(_bdot(mix(1), w1_ref[...])), w2_ref[...])
    z = -wl
    softplus = jnp.maximum(z, 0.0) + jnp.log1p(jnp.exp(-jnp.abs(z)))
    lw = -jnp.exp(-softplus - 0.5)
    a_lr = _sigmoid(a0 + _bdot(_bdot(mix(4), a1_ref[...]), a2_ref[...]))
    g = _bdot(_sigmoid(_bdot(mix(5), g1_ref[...])), g2_ref[...])
    kk = k * k_k
    kk = kk / jnp.maximum(jnp.sqrt(_segsum(kk * kk, _head_ones())), 1e-12)
    kmod = k * (1.0 + (a_lr - 1.0) * k_a)
    outs = ((r_ref, r), (lw_ref, lw), (k_ref, kmod), (v_ref, v), (a_ref, -kk), (b_ref, kk * a_lr), (g_ref, g))
    for ref, val in outs:
        if seq_mode:
            _to_pairs(ref, val)
        else:
            ref[...] = val


def _rwkv_pre(x, sc, sh, prev, w, *, seq_mode, tm):
    bsz, t, _ = x.shape
    grid = (bsz, t // tm)
    row = pl.BlockSpec((None, tm, D), lambda b, i: (b, i, 0))
    per_b = pl.BlockSpec((None, 1, D), lambda b, i: (b, 0, 0))
    mod = per_b if seq_mode else row

    def const(shape):
        return pl.BlockSpec(shape, lambda b, i: (0,) * len(shape))

    if seq_mode:
        out_big = pl.BlockSpec((None, PAIRS, tm, LANES), lambda b, i: (b, 0, i, 0))
        big_shape = jax.ShapeDtypeStruct((bsz, PAIRS, t, LANES), F32)
        hm_spec, hm_shape = per_b, jax.ShapeDtypeStruct((bsz, 1, D), F32)
    else:
        out_big, big_shape = row, jax.ShapeDtypeStruct((bsz, t, D), F32)
        hm_spec, hm_shape = row, jax.ShapeDtypeStruct((bsz, t, D), F32)
    return pl.pallas_call(
        functools.partial(_rwkv_pre_kernel, seq_mode=seq_mode, tm=tm),
        grid=grid,
        in_specs=[row, mod, mod, mod, const((6, D)), const((3, D, D)), const((D, LANES)), const((LANES, D)),
                  const((D, LANES)), const((LANES, D)), const((D, 2 * LANES)), const((2 * LANES, D)),
                  const((4, D))],
        out_specs=[out_big] * 7 + [hm_spec],
        out_shape=[big_shape] * 7 + [hm_shape],
        scratch_shapes=[pltpu.VMEM((1, D), F32)],
        compiler_params=_cp(("arbitrary", "arbitrary")),
        name="rwkv_pre",
    )(x, sc, sh, prev, w["mu"], w["w_rkv"], w["w1"], w["w2"], w["a1"], w["a2"], w["g1"], w["g2"], w["vec"])


def _wkv_chunk_kernel(r_ref, lw_ref, k_ref, v_ref, a_ref, b_ref, y_ref, zf_ref, z_ref, *, n_pairs, group):
    c = pl.program_id(0)

    @pl.when(c == 0)
    def _():
        z_ref[...] = jnp.zeros_like(z_ref)

    L = CHUNK
    row_l = lax.broadcasted_iota(I32, (L, L), 0)
    col_l = lax.broadcasted_iota(I32, (L, L), 1)
    tri_incl = (row_l >= col_l).astype(BF16)
    lane = lax.broadcasted_iota(I32, (L, LANES), 1)
    head0 = lane < HD
    row = lax.broadcasted_iota(I32, (LANES, LANES), 0)
    col = lax.broadcasted_iota(I32, (LANES, LANES), 1)
    strict = row > col
    incl = row >= col
    eye = (row == col).astype(F32)

    def expand(x):
        return jnp.concatenate([jnp.where(head0, x, 0.0), jnp.where(head0, 0.0, x)], axis=0)

    def group_body(gi, carry):
        ids = [gi * group + j for j in range(group)]
        bp = [(i // PAIRS, i % PAIRS) for i in ids]
        G = range(group)
        r = [r_ref[b_i, p_i] for b_i, p_i in bp]
        lw = [lw_ref[b_i, p_i] for b_i, p_i in bp]
        k = [k_ref[b_i, p_i] for b_i, p_i in bp]
        v = [v_ref[b_i, p_i] for b_i, p_i in bp]
        a = [a_ref[b_i, p_i] for b_i, p_i in bp]
        b = [b_ref[b_i, p_i] for b_i, p_i in bp]
        z0 = [z_ref[i] for i in ids]
        cum = [_split_dot_left(tri_incl, lw[j]) for j in G]
        cum_l = [cum[j][L - 1:L, :] for j in G]
        inv = [jnp.exp(-cum[j]) for j in G]
        tail = [jnp.exp(cum_l[j] - cum[j]) for j in G]
        a_e = [expand(a[j] * jnp.exp(cum[j] - lw[j])) for j in G]
        r_e = [expand(r[j] * jnp.exp(cum[j])) for j in G]
        b_e = [expand(b[j] * inv[j]) for j in G]
        k_e = [expand(k[j] * inv[j]) for j in G]
        bd_e = [expand(b[j] * tail[j]) for j in G]
        kd_e = [expand(k[j] * tail[j]) for j in G]
        v_e = [expand(v[j]) for j in G]
        gm = [_bdot_nt(jnp.concatenate([a_e[j], r_e[j]], axis=0), jnp.concatenate([b_e[j], k_e[j]], axis=0))
              for j in G]
        m_ab = [jnp.where(strict, gm[j][:LANES, :LANES], 0.0) for j in G]
        m_ak = [jnp.where(strict, gm[j][:LANES, LANES:], 0.0) for j in G]
        m_rb = [jnp.where(incl, gm[j][LANES:, :LANES], 0.0) for j in G]
        m_rk = [jnp.where(incl, gm[j][LANES:, LANES:], 0.0) for j in G]
        tinv = [eye + m_ab[j] for j in G]
        pw = m_ab
        for _ in range(5):
            pw = [_bdot(pw[j], pw[j]) for j in G]
            tinv = [tinv[j] + _bdot(pw[j], tinv[j]) for j in G]
        mv = [_bdot(jnp.concatenate([m_ak[j], m_rk[j]], axis=0), v_e[j]) for j in G]
        taw = [_bdot(tinv[j], jnp.concatenate([a_e[j], mv[j][:LANES]], axis=1)) for j in G]
        qy = [_bdot(m_rb[j], taw[j]) for j in G]
        pp = [_bdot_tn(bd_e[j], taw[j]) for j in G]
        kv = [_bdot_tn(kd_e[j], v_e[j]) for j in G]
        phi = [eye * jnp.exp(cum_l[j]) + pp[j][:, :LANES] for j in G]
        q = [r_e[j] + qy[j][:, :LANES] for j in G]
        zq = [_bdot(jnp.concatenate([phi[j], q[j]], axis=0), z0[j]) for j in G]
        for j in G:
            z_ref[ids[j]] = zq[j][:LANES] + pp[j][:, LANES:] + kv[j]
            y_e = zq[j][LANES:] + qy[j][:, LANES:] + mv[j][LANES:]
            y_ref[bp[j][0], bp[j][1]] = y_e[:L] + y_e[L:]
        return carry

    lax.fori_loop(0, n_pairs // group, group_body, 0)

    @pl.when(c == pl.num_programs(0) - 1)
    def _():
        zf_ref[...] = z_ref[...]


def _split_dot_left(m, x):
    hi = x.astype(BF16)
    lo = (x - hi.astype(F32)).astype(BF16)
    return (jnp.dot(m, hi, preferred_element_type=F32) + jnp.dot(m, lo, preferred_element_type=F32))


def _wkv_chunked(r, lw, k, v, a, b):
    bsz, _, t, _ = r.shape
    n_pairs = bsz * PAIRS
    blk = pl.BlockSpec((bsz, PAIRS, CHUNK, LANES), lambda c: (0, 0, c, 0))
    return pl.pallas_call(
        functools.partial(_wkv_chunk_kernel, n_pairs=n_pairs, group=WKV_GROUP),
        grid=(t // CHUNK,),
        in_specs=[blk] * 6,
        out_specs=[blk, pl.BlockSpec((n_pairs, LANES, LANES), lambda c: (0, 0, 0))],
        out_shape=[jax.ShapeDtypeStruct(r.shape, F32), jax.ShapeDtypeStruct((n_pairs, LANES, LANES), F32)],
        scratch_shapes=[pltpu.VMEM((n_pairs, LANES, LANES), F32)],
        compiler_params=_cp(("arbitrary",)),
        name="wkv_chunked",
    )(r, lw, k, v, a, b)


def _wkv_step_kernel(s_ref, r_ref, lw_ref, k_ref, v_ref, a_ref, b_ref, so_ref, y_ref):
    w = jnp.exp(lw_ref[...])
    a, b, k, r = a_ref[...], b_ref[...], k_ref[...], r_ref[...]

    def body(i, c):
        s = s_ref[i]
        sa = jnp.sum(s * a, axis=0, keepdims=True)
        s_new = s * w + sa * b + v_ref[pl.ds(i, 1), :] * k
        so_ref[i] = s_new
        y_ref[pl.ds(i, 1), :] = jnp.sum(s_new * r, axis=0, keepdims=True)
        return c

    lax.fori_loop(0, HD, body, 0, unroll=4)


def _wkv_step(state_t, r, lw, k, v, a, b):
    bsz = state_t.shape[-1]
    big = pl.BlockSpec((None, HD, HD, bsz), lambda h: (h, 0, 0, 0))
    vec = pl.BlockSpec((HD, bsz), lambda h: (h, 0))
    return pl.pallas_call(
        _wkv_step_kernel,
        grid=(HEADS,),
        in_specs=[big] + [vec] * 6,
        out_specs=[big, vec],
        out_shape=[jax.ShapeDtypeStruct(state_t.shape, F32), jax.ShapeDtypeStruct((D, bsz), F32)],
        compiler_params=_cp(("arbitrary",)),
        name="wkv_step",
    )(state_t, r, lw, k, v, a, b)


def _rwkv_post_kernel(y_ref, r_ref, k_ref, v_ref, g_ref, x_ref, gt_ref, vec_ref, wo_ref, o_ref, *, pairs):
    load = _from_pairs if pairs else (lambda ref: ref[...])
    y, r, k, v, g = (load(ref) for ref in (y_ref, r_ref, k_ref, v_ref, g_ref))
    r_k, lnx_g, lnx_b, ln_g, ln_b = (vec_ref[i:i + 1, :] for i in range(5))
    ones_bd = _head_ones()
    ym = _segsum(y, ones_bd) * (1.0 / HD)
    yc = y - ym
    yv = _segsum(yc * yc, ones_bd) * (1.0 / HD)
    yn = yc * lax.rsqrt(yv + GN_EPS) * lnx_g + lnx_b
    bonus = _segsum(r * k * r_k, ones_bd) * v
    mix = _bdot((yn + bonus) * g, wo_ref[...])
    o_ref[...] = _layer_norm(DN_ALPHA * x_ref[...] + gt_ref[...] * mix, ln_g, ln_b)


def _rwkv_post(y, r, k, v, g, x, gt, vec, w_o, *, pairs, tm):
    bsz, t, _ = x.shape
    row = pl.BlockSpec((None, tm, D), lambda b, i: (b, i, 0))
    big = pl.BlockSpec((None, PAIRS, tm, LANES), lambda b, i: (b, 0, i, 0)) if pairs else row
    mod = pl.BlockSpec((None, 1, D), lambda b, i: (b, 0, 0)) if pairs else row
    return pl.pallas_call(
        functools.partial(_rwkv_post_kernel, pairs=pairs),
        grid=(bsz, t // tm),
        in_specs=[big] * 5 + [row, mod, pl.BlockSpec((5, D), lambda b, i: (0, 0)),
                              pl.BlockSpec((D, D), lambda b, i: (0, 0))],
        out_specs=row,
        out_shape=jax.ShapeDtypeStruct(x.shape, F32),
        compiler_params=_cp(("arbitrary", "arbitrary")),
        name="rwkv_post",
    )(y, r, k, v, g, x, gt, vec, w_o)


def _top2(v):
    io = lax.broadcasted_iota(I32, v.shape, 0)
    m1 = jnp.max(v, axis=0, keepdims=True)
    i1 = jnp.min(jnp.where(v == m1, io, EXPERTS_PER_GROUP), axis=0, keepdims=True)
    v2 = jnp.where(io == i1, -jnp.inf, v)
    m2 = jnp.max(v2, axis=0, keepdims=True)
    i2 = jnp.min(jnp.where(v2 == m2, io, EXPERTS_PER_GROUP), axis=0, keepdims=True)
    return m1 + m2, i1, i2


def _moe_route_kernel(x_ref, sc_ref, sh_ref, rw_ref, rb_ref, h_ref, idx_ref, gate_ref, gmat_ref, cnt_ref,
                      carry_ref, *, tm, rows_out):
    @pl.when((pl.program_id(0) == 0) & (pl.program_id(1) == 0))
    def _():
        carry_ref[...] = jnp.zeros_like(carry_ref)

    h = x_ref[...] * (1.0 + sc_ref[...]) + sh_ref[...]
    h_ref[...] = h
    aff = _sigmoid(_bdot_nt(rw_ref[...], h))
    sel = aff + rb_ref[...]
    best = gi = i1 = i2 = None
    for g in range(N_EXPERT_GROUPS):
        sc, j1, j2 = _top2(sel[g * EXPERTS_PER_GROUP:(g + 1) * EXPERTS_PER_GROUP, :])
        if g == 0:
            best, gi, i1, i2 = sc, jnp.zeros_like(j1), j1, j2
        else:
            upd = sc > best
            best = jnp.where(upd, sc, best)
            gi = jnp.where(upd, g, gi)
            i1 = jnp.where(upd, j1, i1)
            i2 = jnp.where(upd, j2, i2)
    e0 = gi * EXPERTS_PER_GROUP + i1
    e1 = gi * EXPERTS_PER_GROUP + i2
    io = lax.broadcasted_iota(I32, (N_EXPERTS, tm), 0)
    oh0 = io == e0
    oh1 = io == e1
    a0 = jnp.sum(jnp.where(oh0, aff, 0.0), axis=0, keepdims=True)
    a1 = jnp.sum(jnp.where(oh1, aff, 0.0), axis=0, keepdims=True)
    den = a0 + a1
    g0 = a0 / den
    g1 = a1 / den
    oh = jnp.where(oh0 | oh1, 1.0, 0.0)
    tr = lax.broadcasted_iota(I32, (tm, tm), 0)
    tc = lax.broadcasted_iota(I32, (tm, tm), 1)
    before = jnp.dot(oh.astype(BF16), (tr < tc).astype(BF16), preferred_element_type=F32) + carry_ref[...]
    rank0 = jnp.sum(jnp.where(oh0, before, 0.0), axis=0, keepdims=True)
    rank1 = jnp.sum(jnp.where(oh1, before, 0.0), axis=0, keepdims=True)
    carry_ref[...] = carry_ref[...] + jnp.sum(oh, axis=1, keepdims=True)
    zi = jnp.zeros((4, tm), I32)
    idx_ref[...] = jnp.concatenate([e0, e1, rank0.astype(I32), rank1.astype(I32), zi], axis=0)
    gate_ref[...] = jnp.concatenate([g0, g1, jnp.zeros((6, tm), F32)], axis=0)
    gmat_ref[...] = jnp.where(oh0, g0, 0.0) + jnp.where(oh1, g1, 0.0)
    cnt_ref[...] = jnp.broadcast_to(carry_ref[...], cnt_ref.shape)


def _moe_route(x, sc, sh, router_w_t, router_b, *, per_b_mod, rows_out, tm):
    bsz, t, _ = x.shape
    n = bsz * t
    nt = t // tm
    row = pl.BlockSpec((None, tm, D), lambda b, i: (b, i, 0))
    mod = pl.BlockSpec((None, 1, D), lambda b, i: (b, 0, 0)) if per_b_mod else row
    if rows_out:
        h_spec = pl.BlockSpec((tm, D), lambda b, i: (b * nt + i, 0))
        h_shape = jax.ShapeDtypeStruct((n, D), F32)
    else:
        h_spec, h_shape = row, jax.ShapeDtypeStruct(x.shape, F32)
    tok = lambda rows: pl.BlockSpec((rows, tm), lambda b, i: (0, b * nt + i))
    return pl.pallas_call(
        functools.partial(_moe_route_kernel, tm=tm, rows_out=rows_out),
        grid=(bsz, nt),
        in_specs=[row, mod, mod, pl.BlockSpec((N_EXPERTS, D), lambda b, i: (0, 0)),
                  pl.BlockSpec((N_EXPERTS, 1), lambda b, i: (0, 0))],
        out_specs=[h_spec, tok(8), tok(8), tok(N_EXPERTS), pl.BlockSpec((N_EXPERTS, LANES), lambda b, i: (0, 0))],
        out_shape=[h_shape, jax.ShapeDtypeStruct((8, n), I32), jax.ShapeDtypeStruct((8, n), F32),
                   jax.ShapeDtypeStruct((N_EXPERTS, n), F32), jax.ShapeDtypeStruct((N_EXPERTS, LANES), F32)],
        scratch_shapes=[pltpu.VMEM((N_EXPERTS, 1), F32)],
        compiler_params=_cp(("arbitrary", "arbitrary")),
        name="moe_route",
    )(x, sc, sh, router_w_t, router_b)


def _row_copy(src, src_row, dst, dst_row, sem):
    return pltpu.make_async_copy(src.at[pl.ds(src_row, 1), :], dst.at[pl.ds(dst_row, 1), :], sem)


def _wait_rows(src, dst, rows, sem):
    pltpu.make_async_copy(src.at[pl.ds(0, rows), :], dst.at[pl.ds(0, rows), :], sem).wait()


def _moe_dispatch_kernel(d0_ref, d1_ref, zs_ref, h_ref, xs_ref, zbuf, sem, *, tm):
    base = pl.program_id(0) * tm

    @pl.when(pl.program_id(0) == 0)
    def _():
        zbuf[...] = jnp.zeros_like(zbuf)

        def block_copy(e):
            return pltpu.make_async_copy(zbuf, xs_ref.at[pl.ds(pl.multiple_of(zs_ref[e], 8), MOE_BLK), :],
                                         sem.at[2])

        fresh = [zs_ref[e] != zs_ref[e - 1] for e in range(1, zs_ref.shape[0])]
        block_copy(0).start()
        for e, f in enumerate(fresh, start=1):
            pl.when(f)(lambda e=e: block_copy(e).start())
        block_copy(0).wait()
        for e, f in enumerate(fresh, start=1):
            pl.when(f)(lambda e=e: block_copy(e).wait())

    def issue(i, c):
        _row_copy(h_ref, i, xs_ref, d0_ref[base + i], sem.at[0]).start(priority=0)
        _row_copy(h_ref, i, xs_ref, d1_ref[base + i], sem.at[1]).start(priority=1)
        return c

    lax.fori_loop(0, tm, issue, 0, unroll=DMA_UNROLL)
    _wait_rows(h_ref, xs_ref, tm, sem.at[0])
    _wait_rows(h_ref, xs_ref, tm, sem.at[1])


def _moe_dispatch(h_rows, dest0, dest1, zero_start, n_slots, tm):
    n = h_rows.shape[0]
    return pl.pallas_call(
        functools.partial(_moe_dispatch_kernel, tm=tm),
        grid_spec=pltpu.PrefetchScalarGridSpec(
            num_scalar_prefetch=3, grid=(n // tm,),
            in_specs=[pl.BlockSpec((tm, D), lambda i, d0, d1, zs: (i, 0))],
            out_specs=pl.BlockSpec(memory_space=pl.ANY),
            scratch_shapes=[pltpu.VMEM((MOE_BLK, D), F32), pltpu.SemaphoreType.DMA((3,))]),
        out_shape=jax.ShapeDtypeStruct((n_slots, D), F32),
        compiler_params=_cp(("arbitrary",)),
        name="moe_dispatch",
    )(dest0, dest1, zero_start, h_rows)


def _moe_expert_kernel(be_ref, used_ref, xs_ref, win_ref, wout_ref, ys_ref, win_bf, wout_bf):
    i = pl.program_id(0)
    fresh = (i == 0) | (be_ref[i] != be_ref[jnp.maximum(i - 1, 0)])

    @pl.when(fresh)
    def _():
        win_bf[...] = win_ref[...].astype(BF16)
        wout_bf[...] = wout_ref[...].astype(BF16)

    @pl.when(i < used_ref[0])
    def _():
        hmid = jnp.dot(xs_ref[...].astype(BF16), win_bf[...], preferred_element_type=F32)
        gt = hmid[:, :D_EXPERT]
        up = hmid[:, D_EXPERT:]
        act = gt * _sigmoid(gt) * up
        ys_ref[...] = jnp.dot(act.astype(BF16), wout_bf[...], preferred_element_type=F32)

    @pl.when(i >= used_ref[0])
    def _():
        ys_ref[...] = jnp.zeros_like(ys_ref)


def _moe_experts_sorted(xs, blk_exp, n_used, w_in, w_out, layer):
    n_slots = xs.shape[0]
    n_blocks = n_slots // MOE_BLK
    x_map = lambda i, be, u: (jnp.minimum(i, u[0] - 1), 0)
    return pl.pallas_call(
        _moe_expert_kernel,
        grid_spec=pltpu.PrefetchScalarGridSpec(
            num_scalar_prefetch=2, grid=(n_blocks,),
            in_specs=[pl.BlockSpec((MOE_BLK, D), x_map),
                      pl.BlockSpec((None, None, D, 2 * D_EXPERT), lambda i, be, u: (layer, be[i], 0, 0)),
                      pl.BlockSpec((None, None, D_EXPERT, D), lambda i, be, u: (layer, be[i], 0, 0))],
            out_specs=pl.BlockSpec((MOE_BLK, D), lambda i, be, u: (i, 0)),
            scratch_shapes=[pltpu.VMEM((D, 2 * D_EXPERT), BF16), pltpu.VMEM((D_EXPERT, D), BF16)]),
        out_shape=jax.ShapeDtypeStruct((n_slots, D), F32),
        compiler_params=_cp(("arbitrary",)),
        name="moe_experts_sorted",
    )(blk_exp, n_used, xs, w_in, w_out)


def _moe_combine_kernel(d0_ref, d1_ref, ys_ref, gate_ref, x_ref, gt_ref, vec_ref, o_ref, ya, yb, sem, *, tm):
    step = pl.program_id(0)
    slot = step % 2

    def gather(s, sl):
        def issue(i, c):
            _row_copy(ys_ref, d0_ref[s * tm + i], ya.at[sl], i, sem.at[sl, 0]).start(priority=0)
            _row_copy(ys_ref, d1_ref[s * tm + i], yb.at[sl], i, sem.at[sl, 1]).start(priority=1)
            return c
        lax.fori_loop(0, tm, issue, 0, unroll=DMA_UNROLL)

    @pl.when(step == 0)
    def _():
        gather(0, 0)

    @pl.when(step + 1 < pl.num_programs(0))
    def _():
        gather(step + 1, 1 - slot)

    _wait_rows(ys_ref, ya.at[slot], tm, sem.at[slot, 0])
    _wait_rows(ys_ref, yb.at[slot], tm, sem.at[slot, 1])
    ff = ya[slot] * gate_ref[:, 0:1] + yb[slot] * gate_ref[:, 1:2]
    o_ref[...] = _layer_norm(DN_ALPHA * x_ref[...] + gt_ref[...] * ff, vec_ref[0:1, :], vec_ref[1:2, :])


def _moe_combine(ys, dest0, dest1, gates, x, gt, vec, tm):
    bsz, t, _ = x.shape
    nt = t // tm
    return pl.pallas_call(
        functools.partial(_moe_combine_kernel, tm=tm),
        grid_spec=pltpu.PrefetchScalarGridSpec(
            num_scalar_prefetch=2, grid=(bsz * nt,),
            in_specs=[pl.BlockSpec(memory_space=pl.ANY),
                      pl.BlockSpec((tm, 2), lambda i, d0, d1: (i, 0)),
                      pl.BlockSpec((None, tm, D), lambda i, d0, d1: (i // nt, i % nt, 0)),
                      pl.BlockSpec((None, 1, D), lambda i, d0, d1: (i // nt, 0, 0)),
                      pl.BlockSpec((2, D), lambda i, d0, d1: (0, 0))],
            out_specs=pl.BlockSpec((None, tm, D), lambda i, d0, d1: (i // nt, i % nt, 0)),
            scratch_shapes=[pltpu.VMEM((2, tm, D), F32), pltpu.VMEM((2, tm, D), F32),
                            pltpu.SemaphoreType.DMA((2, 2))]),
        out_shape=jax.ShapeDtypeStruct(x.shape, F32),
        compiler_params=_cp(("arbitrary",)),
        name="moe_combine",
    )(dest0, dest1, ys, gates, x, gt, vec)


def _moe_sorted(x, sc, sh, gt, vec, w, layer, tm=256):
    bsz, t, _ = x.shape
    n = bsz * t
    n_blocks = (n * 2) // MOE_BLK + N_EXPERTS
    h_rows, idx, gate, _, cnt = _moe_route(x, sc, sh, w["router_w_t"], w["router_b"], per_b_mod=True,
                                           rows_out=True, tm=tm)
    counts = cnt[:, 0].astype(I32)
    padded = (counts + MOE_BLK - 1) // MOE_BLK * MOE_BLK
    pad_end = jnp.cumsum(padded)
    pad_start = pad_end - padded
    dest0 = pad_start[idx[0]] + idx[2]
    dest1 = pad_start[idx[1]] + idx[3]
    blk_start = jnp.arange(n_blocks, dtype=I32) * MOE_BLK
    blk_exp = jnp.minimum(jnp.sum((pad_end[None, :] <= blk_start[:, None]).astype(I32), axis=1), N_EXPERTS - 1)
    n_used = (pad_end[-1:] // MOE_BLK).astype(I32)
    tail = jnp.minimum(n_used[0] + jnp.arange(N_EXPERTS, dtype=I32), n_blocks - 1) * MOE_BLK
    zero_start = jnp.concatenate([jnp.maximum(pad_end - MOE_BLK, 0), tail])
    xs = _moe_dispatch(h_rows, dest0, dest1, zero_start, n_blocks * MOE_BLK, tm)
    ys = _moe_experts_sorted(xs, blk_exp, n_used, w["moe_w_in"], w["moe_w_out"], layer)
    return _moe_combine(ys, dest0, dest1, gate[:2].T, x, gt, vec, tm)


def _moe_dense_kernel(h_ref, gm_ref, win_ref, wout_ref, x_ref, gt_ref, vec_ref, o_ref, acc_ref):
    e = pl.program_id(0)

    @pl.when(e == 0)
    def _():
        acc_ref[...] = jnp.zeros_like(acc_ref)

    lane = lax.broadcasted_iota(I32, gm_ref.shape, 1)
    gcol = jnp.sum(jnp.where(lane == e, gm_ref[...], 0.0), axis=1, keepdims=True)
    hmid = _bdot(h_ref[...], win_ref[...])
    gt = hmid[:, :D_EXPERT]
    up = hmid[:, D_EXPERT:]
    y = _bdot(gt * _sigmoid(gt) * up, wout_ref[...])
    acc_ref[...] = acc_ref[...] + jnp.where(gcol != 0.0, gcol * y, 0.0)

    @pl.when(e == N_EXPERTS - 1)
    def _():
        o_ref[...] = _layer_norm(DN_ALPHA * x_ref[...] + gt_ref[...] * acc_ref[...], vec_ref[0:1, :],
                                 vec_ref[1:2, :])


def _moe_dense(x, sc, sh, gt, vec, w, layer):
    _, m, _ = x.shape
    h, _, _, gmat, _ = _moe_route(x, sc, sh, w["router_w_t"], w["router_b"], per_b_mod=False, rows_out=False,
                                  tm=m)
    full = pl.BlockSpec((m, D), lambda e: (0, 0))
    out = pl.pallas_call(
        _moe_dense_kernel,
        grid=(N_EXPERTS,),
        in_specs=[full, pl.BlockSpec((m, N_EXPERTS), lambda e: (0, 0)),
                  pl.BlockSpec((None, None, D, 2 * D_EXPERT), lambda e: (layer, e, 0, 0)),
                  pl.BlockSpec((None, None, D_EXPERT, D), lambda e: (layer, e, 0, 0)),
                  full, full, pl.BlockSpec((2, D), lambda e: (0, 0))],
        out_specs=full,
        out_shape=jax.ShapeDtypeStruct((m, D), F32),
        scratch_shapes=[pltpu.VMEM((m, D), F32)],
        compiler_params=_cp(("arbitrary",)),
        name="moe_dense",
    )(h[0], gmat.T, w["moe_w_in"], w["moe_w_out"], x[0], gt[0], vec)
    return out[None]


def _rotary(x, cos_t, sa_t, sb_t):
    reps = x.shape[1] // LANES
    tile = lambda t: t if reps == 1 else jnp.concatenate([t] * reps, axis=1)
    n = x.shape[1]
    half = ROT_DIM // 2
    return x * tile(cos_t) + pltpu.roll(x, n - half, axis=1) * tile(sa_t) + pltpu.roll(x, half, axis=1) * tile(sb_t)


def _class_rows(c, rows, dil):
    return pl.ds(c, rows, stride=dil) if dil > 1 else pl.ds(0, rows)


def _attn_pre_kernel(x_ref, sc_ref, sh_ref, ksc_ref, ksh_ref, rot_ref, wq_ref, wkv_ref, *refs, dils, tm):
    x = x_ref[...]
    cos_t, sa_t, sb_t = rot_ref[0], rot_ref[1], rot_ref[2]
    q = _rotary(_bdot(x * (1.0 + sc_ref[...]) + sh_ref[...], wq_ref[...]), cos_t, sa_t, sb_t) * HD ** -0.5
    kv = _bdot(x * (1.0 + ksc_ref[...]) + ksh_ref[...], wkv_ref[...])
    k = _rotary(kv[:, :KVW], cos_t, sa_t, sb_t)
    v = kv[:, KVW:]
    k_ref, v_ref = refs[0], refs[1]
    k_ref[...] = k
    v_ref[...] = v
    if dils is None:
        refs[2][...] = q
        return
    ng = len(dils)
    q_refs, kb_refs, vb_refs = refs[2:2 + ng], refs[2 + ng:2 + 2 * ng], refs[2 + 2 * ng:2 + 3 * ng]
    qs, kvs = refs[2 + 3 * ng:]
    for j in range(q.shape[1] // LANES):
        qs[j] = q[:, j * LANES:(j + 1) * LANES]
    for j in range(kv.shape[1] // LANES):
        kvs[j] = (k if j < KVW // LANES else v)[:, (j % (KVW // LANES)) * LANES:(j % (KVW // LANES) + 1) * LANES]

    def gather(ref, tiles, sl):
        return jnp.concatenate([ref[j, sl, :] for j in tiles], axis=1).astype(BF16)

    kt = KVW // LANES
    for g, dil in enumerate(dils):
        rows = tm // dil
        for c in range(dil):
            sl = _class_rows(c, rows, dil)
            q_refs[g][c] = gather(qs, range(g * PAIRS, (g + 1) * PAIRS), sl)
            kb_refs[g][c] = gather(kvs, range(kt), sl)
            vb_refs[g][c] = gather(kvs, range(kt, 2 * kt), sl)


def _attn_pre(x, sc, sh, ksc, ksh, rot, w_q, w_kv, *, per_b_mod, tm, dils=None):
    bsz, t, _ = x.shape
    row = lambda wd: pl.BlockSpec((None, tm, wd), lambda b, i: (b, i, 0))
    mod = pl.BlockSpec((None, 1, D), lambda b, i: (b, 0, 0)) if per_b_mod else row(D)
    nq = w_q.shape[1]
    out_specs = [row(KVW), row(KVW)]
    out_shape = [jax.ShapeDtypeStruct((bsz, t, KVW), F32)] * 2
    scratch = []
    if dils is None:
        out_specs.append(row(nq))
        out_shape.append(jax.ShapeDtypeStruct((bsz, t, nq), F32))
    else:
        for wd in (D, KVW, KVW):
            for dil in dils:
                out_specs.append(pl.BlockSpec((None, dil, tm // dil, wd), lambda b, i: (b, 0, i, 0)))
                out_shape.append(jax.ShapeDtypeStruct((bsz, dil, t // dil, wd), BF16))
        scratch = [pltpu.VMEM((nq // LANES, tm, LANES), F32), pltpu.VMEM((2 * KVW // LANES, tm, LANES), F32)]
    outs = pl.pallas_call(
        functools.partial(_attn_pre_kernel, dils=dils, tm=tm),
        grid=(bsz, t // tm),
        in_specs=[row(D), mod, mod, mod, mod, pl.BlockSpec((3, tm, LANES), lambda b, i: (0, i, 0)),
                  pl.BlockSpec((D, nq), lambda b, i: (0, 0)), pl.BlockSpec((D, 2 * KVW), lambda b, i: (0, 0))],
        out_specs=out_specs,
        out_shape=out_shape,
        scratch_shapes=scratch,
        compiler_params=_cp(("arbitrary", "arbitrary")),
        name="attn_pre",
    )(x, sc, sh, ksc, ksh, rot, w_q, w_kv)
    if dils is None:
        return outs
    ng = len(dils)
    return outs[0], outs[1], outs[2:2 + ng], outs[2 + ng:2 + 2 * ng], outs[2 + 2 * ng:]


def _attn_band_kernel(q_ref, ka_ref, kb_ref, va_ref, vb_ref, o_ref, lse_ref):
    i = pl.program_id(2)
    bq = ATT_WIN
    kcat = jnp.concatenate([ka_ref[...], kb_ref[...]], axis=0)
    vcat = jnp.concatenate([va_ref[...], vb_ref[...]], axis=0)
    qs = jnp.concatenate([q_ref[:, rep * KVW:(rep + 1) * KVW] for rep in range(REP)], axis=0)
    qrow = lax.broadcasted_iota(I32, (REP * bq, 2 * bq), 0) % bq
    kcol = lax.broadcasted_iota(I32, (REP * bq, 2 * bq), 1)
    valid = (kcol >= qrow) & (kcol <= qrow + bq) & ((i > 0) | (kcol >= bq))
    lane = lax.broadcasted_iota(I32, (REP * bq, KVW), 1) // HD
    kv_lane = lax.broadcasted_iota(I32, (2 * bq, KVW), 1) // HD
    lse_lane = lax.broadcasted_iota(I32, (bq, LANES), 1)
    lse = jnp.zeros((bq, LANES), F32)
    linv = jnp.ones((REP * bq, KVW), F32)
    ps = []
    for h in range(KV_HEADS):
        kh = jnp.where(kv_lane == h, kcat, jnp.zeros_like(kcat))
        s = lax.dot_general(qs, kh, (((1,), (1,)), ((), ())), preferred_element_type=F32)
        s = jnp.where(valid, s, -jnp.inf)
        m = jnp.max(s, axis=-1, keepdims=True)
        p = jnp.exp(s - m)
        l = jnp.sum(p, axis=-1, keepdims=True)
        ps.append(p.astype(BF16))
        linv = jnp.where(lane == h, 1.0 / l, linv)
        lse_h = m + jnp.log(l)
        for rep in range(REP):
            lse = jnp.where(lse_lane == rep * KV_HEADS + h, lse_h[rep * bq:(rep + 1) * bq], lse)
    vstack = jnp.concatenate([jnp.where(kv_lane == h, vcat, jnp.zeros_like(vcat)) for h in range(KV_HEADS)],
                             axis=0)
    o = jnp.dot(jnp.concatenate(ps, axis=1), vstack, preferred_element_type=F32) * linv
    for rep in range(REP):
        o_ref[:, rep * KVW:(rep + 1) * KVW] = o[rep * bq:(rep + 1) * bq]
    lse_ref[...] = lse


def _attn_band(q, kb, vb, dil):
    bsz, _, tc, _ = q.shape
    nb = tc // ATT_WIN
    blk = lambda wd: pl.BlockSpec((None, None, ATT_WIN, wd), lambda b, c, i: (b, c, i, 0))
    prev = pl.BlockSpec((None, None, ATT_WIN, KVW), lambda b, c, i: (b, c, jnp.maximum(i - 1, 0), 0))
    return pl.pallas_call(
        _attn_band_kernel,
        grid=(bsz, dil, nb),
        in_specs=[blk(D), prev, blk(KVW), prev, blk(KVW)],
        out_specs=[blk(D), blk(LANES)],
        out_shape=[jax.ShapeDtypeStruct((bsz, dil, tc, D), F32), jax.ShapeDtypeStruct((bsz, dil, tc, LANES), F32)],
        compiler_params=_cp(("arbitrary", "arbitrary", "arbitrary")),
        name=f"attn_band_d{dil}",
    )(q, kb, kb, vb, vb)


def _attn_step_kernel(q_ref, kn_ref, vn_ref, kt_ref, vt_ref, o_ref, lse_ref, *, wbuf):
    nq = len(DILATED_GROUPS) * REP
    row = lax.broadcasted_iota(I32, (nq, wbuf), 0)
    pos = lax.broadcasted_iota(I32, (nq, wbuf), 1)
    valid = None
    for g, (_, dil) in enumerate(DILATED_GROUPS):
        ok = (row // REP == g) & (pos >= wbuf - ATT_WIN * dil) & (pos % dil == 0)
        valid = ok if valid is None else valid | ok
    rnd = lambda t: t.astype(BF16).astype(F32)
    lse_lane = lax.broadcasted_iota(I32, (nq, LANES), 1)
    lse = jnp.zeros((nq, LANES), F32)
    for h in range(KV_HEADS):
        hs = slice(h * HD, (h + 1) * HD)
        qh = q_ref[:, hs]
        s = jnp.where(valid, _bdot(qh, kt_ref[h]), -1e30)
        s_n = jnp.sum(rnd(qh) * rnd(kn_ref[:, hs]), axis=-1, keepdims=True)
        m = jnp.maximum(jnp.max(s, axis=-1, keepdims=True), s_n)
        p = jnp.exp(s - m)
        p_n = jnp.exp(s_n - m)
        l = jnp.sum(p, axis=-1, keepdims=True) + p_n
        o_ref[:, hs] = (_bdot_nt(p, vt_ref[h]) + rnd(p_n) * rnd(vn_ref[:, hs])) / l
        lse = jnp.where(lse_lane == h, m + jnp.log(l), lse)
    lse_ref[...] = lse


def _attn_step(q, k_new, v_new, cache_k, cache_v):
    bsz = q.shape[0]
    wbuf = cache_k.shape[1]
    ng = len(DILATED_GROUPS)
    assert all(wbuf % dil == 0 and wbuf >= ATT_WIN * dil for _, dil in DILATED_GROUPS)
    nq = ng * REP
    cache_spec = pl.BlockSpec((None, KV_HEADS, HD, wbuf), lambda b: (b, 0, 0, 0))
    o, lse = pl.pallas_call(
        functools.partial(_attn_step_kernel, wbuf=wbuf),
        grid=(bsz,),
        in_specs=[pl.BlockSpec((None, nq, KVW), lambda b: (b, 0, 0)),
                  pl.BlockSpec((None, 1, KVW), lambda b: (b, 0, 0)),
                  pl.BlockSpec((None, 1, KVW), lambda b: (b, 0, 0)), cache_spec, cache_spec],
        out_specs=[pl.BlockSpec((None, nq, KVW), lambda b: (b, 0, 0)),
                   pl.BlockSpec((None, nq, LANES), lambda b: (b, 0, 0))],
        out_shape=[jax.ShapeDtypeStruct((bsz, nq, KVW), F32), jax.ShapeDtypeStruct((bsz, nq, LANES), F32)],
        compiler_params=_cp(("arbitrary",)),
        name="attn_step",
    )(q.reshape(bsz, nq, KVW), k_new.reshape(bsz, 1, KVW), v_new.reshape(bsz, 1, KVW),
      cache_k.transpose(0, 2, 3, 1), cache_v.transpose(0, 2, 3, 1))
    o = o.reshape(bsz, ng, D).transpose(1, 0, 2)
    lse = lse[:, :, :KV_HEADS].reshape(bsz, ng, REP * KV_HEADS).transpose(1, 0, 2)
    return o, jnp.pad(lse, ((0, 0), (0, 0), (0, LANES - REP * KV_HEADS)))


def _attn_post_kernel(o0_ref, o1_ref, o2_ref, l0_ref, l1_ref, l2_ref, x_ref, gt_ref, vec_ref, wo_ref, out_ref,
                      *scratch, dils, tm):
    o_refs = [o0_ref, o1_ref, o2_ref]
    l_refs = [l0_ref, l1_ref, l2_ref]
    if dils is not None:
        for g, dil in enumerate(dils):
            if dil == 1:
                o_refs[g], l_refs[g] = o_refs[g].at[0], l_refs[g].at[0]
                continue
            so, sl = scratch[2 * g], scratch[2 * g + 1]
            for c in range(dil):
                rows = _class_rows(c, tm // dil, dil)
                oc = o_refs[g][c]
                for j in range(PAIRS):
                    so[j, rows, :] = oc[:, j * LANES:(j + 1) * LANES]
                sl[rows, :] = l_refs[g][c]
            o_refs[g], l_refs[g] = so, sl
    o0, o1, o2 = (_from_pairs(r) if len(r.shape) == 3 else r[...] for r in o_refs)
    l0, l1, l2 = (r[...] for r in l_refs)
    m = jnp.maximum(jnp.maximum(l0, l1), l2)
    e0, e1, e2 = jnp.exp(l0 - m), jnp.exp(l1 - m), jnp.exp(l2 - m)
    den = e0 + e1 + e2
    r = lax.broadcasted_iota(I32, (LANES, D), 0)
    c = lax.broadcasted_iota(I32, (LANES, D), 1) // HD
    spread = (r == c).astype(BF16)
    mix = (_split_dot(e0 / den, spread) * o0 + _split_dot(e1 / den, spread) * o1
           + _split_dot(e2 / den, spread) * o2)
    out = _bdot(mix, wo_ref[...])
    out_ref[...] = _layer_norm(DN_ALPHA * x_ref[...] + gt_ref[...] * out, vec_ref[0:1, :], vec_ref[1:2, :])


def _attn_post(os, lses, x, gt, vec, w_o, *, per_b_mod, tm, dils=None):
    bsz, t, _ = x.shape
    row = lambda wd: pl.BlockSpec((None, tm, wd), lambda b, i: (b, i, 0))
    mod = pl.BlockSpec((None, 1, D), lambda b, i: (b, 0, 0)) if per_b_mod else row(D)
    if dils is None:
        o_specs, l_specs, scratch = [row(D)] * 3, [row(LANES)] * 3, []
    else:
        cls = lambda dil, wd: pl.BlockSpec((None, dil, tm // dil, wd), lambda b, i: (b, 0, i, 0))
        o_specs = [cls(dil, D) for dil in dils]
        l_specs = [cls(dil, LANES) for dil in dils]
        scratch = [pltpu.VMEM(shape, F32) for _ in dils for shape in ((PAIRS, tm, LANES), (tm, LANES))]
    return pl.pallas_call(
        functools.partial(_attn_post_kernel, dils=dils, tm=tm),
        grid=(bsz, t // tm),
        in_specs=o_specs + l_specs + [row(D), mod, pl.BlockSpec((2, D), lambda b, i: (0, 0)),
                                      pl.BlockSpec((D, D), lambda b, i: (0, 0))],
        out_specs=row(D),
        out_shape=jax.ShapeDtypeStruct(x.shape, F32),
        scratch_shapes=scratch,
        compiler_params=_cp(("arbitrary", "arbitrary")),
        name="attn_post",
    )(*os, *lses, x, gt, vec, w_o)


def _rot_tables(pos):
    half = ROT_DIM // 2
    inv = ROPE_THETA ** (-jnp.arange(half, dtype=F32) * 2.0 / ROT_DIM)
    ang = pos.astype(F32)[:, None] * inv[None, :]
    cos, sin = jnp.cos(ang), jnp.sin(ang)
    t = pos.shape[0]
    one = jnp.ones((t, HD - ROT_DIM), F32)
    zero = jnp.zeros((t, HD - ROT_DIM), F32)
    zh = jnp.zeros((t, half), F32)
    cos_t = jnp.concatenate([cos, cos, one], axis=1)
    sa_t = jnp.concatenate([-sin, zh, zero], axis=1)
    sb_t = jnp.concatenate([zh, sin, zero], axis=1)
    return jnp.stack([jnp.tile(x, (1, 2)) for x in (cos_t, sa_t, sb_t)])


def _prep_weights(p):
    pad_c = lambda m, n: jnp.pad(m, ((0, 0), (0, n - m.shape[1])))
    pad_r = lambda m, n: jnp.pad(m, ((0, n - m.shape[0]), (0, 0)))
    w = {}
    w["mu"] = p["rwkv_mu"][0]
    w["w_rkv"] = p["rwkv_w_rkv"][0].astype(BF16)
    w["w1"] = pad_c(p["rwkv_w1"][0], LANES).astype(BF16)
    w["w2"] = pad_r(p["rwkv_w2"][0], LANES).astype(BF16)
    w["a1"] = pad_c(p["rwkv_a1"][0], LANES).astype(BF16)
    w["a2"] = pad_r(p["rwkv_a2"][0], LANES).astype(BF16)
    w["g1"] = pad_c(p["rwkv_g1"][0], 2 * LANES).astype(BF16)
    w["g2"] = pad_r(p["rwkv_g2"][0], 2 * LANES).astype(BF16)
    w["vec"] = jnp.stack([p["rwkv_w0"][0], p["rwkv_a0"][0], p["rwkv_k_k"][0], p["rwkv_k_a"][0]])
    w["post_vec"] = jnp.stack([p["rwkv_r_k"][0].reshape(D), p["rwkv_lnx_g"][0], p["rwkv_lnx_b"][0],
                               p["ln_g"][0, 0], p["ln_b"][0, 0]])
    w["rwkv_w_o"] = p["rwkv_w_o"][0].astype(BF16)
    wq = p["w_q"][0].reshape(D, 3, KV_HEADS, REP, HD).transpose(0, 1, 3, 2, 4).reshape(D, 3 * D)
    w["w_q"] = wq.astype(BF16)
    w["w_kv"] = p["w_kv"].astype(BF16)
    wo = p["w_o_attn"][0].reshape(KV_HEADS, REP, HD, D).transpose(1, 0, 2, 3).reshape(D, D)
    w["w_o_attn"] = wo.astype(BF16)
    w["router_w_t"] = p["router_w"].T.astype(BF16)
    w["router_b"] = p["router_b"].reshape(N_EXPERTS, 1)
    w["moe_w_in"] = p["moe_w_in"]
    w["moe_w_out"] = p["moe_w_out"]
    w["ln"] = [[jnp.stack([p["ln_g"][l, i], p["ln_b"][l, i]]) for i in range(2)] for l in range(DEPTH)]
    return w


def _modulations(c_prompt, c_sample, p):
    nb = c_prompt.shape[0]
    c = jnp.concatenate([c_prompt, c_sample], axis=0)
    pad = (-c.shape[0]) % 8
    c = jnp.pad(c, ((0, pad), (0, 0)))
    m3 = _ada_linear(c, p["ada_w"].reshape(2 * DEPTH, D, 3 * D), p["ada_b"].reshape(2 * DEPTH, 3 * D))
    m2 = _ada_linear(c, p["kv_ada_w"][None], p["kv_ada_b"][None])[0]
    n_all = nb + c_sample.shape[0]

    def split(m, parts, lo, hi):
        return [m[lo:hi, i * D:(i + 1) * D] for i in range(parts)]

    out = {}
    for name, lo, hi in (("prompt", 0, nb), ("sample", nb, n_all)):
        out[name] = {"ada": [[split(m3[2 * l + i], 3, lo, hi) for i in range(2)] for l in range(DEPTH)],
                     "kv": split(m2, 2, lo, hi)}
    return out


def _trunk_prompt(x, mods, w):
    bsz, t, _ = x.shape
    per_b = lambda m: m[:, None, :]
    sh, sc, gt = (per_b(m) for m in mods["ada"][0][0])
    zeros = jnp.zeros((bsz, 1, D), F32)
    r, lw, k, v, a, b, g, last = _rwkv_pre(x, sc, sh, zeros, w, seq_mode=True, tm=256)
    y, zf = _wkv_chunked(r, lw, k, v, a, b)
    x = _rwkv_post(y, r, k, v, g, x, gt, w["post_vec"], w["rwkv_w_o"], pairs=True, tm=256)
    zf = zf.reshape(bsz, PAIRS, 2, HD, 2, HD)
    wkv = jnp.stack([zf[:, :, 0, :, 0, :], zf[:, :, 1, :, 1, :]], axis=2)
    wkv = wkv.reshape(bsz, HEADS, HD, HD).transpose(0, 1, 3, 2)
    sh2, sc2, gt2 = (per_b(m) for m in mods["ada"][0][1])
    x = _moe_sorted(x, sc2, sh2, gt2, w["ln"][0][1], w, 0)

    sh, sc, gt = (per_b(m) for m in mods["ada"][1][0])
    ksh, ksc = (per_b(m) for m in mods["kv"])
    rot = _rot_tables(jnp.arange(t, dtype=I32))
    dils = tuple(dil for _, dil in DILATED_GROUPS)
    k_new, v_new, qs, kbs, vbs = _attn_pre(x, sc, sh, ksc, ksh, rot, w["w_q"], w["w_kv"], per_b_mod=True,
                                           tm=256, dils=dils)
    os, lses = zip(*[_attn_band(qs[g], kbs[g], vbs[g], dil) for g, dil in enumerate(dils)])
    x = _attn_post(os, lses, x, gt, w["ln"][1][0], w["w_o_attn"], per_b_mod=True, tm=256, dils=dils)
    sh2, sc2, gt2 = (per_b(m) for m in mods["ada"][1][1])
    x = _moe_sorted(x, sc2, sh2, gt2, w["ln"][1][1], w, 1)
    return x, wkv[None], last.reshape(1, bsz, D), k_new, v_new


def _trunk_sample(x, mods, state_wkv, state_shift, cache_k, cache_v, w):
    bsz = x.shape[0]
    xs = x.reshape(1, bsz, D)
    row = lambda m: m[None]
    sh, sc, gt = (row(m) for m in mods["ada"][0][0])
    r, lw, k, v, a, b, g, hm = _rwkv_pre(xs, sc, sh, state_shift[0][None], w, seq_mode=False, tm=bsz)
    s_new, y = _wkv_step(state_wkv[0].transpose(1, 2, 3, 0), *(t[0].T for t in (r, lw, k, v, a, b)))
    s_new = s_new.transpose(3, 0, 1, 2)
    xs = _rwkv_post(y.T[None], r, k, v, g, xs, gt, w["post_vec"], w["rwkv_w_o"], pairs=False, tm=bsz)
    sh2, sc2, gt2 = (row(m) for m in mods["ada"][0][1])
    xs = _moe_dense(xs, sc2, sh2, gt2, w["ln"][0][1], w, 0)

    sh, sc, gt = (row(m) for m in mods["ada"][1][0])
    ksh, ksc = (row(m) for m in mods["kv"])
    rot = _rot_tables(jnp.full((bsz,), PAST_LEN, I32))
    k_new, v_new, q = _attn_pre(xs, sc, sh, ksc, ksh, rot, w["w_q"], w["w_kv"], per_b_mod=False, tm=bsz)
    o, lse = _attn_step(q[0], k_new[0], v_new[0], cache_k, cache_v)
    xs = _attn_post([o[i][None] for i in range(3)], [lse[i][None] for i in range(3)], xs, gt, w["ln"][1][0],
                    w["w_o_attn"], per_b_mod=False, tm=bsz)
    sh2, sc2, gt2 = (row(m) for m in mods["ada"][1][1])
    xs = _moe_dense(xs, sc2, sh2, gt2, w["ln"][1][1], w, 1)
    return (xs.reshape(bsz, 1, D), s_new[None], hm, k_new.reshape(bsz, 1, KV_HEADS, HD),
            v_new.reshape(bsz, 1, KV_HEADS, HD))


def kernel(x_prompt, x_sample, state_wkv, state_shift, cache_k, cache_v, c_prompt, c_sample, ada_w, ada_b, ln_g, ln_b, rwkv_mu, rwkv_w_rkv, rwkv_w0, rwkv_w1, rwkv_w2, rwkv_a0, rwkv_a1, rwkv_a2, rwkv_g1, rwkv_g2, rwkv_k_k, rwkv_k_a, rwkv_r_k, rwkv_lnx_g, rwkv_lnx_b, rwkv_w_o, w_q, w_kv, kv_ada_w, kv_ada_b, w_o_attn, router_w, router_b, moe_w_in, moe_w_out):
    p = {"ada_w": ada_w, "ada_b": ada_b, "ln_g": ln_g, "ln_b": ln_b, "rwkv_mu": rwkv_mu, "rwkv_w_rkv": rwkv_w_rkv,
         "rwkv_w0": rwkv_w0, "rwkv_w1": rwkv_w1, "rwkv_w2": rwkv_w2, "rwkv_a0": rwkv_a0, "rwkv_a1": rwkv_a1,
         "rwkv_a2": rwkv_a2, "rwkv_g1": rwkv_g1, "rwkv_g2": rwkv_g2, "rwkv_k_k": rwkv_k_k, "rwkv_k_a": rwkv_k_a,
         "rwkv_r_k": rwkv_r_k, "rwkv_lnx_g": rwkv_lnx_g, "rwkv_lnx_b": rwkv_lnx_b, "rwkv_w_o": rwkv_w_o,
         "w_q": w_q, "w_kv": w_kv, "kv_ada_w": kv_ada_w, "kv_ada_b": kv_ada_b, "w_o_attn": w_o_attn,
         "router_w": router_w, "router_b": router_b, "moe_w_in": moe_w_in, "moe_w_out": moe_w_out}
    w = _prep_weights(p)
    mods = _modulations(c_prompt, c_sample, p)
    bp, tp, _ = x_prompt.shape
    y_p, wkv_p, shift_p, k_p, v_p = _trunk_prompt(x_prompt, mods["prompt"], w)
    keep = min(PAST_LEN, tp)
    k_p = k_p[:, tp - keep:].reshape(bp, keep, KV_HEADS, HD)
    v_p = v_p[:, tp - keep:].reshape(bp, keep, KV_HEADS, HD)
    y_s, wkv_s, shift_s, k_s, v_s = _trunk_sample(x_sample, mods["sample"], state_wkv, state_shift, cache_k,
                                                  cache_v, w)
    return (y_p, y_s, wkv_p, shift_p, k_p, v_p, wkv_s, shift_s, k_s, v_s)
```

```python
import functools

import jax
import jax.numpy as jnp
from jax import lax
from jax.experimental import pallas as pl
from jax.experimental.pallas import tpu as pltpu

F32 = jnp.float32
BF16 = jnp.bfloat16
I32 = jnp.int32

D = 1024
HEADS = 16
HD = 64
LANES = 128
PAIRS = D // LANES
CHUNK = 64
WKV_GROUP = 8
N_EXPERTS = 32
EXPERTS_PER_GROUP = 8
N_EXPERT_GROUPS = 4
D_EXPERT = 512
MOE_BLK = 256
DMA_UNROLL = 8
KV_HEADS = 4
REP = 4
KVW = KV_HEADS * HD
DILATED_GROUPS = ((128, 1), (512, 4), (2048, 16))
ATT_WIN = 128
PAST_LEN = 2048
ROT_DIM = 16
ROPE_THETA = 500000.0
DEPTH = 2
DN_ALPHA = (2 * DEPTH) ** 0.25
LN_EPS = 1e-5
GN_EPS = 64e-5
VMEM_LIMIT = 56 * 1024 * 1024


def _cp(sem):
    return pltpu.CompilerParams(dimension_semantics=sem, vmem_limit_bytes=VMEM_LIMIT)


def _bdot(a, b):
    return jnp.dot(a.astype(BF16), b.astype(BF16), preferred_element_type=F32)


def _bdot_nt(a, b):
    return lax.dot_general(a.astype(BF16), b.astype(BF16), (((1,), (1,)), ((), ())),
                           preferred_element_type=F32)


def _bdot_tn(a, b):
    return lax.dot_general(a.astype(BF16), b.astype(BF16), (((0,), (0,)), ((), ())),
                           preferred_element_type=F32)


def _split_dot(x, m):
    hi = x.astype(BF16)
    lo = (x - hi.astype(F32)).astype(BF16)
    return (jnp.dot(hi, m, preferred_element_type=F32) + jnp.dot(lo, m, preferred_element_type=F32))


def _head_ones():
    r = lax.broadcasted_iota(I32, (LANES, LANES), 0) // HD
    c = lax.broadcasted_iota(I32, (LANES, LANES), 1) // HD
    return (r == c).astype(BF16)


def _segsum(x, ones_bd):
    cols = [_split_dot(x[:, c * LANES:(c + 1) * LANES], ones_bd) for c in range(x.shape[1] // LANES)]
    return cols[0] if len(cols) == 1 else jnp.concatenate(cols, axis=1)


def _layer_norm(x, g, b):
    mu = jnp.mean(x, axis=-1, keepdims=True)
    xc = x - mu
    var = jnp.mean(xc * xc, axis=-1, keepdims=True)
    return xc * lax.rsqrt(var + LN_EPS) * g + b


def _sigmoid(x):
    return 1.0 / (1.0 + jnp.exp(-x))


def _to_pairs(ref, val):
    for p in range(PAIRS):
        ref[p] = val[:, p * LANES:(p + 1) * LANES]


def _from_pairs(ref):
    return jnp.concatenate([ref[p] for p in range(PAIRS)], axis=1)


def _ada_kernel(c_ref, w_ref, b_ref, o_ref):
    c = c_ref[...]
    o_ref[...] = _bdot(c * _sigmoid(c), w_ref[...]) + b_ref[...]


def _ada_linear(c, w, b, tn=512):
    s, _, n = w.shape
    m = c.shape[0]
    return pl.pallas_call(
        _ada_kernel,
        grid=(s, n // tn),
        in_specs=[pl.BlockSpec((m, D), lambda i, j: (0, 0)),
                  pl.BlockSpec((None, D, tn), lambda i, j: (i, 0, j)),
                  pl.BlockSpec((None, 1, tn), lambda i, j: (i, 0, j))],
        out_specs=pl.BlockSpec((None, m, tn), lambda i, j: (i, 0, j)),
        out_shape=jax.ShapeDtypeStruct((s, m, n), F32),
        compiler_params=_cp(("arbitrary", "arbitrary")),
        name="ada_linear",
    )(c, w, b.reshape(s, 1, n))


def _rwkv_pre_kernel(x_ref, sc_ref, sh_ref, prev_ref, mu_ref, wrkv_ref, w1_ref, w2_ref, a1_ref, a2_ref,
                     g1_ref, g2_ref, vec_ref,
                     r_ref, lw_ref, k_ref, v_ref, a_ref, b_ref, g_ref, hm_ref, carry_ref, *, seq_mode, tm):
    x = x_ref[...]
    hm = x * (1.0 + sc_ref[...]) + sh_ref[...]
    if seq_mode:
        @pl.when(pl.program_id(1) == 0)
        def _():
            carry_ref[...] = prev_ref[...]
        row = lax.broadcasted_iota(I32, hm.shape, 0)
        hprev = jnp.where(row == 0, carry_ref[...], pltpu.roll(hm, 1, axis=0))
        carry_ref[...] = hm[tm - 1:tm, :]
        hm_ref[...] = hm[tm - 1:tm, :]
    else:
        hprev = prev_ref[...]
        hm_ref[...] = hm
    xx = hprev - hm

    def mix(i):
        return hm + xx * mu_ref[i:i + 1, :]

    w0, a0, k_k, k_a = (vec_ref[i:i + 1, :] for i in range(4))
    r = _bdot(mix(0), wrkv_ref[0])
    k = _bdot(mix(2), wrkv_ref[1])
    v = _bdot(mix(3), wrkv_ref[2])
    wl = w0 + _bdot(jnp.tanh(_bdot(mix(1), w1_ref[...])), w2_ref[...])
    z = -wl
    softplus = jnp.maximum(z, 0.0) + jnp.log1p(jnp.exp(-jnp.abs(z)))
    lw = -jnp.exp(-softplus - 0.5)
    a_lr = _sigmoid(a0 + _bdot(_bdot(mix(4), a1_ref[...]), a2_ref[...]))
    g = _bdot(_sigmoid(_bdot(mix(5), g1_ref[...])), g2_ref[...])
    kk = k * k_k
    kk = kk / jnp.maximum(jnp.sqrt(_segsum(kk * kk, _head_ones())), 1e-12)
    kmod = k * (1.0 + (a_lr - 1.0) * k_a)
    outs = ((r_ref, r), (lw_ref, lw), (k_ref, kmod), (v_ref, v), (a_ref, -kk), (b_ref, kk * a_lr), (g_ref, g))
    for ref, val in outs:
        if seq_mode:
            _to_pairs(ref, val)
        else:
            ref[...] = val


def _rwkv_pre(x, sc, sh, prev, w, *, seq_mode, tm):
    bsz, t, _ = x.shape
    grid = (bsz, t // tm)
    row = pl.BlockSpec((None, tm, D), lambda b, i: (b, i, 0))
    per_b = pl.BlockSpec((None, 1, D), lambda b, i: (b, 0, 0))
    mod = per_b if seq_mode else row

    def const(shape):
        return pl.BlockSpec(shape, lambda b, i: (0,) * len(shape))

    if seq_mode:
        out_big = pl.BlockSpec((None, PAIRS, tm, LANES), lambda b, i: (b, 0, i, 0))
        big_shape = jax.ShapeDtypeStruct((bsz, PAIRS, t, LANES), F32)
        hm_spec, hm_shape = per_b, jax.ShapeDtypeStruct((bsz, 1, D), F32)
    else:
        out_big, big_shape = row, jax.ShapeDtypeStruct((bsz, t, D), F32)
        hm_spec, hm_shape = row, jax.ShapeDtypeStruct((bsz, t, D), F32)
    return pl.pallas_call(
        functools.partial(_rwkv_pre_kernel, seq_mode=seq_mode, tm=tm),
        grid=grid,
        in_specs=[row, mod, mod, mod, const((6, D)), const((3, D, D)), const((D, LANES)), const((LANES, D)),
                  const((D, LANES)), const((LANES, D)), const((D, 2 * LANES)), const((2 * LANES, D)),
                  const((4, D))],
        out_specs=[out_big] * 7 + [hm_spec],
        out_shape=[big_shape] * 7 + [hm_shape],
        scratch_shapes=[pltpu.VMEM((1, D), F32)],
        compiler_params=_cp(("arbitrary", "arbitrary")),
        name="rwkv_pre",
    )(x, sc, sh, prev, w["mu"], w["w_rkv"], w["w1"], w["w2"], w["a1"], w["a2"], w["g1"], w["g2"], w["vec"])


def _wkv_chunk_kernel(r_ref, lw_ref, k_ref, v_ref, a_ref, b_ref, y_ref, zf_ref, z_ref, *, n_pairs, group):
    c = pl.program_id(0)

    @pl.when(c == 0)
    def _():
        z_ref[...] = jnp.zeros_like(z_ref)

    L = CHUNK
    row_l = lax.broadcasted_iota(I32, (L, L), 0)
    col_l = lax.broadcasted_iota(I32, (L, L), 1)
    tri_incl = (row_l >= col_l).astype(BF16)
    lane = lax.broadcasted_iota(I32, (L, LANES), 1)
    head0 = lane < HD
    row = lax.broadcasted_iota(I32, (LANES, LANES), 0)
    col = lax.broadcasted_iota(I32, (LANES, LANES), 1)
    strict = row > col
    incl = row >= col
    eye = (row == col).astype(F32)

    def expand(x):
        return jnp.concatenate([jnp.where(head0, x, 0.0), jnp.where(head0, 0.0, x)], axis=0)

    def group_body(gi, carry):
        ids = [gi * group + j for j in range(group)]
        bp = [(i // PAIRS, i % PAIRS) for i in ids]
        G = range(group)
        r = [r_ref[b_i, p_i] for b_i, p_i in bp]
        lw = [lw_ref[b_i, p_i] for b_i, p_i in bp]
        k = [k_ref[b_i, p_i] for b_i, p_i in bp]
        v = [v_ref[b_i, p_i] for b_i, p_i in bp]
        a = [a_ref[b_i, p_i] for b_i, p_i in bp]
        b = [b_ref[b_i, p_i] for b_i, p_i in bp]
        z0 = [z_ref[i] for i in ids]
        cum = [_split_dot_left(tri_incl, lw[j]) for j in G]
        cum_l = [cum[j][L - 1:L, :] for j in G]
        inv = [jnp.exp(-cum[j]) for j in G]
        tail = [jnp.exp(cum_l[j] - cum[j]) for j in G]
        a_e = [expand(a[j] * jnp.exp(cum[j] - lw[j])) for j in G]
        r_e = [expand(r[j] * jnp.exp(cum[j])) for j in G]
        b_e = [expand(b[j] * inv[j]) for j in G]
        k_e = [expand(k[j] * inv[j]) for j in G]
        bd_e = [expand(b[j] * tail[j]) for j in G]
        kd_e = [expand(k[j] * tail[j]) for j in G]
        v_e = [expand(v[j]) for j in G]
        gm = [_bdot_nt(jnp.concatenate([a_e[j], r_e[j]], axis=0), jnp.concatenate([b_e[j], k_e[j]], axis=0))
              for j in G]
        m_ab = [jnp.where(strict, gm[j][:LANES, :LANES], 0.0) for j in G]
        m_ak = [jnp.where(strict, gm[j][:LANES, LANES:], 0.0) for j in G]
        m_rb = [jnp.where(incl, gm[j][LANES:, :LANES], 0.0) for j in G]
        m_rk = [jnp.where(incl, gm[j][LANES:, LANES:], 0.0) for j in G]
        tinv = [eye + m_ab[j] for j in G]
        pw = [_bdot(m_ab[j], m_ab[j]) for j in G]
        for _ in range(4):
            both = [_bdot(pw[j], jnp.concatenate([pw[j], tinv[j]], axis=1)) for j in G]
            pw = [both[j][:, :LANES] for j in G]
            tinv = [tinv[j] + both[j][:, LANES:] for j in G]
        tinv = [tinv[j] + _bdot(pw[j], tinv[j]) for j in G]
        rhs = [_bdot(a_e[j], z0[j]) + _bdot(m_ak[j], v_e[j]) for j in G]
        u = [_bdot(tinv[j], rhs[j]) for j in G]
        zuv = [jnp.concatenate([z0[j], u[j], v_e[j]], axis=0) for j in G]
        decay = [eye * jnp.exp(cum_l[j]) for j in G]
        for j in G:
            z_ref[ids[j]] = _bdot_tn(jnp.concatenate([decay[j], bd_e[j], kd_e[j]], axis=0), zuv[j])
            y_e = _bdot(jnp.concatenate([r_e[j], m_rb[j], m_rk[j]], axis=1), zuv[j])
            y_ref[bp[j][0], bp[j][1]] = y_e[:L] + y_e[L:]
        return carry

    lax.fori_loop(0, n_pairs // group, group_body, 0)

    @pl.when(c == pl.num_programs(0) - 1)
    def _():
        zf_ref[...] = z_ref[...]


def _split_dot_left(m, x):
    hi = x.astype(BF16)
    lo = (x - hi.astype(F32)).astype(BF16)
    return (jnp.dot(m, hi, preferred_element_type=F32) + jnp.dot(m, lo, preferred_element_type=F32))


def _wkv_chunked(r, lw, k, v, a, b):
    bsz, _, t, _ = r.shape
    n_pairs = bsz * PAIRS
    blk = pl.BlockSpec((bsz, PAIRS, CHUNK, LANES), lambda c: (0, 0, c, 0))
    return pl.pallas_call(
        functools.partial(_wkv_chunk_kernel, n_pairs=n_pairs, group=WKV_GROUP),
        grid=(t // CHUNK,),
        in_specs=[blk] * 6,
        out_specs=[blk, pl.BlockSpec((n_pairs, LANES, LANES), lambda c: (0, 0, 0))],
        out_shape=[jax.ShapeDtypeStruct(r.shape, F32), jax.ShapeDtypeStruct((n_pairs, LANES, LANES), F32)],
        scratch_shapes=[pltpu.VMEM((n_pairs, LANES, LANES), F32)],
        compiler_params=_cp(("arbitrary",)),
        name="wkv_chunked",
    )(r, lw, k, v, a, b)


def _wkv_step_kernel(s_ref, r_ref, lw_ref, k_ref, v_ref, a_ref, b_ref, so_ref, y_ref):
    w = jnp.exp(lw_ref[...])
    a, b, k, r = a_ref[...], b_ref[...], k_ref[...], r_ref[...]

    def body(i, c):
        s = s_ref[i]
        sa = jnp.sum(s * a, axis=0, keepdims=True)
        s_new = s * w + sa * b + v_ref[pl.ds(i, 1), :] * k
        so_ref[i] = s_new
        y_ref[pl.ds(i, 1), :] = jnp.sum(s_new * r, axis=0, keepdims=True)
        return c

    lax.fori_loop(0, HD, body, 0, unroll=4)


def _wkv_step(state_t, r, lw, k, v, a, b):
    bsz = state_t.shape[-1]
    big = pl.BlockSpec((None, HD, HD, bsz), lambda h: (h, 0, 0, 0))
    vec = pl.BlockSpec((HD, bsz), lambda h: (h, 0))
    return pl.pallas_call(
        _wkv_step_kernel,
        grid=(HEADS,),
        in_specs=[big] + [vec] * 6,
        out_specs=[big, vec],
        out_shape=[jax.ShapeDtypeStruct(state_t.shape, F32), jax.ShapeDtypeStruct((D, bsz), F32)],
        compiler_params=_cp(("arbitrary",)),
        name="wkv_step",
    )(state_t, r, lw, k, v, a, b)


def _rwkv_post_kernel(y_ref, r_ref, k_ref, v_ref, g_ref, x_ref, gt_ref, vec_ref, wo_ref, o_ref, *, pairs):
    load = _from_pairs if pairs else (lambda ref: ref[...])
    y, r, k, v, g = (load(ref) for ref in (y_ref, r_ref, k_ref, v_ref, g_ref))
    r_k, lnx_g, lnx_b, ln_g, ln_b = (vec_ref[i:i + 1, :] for i in range(5))
    ones_bd = _head_ones()
    ym = _segsum(y, ones_bd) * (1.0 / HD)
    yc = y - ym
    yv = _segsum(yc * yc, ones_bd) * (1.0 / HD)
    yn = yc * lax.rsqrt(yv + GN_EPS) * lnx_g + lnx_b
    bonus = _segsum(r * k * r_k, ones_bd) * v
    mix = _bdot((yn + bonus) * g, wo_ref[...])
    o_ref[...] = _layer_norm(DN_ALPHA * x_ref[...] + gt_ref[...] * mix, ln_g, ln_b)


def _rwkv_post(y, r, k, v, g, x, gt, vec, w_o, *, pairs, tm):
    bsz, t, _ = x.shape
    row = pl.BlockSpec((None, tm, D), lambda b, i: (b, i, 0))
    big = pl.BlockSpec((None, PAIRS, tm, LANES), lambda b, i: (b, 0, i, 0)) if pairs else row
    mod = pl.BlockSpec((None, 1, D), lambda b, i: (b, 0, 0)) if pairs else row
    return pl.pallas_call(
        functools.partial(_rwkv_post_kernel, pairs=pairs),
        grid=(bsz, t // tm),
        in_specs=[big] * 5 + [row, mod, pl.BlockSpec((5, D), lambda b, i: (0, 0)),
                              pl.BlockSpec((D, D), lambda b, i: (0, 0))],
        out_specs=row,
        out_shape=jax.ShapeDtypeStruct(x.shape, F32),
        compiler_params=_cp(("arbitrary", "arbitrary")),
        name="rwkv_post",
    )(y, r, k, v, g, x, gt, vec, w_o)


def _top2(v):
    io = lax.broadcasted_iota(I32, v.shape, 0)
    m1 = jnp.max(v, axis=0, keepdims=True)
    i1 = jnp.min(jnp.where(v == m1, io, EXPERTS_PER_GROUP), axis=0, keepdims=True)
    v2 = jnp.where(io == i1, -jnp.inf, v)
    m2 = jnp.max(v2, axis=0, keepdims=True)
    i2 = jnp.min(jnp.where(v2 == m2, io, EXPERTS_PER_GROUP), axis=0, keepdims=True)
    return m1 + m2, i1, i2


def _moe_route_kernel(x_ref, sc_ref, sh_ref, rw_ref, rb_ref, h_ref, idx_ref, gate_ref, gmat_ref, cnt_ref,
                      carry_ref, *, tm, rows_out):
    @pl.when((pl.program_id(0) == 0) & (pl.program_id(1) == 0))
    def _():
        carry_ref[...] = jnp.zeros_like(carry_ref)

    h = x_ref[...] * (1.0 + sc_ref[...]) + sh_ref[...]
    h_ref[...] = h
    aff = _sigmoid(_bdot_nt(rw_ref[...], h))
    sel = aff + rb_ref[...]
    best = gi = i1 = i2 = None
    for g in range(N_EXPERT_GROUPS):
        sc, j1, j2 = _top2(sel[g * EXPERTS_PER_GROUP:(g + 1) * EXPERTS_PER_GROUP, :])
        if g == 0:
            best, gi, i1, i2 = sc, jnp.zeros_like(j1), j1, j2
        else:
            upd = sc > best
            best = jnp.where(upd, sc, best)
            gi = jnp.where(upd, g, gi)
            i1 = jnp.where(upd, j1, i1)
            i2 = jnp.where(upd, j2, i2)
    e0 = gi * EXPERTS_PER_GROUP + i1
    e1 = gi * EXPERTS_PER_GROUP + i2
    io = lax.broadcasted_iota(I32, (N_EXPERTS, tm), 0)
    oh0 = io == e0
    oh1 = io == e1
    a0 = jnp.sum(jnp.where(oh0, aff, 0.0), axis=0, keepdims=True)
    a1 = jnp.sum(jnp.where(oh1, aff, 0.0), axis=0, keepdims=True)
    den = a0 + a1
    g0 = a0 / den
    g1 = a1 / den
    oh = jnp.where(oh0 | oh1, 1.0, 0.0)
    tr = lax.broadcasted_iota(I32, (tm, tm), 0)
    tc = lax.broadcasted_iota(I32, (tm, tm), 1)
    before = jnp.dot(oh.astype(BF16), (tr < tc).astype(BF16), preferred_element_type=F32) + carry_ref[...]
    rank0 = jnp.sum(jnp.where(oh0, before, 0.0), axis=0, keepdims=True)
    rank1 = jnp.sum(jnp.where(oh1, before, 0.0), axis=0, keepdims=True)
    carry_ref[...] = carry_ref[...] + jnp.sum(oh, axis=1, keepdims=True)
    zi = jnp.zeros((4, tm), I32)
    idx_ref[...] = jnp.concatenate([e0, e1, rank0.astype(I32), rank1.astype(I32), zi], axis=0)
    gate_ref[...] = jnp.concatenate([g0, g1, jnp.zeros((6, tm), F32)], axis=0)
    gmat_ref[...] = jnp.where(oh0, g0, 0.0) + jnp.where(oh1, g1, 0.0)
    cnt_ref[...] = jnp.broadcast_to(carry_ref[...], cnt_ref.shape)


def _moe_route(x, sc, sh, router_w_t, router_b, *, per_b_mod, rows_out, tm):
    bsz, t, _ = x.shape
    n = bsz * t
    nt = t // tm
    row = pl.BlockSpec((None, tm, D), lambda b, i: (b, i, 0))
    mod = pl.BlockSpec((None, 1, D), lambda b, i: (b, 0, 0)) if per_b_mod else row
    if rows_out:
        h_spec = pl.BlockSpec((tm, D), lambda b, i: (b * nt + i, 0))
        h_shape = jax.ShapeDtypeStruct((n, D), F32)
    else:
        h_spec, h_shape = row, jax.ShapeDtypeStruct(x.shape, F32)
    tok = lambda rows: pl.BlockSpec((rows, tm), lambda b, i: (0, b * nt + i))
    return pl.pallas_call(
        functools.partial(_moe_route_kernel, tm=tm, rows_out=rows_out),
        grid=(bsz, nt),
        in_specs=[row, mod, mod, pl.BlockSpec((N_EXPERTS, D), lambda b, i: (0, 0)),
                  pl.BlockSpec((N_EXPERTS, 1), lambda b, i: (0, 0))],
        out_specs=[h_spec, tok(8), tok(8), tok(N_EXPERTS), pl.BlockSpec((N_EXPERTS, LANES), lambda b, i: (0, 0))],
        out_shape=[h_shape, jax.ShapeDtypeStruct((8, n), I32), jax.ShapeDtypeStruct((8, n), F32),
                   jax.ShapeDtypeStruct((N_EXPERTS, n), F32), jax.ShapeDtypeStruct((N_EXPERTS, LANES), F32)],
        scratch_shapes=[pltpu.VMEM((N_EXPERTS, 1), F32)],
        compiler_params=_cp(("arbitrary", "arbitrary")),
        name="moe_route",
    )(x, sc, sh, router_w_t, router_b)


def _row_copy(src, src_row, dst, dst_row, sem):
    return pltpu.make_async_copy(src.at[pl.ds(src_row, 1), :], dst.at[pl.ds(dst_row, 1), :], sem)


def _wait_rows(src, dst, rows, sem):
    pltpu.make_async_copy(src.at[pl.ds(0, rows), :], dst.at[pl.ds(0, rows), :], sem).wait()


def _slot(idx_ref, ps_ref, n, k, t):
    return ps_ref[idx_ref[k * n + t]] + idx_ref[(2 + k) * n + t]


def _moe_dispatch_kernel(idx_ref, ps_ref, zs_ref, h_ref, xs_ref, zbuf, sem, *, tm, n):
    base = pl.program_id(0) * tm

    @pl.when(pl.program_id(0) == 0)
    def _():
        zbuf[...] = jnp.zeros_like(zbuf)

        def block_copy(e):
            return pltpu.make_async_copy(zbuf, xs_ref.at[pl.ds(pl.multiple_of(zs_ref[e], 8), MOE_BLK), :],
                                         sem.at[2])

        fresh = [zs_ref[e] != zs_ref[e - 1] for e in range(1, zs_ref.shape[0])]
        block_copy(0).start()
        for e, f in enumerate(fresh, start=1):
            pl.when(f)(lambda e=e: block_copy(e).start())
        block_copy(0).wait()
        for e, f in enumerate(fresh, start=1):
            pl.when(f)(lambda e=e: block_copy(e).wait())

    def issue(i, c):
        _row_copy(h_ref, i, xs_ref, _slot(idx_ref, ps_ref, n, 0, base + i), sem.at[0]).start(priority=0)
        _row_copy(h_ref, i, xs_ref, _slot(idx_ref, ps_ref, n, 1, base + i), sem.at[1]).start(priority=1)
        return c

    lax.fori_loop(0, tm, issue, 0, unroll=DMA_UNROLL)
    _wait_rows(h_ref, xs_ref, tm, sem.at[0])
    _wait_rows(h_ref, xs_ref, tm, sem.at[1])


def _moe_dispatch(h_rows, idx_flat, pad_start, zero_start, n_slots, tm):
    n = h_rows.shape[0]
    return pl.pallas_call(
        functools.partial(_moe_dispatch_kernel, tm=tm, n=n),
        grid_spec=pltpu.PrefetchScalarGridSpec(
            num_scalar_prefetch=3, grid=(n // tm,),
            in_specs=[pl.BlockSpec((tm, D), lambda i, ix, ps, zs: (i, 0))],
            out_specs=pl.BlockSpec(memory_space=pl.ANY),
            scratch_shapes=[pltpu.VMEM((MOE_BLK, D), F32), pltpu.SemaphoreType.DMA((3,))]),
        out_shape=jax.ShapeDtypeStruct((n_slots, D), F32),
        compiler_params=_cp(("arbitrary",)),
        name="moe_dispatch",
    )(idx_flat, pad_start, zero_start, h_rows)


def _moe_expert_kernel(be_ref, used_ref, xs_ref, win_ref, wout_ref, ys_ref, win_bf, wout_bf):
    i = pl.program_id(0)
    fresh = (i == 0) | (be_ref[i] != be_ref[jnp.maximum(i - 1, 0)])

    @pl.when(fresh)
    def _():
        win_bf[...] = win_ref[...].astype(BF16)
        wout_bf[...] = wout_ref[...].astype(BF16)

    @pl.when(i < used_ref[0])
    def _():
        hmid = jnp.dot(xs_ref[...].astype(BF16), win_bf[...], preferred_element_type=F32)
        gt = hmid[:, :D_EXPERT]
        up = hmid[:, D_EXPERT:]
        act = gt * _sigmoid(gt) * up
        ys_ref[...] = jnp.dot(act.astype(BF16), wout_bf[...], preferred_element_type=F32)

    @pl.when(i >= used_ref[0])
    def _():
        ys_ref[...] = jnp.zeros_like(ys_ref)


def _moe_experts_sorted(xs, blk_exp, n_used, w_in, w_out, layer):
    n_slots = xs.shape[0]
    n_blocks = n_slots // MOE_BLK
    x_map = lambda i, be, u: (jnp.minimum(i, u[0] - 1), 0)
    return pl.pallas_call(
        _moe_expert_kernel,
        grid_spec=pltpu.PrefetchScalarGridSpec(
            num_scalar_prefetch=2, grid=(n_blocks,),
            in_specs=[pl.BlockSpec((MOE_BLK, D), x_map),
                      pl.BlockSpec((None, None, D, 2 * D_EXPERT), lambda i, be, u: (layer, be[i], 0, 0)),
                      pl.BlockSpec((None, None, D_EXPERT, D), lambda i, be, u: (layer, be[i], 0, 0))],
            out_specs=pl.BlockSpec((MOE_BLK, D), lambda i, be, u: (i, 0)),
            scratch_shapes=[pltpu.VMEM((D, 2 * D_EXPERT), BF16), pltpu.VMEM((D_EXPERT, D), BF16)]),
        out_shape=jax.ShapeDtypeStruct((n_slots, D), F32),
        compiler_params=_cp(("arbitrary",)),
        name="moe_experts_sorted",
    )(blk_exp, n_used, xs, w_in, w_out)


def _moe_combine_kernel(idx_ref, ps_ref, ys_ref, gate_ref, x_ref, gt_ref, vec_ref, o_ref, ya, yb, sem, *, tm, n):
    step = pl.program_id(0)
    slot = step % 2

    def gather(s, sl):
        def issue(i, c):
            t = s * tm + i
            _row_copy(ys_ref, _slot(idx_ref, ps_ref, n, 0, t), ya.at[sl], i, sem.at[sl, 0]).start(priority=0)
            _row_copy(ys_ref, _slot(idx_ref, ps_ref, n, 1, t), yb.at[sl], i, sem.at[sl, 1]).start(priority=1)
            return c
        lax.fori_loop(0, tm, issue, 0, unroll=DMA_UNROLL)

    @pl.when(step == 0)
    def _():
        gather(0, 0)

    @pl.when(step + 1 < pl.num_programs(0))
    def _():
        gather(step + 1, 1 - slot)

    _wait_rows(ys_ref, ya.at[slot], tm, sem.at[slot, 0])
    _wait_rows(ys_ref, yb.at[slot], tm, sem.at[slot, 1])
    ff = ya[slot] * gate_ref[:, 0:1] + yb[slot] * gate_ref[:, 1:2]
    o_ref[...] = _layer_norm(DN_ALPHA * x_ref[...] + gt_ref[...] * ff, vec_ref[0:1, :], vec_ref[1:2, :])


def _moe_combine(ys, idx_flat, pad_start, gates, x, gt, vec, tm):
    bsz, t, _ = x.shape
    nt = t // tm
    return pl.pallas_call(
        functools.partial(_moe_combine_kernel, tm=tm, n=bsz * t),
        grid_spec=pltpu.PrefetchScalarGridSpec(
            num_scalar_prefetch=2, grid=(bsz * nt,),
            in_specs=[pl.BlockSpec(memory_space=pl.ANY),
                      pl.BlockSpec((tm, 2), lambda i, d0, d1: (i, 0)),
                      pl.BlockSpec((None, tm, D), lambda i, d0, d1: (i // nt, i % nt, 0)),
                      pl.BlockSpec((None, 1, D), lambda i, d0, d1: (i // nt, 0, 0)),
                      pl.BlockSpec((2, D), lambda i, d0, d1: (0, 0))],
            out_specs=pl.BlockSpec((None, tm, D), lambda i, d0, d1: (i // nt, i % nt, 0)),
            scratch_shapes=[pltpu.VMEM((2, tm, D), F32), pltpu.VMEM((2, tm, D), F32),
                            pltpu.SemaphoreType.DMA((2, 2))]),
        out_shape=jax.ShapeDtypeStruct(x.shape, F32),
        compiler_params=_cp(("arbitrary",)),
        name="moe_combine",
    )(idx_flat, pad_start, ys, gates, x, gt, vec)


def _moe_sorted(x, sc, sh, gt, vec, w, layer, tm=256):
    bsz, t, _ = x.shape
    n = bsz * t
    n_blocks = (n * 2) // MOE_BLK + N_EXPERTS
    h_rows, idx, gate, _, cnt = _moe_route(x, sc, sh, w["router_w_t"], w["router_b"], per_b_mod=True,
                                           rows_out=True, tm=tm)
    counts = cnt[:, 0].astype(I32)
    padded = (counts + MOE_BLK - 1) // MOE_BLK * MOE_BLK
    pad_end = jnp.cumsum(padded)
    pad_start = pad_end - padded
    idx_flat = idx[:4].reshape(-1)
    blk_start = jnp.arange(n_blocks, dtype=I32) * MOE_BLK
    blk_exp = jnp.minimum(jnp.sum((pad_end[None, :] <= blk_start[:, None]).astype(I32), axis=1), N_EXPERTS - 1)
    n_used = (pad_end[-1:] // MOE_BLK).astype(I32)
    tail = jnp.minimum(n_used[0] + jnp.arange(N_EXPERTS, dtype=I32), n_blocks - 1) * MOE_BLK
    zero_start = jnp.concatenate([jnp.maximum(pad_end - MOE_BLK, 0), tail])
    xs = _moe_dispatch(h_rows, idx_flat, pad_start, zero_start, n_blocks * MOE_BLK, tm)
    ys = _moe_experts_sorted(xs, blk_exp, n_used, w["moe_w_in"], w["moe_w_out"], layer)
    return _moe_combine(ys, idx_flat, pad_start, gate[:2].T, x, gt, vec, tm)


def _moe_dense_kernel(h_ref, gm_ref, win_ref, wout_ref, x_ref, gt_ref, vec_ref, o_ref, acc_ref):
    e = pl.program_id(0)

    @pl.when(e == 0)
    def _():
        acc_ref[...] = jnp.zeros_like(acc_ref)

    lane = lax.broadcasted_iota(I32, gm_ref.shape, 1)
    gcol = jnp.sum(jnp.where(lane == e, gm_ref[...], 0.0), axis=1, keepdims=True)
    hmid = _bdot(h_ref[...], win_ref[...])
    gt = hmid[:, :D_EXPERT]
    up = hmid[:, D_EXPERT:]
    y = _bdot(gt * _sigmoid(gt) * up, wout_ref[...])
    acc_ref[...] = acc_ref[...] + jnp.where(gcol != 0.0, gcol * y, 0.0)

    @pl.when(e == N_EXPERTS - 1)
    def _():
        o_ref[...] = _layer_norm(DN_ALPHA * x_ref[...] + gt_ref[...] * acc_ref[...], vec_ref[0:1, :],
                                 vec_ref[1:2, :])


def _moe_dense(x, sc, sh, gt, vec, w, layer):
    _, m, _ = x.shape
    h, _, _, gmat, _ = _moe_route(x, sc, sh, w["router_w_t"], w["router_b"], per_b_mod=False, rows_out=False,
                                  tm=m)
    full = pl.BlockSpec((m, D), lambda e: (0, 0))
    out = pl.pallas_call(
        _moe_dense_kernel,
        grid=(N_EXPERTS,),
        in_specs=[full, pl.BlockSpec((m, N_EXPERTS), lambda e: (0, 0)),
                  pl.BlockSpec((None, None, D, 2 * D_EXPERT), lambda e: (layer, e, 0, 0)),
                  pl.BlockSpec((None, None, D_EXPERT, D), lambda e: (layer, e, 0, 0)),
                  full, full, pl.BlockSpec((2, D), lambda e: (0, 0))],
        out_specs=full,
        out_shape=jax.ShapeDtypeStruct((m, D), F32),
        scratch_shapes=[pltpu.VMEM((m, D), F32)],
        compiler_params=_cp(("arbitrary",)),
        name="moe_dense",
    )(h[0], gmat.T, w["moe_w_in"], w["moe_w_out"], x[0], gt[0], vec)
    return out[None]


def _rotary(x, cos_t, sa_t, sb_t):
    reps = x.shape[1] // LANES
    tile = lambda t: t if reps == 1 else jnp.concatenate([t] * reps, axis=1)
    n = x.shape[1]
    half = ROT_DIM // 2
    return x * tile(cos_t) + pltpu.roll(x, n - half, axis=1) * tile(sa_t) + pltpu.roll(x, half, axis=1) * tile(sb_t)


def _class_rows(c, rows, dil):
    return pl.ds(c, rows, stride=dil) if dil > 1 else pl.ds(0, rows)


def _attn_pre_kernel(x_ref, sc_ref, sh_ref, ksc_ref, ksh_ref, rot_ref, wq_ref, wkv_ref, *refs, dils, tm):
    x = x_ref[...]
    cos_t, sa_t, sb_t = rot_ref[0], rot_ref[1], rot_ref[2]
    q = _rotary(_bdot(x * (1.0 + sc_ref[...]) + sh_ref[...], wq_ref[...]), cos_t, sa_t, sb_t) * HD ** -0.5
    kv = _bdot(x * (1.0 + ksc_ref[...]) + ksh_ref[...], wkv_ref[...])
    k = _rotary(kv[:, :KVW], cos_t, sa_t, sb_t)
    v = kv[:, KVW:]
    k_ref, v_ref = refs[0], refs[1]
    k_ref[...] = k
    v_ref[...] = v
    if dils is None:
        refs[2][...] = q
        return
    ng = len(dils)
    q_refs, kb_refs, vb_refs = refs[2:2 + ng], refs[2 + ng:2 + 2 * ng], refs[2 + 2 * ng:2 + 3 * ng]
    qs, kvs = refs[2 + 3 * ng:]
    for j in range(q.shape[1] // LANES):
        qs[j] = q[:, j * LANES:(j + 1) * LANES]
    for j in range(kv.shape[1] // LANES):
        kvs[j] = (k if j < KVW // LANES else v)[:, (j % (KVW // LANES)) * LANES:(j % (KVW // LANES) + 1) * LANES]

    def gather(ref, tiles, sl):
        return jnp.concatenate([ref[j, sl, :] for j in tiles], axis=1).astype(BF16)

    kt = KVW // LANES
    for g, dil in enumerate(dils):
        rows = tm // dil
        for c in range(dil):
            sl = _class_rows(c, rows, dil)
            q_refs[g][c] = gather(qs, range(g * PAIRS, (g + 1) * PAIRS), sl)
            kb_refs[g][c] = gather(kvs, range(kt), sl)
            vb_refs[g][c] = gather(kvs, range(kt, 2 * kt), sl)


def _attn_pre(x, sc, sh, ksc, ksh, rot, w_q, w_kv, *, per_b_mod, tm, dils=None):
    bsz, t, _ = x.shape
    row = lambda wd: pl.BlockSpec((None, tm, wd), lambda b, i: (b, i, 0))
    mod = pl.BlockSpec((None, 1, D), lambda b, i: (b, 0, 0)) if per_b_mod else row(D)
    nq = w_q.shape[1]
    out_specs = [row(KVW), row(KVW)]
    out_shape = [jax.ShapeDtypeStruct((bsz, t, KVW), F32)] * 2
    scratch = []
    if dils is None:
        out_specs.append(row(nq))
        out_shape.append(jax.ShapeDtypeStruct((bsz, t, nq), F32))
    else:
        for wd in (D, KVW, KVW):
            for dil in dils:
                out_specs.append(pl.BlockSpec((None, dil, tm // dil, wd), lambda b, i: (b, 0, i, 0)))
                out_shape.append(jax.ShapeDtypeStruct((bsz, dil, t // dil, wd), BF16))
        scratch = [pltpu.VMEM((nq // LANES, tm, LANES), F32), pltpu.VMEM((2 * KVW // LANES, tm, LANES), F32)]
    outs = pl.pallas_call(
        functools.partial(_attn_pre_kernel, dils=dils, tm=tm),
        grid=(bsz, t // tm),
        in_specs=[row(D), mod, mod, mod, mod, pl.BlockSpec((3, tm, LANES), lambda b, i: (0, i, 0)),
                  pl.BlockSpec((D, nq), lambda b, i: (0, 0)), pl.BlockSpec((D, 2 * KVW), lambda b, i: (0, 0))],
        out_specs=out_specs,
        out_shape=out_shape,
        scratch_shapes=scratch,
        compiler_params=_cp(("arbitrary", "arbitrary")),
        name="attn_pre",
    )(x, sc, sh, ksc, ksh, rot, w_q, w_kv)
    if dils is None:
        return outs
    ng = len(dils)
    return outs[0], outs[1], outs[2:2 + ng], outs[2 + ng:2 + 2 * ng], outs[2 + 2 * ng:]


def _attn_band_kernel(q_ref, ka_ref, kb_ref, va_ref, vb_ref, o_ref, lse_ref):
    i = pl.program_id(2)
    bq = ATT_WIN
    kcat = jnp.concatenate([ka_ref[...], kb_ref[...]], axis=0)
    vcat = jnp.concatenate([va_ref[...], vb_ref[...]], axis=0)
    qs = jnp.concatenate([q_ref[:, rep * KVW:(rep + 1) * KVW] for rep in range(REP)], axis=0)
    qrow = lax.broadcasted_iota(I32, (REP * bq, 2 * bq), 0) % bq
    kcol = lax.broadcasted_iota(I32, (REP * bq, 2 * bq), 1)
    valid = (kcol >= qrow) & (kcol <= qrow + bq) & ((i > 0) | (kcol >= bq))
    lane = lax.broadcasted_iota(I32, (REP * bq, KVW), 1) // HD
    kv_lane = lax.broadcasted_iota(I32, (2 * bq, KVW), 1) // HD
    lse_lane = lax.broadcasted_iota(I32, (bq, LANES), 1)
    lse = jnp.zeros((bq, LANES), F32)
    linv = jnp.ones((REP * bq, KVW), F32)
    ps = []
    for h in range(KV_HEADS):
        kh = jnp.where(kv_lane == h, kcat, jnp.zeros_like(kcat))
        s = lax.dot_general(qs, kh, (((1,), (1,)), ((), ())), preferred_element_type=F32)
        s = jnp.where(valid, s, -jnp.inf)
        m = jnp.max(s, axis=-1, keepdims=True)
        p = jnp.exp(s - m)
        l = jnp.sum(p, axis=-1, keepdims=True)
        ps.append(p.astype(BF16))
        linv = jnp.where(lane == h, 1.0 / l, linv)
        lse_h = m + jnp.log(l)
        for rep in range(REP):
            lse = jnp.where(lse_lane == rep * KV_HEADS + h, lse_h[rep * bq:(rep + 1) * bq], lse)
    vstack = jnp.concatenate([jnp.where(kv_lane == h, vcat, jnp.zeros_like(vcat)) for h in range(KV_HEADS)],
                             axis=0)
    o = jnp.dot(jnp.concatenate(ps, axis=1), vstack, preferred_element_type=F32) * linv
    for rep in range(REP):
        o_ref[:, rep * KVW:(rep + 1) * KVW] = o[rep * bq:(rep + 1) * bq]
    lse_ref[...] = lse


def _attn_band(q, kb, vb, dil):
    bsz, _, tc, _ = q.shape
    nb = tc // ATT_WIN
    blk = lambda wd: pl.BlockSpec((None, None, ATT_WIN, wd), lambda b, c, i: (b, c, i, 0))
    prev = pl.BlockSpec((None, None, ATT_WIN, KVW), lambda b, c, i: (b, c, jnp.maximum(i - 1, 0), 0))
    return pl.pallas_call(
        _attn_band_kernel,
        grid=(bsz, dil, nb),
        in_specs=[blk(D), prev, blk(KVW), prev, blk(KVW)],
        out_specs=[blk(D), blk(LANES)],
        out_shape=[jax.ShapeDtypeStruct((bsz, dil, tc, D), F32), jax.ShapeDtypeStruct((bsz, dil, tc, LANES), F32)],
        compiler_params=_cp(("arbitrary", "arbitrary", "arbitrary")),
        name=f"attn_band_d{dil}",
    )(q, kb, kb, vb, vb)


def _attn_step_kernel(q_ref, kn_ref, vn_ref, kt_ref, vt_ref, o_ref, lse_ref, *, wbuf):
    nq = len(DILATED_GROUPS) * REP
    row = lax.broadcasted_iota(I32, (nq, wbuf), 0)
    pos = lax.broadcasted_iota(I32, (nq, wbuf), 1)
    valid = None
    for g, (_, dil) in enumerate(DILATED_GROUPS):
        ok = (row // REP == g) & (pos >= wbuf - ATT_WIN * dil) & (pos % dil == 0)
        valid = ok if valid is None else valid | ok
    rnd = lambda t: t.astype(BF16).astype(F32)
    lse_lane = lax.broadcasted_iota(I32, (nq, LANES), 1)
    lse = jnp.zeros((nq, LANES), F32)
    for h in range(KV_HEADS):
        hs = slice(h * HD, (h + 1) * HD)
        qh = q_ref[:, hs]
        s = jnp.where(valid, _bdot(qh, kt_ref[h]), -1e30)
        s_n = jnp.sum(rnd(qh) * rnd(kn_ref[:, hs]), axis=-1, keepdims=True)
        m = jnp.maximum(jnp.max(s, axis=-1, keepdims=True), s_n)
        p = jnp.exp(s - m)
        p_n = jnp.exp(s_n - m)
        l = jnp.sum(p, axis=-1, keepdims=True) + p_n
        o_ref[:, hs] = (_bdot_nt(p, vt_ref[h]) + rnd(p_n) * rnd(vn_ref[:, hs])) / l
        lse = jnp.where(lse_lane == h, m + jnp.log(l), lse)
    lse_ref[...] = lse


def _attn_step(q, k_new, v_new, cache_k, cache_v):
    bsz = q.shape[0]
    wbuf = cache_k.shape[1]
    ng = len(DILATED_GROUPS)
    assert all(wbuf % dil == 0 and wbuf >= ATT_WIN * dil for _, dil in DILATED_GROUPS)
    nq = ng * REP
    cache_spec = pl.BlockSpec((None, KV_HEADS, HD, wbuf), lambda b: (b, 0, 0, 0))
    o, lse = pl.pallas_call(
        functools.partial(_attn_step_kernel, wbuf=wbuf),
        grid=(bsz,),
        in_specs=[pl.BlockSpec((None, nq, KVW), lambda b: (b, 0, 0)),
                  pl.BlockSpec((None, 1, KVW), lambda b: (b, 0, 0)),
                  pl.BlockSpec((None, 1, KVW), lambda b: (b, 0, 0)), cache_spec, cache_spec],
        out_specs=[pl.BlockSpec((None, nq, KVW), lambda b: (b, 0, 0)),
                   pl.BlockSpec((None, nq, LANES), lambda b: (b, 0, 0))],
        out_shape=[jax.ShapeDtypeStruct((bsz, nq, KVW), F32), jax.ShapeDtypeStruct((bsz, nq, LANES), F32)],
        compiler_params=_cp(("arbitrary",)),
        name="attn_step",
    )(q.reshape(bsz, nq, KVW), k_new.reshape(bsz, 1, KVW), v_new.reshape(bsz, 1, KVW),
      cache_k.transpose(0, 2, 3, 1), cache_v.transpose(0, 2, 3, 1))
    o = o.reshape(bsz, ng, D).transpose(1, 0, 2)
    lse = lse[:, :, :KV_HEADS].reshape(bsz, ng, REP * KV_HEADS).transpose(1, 0, 2)
    return o, jnp.pad(lse, ((0, 0), (0, 0), (0, LANES - REP * KV_HEADS)))


def _attn_post_kernel(o0_ref, o1_ref, o2_ref, l0_ref, l1_ref, l2_ref, x_ref, gt_ref, vec_ref, wo_ref, out_ref,
                      *scratch, dils, tm):
    o_refs = [o0_ref, o1_ref, o2_ref]
    l_refs = [l0_ref, l1_ref, l2_ref]
    if dils is not None:
        for g, dil in enumerate(dils):
            if dil == 1:
                o_refs[g], l_refs[g] = o_refs[g].at[0], l_refs[g].at[0]
                continue
            so, sl = scratch[2 * g], scratch[2 * g + 1]
            for c in range(dil):
                rows = _class_rows(c, tm // dil, dil)
                oc = o_refs[g][c]
                for j in range(PAIRS):
                    so[j, rows, :] = oc[:, j * LANES:(j + 1) * LANES]
                sl[rows, :] = l_refs[g][c]
            o_refs[g], l_refs[g] = so, sl
    o0, o1, o2 = (_from_pairs(r) if len(r.shape) == 3 else r[...] for r in o_refs)
    l0, l1, l2 = (r[...] for r in l_refs)
    m = jnp.maximum(jnp.maximum(l0, l1), l2)
    e0, e1, e2 = jnp.exp(l0 - m), jnp.exp(l1 - m), jnp.exp(l2 - m)
    den = e0 + e1 + e2
    r = lax.broadcasted_iota(I32, (LANES, D), 0)
    c = lax.broadcasted_iota(I32, (LANES, D), 1) // HD
    spread = (r == c).astype(BF16)
    mix = (_split_dot(e0 / den, spread) * o0 + _split_dot(e1 / den, spread) * o1
           + _split_dot(e2 / den, spread) * o2)
    out = _bdot(mix, wo_ref[...])
    out_ref[...] = _layer_norm(DN_ALPHA * x_ref[...] + gt_ref[...] * out, vec_ref[0:1, :], vec_ref[1:2, :])


def _attn_post(os, lses, x, gt, vec, w_o, *, per_b_mod, tm, dils=None):
    bsz, t, _ = x.shape
    row = lambda wd: pl.BlockSpec((None, tm, wd), lambda b, i: (b, i, 0))
    mod = pl.BlockSpec((None, 1, D), lambda b, i: (b, 0, 0)) if per_b_mod else row(D)
    if dils is None:
        o_specs, l_specs, scratch = [row(D)] * 3, [row(LANES)] * 3, []
    else:
        cls = lambda dil, wd: pl.BlockSpec((None, dil, tm // dil, wd), lambda b, i: (b, 0, i, 0))
        o_specs = [cls(dil, D) for dil in dils]
        l_specs = [cls(dil, LANES) for dil in dils]
        scratch = [pltpu.VMEM(shape, F32) for _ in dils for shape in ((PAIRS, tm, LANES), (tm, LANES))]
    return pl.pallas_call(
        functools.partial(_attn_post_kernel, dils=dils, tm=tm),
        grid=(bsz, t // tm),
        in_specs=o_specs + l_specs + [row(D), mod, pl.BlockSpec((2, D), lambda b, i: (0, 0)),
                                      pl.BlockSpec((D, D), lambda b, i: (0, 0))],
        out_specs=row(D),
        out_shape=jax.ShapeDtypeStruct(x.shape, F32),
        scratch_shapes=scratch,
        compiler_params=_cp(("arbitrary", "arbitrary")),
        name="attn_post",
    )(*os, *lses, x, gt, vec, w_o)


def _rot_tables(pos):
    half = ROT_DIM // 2
    inv = ROPE_THETA ** (-jnp.arange(half, dtype=F32) * 2.0 / ROT_DIM)
    ang = pos.astype(F32)[:, None] * inv[None, :]
    cos, sin = jnp.cos(ang), jnp.sin(ang)
    t = pos.shape[0]
    one = jnp.ones((t, HD - ROT_DIM), F32)
    zero = jnp.zeros((t, HD - ROT_DIM), F32)
    zh = jnp.zeros((t, half), F32)
    cos_t = jnp.concatenate([cos, cos, one], axis=1)
    sa_t = jnp.concatenate([-sin, zh, zero], axis=1)
    sb_t = jnp.concatenate([zh, sin, zero], axis=1)
    return jnp.stack([jnp.tile(x, (1, 2)) for x in (cos_t, sa_t, sb_t)])


def _prep_weights(p):
    pad_c = lambda m, n: jnp.pad(m, ((0, 0), (0, n - m.shape[1])))
    pad_r = lambda m, n: jnp.pad(m, ((0, n - m.shape[0]), (0, 0)))
    w = {}
    w["mu"] = p["rwkv_mu"][0]
    w["w_rkv"] = p["rwkv_w_rkv"][0].astype(BF16)
    w["w1"] = pad_c(p["rwkv_w1"][0], LANES).astype(BF16)
    w["w2"] = pad_r(p["rwkv_w2"][0], LANES).astype(BF16)
    w["a1"] = pad_c(p["rwkv_a1"][0], LANES).astype(BF16)
    w["a2"] = pad_r(p["rwkv_a2"][0], LANES).astype(BF16)
    w["g1"] = pad_c(p["rwkv_g1"][0], 2 * LANES).astype(BF16)
    w["g2"] = pad_r(p["rwkv_g2"][0], 2 * LANES).astype(BF16)
    w["vec"] = jnp.stack([p["rwkv_w0"][0], p["rwkv_a0"][0], p["rwkv_k_k"][0], p["rwkv_k_a"][0]])
    w["post_vec"] = jnp.stack([p["rwkv_r_k"][0].reshape(D), p["rwkv_lnx_g"][0], p["rwkv_lnx_b"][0],
                               p["ln_g"][0, 0], p["ln_b"][0, 0]])
    w["rwkv_w_o"] = p["rwkv_w_o"][0].astype(BF16)
    wq = p["w_q"][0].reshape(D, 3, KV_HEADS, REP, HD).transpose(0, 1, 3, 2, 4).reshape(D, 3 * D)
    w["w_q"] = wq.astype(BF16)
    w["w_kv"] = p["w_kv"].astype(BF16)
    wo = p["w_o_attn"][0].reshape(KV_HEADS, REP, HD, D).transpose(1, 0, 2, 3).reshape(D, D)
    w["w_o_attn"] = wo.astype(BF16)
    w["router_w_t"] = p["router_w"].T.astype(BF16)
    w["router_b"] = p["router_b"].reshape(N_EXPERTS, 1)
    w["moe_w_in"] = p["moe_w_in"]
    w["moe_w_out"] = p["moe_w_out"]
    w["ln"] = [[jnp.stack([p["ln_g"][l, i], p["ln_b"][l, i]]) for i in range(2)] for l in range(DEPTH)]
    return w


def _modulations(c_prompt, c_sample, p):
    nb = c_prompt.shape[0]
    c = jnp.concatenate([c_prompt, c_sample], axis=0)
    pad = (-c.shape[0]) % 8
    c = jnp.pad(c, ((0, pad), (0, 0)))
    m3 = _ada_linear(c, p["ada_w"].reshape(2 * DEPTH, D, 3 * D), p["ada_b"].reshape(2 * DEPTH, 3 * D))
    m2 = _ada_linear(c, p["kv_ada_w"][None], p["kv_ada_b"][None])[0]
    n_all = nb + c_sample.shape[0]

    def split(m, parts, lo, hi):
        return [m[lo:hi, i * D:(i + 1) * D] for i in range(parts)]

    out = {}
    for name, lo, hi in (("prompt", 0, nb), ("sample", nb, n_all)):
        out[name] = {"ada": [[split(m3[2 * l + i], 3, lo, hi) for i in range(2)] for l in range(DEPTH)],
                     "kv": split(m2, 2, lo, hi)}
    return out


def _trunk_prompt(x, mods, w):
    bsz, t, _ = x.shape
    per_b = lambda m: m[:, None, :]
    sh, sc, gt = (per_b(m) for m in mods["ada"][0][0])
    zeros = jnp.zeros((bsz, 1, D), F32)
    r, lw, k, v, a, b, g, last = _rwkv_pre(x, sc, sh, zeros, w, seq_mode=True, tm=256)
    y, zf = _wkv_chunked(r, lw, k, v, a, b)
    x = _rwkv_post(y, r, k, v, g, x, gt, w["post_vec"], w["rwkv_w_o"], pairs=True, tm=256)
    zf = zf.reshape(bsz, PAIRS, 2, HD, 2, HD)
    wkv = jnp.stack([zf[:, :, 0, :, 0, :], zf[:, :, 1, :, 1, :]], axis=2)
    wkv = wkv.reshape(bsz, HEADS, HD, HD).transpose(0, 1, 3, 2)
    sh2, sc2, gt2 = (per_b(m) for m in mods["ada"][0][1])
    x = _moe_sorted(x, sc2, sh2, gt2, w["ln"][0][1], w, 0)

    sh, sc, gt = (per_b(m) for m in mods["ada"][1][0])
    ksh, ksc = (per_b(m) for m in mods["kv"])
    rot = _rot_tables(jnp.arange(t, dtype=I32))
    dils = tuple(dil for _, dil in DILATED_GROUPS)
    k_new, v_new, qs, kbs, vbs = _attn_pre(x, sc, sh, ksc, ksh, rot, w["w_q"], w["w_kv"], per_b_mod=True,
                                           tm=256, dils=dils)
    os, lses = zip(*[_attn_band(qs[g], kbs[g], vbs[g], dil) for g, dil in enumerate(dils)])
    x = _attn_post(os, lses, x, gt, w["ln"][1][0], w["w_o_attn"], per_b_mod=True, tm=256, dils=dils)
    sh2, sc2, gt2 = (per_b(m) for m in mods["ada"][1][1])
    x = _moe_sorted(x, sc2, sh2, gt2, w["ln"][1][1], w, 1)
    return x, wkv[None], last.reshape(1, bsz, D), k_new, v_new


def _trunk_sample(x, mods, state_wkv, state_shift, cache_k, cache_v, w):
    bsz = x.shape[0]
    xs = x.reshape(1, bsz, D)
    row = lambda m: m[None]
    sh, sc, gt = (row(m) for m in mods["ada"][0][0])
    r, lw, k, v, a, b, g, hm = _rwkv_pre(xs, sc, sh, state_shift[0][None], w, seq_mode=False, tm=bsz)
    s_new, y = _wkv_step(state_wkv[0].transpose(1, 2, 3, 0), *(t[0].T for t in (r, lw, k, v, a, b)))
    s_new = s_new.transpose(3, 0, 1, 2)
    xs = _rwkv_post(y.T[None], r, k, v, g, xs, gt, w["post_vec"], w["rwkv_w_o"], pairs=False, tm=bsz)
    sh2, sc2, gt2 = (row(m) for m in mods["ada"][0][1])
    xs = _moe_dense(xs, sc2, sh2, gt2, w["ln"][0][1], w, 0)

    sh, sc, gt = (row(m) for m in mods["ada"][1][0])
    ksh, ksc = (row(m) for m in mods["kv"])
    rot = _rot_tables(jnp.full((bsz,), PAST_LEN, I32))
    k_new, v_new, q = _attn_pre(xs, sc, sh, ksc, ksh, rot, w["w_q"], w["w_kv"], per_b_mod=False, tm=bsz)
    o, lse = _attn_step(q[0], k_new[0], v_new[0], cache_k, cache_v)
    xs = _attn_post([o[i][None] for i in range(3)], [lse[i][None] for i in range(3)], xs, gt, w["ln"][1][0],
                    w["w_o_attn"], per_b_mod=False, tm=bsz)
    sh2, sc2, gt2 = (row(m) for m in mods["ada"][1][1])
    xs = _moe_dense(xs, sc2, sh2, gt2, w["ln"][1][1], w, 1)
    return (xs.reshape(bsz, 1, D), s_new[None], hm, k_new.reshape(bsz, 1, KV_HEADS, HD),
            v_new.reshape(bsz, 1, KV_HEADS, HD))


def kernel(x_prompt, x_sample, state_wkv, state_shift, cache_k, cache_v, c_prompt, c_sample, ada_w, ada_b, ln_g, ln_b, rwkv_mu, rwkv_w_rkv, rwkv_w0, rwkv_w1, rwkv_w2, rwkv_a0, rwkv_a1, rwkv_a2, rwkv_g1, rwkv_g2, rwkv_k_k, rwkv_k_a, rwkv_r_k, rwkv_lnx_g, rwkv_lnx_b, rwkv_w_o, w_q, w_kv, kv_ada_w, kv_ada_b, w_o_attn, router_w, router_b, moe_w_in, moe_w_out):
    p = {"ada_w": ada_w, "ada_b": ada_b, "ln_g": ln_g, "ln_b": ln_b, "rwkv_mu": rwkv_mu, "rwkv_w_rkv": rwkv_w_rkv,
         "rwkv_w0": rwkv_w0, "rwkv_w1": rwkv_w1, "rwkv_w2": rwkv_w2, "rwkv_a0": rwkv_a0, "rwkv_a1": rwkv_a1,
         "rwkv_a2": rwkv_a2, "rwkv_g1": rwkv_g1, "rwkv_g2": rwkv_g2, "rwkv_k_k": rwkv_k_k, "rwkv_k_a": rwkv_k_a,
         "rwkv_r_k": rwkv_r_k, "rwkv_lnx_g": rwkv_lnx_g, "rwkv_lnx_b": rwkv_lnx_b, "rwkv_w_o": rwkv_w_o,
         "w_q": w_q, "w_kv": w_kv, "kv_ada_w": kv_ada_w, "kv_ada_b": kv_ada_b, "w_o_attn": w_o_attn,
         "router_w": router_w, "router_b": router_b, "moe_w_in": moe_w_in, "moe_w_out": moe_w_out}
    w = _prep_weights(p)
    mods = _modulations(c_prompt, c_sample, p)
    bp, tp, _ = x_prompt.shape
    y_p, wkv_p, shift_p, k_p, v_p = _trunk_prompt(x_prompt, mods["prompt"], w)
    keep = min(PAST_LEN, tp)
    k_p = k_p[:, tp - keep:].reshape(bp, keep, KV_HEADS, HD)
    v_p = v_p[:, tp - keep:].reshape(bp, keep, KV_HEADS, HD)
    y_s, wkv_s, shift_s, k_s, v_s = _trunk_sample(x_sample, mods["sample"], state_wkv, state_shift, cache_k,
                                                  cache_v, w)
    return (y_p, y_s, wkv_p, shift_p, k_p, v_p, wkv_s, shift_s, k_s, v_s)
```

```python
import functools

import jax
import jax.numpy as jnp
from jax import lax
from jax.experimental import pallas as pl
from jax.experimental.pallas import tpu as pltpu

F32 = jnp.float32
BF16 = jnp.bfloat16
I32 = jnp.int32

D = 1024
HEADS = 16
HD = 64
LANES = 128
PAIRS = D // LANES
CHUNK = 64
WKV_GROUP = 8
N_EXPERTS = 32
EXPERTS_PER_GROUP = 8
N_EXPERT_GROUPS = 4
D_EXPERT = 512
MOE_BLK = 256
DMA_UNROLL = 8
KV_HEADS = 4
REP = 4
KVW = KV_HEADS * HD
DILATED_GROUPS = ((128, 1), (512, 4), (2048, 16))
ATT_WIN = 128
PAST_LEN = 2048
ROT_DIM = 16
ROPE_THETA = 500000.0
DEPTH = 2
DN_ALPHA = (2 * DEPTH) ** 0.25
LN_EPS = 1e-5
GN_EPS = 64e-5
VMEM_LIMIT = 56 * 1024 * 1024


def _cp(sem):
    return pltpu.CompilerParams(dimension_semantics=sem, vmem_limit_bytes=VMEM_LIMIT)


def _bdot(a, b):
    return jnp.dot(a.astype(BF16), b.astype(BF16), preferred_element_type=F32)


def _bdot_nt(a, b):
    return lax.dot_general(a.astype(BF16), b.astype(BF16), (((1,), (1,)), ((), ())),
                           preferred_element_type=F32)


def _bdot_tn(a, b):
    return lax.dot_general(a.astype(BF16), b.astype(BF16), (((0,), (0,)), ((), ())),
                           preferred_element_type=F32)


def _split_dot(x, m):
    hi = x.astype(BF16)
    lo = (x - hi.astype(F32)).astype(BF16)
    return (jnp.dot(hi, m, preferred_element_type=F32) + jnp.dot(lo, m, preferred_element_type=F32))


def _head_ones():
    r = lax.broadcasted_iota(I32, (LANES, LANES), 0) // HD
    c = lax.broadcasted_iota(I32, (LANES, LANES), 1) // HD
    return (r == c).astype(BF16)


def _segsum(x, ones_bd):
    cols = [_split_dot(x[:, c * LANES:(c + 1) * LANES], ones_bd) for c in range(x.shape[1] // LANES)]
    return cols[0] if len(cols) == 1 else jnp.concatenate(cols, axis=1)


def _layer_norm(x, g, b):
    mu = jnp.mean(x, axis=-1, keepdims=True)
    xc = x - mu
    var = jnp.mean(xc * xc, axis=-1, keepdims=True)
    return xc * lax.rsqrt(var + LN_EPS) * g + b


def _sigmoid(x):
    return 1.0 / (1.0 + jnp.exp(-x))


def _to_pairs(ref, val):
    for p in range(PAIRS):
        ref[p] = val[:, p * LANES:(p + 1) * LANES]


def _from_pairs(ref):
    return jnp.concatenate([ref[p] for p in range(PAIRS)], axis=1)


def _ada_kernel(c_ref, w_ref, b_ref, o_ref):
    c = c_ref[...]
    o_ref[...] = _bdot(c * _sigmoid(c), w_ref[...]) + b_ref[...]


def _ada_linear(c, w, b, tn=512):
    s, _, n = w.shape
    m = c.shape[0]
    return pl.pallas_call(
        _ada_kernel,
        grid=(s, n // tn),
        in_specs=[pl.BlockSpec((m, D), lambda i, j: (0, 0)),
                  pl.BlockSpec((None, D, tn), lambda i, j: (i, 0, j)),
                  pl.BlockSpec((None, 1, tn), lambda i, j: (i, 0, j))],
        out_specs=pl.BlockSpec((None, m, tn), lambda i, j: (i, 0, j)),
        out_shape=jax.ShapeDtypeStruct((s, m, n), F32),
        compiler_params=_cp(("arbitrary", "arbitrary")),
        name="ada_linear",
    )(c, w, b.reshape(s, 1, n))


def _rwkv_pre_kernel(x_ref, sc_ref, sh_ref, prev_ref, mu_ref, wrkv_ref, w1_ref, w2_ref, a1_ref, a2_ref,
                     g1_ref, g2_ref, vec_ref,
                     r_ref, lw_ref, k_ref, v_ref, a_ref, b_ref, g_ref, hm_ref, carry_ref, *, seq_mode, tm):
    x = x_ref[...]
    hm = x * (1.0 + sc_ref[...]) + sh_ref[...]
    if seq_mode:
        @pl.when(pl.program_id(1) == 0)
        def _():
            carry_ref[...] = prev_ref[...]
        row = lax.broadcasted_iota(I32, hm.shape, 0)
        hprev = jnp.where(row == 0, carry_ref[...], pltpu.roll(hm, 1, axis=0))
        carry_ref[...] = hm[tm - 1:tm, :]
        hm_ref[...] = hm[tm - 1:tm, :]
    else:
        hprev = prev_ref[...]
        hm_ref[...] = hm
    xx = hprev - hm

    def mix(i):
        return hm + xx * mu_ref[i:i + 1, :]

    w0, a0, k_k, k_a = (vec_ref[i:i + 1, :] for i in range(4))
    r = _bdot(mix(0), wrkv_ref[0])
    k = _bdot(mix(2), wrkv_ref[1])
    v = _bdot(mix(3), wrkv_ref[2])
    wl = w0 + _bdot(jnp.tanh(_bdot(mix(1), w1_ref[...])), w2_ref[...])
    z = -wl
    softplus = jnp.maximum(z, 0.0) + jnp.log1p(jnp.exp(-jnp.abs(z)))
    lw = -jnp.exp(-softplus - 0.5)
    a_lr = _sigmoid(a0 + _bdot(_bdot(mix(4), a1_ref[...]), a2_ref[...]))
    g = _bdot(_sigmoid(_bdot(mix(5), g1_ref[...])), g2_ref[...])
    kk = k * k_k
    kk = kk / jnp.maximum(jnp.sqrt(_segsum(kk * kk, _head_ones())), 1e-12)
    kmod = k * (1.0 + (a_lr - 1.0) * k_a)
    outs = ((r_ref, r), (lw_ref, lw), (k_ref, kmod), (v_ref, v), (a_ref, -kk), (b_ref, kk * a_lr), (g_ref, g))
    for ref, val in outs:
        if seq_mode:
            _to_pairs(ref, val)
        else:
            ref[...] = val


def _rwkv_pre(x, sc, sh, prev, w, *, seq_mode, tm):
    bsz, t, _ = x.shape
    grid = (bsz, t // tm)
    row = pl.BlockSpec((None, tm, D), lambda b, i: (b, i, 0))
    per_b = pl.BlockSpec((None, 1, D), lambda b, i: (b, 0, 0))
    mod = per_b if seq_mode else row

    def const(shape):
        return pl.BlockSpec(shape, lambda b, i: (0,) * len(shape))

    if seq_mode:
        out_big = pl.BlockSpec((None, PAIRS, tm, LANES), lambda b, i: (b, 0, i, 0))
        big_shape = jax.ShapeDtypeStruct((bsz, PAIRS, t, LANES), F32)
        hm_spec, hm_shape = per_b, jax.ShapeDtypeStruct((bsz, 1, D), F32)
    else:
        out_big, big_shape = row, jax.ShapeDtypeStruct((bsz, t, D), F32)
        hm_spec, hm_shape = row, jax.ShapeDtypeStruct((bsz, t, D), F32)
    return pl.pallas_call(
        functools.partial(_rwkv_pre_kernel, seq_mode=seq_mode, tm=tm),
        grid=grid,
        in_specs=[row, mod, mod, mod, const((6, D)), const((3, D, D)), const((D, LANES)), const((LANES, D)),
                  const((D, LANES)), const((LANES, D)), const((D, 2 * LANES)), const((2 * LANES, D)),
                  const((4, D))],
        out_specs=[out_big] * 7 + [hm_spec],
        out_shape=[big_shape] * 7 + [hm_shape],
        scratch_shapes=[pltpu.VMEM((1, D), F32)],
        compiler_params=_cp(("arbitrary", "arbitrary")),
        name="rwkv_pre",
    )(x, sc, sh, prev, w["mu"], w["w_rkv"], w["w1"], w["w2"], w["a1"], w["a2"], w["g1"], w["g2"], w["vec"])


def _wkv_chunk_kernel(r_ref, lw_ref, k_ref, v_ref, a_ref, b_ref, y_ref, zf_ref, z_ref, *, n_pairs, group):
    c = pl.program_id(0)

    @pl.when(c == 0)
    def _():
        z_ref[...] = jnp.zeros_like(z_ref)

    L = CHUNK
    row_l = lax.broadcasted_iota(I32, (L, L), 0)
    col_l = lax.broadcasted_iota(I32, (L, L), 1)
    tri_incl = (row_l >= col_l).astype(BF16)
    lane = lax.broadcasted_iota(I32, (L, LANES), 1)
    head0 = lane < HD
    row = lax.broadcasted_iota(I32, (LANES, LANES), 0)
    col = lax.broadcasted_iota(I32, (LANES, LANES), 1)
    strict = row > col
    incl = row >= col
    eye = (row == col).astype(F32)

    def expand(x):
        return jnp.concatenate([jnp.where(head0, x, 0.0), jnp.where(head0, 0.0, x)], axis=0)

    def group_body(gi, carry):
        ids = [gi * group + j for j in range(group)]
        bp = [(i // PAIRS, i % PAIRS) for i in ids]
        G = range(group)
        r = [r_ref[b_i, p_i] for b_i, p_i in bp]
        lw = [lw_ref[b_i, p_i] for b_i, p_i in bp]
        k = [k_ref[b_i, p_i] for b_i, p_i in bp]
        v = [v_ref[b_i, p_i] for b_i, p_i in bp]
        a = [a_ref[b_i, p_i] for b_i, p_i in bp]
        b = [b_ref[b_i, p_i] for b_i, p_i in bp]
        z0 = [z_ref[i] for i in ids]
        cum = [_split_dot_left(tri_incl, lw[j]) for j in G]
        cum_l = [cum[j][L - 1:L, :] for j in G]
        inv = [jnp.exp(-cum[j]) for j in G]
        tail = [jnp.exp(cum_l[j] - cum[j]) for j in G]
        a_e = [expand(a[j] * jnp.exp(cum[j] - lw[j])) for j in G]
        r_e = [expand(r[j] * jnp.exp(cum[j])) for j in G]
        b_e = [expand(b[j] * inv[j]) for j in G]
        k_e = [expand(k[j] * inv[j]) for j in G]
        bd_e = [expand(b[j] * tail[j]) for j in G]
        kd_e = [expand(k[j] * tail[j]) for j in G]
        v_e = [expand(v[j]) for j in G]
        gm = [_bdot_nt(jnp.concatenate([a_e[j], r_e[j]], axis=0), jnp.concatenate([b_e[j], k_e[j]], axis=0))
              for j in G]
        m_ab = [jnp.where(strict, gm[j][:LANES, :LANES], 0.0) for j in G]
        m_ak = [jnp.where(strict, gm[j][:LANES, LANES:], 0.0) for j in G]
        m_rb = [jnp.where(incl, gm[j][LANES:, :LANES], 0.0) for j in G]
        m_rk = [jnp.where(incl, gm[j][LANES:, LANES:], 0.0) for j in G]
        tinv = [eye + m_ab[j] for j in G]
        pw = [_bdot(m_ab[j], m_ab[j]) for j in G]
        for _ in range(4):
            both = [_bdot(pw[j], jnp.concatenate([pw[j], tinv[j]], axis=1)) for j in G]
            pw = [both[j][:, :LANES] for j in G]
            tinv = [tinv[j] + both[j][:, LANES:] for j in G]
        tinv = [tinv[j] + _bdot(pw[j], tinv[j]) for j in G]
        rhs = [_bdot(a_e[j], z0[j]) + _bdot(m_ak[j], v_e[j]) for j in G]
        u = [_bdot(tinv[j], rhs[j]) for j in G]
        zuv = [jnp.concatenate([z0[j], u[j], v_e[j]], axis=0) for j in G]
        decay = [eye * jnp.exp(cum_l[j]) for j in G]
        for j in G:
            z_ref[ids[j]] = _bdot_tn(jnp.concatenate([decay[j], bd_e[j], kd_e[j]], axis=0), zuv[j])
            y_e = _bdot(jnp.concatenate([r_e[j], m_rb[j], m_rk[j]], axis=1), zuv[j])
            y_ref[bp[j][0], bp[j][1]] = y_e[:L] + y_e[L:]
        return carry

    lax.fori_loop(0, n_pairs // group, group_body, 0)

    @pl.when(c == pl.num_programs(0) - 1)
    def _():
        zf_ref[...] = z_ref[...]


def _split_dot_left(m, x):
    hi = x.astype(BF16)
    lo = (x - hi.astype(F32)).astype(BF16)
    return (jnp.dot(m, hi, preferred_element_type=F32) + jnp.dot(m, lo, preferred_element_type=F32))


def _wkv_chunked(r, lw, k, v, a, b):
    bsz, _, t, _ = r.shape
    n_pairs = bsz * PAIRS
    blk = pl.BlockSpec((bsz, PAIRS, CHUNK, LANES), lambda c: (0, 0, c, 0))
    return pl.pallas_call(
        functools.partial(_wkv_chunk_kernel, n_pairs=n_pairs, group=WKV_GROUP),
        grid=(t // CHUNK,),
        in_specs=[blk] * 6,
        out_specs=[blk, pl.BlockSpec((n_pairs, LANES, LANES), lambda c: (0, 0, 0))],
        out_shape=[jax.ShapeDtypeStruct(r.shape, F32), jax.ShapeDtypeStruct((n_pairs, LANES, LANES), F32)],
        scratch_shapes=[pltpu.VMEM((n_pairs, LANES, LANES), F32)],
        compiler_params=_cp(("arbitrary",)),
        name="wkv_chunked",
    )(r, lw, k, v, a, b)


def _wkv_step_kernel(s_ref, r_ref, lw_ref, k_ref, v_ref, a_ref, b_ref, so_ref, y_ref):
    w = jnp.exp(lw_ref[...])
    a, b, k, r = a_ref[...], b_ref[...], k_ref[...], r_ref[...]

    def body(i, c):
        s = s_ref[i]
        sa = jnp.sum(s * a, axis=0, keepdims=True)
        s_new = s * w + sa * b + v_ref[pl.ds(i, 1), :] * k
        so_ref[i] = s_new
        y_ref[pl.ds(i, 1), :] = jnp.sum(s_new * r, axis=0, keepdims=True)
        return c

    lax.fori_loop(0, HD, body, 0, unroll=4)


def _wkv_step(state_t, r, lw, k, v, a, b):
    bsz = state_t.shape[-1]
    big = pl.BlockSpec((None, HD, HD, bsz), lambda h: (h, 0, 0, 0))
    vec = pl.BlockSpec((HD, bsz), lambda h: (h, 0))
    return pl.pallas_call(
        _wkv_step_kernel,
        grid=(HEADS,),
        in_specs=[big] + [vec] * 6,
        out_specs=[big, vec],
        out_shape=[jax.ShapeDtypeStruct(state_t.shape, F32), jax.ShapeDtypeStruct((D, bsz), F32)],
        compiler_params=_cp(("arbitrary",)),
        name="wkv_step",
    )(state_t, r, lw, k, v, a, b)


def _rwkv_post_kernel(y_ref, r_ref, k_ref, v_ref, g_ref, x_ref, gt_ref, vec_ref, wo_ref, o_ref, *, pairs):
    load = _from_pairs if pairs else (lambda ref: ref[...])
    y, r, k, v, g = (load(ref) for ref in (y_ref, r_ref, k_ref, v_ref, g_ref))
    r_k, lnx_g, lnx_b, ln_g, ln_b = (vec_ref[i:i + 1, :] for i in range(5))
    ones_bd = _head_ones()
    ym = _segsum(y, ones_bd) * (1.0 / HD)
    yc = y - ym
    yv = _segsum(yc * yc, ones_bd) * (1.0 / HD)
    yn = yc * lax.rsqrt(yv + GN_EPS) * lnx_g + lnx_b
    bonus = _segsum(r * k * r_k, ones_bd) * v
    mix = _bdot((yn + bonus) * g, wo_ref[...])
    o_ref[...] = _layer_norm(DN_ALPHA * x_ref[...] + gt_ref[...] * mix, ln_g, ln_b)


def _rwkv_post(y, r, k, v, g, x, gt, vec, w_o, *, pairs, tm):
    bsz, t, _ = x.shape
    row = pl.BlockSpec((None, tm, D), lambda b, i: (b, i, 0))
    big = pl.BlockSpec((None, PAIRS, tm, LANES), lambda b, i: (b, 0, i, 0)) if pairs else row
    mod = pl.BlockSpec((None, 1, D), lambda b, i: (b, 0, 0)) if pairs else row
    return pl.pallas_call(
        functools.partial(_rwkv_post_kernel, pairs=pairs),
        grid=(bsz, t // tm),
        in_specs=[big] * 5 + [row, mod, pl.BlockSpec((5, D), lambda b, i: (0, 0)),
                              pl.BlockSpec((D, D), lambda b, i: (0, 0))],
        out_specs=row,
        out_shape=jax.ShapeDtypeStruct(x.shape, F32),
        compiler_params=_cp(("arbitrary", "arbitrary")),
        name="rwkv_post",
    )(y, r, k, v, g, x, gt, vec, w_o)


def _top2(v):
    io = lax.broadcasted_iota(I32, v.shape, 0)
    m1 = jnp.max(v, axis=0, keepdims=True)
    i1 = jnp.min(jnp.where(v == m1, io, EXPERTS_PER_GROUP), axis=0, keepdims=True)
    v2 = jnp.where(io == i1, -jnp.inf, v)
    m2 = jnp.max(v2, axis=0, keepdims=True)
    i2 = jnp.min(jnp.where(v2 == m2, io, EXPERTS_PER_GROUP), axis=0, keepdims=True)
    return m1 + m2, i1, i2


def _moe_route_kernel(x_ref, sc_ref, sh_ref, rw_ref, rb_ref, h_ref, idx_ref, gate_ref, gmat_ref, cnt_ref,
                      carry_ref, *, tm, rows_out):
    @pl.when((pl.program_id(0) == 0) & (pl.program_id(1) == 0))
    def _():
        carry_ref[...] = jnp.zeros_like(carry_ref)

    h = x_ref[...] * (1.0 + sc_ref[...]) + sh_ref[...]
    h_ref[...] = h
    aff = _sigmoid(_bdot_nt(rw_ref[...], h))
    sel = aff + rb_ref[...]
    best = gi = i1 = i2 = None
    for g in range(N_EXPERT_GROUPS):
        sc, j1, j2 = _top2(sel[g * EXPERTS_PER_GROUP:(g + 1) * EXPERTS_PER_GROUP, :])
        if g == 0:
            best, gi, i1, i2 = sc, jnp.zeros_like(j1), j1, j2
        else:
            upd = sc > best
            best = jnp.where(upd, sc, best)
            gi = jnp.where(upd, g, gi)
            i1 = jnp.where(upd, j1, i1)
            i2 = jnp.where(upd, j2, i2)
    e0 = gi * EXPERTS_PER_GROUP + i1
    e1 = gi * EXPERTS_PER_GROUP + i2
    io = lax.broadcasted_iota(I32, (N_EXPERTS, tm), 0)
    oh0 = io == e0
    oh1 = io == e1
    a0 = jnp.sum(jnp.where(oh0, aff, 0.0), axis=0, keepdims=True)
    a1 = jnp.sum(jnp.where(oh1, aff, 0.0), axis=0, keepdims=True)
    den = a0 + a1
    g0 = a0 / den
    g1 = a1 / den
    oh = jnp.where(oh0 | oh1, 1.0, 0.0)
    tr = lax.broadcasted_iota(I32, (tm, tm), 0)
    tc = lax.broadcasted_iota(I32, (tm, tm), 1)
    before = jnp.dot(oh.astype(BF16), (tr < tc).astype(BF16), preferred_element_type=F32) + carry_ref[...]
    rank0 = jnp.sum(jnp.where(oh0, before, 0.0), axis=0, keepdims=True)
    rank1 = jnp.sum(jnp.where(oh1, before, 0.0), axis=0, keepdims=True)
    carry_ref[...] = carry_ref[...] + jnp.sum(oh, axis=1, keepdims=True)
    zi = jnp.zeros((4, tm), I32)
    idx_ref[...] = jnp.concatenate([e0, e1, rank0.astype(I32), rank1.astype(I32), zi], axis=0)
    gate_ref[...] = jnp.concatenate([g0, g1, jnp.zeros((6, tm), F32)], axis=0)
    gmat_ref[...] = jnp.where(oh0, g0, 0.0) + jnp.where(oh1, g1, 0.0)
    cnt_ref[...] = jnp.broadcast_to(carry_ref[...], cnt_ref.shape)


def _moe_route(x, sc, sh, router_w_t, router_b, *, per_b_mod, rows_out, tm):
    bsz, t, _ = x.shape
    n = bsz * t
    nt = t // tm
    row = pl.BlockSpec((None, tm, D), lambda b, i: (b, i, 0))
    mod = pl.BlockSpec((None, 1, D), lambda b, i: (b, 0, 0)) if per_b_mod else row
    if rows_out:
        h_spec = pl.BlockSpec((tm, D), lambda b, i: (b * nt + i, 0))
        h_shape = jax.ShapeDtypeStruct((n, D), F32)
    else:
        h_spec, h_shape = row, jax.ShapeDtypeStruct(x.shape, F32)
    tok = lambda rows: pl.BlockSpec((rows, tm), lambda b, i: (0, b * nt + i))
    return pl.pallas_call(
        functools.partial(_moe_route_kernel, tm=tm, rows_out=rows_out),
        grid=(bsz, nt),
        in_specs=[row, mod, mod, pl.BlockSpec((N_EXPERTS, D), lambda b, i: (0, 0)),
                  pl.BlockSpec((N_EXPERTS, 1), lambda b, i: (0, 0))],
        out_specs=[h_spec, tok(8), tok(8), tok(N_EXPERTS), pl.BlockSpec((N_EXPERTS, LANES), lambda b, i: (0, 0))],
        out_shape=[h_shape, jax.ShapeDtypeStruct((8, n), I32), jax.ShapeDtypeStruct((8, n), F32),
                   jax.ShapeDtypeStruct((N_EXPERTS, n), F32), jax.ShapeDtypeStruct((N_EXPERTS, LANES), F32)],
        scratch_shapes=[pltpu.VMEM((N_EXPERTS, 1), F32)],
        compiler_params=_cp(("arbitrary", "arbitrary")),
        name="moe_route",
    )(x, sc, sh, router_w_t, router_b)


def _row_copy(src, src_row, dst, dst_row, sem):
    return pltpu.make_async_copy(src.at[pl.ds(src_row, 1), :], dst.at[pl.ds(dst_row, 1), :], sem)


def _wait_rows(src, dst, rows, sem):
    pltpu.make_async_copy(src.at[pl.ds(0, rows), :], dst.at[pl.ds(0, rows), :], sem).wait()


def _moe_dispatch_kernel(slot_ref, zs_ref, h_ref, xs_ref, zbuf, sem, *, tm, n):
    base = pl.program_id(0) * tm

    @pl.when(pl.program_id(0) == 0)
    def _():
        zbuf[...] = jnp.zeros_like(zbuf)

        def block_copy(e):
            return pltpu.make_async_copy(zbuf, xs_ref.at[pl.ds(pl.multiple_of(zs_ref[e], 8), MOE_BLK), :],
                                         sem.at[2])

        fresh = [zs_ref[e] != zs_ref[e - 1] for e in range(1, zs_ref.shape[0])]
        block_copy(0).start()
        for e, f in enumerate(fresh, start=1):
            pl.when(f)(lambda e=e: block_copy(e).start())
        block_copy(0).wait()
        for e, f in enumerate(fresh, start=1):
            pl.when(f)(lambda e=e: block_copy(e).wait())

    def issue(i, c):
        _row_copy(h_ref, i, xs_ref, slot_ref[base + i], sem.at[0]).start(priority=0)
        _row_copy(h_ref, i, xs_ref, slot_ref[n + base + i], sem.at[1]).start(priority=1)
        return c

    lax.fori_loop(0, tm, issue, 0, unroll=DMA_UNROLL)
    _wait_rows(h_ref, xs_ref, tm, sem.at[0])
    _wait_rows(h_ref, xs_ref, tm, sem.at[1])


def _moe_dispatch(h_rows, slots, zero_start, n_slots, tm):
    n = h_rows.shape[0]
    return pl.pallas_call(
        functools.partial(_moe_dispatch_kernel, tm=tm, n=n),
        grid_spec=pltpu.PrefetchScalarGridSpec(
            num_scalar_prefetch=2, grid=(n // tm,),
            in_specs=[pl.BlockSpec((tm, D), lambda i, sl, zs: (i, 0))],
            out_specs=pl.BlockSpec(memory_space=pl.ANY),
            scratch_shapes=[pltpu.VMEM((MOE_BLK, D), F32), pltpu.SemaphoreType.DMA((3,))]),
        out_shape=jax.ShapeDtypeStruct((n_slots, D), F32),
        compiler_params=_cp(("arbitrary",)),
        name="moe_dispatch",
    )(slots, zero_start, h_rows)


def _moe_expert_kernel(be_ref, used_ref, xs_ref, win_ref, wout_ref, ys_ref, win_bf, wout_bf):
    i = pl.program_id(0)
    fresh = (i == 0) | (be_ref[i] != be_ref[jnp.maximum(i - 1, 0)])

    @pl.when(fresh)
    def _():
        win_bf[...] = win_ref[...].astype(BF16)
        wout_bf[...] = wout_ref[...].astype(BF16)

    @pl.when(i < used_ref[0])
    def _():
        hmid = jnp.dot(xs_ref[...].astype(BF16), win_bf[...], preferred_element_type=F32)
        gt = hmid[:, :D_EXPERT]
        up = hmid[:, D_EXPERT:]
        act = gt * _sigmoid(gt) * up
        ys_ref[...] = jnp.dot(act.astype(BF16), wout_bf[...], preferred_element_type=F32)

    @pl.when(i >= used_ref[0])
    def _():
        ys_ref[...] = jnp.zeros_like(ys_ref)


def _moe_experts_sorted(xs, blk_exp, n_used, w_in, w_out, layer):
    n_slots = xs.shape[0]
    n_blocks = n_slots // MOE_BLK
    x_map = lambda i, be, u: (jnp.minimum(i, u[0] - 1), 0)
    return pl.pallas_call(
        _moe_expert_kernel,
        grid_spec=pltpu.PrefetchScalarGridSpec(
            num_scalar_prefetch=2, grid=(n_blocks,),
            in_specs=[pl.BlockSpec((MOE_BLK, D), x_map),
                      pl.BlockSpec((None, None, D, 2 * D_EXPERT), lambda i, be, u: (layer, be[i], 0, 0)),
                      pl.BlockSpec((None, None, D_EXPERT, D), lambda i, be, u: (layer, be[i], 0, 0))],
            out_specs=pl.BlockSpec((MOE_BLK, D), lambda i, be, u: (i, 0)),
            scratch_shapes=[pltpu.VMEM((D, 2 * D_EXPERT), BF16), pltpu.VMEM((D_EXPERT, D), BF16)]),
        out_shape=jax.ShapeDtypeStruct((n_slots, D), F32),
        compiler_params=_cp(("arbitrary",)),
        name="moe_experts_sorted",
    )(blk_exp, n_used, xs, w_in, w_out)


def _moe_combine_kernel(slot_ref, ys_ref, gate_ref, x_ref, gt_ref, vec_ref, o_ref, ya, yb, sem, *, tm, n):
    step = pl.program_id(0)
    slot = step % 2

    def gather(s, sl):
        def issue(i, c):
            t = s * tm + i
            _row_copy(ys_ref, slot_ref[t], ya.at[sl], i, sem.at[sl, 0]).start(priority=0)
            _row_copy(ys_ref, slot_ref[n + t], yb.at[sl], i, sem.at[sl, 1]).start(priority=1)
            return c
        lax.fori_loop(0, tm, issue, 0, unroll=DMA_UNROLL)

    @pl.when(step == 0)
    def _():
        gather(0, 0)

    @pl.when(step + 1 < pl.num_programs(0))
    def _():
        gather(step + 1, 1 - slot)

    _wait_rows(ys_ref, ya.at[slot], tm, sem.at[slot, 0])
    _wait_rows(ys_ref, yb.at[slot], tm, sem.at[slot, 1])
    ff = ya[slot] * gate_ref[:, 0:1] + yb[slot] * gate_ref[:, 1:2]
    o_ref[...] = _layer_norm(DN_ALPHA * x_ref[...] + gt_ref[...] * ff, vec_ref[0:1, :], vec_ref[1:2, :])


def _moe_combine(ys, slots, gates, x, gt, vec, tm):
    bsz, t, _ = x.shape
    nt = t // tm
    return pl.pallas_call(
        functools.partial(_moe_combine_kernel, tm=tm, n=bsz * t),
        grid_spec=pltpu.PrefetchScalarGridSpec(
            num_scalar_prefetch=1, grid=(bsz * nt,),
            in_specs=[pl.BlockSpec(memory_space=pl.ANY),
                      pl.BlockSpec((tm, 2), lambda i, sl: (i, 0)),
                      pl.BlockSpec((None, tm, D), lambda i, sl: (i // nt, i % nt, 0)),
                      pl.BlockSpec((None, 1, D), lambda i, sl: (i // nt, 0, 0)),
                      pl.BlockSpec((2, D), lambda i, sl: (0, 0))],
            out_specs=pl.BlockSpec((None, tm, D), lambda i, sl: (i // nt, i % nt, 0)),
            scratch_shapes=[pltpu.VMEM((2, tm, D), F32), pltpu.VMEM((2, tm, D), F32),
                            pltpu.SemaphoreType.DMA((2, 2))]),
        out_shape=jax.ShapeDtypeStruct(x.shape, F32),
        compiler_params=_cp(("arbitrary",)),
        name="moe_combine",
    )(slots, ys, gates, x, gt, vec)


def _moe_sorted(x, sc, sh, gt, vec, w, layer, tm=256):
    bsz, t, _ = x.shape
    n = bsz * t
    n_blocks = (n * 2) // MOE_BLK + N_EXPERTS
    h_rows, idx, gate, _, cnt = _moe_route(x, sc, sh, w["router_w_t"], w["router_b"], per_b_mod=True,
                                           rows_out=True, tm=tm)
    counts = cnt[:, 0].astype(I32)
    padded = (counts + MOE_BLK - 1) // MOE_BLK * MOE_BLK
    pad_end = jnp.cumsum(padded)
    pad_start = pad_end - padded
    slots = jnp.concatenate([pad_start[idx[0]] + idx[2], pad_start[idx[1]] + idx[3]])
    blk_start = jnp.arange(n_blocks, dtype=I32) * MOE_BLK
    blk_exp = jnp.minimum(jnp.sum((pad_end[None, :] <= blk_start[:, None]).astype(I32), axis=1), N_EXPERTS - 1)
    n_used = (pad_end[-1:] // MOE_BLK).astype(I32)
    tail = jnp.minimum(n_used[0] + jnp.arange(N_EXPERTS, dtype=I32), n_blocks - 1) * MOE_BLK
    zero_start = jnp.concatenate([jnp.maximum(pad_end - MOE_BLK, 0), tail])
    xs = _moe_dispatch(h_rows, slots, zero_start, n_blocks * MOE_BLK, tm)
    ys = _moe_experts_sorted(xs, blk_exp, n_used, w["moe_w_in"], w["moe_w_out"], layer)
    return _moe_combine(ys, slots, gate[:2].T, x, gt, vec, tm)


def _moe_dense_kernel(h_ref, gm_ref, win_ref, wout_ref, x_ref, gt_ref, vec_ref, o_ref, acc_ref):
    e = pl.program_id(0)

    @pl.when(e == 0)
    def _():
        acc_ref[...] = jnp.zeros_like(acc_ref)

    lane = lax.broadcasted_iota(I32, gm_ref.shape, 1)
    gcol = jnp.sum(jnp.where(lane == e, gm_ref[...], 0.0), axis=1, keepdims=True)
    hmid = _bdot(h_ref[...], win_ref[...])
    gt = hmid[:, :D_EXPERT]
    up = hmid[:, D_EXPERT:]
    y = _bdot(gt * _sigmoid(gt) * up, wout_ref[...])
    acc_ref[...] = acc_ref[...] + jnp.where(gcol != 0.0, gcol * y, 0.0)

    @pl.when(e == N_EXPERTS - 1)
    def _():
        o_ref[...] = _layer_norm(DN_ALPHA * x_ref[...] + gt_ref[...] * acc_ref[...], vec_ref[0:1, :],
                                 vec_ref[1:2, :])


def _moe_dense(x, sc, sh, gt, vec, w, layer):
    _, m, _ = x.shape
    h, _, _, gmat, _ = _moe_route(x, sc, sh, w["router_w_t"], w["router_b"], per_b_mod=False, rows_out=False,
                                  tm=m)
    full = pl.BlockSpec((m, D), lambda e: (0, 0))
    out = pl.pallas_call(
        _moe_dense_kernel,
        grid=(N_EXPERTS,),
        in_specs=[full, pl.BlockSpec((m, N_EXPERTS), lambda e: (0, 0)),
                  pl.BlockSpec((None, None, D, 2 * D_EXPERT), lambda e: (layer, e, 0, 0)),
                  pl.BlockSpec((None, None, D_EXPERT, D), lambda e: (layer, e, 0, 0)),
                  full, full, pl.BlockSpec((2, D), lambda e: (0, 0))],
        out_specs=full,
        out_shape=jax.ShapeDtypeStruct((m, D), F32),
        scratch_shapes=[pltpu.VMEM((m, D), F32)],
        compiler_params=_cp(("arbitrary",)),
        name="moe_dense",
    )(h[0], gmat.T, w["moe_w_in"], w["moe_w_out"], x[0], gt[0], vec)
    return out[None]


def _rotary(x, cos_t, sa_t, sb_t):
    reps = x.shape[1] // LANES
    tile = lambda t: t if reps == 1 else jnp.concatenate([t] * reps, axis=1)
    n = x.shape[1]
    half = ROT_DIM // 2
    return x * tile(cos_t) + pltpu.roll(x, n - half, axis=1) * tile(sa_t) + pltpu.roll(x, half, axis=1) * tile(sb_t)


def _class_rows(c, rows, dil):
    return pl.ds(c, rows, stride=dil) if dil > 1 else pl.ds(0, rows)


def _class_major_perm(tm, dil, inverse=False):
    cm_row = lax.broadcasted_iota(I32, (tm, tm), 1 if inverse else 0)
    tok_row = lax.broadcasted_iota(I32, (tm, tm), 0 if inverse else 1)
    rows = tm // dil
    return ((cm_row // rows == tok_row % dil) & (cm_row % rows == tok_row // dil)).astype(BF16)


def _attn_pre_kernel(x_ref, sc_ref, sh_ref, ksc_ref, ksh_ref, rot_ref, wq_ref, wkv_ref, *refs, dils, tm):
    x = x_ref[...]
    cos_t, sa_t, sb_t = rot_ref[0], rot_ref[1], rot_ref[2]
    q = _rotary(_bdot(x * (1.0 + sc_ref[...]) + sh_ref[...], wq_ref[...]), cos_t, sa_t, sb_t) * HD ** -0.5
    kv = _bdot(x * (1.0 + ksc_ref[...]) + ksh_ref[...], wkv_ref[...])
    k = _rotary(kv[:, :KVW], cos_t, sa_t, sb_t)
    v = kv[:, KVW:]
    k_ref, v_ref = refs[0], refs[1]
    k_ref[...] = k
    v_ref[...] = v
    if dils is None:
        refs[2][...] = q
        return
    ng = len(dils)
    q_refs, kb_refs, vb_refs = refs[2:2 + ng], refs[2 + ng:2 + 2 * ng], refs[2 + 2 * ng:2 + 3 * ng]
    qb, kb, vb = q.astype(BF16), k.astype(BF16), v.astype(BF16)
    for g, dil in enumerate(dils):
        rows = tm // dil
        blk = jnp.concatenate([qb[:, g * D:(g + 1) * D], kb, vb], axis=1)
        if dil > 1:
            blk = jnp.dot(_class_major_perm(tm, dil), blk, preferred_element_type=F32).astype(BF16)
        for c in range(dil):
            rs = slice(c * rows, (c + 1) * rows)
            q_refs[g][c] = blk[rs, :D]
            kb_refs[g][c] = blk[rs, D:D + KVW]
            vb_refs[g][c] = blk[rs, D + KVW:]


def _attn_pre(x, sc, sh, ksc, ksh, rot, w_q, w_kv, *, per_b_mod, tm, dils=None):
    bsz, t, _ = x.shape
    row = lambda wd: pl.BlockSpec((None, tm, wd), lambda b, i: (b, i, 0))
    mod = pl.BlockSpec((None, 1, D), lambda b, i: (b, 0, 0)) if per_b_mod else row(D)
    nq = w_q.shape[1]
    out_specs = [row(KVW), row(KVW)]
    out_shape = [jax.ShapeDtypeStruct((bsz, t, KVW), F32)] * 2
    scratch = []
    if dils is None:
        out_specs.append(row(nq))
        out_shape.append(jax.ShapeDtypeStruct((bsz, t, nq), F32))
    else:
        for wd in (D, KVW, KVW):
            for dil in dils:
                out_specs.append(pl.BlockSpec((None, dil, tm // dil, wd), lambda b, i: (b, 0, i, 0)))
                out_shape.append(jax.ShapeDtypeStruct((bsz, dil, t // dil, wd), BF16))
    outs = pl.pallas_call(
        functools.partial(_attn_pre_kernel, dils=dils, tm=tm),
        grid=(bsz, t // tm),
        in_specs=[row(D), mod, mod, mod, mod, pl.BlockSpec((3, tm, LANES), lambda b, i: (0, i, 0)),
                  pl.BlockSpec((D, nq), lambda b, i: (0, 0)), pl.BlockSpec((D, 2 * KVW), lambda b, i: (0, 0))],
        out_specs=out_specs,
        out_shape=out_shape,
        scratch_shapes=scratch,
        compiler_params=_cp(("arbitrary", "arbitrary")),
        name="attn_pre",
    )(x, sc, sh, ksc, ksh, rot, w_q, w_kv)
    if dils is None:
        return outs
    ng = len(dils)
    return outs[0], outs[1], outs[2:2 + ng], outs[2 + ng:2 + 2 * ng], outs[2 + 2 * ng:]


def _attn_band_kernel(q_ref, ka_ref, kb_ref, va_ref, vb_ref, o_ref, lse_ref):
    i = pl.program_id(2)
    bq = ATT_WIN
    kcat = jnp.concatenate([ka_ref[...], kb_ref[...]], axis=0)
    vcat = jnp.concatenate([va_ref[...], vb_ref[...]], axis=0)
    qs = jnp.concatenate([q_ref[:, rep * KVW:(rep + 1) * KVW] for rep in range(REP)], axis=0)
    qrow = lax.broadcasted_iota(I32, (REP * bq, 2 * bq), 0) % bq
    kcol = lax.broadcasted_iota(I32, (REP * bq, 2 * bq), 1)
    valid = (kcol >= qrow) & (kcol <= qrow + bq) & ((i > 0) | (kcol >= bq))
    lane = lax.broadcasted_iota(I32, (REP * bq, KVW), 1) // HD
    kv_lane = lax.broadcasted_iota(I32, (2 * bq, KVW), 1) // HD
    lse_lane = lax.broadcasted_iota(I32, (bq, LANES), 1)
    lse = jnp.zeros((bq, LANES), F32)
    linv = jnp.ones((REP * bq, KVW), F32)
    ps = []
    for h in range(KV_HEADS):
        kh = jnp.where(kv_lane == h, kcat, jnp.zeros_like(kcat))
        s = lax.dot_general(qs, kh, (((1,), (1,)), ((), ())), preferred_element_type=F32)
        s = jnp.where(valid, s, -jnp.inf)
        m = jnp.max(s, axis=-1, keepdims=True)
        p = jnp.exp(s - m)
        l = jnp.sum(p, axis=-1, keepdims=True)
        ps.append(p.astype(BF16))
        linv = jnp.where(lane == h, 1.0 / l, linv)
        lse_h = m + jnp.log(l)
        for rep in range(REP):
            lse = jnp.where(lse_lane == rep * KV_HEADS + h, lse_h[rep * bq:(rep + 1) * bq], lse)
    vstack = jnp.concatenate([jnp.where(kv_lane == h, vcat, jnp.zeros_like(vcat)) for h in range(KV_HEADS)],
                             axis=0)
    o = (jnp.dot(jnp.concatenate(ps, axis=1), vstack, preferred_element_type=F32) * linv).astype(BF16)
    for rep in range(REP):
        o_ref[:, rep * KVW:(rep + 1) * KVW] = o[rep * bq:(rep + 1) * bq]
    lse_ref[...] = lse


def _attn_band(q, kb, vb, dil):
    bsz, _, tc, _ = q.shape
    nb = tc // ATT_WIN
    blk = lambda wd: pl.BlockSpec((None, None, ATT_WIN, wd), lambda b, c, i: (b, c, i, 0))
    prev = pl.BlockSpec((None, None, ATT_WIN, KVW), lambda b, c, i: (b, c, jnp.maximum(i - 1, 0), 0))
    return pl.pallas_call(
        _attn_band_kernel,
        grid=(bsz, dil, nb),
        in_specs=[blk(D), prev, blk(KVW), prev, blk(KVW)],
        out_specs=[blk(D), blk(LANES)],
        out_shape=[jax.ShapeDtypeStruct((bsz, dil, tc, D), BF16), jax.ShapeDtypeStruct((bsz, dil, tc, LANES), F32)],
        compiler_params=_cp(("arbitrary", "arbitrary", "arbitrary")),
        name=f"attn_band_d{dil}",
    )(q, kb, kb, vb, vb)


def _attn_step_kernel(q_ref, kn_ref, vn_ref, kt_ref, vt_ref, o_ref, lse_ref, *, wbuf):
    nq = len(DILATED_GROUPS) * REP
    row = lax.broadcasted_iota(I32, (nq, wbuf), 0)
    pos = lax.broadcasted_iota(I32, (nq, wbuf), 1)
    valid = None
    for g, (_, dil) in enumerate(DILATED_GROUPS):
        ok = (row // REP == g) & (pos >= wbuf - ATT_WIN * dil) & (pos % dil == 0)
        valid = ok if valid is None else valid | ok
    rnd = lambda t: t.astype(BF16).astype(F32)
    lse_lane = lax.broadcasted_iota(I32, (nq, LANES), 1)
    lse = jnp.zeros((nq, LANES), F32)
    for h in range(KV_HEADS):
        hs = slice(h * HD, (h + 1) * HD)
        qh = q_ref[:, hs]
        s = jnp.where(valid, _bdot(qh, kt_ref[h]), -1e30)
        s_n = jnp.sum(rnd(qh) * rnd(kn_ref[:, hs]), axis=-1, keepdims=True)
        m = jnp.maximum(jnp.max(s, axis=-1, keepdims=True), s_n)
        p = jnp.exp(s - m)
        p_n = jnp.exp(s_n - m)
        l = jnp.sum(p, axis=-1, keepdims=True) + p_n
        o_ref[:, hs] = (_bdot_nt(p, vt_ref[h]) + rnd(p_n) * rnd(vn_ref[:, hs])) / l
        lse = jnp.where(lse_lane == h, m + jnp.log(l), lse)
    lse_ref[...] = lse


def _attn_step(q, k_new, v_new, cache_k, cache_v):
    bsz = q.shape[0]
    wbuf = cache_k.shape[1]
    ng = len(DILATED_GROUPS)
    assert all(wbuf % dil == 0 and wbuf >= ATT_WIN * dil for _, dil in DILATED_GROUPS)
    nq = ng * REP
    cache_spec = pl.BlockSpec((None, KV_HEADS, HD, wbuf), lambda b: (b, 0, 0, 0))
    o, lse = pl.pallas_call(
        functools.partial(_attn_step_kernel, wbuf=wbuf),
        grid=(bsz,),
        in_specs=[pl.BlockSpec((None, nq, KVW), lambda b: (b, 0, 0)),
                  pl.BlockSpec((None, 1, KVW), lambda b: (b, 0, 0)),
                  pl.BlockSpec((None, 1, KVW), lambda b: (b, 0, 0)), cache_spec, cache_spec],
        out_specs=[pl.BlockSpec((None, nq, KVW), lambda b: (b, 0, 0)),
                   pl.BlockSpec((None, nq, LANES), lambda b: (b, 0, 0))],
        out_shape=[jax.ShapeDtypeStruct((bsz, nq, KVW), F32), jax.ShapeDtypeStruct((bsz, nq, LANES), F32)],
        compiler_params=_cp(("arbitrary",)),
        name="attn_step",
    )(q.reshape(bsz, nq, KVW), k_new.reshape(bsz, 1, KVW), v_new.reshape(bsz, 1, KVW),
      cache_k.transpose(0, 2, 3, 1), cache_v.transpose(0, 2, 3, 1))
    o = o.reshape(bsz, ng, D).transpose(1, 0, 2)
    lse = lse[:, :, :KV_HEADS].reshape(bsz, ng, REP * KV_HEADS).transpose(1, 0, 2)
    return o, jnp.pad(lse, ((0, 0), (0, 0), (0, LANES - REP * KV_HEADS)))


def _attn_post_kernel(o0_ref, o1_ref, o2_ref, l0_ref, l1_ref, l2_ref, x_ref, gt_ref, vec_ref, wo_ref, out_ref,
                      *scratch, dils, tm):
    o_refs = [o0_ref, o1_ref, o2_ref]
    l_refs = [l0_ref, l1_ref, l2_ref]
    if dils is not None:
        for g, dil in enumerate(dils):
            if dil == 1:
                o_refs[g], l_refs[g] = o_refs[g].at[0], l_refs[g].at[0]
                continue
            sl = scratch[g]
            for c in range(dil):
                sl[_class_rows(c, tm // dil, dil), :] = l_refs[g][c]
            l_refs[g] = sl
            o_cm = jnp.concatenate([o_refs[g][c] for c in range(dil)], axis=0)
            o_refs[g] = jnp.dot(_class_major_perm(tm, dil, inverse=True), o_cm, preferred_element_type=F32)
    o0, o1, o2 = (r if isinstance(r, jax.Array) else r[...] for r in o_refs)
    l0, l1, l2 = (r[...] for r in l_refs)
    m = jnp.maximum(jnp.maximum(l0, l1), l2)
    e0, e1, e2 = jnp.exp(l0 - m), jnp.exp(l1 - m), jnp.exp(l2 - m)
    den = e0 + e1 + e2
    r = lax.broadcasted_iota(I32, (LANES, D), 0)
    c = lax.broadcasted_iota(I32, (LANES, D), 1) // HD
    spread = (r == c).astype(BF16)
    mix = (_split_dot(e0 / den, spread) * o0 + _split_dot(e1 / den, spread) * o1
           + _split_dot(e2 / den, spread) * o2)
    out = _bdot(mix, wo_ref[...])
    out_ref[...] = _layer_norm(DN_ALPHA * x_ref[...] + gt_ref[...] * out, vec_ref[0:1, :], vec_ref[1:2, :])


def _attn_post(os, lses, x, gt, vec, w_o, *, per_b_mod, tm, dils=None):
    bsz, t, _ = x.shape
    row = lambda wd: pl.BlockSpec((None, tm, wd), lambda b, i: (b, i, 0))
    mod = pl.BlockSpec((None, 1, D), lambda b, i: (b, 0, 0)) if per_b_mod else row(D)
    if dils is None:
        o_specs, l_specs, scratch = [row(D)] * 3, [row(LANES)] * 3, []
    else:
        cls = lambda dil, wd: pl.BlockSpec((None, dil, tm // dil, wd), lambda b, i: (b, 0, i, 0))
        o_specs = [cls(dil, D) for dil in dils]
        l_specs = [cls(dil, LANES) for dil in dils]
        scratch = [pltpu.VMEM((tm, LANES), F32) for _ in dils]
    return pl.pallas_call(
        functools.partial(_attn_post_kernel, dils=dils, tm=tm),
        grid=(bsz, t // tm),
        in_specs=o_specs + l_specs + [row(D), mod, pl.BlockSpec((2, D), lambda b, i: (0, 0)),
                                      pl.BlockSpec((D, D), lambda b, i: (0, 0))],
        out_specs=row(D),
        out_shape=jax.ShapeDtypeStruct(x.shape, F32),
        scratch_shapes=scratch,
        compiler_params=_cp(("arbitrary", "arbitrary")),
        name="attn_post",
    )(*os, *lses, x, gt, vec, w_o)


def _rot_tables(pos):
    half = ROT_DIM // 2
    inv = ROPE_THETA ** (-jnp.arange(half, dtype=F32) * 2.0 / ROT_DIM)
    ang = pos.astype(F32)[:, None] * inv[None, :]
    cos, sin = jnp.cos(ang), jnp.sin(ang)
    t = pos.shape[0]
    one = jnp.ones((t, HD - ROT_DIM), F32)
    zero = jnp.zeros((t, HD - ROT_DIM), F32)
    zh = jnp.zeros((t, half), F32)
    cos_t = jnp.concatenate([cos, cos, one], axis=1)
    sa_t = jnp.concatenate([-sin, zh, zero], axis=1)
    sb_t = jnp.concatenate([zh, sin, zero], axis=1)
    return jnp.stack([jnp.tile(x, (1, 2)) for x in (cos_t, sa_t, sb_t)])


def _prep_weights(p):
    pad_c = lambda m, n: jnp.pad(m, ((0, 0), (0, n - m.shape[1])))
    pad_r = lambda m, n: jnp.pad(m, ((0, n - m.shape[0]), (0, 0)))
    w = {}
    w["mu"] = p["rwkv_mu"][0]
    w["w_rkv"] = p["rwkv_w_rkv"][0].astype(BF16)
    w["w1"] = pad_c(p["rwkv_w1"][0], LANES).astype(BF16)
    w["w2"] = pad_r(p["rwkv_w2"][0], LANES).astype(BF16)
    w["a1"] = pad_c(p["rwkv_a1"][0], LANES).astype(BF16)
    w["a2"] = pad_r(p["rwkv_a2"][0], LANES).astype(BF16)
    w["g1"] = pad_c(p["rwkv_g1"][0], 2 * LANES).astype(BF16)
    w["g2"] = pad_r(p["rwkv_g2"][0], 2 * LANES).astype(BF16)
    w["vec"] = jnp.stack([p["rwkv_w0"][0], p["rwkv_a0"][0], p["rwkv_k_k"][0], p["rwkv_k_a"][0]])
    w["post_vec"] = jnp.stack([p["rwkv_r_k"][0].reshape(D), p["rwkv_lnx_g"][0], p["rwkv_lnx_b"][0],
                               p["ln_g"][0, 0], p["ln_b"][0, 0]])
    w["rwkv_w_o"] = p["rwkv_w_o"][0].astype(BF16)
    wq = p["w_q"][0].reshape(D, 3, KV_HEADS, REP, HD).transpose(0, 1, 3, 2, 4).reshape(D, 3 * D)
    w["w_q"] = wq.astype(BF16)
    w["w_kv"] = p["w_kv"].astype(BF16)
    wo = p["w_o_attn"][0].reshape(KV_HEADS, REP, HD, D).transpose(1, 0, 2, 3).reshape(D, D)
    w["w_o_attn"] = wo.astype(BF16)
    w["router_w_t"] = p["router_w"].T.astype(BF16)
    w["router_b"] = p["router_b"].reshape(N_EXPERTS, 1)
    w["moe_w_in"] = p["moe_w_in"]
    w["moe_w_out"] = p["moe_w_out"]
    w["ln"] = [[jnp.stack([p["ln_g"][l, i], p["ln_b"][l, i]]) for i in range(2)] for l in range(DEPTH)]
    return w


def _modulations(c_prompt, c_sample, p):
    nb = c_prompt.shape[0]
    c = jnp.concatenate([c_prompt, c_sample], axis=0)
    pad = (-c.shape[0]) % 8
    c = jnp.pad(c, ((0, pad), (0, 0)))
    m3 = _ada_linear(c, p["ada_w"].reshape(2 * DEPTH, D, 3 * D), p["ada_b"].reshape(2 * DEPTH, 3 * D))
    m2 = _ada_linear(c, p["kv_ada_w"][None], p["kv_ada_b"][None])[0]
    n_all = nb + c_sample.shape[0]

    def split(m, parts, lo, hi):
        return [m[lo:hi, i * D:(i + 1) * D] for i in range(parts)]

    out = {}
    for name, lo, hi in (("prompt", 0, nb), ("sample", nb, n_all)):
        out[name] = {"ada": [[split(m3[2 * l + i], 3, lo, hi) for i in range(2)] for l in range(DEPTH)],
                     "kv": split(m2, 2, lo, hi)}
    return out


def _trunk_prompt(x, mods, w):
    bsz, t, _ = x.shape
    per_b = lambda m: m[:, None, :]
    sh, sc, gt = (per_b(m) for m in mods["ada"][0][0])
    zeros = jnp.zeros((bsz, 1, D), F32)
    r, lw, k, v, a, b, g, last = _rwkv_pre(x, sc, sh, zeros, w, seq_mode=True, tm=256)
    y, zf = _wkv_chunked(r, lw, k, v, a, b)
    x = _rwkv_post(y, r, k, v, g, x, gt, w["post_vec"], w["rwkv_w_o"], pairs=True, tm=256)
    zf = zf.reshape(bsz, PAIRS, 2, HD, 2, HD)
    wkv = jnp.stack([zf[:, :, 0, :, 0, :], zf[:, :, 1, :, 1, :]], axis=2)
    wkv = wkv.reshape(bsz, HEADS, HD, HD).transpose(0, 1, 3, 2)
    sh2, sc2, gt2 = (per_b(m) for m in mods["ada"][0][1])
    x = _moe_sorted(x, sc2, sh2, gt2, w["ln"][0][1], w, 0)

    sh, sc, gt = (per_b(m) for m in mods["ada"][1][0])
    ksh, ksc = (per_b(m) for m in mods["kv"])
    rot = _rot_tables(jnp.arange(t, dtype=I32))
    dils = tuple(dil for _, dil in DILATED_GROUPS)
    k_new, v_new, qs, kbs, vbs = _attn_pre(x, sc, sh, ksc, ksh, rot, w["w_q"], w["w_kv"], per_b_mod=True,
                                           tm=256, dils=dils)
    os, lses = zip(*[_attn_band(qs[g], kbs[g], vbs[g], dil) for g, dil in enumerate(dils)])
    x = _attn_post(os, lses, x, gt, w["ln"][1][0], w["w_o_attn"], per_b_mod=True, tm=256, dils=dils)
    sh2, sc2, gt2 = (per_b(m) for m in mods["ada"][1][1])
    x = _moe_sorted(x, sc2, sh2, gt2, w["ln"][1][1], w, 1)
    return x, wkv[None], last.reshape(1, bsz, D), k_new, v_new


def _trunk_sample(x, mods, state_wkv, state_shift, cache_k, cache_v, w):
    bsz = x.shape[0]
    xs = x.reshape(1, bsz, D)
    row = lambda m: m[None]
    sh, sc, gt = (row(m) for m in mods["ada"][0][0])
    r, lw, k, v, a, b, g, hm = _rwkv_pre(xs, sc, sh, state_shift[0][None], w, seq_mode=False, tm=bsz)
    s_new, y = _wkv_step(state_wkv[0].transpose(1, 2, 3, 0), *(t[0].T for t in (r, lw, k, v, a, b)))
    s_new = s_new.transpose(3, 0, 1, 2)
    xs = _rwkv_post(y.T[None], r, k, v, g, xs, gt, w["post_vec"], w["rwkv_w_o"], pairs=False, tm=bsz)
    sh2, sc2, gt2 = (row(m) for m in mods["ada"][0][1])
    xs = _moe_dense(xs, sc2, sh2, gt2, w["ln"][0][1], w, 0)

    sh, sc, gt = (row(m) for m in mods["ada"][1][0])
    ksh, ksc = (row(m) for m in mods["kv"])
    rot = _rot_tables(jnp.full((bsz,), PAST_LEN, I32))
    k_new, v_new, q = _attn_pre(xs, sc, sh, ksc, ksh, rot, w["w_q"], w["w_kv"], per_b_mod=False, tm=bsz)
    o, lse = _attn_step(q[0], k_new[0], v_new[0], cache_k, cache_v)
    xs = _attn_post([o[i][None] for i in range(3)], [lse[i][None] for i in range(3)], xs, gt, w["ln"][1][0],
                    w["w_o_attn"], per_b_mod=False, tm=bsz)
    sh2, sc2, gt2 = (row(m) for m in mods["ada"][1][1])
    xs = _moe_dense(xs, sc2, sh2, gt2, w["ln"][1][1], w, 1)
    return (xs.reshape(bsz, 1, D), s_new[None], hm, k_new.reshape(bsz, 1, KV_HEADS, HD),
            v_new.reshape(bsz, 1, KV_HEADS, HD))


def kernel(x_prompt, x_sample, state_wkv, state_shift, cache_k, cache_v, c_prompt, c_sample, ada_w, ada_b, ln_g, ln_b, rwkv_mu, rwkv_w_rkv, rwkv_w0, rwkv_w1, rwkv_w2, rwkv_a0, rwkv_a1, rwkv_a2, rwkv_g1, rwkv_g2, rwkv_k_k, rwkv_k_a, rwkv_r_k, rwkv_lnx_g, rwkv_lnx_b, rwkv_w_o, w_q, w_kv, kv_ada_w, kv_ada_b, w_o_attn, router_w, router_b, moe_w_in, moe_w_out):
    p = {"ada_w": ada_w, "ada_b": ada_b, "ln_g": ln_g, "ln_b": ln_b, "rwkv_mu": rwkv_mu, "rwkv_w_rkv": rwkv_w_rkv,
         "rwkv_w0": rwkv_w0, "rwkv_w1": rwkv_w1, "rwkv_w2": rwkv_w2, "rwkv_a0": rwkv_a0, "rwkv_a1": rwkv_a1,
         "rwkv_a2": rwkv_a2, "rwkv_g1": rwkv_g1, "rwkv_g2": rwkv_g2, "rwkv_k_k": rwkv_k_k, "rwkv_k_a": rwkv_k_a,
         "rwkv_r_k": rwkv_r_k, "rwkv_lnx_g": rwkv_lnx_g, "rwkv_lnx_b": rwkv_lnx_b, "rwkv_w_o": rwkv_w_o,
         "w_q": w_q, "w_kv": w_kv, "kv_ada_w": kv_ada_w, "kv_ada_b": kv_ada_b, "w_o_attn": w_o_attn,
         "router_w": router_w, "router_b": router_b, "moe_w_in": moe_w_in, "moe_w_out": moe_w_out}
    w = _prep_weights(p)
    mods = _modulations(c_prompt, c_sample, p)
    bp, tp, _ = x_prompt.shape
    y_p, wkv_p, shift_p, k_p, v_p = _trunk_prompt(x_prompt, mods["prompt"], w)
    keep = min(PAST_LEN, tp)
    k_p = k_p[:, tp - keep:].reshape(bp, keep, KV_HEADS, HD)
    v_p = v_p[:, tp - keep:].reshape(bp, keep, KV_HEADS, HD)
    y_s, wkv_s, shift_s, k_s, v_s = _trunk_sample(x_sample, mods["sample"], state_wkv, state_shift, cache_k,
                                                  cache_v, w)
    return (y_p, y_s, wkv_p, shift_p, k_p, v_p, wkv_s, shift_s, k_s, v_s)
```

```python
import functools
import math

import jax
import jax.numpy as jnp
from jax import lax
from jax.experimental import pallas as pl
from jax.experimental.pallas import tpu as pltpu

F32 = jnp.float32
BF16 = jnp.bfloat16
I32 = jnp.int32

D = 1024
HEADS = 16
HD = 64
LANES = 128
PAIRS = D // LANES
CHUNK = 64
WKV_GROUP = 8
N_EXPERTS = 32
EXPERTS_PER_GROUP = 8
N_EXPERT_GROUPS = 4
D_EXPERT = 512
MOE_BLK = 256
DMA_UNROLL = 8
KV_HEADS = 4
REP = 4
KVW = KV_HEADS * HD
DILATED_GROUPS = ((128, 1), (512, 4), (2048, 16))
ATT_WIN = 128
PAST_LEN = 2048
ROT_DIM = 16
ROPE_THETA = 500000.0
DEPTH = 2
DN_ALPHA = (2 * DEPTH) ** 0.25
LN_EPS = 1e-5
GN_EPS = 64e-5
VMEM_LIMIT = 56 * 1024 * 1024


def _cp(sem):
    return pltpu.CompilerParams(dimension_semantics=sem, vmem_limit_bytes=VMEM_LIMIT)


def _bdot(a, b):
    return jnp.dot(a.astype(BF16), b.astype(BF16), preferred_element_type=F32)


def _bdot_nt(a, b):
    return lax.dot_general(a.astype(BF16), b.astype(BF16), (((1,), (1,)), ((), ())),
                           preferred_element_type=F32)


def _bdot_tn(a, b):
    return lax.dot_general(a.astype(BF16), b.astype(BF16), (((0,), (0,)), ((), ())),
                           preferred_element_type=F32)


def _split_dot(x, m):
    hi = x.astype(BF16)
    lo = (x - hi.astype(F32)).astype(BF16)
    return jnp.dot(jnp.concatenate([hi, lo], axis=1), jnp.concatenate([m, m], axis=0),
                   preferred_element_type=F32)


def _head_ones():
    r = lax.broadcasted_iota(I32, (LANES, LANES), 0) // HD
    c = lax.broadcasted_iota(I32, (LANES, LANES), 1) // HD
    return (r == c).astype(BF16)


def _segsum(x, ones_bd):
    cols = [_split_dot(x[:, c * LANES:(c + 1) * LANES], ones_bd) for c in range(x.shape[1] // LANES)]
    return cols[0] if len(cols) == 1 else jnp.concatenate(cols, axis=1)


def _layer_norm(x, g, b):
    mu = jnp.mean(x, axis=-1, keepdims=True)
    xc = x - mu
    var = jnp.mean(xc * xc, axis=-1, keepdims=True)
    return xc * lax.rsqrt(var + LN_EPS) * g + b


def _sigmoid(x):
    return 1.0 / (1.0 + jnp.exp(-x))


def _to_pairs(ref, val):
    for p in range(PAIRS):
        ref[p] = val[:, p * LANES:(p + 1) * LANES]


def _from_pairs(ref):
    return jnp.concatenate([ref[p] for p in range(PAIRS)], axis=1)


def _ada_kernel(c_ref, w_ref, b_ref, o_ref):
    c = c_ref[...]
    o_ref[...] = _bdot(c * _sigmoid(c), w_ref[...]) + b_ref[...]


def _ada_linear(c, w, b, tn=512):
    s, _, n = w.shape
    m = c.shape[0]
    return pl.pallas_call(
        _ada_kernel,
        grid=(s, n // tn),
        in_specs=[pl.BlockSpec((m, D), lambda i, j: (0, 0)),
                  pl.BlockSpec((None, D, tn), lambda i, j: (i, 0, j)),
                  pl.BlockSpec((None, 1, tn), lambda i, j: (i, 0, j))],
        out_specs=pl.BlockSpec((None, m, tn), lambda i, j: (i, 0, j)),
        out_shape=jax.ShapeDtypeStruct((s, m, n), F32),
        compiler_params=_cp(("arbitrary", "arbitrary")),
        name="ada_linear",
    )(c, w, b.reshape(s, 1, n))


def _rwkv_pre_kernel(x_ref, sc_ref, sh_ref, prev_ref, mu_ref, wrkv_ref, w1_ref, w2_ref, a1_ref, a2_ref,
                     g1_ref, g2_ref, vec_ref,
                     r_ref, lw_ref, k_ref, v_ref, a_ref, b_ref, g_ref, hm_ref, carry_ref, *, seq_mode, tm):
    x = x_ref[...]
    hm = x * (1.0 + sc_ref[...]) + sh_ref[...]
    if seq_mode:
        @pl.when(pl.program_id(1) == 0)
        def _():
            carry_ref[...] = prev_ref[...]
        row = lax.broadcasted_iota(I32, hm.shape, 0)
        hprev = jnp.where(row == 0, carry_ref[...], pltpu.roll(hm, 1, axis=0))
        carry_ref[...] = hm[tm - 1:tm, :]
        hm_ref[...] = hm[tm - 1:tm, :]
    else:
        hprev = prev_ref[...]
        hm_ref[...] = hm
    xx = hprev - hm

    def mix(i):
        return hm + xx * mu_ref[i:i + 1, :]

    w0, a0, k_k, k_a = (vec_ref[i:i + 1, :] for i in range(4))
    r = _bdot(mix(0), wrkv_ref[0])
    k = _bdot(mix(2), wrkv_ref[1])
    v = _bdot(mix(3), wrkv_ref[2])
    wl = w0 + _bdot(jnp.tanh(_bdot(mix(1), w1_ref[...])), w2_ref[...])
    lw = -math.exp(-0.5) * _sigmoid(wl)
    a_lr = _sigmoid(a0 + _bdot(_bdot(mix(4), a1_ref[...]), a2_ref[...]))
    g = _bdot(_sigmoid(_bdot(mix(5), g1_ref[...])), g2_ref[...])
    kk = k * k_k
    kk = kk * jnp.minimum(lax.rsqrt(_segsum(kk * kk, _head_ones())), 1e12)
    kmod = k * (1.0 + (a_lr - 1.0) * k_a)
    outs = ((r_ref, r), (lw_ref, lw), (k_ref, kmod), (v_ref, v), (a_ref, -kk), (b_ref, kk * a_lr), (g_ref, g))
    for ref, val in outs:
        if seq_mode:
            _to_pairs(ref, val)
        else:
            ref[...] = val


def _rwkv_pre(x, sc, sh, prev, w, *, seq_mode, tm):
    bsz, t, _ = x.shape
    grid = (bsz, t // tm)
    row = pl.BlockSpec((None, tm, D), lambda b, i: (b, i, 0))
    per_b = pl.BlockSpec((None, 1, D), lambda b, i: (b, 0, 0))
    mod = per_b if seq_mode else row

    def const(shape):
        return pl.BlockSpec(shape, lambda b, i: (0,) * len(shape))

    if seq_mode:
        out_big = pl.BlockSpec((None, PAIRS, tm, LANES), lambda b, i: (b, 0, i, 0))
        big_shape = jax.ShapeDtypeStruct((bsz, PAIRS, t, LANES), F32)
        hm_spec, hm_shape = per_b, jax.ShapeDtypeStruct((bsz, 1, D), F32)
    else:
        out_big, big_shape = row, jax.ShapeDtypeStruct((bsz, t, D), F32)
        hm_spec, hm_shape = row, jax.ShapeDtypeStruct((bsz, t, D), F32)
    return pl.pallas_call(
        functools.partial(_rwkv_pre_kernel, seq_mode=seq_mode, tm=tm),
        grid=grid,
        in_specs=[row, mod, mod, mod, const((6, D)), const((3, D, D)), const((D, LANES)), const((LANES, D)),
                  const((D, LANES)), const((LANES, D)), const((D, 2 * LANES)), const((2 * LANES, D)),
                  const((4, D))],
        out_specs=[out_big] * 7 + [hm_spec],
        out_shape=[big_shape] * 7 + [hm_shape],
        scratch_shapes=[pltpu.VMEM((1, D), F32)],
        compiler_params=_cp(("arbitrary", "arbitrary")),
        name="rwkv_pre",
    )(x, sc, sh, prev, w["mu"], w["w_rkv"], w["w1"], w["w2"], w["a1"], w["a2"], w["g1"], w["g2"], w["vec"])


def _wkv_chunk_kernel(r_ref, lw_ref, k_ref, v_ref, a_ref, b_ref, y_ref, zf_ref, z_ref, *, n_pairs, group):
    c = pl.program_id(0)

    @pl.when(c == 0)
    def _():
        z_ref[...] = jnp.zeros_like(z_ref)

    L = CHUNK
    row_l = lax.broadcasted_iota(I32, (L, L), 0)
    col_l = lax.broadcasted_iota(I32, (L, L), 1)
    tri_incl = (row_l >= col_l).astype(BF16)
    lane = lax.broadcasted_iota(I32, (L, LANES), 1)
    head0 = lane < HD
    row = lax.broadcasted_iota(I32, (LANES, LANES), 0)
    col = lax.broadcasted_iota(I32, (LANES, LANES), 1)
    strict = row > col
    incl = row >= col
    eye = (row == col).astype(F32)

    def expand(x):
        return jnp.concatenate([jnp.where(head0, x, 0.0), jnp.where(head0, 0.0, x)], axis=0)

    def group_body(gi, carry):
        ids = [gi * group + j for j in range(group)]
        bp = [(i // PAIRS, i % PAIRS) for i in ids]
        G = range(group)
        r = [r_ref[b_i, p_i] for b_i, p_i in bp]
        lw = [lw_ref[b_i, p_i] for b_i, p_i in bp]
        k = [k_ref[b_i, p_i] for b_i, p_i in bp]
        v = [v_ref[b_i, p_i] for b_i, p_i in bp]
        a = [a_ref[b_i, p_i] for b_i, p_i in bp]
        b = [b_ref[b_i, p_i] for b_i, p_i in bp]
        z0 = [z_ref[i] for i in ids]
        cum = [_split_dot_left(tri_incl, lw[j]) for j in G]
        cum_l = [cum[j][L - 1:L, :] for j in G]
        inv = [jnp.exp(-cum[j]) for j in G]
        tail = [jnp.exp(cum_l[j] - cum[j]) for j in G]
        a_e = [expand(a[j] * jnp.exp(cum[j] - lw[j])) for j in G]
        r_e = [expand(r[j] * jnp.exp(cum[j])) for j in G]
        b_e = [expand(b[j] * inv[j]) for j in G]
        k_e = [expand(k[j] * inv[j]) for j in G]
        bd_e = [expand(b[j] * tail[j]) for j in G]
        kd_e = [expand(k[j] * tail[j]) for j in G]
        v_e = [expand(v[j]) for j in G]
        gm = [_bdot_nt(jnp.concatenate([a_e[j], r_e[j]], axis=0), jnp.concatenate([b_e[j], k_e[j]], axis=0))
              for j in G]
        m_ab = [jnp.where(strict, gm[j][:LANES, :LANES], 0.0) for j in G]
        m_ak = [jnp.where(strict, gm[j][:LANES, LANES:], 0.0) for j in G]
        m_rb = [jnp.where(incl, gm[j][LANES:, :LANES], 0.0) for j in G]
        m_rk = [jnp.where(incl, gm[j][LANES:, LANES:], 0.0) for j in G]
        tinv = [eye + m_ab[j] for j in G]
        pw = [_bdot(m_ab[j], m_ab[j]) for j in G]
        for _ in range(4):
            both = [_bdot(pw[j], jnp.concatenate([pw[j], tinv[j]], axis=1)) for j in G]
            pw = [both[j][:, :LANES] for j in G]
            tinv = [tinv[j] + both[j][:, LANES:] for j in G]
        tinv = [tinv[j] + _bdot(pw[j], tinv[j]) for j in G]
        rhs = [_bdot(a_e[j], z0[j]) + _bdot(m_ak[j], v_e[j]) for j in G]
        u = [_bdot(tinv[j], rhs[j]) for j in G]
        zuv = [jnp.concatenate([z0[j], u[j], v_e[j]], axis=0) for j in G]
        decay = [eye * jnp.exp(cum_l[j]) for j in G]
        for j in G:
            z_ref[ids[j]] = _bdot_tn(jnp.concatenate([decay[j], bd_e[j], kd_e[j]], axis=0), zuv[j])
            y_e = _bdot(jnp.concatenate([r_e[j], m_rb[j], m_rk[j]], axis=1), zuv[j])
            y_ref[bp[j][0], bp[j][1]] = y_e[:L] + y_e[L:]
        return carry

    lax.fori_loop(0, n_pairs // group, group_body, 0)

    @pl.when(c == pl.num_programs(0) - 1)
    def _():
        zf_ref[...] = z_ref[...]


def _split_dot_left(m, x):
    hi = x.astype(BF16)
    lo = (x - hi.astype(F32)).astype(BF16)
    return jnp.dot(jnp.concatenate([m, m], axis=1), jnp.concatenate([hi, lo], axis=0),
                   preferred_element_type=F32)


def _wkv_chunked(r, lw, k, v, a, b):
    bsz, _, t, _ = r.shape
    n_pairs = bsz * PAIRS
    blk = pl.BlockSpec((bsz, PAIRS, CHUNK, LANES), lambda c: (0, 0, c, 0))
    return pl.pallas_call(
        functools.partial(_wkv_chunk_kernel, n_pairs=n_pairs, group=WKV_GROUP),
        grid=(t // CHUNK,),
        in_specs=[blk] * 6,
        out_specs=[blk, pl.BlockSpec((n_pairs, LANES, LANES), lambda c: (0, 0, 0))],
        out_shape=[jax.ShapeDtypeStruct(r.shape, F32), jax.ShapeDtypeStruct((n_pairs, LANES, LANES), F32)],
        scratch_shapes=[pltpu.VMEM((n_pairs, LANES, LANES), F32)],
        compiler_params=_cp(("arbitrary",)),
        name="wkv_chunked",
    )(r, lw, k, v, a, b)


def _wkv_step_kernel(s_ref, r_ref, lw_ref, k_ref, v_ref, a_ref, b_ref, so_ref, y_ref):
    w = jnp.exp(lw_ref[...])
    a, b, k, r = a_ref[...], b_ref[...], k_ref[...], r_ref[...]

    def body(i, c):
        s = s_ref[i]
        sa = jnp.sum(s * a, axis=0, keepdims=True)
        s_new = s * w + sa * b + v_ref[pl.ds(i, 1), :] * k
        so_ref[i] = s_new
        y_ref[pl.ds(i, 1), :] = jnp.sum(s_new * r, axis=0, keepdims=True)
        return c

    lax.fori_loop(0, HD, body, 0, unroll=4)


def _wkv_step(state_t, r, lw, k, v, a, b):
    bsz = state_t.shape[-1]
    big = pl.BlockSpec((None, HD, HD, bsz), lambda h: (h, 0, 0, 0))
    vec = pl.BlockSpec((HD, bsz), lambda h: (h, 0))
    return pl.pallas_call(
        _wkv_step_kernel,
        grid=(HEADS,),
        in_specs=[big] + [vec] * 6,
        out_specs=[big, vec],
        out_shape=[jax.ShapeDtypeStruct(state_t.shape, F32), jax.ShapeDtypeStruct((D, bsz), F32)],
        compiler_params=_cp(("arbitrary",)),
        name="wkv_step",
    )(state_t, r, lw, k, v, a, b)


def _rwkv_post_kernel(y_ref, r_ref, k_ref, v_ref, g_ref, x_ref, gt_ref, vec_ref, wo_ref, o_ref, *, pairs):
    load = _from_pairs if pairs else (lambda ref: ref[...])
    y, r, k, v, g = (load(ref) for ref in (y_ref, r_ref, k_ref, v_ref, g_ref))
    r_k, lnx_g, lnx_b, ln_g, ln_b = (vec_ref[i:i + 1, :] for i in range(5))
    ones_bd = _head_ones()
    ym = _segsum(y, ones_bd) * (1.0 / HD)
    yc = y - ym
    yv = _segsum(yc * yc, ones_bd) * (1.0 / HD)
    yn = yc * lax.rsqrt(yv + GN_EPS) * lnx_g + lnx_b
    bonus = _segsum(r * k * r_k, ones_bd) * v
    mix = _bdot((yn + bonus) * g, wo_ref[...])
    o_ref[...] = _layer_norm(DN_ALPHA * x_ref[...] + gt_ref[...] * mix, ln_g, ln_b)


def _rwkv_post(y, r, k, v, g, x, gt, vec, w_o, *, pairs, tm):
    bsz, t, _ = x.shape
    row = pl.BlockSpec((None, tm, D), lambda b, i: (b, i, 0))
    big = pl.BlockSpec((None, PAIRS, tm, LANES), lambda b, i: (b, 0, i, 0)) if pairs else row
    mod = pl.BlockSpec((None, 1, D), lambda b, i: (b, 0, 0)) if pairs else row
    return pl.pallas_call(
        functools.partial(_rwkv_post_kernel, pairs=pairs),
        grid=(bsz, t // tm),
        in_specs=[big] * 5 + [row, mod, pl.BlockSpec((5, D), lambda b, i: (0, 0)),
                              pl.BlockSpec((D, D), lambda b, i: (0, 0))],
        out_specs=row,
        out_shape=jax.ShapeDtypeStruct(x.shape, F32),
        compiler_params=_cp(("arbitrary", "arbitrary")),
        name="rwkv_post",
    )(y, r, k, v, g, x, gt, vec, w_o)


def _top2(v):
    io = lax.broadcasted_iota(I32, v.shape, 0)
    m1 = jnp.max(v, axis=0, keepdims=True)
    i1 = jnp.min(jnp.where(v == m1, io, EXPERTS_PER_GROUP), axis=0, keepdims=True)
    v2 = jnp.where(io == i1, -jnp.inf, v)
    m2 = jnp.max(v2, axis=0, keepdims=True)
    i2 = jnp.min(jnp.where(v2 == m2, io, EXPERTS_PER_GROUP), axis=0, keepdims=True)
    return m1 + m2, i1, i2


def _moe_route_kernel(x_ref, sc_ref, sh_ref, rw_ref, rb_ref, h_ref, idx_ref, gate_ref, gmat_ref, cnt_ref,
                      carry_ref, *, tm, rows_out):
    @pl.when((pl.program_id(0) == 0) & (pl.program_id(1) == 0))
    def _():
        carry_ref[...] = jnp.zeros_like(carry_ref)

    h = x_ref[...] * (1.0 + sc_ref[...]) + sh_ref[...]
    h_ref[...] = h
    aff = _sigmoid(_bdot_nt(rw_ref[...], h))
    sel = aff + rb_ref[...]
    best = gi = i1 = i2 = None
    for g in range(N_EXPERT_GROUPS):
        sc, j1, j2 = _top2(sel[g * EXPERTS_PER_GROUP:(g + 1) * EXPERTS_PER_GROUP, :])
        if g == 0:
            best, gi, i1, i2 = sc, jnp.zeros_like(j1), j1, j2
        else:
            upd = sc > best
            best = jnp.where(upd, sc, best)
            gi = jnp.where(upd, g, gi)
            i1 = jnp.where(upd, j1, i1)
            i2 = jnp.where(upd, j2, i2)
    e0 = gi * EXPERTS_PER_GROUP + i1
    e1 = gi * EXPERTS_PER_GROUP + i2
    io = lax.broadcasted_iota(I32, (N_EXPERTS, tm), 0)
    oh0 = io == e0
    oh1 = io == e1
    a0 = jnp.sum(jnp.where(oh0, aff, 0.0), axis=0, keepdims=True)
    a1 = jnp.sum(jnp.where(oh1, aff, 0.0), axis=0, keepdims=True)
    den = a0 + a1
    g0 = a0 / den
    g1 = a1 / den
    oh = jnp.where(oh0 | oh1, 1.0, 0.0)
    tr = lax.broadcasted_iota(I32, (tm, tm), 0)
    tc = lax.broadcasted_iota(I32, (tm, tm), 1)
    before = jnp.dot(oh.astype(BF16), (tr < tc).astype(BF16), preferred_element_type=F32) + carry_ref[...]
    rank0 = jnp.sum(jnp.where(oh0, before, 0.0), axis=0, keepdims=True)
    rank1 = jnp.sum(jnp.where(oh1, before, 0.0), axis=0, keepdims=True)
    carry_ref[...] = carry_ref[...] + jnp.sum(oh, axis=1, keepdims=True)
    zi = jnp.zeros((4, tm), I32)
    idx_ref[...] = jnp.concatenate([e0, e1, rank0.astype(I32), rank1.astype(I32), zi], axis=0)
    gate_ref[...] = jnp.concatenate([g0, g1, jnp.zeros((6, tm), F32)], axis=0)
    gmat_ref[...] = jnp.where(oh0, g0, 0.0) + jnp.where(oh1, g1, 0.0)
    cnt_ref[...] = jnp.broadcast_to(carry_ref[...], cnt_ref.shape)


def _moe_route(x, sc, sh, router_w_t, router_b, *, per_b_mod, rows_out, tm):
    bsz, t, _ = x.shape
    n = bsz * t
    nt = t // tm
    row = pl.BlockSpec((None, tm, D), lambda b, i: (b, i, 0))
    mod = pl.BlockSpec((None, 1, D), lambda b, i: (b, 0, 0)) if per_b_mod else row
    if rows_out:
        h_spec = pl.BlockSpec((tm, D), lambda b, i: (b * nt + i, 0))
        h_shape = jax.ShapeDtypeStruct((n, D), F32)
    else:
        h_spec, h_shape = row, jax.ShapeDtypeStruct(x.shape, F32)
    tok = lambda rows: pl.BlockSpec((rows, tm), lambda b, i: (0, b * nt + i))
    return pl.pallas_call(
        functools.partial(_moe_route_kernel, tm=tm, rows_out=rows_out),
        grid=(bsz, nt),
        in_specs=[row, mod, mod, pl.BlockSpec((N_EXPERTS, D), lambda b, i: (0, 0)),
                  pl.BlockSpec((N_EXPERTS, 1), lambda b, i: (0, 0))],
        out_specs=[h_spec, tok(8), tok(8), tok(N_EXPERTS), pl.BlockSpec((N_EXPERTS, LANES), lambda b, i: (0, 0))],
        out_shape=[h_shape, jax.ShapeDtypeStruct((8, n), I32), jax.ShapeDtypeStruct((8, n), F32),
                   jax.ShapeDtypeStruct((N_EXPERTS, n), F32), jax.ShapeDtypeStruct((N_EXPERTS, LANES), F32)],
        scratch_shapes=[pltpu.VMEM((N_EXPERTS, 1), F32)],
        compiler_params=_cp(("arbitrary", "arbitrary")),
        name="moe_route",
    )(x, sc, sh, router_w_t, router_b)


def _row_copy(src, src_row, dst, dst_row, sem):
    return pltpu.make_async_copy(src.at[pl.ds(src_row, 1), :], dst.at[pl.ds(dst_row, 1), :], sem)


def _wait_rows(src, dst, rows, sem):
    pltpu.make_async_copy(src.at[pl.ds(0, rows), :], dst.at[pl.ds(0, rows), :], sem).wait()


def _moe_dispatch_kernel(slot_ref, zs_ref, h_ref, xs_ref, zbuf, sem, *, tm, n):
    base = pl.program_id(0) * tm

    @pl.when(pl.program_id(0) == 0)
    def _():
        zbuf[...] = jnp.zeros_like(zbuf)

        def block_copy(e):
            return pltpu.make_async_copy(zbuf, xs_ref.at[pl.ds(pl.multiple_of(zs_ref[e], 8), MOE_BLK), :],
                                         sem.at[2])

        fresh = [zs_ref[e] != zs_ref[e - 1] for e in range(1, zs_ref.shape[0])]
        block_copy(0).start()
        for e, f in enumerate(fresh, start=1):
            pl.when(f)(lambda e=e: block_copy(e).start())
        block_copy(0).wait()
        for e, f in enumerate(fresh, start=1):
            pl.when(f)(lambda e=e: block_copy(e).wait())

    def issue(i, c):
        _row_copy(h_ref, i, xs_ref, slot_ref[base + i], sem.at[0]).start(priority=0)
        _row_copy(h_ref, i, xs_ref, slot_ref[n + base + i], sem.at[1]).start(priority=1)
        return c

    lax.fori_loop(0, tm, issue, 0, unroll=DMA_UNROLL)
    _wait_rows(h_ref, xs_ref, tm, sem.at[0])
    _wait_rows(h_ref, xs_ref, tm, sem.at[1])


def _moe_dispatch(h_rows, slots, zero_start, n_slots, tm):
    n = h_rows.shape[0]
    return pl.pallas_call(
        functools.partial(_moe_dispatch_kernel, tm=tm, n=n),
        grid_spec=pltpu.PrefetchScalarGridSpec(
            num_scalar_prefetch=2, grid=(n // tm,),
            in_specs=[pl.BlockSpec((tm, D), lambda i, sl, zs: (i, 0))],
            out_specs=pl.BlockSpec(memory_space=pl.ANY),
            scratch_shapes=[pltpu.VMEM((MOE_BLK, D), F32), pltpu.SemaphoreType.DMA((3,))]),
        out_shape=jax.ShapeDtypeStruct((n_slots, D), F32),
        compiler_params=_cp(("arbitrary",)),
        name="moe_dispatch",
    )(slots, zero_start, h_rows)


def _moe_expert_kernel(be_ref, used_ref, nxt_ref, xs_ref, win_hbm, wout_hbm, ys_ref,
                       win_f, wout_f, win_bf, wout_bf, run_ref, sem, *, layer):
    i = pl.program_id(0)
    used = used_ref[0]

    def weight_copies(e, s):
        return (pltpu.make_async_copy(win_hbm.at[layer, e], win_f.at[s], sem.at[s, 0]),
                pltpu.make_async_copy(wout_hbm.at[layer, e], wout_f.at[s], sem.at[s, 1]))

    @pl.when(i == 0)
    def _():
        run_ref[0] = 0
        for cp in weight_copies(be_ref[0], 0):
            cp.start()

    fresh = (i == 0) | (be_ref[i] != be_ref[jnp.maximum(i - 1, 0)])

    @pl.when(fresh & (i < used))
    def _():
        run = run_ref[0] + jnp.where(i > 0, 1, 0)
        run_ref[0] = run
        s = run % 2
        for cp in weight_copies(be_ref[i], s):
            cp.wait()
        win_bf[...] = win_f[s].astype(BF16)
        wout_bf[...] = wout_f[s].astype(BF16)

        @pl.when(nxt_ref[i] >= 0)
        def _():
            for cp in weight_copies(nxt_ref[i], 1 - s):
                cp.start()

    @pl.when(i < used)
    def _():
        hmid = jnp.dot(xs_ref[...].astype(BF16), win_bf[...], preferred_element_type=F32)
        gt = hmid[:, :D_EXPERT]
        up = hmid[:, D_EXPERT:]
        act = gt * _sigmoid(gt) * up
        ys_ref[...] = jnp.dot(act.astype(BF16), wout_bf[...], preferred_element_type=F32)

    @pl.when(i >= used_ref[0])
    def _():
        ys_ref[...] = jnp.zeros_like(ys_ref)


def _moe_experts_sorted(xs, blk_exp, n_used, w_in, w_out, layer):
    n_slots = xs.shape[0]
    n_blocks = n_slots // MOE_BLK
    blk = jnp.arange(n_blocks, dtype=I32)
    later = (blk[None, :] > blk[:, None]) & (blk_exp[None, :] != blk_exp[:, None]) & (blk[None, :] < n_used[0])
    first_later = jnp.min(jnp.where(later, blk[None, :], n_blocks), axis=1)
    nxt = jnp.where(first_later < n_blocks, blk_exp[jnp.minimum(first_later, n_blocks - 1)], -1).astype(I32)
    x_map = lambda i, be, u, nx: (jnp.minimum(i, u[0] - 1), 0)
    return pl.pallas_call(
        functools.partial(_moe_expert_kernel, layer=layer),
        grid_spec=pltpu.PrefetchScalarGridSpec(
            num_scalar_prefetch=3, grid=(n_blocks,),
            in_specs=[pl.BlockSpec((MOE_BLK, D), x_map),
                      pl.BlockSpec(memory_space=pl.ANY), pl.BlockSpec(memory_space=pl.ANY)],
            out_specs=pl.BlockSpec((MOE_BLK, D), lambda i, be, u, nx: (i, 0)),
            scratch_shapes=[pltpu.VMEM((2, D, 2 * D_EXPERT), F32), pltpu.VMEM((2, D_EXPERT, D), F32),
                            pltpu.VMEM((D, 2 * D_EXPERT), BF16), pltpu.VMEM((D_EXPERT, D), BF16),
                            pltpu.SMEM((1,), I32), pltpu.SemaphoreType.DMA((2, 2))]),
        out_shape=jax.ShapeDtypeStruct((n_slots, D), F32),
        compiler_params=_cp(("arbitrary",)),
        name="moe_experts_sorted",
    )(blk_exp, n_used, nxt, xs, w_in, w_out)


def _moe_combine_kernel(slot_ref, ys_ref, gate_ref, x_ref, gt_ref, vec_ref, o_ref, ya, yb, sem, *, tm, n):
    step = pl.program_id(0)
    slot = step % 2

    def gather(s, sl):
        def issue(i, c):
            t = s * tm + i
            _row_copy(ys_ref, slot_ref[t], ya.at[sl], i, sem.at[sl, 0]).start(priority=0)
            _row_copy(ys_ref, slot_ref[n + t], yb.at[sl], i, sem.at[sl, 1]).start(priority=1)
            return c
        lax.fori_loop(0, tm, issue, 0, unroll=DMA_UNROLL)

    @pl.when(step == 0)
    def _():
        gather(0, 0)

    @pl.when(step + 1 < pl.num_programs(0))
    def _():
        gather(step + 1, 1 - slot)

    _wait_rows(ys_ref, ya.at[slot], tm, sem.at[slot, 0])
    _wait_rows(ys_ref, yb.at[slot], tm, sem.at[slot, 1])
    ff = ya[slot] * gate_ref[:, 0:1] + yb[slot] * gate_ref[:, 1:2]
    o_ref[...] = _layer_norm(DN_ALPHA * x_ref[...] + gt_ref[...] * ff, vec_ref[0:1, :], vec_ref[1:2, :])


def _moe_combine(ys, slots, gates, x, gt, vec, tm):
    bsz, t, _ = x.shape
    nt = t // tm
    return pl.pallas_call(
        functools.partial(_moe_combine_kernel, tm=tm, n=bsz * t),
        grid_spec=pltpu.PrefetchScalarGridSpec(
            num_scalar_prefetch=1, grid=(bsz * nt,),
            in_specs=[pl.BlockSpec(memory_space=pl.ANY),
                      pl.BlockSpec((tm, 2), lambda i, sl: (i, 0)),
                      pl.BlockSpec((None, tm, D), lambda i, sl: (i // nt, i % nt, 0)),
                      pl.BlockSpec((None, 1, D), lambda i, sl: (i // nt, 0, 0)),
                      pl.BlockSpec((2, D), lambda i, sl: (0, 0))],
            out_specs=pl.BlockSpec((None, tm, D), lambda i, sl: (i // nt, i % nt, 0)),
            scratch_shapes=[pltpu.VMEM((2, tm, D), F32), pltpu.VMEM((2, tm, D), F32),
                            pltpu.SemaphoreType.DMA((2, 2))]),
        out_shape=jax.ShapeDtypeStruct(x.shape, F32),
        compiler_params=_cp(("arbitrary",)),
        name="moe_combine",
    )(slots, ys, gates, x, gt, vec)


def _moe_sorted(x, sc, sh, gt, vec, w, layer, tm=256):
    bsz, t, _ = x.shape
    n = bsz * t
    n_blocks = (n * 2) // MOE_BLK + N_EXPERTS
    h_rows, idx, gate, _, cnt = _moe_route(x, sc, sh, w["router_w_t"], w["router_b"], per_b_mod=True,
                                           rows_out=True, tm=tm)
    counts = cnt[:, 0].astype(I32)
    padded = (counts + MOE_BLK - 1) // MOE_BLK * MOE_BLK
    pad_end = jnp.cumsum(padded)
    pad_start = pad_end - padded
    slots = jnp.concatenate([pad_start[idx[0]] + idx[2], pad_start[idx[1]] + idx[3]])
    blk_start = jnp.arange(n_blocks, dtype=I32) * MOE_BLK
    blk_exp = jnp.minimum(jnp.sum((pad_end[None, :] <= blk_start[:, None]).astype(I32), axis=1), N_EXPERTS - 1)
    n_used = (pad_end[-1:] // MOE_BLK).astype(I32)
    tail = jnp.minimum(n_used[0] + jnp.arange(N_EXPERTS, dtype=I32), n_blocks - 1) * MOE_BLK
    zero_start = jnp.concatenate([jnp.maximum(pad_end - MOE_BLK, 0), tail])
    xs = _moe_dispatch(h_rows, slots, zero_start, n_blocks * MOE_BLK, tm)
    ys = _moe_experts_sorted(xs, blk_exp, n_used, w["moe_w_in"], w["moe_w_out"], layer)
    return _moe_combine(ys, slots, gate[:2].T, x, gt, vec, tm)


def _moe_dense_kernel(h_ref, gm_ref, win_ref, wout_ref, x_ref, gt_ref, vec_ref, o_ref, acc_ref):
    e = pl.program_id(0)

    @pl.when(e == 0)
    def _():
        acc_ref[...] = jnp.zeros_like(acc_ref)

    lane = lax.broadcasted_iota(I32, gm_ref.shape, 1)
    gcol = jnp.sum(jnp.where(lane == e, gm_ref[...], 0.0), axis=1, keepdims=True)
    hmid = _bdot(h_ref[...], win_ref[...])
    gt = hmid[:, :D_EXPERT]
    up = hmid[:, D_EXPERT:]
    y = _bdot(gt * _sigmoid(gt) * up, wout_ref[...])
    acc_ref[...] = acc_ref[...] + jnp.where(gcol != 0.0, gcol * y, 0.0)

    @pl.when(e == N_EXPERTS - 1)
    def _():
        o_ref[...] = _layer_norm(DN_ALPHA * x_ref[...] + gt_ref[...] * acc_ref[...], vec_ref[0:1, :],
                                 vec_ref[1:2, :])


def _moe_dense(x, sc, sh, gt, vec, w, layer):
    _, m, _ = x.shape
    h, _, _, gmat, _ = _moe_route(x, sc, sh, w["router_w_t"], w["router_b"], per_b_mod=False, rows_out=False,
                                  tm=m)
    full = pl.BlockSpec((m, D), lambda e: (0, 0))
    out = pl.pallas_call(
        _moe_dense_kernel,
        grid=(N_EXPERTS,),
        in_specs=[full, pl.BlockSpec((m, N_EXPERTS), lambda e: (0, 0)),
                  pl.BlockSpec((None, None, D, 2 * D_EXPERT), lambda e: (layer, e, 0, 0)),
                  pl.BlockSpec((None, None, D_EXPERT, D), lambda e: (layer, e, 0, 0)),
                  full, full, pl.BlockSpec((2, D), lambda e: (0, 0))],
        out_specs=full,
        out_shape=jax.ShapeDtypeStruct((m, D), F32),
        scratch_shapes=[pltpu.VMEM((m, D), F32)],
        compiler_params=_cp(("arbitrary",)),
        name="moe_dense",
    )(h[0], gmat.T, w["moe_w_in"], w["moe_w_out"], x[0], gt[0], vec)
    return out[None]


def _rotary(x, cos_t, sa_t, sb_t):
    reps = x.shape[1] // LANES
    tile = lambda t: t if reps == 1 else jnp.concatenate([t] * reps, axis=1)
    n = x.shape[1]
    half = ROT_DIM // 2
    return x * tile(cos_t) + pltpu.roll(x, n - half, axis=1) * tile(sa_t) + pltpu.roll(x, half, axis=1) * tile(sb_t)


def _class_rows(c, rows, dil):
    return pl.ds(c, rows, stride=dil) if dil > 1 else pl.ds(0, rows)


def _class_major_perm(tm, dil, inverse=False):
    cm_row = lax.broadcasted_iota(I32, (tm, tm), 1 if inverse else 0)
    tok_row = lax.broadcasted_iota(I32, (tm, tm), 0 if inverse else 1)
    rows = tm // dil
    return ((cm_row // rows == tok_row % dil) & (cm_row % rows == tok_row // dil)).astype(BF16)


def _attn_pre_kernel(x_ref, sc_ref, sh_ref, ksc_ref, ksh_ref, rot_ref, wq_ref, wkv_ref, *refs, dils, tm):
    x = x_ref[...]
    cos_t, sa_t, sb_t = rot_ref[0], rot_ref[1], rot_ref[2]
    q = _rotary(_bdot(x * (1.0 + sc_ref[...]) + sh_ref[...], wq_ref[...]), cos_t, sa_t, sb_t) * HD ** -0.5
    kv = _bdot(x * (1.0 + ksc_ref[...]) + ksh_ref[...], wkv_ref[...])
    k = _rotary(kv[:, :KVW], cos_t, sa_t, sb_t)
    v = kv[:, KVW:]
    k_ref, v_ref = refs[0], refs[1]
    k_ref[...] = k
    v_ref[...] = v
    if dils is None:
        refs[2][...] = q
        return
    ng = len(dils)
    q_refs, kb_refs, vb_refs = refs[2:2 + ng], refs[2 + ng:2 + 2 * ng], refs[2 + 2 * ng:2 + 3 * ng]
    qb, kb, vb = q.astype(BF16), k.astype(BF16), v.astype(BF16)
    for g, dil in enumerate(dils):
        rows = tm // dil
        blk = jnp.concatenate([qb[:, g * D:(g + 1) * D], kb, vb], axis=1)
        if dil > 1:
            blk = jnp.dot(_class_major_perm(tm, dil), blk, preferred_element_type=F32).astype(BF16)
        for c in range(dil):
            rs = slice(c * rows, (c + 1) * rows)
            q_refs[g][c] = blk[rs, :D]
            kb_refs[g][c] = blk[rs, D:D + KVW]
            vb_refs[g][c] = blk[rs, D + KVW:]


def _attn_pre(x, sc, sh, ksc, ksh, rot, w_q, w_kv, *, per_b_mod, tm, dils=None):
    bsz, t, _ = x.shape
    row = lambda wd: pl.BlockSpec((None, tm, wd), lambda b, i: (b, i, 0))
    mod = pl.BlockSpec((None, 1, D), lambda b, i: (b, 0, 0)) if per_b_mod else row(D)
    nq = w_q.shape[1]
    out_specs = [row(KVW), row(KVW)]
    out_shape = [jax.ShapeDtypeStruct((bsz, t, KVW), F32)] * 2
    scratch = []
    if dils is None:
        out_specs.append(row(nq))
        out_shape.append(jax.ShapeDtypeStruct((bsz, t, nq), F32))
    else:
        for wd in (D, KVW, KVW):
            for dil in dils:
                out_specs.append(pl.BlockSpec((None, dil, tm // dil, wd), lambda b, i: (b, 0, i, 0)))
                out_shape.append(jax.ShapeDtypeStruct((bsz, dil, t // dil, wd), BF16))
    outs = pl.pallas_call(
        functools.partial(_attn_pre_kernel, dils=dils, tm=tm),
        grid=(bsz, t // tm),
        in_specs=[row(D), mod, mod, mod, mod, pl.BlockSpec((3, tm, LANES), lambda b, i: (0, i, 0)),
                  pl.BlockSpec((D, nq), lambda b, i: (0, 0)), pl.BlockSpec((D, 2 * KVW), lambda b, i: (0, 0))],
        out_specs=out_specs,
        out_shape=out_shape,
        scratch_shapes=scratch,
        compiler_params=_cp(("arbitrary", "arbitrary")),
        name="attn_pre",
    )(x, sc, sh, ksc, ksh, rot, w_q, w_kv)
    if dils is None:
        return outs
    ng = len(dils)
    return outs[0], outs[1], outs[2:2 + ng], outs[2 + ng:2 + 2 * ng], outs[2 + 2 * ng:]


def _attn_band_kernel(q_ref, ka_ref, kb_ref, va_ref, vb_ref, o_ref, lse_ref):
    i = pl.program_id(2)
    bq = ATT_WIN
    kcat = jnp.concatenate([ka_ref[...], kb_ref[...]], axis=0)
    vcat = jnp.concatenate([va_ref[...], vb_ref[...]], axis=0)
    qs = jnp.concatenate([q_ref[:, rep * KVW:(rep + 1) * KVW] for rep in range(REP)], axis=0)
    qrow = lax.broadcasted_iota(I32, (REP * bq, 2 * bq), 0) % bq
    kcol = lax.broadcasted_iota(I32, (REP * bq, 2 * bq), 1)
    valid = (kcol >= qrow) & (kcol <= qrow + bq) & ((i > 0) | (kcol >= bq))
    lane = lax.broadcasted_iota(I32, (REP * bq, KVW), 1) // HD
    kv_lane = lax.broadcasted_iota(I32, (2 * bq, KVW), 1) // HD
    lse_lane = lax.broadcasted_iota(I32, (bq, LANES), 1)
    lse = jnp.zeros((bq, LANES), F32)
    linv = jnp.ones((REP * bq, KVW), F32)
    ps = []
    for h in range(KV_HEADS):
        kh = jnp.where(kv_lane == h, kcat, jnp.zeros_like(kcat))
        s = lax.dot_general(qs, kh, (((1,), (1,)), ((), ())), preferred_element_type=F32)
        s = jnp.where(valid, s, -jnp.inf)
        m = jnp.max(s, axis=-1, keepdims=True)
        p = jnp.exp(s - m)
        l = jnp.sum(p, axis=-1, keepdims=True)
        ps.append(p.astype(BF16))
        linv = jnp.where(lane == h, 1.0 / l, linv)
        lse_h = m + jnp.log(l)
        for rep in range(REP):
            lse = jnp.where(lse_lane == rep * KV_HEADS + h, lse_h[rep * bq:(rep + 1) * bq], lse)
    vstack = jnp.concatenate([jnp.where(kv_lane == h, vcat, jnp.zeros_like(vcat)) for h in range(KV_HEADS)],
                             axis=0)
    o = (jnp.dot(jnp.concatenate(ps, axis=1), vstack, preferred_element_type=F32) * linv).astype(BF16)
    for rep in range(REP):
        o_ref[:, rep * KVW:(rep + 1) * KVW] = o[rep * bq:(rep + 1) * bq]
    lse_ref[...] = lse


def _attn_band(q, kb, vb, dil):
    bsz, _, tc, _ = q.shape
    nb = tc // ATT_WIN
    blk = lambda wd: pl.BlockSpec((None, None, ATT_WIN, wd), lambda b, c, i: (b, c, i, 0))
    prev = pl.BlockSpec((None, None, ATT_WIN, KVW), lambda b, c, i: (b, c, jnp.maximum(i - 1, 0), 0))
    return pl.pallas_call(
        _attn_band_kernel,
        grid=(bsz, dil, nb),
        in_specs=[blk(D), prev, blk(KVW), prev, blk(KVW)],
        out_specs=[blk(D), blk(LANES)],
        out_shape=[jax.ShapeDtypeStruct((bsz, dil, tc, D), BF16), jax.ShapeDtypeStruct((bsz, dil, tc, LANES), F32)],
        compiler_params=_cp(("arbitrary", "arbitrary", "arbitrary")),
        name=f"attn_band_d{dil}",
    )(q, kb, kb, vb, vb)


def _attn_step_kernel(q_ref, kn_ref, vn_ref, kt_ref, vt_ref, o_ref, lse_ref, *, wbuf):
    nq = len(DILATED_GROUPS) * REP
    row = lax.broadcasted_iota(I32, (nq, wbuf), 0)
    pos = lax.broadcasted_iota(I32, (nq, wbuf), 1)
    valid = None
    for g, (_, dil) in enumerate(DILATED_GROUPS):
        ok = (row // REP == g) & (pos >= wbuf - ATT_WIN * dil) & (pos % dil == 0)
        valid = ok if valid is None else valid | ok
    rnd = lambda t: t.astype(BF16).astype(F32)
    lse_lane = lax.broadcasted_iota(I32, (nq, LANES), 1)
    lse = jnp.zeros((nq, LANES), F32)
    for h in range(KV_HEADS):
        hs = slice(h * HD, (h + 1) * HD)
        qh = q_ref[:, hs]
        s = jnp.where(valid, _bdot(qh, kt_ref[h]), -1e30)
        s_n = jnp.sum(rnd(qh) * rnd(kn_ref[:, hs]), axis=-1, keepdims=True)
        m = jnp.maximum(jnp.max(s, axis=-1, keepdims=True), s_n)
        p = jnp.exp(s - m)
        p_n = jnp.exp(s_n - m)
        l = jnp.sum(p, axis=-1, keepdims=True) + p_n
        o_ref[:, hs] = (_bdot_nt(p, vt_ref[h]) + rnd(p_n) * rnd(vn_ref[:, hs])) / l
        lse = jnp.where(lse_lane == h, m + jnp.log(l), lse)
    lse_ref[...] = lse


def _attn_step(q, k_new, v_new, cache_k, cache_v):
    bsz = q.shape[0]
    wbuf = cache_k.shape[1]
    ng = len(DILATED_GROUPS)
    assert all(wbuf % dil == 0 and wbuf >= ATT_WIN * dil for _, dil in DILATED_GROUPS)
    nq = ng * REP
    cache_spec = pl.BlockSpec((None, KV_HEADS, HD, wbuf), lambda b: (b, 0, 0, 0))
    o, lse = pl.pallas_call(
        functools.partial(_attn_step_kernel, wbuf=wbuf),
        grid=(bsz,),
        in_specs=[pl.BlockSpec((None, nq, KVW), lambda b: (b, 0, 0)),
                  pl.BlockSpec((None, 1, KVW), lambda b: (b, 0, 0)),
                  pl.BlockSpec((None, 1, KVW), lambda b: (b, 0, 0)), cache_spec, cache_spec],
        out_specs=[pl.BlockSpec((None, nq, KVW), lambda b: (b, 0, 0)),
                   pl.BlockSpec((None, nq, LANES), lambda b: (b, 0, 0))],
        out_shape=[jax.ShapeDtypeStruct((bsz, nq, KVW), F32), jax.ShapeDtypeStruct((bsz, nq, LANES), F32)],
        compiler_params=_cp(("arbitrary",)),
        name="attn_step",
    )(q.reshape(bsz, nq, KVW), k_new.reshape(bsz, 1, KVW), v_new.reshape(bsz, 1, KVW),
      cache_k.transpose(0, 2, 3, 1), cache_v.transpose(0, 2, 3, 1))
    o = o.reshape(bsz, ng, D).transpose(1, 0, 2)
    lse = lse[:, :, :KV_HEADS].reshape(bsz, ng, REP * KV_HEADS).transpose(1, 0, 2)
    return o, jnp.pad(lse, ((0, 0), (0, 0), (0, LANES - REP * KV_HEADS)))


def _attn_post_kernel(o0_ref, o1_ref, o2_ref, l0_ref, l1_ref, l2_ref, x_ref, gt_ref, vec_ref, wo_ref, out_ref,
                      *scratch, dils, tm):
    o_refs = [o0_ref, o1_ref, o2_ref]
    l_refs = [l0_ref, l1_ref, l2_ref]
    if dils is not None:
        for g, dil in enumerate(dils):
            if dil == 1:
                o_refs[g], l_refs[g] = o_refs[g].at[0], l_refs[g].at[0]
                continue
            sl = scratch[g]
            for c in range(dil):
                sl[_class_rows(c, tm // dil, dil), :] = l_refs[g][c]
            l_refs[g] = sl
            o_cm = jnp.concatenate([o_refs[g][c] for c in range(dil)], axis=0)
            o_refs[g] = jnp.dot(_class_major_perm(tm, dil, inverse=True), o_cm, preferred_element_type=F32)
    o0, o1, o2 = (r if isinstance(r, jax.Array) else r[...] for r in o_refs)
    l0, l1, l2 = (r[...] for r in l_refs)
    m = jnp.maximum(jnp.maximum(l0, l1), l2)
    e0, e1, e2 = jnp.exp(l0 - m), jnp.exp(l1 - m), jnp.exp(l2 - m)
    den = e0 + e1 + e2
    r = lax.broadcasted_iota(I32, (LANES, D), 0)
    c = lax.broadcasted_iota(I32, (LANES, D), 1) // HD
    spread = (r == c).astype(BF16)
    mix = (_split_dot(e0 / den, spread) * o0 + _split_dot(e1 / den, spread) * o1
           + _split_dot(e2 / den, spread) * o2)
    out = _bdot(mix, wo_ref[...])
    out_ref[...] = _layer_norm(DN_ALPHA * x_ref[...] + gt_ref[...] * out, vec_ref[0:1, :], vec_ref[1:2, :])


def _attn_post(os, lses, x, gt, vec, w_o, *, per_b_mod, tm, dils=None):
    bsz, t, _ = x.shape
    row = lambda wd: pl.BlockSpec((None, tm, wd), lambda b, i: (b, i, 0))
    mod = pl.BlockSpec((None, 1, D), lambda b, i: (b, 0, 0)) if per_b_mod else row(D)
    if dils is None:
        o_specs, l_specs, scratch = [row(D)] * 3, [row(LANES)] * 3, []
    else:
        cls = lambda dil, wd: pl.BlockSpec((None, dil, tm // dil, wd), lambda b, i: (b, 0, i, 0))
        o_specs = [cls(dil, D) for dil in dils]
        l_specs = [cls(dil, LANES) for dil in dils]
        scratch = [pltpu.VMEM((tm, LANES), F32) for _ in dils]
    return pl.pallas_call(
        functools.partial(_attn_post_kernel, dils=dils, tm=tm),
        grid=(bsz, t // tm),
        in_specs=o_specs + l_specs + [row(D), mod, pl.BlockSpec((2, D), lambda b, i: (0, 0)),
                                      pl.BlockSpec((D, D), lambda b, i: (0, 0))],
        out_specs=row(D),
        out_shape=jax.ShapeDtypeStruct(x.shape, F32),
        scratch_shapes=scratch,
        compiler_params=_cp(("arbitrary", "arbitrary")),
        name="attn_post",
    )(*os, *lses, x, gt, vec, w_o)


def _rot_tables(pos):
    half = ROT_DIM // 2
    inv = ROPE_THETA ** (-jnp.arange(half, dtype=F32) * 2.0 / ROT_DIM)
    ang = pos.astype(F32)[:, None] * inv[None, :]
    cos, sin = jnp.cos(ang), jnp.sin(ang)
    t = pos.shape[0]
    one = jnp.ones((t, HD - ROT_DIM), F32)
    zero = jnp.zeros((t, HD - ROT_DIM), F32)
    zh = jnp.zeros((t, half), F32)
    cos_t = jnp.concatenate([cos, cos, one], axis=1)
    sa_t = jnp.concatenate([-sin, zh, zero], axis=1)
    sb_t = jnp.concatenate([zh, sin, zero], axis=1)
    return jnp.stack([jnp.tile(x, (1, 2)) for x in (cos_t, sa_t, sb_t)])


def _prep_weights(p):
    pad_c = lambda m, n: jnp.pad(m, ((0, 0), (0, n - m.shape[1])))
    pad_r = lambda m, n: jnp.pad(m, ((0, n - m.shape[0]), (0, 0)))
    w = {}
    w["mu"] = p["rwkv_mu"][0]
    w["w_rkv"] = p["rwkv_w_rkv"][0].astype(BF16)
    w["w1"] = pad_c(p["rwkv_w1"][0], LANES).astype(BF16)
    w["w2"] = pad_r(p["rwkv_w2"][0], LANES).astype(BF16)
    w["a1"] = pad_c(p["rwkv_a1"][0], LANES).astype(BF16)
    w["a2"] = pad_r(p["rwkv_a2"][0], LANES).astype(BF16)
    w["g1"] = pad_c(p["rwkv_g1"][0], 2 * LANES).astype(BF16)
    w["g2"] = pad_r(p["rwkv_g2"][0], 2 * LANES).astype(BF16)
    w["vec"] = jnp.stack([p["rwkv_w0"][0], p["rwkv_a0"][0], p["rwkv_k_k"][0], p["rwkv_k_a"][0]])
    w["post_vec"] = jnp.stack([p["rwkv_r_k"][0].reshape(D), p["rwkv_lnx_g"][0], p["rwkv_lnx_b"][0],
                               p["ln_g"][0, 0], p["ln_b"][0, 0]])
    w["rwkv_w_o"] = p["rwkv_w_o"][0].astype(BF16)
    wq = p["w_q"][0].reshape(D, 3, KV_HEADS, REP, HD).transpose(0, 1, 3, 2, 4).reshape(D, 3 * D)
    w["w_q"] = wq.astype(BF16)
    w["w_kv"] = p["w_kv"].astype(BF16)
    wo = p["w_o_attn"][0].reshape(KV_HEADS, REP, HD, D).transpose(1, 0, 2, 3).reshape(D, D)
    w["w_o_attn"] = wo.astype(BF16)
    w["router_w_t"] = p["router_w"].T.astype(BF16)
    w["router_b"] = p["router_b"].reshape(N_EXPERTS, 1)
    w["moe_w_in"] = p["moe_w_in"]
    w["moe_w_out"] = p["moe_w_out"]
    w["ln"] = [[jnp.stack([p["ln_g"][l, i], p["ln_b"][l, i]]) for i in range(2)] for l in range(DEPTH)]
    return w


def _modulations(c_prompt, c_sample, p):
    nb = c_prompt.shape[0]
    c = jnp.concatenate([c_prompt, c_sample], axis=0)
    pad = (-c.shape[0]) % 8
    c = jnp.pad(c, ((0, pad), (0, 0)))
    m3 = _ada_linear(c, p["ada_w"].reshape(2 * DEPTH, D, 3 * D), p["ada_b"].reshape(2 * DEPTH, 3 * D))
    m2 = _ada_linear(c, p["kv_ada_w"][None], p["kv_ada_b"][None])[0]
    n_all = nb + c_sample.shape[0]

    def split(m, parts, lo, hi):
        return [m[lo:hi, i * D:(i + 1) * D] for i in range(parts)]

    out = {}
    for name, lo, hi in (("prompt", 0, nb), ("sample", nb, n_all)):
        out[name] = {"ada": [[split(m3[2 * l + i], 3, lo, hi) for i in range(2)] for l in range(DEPTH)],
                     "kv": split(m2, 2, lo, hi)}
    return out


def _trunk_prompt(x, mods, w):
    bsz, t, _ = x.shape
    per_b = lambda m: m[:, None, :]
    sh, sc, gt = (per_b(m) for m in mods["ada"][0][0])
    zeros = jnp.zeros((bsz, 1, D), F32)
    r, lw, k, v, a, b, g, last = _rwkv_pre(x, sc, sh, zeros, w, seq_mode=True, tm=256)
    y, zf = _wkv_chunked(r, lw, k, v, a, b)
    x = _rwkv_post(y, r, k, v, g, x, gt, w["post_vec"], w["rwkv_w_o"], pairs=True, tm=256)
    zf = zf.reshape(bsz, PAIRS, 2, HD, 2, HD)
    wkv = jnp.stack([zf[:, :, 0, :, 0, :], zf[:, :, 1, :, 1, :]], axis=2)
    wkv = wkv.reshape(bsz, HEADS, HD, HD).transpose(0, 1, 3, 2)
    sh2, sc2, gt2 = (per_b(m) for m in mods["ada"][0][1])
    x = _moe_sorted(x, sc2, sh2, gt2, w["ln"][0][1], w, 0)

    sh, sc, gt = (per_b(m) for m in mods["ada"][1][0])
    ksh, ksc = (per_b(m) for m in mods["kv"])
    rot = _rot_tables(jnp.arange(t, dtype=I32))
    dils = tuple(dil for _, dil in DILATED_GROUPS)
    k_new, v_new, qs, kbs, vbs = _attn_pre(x, sc, sh, ksc, ksh, rot, w["w_q"], w["w_kv"], per_b_mod=True,
                                           tm=256, dils=dils)
    os, lses = zip(*[_attn_band(qs[g], kbs[g], vbs[g], dil) for g, dil in enumerate(dils)])
    x = _attn_post(os, lses, x, gt, w["ln"][1][0], w["w_o_attn"], per_b_mod=True, tm=256, dils=dils)
    sh2, sc2, gt2 = (per_b(m) for m in mods["ada"][1][1])
    x = _moe_sorted(x, sc2, sh2, gt2, w["ln"][1][1], w, 1)
    return x, wkv[None], last.reshape(1, bsz, D), k_new, v_new


def _trunk_sample(x, mods, state_wkv, state_shift, cache_k, cache_v, w):
    bsz = x.shape[0]
    xs = x.reshape(1, bsz, D)
    row = lambda m: m[None]
    sh, sc, gt = (row(m) for m in mods["ada"][0][0])
    r, lw, k, v, a, b, g, hm = _rwkv_pre(xs, sc, sh, state_shift[0][None], w, seq_mode=False, tm=bsz)
    s_new, y = _wkv_step(state_wkv[0].transpose(1, 2, 3, 0), *(t[0].T for t in (r, lw, k, v, a, b)))
    s_new = s_new.transpose(3, 0, 1, 2)
    xs = _rwkv_post(y.T[None], r, k, v, g, xs, gt, w["post_vec"], w["rwkv_w_o"], pairs=False, tm=bsz)
    sh2, sc2, gt2 = (row(m) for m in mods["ada"][0][1])
    xs = _moe_dense(xs, sc2, sh2, gt2, w["ln"][0][1], w, 0)

    sh, sc, gt = (row(m) for m in mods["ada"][1][0])
    ksh, ksc = (row(m) for m in mods["kv"])
    rot = _rot_tables(jnp.full((bsz,), PAST_LEN, I32))
    k_new, v_new, q = _attn_pre(xs, sc, sh, ksc, ksh, rot, w["w_q"], w["w_kv"], per_b_mod=False, tm=bsz)
    o, lse = _attn_step(q[0], k_new[0], v_new[0], cache_k, cache_v)
    xs = _attn_post([o[i][None] for i in range(3)], [lse[i][None] for i in range(3)], xs, gt, w["ln"][1][0],
                    w["w_o_attn"], per_b_mod=False, tm=bsz)
    sh2, sc2, gt2 = (row(m) for m in mods["ada"][1][1])
    xs = _moe_dense(xs, sc2, sh2, gt2, w["ln"][1][1], w, 1)
    return (xs.reshape(bsz, 1, D), s_new[None], hm, k_new.reshape(bsz, 1, KV_HEADS, HD),
            v_new.reshape(bsz, 1, KV_HEADS, HD))


def kernel(x_prompt, x_sample, state_wkv, state_shift, cache_k, cache_v, c_prompt, c_sample, ada_w, ada_b, ln_g, ln_b, rwkv_mu, rwkv_w_rkv, rwkv_w0, rwkv_w1, rwkv_w2, rwkv_a0, rwkv_a1, rwkv_a2, rwkv_g1, rwkv_g2, rwkv_k_k, rwkv_k_a, rwkv_r_k, rwkv_lnx_g, rwkv_lnx_b, rwkv_w_o, w_q, w_kv, kv_ada_w, kv_ada_b, w_o_attn, router_w, router_b, moe_w_in, moe_w_out):
    p = {"ada_w": ada_w, "ada_b": ada_b, "ln_g": ln_g, "ln_b": ln_b, "rwkv_mu": rwkv_mu, "rwkv_w_rkv": rwkv_w_rkv,
         "rwkv_w0": rwkv_w0, "rwkv_w1": rwkv_w1, "rwkv_w2": rwkv_w2, "rwkv_a0": rwkv_a0, "rwkv_a1": rwkv_a1,
         "rwkv_a2": rwkv_a2, "rwkv_g1": rwkv_g1, "rwkv_g2": rwkv_g2, "rwkv_k_k": rwkv_k_k, "rwkv_k_a": rwkv_k_a,
         "rwkv_r_k": rwkv_r_k, "rwkv_lnx_g": rwkv_lnx_g, "rwkv_lnx_b": rwkv_lnx_b, "rwkv_w_o": rwkv_w_o,
         "w_q": w_q, "w_kv": w_kv, "kv_ada_w": kv_ada_w, "kv_ada_b": kv_ada_b, "w_o_attn": w_o_attn,
         "router_w": router_w, "router_b": router_b, "moe_w_in": moe_w_in, "moe_w_out": moe_w_out}
    w = _prep_weights(p)
    mods = _modulations(c_prompt, c_sample, p)
    bp, tp, _ = x_prompt.shape
    y_p, wkv_p, shift_p, k_p, v_p = _trunk_prompt(x_prompt, mods["prompt"], w)
    keep = min(PAST_LEN, tp)
    k_p = k_p[:, tp - keep:].reshape(bp, keep, KV_HEADS, HD)
    v_p = v_p[:, tp - keep:].reshape(bp, keep, KV_HEADS, HD)
    y_s, wkv_s, shift_s, k_s, v_s = _trunk_sample(x_sample, mods["sample"], state_wkv, state_shift, cache_k,
                                                  cache_v, w)
    return (y_p, y_s, wkv_p, shift_p, k_p, v_p, wkv_s, shift_s, k_s, v_s)
```

```python
import functools
import math

import jax
import jax.numpy as jnp
from jax import lax
from jax.experimental import pallas as pl
from jax.experimental.pallas import tpu as pltpu

F32 = jnp.float32
BF16 = jnp.bfloat16
I32 = jnp.int32

D = 1024
HEADS = 16
HD = 64
LANES = 128
PAIRS = D // LANES
CHUNK = 64
WKV_GROUP = 16
N_EXPERTS = 32
EXPERTS_PER_GROUP = 8
N_EXPERT_GROUPS = 4
D_EXPERT = 512
MOE_BLK = 256
DMA_UNROLL = 8
KV_HEADS = 4
REP = 4
KVW = KV_HEADS * HD
DILATED_GROUPS = ((128, 1), (512, 4), (2048, 16))
ATT_WIN = 128
PAST_LEN = 2048
ROT_DIM = 16
ROPE_THETA = 500000.0
DEPTH = 2
DN_ALPHA = (2 * DEPTH) ** 0.25
LN_EPS = 1e-5
GN_EPS = 64e-5
VMEM_LIMIT = 56 * 1024 * 1024


def _cp(sem):
    return pltpu.CompilerParams(dimension_semantics=sem, vmem_limit_bytes=VMEM_LIMIT)


def _bdot(a, b):
    return jnp.dot(a.astype(BF16), b.astype(BF16), preferred_element_type=F32)


def _bdot_nt(a, b):
    return lax.dot_general(a.astype(BF16), b.astype(BF16), (((1,), (1,)), ((), ())),
                           preferred_element_type=F32)


def _bdot_tn(a, b):
    return lax.dot_general(a.astype(BF16), b.astype(BF16), (((0,), (0,)), ((), ())),
                           preferred_element_type=F32)


def _split_dot(x, m):
    hi = x.astype(BF16)
    lo = (x - hi.astype(F32)).astype(BF16)
    return jnp.dot(jnp.concatenate([hi, lo], axis=1), jnp.concatenate([m, m], axis=0),
                   preferred_element_type=F32)


def _head_ones():
    r = lax.broadcasted_iota(I32, (LANES, LANES), 0) // HD
    c = lax.broadcasted_iota(I32, (LANES, LANES), 1) // HD
    return (r == c).astype(BF16)


def _segsum(x, ones_bd):
    cols = [_split_dot(x[:, c * LANES:(c + 1) * LANES], ones_bd) for c in range(x.shape[1] // LANES)]
    return cols[0] if len(cols) == 1 else jnp.concatenate(cols, axis=1)


def _layer_norm(x, g, b):
    mu = jnp.mean(x, axis=-1, keepdims=True)
    xc = x - mu
    var = jnp.mean(xc * xc, axis=-1, keepdims=True)
    return xc * lax.rsqrt(var + LN_EPS) * g + b


def _sigmoid(x):
    return 1.0 / (1.0 + jnp.exp(-x))


def _to_pairs(ref, val):
    for p in range(PAIRS):
        ref[p] = val[:, p * LANES:(p + 1) * LANES]


def _from_pairs(ref):
    return jnp.concatenate([ref[p] for p in range(PAIRS)], axis=1)


def _ada_kernel(c_ref, w_ref, b_ref, o_ref):
    c = c_ref[...]
    o_ref[...] = _bdot(c * _sigmoid(c), w_ref[...]) + b_ref[...]


def _ada_linear(c, w, b, tn=512):
    s, _, n = w.shape
    m = c.shape[0]
    return pl.pallas_call(
        _ada_kernel,
        grid=(s, n // tn),
        in_specs=[pl.BlockSpec((m, D), lambda i, j: (0, 0)),
                  pl.BlockSpec((None, D, tn), lambda i, j: (i, 0, j)),
                  pl.BlockSpec((None, 1, tn), lambda i, j: (i, 0, j))],
        out_specs=pl.BlockSpec((None, m, tn), lambda i, j: (i, 0, j)),
        out_shape=jax.ShapeDtypeStruct((s, m, n), F32),
        compiler_params=_cp(("arbitrary", "arbitrary")),
        name="ada_linear",
    )(c, w, b.reshape(s, 1, n))


def _rwkv_pre_kernel(x_ref, sc_ref, sh_ref, prev_ref, mu_ref, wrkv_ref, w1_ref, w2_ref, a1_ref, a2_ref,
                     g1_ref, g2_ref, vec_ref,
                     r_ref, lw_ref, k_ref, v_ref, a_ref, b_ref, g_ref, hm_ref, carry_ref, *, seq_mode, tm):
    x = x_ref[...]
    hm = x * (1.0 + sc_ref[...]) + sh_ref[...]
    if seq_mode:
        @pl.when(pl.program_id(1) == 0)
        def _():
            carry_ref[...] = prev_ref[...]
        row = lax.broadcasted_iota(I32, hm.shape, 0)
        hprev = jnp.where(row == 0, carry_ref[...], pltpu.roll(hm, 1, axis=0))
        carry_ref[...] = hm[tm - 1:tm, :]
        hm_ref[...] = hm[tm - 1:tm, :]
    else:
        hprev = prev_ref[...]
        hm_ref[...] = hm
    xx = hprev - hm

    def mix(i):
        return hm + xx * mu_ref[i:i + 1, :]

    w0, a0, k_k, k_a = (vec_ref[i:i + 1, :] for i in range(4))
    r = _bdot(mix(0), wrkv_ref[0])
    k = _bdot(mix(2), wrkv_ref[1])
    v = _bdot(mix(3), wrkv_ref[2])
    wl = w0 + _bdot(jnp.tanh(_bdot(mix(1), w1_ref[...])), w2_ref[...])
    lw = -math.exp(-0.5) * _sigmoid(wl)
    a_lr = _sigmoid(a0 + _bdot(_bdot(mix(4), a1_ref[...]), a2_ref[...]))
    g = _bdot(_sigmoid(_bdot(mix(5), g1_ref[...])), g2_ref[...])
    kk = k * k_k
    kk = kk * jnp.minimum(lax.rsqrt(_segsum(kk * kk, _head_ones())), 1e12)
    kmod = k * (1.0 + (a_lr - 1.0) * k_a)
    outs = ((r_ref, r), (lw_ref, lw), (k_ref, kmod), (v_ref, v), (a_ref, -kk), (b_ref, kk * a_lr), (g_ref, g))
    for ref, val in outs:
        if seq_mode:
            _to_pairs(ref, val)
        else:
            ref[...] = val


def _rwkv_pre(x, sc, sh, prev, w, *, seq_mode, tm):
    bsz, t, _ = x.shape
    grid = (bsz, t // tm)
    row = pl.BlockSpec((None, tm, D), lambda b, i: (b, i, 0))
    per_b = pl.BlockSpec((None, 1, D), lambda b, i: (b, 0, 0))
    mod = per_b if seq_mode else row

    def const(shape):
        return pl.BlockSpec(shape, lambda b, i: (0,) * len(shape))

    if seq_mode:
        out_big = pl.BlockSpec((None, PAIRS, tm, LANES), lambda b, i: (b, 0, i, 0))
        big_shape = jax.ShapeDtypeStruct((bsz, PAIRS, t, LANES), F32)
        hm_spec, hm_shape = per_b, jax.ShapeDtypeStruct((bsz, 1, D), F32)
    else:
        out_big, big_shape = row, jax.ShapeDtypeStruct((bsz, t, D), F32)
        hm_spec, hm_shape = row, jax.ShapeDtypeStruct((bsz, t, D), F32)
    return pl.pallas_call(
        functools.partial(_rwkv_pre_kernel, seq_mode=seq_mode, tm=tm),
        grid=grid,
        in_specs=[row, mod, mod, mod, const((6, D)), const((3, D, D)), const((D, LANES)), const((LANES, D)),
                  const((D, LANES)), const((LANES, D)), const((D, 2 * LANES)), const((2 * LANES, D)),
                  const((4, D))],
        out_specs=[out_big] * 7 + [hm_spec],
        out_shape=[big_shape] * 7 + [hm_shape],
        scratch_shapes=[pltpu.VMEM((1, D), F32)],
        compiler_params=_cp(("arbitrary", "arbitrary")),
        name="rwkv_pre",
    )(x, sc, sh, prev, w["mu"], w["w_rkv"], w["w1"], w["w2"], w["a1"], w["a2"], w["g1"], w["g2"], w["vec"])


def _wkv_chunk_kernel(r_ref, lw_ref, k_ref, v_ref, a_ref, b_ref, y_ref, zf_ref, z_ref, *, n_pairs, group):
    c = pl.program_id(0)

    @pl.when(c == 0)
    def _():
        z_ref[...] = jnp.zeros_like(z_ref)

    L = CHUNK
    row_l = lax.broadcasted_iota(I32, (L, L), 0)
    col_l = lax.broadcasted_iota(I32, (L, L), 1)
    tri_incl = (row_l >= col_l).astype(BF16)
    lane = lax.broadcasted_iota(I32, (L, LANES), 1)
    head0 = lane < HD
    row = lax.broadcasted_iota(I32, (LANES, LANES), 0)
    col = lax.broadcasted_iota(I32, (LANES, LANES), 1)
    strict = row > col
    incl = row >= col
    eye = (row == col).astype(F32)

    def expand(x):
        return jnp.concatenate([jnp.where(head0, x, 0.0), jnp.where(head0, 0.0, x)], axis=0)

    def group_body(gi, carry):
        ids = [gi * group + j for j in range(group)]
        bp = [(i // PAIRS, i % PAIRS) for i in ids]
        G = range(group)
        r = [r_ref[b_i, p_i] for b_i, p_i in bp]
        lw = [lw_ref[b_i, p_i] for b_i, p_i in bp]
        k = [k_ref[b_i, p_i] for b_i, p_i in bp]
        v = [v_ref[b_i, p_i] for b_i, p_i in bp]
        a = [a_ref[b_i, p_i] for b_i, p_i in bp]
        b = [b_ref[b_i, p_i] for b_i, p_i in bp]
        z0 = [z_ref[i] for i in ids]
        cum = [_split_dot_left(tri_incl, lw[j]) for j in G]
        cum_l = [cum[j][L - 1:L, :] for j in G]
        inv = [jnp.exp(-cum[j]) for j in G]
        tail = [jnp.exp(cum_l[j] - cum[j]) for j in G]
        a_e = [expand(a[j] * jnp.exp(cum[j] - lw[j])) for j in G]
        r_e = [expand(r[j] * jnp.exp(cum[j])) for j in G]
        b_e = [expand(b[j] * inv[j]) for j in G]
        k_e = [expand(k[j] * inv[j]) for j in G]
        bd_e = [expand(b[j] * tail[j]) for j in G]
        kd_e = [expand(k[j] * tail[j]) for j in G]
        v_e = [expand(v[j]) for j in G]
        gm = [_bdot_nt(jnp.concatenate([a_e[j], r_e[j]], axis=0), jnp.concatenate([b_e[j], k_e[j]], axis=0))
              for j in G]
        m_ab = [jnp.where(strict, gm[j][:LANES, :LANES], 0.0) for j in G]
        m_ak = [jnp.where(strict, gm[j][:LANES, LANES:], 0.0) for j in G]
        m_rb = [jnp.where(incl, gm[j][LANES:, :LANES], 0.0) for j in G]
        m_rk = [jnp.where(incl, gm[j][LANES:, LANES:], 0.0) for j in G]
        tinv = [eye + m_ab[j] for j in G]
        pw = [_bdot(m_ab[j], m_ab[j]) for j in G]
        for _ in range(4):
            both = [_bdot(pw[j], jnp.concatenate([pw[j], tinv[j]], axis=1)) for j in G]
            pw = [both[j][:, :LANES] for j in G]
            tinv = [tinv[j] + both[j][:, LANES:] for j in G]
        tinv = [tinv[j] + _bdot(pw[j], tinv[j]) for j in G]
        rhs = [_bdot(a_e[j], z0[j]) + _bdot(m_ak[j], v_e[j]) for j in G]
        u = [_bdot(tinv[j], rhs[j]) for j in G]
        zuv = [jnp.concatenate([z0[j], u[j], v_e[j]], axis=0) for j in G]
        decay = [eye * jnp.exp(cum_l[j]) for j in G]
        for j in G:
            z_ref[ids[j]] = _bdot_tn(jnp.concatenate([decay[j], bd_e[j], kd_e[j]], axis=0), zuv[j])
            y_e = _bdot(jnp.concatenate([r_e[j], m_rb[j], m_rk[j]], axis=1), zuv[j])
            y_ref[bp[j][0], bp[j][1]] = y_e[:L] + y_e[L:]
        return carry

    lax.fori_loop(0, n_pairs // group, group_body, 0)

    @pl.when(c == pl.num_programs(0) - 1)
    def _():
        zf_ref[...] = z_ref[...]


def _split_dot_left(m, x):
    hi = x.astype(BF16)
    lo = (x - hi.astype(F32)).astype(BF16)
    return (jnp.dot(m, hi, preferred_element_type=F32) + jnp.dot(m, lo, preferred_element_type=F32))


def _wkv_chunked(r, lw, k, v, a, b):
    bsz, _, t, _ = r.shape
    n_pairs = bsz * PAIRS
    blk = pl.BlockSpec((bsz, PAIRS, CHUNK, LANES), lambda c: (0, 0, c, 0))
    return pl.pallas_call(
        functools.partial(_wkv_chunk_kernel, n_pairs=n_pairs, group=WKV_GROUP),
        grid=(t // CHUNK,),
        in_specs=[blk] * 6,
        out_specs=[blk, pl.BlockSpec((n_pairs, LANES, LANES), lambda c: (0, 0, 0))],
        out_shape=[jax.ShapeDtypeStruct(r.shape, F32), jax.ShapeDtypeStruct((n_pairs, LANES, LANES), F32)],
        scratch_shapes=[pltpu.VMEM((n_pairs, LANES, LANES), F32)],
        compiler_params=_cp(("arbitrary",)),
        name="wkv_chunked",
    )(r, lw, k, v, a, b)


def _wkv_step_kernel(s_ref, r_ref, lw_ref, k_ref, v_ref, a_ref, b_ref, so_ref, y_ref):
    w = jnp.exp(lw_ref[...])
    a, b, k, r = a_ref[...], b_ref[...], k_ref[...], r_ref[...]

    def body(i, c):
        s = s_ref[i]
        sa = jnp.sum(s * a, axis=0, keepdims=True)
        s_new = s * w + sa * b + v_ref[pl.ds(i, 1), :] * k
        so_ref[i] = s_new
        y_ref[pl.ds(i, 1), :] = jnp.sum(s_new * r, axis=0, keepdims=True)
        return c

    lax.fori_loop(0, HD, body, 0, unroll=4)


def _wkv_step(state_t, r, lw, k, v, a, b):
    bsz = state_t.shape[-1]
    big = pl.BlockSpec((None, HD, HD, bsz), lambda h: (h, 0, 0, 0))
    vec = pl.BlockSpec((HD, bsz), lambda h: (h, 0))
    return pl.pallas_call(
        _wkv_step_kernel,
        grid=(HEADS,),
        in_specs=[big] + [vec] * 6,
        out_specs=[big, vec],
        out_shape=[jax.ShapeDtypeStruct(state_t.shape, F32), jax.ShapeDtypeStruct((D, bsz), F32)],
        compiler_params=_cp(("arbitrary",)),
        name="wkv_step",
    )(state_t, r, lw, k, v, a, b)


def _rwkv_post_kernel(y_ref, r_ref, k_ref, v_ref, g_ref, x_ref, gt_ref, vec_ref, wo_ref, o_ref, *, pairs):
    load = _from_pairs if pairs else (lambda ref: ref[...])
    y, r, k, v, g = (load(ref) for ref in (y_ref, r_ref, k_ref, v_ref, g_ref))
    r_k, lnx_g, lnx_b, ln_g, ln_b = (vec_ref[i:i + 1, :] for i in range(5))
    ones_bd = _head_ones()
    ym = _segsum(y, ones_bd) * (1.0 / HD)
    yc = y - ym
    yv = _segsum(yc * yc, ones_bd) * (1.0 / HD)
    yn = yc * lax.rsqrt(yv + GN_EPS) * lnx_g + lnx_b
    bonus = _segsum(r * k * r_k, ones_bd) * v
    mix = _bdot((yn + bonus) * g, wo_ref[...])
    o_ref[...] = _layer_norm(DN_ALPHA * x_ref[...] + gt_ref[...] * mix, ln_g, ln_b)


def _rwkv_post(y, r, k, v, g, x, gt, vec, w_o, *, pairs, tm):
    bsz, t, _ = x.shape
    row = pl.BlockSpec((None, tm, D), lambda b, i: (b, i, 0))
    big = pl.BlockSpec((None, PAIRS, tm, LANES), lambda b, i: (b, 0, i, 0)) if pairs else row
    mod = pl.BlockSpec((None, 1, D), lambda b, i: (b, 0, 0)) if pairs else row
    return pl.pallas_call(
        functools.partial(_rwkv_post_kernel, pairs=pairs),
        grid=(bsz, t // tm),
        in_specs=[big] * 5 + [row, mod, pl.BlockSpec((5, D), lambda b, i: (0, 0)),
                              pl.BlockSpec((D, D), lambda b, i: (0, 0))],
        out_specs=row,
        out_shape=jax.ShapeDtypeStruct(x.shape, F32),
        compiler_params=_cp(("arbitrary", "arbitrary")),
        name="rwkv_post",
    )(y, r, k, v, g, x, gt, vec, w_o)


def _top2(v):
    io = lax.broadcasted_iota(I32, v.shape, 0)
    m1 = jnp.max(v, axis=0, keepdims=True)
    i1 = jnp.min(jnp.where(v == m1, io, EXPERTS_PER_GROUP), axis=0, keepdims=True)
    v2 = jnp.where(io == i1, -jnp.inf, v)
    m2 = jnp.max(v2, axis=0, keepdims=True)
    i2 = jnp.min(jnp.where(v2 == m2, io, EXPERTS_PER_GROUP), axis=0, keepdims=True)
    return m1 + m2, i1, i2


def _moe_route_kernel(x_ref, sc_ref, sh_ref, rw_ref, rb_ref, h_ref, idx_ref, gate_ref, gmat_ref, cnt_ref,
                      carry_ref, *, tm, rows_out):
    @pl.when((pl.program_id(0) == 0) & (pl.program_id(1) == 0))
    def _():
        carry_ref[...] = jnp.zeros_like(carry_ref)

    h = x_ref[...] * (1.0 + sc_ref[...]) + sh_ref[...]
    h_ref[...] = h
    aff = _sigmoid(_bdot_nt(rw_ref[...], h))
    sel = aff + rb_ref[...]
    best = gi = i1 = i2 = None
    for g in range(N_EXPERT_GROUPS):
        sc, j1, j2 = _top2(sel[g * EXPERTS_PER_GROUP:(g + 1) * EXPERTS_PER_GROUP, :])
        if g == 0:
            best, gi, i1, i2 = sc, jnp.zeros_like(j1), j1, j2
        else:
            upd = sc > best
            best = jnp.where(upd, sc, best)
            gi = jnp.where(upd, g, gi)
            i1 = jnp.where(upd, j1, i1)
            i2 = jnp.where(upd, j2, i2)
    e0 = gi * EXPERTS_PER_GROUP + i1
    e1 = gi * EXPERTS_PER_GROUP + i2
    io = lax.broadcasted_iota(I32, (N_EXPERTS, tm), 0)
    oh0 = io == e0
    oh1 = io == e1
    a0 = jnp.sum(jnp.where(oh0, aff, 0.0), axis=0, keepdims=True)
    a1 = jnp.sum(jnp.where(oh1, aff, 0.0), axis=0, keepdims=True)
    den = a0 + a1
    g0 = a0 / den
    g1 = a1 / den
    oh = jnp.where(oh0 | oh1, 1.0, 0.0)
    tr = lax.broadcasted_iota(I32, (tm, tm), 0)
    tc = lax.broadcasted_iota(I32, (tm, tm), 1)
    before = jnp.dot(oh.astype(BF16), (tr < tc).astype(BF16), preferred_element_type=F32) + carry_ref[...]
    rank0 = jnp.sum(jnp.where(oh0, before, 0.0), axis=0, keepdims=True)
    rank1 = jnp.sum(jnp.where(oh1, before, 0.0), axis=0, keepdims=True)
    carry_ref[...] = carry_ref[...] + jnp.sum(oh, axis=1, keepdims=True)
    zi = jnp.zeros((4, tm), I32)
    idx_ref[...] = jnp.concatenate([e0, e1, rank0.astype(I32), rank1.astype(I32), zi], axis=0)
    gate_ref[...] = jnp.concatenate([g0, g1, jnp.zeros((6, tm), F32)], axis=0)
    gmat_ref[...] = jnp.where(oh0, g0, 0.0) + jnp.where(oh1, g1, 0.0)
    cnt_ref[...] = jnp.broadcast_to(carry_ref[...], cnt_ref.shape)


def _moe_route(x, sc, sh, router_w_t, router_b, *, per_b_mod, rows_out, tm):
    bsz, t, _ = x.shape
    n = bsz * t
    nt = t // tm
    row = pl.BlockSpec((None, tm, D), lambda b, i: (b, i, 0))
    mod = pl.BlockSpec((None, 1, D), lambda b, i: (b, 0, 0)) if per_b_mod else row
    if rows_out:
        h_spec = pl.BlockSpec((tm, D), lambda b, i: (b * nt + i, 0))
        h_shape = jax.ShapeDtypeStruct((n, D), F32)
    else:
        h_spec, h_shape = row, jax.ShapeDtypeStruct(x.shape, F32)
    tok = lambda rows: pl.BlockSpec((rows, tm), lambda b, i: (0, b * nt + i))
    return pl.pallas_call(
        functools.partial(_moe_route_kernel, tm=tm, rows_out=rows_out),
        grid=(bsz, nt),
        in_specs=[row, mod, mod, pl.BlockSpec((N_EXPERTS, D), lambda b, i: (0, 0)),
                  pl.BlockSpec((N_EXPERTS, 1), lambda b, i: (0, 0))],
        out_specs=[h_spec, tok(8), tok(8), tok(N_EXPERTS), pl.BlockSpec((N_EXPERTS, LANES), lambda b, i: (0, 0))],
        out_shape=[h_shape, jax.ShapeDtypeStruct((8, n), I32), jax.ShapeDtypeStruct((8, n), F32),
                   jax.ShapeDtypeStruct((N_EXPERTS, n), F32), jax.ShapeDtypeStruct((N_EXPERTS, LANES), F32)],
        scratch_shapes=[pltpu.VMEM((N_EXPERTS, 1), F32)],
        compiler_params=_cp(("arbitrary", "arbitrary")),
        name="moe_route",
    )(x, sc, sh, router_w_t, router_b)


def _row_copy(src, src_row, dst, dst_row, sem):
    return pltpu.make_async_copy(src.at[pl.ds(src_row, 1), :], dst.at[pl.ds(dst_row, 1), :], sem)


def _wait_rows(src, dst, rows, sem):
    pltpu.make_async_copy(src.at[pl.ds(0, rows), :], dst.at[pl.ds(0, rows), :], sem).wait()


def _moe_dispatch_kernel(slot_ref, zs_ref, h_ref, xs_ref, zbuf, sem, *, tm, n):
    base = pl.program_id(0) * tm

    @pl.when(pl.program_id(0) == 0)
    def _():
        zbuf[...] = jnp.zeros_like(zbuf)

        def block_copy(e):
            return pltpu.make_async_copy(zbuf, xs_ref.at[pl.ds(pl.multiple_of(zs_ref[e], 8), MOE_BLK), :],
                                         sem.at[2])

        fresh = [zs_ref[e] != zs_ref[e - 1] for e in range(1, zs_ref.shape[0])]
        block_copy(0).start()
        for e, f in enumerate(fresh, start=1):
            pl.when(f)(lambda e=e: block_copy(e).start())
        block_copy(0).wait()
        for e, f in enumerate(fresh, start=1):
            pl.when(f)(lambda e=e: block_copy(e).wait())

    def issue(i, c):
        _row_copy(h_ref, i, xs_ref, slot_ref[base + i], sem.at[0]).start(priority=0)
        _row_copy(h_ref, i, xs_ref, slot_ref[n + base + i], sem.at[1]).start(priority=1)
        return c

    lax.fori_loop(0, tm, issue, 0, unroll=DMA_UNROLL)
    _wait_rows(h_ref, xs_ref, tm, sem.at[0])
    _wait_rows(h_ref, xs_ref, tm, sem.at[1])


def _moe_dispatch(h_rows, slots, zero_start, n_slots, tm):
    n = h_rows.shape[0]
    return pl.pallas_call(
        functools.partial(_moe_dispatch_kernel, tm=tm, n=n),
        grid_spec=pltpu.PrefetchScalarGridSpec(
            num_scalar_prefetch=2, grid=(n // tm,),
            in_specs=[pl.BlockSpec((tm, D), lambda i, sl, zs: (i, 0))],
            out_specs=pl.BlockSpec(memory_space=pl.ANY),
            scratch_shapes=[pltpu.VMEM((MOE_BLK, D), F32), pltpu.SemaphoreType.DMA((3,))]),
        out_shape=jax.ShapeDtypeStruct((n_slots, D), F32),
        compiler_params=_cp(("arbitrary",)),
        name="moe_dispatch",
    )(slots, zero_start, h_rows)


def _moe_expert_kernel(be_ref, used_ref, nxt_ref, xs_ref, win_hbm, wout_hbm, ys_ref,
                       win_f, wout_f, win_bf, wout_bf, run_ref, sem, *, layer):
    i = pl.program_id(0)
    used = used_ref[0]

    def weight_copies(e, s):
        return (pltpu.make_async_copy(win_hbm.at[layer, e], win_f.at[s], sem.at[s, 0]),
                pltpu.make_async_copy(wout_hbm.at[layer, e], wout_f.at[s], sem.at[s, 1]))

    @pl.when(i == 0)
    def _():
        run_ref[0] = 0
        for cp in weight_copies(be_ref[0], 0):
            cp.start()

    fresh = (i == 0) | (be_ref[i] != be_ref[jnp.maximum(i - 1, 0)])

    @pl.when(fresh & (i < used))
    def _():
        run = run_ref[0] + jnp.where(i > 0, 1, 0)
        run_ref[0] = run
        s = run % 2
        for cp in weight_copies(be_ref[i], s):
            cp.wait()
        win_bf[...] = win_f[s].astype(BF16)
        wout_bf[...] = wout_f[s].astype(BF16)

        @pl.when(nxt_ref[i] >= 0)
        def _():
            for cp in weight_copies(nxt_ref[i], 1 - s):
                cp.start()

    @pl.when(i < used)
    def _():
        hmid = jnp.dot(xs_ref[...].astype(BF16), win_bf[...], preferred_element_type=F32)
        gt = hmid[:, :D_EXPERT]
        up = hmid[:, D_EXPERT:]
        act = gt * _sigmoid(gt) * up
        ys_ref[...] = jnp.dot(act.astype(BF16), wout_bf[...], preferred_element_type=F32)

    @pl.when(i >= used_ref[0])
    def _():
        ys_ref[...] = jnp.zeros_like(ys_ref)


def _moe_experts_sorted(xs, blk_exp, n_used, w_in, w_out, layer):
    n_slots = xs.shape[0]
    n_blocks = n_slots // MOE_BLK
    blk = jnp.arange(n_blocks, dtype=I32)
    later = (blk[None, :] > blk[:, None]) & (blk_exp[None, :] != blk_exp[:, None]) & (blk[None, :] < n_used[0])
    first_later = jnp.min(jnp.where(later, blk[None, :], n_blocks), axis=1)
    nxt = jnp.where(first_later < n_blocks, blk_exp[jnp.minimum(first_later, n_blocks - 1)], -1).astype(I32)
    x_map = lambda i, be, u, nx: (jnp.minimum(i, u[0] - 1), 0)
    return pl.pallas_call(
        functools.partial(_moe_expert_kernel, layer=layer),
        grid_spec=pltpu.PrefetchScalarGridSpec(
            num_scalar_prefetch=3, grid=(n_blocks,),
            in_specs=[pl.BlockSpec((MOE_BLK, D), x_map),
                      pl.BlockSpec(memory_space=pl.ANY), pl.BlockSpec(memory_space=pl.ANY)],
            out_specs=pl.BlockSpec((MOE_BLK, D), lambda i, be, u, nx: (i, 0)),
            scratch_shapes=[pltpu.VMEM((2, D, 2 * D_EXPERT), F32), pltpu.VMEM((2, D_EXPERT, D), F32),
                            pltpu.VMEM((D, 2 * D_EXPERT), BF16), pltpu.VMEM((D_EXPERT, D), BF16),
                            pltpu.SMEM((1,), I32), pltpu.SemaphoreType.DMA((2, 2))]),
        out_shape=jax.ShapeDtypeStruct((n_slots, D), F32),
        compiler_params=_cp(("arbitrary",)),
        name="moe_experts_sorted",
    )(blk_exp, n_used, nxt, xs, w_in, w_out)


def _moe_combine_kernel(slot_ref, ys_ref, gate_ref, x_ref, gt_ref, vec_ref, o_ref, ya, yb, sem, *, tm, n):
    step = pl.program_id(0)
    slot = step % 2

    def gather(s, sl):
        def issue(i, c):
            t = s * tm + i
            _row_copy(ys_ref, slot_ref[t], ya.at[sl], i, sem.at[sl, 0]).start(priority=0)
            _row_copy(ys_ref, slot_ref[n + t], yb.at[sl], i, sem.at[sl, 1]).start(priority=1)
            return c
        lax.fori_loop(0, tm, issue, 0, unroll=DMA_UNROLL)

    @pl.when(step == 0)
    def _():
        gather(0, 0)

    @pl.when(step + 1 < pl.num_programs(0))
    def _():
        gather(step + 1, 1 - slot)

    _wait_rows(ys_ref, ya.at[slot], tm, sem.at[slot, 0])
    _wait_rows(ys_ref, yb.at[slot], tm, sem.at[slot, 1])
    ff = ya[slot] * gate_ref[:, 0:1] + yb[slot] * gate_ref[:, 1:2]
    o_ref[...] = _layer_norm(DN_ALPHA * x_ref[...] + gt_ref[...] * ff, vec_ref[0:1, :], vec_ref[1:2, :])


def _moe_combine(ys, slots, gates, x, gt, vec, tm):
    bsz, t, _ = x.shape
    nt = t // tm
    return pl.pallas_call(
        functools.partial(_moe_combine_kernel, tm=tm, n=bsz * t),
        grid_spec=pltpu.PrefetchScalarGridSpec(
            num_scalar_prefetch=1, grid=(bsz * nt,),
            in_specs=[pl.BlockSpec(memory_space=pl.ANY),
                      pl.BlockSpec((tm, 2), lambda i, sl: (i, 0)),
                      pl.BlockSpec((None, tm, D), lambda i, sl: (i // nt, i % nt, 0)),
                      pl.BlockSpec((None, 1, D), lambda i, sl: (i // nt, 0, 0)),
                      pl.BlockSpec((2, D), lambda i, sl: (0, 0))],
            out_specs=pl.BlockSpec((None, tm, D), lambda i, sl: (i // nt, i % nt, 0)),
            scratch_shapes=[pltpu.VMEM((2, tm, D), F32), pltpu.VMEM((2, tm, D), F32),
                            pltpu.SemaphoreType.DMA((2, 2))]),
        out_shape=jax.ShapeDtypeStruct(x.shape, F32),
        compiler_params=_cp(("arbitrary",)),
        name="moe_combine",
    )(slots, ys, gates, x, gt, vec)


def _moe_sorted(x, sc, sh, gt, vec, w, layer, tm=256):
    bsz, t, _ = x.shape
    n = bsz * t
    n_blocks = (n * 2) // MOE_BLK + N_EXPERTS
    h_rows, idx, gate, _, cnt = _moe_route(x, sc, sh, w["router_w_t"], w["router_b"], per_b_mod=True,
                                           rows_out=True, tm=tm)
    counts = cnt[:, 0].astype(I32)
    padded = (counts + MOE_BLK - 1) // MOE_BLK * MOE_BLK
    pad_end = jnp.cumsum(padded)
    pad_start = pad_end - padded
    slots = jnp.concatenate([pad_start[idx[0]] + idx[2], pad_start[idx[1]] + idx[3]])
    blk_start = jnp.arange(n_blocks, dtype=I32) * MOE_BLK
    blk_exp = jnp.minimum(jnp.sum((pad_end[None, :] <= blk_start[:, None]).astype(I32), axis=1), N_EXPERTS - 1)
    n_used = (pad_end[-1:] // MOE_BLK).astype(I32)
    tail = jnp.minimum(n_used[0] + jnp.arange(N_EXPERTS, dtype=I32), n_blocks - 1) * MOE_BLK
    zero_start = jnp.concatenate([jnp.maximum(pad_end - MOE_BLK, 0), tail])
    xs = _moe_dispatch(h_rows, slots, zero_start, n_blocks * MOE_BLK, tm)
    ys = _moe_experts_sorted(xs, blk_exp, n_used, w["moe_w_in"], w["moe_w_out"], layer)
    return _moe_combine(ys, slots, gate[:2].T, x, gt, vec, tm)


def _moe_dense_kernel(h_ref, gm_ref, win_ref, wout_ref, x_ref, gt_ref, vec_ref, o_ref, acc_ref):
    e = pl.program_id(0)

    @pl.when(e == 0)
    def _():
        acc_ref[...] = jnp.zeros_like(acc_ref)

    lane = lax.broadcasted_iota(I32, gm_ref.shape, 1)
    gcol = jnp.sum(jnp.where(lane == e, gm_ref[...], 0.0), axis=1, keepdims=True)
    hmid = _bdot(h_ref[...], win_ref[...])
    gt = hmid[:, :D_EXPERT]
    up = hmid[:, D_EXPERT:]
    y = _bdot(gt * _sigmoid(gt) * up, wout_ref[...])
    acc_ref[...] = acc_ref[...] + jnp.where(gcol != 0.0, gcol * y, 0.0)

    @pl.when(e == N_EXPERTS - 1)
    def _():
        o_ref[...] = _layer_norm(DN_ALPHA * x_ref[...] + gt_ref[...] * acc_ref[...], vec_ref[0:1, :],
                                 vec_ref[1:2, :])


def _moe_dense(x, sc, sh, gt, vec, w, layer):
    _, m, _ = x.shape
    h, _, _, gmat, _ = _moe_route(x, sc, sh, w["router_w_t"], w["router_b"], per_b_mod=False, rows_out=False,
                                  tm=m)
    full = pl.BlockSpec((m, D), lambda e: (0, 0))
    out = pl.pallas_call(
        _moe_dense_kernel,
        grid=(N_EXPERTS,),
        in_specs=[full, pl.BlockSpec((m, N_EXPERTS), lambda e: (0, 0)),
                  pl.BlockSpec((None, None, D, 2 * D_EXPERT), lambda e: (layer, e, 0, 0)),
                  pl.BlockSpec((None, None, D_EXPERT, D), lambda e: (layer, e, 0, 0)),
                  full, full, pl.BlockSpec((2, D), lambda e: (0, 0))],
        out_specs=full,
        out_shape=jax.ShapeDtypeStruct((m, D), F32),
        scratch_shapes=[pltpu.VMEM((m, D), F32)],
        compiler_params=_cp(("arbitrary",)),
        name="moe_dense",
    )(h[0], gmat.T, w["moe_w_in"], w["moe_w_out"], x[0], gt[0], vec)
    return out[None]


def _rotary(x, cos_t, sa_t, sb_t):
    reps = x.shape[1] // LANES
    tile = lambda t: t if reps == 1 else jnp.concatenate([t] * reps, axis=1)
    n = x.shape[1]
    half = ROT_DIM // 2
    return x * tile(cos_t) + pltpu.roll(x, n - half, axis=1) * tile(sa_t) + pltpu.roll(x, half, axis=1) * tile(sb_t)


def _class_rows(c, rows, dil):
    return pl.ds(c, rows, stride=dil) if dil > 1 else pl.ds(0, rows)


def _class_major_perm(tm, dil, inverse=False):
    cm_row = lax.broadcasted_iota(I32, (tm, tm), 1 if inverse else 0)
    tok_row = lax.broadcasted_iota(I32, (tm, tm), 0 if inverse else 1)
    rows = tm // dil
    return ((cm_row // rows == tok_row % dil) & (cm_row % rows == tok_row // dil)).astype(BF16)


def _attn_pre_kernel(x_ref, sc_ref, sh_ref, ksc_ref, ksh_ref, rot_ref, wq_ref, wkv_ref, *refs, dils, tm):
    x = x_ref[...]
    cos_t, sa_t, sb_t = rot_ref[0], rot_ref[1], rot_ref[2]
    q = _rotary(_bdot(x * (1.0 + sc_ref[...]) + sh_ref[...], wq_ref[...]), cos_t, sa_t, sb_t) * HD ** -0.5
    kv = _bdot(x * (1.0 + ksc_ref[...]) + ksh_ref[...], wkv_ref[...])
    k = _rotary(kv[:, :KVW], cos_t, sa_t, sb_t)
    v = kv[:, KVW:]
    k_ref, v_ref = refs[0], refs[1]
    k_ref[...] = k
    v_ref[...] = v
    if dils is None:
        refs[2][...] = q
        return
    ng = len(dils)
    q_refs, kb_refs, vb_refs = refs[2:2 + ng], refs[2 + ng:2 + 2 * ng], refs[2 + 2 * ng:2 + 3 * ng]
    qb, kb, vb = q.astype(BF16), k.astype(BF16), v.astype(BF16)
    for g, dil in enumerate(dils):
        rows = tm // dil
        blk = jnp.concatenate([qb[:, g * D:(g + 1) * D], kb, vb], axis=1)
        if dil > 1:
            blk = jnp.dot(_class_major_perm(tm, dil), blk, preferred_element_type=F32).astype(BF16)
        for c in range(dil):
            rs = slice(c * rows, (c + 1) * rows)
            q_refs[g][c] = blk[rs, :D]
            kb_refs[g][c] = blk[rs, D:D + KVW]
            vb_refs[g][c] = blk[rs, D + KVW:]


def _attn_pre(x, sc, sh, ksc, ksh, rot, w_q, w_kv, *, per_b_mod, tm, dils=None):
    bsz, t, _ = x.shape
    row = lambda wd: pl.BlockSpec((None, tm, wd), lambda b, i: (b, i, 0))
    mod = pl.BlockSpec((None, 1, D), lambda b, i: (b, 0, 0)) if per_b_mod else row(D)
    nq = w_q.shape[1]
    out_specs = [row(KVW), row(KVW)]
    out_shape = [jax.ShapeDtypeStruct((bsz, t, KVW), F32)] * 2
    scratch = []
    if dils is None:
        out_specs.append(row(nq))
        out_shape.append(jax.ShapeDtypeStruct((bsz, t, nq), F32))
    else:
        for wd in (D, KVW, KVW):
            for dil in dils:
                out_specs.append(pl.BlockSpec((None, dil, tm // dil, wd), lambda b, i: (b, 0, i, 0)))
                out_shape.append(jax.ShapeDtypeStruct((bsz, dil, t // dil, wd), BF16))
    outs = pl.pallas_call(
        functools.partial(_attn_pre_kernel, dils=dils, tm=tm),
        grid=(bsz, t // tm),
        in_specs=[row(D), mod, mod, mod, mod, pl.BlockSpec((3, tm, LANES), lambda b, i: (0, i, 0)),
                  pl.BlockSpec((D, nq), lambda b, i: (0, 0)), pl.BlockSpec((D, 2 * KVW), lambda b, i: (0, 0))],
        out_specs=out_specs,
        out_shape=out_shape,
        scratch_shapes=scratch,
        compiler_params=_cp(("arbitrary", "arbitrary")),
        name="attn_pre",
    )(x, sc, sh, ksc, ksh, rot, w_q, w_kv)
    if dils is None:
        return outs
    ng = len(dils)
    return outs[0], outs[1], outs[2:2 + ng], outs[2 + ng:2 + 2 * ng], outs[2 + 2 * ng:]


def _attn_band_kernel(q_ref, ka_ref, kb_ref, va_ref, vb_ref, o_ref, lse_ref):
    i = pl.program_id(2)
    bq = ATT_WIN
    kcat = jnp.concatenate([ka_ref[...], kb_ref[...]], axis=0)
    vcat = jnp.concatenate([va_ref[...], vb_ref[...]], axis=0)
    qs = jnp.concatenate([q_ref[:, rep * KVW:(rep + 1) * KVW] for rep in range(REP)], axis=0)
    qrow = lax.broadcasted_iota(I32, (REP * bq, 2 * bq), 0) % bq
    kcol = lax.broadcasted_iota(I32, (REP * bq, 2 * bq), 1)
    valid = (kcol >= qrow) & (kcol <= qrow + bq) & ((i > 0) | (kcol >= bq))
    lane = lax.broadcasted_iota(I32, (REP * bq, KVW), 1) // HD
    kv_lane = lax.broadcasted_iota(I32, (2 * bq, KVW), 1) // HD
    lse_lane = lax.broadcasted_iota(I32, (bq, LANES), 1)
    lse = jnp.zeros((bq, LANES), F32)
    linv = jnp.ones((REP * bq, KVW), F32)
    ps = []
    for h in range(KV_HEADS):
        kh = jnp.where(kv_lane == h, kcat, jnp.zeros_like(kcat))
        s = lax.dot_general(qs, kh, (((1,), (1,)), ((), ())), preferred_element_type=F32)
        s = jnp.where(valid, s, -jnp.inf)
        m = jnp.max(s, axis=-1, keepdims=True)
        p = jnp.exp(s - m)
        l = jnp.sum(p, axis=-1, keepdims=True)
        ps.append(p.astype(BF16))
        linv = jnp.where(lane == h, 1.0 / l, linv)
        lse_h = m + jnp.log(l)
        for rep in range(REP):
            lse = jnp.where(lse_lane == rep * KV_HEADS + h, lse_h[rep * bq:(rep + 1) * bq], lse)
    vstack = jnp.concatenate([jnp.where(kv_lane == h, vcat, jnp.zeros_like(vcat)) for h in range(KV_HEADS)],
                             axis=0)
    o = (jnp.dot(jnp.concatenate(ps, axis=1), vstack, preferred_element_type=F32) * linv).astype(BF16)
    for rep in range(REP):
        o_ref[:, rep * KVW:(rep + 1) * KVW] = o[rep * bq:(rep + 1) * bq]
    lse_ref[...] = lse


def _attn_band(q, kb, vb, dil):
    bsz, _, tc, _ = q.shape
    nb = tc // ATT_WIN
    blk = lambda wd: pl.BlockSpec((None, None, ATT_WIN, wd), lambda b, c, i: (b, c, i, 0))
    prev = pl.BlockSpec((None, None, ATT_WIN, KVW), lambda b, c, i: (b, c, jnp.maximum(i - 1, 0), 0))
    return pl.pallas_call(
        _attn_band_kernel,
        grid=(bsz, dil, nb),
        in_specs=[blk(D), prev, blk(KVW), prev, blk(KVW)],
        out_specs=[blk(D), blk(LANES)],
        out_shape=[jax.ShapeDtypeStruct((bsz, dil, tc, D), BF16), jax.ShapeDtypeStruct((bsz, dil, tc, LANES), F32)],
        compiler_params=_cp(("arbitrary", "arbitrary", "arbitrary")),
        name=f"attn_band_d{dil}",
    )(q, kb, kb, vb, vb)


def _attn_step_kernel(q_ref, kn_ref, vn_ref, kt_ref, vt_ref, o_ref, lse_ref, *, wbuf):
    nq = len(DILATED_GROUPS) * REP
    row = lax.broadcasted_iota(I32, (nq, wbuf), 0)
    pos = lax.broadcasted_iota(I32, (nq, wbuf), 1)
    valid = None
    for g, (_, dil) in enumerate(DILATED_GROUPS):
        ok = (row // REP == g) & (pos >= wbuf - ATT_WIN * dil) & (pos % dil == 0)
        valid = ok if valid is None else valid | ok
    rnd = lambda t: t.astype(BF16).astype(F32)
    lse_lane = lax.broadcasted_iota(I32, (nq, LANES), 1)
    lse = jnp.zeros((nq, LANES), F32)
    for h in range(KV_HEADS):
        hs = slice(h * HD, (h + 1) * HD)
        qh = q_ref[:, hs]
        s = jnp.where(valid, _bdot(qh, kt_ref[h]), -1e30)
        s_n = jnp.sum(rnd(qh) * rnd(kn_ref[:, hs]), axis=-1, keepdims=True)
        m = jnp.maximum(jnp.max(s, axis=-1, keepdims=True), s_n)
        p = jnp.exp(s - m)
        p_n = jnp.exp(s_n - m)
        l = jnp.sum(p, axis=-1, keepdims=True) + p_n
        o_ref[:, hs] = (_bdot_nt(p, vt_ref[h]) + rnd(p_n) * rnd(vn_ref[:, hs])) / l
        lse = jnp.where(lse_lane == h, m + jnp.log(l), lse)
    lse_ref[...] = lse


def _attn_step(q, k_new, v_new, cache_k, cache_v):
    bsz = q.shape[0]
    wbuf = cache_k.shape[1]
    ng = len(DILATED_GROUPS)
    assert all(wbuf % dil == 0 and wbuf >= ATT_WIN * dil for _, dil in DILATED_GROUPS)
    nq = ng * REP
    cache_spec = pl.BlockSpec((None, KV_HEADS, HD, wbuf), lambda b: (b, 0, 0, 0))
    o, lse = pl.pallas_call(
        functools.partial(_attn_step_kernel, wbuf=wbuf),
        grid=(bsz,),
        in_specs=[pl.BlockSpec((None, nq, KVW), lambda b: (b, 0, 0)),
                  pl.BlockSpec((None, 1, KVW), lambda b: (b, 0, 0)),
                  pl.BlockSpec((None, 1, KVW), lambda b: (b, 0, 0)), cache_spec, cache_spec],
        out_specs=[pl.BlockSpec((None, nq, KVW), lambda b: (b, 0, 0)),
                   pl.BlockSpec((None, nq, LANES), lambda b: (b, 0, 0))],
        out_shape=[jax.ShapeDtypeStruct((bsz, nq, KVW), F32), jax.ShapeDtypeStruct((bsz, nq, LANES), F32)],
        compiler_params=_cp(("arbitrary",)),
        name="attn_step",
    )(q.reshape(bsz, nq, KVW), k_new.reshape(bsz, 1, KVW), v_new.reshape(bsz, 1, KVW),
      cache_k.transpose(0, 2, 3, 1), cache_v.transpose(0, 2, 3, 1))
    o = o.reshape(bsz, ng, D).transpose(1, 0, 2)
    lse = lse[:, :, :KV_HEADS].reshape(bsz, ng, REP * KV_HEADS).transpose(1, 0, 2)
    return o, jnp.pad(lse, ((0, 0), (0, 0), (0, LANES - REP * KV_HEADS)))


def _attn_post_kernel(o0_ref, o1_ref, o2_ref, l0_ref, l1_ref, l2_ref, x_ref, gt_ref, vec_ref, wo_ref, out_ref,
                      *scratch, dils, tm):
    o_refs = [o0_ref, o1_ref, o2_ref]
    l_refs = [l0_ref, l1_ref, l2_ref]
    if dils is not None:
        for g, dil in enumerate(dils):
            if dil == 1:
                o_refs[g], l_refs[g] = o_refs[g].at[0], l_refs[g].at[0]
                continue
            sl = scratch[g]
            for c in range(dil):
                sl[_class_rows(c, tm // dil, dil), :] = l_refs[g][c]
            l_refs[g] = sl
            o_cm = jnp.concatenate([o_refs[g][c] for c in range(dil)], axis=0)
            o_refs[g] = jnp.dot(_class_major_perm(tm, dil, inverse=True), o_cm, preferred_element_type=F32)
    o0, o1, o2 = (r if isinstance(r, jax.Array) else r[...] for r in o_refs)
    l0, l1, l2 = (r[...] for r in l_refs)
    m = jnp.maximum(jnp.maximum(l0, l1), l2)
    e0, e1, e2 = jnp.exp(l0 - m), jnp.exp(l1 - m), jnp.exp(l2 - m)
    den = e0 + e1 + e2
    r = lax.broadcasted_iota(I32, (LANES, D), 0)
    c = lax.broadcasted_iota(I32, (LANES, D), 1) // HD
    spread = (r == c).astype(BF16)
    mix = (_split_dot(e0 / den, spread) * o0 + _split_dot(e1 / den, spread) * o1
           + _split_dot(e2 / den, spread) * o2)
    out = _bdot(mix, wo_ref[...])
    out_ref[...] = _layer_norm(DN_ALPHA * x_ref[...] + gt_ref[...] * out, vec_ref[0:1, :], vec_ref[1:2, :])


def _attn_post(os, lses, x, gt, vec, w_o, *, per_b_mod, tm, dils=None):
    bsz, t, _ = x.shape
    row = lambda wd: pl.BlockSpec((None, tm, wd), lambda b, i: (b, i, 0))
    mod = pl.BlockSpec((None, 1, D), lambda b, i: (b, 0, 0)) if per_b_mod else row(D)
    if dils is None:
        o_specs, l_specs, scratch = [row(D)] * 3, [row(LANES)] * 3, []
    else:
        cls = lambda dil, wd: pl.BlockSpec((None, dil, tm // dil, wd), lambda b, i: (b, 0, i, 0))
        o_specs = [cls(dil, D) for dil in dils]
        l_specs = [cls(dil, LANES) for dil in dils]
        scratch = [pltpu.VMEM((tm, LANES), F32) for _ in dils]
    return pl.pallas_call(
        functools.partial(_attn_post_kernel, dils=dils, tm=tm),
        grid=(bsz, t // tm),
        in_specs=o_specs + l_specs + [row(D), mod, pl.BlockSpec((2, D), lambda b, i: (0, 0)),
                                      pl.BlockSpec((D, D), lambda b, i: (0, 0))],
        out_specs=row(D),
        out_shape=jax.ShapeDtypeStruct(x.shape, F32),
        scratch_shapes=scratch,
        compiler_params=_cp(("arbitrary", "arbitrary")),
        name="attn_post",
    )(*os, *lses, x, gt, vec, w_o)


def _rot_tables(pos):
    half = ROT_DIM // 2
    inv = ROPE_THETA ** (-jnp.arange(half, dtype=F32) * 2.0 / ROT_DIM)
    ang = pos.astype(F32)[:, None] * inv[None, :]
    cos, sin = jnp.cos(ang), jnp.sin(ang)
    t = pos.shape[0]
    one = jnp.ones((t, HD - ROT_DIM), F32)
    zero = jnp.zeros((t, HD - ROT_DIM), F32)
    zh = jnp.zeros((t, half), F32)
    cos_t = jnp.concatenate([cos, cos, one], axis=1)
    sa_t = jnp.concatenate([-sin, zh, zero], axis=1)
    sb_t = jnp.concatenate([zh, sin, zero], axis=1)
    return jnp.stack([jnp.tile(x, (1, 2)) for x in (cos_t, sa_t, sb_t)])


def _prep_weights(p):
    pad_c = lambda m, n: jnp.pad(m, ((0, 0), (0, n - m.shape[1])))
    pad_r = lambda m, n: jnp.pad(m, ((0, n - m.shape[0]), (0, 0)))
    w = {}
    w["mu"] = p["rwkv_mu"][0]
    w["w_rkv"] = p["rwkv_w_rkv"][0].astype(BF16)
    w["w1"] = pad_c(p["rwkv_w1"][0], LANES).astype(BF16)
    w["w2"] = pad_r(p["rwkv_w2"][0], LANES).astype(BF16)
    w["a1"] = pad_c(p["rwkv_a1"][0], LANES).astype(BF16)
    w["a2"] = pad_r(p["rwkv_a2"][0], LANES).astype(BF16)
    w["g1"] = pad_c(p["rwkv_g1"][0], 2 * LANES).astype(BF16)
    w["g2"] = pad_r(p["rwkv_g2"][0], 2 * LANES).astype(BF16)
    w["vec"] = jnp.stack([p["rwkv_w0"][0], p["rwkv_a0"][0], p["rwkv_k_k"][0], p["rwkv_k_a"][0]])
    w["post_vec"] = jnp.stack([p["rwkv_r_k"][0].reshape(D), p["rwkv_lnx_g"][0], p["rwkv_lnx_b"][0],
                               p["ln_g"][0, 0], p["ln_b"][0, 0]])
    w["rwkv_w_o"] = p["rwkv_w_o"][0].astype(BF16)
    wq = p["w_q"][0].reshape(D, 3, KV_HEADS, REP, HD).transpose(0, 1, 3, 2, 4).reshape(D, 3 * D)
    w["w_q"] = wq.astype(BF16)
    w["w_kv"] = p["w_kv"].astype(BF16)
    wo = p["w_o_attn"][0].reshape(KV_HEADS, REP, HD, D).transpose(1, 0, 2, 3).reshape(D, D)
    w["w_o_attn"] = wo.astype(BF16)
    w["router_w_t"] = p["router_w"].T.astype(BF16)
    w["router_b"] = p["router_b"].reshape(N_EXPERTS, 1)
    w["moe_w_in"] = p["moe_w_in"]
    w["moe_w_out"] = p["moe_w_out"]
    w["ln"] = [[jnp.stack([p["ln_g"][l, i], p["ln_b"][l, i]]) for i in range(2)] for l in range(DEPTH)]
    return w


def _modulations(c_prompt, c_sample, p):
    nb = c_prompt.shape[0]
    c = jnp.concatenate([c_prompt, c_sample], axis=0)
    pad = (-c.shape[0]) % 8
    c = jnp.pad(c, ((0, pad), (0, 0)))
    m3 = _ada_linear(c, p["ada_w"].reshape(2 * DEPTH, D, 3 * D), p["ada_b"].reshape(2 * DEPTH, 3 * D))
    m2 = _ada_linear(c, p["kv_ada_w"][None], p["kv_ada_b"][None])[0]
    n_all = nb + c_sample.shape[0]

    def split(m, parts, lo, hi):
        return [m[lo:hi, i * D:(i + 1) * D] for i in range(parts)]

    out = {}
    for name, lo, hi in (("prompt", 0, nb), ("sample", nb, n_all)):
        out[name] = {"ada": [[split(m3[2 * l + i], 3, lo, hi) for i in range(2)] for l in range(DEPTH)],
                     "kv": split(m2, 2, lo, hi)}
    return out


def _trunk_prompt(x, mods, w):
    bsz, t, _ = x.shape
    per_b = lambda m: m[:, None, :]
    sh, sc, gt = (per_b(m) for m in mods["ada"][0][0])
    zeros = jnp.zeros((bsz, 1, D), F32)
    r, lw, k, v, a, b, g, last = _rwkv_pre(x, sc, sh, zeros, w, seq_mode=True, tm=256)
    y, zf = _wkv_chunked(r, lw, k, v, a, b)
    x = _rwkv_post(y, r, k, v, g, x, gt, w["post_vec"], w["rwkv_w_o"], pairs=True, tm=256)
    zf = zf.reshape(bsz, PAIRS, 2, HD, 2, HD)
    wkv = jnp.stack([zf[:, :, 0, :, 0, :], zf[:, :, 1, :, 1, :]], axis=2)
    wkv = wkv.reshape(bsz, HEADS, HD, HD).transpose(0, 1, 3, 2)
    sh2, sc2, gt2 = (per_b(m) for m in mods["ada"][0][1])
    x = _moe_sorted(x, sc2, sh2, gt2, w["ln"][0][1], w, 0)

    sh, sc, gt = (per_b(m) for m in mods["ada"][1][0])
    ksh, ksc = (per_b(m) for m in mods["kv"])
    rot = _rot_tables(jnp.arange(t, dtype=I32))
    dils = tuple(dil for _, dil in DILATED_GROUPS)
    k_new, v_new, qs, kbs, vbs = _attn_pre(x, sc, sh, ksc, ksh, rot, w["w_q"], w["w_kv"], per_b_mod=True,
                                           tm=256, dils=dils)
    os, lses = zip(*[_attn_band(qs[g], kbs[g], vbs[g], dil) for g, dil in enumerate(dils)])
    x = _attn_post(os, lses, x, gt, w["ln"][1][0], w["w_o_attn"], per_b_mod=True, tm=256, dils=dils)
    sh2, sc2, gt2 = (per_b(m) for m in mods["ada"][1][1])
    x = _moe_sorted(x, sc2, sh2, gt2, w["ln"][1][1], w, 1)
    return x, wkv[None], last.reshape(1, bsz, D), k_new, v_new


def _trunk_sample(x, mods, state_wkv, state_shift, cache_k, cache_v, w):
    bsz = x.shape[0]
    xs = x.reshape(1, bsz, D)
    row = lambda m: m[None]
    sh, sc, gt = (row(m) for m in mods["ada"][0][0])
    r, lw, k, v, a, b, g, hm = _rwkv_pre(xs, sc, sh, state_shift[0][None], w, seq_mode=False, tm=bsz)
    s_new, y = _wkv_step(state_wkv[0].transpose(1, 2, 3, 0), *(t[0].T for t in (r, lw, k, v, a, b)))
    s_new = s_new.transpose(3, 0, 1, 2)
    xs = _rwkv_post(y.T[None], r, k, v, g, xs, gt, w["post_vec"], w["rwkv_w_o"], pairs=False, tm=bsz)
    sh2, sc2, gt2 = (row(m) for m in mods["ada"][0][1])
    xs = _moe_dense(xs, sc2, sh2, gt2, w["ln"][0][1], w, 0)

    sh, sc, gt = (row(m) for m in mods["ada"][1][0])
    ksh, ksc = (row(m) for m in mods["kv"])
    rot = _rot_tables(jnp.full((bsz,), PAST_LEN, I32))
    k_new, v_new, q = _attn_pre(xs, sc, sh, ksc, ksh, rot, w["w_q"], w["w_kv"], per_b_mod=False, tm=bsz)
    o, lse = _attn_step(q[0], k_new[0], v_new[0], cache_k, cache_v)
    xs = _attn_post([o[i][None] for i in range(3)], [lse[i][None] for i in range(3)], xs, gt, w["ln"][1][0],
                    w["w_o_attn"], per_b_mod=False, tm=bsz)
    sh2, sc2, gt2 = (row(m) for m in mods["ada"][1][1])
    xs = _moe_dense(xs, sc2, sh2, gt2, w["ln"][1][1], w, 1)
    return (xs.reshape(bsz, 1, D), s_new[None], hm, k_new.reshape(bsz, 1, KV_HEADS, HD),
            v_new.reshape(bsz, 1, KV_HEADS, HD))


def kernel(x_prompt, x_sample, state_wkv, state_shift, cache_k, cache_v, c_prompt, c_sample, ada_w, ada_b, ln_g, ln_b, rwkv_mu, rwkv_w_rkv, rwkv_w0, rwkv_w1, rwkv_w2, rwkv_a0, rwkv_a1, rwkv_a2, rwkv_g1, rwkv_g2, rwkv_k_k, rwkv_k_a, rwkv_r_k, rwkv_lnx_g, rwkv_lnx_b, rwkv_w_o, w_q, w_kv, kv_ada_w, kv_ada_b, w_o_attn, router_w, router_b, moe_w_in, moe_w_out):
    p = {"ada_w": ada_w, "ada_b": ada_b, "ln_g": ln_g, "ln_b": ln_b, "rwkv_mu": rwkv_mu, "rwkv_w_rkv": rwkv_w_rkv,
         "rwkv_w0": rwkv_w0, "rwkv_w1": rwkv_w1, "rwkv_w2": rwkv_w2, "rwkv_a0": rwkv_a0, "rwkv_a1": rwkv_a1,
         "rwkv_a2": rwkv_a2, "rwkv_g1": rwkv_g1, "rwkv_g2": rwkv_g2, "rwkv_k_k": rwkv_k_k, "rwkv_k_a": rwkv_k_a,
         "rwkv_r_k": rwkv_r_k, "rwkv_lnx_g": rwkv_lnx_g, "rwkv_lnx_b": rwkv_lnx_b, "rwkv_w_o": rwkv_w_o,
         "w_q": w_q, "w_kv": w_kv, "kv_ada_w": kv_ada_w, "kv_ada_b": kv_ada_b, "w_o_attn": w_o_attn,
         "router_w": router_w, "router_b": router_b, "moe_w_in": moe_w_in, "moe_w_out": moe_w_out}
    w = _prep_weights(p)
    mods = _modulations(c_prompt, c_sample, p)
    bp, tp, _ = x_prompt.shape
    y_p, wkv_p, shift_p, k_p, v_p = _trunk_prompt(x_prompt, mods["prompt"], w)
    keep = min(PAST_LEN, tp)
    k_p = k_p[:, tp - keep:].reshape(bp, keep, KV_HEADS, HD)
    v_p = v_p[:, tp - keep:].reshape(bp, keep, KV_HEADS, HD)
    y_s, wkv_s, shift_s, k_s, v_s = _trunk_sample(x_sample, mods["sample"], state_wkv, state_shift, cache_k,
                                                  cache_v, w)
    return (y_p, y_s, wkv_p, shift_p, k_p, v_p, wkv_s, shift_s, k_s, v_s)
```

```python
import functools
import math

import jax
import jax.numpy as jnp
from jax import lax
from jax.experimental import pallas as pl
from jax.experimental.pallas import tpu as pltpu

F32 = jnp.float32
BF16 = jnp.bfloat16
I32 = jnp.int32

D = 1024
HEADS = 16
HD = 64
LANES = 128
PAIRS = D // LANES
CHUNK = 64
WKV_GROUP = 16
N_EXPERTS = 32
EXPERTS_PER_GROUP = 8
N_EXPERT_GROUPS = 4
D_EXPERT = 512
MOE_BLK = 256
DMA_UNROLL = 8
KV_HEADS = 4
REP = 4
KVW = KV_HEADS * HD
DILATED_GROUPS = ((128, 1), (512, 4), (2048, 16))
ATT_WIN = 128
PAST_LEN = 2048
ROT_DIM = 16
ROPE_THETA = 500000.0
DEPTH = 2
DN_ALPHA = (2 * DEPTH) ** 0.25
LN_EPS = 1e-5
GN_EPS = 64e-5
VMEM_LIMIT = 56 * 1024 * 1024


def _cp(sem):
    return pltpu.CompilerParams(dimension_semantics=sem, vmem_limit_bytes=VMEM_LIMIT)


def _bdot(a, b):
    return jnp.dot(a.astype(BF16), b.astype(BF16), preferred_element_type=F32)


def _bdot_nt(a, b):
    return lax.dot_general(a.astype(BF16), b.astype(BF16), (((1,), (1,)), ((), ())),
                           preferred_element_type=F32)


def _bdot_tn(a, b):
    return lax.dot_general(a.astype(BF16), b.astype(BF16), (((0,), (0,)), ((), ())),
                           preferred_element_type=F32)


def _split_dot(x, m):
    hi = x.astype(BF16)
    lo = (x - hi.astype(F32)).astype(BF16)
    return jnp.dot(jnp.concatenate([hi, lo], axis=1), jnp.concatenate([m, m], axis=0),
                   preferred_element_type=F32)


def _head_ones():
    r = lax.broadcasted_iota(I32, (LANES, LANES), 0) // HD
    c = lax.broadcasted_iota(I32, (LANES, LANES), 1) // HD
    return (r == c).astype(BF16)


def _segsum(x, ones_bd):
    cols = [_split_dot(x[:, c * LANES:(c + 1) * LANES], ones_bd) for c in range(x.shape[1] // LANES)]
    return cols[0] if len(cols) == 1 else jnp.concatenate(cols, axis=1)


def _layer_norm(x, g, b):
    mu = jnp.mean(x, axis=-1, keepdims=True)
    xc = x - mu
    var = jnp.mean(xc * xc, axis=-1, keepdims=True)
    return xc * lax.rsqrt(var + LN_EPS) * g + b


def _sigmoid(x):
    return 1.0 / (1.0 + jnp.exp(-x))


def _to_pairs(ref, val):
    for p in range(PAIRS):
        ref[p] = val[:, p * LANES:(p + 1) * LANES]


def _from_pairs(ref):
    return jnp.concatenate([ref[p] for p in range(PAIRS)], axis=1)


def _ada_kernel(c_ref, w_ref, b_ref, o_ref):
    c = c_ref[...]
    o_ref[...] = _bdot(c * _sigmoid(c), w_ref[...]) + b_ref[...]


def _ada_linear(c, w, b, tn=512):
    s, _, n = w.shape
    m = c.shape[0]
    return pl.pallas_call(
        _ada_kernel,
        grid=(s, n // tn),
        in_specs=[pl.BlockSpec((m, D), lambda i, j: (0, 0)),
                  pl.BlockSpec((None, D, tn), lambda i, j: (i, 0, j)),
                  pl.BlockSpec((None, 1, tn), lambda i, j: (i, 0, j))],
        out_specs=pl.BlockSpec((None, m, tn), lambda i, j: (i, 0, j)),
        out_shape=jax.ShapeDtypeStruct((s, m, n), F32),
        compiler_params=_cp(("arbitrary", "arbitrary")),
        name="ada_linear",
    )(c, w, b.reshape(s, 1, n))


def _rwkv_pre_kernel(x_ref, sc_ref, sh_ref, prev_ref, mu_ref, wrkv_ref, w1_ref, w2_ref, a1_ref, a2_ref,
                     g1_ref, g2_ref, vec_ref,
                     r_ref, lw_ref, k_ref, v_ref, a_ref, b_ref, g_ref, hm_ref, carry_ref, *, seq_mode, tm):
    x = x_ref[...]
    hm = x * (1.0 + sc_ref[...]) + sh_ref[...]
    if seq_mode:
        @pl.when(pl.program_id(1) == 0)
        def _():
            carry_ref[...] = prev_ref[...]
        row = lax.broadcasted_iota(I32, hm.shape, 0)
        hprev = jnp.where(row == 0, carry_ref[...], pltpu.roll(hm, 1, axis=0))
        carry_ref[...] = hm[tm - 1:tm, :]
        hm_ref[...] = hm[tm - 1:tm, :]
    else:
        hprev = prev_ref[...]
        hm_ref[...] = hm
    xx = hprev - hm

    def mix(i):
        return hm + xx * mu_ref[i:i + 1, :]

    w0, a0, k_k, k_a = (vec_ref[i:i + 1, :] for i in range(4))
    r = _bdot(mix(0), wrkv_ref[0])
    k = _bdot(mix(2), wrkv_ref[1])
    v = _bdot(mix(3), wrkv_ref[2])
    wl = w0 + _bdot(jnp.tanh(_bdot(mix(1), w1_ref[...])), w2_ref[...])
    lw = -math.exp(-0.5) * _sigmoid(wl)
    a_lr = _sigmoid(a0 + _bdot(_bdot(mix(4), a1_ref[...]), a2_ref[...]))
    g = _bdot(_sigmoid(_bdot(mix(5), g1_ref[...])), g2_ref[...])
    kk = k * k_k
    kk = kk * jnp.minimum(lax.rsqrt(_segsum(kk * kk, _head_ones())), 1e12)
    kmod = k * (1.0 + (a_lr - 1.0) * k_a)
    outs = ((r_ref, r), (lw_ref, lw), (k_ref, kmod), (v_ref, v), (a_ref, -kk), (b_ref, kk * a_lr), (g_ref, g))
    for ref, val in outs:
        if seq_mode:
            _to_pairs(ref, val)
        else:
            ref[...] = val


def _rwkv_pre(x, sc, sh, prev, w, *, seq_mode, tm):
    bsz, t, _ = x.shape
    grid = (bsz, t // tm)
    row = pl.BlockSpec((None, tm, D), lambda b, i: (b, i, 0))
    per_b = pl.BlockSpec((None, 1, D), lambda b, i: (b, 0, 0))
    mod = per_b if seq_mode else row

    def const(shape):
        return pl.BlockSpec(shape, lambda b, i: (0,) * len(shape))

    if seq_mode:
        out_big = pl.BlockSpec((None, PAIRS, tm, LANES), lambda b, i: (b, 0, i, 0))
        big_shape = jax.ShapeDtypeStruct((bsz, PAIRS, t, LANES), F32)
        hm_spec, hm_shape = per_b, jax.ShapeDtypeStruct((bsz, 1, D), F32)
    else:
        out_big, big_shape = row, jax.ShapeDtypeStruct((bsz, t, D), F32)
        hm_spec, hm_shape = row, jax.ShapeDtypeStruct((bsz, t, D), F32)
    return pl.pallas_call(
        functools.partial(_rwkv_pre_kernel, seq_mode=seq_mode, tm=tm),
        grid=grid,
        in_specs=[row, mod, mod, mod, const((6, D)), const((3, D, D)), const((D, LANES)), const((LANES, D)),
                  const((D, LANES)), const((LANES, D)), const((D, 2 * LANES)), const((2 * LANES, D)),
                  const((4, D))],
        out_specs=[out_big] * 7 + [hm_spec],
        out_shape=[big_shape] * 7 + [hm_shape],
        scratch_shapes=[pltpu.VMEM((1, D), F32)],
        compiler_params=_cp(("arbitrary", "arbitrary")),
        name="rwkv_pre",
    )(x, sc, sh, prev, w["mu"], w["w_rkv"], w["w1"], w["w2"], w["a1"], w["a2"], w["g1"], w["g2"], w["vec"])


def _wkv_chunk_kernel(r_ref, lw_ref, k_ref, v_ref, a_ref, b_ref, y_ref, zf_ref, z_ref, *, n_pairs, group):
    c = pl.program_id(0)

    @pl.when(c == 0)
    def _():
        z_ref[...] = jnp.zeros_like(z_ref)

    L = CHUNK
    row_l = lax.broadcasted_iota(I32, (L, L), 0)
    col_l = lax.broadcasted_iota(I32, (L, L), 1)
    tri_incl = (row_l >= col_l).astype(BF16)
    lane = lax.broadcasted_iota(I32, (L, LANES), 1)
    head0 = lane < HD
    row = lax.broadcasted_iota(I32, (LANES, LANES), 0)
    col = lax.broadcasted_iota(I32, (LANES, LANES), 1)
    strict = row > col
    incl = row >= col
    eye = (row == col).astype(F32)

    def expand(x):
        return jnp.concatenate([jnp.where(head0, x, 0.0), jnp.where(head0, 0.0, x)], axis=0)

    def group_body(gi, carry):
        ids = [gi * group + j for j in range(group)]
        bp = [(i // PAIRS, i % PAIRS) for i in ids]
        G = range(group)
        r = [r_ref[b_i, p_i] for b_i, p_i in bp]
        lw = [lw_ref[b_i, p_i] for b_i, p_i in bp]
        k = [k_ref[b_i, p_i] for b_i, p_i in bp]
        v = [v_ref[b_i, p_i] for b_i, p_i in bp]
        a = [a_ref[b_i, p_i] for b_i, p_i in bp]
        b = [b_ref[b_i, p_i] for b_i, p_i in bp]
        z0 = [z_ref[i] for i in ids]
        cum = [_split_dot_left(tri_incl, lw[j]) for j in G]
        cum_l = [cum[j][L - 1:L, :] for j in G]
        inv = [jnp.exp(-cum[j]) for j in G]
        tail = [jnp.exp(cum_l[j] - cum[j]) for j in G]
        a_e = [expand(a[j] * jnp.exp(cum[j] - lw[j])) for j in G]
        r_e = [expand(r[j] * jnp.exp(cum[j])) for j in G]
        b_e = [expand(b[j] * inv[j]) for j in G]
        k_e = [expand(k[j] * inv[j]) for j in G]
        bd_e = [expand(b[j] * tail[j]) for j in G]
        kd_e = [expand(k[j] * tail[j]) for j in G]
        v_e = [expand(v[j]) for j in G]
        gm = [_bdot_nt(jnp.concatenate([a_e[j], r_e[j]], axis=0), jnp.concatenate([b_e[j], k_e[j]], axis=0))
              for j in G]
        m_ab = [jnp.where(strict, gm[j][:LANES, :LANES], 0.0) for j in G]
        m_ak = [jnp.where(strict, gm[j][:LANES, LANES:], 0.0) for j in G]
        m_rb = [jnp.where(incl, gm[j][LANES:, :LANES], 0.0) for j in G]
        m_rk = [jnp.where(incl, gm[j][LANES:, LANES:], 0.0) for j in G]
        tinv = [eye + m_ab[j] for j in G]
        pw = [_bdot(m_ab[j], m_ab[j]) for j in G]
        for _ in range(4):
            both = [_bdot(pw[j], jnp.concatenate([pw[j], tinv[j]], axis=1)) for j in G]
            pw = [both[j][:, :LANES] for j in G]
            tinv = [tinv[j] + both[j][:, LANES:] for j in G]
        tinv = [tinv[j] + _bdot(pw[j], tinv[j]) for j in G]
        rhs = [_bdot(a_e[j], z0[j]) + _bdot(m_ak[j], v_e[j]) for j in G]
        u = [_bdot(tinv[j], rhs[j]) for j in G]
        zuv = [jnp.concatenate([z0[j], u[j], v_e[j]], axis=0) for j in G]
        decay = [eye * jnp.exp(cum_l[j]) for j in G]
        for j in G:
            z_ref[ids[j]] = _bdot_tn(jnp.concatenate([decay[j], bd_e[j], kd_e[j]], axis=0), zuv[j])
            y_e = _bdot(jnp.concatenate([r_e[j], m_rb[j], m_rk[j]], axis=1), zuv[j])
            y_ref[bp[j][0], bp[j][1]] = y_e[:L] + y_e[L:]
        return carry

    lax.fori_loop(0, n_pairs // group, group_body, 0)

    @pl.when(c == pl.num_programs(0) - 1)
    def _():
        zf_ref[...] = z_ref[...]


def _split_dot_left(m, x):
    hi = x.astype(BF16)
    lo = (x - hi.astype(F32)).astype(BF16)
    return (jnp.dot(m, hi, preferred_element_type=F32) + jnp.dot(m, lo, preferred_element_type=F32))


def _wkv_chunked(r, lw, k, v, a, b):
    bsz, _, t, _ = r.shape
    n_pairs = bsz * PAIRS
    blk = pl.BlockSpec((bsz, PAIRS, CHUNK, LANES), lambda c: (0, 0, c, 0))
    return pl.pallas_call(
        functools.partial(_wkv_chunk_kernel, n_pairs=n_pairs, group=WKV_GROUP),
        grid=(t // CHUNK,),
        in_specs=[blk] * 6,
        out_specs=[blk, pl.BlockSpec((n_pairs, LANES, LANES), lambda c: (0, 0, 0))],
        out_shape=[jax.ShapeDtypeStruct(r.shape, F32), jax.ShapeDtypeStruct((n_pairs, LANES, LANES), F32)],
        scratch_shapes=[pltpu.VMEM((n_pairs, LANES, LANES), F32)],
        compiler_params=_cp(("arbitrary",)),
        name="wkv_chunked",
    )(r, lw, k, v, a, b)


def _wkv_step_kernel(s_ref, r_ref, lw_ref, k_ref, v_ref, a_ref, b_ref, so_ref, y_ref):
    w = jnp.exp(lw_ref[...])
    a, b, k, r = a_ref[...], b_ref[...], k_ref[...], r_ref[...]

    def body(i, c):
        s = s_ref[i]
        sa = jnp.sum(s * a, axis=0, keepdims=True)
        s_new = s * w + sa * b + v_ref[pl.ds(i, 1), :] * k
        so_ref[i] = s_new
        y_ref[pl.ds(i, 1), :] = jnp.sum(s_new * r, axis=0, keepdims=True)
        return c

    lax.fori_loop(0, HD, body, 0, unroll=4)


def _wkv_step(state_t, r, lw, k, v, a, b):
    bsz = state_t.shape[-1]
    big = pl.BlockSpec((None, HD, HD, bsz), lambda h: (h, 0, 0, 0))
    vec = pl.BlockSpec((HD, bsz), lambda h: (h, 0))
    return pl.pallas_call(
        _wkv_step_kernel,
        grid=(HEADS,),
        in_specs=[big] + [vec] * 6,
        out_specs=[big, vec],
        out_shape=[jax.ShapeDtypeStruct(state_t.shape, F32), jax.ShapeDtypeStruct((D, bsz), F32)],
        compiler_params=_cp(("arbitrary",)),
        name="wkv_step",
    )(state_t, r, lw, k, v, a, b)


def _rwkv_post_kernel(y_ref, r_ref, k_ref, v_ref, g_ref, x_ref, gt_ref, vec_ref, wo_ref, o_ref, *, pairs):
    load = _from_pairs if pairs else (lambda ref: ref[...])
    y, r, k, v, g = (load(ref) for ref in (y_ref, r_ref, k_ref, v_ref, g_ref))
    r_k, lnx_g, lnx_b, ln_g, ln_b = (vec_ref[i:i + 1, :] for i in range(5))
    ones_bd = _head_ones()
    ym = _segsum(y, ones_bd) * (1.0 / HD)
    yc = y - ym
    yv = _segsum(yc * yc, ones_bd) * (1.0 / HD)
    yn = yc * lax.rsqrt(yv + GN_EPS) * lnx_g + lnx_b
    bonus = _segsum(r * k * r_k, ones_bd) * v
    mix = _bdot((yn + bonus) * g, wo_ref[...])
    o_ref[...] = _layer_norm(DN_ALPHA * x_ref[...] + gt_ref[...] * mix, ln_g, ln_b)


def _rwkv_post(y, r, k, v, g, x, gt, vec, w_o, *, pairs, tm):
    bsz, t, _ = x.shape
    row = pl.BlockSpec((None, tm, D), lambda b, i: (b, i, 0))
    big = pl.BlockSpec((None, PAIRS, tm, LANES), lambda b, i: (b, 0, i, 0)) if pairs else row
    mod = pl.BlockSpec((None, 1, D), lambda b, i: (b, 0, 0)) if pairs else row
    return pl.pallas_call(
        functools.partial(_rwkv_post_kernel, pairs=pairs),
        grid=(bsz, t // tm),
        in_specs=[big] * 5 + [row, mod, pl.BlockSpec((5, D), lambda b, i: (0, 0)),
                              pl.BlockSpec((D, D), lambda b, i: (0, 0))],
        out_specs=row,
        out_shape=jax.ShapeDtypeStruct(x.shape, F32),
        compiler_params=_cp(("arbitrary", "arbitrary")),
        name="rwkv_post",
    )(y, r, k, v, g, x, gt, vec, w_o)


def _top2(v):
    io = lax.broadcasted_iota(I32, v.shape, 0)
    m1 = jnp.max(v, axis=0, keepdims=True)
    i1 = jnp.min(jnp.where(v == m1, io, EXPERTS_PER_GROUP), axis=0, keepdims=True)
    v2 = jnp.where(io == i1, -jnp.inf, v)
    m2 = jnp.max(v2, axis=0, keepdims=True)
    i2 = jnp.min(jnp.where(v2 == m2, io, EXPERTS_PER_GROUP), axis=0, keepdims=True)
    return m1 + m2, i1, i2


def _moe_route_kernel(x_ref, sc_ref, sh_ref, rw_ref, rb_ref, h_ref, idx_ref, gate_ref, gmat_ref, cnt_ref,
                      *, tm):
    h = x_ref[...] * (1.0 + sc_ref[...]) + sh_ref[...]
    h_ref[...] = h
    aff = _sigmoid(_bdot_nt(rw_ref[...], h))
    sel = aff + rb_ref[...]
    best = gi = i1 = i2 = None
    for g in range(N_EXPERT_GROUPS):
        sc, j1, j2 = _top2(sel[g * EXPERTS_PER_GROUP:(g + 1) * EXPERTS_PER_GROUP, :])
        if g == 0:
            best, gi, i1, i2 = sc, jnp.zeros_like(j1), j1, j2
        else:
            upd = sc > best
            best = jnp.where(upd, sc, best)
            gi = jnp.where(upd, g, gi)
            i1 = jnp.where(upd, j1, i1)
            i2 = jnp.where(upd, j2, i2)
    e0 = gi * EXPERTS_PER_GROUP + i1
    e1 = gi * EXPERTS_PER_GROUP + i2
    io = lax.broadcasted_iota(I32, (N_EXPERTS, tm), 0)
    oh0 = io == e0
    oh1 = io == e1
    a0 = jnp.sum(jnp.where(oh0, aff, 0.0), axis=0, keepdims=True)
    a1 = jnp.sum(jnp.where(oh1, aff, 0.0), axis=0, keepdims=True)
    den = a0 + a1
    g0 = a0 / den
    g1 = a1 / den
    oh = jnp.where(oh0 | oh1, 1.0, 0.0)
    tr = lax.broadcasted_iota(I32, (tm, tm), 0)
    tc = lax.broadcasted_iota(I32, (tm, tm), 1)
    before = jnp.dot(oh.astype(BF16), (tr < tc).astype(BF16), preferred_element_type=F32)
    rank0 = jnp.sum(jnp.where(oh0, before, 0.0), axis=0, keepdims=True)
    rank1 = jnp.sum(jnp.where(oh1, before, 0.0), axis=0, keepdims=True)
    zi = jnp.zeros((4, tm), I32)
    idx_ref[...] = jnp.concatenate([e0, e1, rank0.astype(I32), rank1.astype(I32), zi], axis=0)
    gate_ref[...] = jnp.concatenate([g0, g1, jnp.zeros((6, tm), F32)], axis=0)
    gmat_ref[...] = jnp.where(oh0, g0, 0.0) + jnp.where(oh1, g1, 0.0)
    cnt_ref[...] = jnp.broadcast_to(jnp.sum(oh, axis=1, keepdims=True), cnt_ref.shape)


def _moe_route(x, sc, sh, router_w_t, router_b, *, per_b_mod, rows_out, tm):
    bsz, t, _ = x.shape
    n = bsz * t
    nt = t // tm
    row = pl.BlockSpec((None, tm, D), lambda b, i: (b, i, 0))
    mod = pl.BlockSpec((None, 1, D), lambda b, i: (b, 0, 0)) if per_b_mod else row
    if rows_out:
        h_spec = pl.BlockSpec((tm, D), lambda b, i: (b * nt + i, 0))
        h_shape = jax.ShapeDtypeStruct((n, D), F32)
    else:
        h_spec, h_shape = row, jax.ShapeDtypeStruct(x.shape, F32)
    tok = lambda rows: pl.BlockSpec((rows, tm), lambda b, i: (0, b * nt + i))
    return pl.pallas_call(
        functools.partial(_moe_route_kernel, tm=tm),
        grid=(bsz, nt),
        in_specs=[row, mod, mod, pl.BlockSpec((N_EXPERTS, D), lambda b, i: (0, 0)),
                  pl.BlockSpec((N_EXPERTS, 1), lambda b, i: (0, 0))],
        out_specs=[h_spec, tok(8), tok(8), tok(N_EXPERTS),
                   pl.BlockSpec((None, N_EXPERTS, LANES), lambda b, i: (b * nt + i, 0, 0))],
        out_shape=[h_shape, jax.ShapeDtypeStruct((8, n), I32), jax.ShapeDtypeStruct((8, n), F32),
                   jax.ShapeDtypeStruct((N_EXPERTS, n), F32),
                   jax.ShapeDtypeStruct((bsz * nt, N_EXPERTS, LANES), F32)],
        compiler_params=_cp(("arbitrary", "arbitrary")),
        name="moe_route",
    )(x, sc, sh, router_w_t, router_b)


def _row_copy(src, src_row, dst, dst_row, sem):
    return pltpu.make_async_copy(src.at[pl.ds(src_row, 1), :], dst.at[pl.ds(dst_row, 1), :], sem)


def _wait_rows(src, dst, rows, sem):
    pltpu.make_async_copy(src.at[pl.ds(0, rows), :], dst.at[pl.ds(0, rows), :], sem).wait()


def _moe_dispatch_kernel(c8_ref, lo_ref, rs_ref, zs_ref, h_ref, idx_ref, tab_ref, xs_ref, slot_ref,
                         buf, zbuf, sem, *, tm):
    step = pl.program_id(0)
    last = pl.num_programs(0) - 1
    s = step % 2

    def runs(tile, sl, start):
        for e in range(N_EXPERTS):
            j = tile * N_EXPERTS + e

            @pl.when(c8_ref[j] > 0)
            def _():
                size = pl.multiple_of(c8_ref[j], 8)
                cp = pltpu.make_async_copy(buf.at[sl, pl.ds(pl.multiple_of(lo_ref[j], 8), size), :],
                                           xs_ref.at[pl.ds(pl.multiple_of(rs_ref[j], 8), size), :], sem.at[sl])
                if start:
                    cp.start()
                else:
                    cp.wait()

    @pl.when(step == 0)
    def _():
        zbuf[...] = jnp.zeros_like(zbuf)

        def block_copy(e):
            return pltpu.make_async_copy(zbuf, xs_ref.at[pl.ds(pl.multiple_of(zs_ref[e], 8), MOE_BLK), :],
                                         sem.at[2])

        fresh = [zs_ref[e] != zs_ref[e - 1] for e in range(1, zs_ref.shape[0])]
        block_copy(0).start()
        for e, f in enumerate(fresh, start=1):
            pl.when(f)(lambda e=e: block_copy(e).start())
        block_copy(0).wait()
        for e, f in enumerate(fresh, start=1):
            pl.when(f)(lambda e=e: block_copy(e).wait())

    @pl.when(step >= 2)
    def _():
        runs(step - 2, s, start=False)

    e0, e1, l0, l1 = (idx_ref[r:r + 1, :] for r in range(4))
    expert = lax.broadcasted_iota(I32, (N_EXPERTS, tm), 0)

    def lookup(col, e):
        return jnp.sum(jnp.where(expert == e, col, 0.0), axis=0, keepdims=True).astype(I32)

    lo_col = tab_ref[:, 0:1].astype(F32)
    rs_col = tab_ref[:, 1:2].astype(F32)
    pos0 = lookup(lo_col, e0) + l0
    pos1 = lookup(lo_col, e1) + l1
    slot_ref[...] = jnp.concatenate([lookup(rs_col, e0) + l0, lookup(rs_col, e1) + l1,
                                     jnp.zeros((6, tm), I32)], axis=0)
    row = lax.broadcasted_iota(I32, (buf.shape[1], tm), 0)
    perm = ((row == pos0) | (row == pos1)).astype(BF16)
    buf[s] = jnp.dot(perm, h_ref[...].astype(BF16), preferred_element_type=F32)
    runs(step, s, start=True)

    @pl.when(step == last)
    def _():
        @pl.when(step >= 1)
        def _():
            runs(step - 1, 1 - s, start=False)
        runs(step, s, start=False)


def _moe_dispatch(h_rows, idx, c8, loff, run_start, zero_start, n_slots, tm):
    n = h_rows.shape[0]
    tiles = n // tm
    buf_rows = 2 * tm + N_EXPERTS * 8
    tab = jnp.pad(jnp.stack([loff, run_start], axis=-1), ((0, 0), (0, 0), (0, LANES - 2)))
    flat = lambda a: a.reshape(-1).astype(I32)
    return pl.pallas_call(
        functools.partial(_moe_dispatch_kernel, tm=tm),
        grid_spec=pltpu.PrefetchScalarGridSpec(
            num_scalar_prefetch=4, grid=(tiles,),
            in_specs=[pl.BlockSpec((tm, D), lambda i, *_: (i, 0)),
                      pl.BlockSpec((8, tm), lambda i, *_: (0, i)),
                      pl.BlockSpec((None, N_EXPERTS, LANES), lambda i, *_: (i, 0, 0))],
            out_specs=[pl.BlockSpec(memory_space=pl.ANY), pl.BlockSpec((8, tm), lambda i, *_: (0, i))],
            scratch_shapes=[pltpu.VMEM((2, buf_rows, D), F32), pltpu.VMEM((MOE_BLK, D), F32),
                            pltpu.SemaphoreType.DMA((3,))]),
        out_shape=[jax.ShapeDtypeStruct((n_slots, D), F32), jax.ShapeDtypeStruct((8, n), I32)],
        compiler_params=_cp(("arbitrary",)),
        name="moe_dispatch",
    )(flat(c8), flat(loff), flat(run_start), zero_start, h_rows, idx, tab)


def _moe_expert_kernel(be_ref, used_ref, nxt_ref, xs_ref, win_hbm, wout_hbm, ys_ref,
                       win_f, wout_f, win_bf, wout_bf, run_ref, sem, *, layer):
    i = pl.program_id(0)
    used = used_ref[0]

    def weight_copies(e, s):
        return (pltpu.make_async_copy(win_hbm.at[layer, e], win_f.at[s], sem.at[s, 0]),
                pltpu.make_async_copy(wout_hbm.at[layer, e], wout_f.at[s], sem.at[s, 1]))

    @pl.when(i == 0)
    def _():
        run_ref[0] = 0
        for cp in weight_copies(be_ref[0], 0):
            cp.start()

    fresh = (i == 0) | (be_ref[i] != be_ref[jnp.maximum(i - 1, 0)])

    @pl.when(fresh & (i < used))
    def _():
        run = run_ref[0] + jnp.where(i > 0, 1, 0)
        run_ref[0] = run
        s = run % 2
        for cp in weight_copies(be_ref[i], s):
            cp.wait()
        win_bf[...] = win_f[s].astype(BF16)
        wout_bf[...] = wout_f[s].astype(BF16)

        @pl.when(nxt_ref[i] >= 0)
        def _():
            for cp in weight_copies(nxt_ref[i], 1 - s):
                cp.start()

    @pl.when(i < used)
    def _():
        hmid = jnp.dot(xs_ref[...].astype(BF16), win_bf[...], preferred_element_type=F32)
        gt = hmid[:, :D_EXPERT]
        up = hmid[:, D_EXPERT:]
        act = gt * _sigmoid(gt) * up
        ys_ref[...] = jnp.dot(act.astype(BF16), wout_bf[...], preferred_element_type=F32)

    @pl.when(i >= used_ref[0])
    def _():
        ys_ref[...] = jnp.zeros_like(ys_ref)


def _moe_experts_sorted(xs, blk_exp, n_used, w_in, w_out, layer):
    n_slots = xs.shape[0]
    n_blocks = n_slots // MOE_BLK
    blk = jnp.arange(n_blocks, dtype=I32)
    later = (blk[None, :] > blk[:, None]) & (blk_exp[None, :] != blk_exp[:, None]) & (blk[None, :] < n_used[0])
    first_later = jnp.min(jnp.where(later, blk[None, :], n_blocks), axis=1)
    nxt = jnp.where(first_later < n_blocks, blk_exp[jnp.minimum(first_later, n_blocks - 1)], -1).astype(I32)
    x_map = lambda i, be, u, nx: (jnp.minimum(i, u[0] - 1), 0)
    return pl.pallas_call(
        functools.partial(_moe_expert_kernel, layer=layer),
        grid_spec=pltpu.PrefetchScalarGridSpec(
            num_scalar_prefetch=3, grid=(n_blocks,),
            in_specs=[pl.BlockSpec((MOE_BLK, D), x_map),
                      pl.BlockSpec(memory_space=pl.ANY), pl.BlockSpec(memory_space=pl.ANY)],
            out_specs=pl.BlockSpec((MOE_BLK, D), lambda i, be, u, nx: (i, 0)),
            scratch_shapes=[pltpu.VMEM((2, D, 2 * D_EXPERT), F32), pltpu.VMEM((2, D_EXPERT, D), F32),
                            pltpu.VMEM((D, 2 * D_EXPERT), BF16), pltpu.VMEM((D_EXPERT, D), BF16),
                            pltpu.SMEM((1,), I32), pltpu.SemaphoreType.DMA((2, 2))]),
        out_shape=jax.ShapeDtypeStruct((n_slots, D), F32),
        compiler_params=_cp(("arbitrary",)),
        name="moe_experts_sorted",
    )(blk_exp, n_used, nxt, xs, w_in, w_out)


def _moe_combine_kernel(slot_ref, ys_ref, gate_ref, x_ref, gt_ref, vec_ref, o_ref, ya, yb, sem, *, tm, n):
    step = pl.program_id(0)
    slot = step % 2

    def gather(s, sl):
        def issue(i, c):
            t = s * tm + i
            _row_copy(ys_ref, slot_ref[t], ya.at[sl], i, sem.at[sl, 0]).start(priority=0)
            _row_copy(ys_ref, slot_ref[n + t], yb.at[sl], i, sem.at[sl, 1]).start(priority=1)
            return c
        lax.fori_loop(0, tm, issue, 0, unroll=DMA_UNROLL)

    @pl.when(step == 0)
    def _():
        gather(0, 0)

    @pl.when(step + 1 < pl.num_programs(0))
    def _():
        gather(step + 1, 1 - slot)

    _wait_rows(ys_ref, ya.at[slot], tm, sem.at[slot, 0])
    _wait_rows(ys_ref, yb.at[slot], tm, sem.at[slot, 1])
    ff = ya[slot] * gate_ref[:, 0:1] + yb[slot] * gate_ref[:, 1:2]
    o_ref[...] = _layer_norm(DN_ALPHA * x_ref[...] + gt_ref[...] * ff, vec_ref[0:1, :], vec_ref[1:2, :])


def _moe_combine(ys, slots, gates, x, gt, vec, tm):
    bsz, t, _ = x.shape
    nt = t // tm
    return pl.pallas_call(
        functools.partial(_moe_combine_kernel, tm=tm, n=bsz * t),
        grid_spec=pltpu.PrefetchScalarGridSpec(
            num_scalar_prefetch=1, grid=(bsz * nt,),
            in_specs=[pl.BlockSpec(memory_space=pl.ANY),
                      pl.BlockSpec((tm, 2), lambda i, sl: (i, 0)),
                      pl.BlockSpec((None, tm, D), lambda i, sl: (i // nt, i % nt, 0)),
                      pl.BlockSpec((None, 1, D), lambda i, sl: (i // nt, 0, 0)),
                      pl.BlockSpec((2, D), lambda i, sl: (0, 0))],
            out_specs=pl.BlockSpec((None, tm, D), lambda i, sl: (i // nt, i % nt, 0)),
            scratch_shapes=[pltpu.VMEM((2, tm, D), F32), pltpu.VMEM((2, tm, D), F32),
                            pltpu.SemaphoreType.DMA((2, 2))]),
        out_shape=jax.ShapeDtypeStruct(x.shape, F32),
        compiler_params=_cp(("arbitrary",)),
        name="moe_combine",
    )(slots, ys, gates, x, gt, vec)


def _moe_sorted(x, sc, sh, gt, vec, w, layer, tm=256):
    bsz, t, _ = x.shape
    n = bsz * t
    tiles = n // tm
    n_blocks = -(-(2 * n + tiles * N_EXPERTS * 7 + N_EXPERTS * (MOE_BLK - 1)) // MOE_BLK)
    h_rows, idx, gate, _, cnt = _moe_route(x, sc, sh, w["router_w_t"], w["router_b"], per_b_mod=True,
                                           rows_out=True, tm=tm)
    c8 = (cnt[:, :, 0].astype(I32) + 7) // 8 * 8
    padded = (jnp.sum(c8, axis=0) + MOE_BLK - 1) // MOE_BLK * MOE_BLK
    pad_end = jnp.cumsum(padded)
    run_start = (pad_end - padded)[None, :] + jnp.cumsum(c8, axis=0) - c8
    loff = jnp.cumsum(c8, axis=1) - c8
    blk_start = jnp.arange(n_blocks, dtype=I32) * MOE_BLK
    blk_exp = jnp.minimum(jnp.sum((pad_end[None, :] <= blk_start[:, None]).astype(I32), axis=1), N_EXPERTS - 1)
    n_used = (pad_end[-1:] // MOE_BLK).astype(I32)
    tail = jnp.minimum(n_used[0] + jnp.arange(N_EXPERTS, dtype=I32), n_blocks - 1) * MOE_BLK
    zero_start = jnp.concatenate([jnp.maximum(pad_end - MOE_BLK, 0), tail])
    xs, slot_rows = _moe_dispatch(h_rows, idx, c8, loff, run_start, zero_start, n_blocks * MOE_BLK, tm)
    slots = jnp.concatenate([slot_rows[0], slot_rows[1]])
    ys = _moe_experts_sorted(xs, blk_exp, n_used, w["moe_w_in"], w["moe_w_out"], layer)
    return _moe_combine(ys, slots, gate[:2].T, x, gt, vec, tm)


def _moe_dense_kernel(h_ref, gm_ref, win_ref, wout_ref, x_ref, gt_ref, vec_ref, o_ref, acc_ref):
    e = pl.program_id(0)

    @pl.when(e == 0)
    def _():
        acc_ref[...] = jnp.zeros_like(acc_ref)

    lane = lax.broadcasted_iota(I32, gm_ref.shape, 1)
    gcol = jnp.sum(jnp.where(lane == e, gm_ref[...], 0.0), axis=1, keepdims=True)
    hmid = _bdot(h_ref[...], win_ref[...])
    gt = hmid[:, :D_EXPERT]
    up = hmid[:, D_EXPERT:]
    y = _bdot(gt * _sigmoid(gt) * up, wout_ref[...])
    acc_ref[...] = acc_ref[...] + jnp.where(gcol != 0.0, gcol * y, 0.0)

    @pl.when(e == N_EXPERTS - 1)
    def _():
        o_ref[...] = _layer_norm(DN_ALPHA * x_ref[...] + gt_ref[...] * acc_ref[...], vec_ref[0:1, :],
                                 vec_ref[1:2, :])


def _moe_dense(x, sc, sh, gt, vec, w, layer):
    _, m, _ = x.shape
    h, _, _, gmat, _ = _moe_route(x, sc, sh, w["router_w_t"], w["router_b"], per_b_mod=False, rows_out=False,
                                  tm=m)
    full = pl.BlockSpec((m, D), lambda e: (0, 0))
    out = pl.pallas_call(
        _moe_dense_kernel,
        grid=(N_EXPERTS,),
        in_specs=[full, pl.BlockSpec((m, N_EXPERTS), lambda e: (0, 0)),
                  pl.BlockSpec((None, None, D, 2 * D_EXPERT), lambda e: (layer, e, 0, 0)),
                  pl.BlockSpec((None, None, D_EXPERT, D), lambda e: (layer, e, 0, 0)),
                  full, full, pl.BlockSpec((2, D), lambda e: (0, 0))],
        out_specs=full,
        out_shape=jax.ShapeDtypeStruct((m, D), F32),
        scratch_shapes=[pltpu.VMEM((m, D), F32)],
        compiler_params=_cp(("arbitrary",)),
        name="moe_dense",
    )(h[0], gmat.T, w["moe_w_in"], w["moe_w_out"], x[0], gt[0], vec)
    return out[None]


def _rotary(x, cos_t, sa_t, sb_t):
    reps = x.shape[1] // LANES
    tile = lambda t: t if reps == 1 else jnp.concatenate([t] * reps, axis=1)
    n = x.shape[1]
    half = ROT_DIM // 2
    return x * tile(cos_t) + pltpu.roll(x, n - half, axis=1) * tile(sa_t) + pltpu.roll(x, half, axis=1) * tile(sb_t)


def _class_rows(c, rows, dil):
    return pl.ds(c, rows, stride=dil) if dil > 1 else pl.ds(0, rows)


def _class_major_perm(tm, dil, inverse=False):
    cm_row = lax.broadcasted_iota(I32, (tm, tm), 1 if inverse else 0)
    tok_row = lax.broadcasted_iota(I32, (tm, tm), 0 if inverse else 1)
    rows = tm // dil
    return ((cm_row // rows == tok_row % dil) & (cm_row % rows == tok_row // dil)).astype(BF16)


def _attn_pre_kernel(x_ref, sc_ref, sh_ref, ksc_ref, ksh_ref, rot_ref, wq_ref, wkv_ref, *refs, dils, tm):
    x = x_ref[...]
    cos_t, sa_t, sb_t = rot_ref[0], rot_ref[1], rot_ref[2]
    q = _rotary(_bdot(x * (1.0 + sc_ref[...]) + sh_ref[...], wq_ref[...]), cos_t, sa_t, sb_t) * HD ** -0.5
    kv = _bdot(x * (1.0 + ksc_ref[...]) + ksh_ref[...], wkv_ref[...])
    k = _rotary(kv[:, :KVW], cos_t, sa_t, sb_t)
    v = kv[:, KVW:]
    k_ref, v_ref = refs[0], refs[1]
    k_ref[...] = k
    v_ref[...] = v
    if dils is None:
        refs[2][...] = q
        return
    ng = len(dils)
    q_refs, kb_refs, vb_refs = refs[2:2 + ng], refs[2 + ng:2 + 2 * ng], refs[2 + 2 * ng:2 + 3 * ng]
    qb, kb, vb = q.astype(BF16), k.astype(BF16), v.astype(BF16)
    for g, dil in enumerate(dils):
        rows = tm // dil
        blk = jnp.concatenate([qb[:, g * D:(g + 1) * D], kb, vb], axis=1)
        if dil > 1:
            blk = jnp.dot(_class_major_perm(tm, dil), blk, preferred_element_type=F32).astype(BF16)
        for c in range(dil):
            rs = slice(c * rows, (c + 1) * rows)
            q_refs[g][c] = blk[rs, :D]
            kb_refs[g][c] = blk[rs, D:D + KVW]
            vb_refs[g][c] = blk[rs, D + KVW:]


def _attn_pre(x, sc, sh, ksc, ksh, rot, w_q, w_kv, *, per_b_mod, tm, dils=None):
    bsz, t, _ = x.shape
    row = lambda wd: pl.BlockSpec((None, tm, wd), lambda b, i: (b, i, 0))
    mod = pl.BlockSpec((None, 1, D), lambda b, i: (b, 0, 0)) if per_b_mod else row(D)
    nq = w_q.shape[1]
    out_specs = [row(KVW), row(KVW)]
    out_shape = [jax.ShapeDtypeStruct((bsz, t, KVW), F32)] * 2
    scratch = []
    if dils is None:
        out_specs.append(row(nq))
        out_shape.append(jax.ShapeDtypeStruct((bsz, t, nq), F32))
    else:
        for wd in (D, KVW, KVW):
            for dil in dils:
                out_specs.append(pl.BlockSpec((None, dil, tm // dil, wd), lambda b, i: (b, 0, i, 0)))
                out_shape.append(jax.ShapeDtypeStruct((bsz, dil, t // dil, wd), BF16))
    outs = pl.pallas_call(
        functools.partial(_attn_pre_kernel, dils=dils, tm=tm),
        grid=(bsz, t // tm),
        in_specs=[row(D), mod, mod, mod, mod, pl.BlockSpec((3, tm, LANES), lambda b, i: (0, i, 0)),
                  pl.BlockSpec((D, nq), lambda b, i: (0, 0)), pl.BlockSpec((D, 2 * KVW), lambda b, i: (0, 0))],
        out_specs=out_specs,
        out_shape=out_shape,
        scratch_shapes=scratch,
        compiler_params=_cp(("arbitrary", "arbitrary")),
        name="attn_pre",
    )(x, sc, sh, ksc, ksh, rot, w_q, w_kv)
    if dils is None:
        return outs
    ng = len(dils)
    return outs[0], outs[1], outs[2:2 + ng], outs[2 + ng:2 + 2 * ng], outs[2 + 2 * ng:]


def _attn_band_kernel(q_ref, ka_ref, kb_ref, va_ref, vb_ref, o_ref, lse_ref):
    i = pl.program_id(2)
    bq = ATT_WIN
    kcat = jnp.concatenate([ka_ref[...], kb_ref[...]], axis=0)
    vcat = jnp.concatenate([va_ref[...], vb_ref[...]], axis=0)
    qs = jnp.concatenate([q_ref[:, rep * KVW:(rep + 1) * KVW] for rep in range(REP)], axis=0)
    qrow = lax.broadcasted_iota(I32, (REP * bq, 2 * bq), 0) % bq
    kcol = lax.broadcasted_iota(I32, (REP * bq, 2 * bq), 1)
    valid = (kcol >= qrow) & (kcol <= qrow + bq) & ((i > 0) | (kcol >= bq))
    lane = lax.broadcasted_iota(I32, (REP * bq, KVW), 1) // HD
    kv_lane = lax.broadcasted_iota(I32, (2 * bq, KVW), 1) // HD
    lse_lane = lax.broadcasted_iota(I32, (bq, LANES), 1)
    lse = jnp.zeros((bq, LANES), F32)
    linv = jnp.ones((REP * bq, KVW), F32)
    ps = []
    for h in range(KV_HEADS):
        kh = jnp.where(kv_lane == h, kcat, jnp.zeros_like(kcat))
        s = lax.dot_general(qs, kh, (((1,), (1,)), ((), ())), preferred_element_type=F32)
        s = jnp.where(valid, s, -jnp.inf)
        m = jnp.max(s, axis=-1, keepdims=True)
        p = jnp.exp(s - m)
        l = jnp.sum(p, axis=-1, keepdims=True)
        ps.append(p.astype(BF16))
        linv = jnp.where(lane == h, 1.0 / l, linv)
        lse_h = m + jnp.log(l)
        for rep in range(REP):
            lse = jnp.where(lse_lane == rep * KV_HEADS + h, lse_h[rep * bq:(rep + 1) * bq], lse)
    vstack = jnp.concatenate([jnp.where(kv_lane == h, vcat, jnp.zeros_like(vcat)) for h in range(KV_HEADS)],
                             axis=0)
    o = (jnp.dot(jnp.concatenate(ps, axis=1), vstack, preferred_element_type=F32) * linv).astype(BF16)
    for rep in range(REP):
        o_ref[:, rep * KVW:(rep + 1) * KVW] = o[rep * bq:(rep + 1) * bq]
    lse_ref[...] = lse


def _attn_band(q, kb, vb, dil):
    bsz, _, tc, _ = q.shape
    nb = tc // ATT_WIN
    blk = lambda wd: pl.BlockSpec((None, None, ATT_WIN, wd), lambda b, c, i: (b, c, i, 0))
    prev = pl.BlockSpec((None, None, ATT_WIN, KVW), lambda b, c, i: (b, c, jnp.maximum(i - 1, 0), 0))
    return pl.pallas_call(
        _attn_band_kernel,
        grid=(bsz, dil, nb),
        in_specs=[blk(D), prev, blk(KVW), prev, blk(KVW)],
        out_specs=[blk(D), blk(LANES)],
        out_shape=[jax.ShapeDtypeStruct((bsz, dil, tc, D), BF16), jax.ShapeDtypeStruct((bsz, dil, tc, LANES), F32)],
        compiler_params=_cp(("arbitrary", "arbitrary", "arbitrary")),
        name=f"attn_band_d{dil}",
    )(q, kb, kb, vb, vb)


def _attn_step_kernel(q_ref, kn_ref, vn_ref, kt_ref, vt_ref, o_ref, lse_ref, *, wbuf):
    nq = len(DILATED_GROUPS) * REP
    row = lax.broadcasted_iota(I32, (nq, wbuf), 0)
    pos = lax.broadcasted_iota(I32, (nq, wbuf), 1)
    valid = None
    for g, (_, dil) in enumerate(DILATED_GROUPS):
        ok = (row // REP == g) & (pos >= wbuf - ATT_WIN * dil) & (pos % dil == 0)
        valid = ok if valid is None else valid | ok
    rnd = lambda t: t.astype(BF16).astype(F32)
    lse_lane = lax.broadcasted_iota(I32, (nq, LANES), 1)
    lse = jnp.zeros((nq, LANES), F32)
    for h in range(KV_HEADS):
        hs = slice(h * HD, (h + 1) * HD)
        qh = q_ref[:, hs]
        s = jnp.where(valid, _bdot(qh, kt_ref[h]), -1e30)
        s_n = jnp.sum(rnd(qh) * rnd(kn_ref[:, hs]), axis=-1, keepdims=True)
        m = jnp.maximum(jnp.max(s, axis=-1, keepdims=True), s_n)
        p = jnp.exp(s - m)
        p_n = jnp.exp(s_n - m)
        l = jnp.sum(p, axis=-1, keepdims=True) + p_n
        o_ref[:, hs] = (_bdot_nt(p, vt_ref[h]) + rnd(p_n) * rnd(vn_ref[:, hs])) / l
        lse = jnp.where(lse_lane == h, m + jnp.log(l), lse)
    lse_ref[...] = lse


def _attn_step(q, k_new, v_new, cache_k, cache_v):
    bsz = q.shape[0]
    wbuf = cache_k.shape[1]
    ng = len(DILATED_GROUPS)
    assert all(wbuf % dil == 0 and wbuf >= ATT_WIN * dil for _, dil in DILATED_GROUPS)
    nq = ng * REP
    cache_spec = pl.BlockSpec((None, KV_HEADS, HD, wbuf), lambda b: (b, 0, 0, 0))
    o, lse = pl.pallas_call(
        functools.partial(_attn_step_kernel, wbuf=wbuf),
        grid=(bsz,),
        in_specs=[pl.BlockSpec((None, nq, KVW), lambda b: (b, 0, 0)),
                  pl.BlockSpec((None, 1, KVW), lambda b: (b, 0, 0)),
                  pl.BlockSpec((None, 1, KVW), lambda b: (b, 0, 0)), cache_spec, cache_spec],
        out_specs=[pl.BlockSpec((None, nq, KVW), lambda b: (b, 0, 0)),
                   pl.BlockSpec((None, nq, LANES), lambda b: (b, 0, 0))],
        out_shape=[jax.ShapeDtypeStruct((bsz, nq, KVW), F32), jax.ShapeDtypeStruct((bsz, nq, LANES), F32)],
        compiler_params=_cp(("arbitrary",)),
        name="attn_step",
    )(q.reshape(bsz, nq, KVW), k_new.reshape(bsz, 1, KVW), v_new.reshape(bsz, 1, KVW),
      cache_k.transpose(0, 2, 3, 1), cache_v.transpose(0, 2, 3, 1))
    o = o.reshape(bsz, ng, D).transpose(1, 0, 2)
    lse = lse[:, :, :KV_HEADS].reshape(bsz, ng, REP * KV_HEADS).transpose(1, 0, 2)
    return o, jnp.pad(lse, ((0, 0), (0, 0), (0, LANES - REP * KV_HEADS)))


def _attn_post_kernel(o0_ref, o1_ref, o2_ref, l0_ref, l1_ref, l2_ref, x_ref, gt_ref, vec_ref, wo_ref, out_ref,
                      *scratch, dils, tm):
    o_refs = [o0_ref, o1_ref, o2_ref]
    l_refs = [l0_ref, l1_ref, l2_ref]
    if dils is not None:
        for g, dil in enumerate(dils):
            if dil == 1:
                o_refs[g], l_refs[g] = o_refs[g].at[0], l_refs[g].at[0]
                continue
            sl = scratch[g]
            for c in range(dil):
                sl[_class_rows(c, tm // dil, dil), :] = l_refs[g][c]
            l_refs[g] = sl
            o_cm = jnp.concatenate([o_refs[g][c] for c in range(dil)], axis=0)
            o_refs[g] = jnp.dot(_class_major_perm(tm, dil, inverse=True), o_cm, preferred_element_type=F32)
    o0, o1, o2 = (r if isinstance(r, jax.Array) else r[...] for r in o_refs)
    l0, l1, l2 = (r[...] for r in l_refs)
    m = jnp.maximum(jnp.maximum(l0, l1), l2)
    e0, e1, e2 = jnp.exp(l0 - m), jnp.exp(l1 - m), jnp.exp(l2 - m)
    den = e0 + e1 + e2
    r = lax.broadcasted_iota(I32, (LANES, D), 0)
    c = lax.broadcasted_iota(I32, (LANES, D), 1) // HD
    spread = (r == c).astype(BF16)
    mix = (_split_dot(e0 / den, spread) * o0 + _split_dot(e1 / den, spread) * o1
           + _split_dot(e2 / den, spread) * o2)
    out = _bdot(mix, wo_ref[...])
    out_ref[...] = _layer_norm(DN_ALPHA * x_ref[...] + gt_ref[...] * out, vec_ref[0:1, :], vec_ref[1:2, :])


def _attn_post(os, lses, x, gt, vec, w_o, *, per_b_mod, tm, dils=None):
    bsz, t, _ = x.shape
    row = lambda wd: pl.BlockSpec((None, tm, wd), lambda b, i: (b, i, 0))
    mod = pl.BlockSpec((None, 1, D), lambda b, i: (b, 0, 0)) if per_b_mod else row(D)
    if dils is None:
        o_specs, l_specs, scratch = [row(D)] * 3, [row(LANES)] * 3, []
    else:
        cls = lambda dil, wd: pl.BlockSpec((None, dil, tm // dil, wd), lambda b, i: (b, 0, i, 0))
        o_specs = [cls(dil, D) for dil in dils]
        l_specs = [cls(dil, LANES) for dil in dils]
        scratch = [pltpu.VMEM((tm, LANES), F32) for _ in dils]
    return pl.pallas_call(
        functools.partial(_attn_post_kernel, dils=dils, tm=tm),
        grid=(bsz, t // tm),
        in_specs=o_specs + l_specs + [row(D), mod, pl.BlockSpec((2, D), lambda b, i: (0, 0)),
                                      pl.BlockSpec((D, D), lambda b, i: (0, 0))],
        out_specs=row(D),
        out_shape=jax.ShapeDtypeStruct(x.shape, F32),
        scratch_shapes=scratch,
        compiler_params=_cp(("arbitrary", "arbitrary")),
        name="attn_post",
    )(*os, *lses, x, gt, vec, w_o)


def _rot_tables(pos):
    half = ROT_DIM // 2
    inv = ROPE_THETA ** (-jnp.arange(half, dtype=F32) * 2.0 / ROT_DIM)
    ang = pos.astype(F32)[:, None] * inv[None, :]
    cos, sin = jnp.cos(ang), jnp.sin(ang)
    t = pos.shape[0]
    one = jnp.ones((t, HD - ROT_DIM), F32)
    zero = jnp.zeros((t, HD - ROT_DIM), F32)
    zh = jnp.zeros((t, half), F32)
    cos_t = jnp.concatenate([cos, cos, one], axis=1)
    sa_t = jnp.concatenate([-sin, zh, zero], axis=1)
    sb_t = jnp.concatenate([zh, sin, zero], axis=1)
    return jnp.stack([jnp.tile(x, (1, 2)) for x in (cos_t, sa_t, sb_t)])


def _prep_weights(p):
    pad_c = lambda m, n: jnp.pad(m, ((0, 0), (0, n - m.shape[1])))
    pad_r = lambda m, n: jnp.pad(m, ((0, n - m.shape[0]), (0, 0)))
    w = {}
    w["mu"] = p["rwkv_mu"][0]
    w["w_rkv"] = p["rwkv_w_rkv"][0].astype(BF16)
    w["w1"] = pad_c(p["rwkv_w1"][0], LANES).astype(BF16)
    w["w2"] = pad_r(p["rwkv_w2"][0], LANES).astype(BF16)
    w["a1"] = pad_c(p["rwkv_a1"][0], LANES).astype(BF16)
    w["a2"] = pad_r(p["rwkv_a2"][0], LANES).astype(BF16)
    w["g1"] = pad_c(p["rwkv_g1"][0], 2 * LANES).astype(BF16)
    w["g2"] = pad_r(p["rwkv_g2"][0], 2 * LANES).astype(BF16)
    w["vec"] = jnp.stack([p["rwkv_w0"][0], p["rwkv_a0"][0], p["rwkv_k_k"][0], p["rwkv_k_a"][0]])
    w["post_vec"] = jnp.stack([p["rwkv_r_k"][0].reshape(D), p["rwkv_lnx_g"][0], p["rwkv_lnx_b"][0],
                               p["ln_g"][0, 0], p["ln_b"][0, 0]])
    w["rwkv_w_o"] = p["rwkv_w_o"][0].astype(BF16)
    wq = p["w_q"][0].reshape(D, 3, KV_HEADS, REP, HD).transpose(0, 1, 3, 2, 4).reshape(D, 3 * D)
    w["w_q"] = wq.astype(BF16)
    w["w_kv"] = p["w_kv"].astype(BF16)
    wo = p["w_o_attn"][0].reshape(KV_HEADS, REP, HD, D).transpose(1, 0, 2, 3).reshape(D, D)
    w["w_o_attn"] = wo.astype(BF16)
    w["router_w_t"] = p["router_w"].T.astype(BF16)
    w["router_b"] = p["router_b"].reshape(N_EXPERTS, 1)
    w["moe_w_in"] = p["moe_w_in"]
    w["moe_w_out"] = p["moe_w_out"]
    w["ln"] = [[jnp.stack([p["ln_g"][l, i], p["ln_b"][l, i]]) for i in range(2)] for l in range(DEPTH)]
    return w


def _modulations(c_prompt, c_sample, p):
    nb = c_prompt.shape[0]
    c = jnp.concatenate([c_prompt, c_sample], axis=0)
    pad = (-c.shape[0]) % 8
    c = jnp.pad(c, ((0, pad), (0, 0)))
    m3 = _ada_linear(c, p["ada_w"].reshape(2 * DEPTH, D, 3 * D), p["ada_b"].reshape(2 * DEPTH, 3 * D))
    m2 = _ada_linear(c, p["kv_ada_w"][None], p["kv_ada_b"][None])[0]
    n_all = nb + c_sample.shape[0]

    def split(m, parts, lo, hi):
        return [m[lo:hi, i * D:(i + 1) * D] for i in range(parts)]

    out = {}
    for name, lo, hi in (("prompt", 0, nb), ("sample", nb, n_all)):
        out[name] = {"ada": [[split(m3[2 * l + i], 3, lo, hi) for i in range(2)] for l in range(DEPTH)],
                     "kv": split(m2, 2, lo, hi)}
    return out


def _trunk_prompt(x, mods, w):
    bsz, t, _ = x.shape
    per_b = lambda m: m[:, None, :]
    sh, sc, gt = (per_b(m) for m in mods["ada"][0][0])
    zeros = jnp.zeros((bsz, 1, D), F32)
    r, lw, k, v, a, b, g, last = _rwkv_pre(x, sc, sh, zeros, w, seq_mode=True, tm=256)
    y, zf = _wkv_chunked(r, lw, k, v, a, b)
    x = _rwkv_post(y, r, k, v, g, x, gt, w["post_vec"], w["rwkv_w_o"], pairs=True, tm=256)
    zf = zf.reshape(bsz, PAIRS, 2, HD, 2, HD)
    wkv = jnp.stack([zf[:, :, 0, :, 0, :], zf[:, :, 1, :, 1, :]], axis=2)
    wkv = wkv.reshape(bsz, HEADS, HD, HD).transpose(0, 1, 3, 2)
    sh2, sc2, gt2 = (per_b(m) for m in mods["ada"][0][1])
    x = _moe_sorted(x, sc2, sh2, gt2, w["ln"][0][1], w, 0)

    sh, sc, gt = (per_b(m) for m in mods["ada"][1][0])
    ksh, ksc = (per_b(m) for m in mods["kv"])
    rot = _rot_tables(jnp.arange(t, dtype=I32))
    dils = tuple(dil for _, dil in DILATED_GROUPS)
    k_new, v_new, qs, kbs, vbs = _attn_pre(x, sc, sh, ksc, ksh, rot, w["w_q"], w["w_kv"], per_b_mod=True,
                                           tm=256, dils=dils)
    os, lses = zip(*[_attn_band(qs[g], kbs[g], vbs[g], dil) for g, dil in enumerate(dils)])
    x = _attn_post(os, lses, x, gt, w["ln"][1][0], w["w_o_attn"], per_b_mod=True, tm=256, dils=dils)
    sh2, sc2, gt2 = (per_b(m) for m in mods["ada"][1][1])
    x = _moe_sorted(x, sc2, sh2, gt2, w["ln"][1][1], w, 1)
    return x, wkv[None], last.reshape(1, bsz, D), k_new, v_new


def _trunk_sample(x, mods, state_wkv, state_shift, cache_k, cache_v, w):
    bsz = x.shape[0]
    xs = x.reshape(1, bsz, D)
    row = lambda m: m[None]
    sh, sc, gt = (row(m) for m in mods["ada"][0][0])
    r, lw, k, v, a, b, g, hm = _rwkv_pre(xs, sc, sh, state_shift[0][None], w, seq_mode=False, tm=bsz)
    s_new, y = _wkv_step(state_wkv[0].transpose(1, 2, 3, 0), *(t[0].T for t in (r, lw, k, v, a, b)))
    s_new = s_new.transpose(3, 0, 1, 2)
    xs = _rwkv_post(y.T[None], r, k, v, g, xs, gt, w["post_vec"], w["rwkv_w_o"], pairs=False, tm=bsz)
    sh2, sc2, gt2 = (row(m) for m in mods["ada"][0][1])
    xs = _moe_dense(xs, sc2, sh2, gt2, w["ln"][0][1], w, 0)

    sh, sc, gt = (row(m) for m in mods["ada"][1][0])
    ksh, ksc = (row(m) for m in mods["kv"])
    rot = _rot_tables(jnp.full((bsz,), PAST_LEN, I32))
    k_new, v_new, q = _attn_pre(xs, sc, sh, ksc, ksh, rot, w["w_q"], w["w_kv"], per_b_mod=False, tm=bsz)
    o, lse = _attn_step(q[0], k_new[0], v_new[0], cache_k, cache_v)
    xs = _attn_post([o[i][None] for i in range(3)], [lse[i][None] for i in range(3)], xs, gt, w["ln"][1][0],
                    w["w_o_attn"], per_b_mod=False, tm=bsz)
    sh2, sc2, gt2 = (row(m) for m in mods["ada"][1][1])
    xs = _moe_dense(xs, sc2, sh2, gt2, w["ln"][1][1], w, 1)
    return (xs.reshape(bsz, 1, D), s_new[None], hm, k_new.reshape(bsz, 1, KV_HEADS, HD),
            v_new.reshape(bsz, 1, KV_HEADS, HD))


def kernel(x_prompt, x_sample, state_wkv, state_shift, cache_k, cache_v, c_prompt, c_sample, ada_w, ada_b, ln_g, ln_b, rwkv_mu, rwkv_w_rkv, rwkv_w0, rwkv_w1, rwkv_w2, rwkv_a0, rwkv_a1, rwkv_a2, rwkv_g1, rwkv_g2, rwkv_k_k, rwkv_k_a, rwkv_r_k, rwkv_lnx_g, rwkv_lnx_b, rwkv_w_o, w_q, w_kv, kv_ada_w, kv_ada_b, w_o_attn, router_w, router_b, moe_w_in, moe_w_out):
    p = {"ada_w": ada_w, "ada_b": ada_b, "ln_g": ln_g, "ln_b": ln_b, "rwkv_mu": rwkv_mu, "rwkv_w_rkv": rwkv_w_rkv,
         "rwkv_w0": rwkv_w0, "rwkv_w1": rwkv_w1, "rwkv_w2": rwkv_w2, "rwkv_a0": rwkv_a0, "rwkv_a1": rwkv_a1,
         "rwkv_a2": rwkv_a2, "rwkv_g1": rwkv_g1, "rwkv_g2": rwkv_g2, "rwkv_k_k": rwkv_k_k, "rwkv_k_a": rwkv_k_a,
         "rwkv_r_k": rwkv_r_k, "rwkv_lnx_g": rwkv_lnx_g, "rwkv_lnx_b": rwkv_lnx_b, "rwkv_w_o": rwkv_w_o,
         "w_q": w_q, "w_kv": w_kv, "kv_ada_w": kv_ada_w, "kv_ada_b": kv_ada_b, "w_o_attn": w_o_attn,
         "router_w": router_w, "router_b": router_b, "moe_w_in": moe_w_in, "moe_w_out": moe_w_out}
    w = _prep_weights(p)
    mods = _modulations(c_prompt, c_sample, p)
    bp, tp, _ = x_prompt.shape
    y_p, wkv_p, shift_p, k_p, v_p = _trunk_prompt(x_prompt, mods["prompt"], w)
    keep = min(PAST_LEN, tp)
    k_p = k_p[:, tp - keep:].reshape(bp, keep, KV_HEADS, HD)
    v_p = v_p[:, tp - keep:].reshape(bp, keep, KV_HEADS, HD)
    y_s, wkv_s, shift_s, k_s, v_s = _trunk_sample(x_sample, mods["sample"], state_wkv, state_shift, cache_k,
                                                  cache_v, w)
    return (y_p, y_s, wkv_p, shift_p, k_p, v_p, wkv_s, shift_s, k_s, v_s)
```

```python
import functools
import math

import jax
import jax.numpy as jnp
from jax import lax
from jax.experimental import pallas as pl
from jax.experimental.pallas import tpu as pltpu

F32 = jnp.float32
BF16 = jnp.bfloat16
I32 = jnp.int32

D = 1024
HEADS = 16
HD = 64
LANES = 128
PAIRS = D // LANES
CHUNK = 64
WKV_GROUP = 16
N_EXPERTS = 32
EXPERTS_PER_GROUP = 8
N_EXPERT_GROUPS = 4
D_EXPERT = 512
MOE_BLK = 256
DMA_UNROLL = 8
KV_HEADS = 4
REP = 4
KVW = KV_HEADS * HD
DILATED_GROUPS = ((128, 1), (512, 4), (2048, 16))
ATT_WIN = 128
PAST_LEN = 2048
ROT_DIM = 16
ROPE_THETA = 500000.0
DEPTH = 2
DN_ALPHA = (2 * DEPTH) ** 0.25
LN_EPS = 1e-5
GN_EPS = 64e-5
VMEM_LIMIT = 56 * 1024 * 1024


def _cp(sem):
    return pltpu.CompilerParams(dimension_semantics=sem, vmem_limit_bytes=VMEM_LIMIT)


def _bdot(a, b):
    return jnp.dot(a.astype(BF16), b.astype(BF16), preferred_element_type=F32)


def _bdot_nt(a, b):
    return lax.dot_general(a.astype(BF16), b.astype(BF16), (((1,), (1,)), ((), ())),
                           preferred_element_type=F32)


def _bdot_tn(a, b):
    return lax.dot_general(a.astype(BF16), b.astype(BF16), (((0,), (0,)), ((), ())),
                           preferred_element_type=F32)


def _split_dot(x, m):
    hi = x.astype(BF16)
    lo = (x - hi.astype(F32)).astype(BF16)
    return jnp.dot(jnp.concatenate([hi, lo], axis=1), jnp.concatenate([m, m], axis=0),
                   preferred_element_type=F32)


def _head_ones():
    r = lax.broadcasted_iota(I32, (LANES, LANES), 0) // HD
    c = lax.broadcasted_iota(I32, (LANES, LANES), 1) // HD
    return (r == c).astype(BF16)


def _segsum(x, ones_bd):
    cols = [_split_dot(x[:, c * LANES:(c + 1) * LANES], ones_bd) for c in range(x.shape[1] // LANES)]
    return cols[0] if len(cols) == 1 else jnp.concatenate(cols, axis=1)


def _layer_norm(x, g, b):
    mu = jnp.mean(x, axis=-1, keepdims=True)
    xc = x - mu
    var = jnp.mean(xc * xc, axis=-1, keepdims=True)
    return xc * lax.rsqrt(var + LN_EPS) * g + b


def _sigmoid(x):
    return 1.0 / (1.0 + jnp.exp(-x))


def _to_pairs(ref, val):
    for p in range(PAIRS):
        ref[p] = val[:, p * LANES:(p + 1) * LANES]


def _from_pairs(ref):
    return jnp.concatenate([ref[p] for p in range(PAIRS)], axis=1)


def _ada_kernel(c_ref, w_ref, b_ref, o_ref):
    c = c_ref[...]
    o_ref[...] = _bdot(c * _sigmoid(c), w_ref[...]) + b_ref[...]


def _ada_linear(c, w, b, tn=512):
    s, _, n = w.shape
    m = c.shape[0]
    return pl.pallas_call(
        _ada_kernel,
        grid=(s, n // tn),
        in_specs=[pl.BlockSpec((m, D), lambda i, j: (0, 0)),
                  pl.BlockSpec((None, D, tn), lambda i, j: (i, 0, j)),
                  pl.BlockSpec((None, 1, tn), lambda i, j: (i, 0, j))],
        out_specs=pl.BlockSpec((None, m, tn), lambda i, j: (i, 0, j)),
        out_shape=jax.ShapeDtypeStruct((s, m, n), F32),
        compiler_params=_cp(("arbitrary", "arbitrary")),
        name="ada_linear",
    )(c, w, b.reshape(s, 1, n))


def _rwkv_pre_kernel(x_ref, sc_ref, sh_ref, prev_ref, mu_ref, wrkv_ref, w1_ref, w2_ref, a1_ref, a2_ref,
                     g1_ref, g2_ref, vec_ref,
                     r_ref, lw_ref, k_ref, v_ref, a_ref, b_ref, g_ref, hm_ref, carry_ref, *, seq_mode, tm):
    x = x_ref[...]
    hm = x * (1.0 + sc_ref[...]) + sh_ref[...]
    if seq_mode:
        @pl.when(pl.program_id(1) == 0)
        def _():
            carry_ref[...] = prev_ref[...]
        row = lax.broadcasted_iota(I32, hm.shape, 0)
        hprev = jnp.where(row == 0, carry_ref[...], pltpu.roll(hm, 1, axis=0))
        carry_ref[...] = hm[tm - 1:tm, :]
        hm_ref[...] = hm[tm - 1:tm, :]
    else:
        hprev = prev_ref[...]
        hm_ref[...] = hm
    xx = hprev - hm

    def mix(i):
        return hm + xx * mu_ref[i:i + 1, :]

    w0, a0, k_k, k_a = (vec_ref[i:i + 1, :] for i in range(4))
    r = _bdot(mix(0), wrkv_ref[0])
    k = _bdot(mix(2), wrkv_ref[1])
    v = _bdot(mix(3), wrkv_ref[2])
    wl = w0 + _bdot(jnp.tanh(_bdot(mix(1), w1_ref[...])), w2_ref[...])
    lw = -math.exp(-0.5) * _sigmoid(wl)
    a_lr = _sigmoid(a0 + _bdot(_bdot(mix(4), a1_ref[...]), a2_ref[...]))
    g = _bdot(_sigmoid(_bdot(mix(5), g1_ref[...])), g2_ref[...])
    kk = k * k_k
    kk = kk * jnp.minimum(lax.rsqrt(_segsum(kk * kk, _head_ones())), 1e12)
    kmod = k * (1.0 + (a_lr - 1.0) * k_a)
    outs = ((r_ref, r), (lw_ref, lw), (k_ref, kmod), (v_ref, v), (a_ref, -kk), (b_ref, kk * a_lr), (g_ref, g))
    for ref, val in outs:
        if seq_mode:
            _to_pairs(ref, val)
        else:
            ref[...] = val


def _rwkv_pre(x, sc, sh, prev, w, *, seq_mode, tm):
    bsz, t, _ = x.shape
    grid = (bsz, t // tm)
    row = pl.BlockSpec((None, tm, D), lambda b, i: (b, i, 0))
    per_b = pl.BlockSpec((None, 1, D), lambda b, i: (b, 0, 0))
    mod = per_b if seq_mode else row

    def const(shape):
        return pl.BlockSpec(shape, lambda b, i: (0,) * len(shape))

    if seq_mode:
        out_big = pl.BlockSpec((None, PAIRS, tm, LANES), lambda b, i: (b, 0, i, 0))
        big_shape = jax.ShapeDtypeStruct((bsz, PAIRS, t, LANES), F32)
        hm_spec, hm_shape = per_b, jax.ShapeDtypeStruct((bsz, 1, D), F32)
    else:
        out_big, big_shape = row, jax.ShapeDtypeStruct((bsz, t, D), F32)
        hm_spec, hm_shape = row, jax.ShapeDtypeStruct((bsz, t, D), F32)
    return pl.pallas_call(
        functools.partial(_rwkv_pre_kernel, seq_mode=seq_mode, tm=tm),
        grid=grid,
        in_specs=[row, mod, mod, mod, const((6, D)), const((3, D, D)), const((D, LANES)), const((LANES, D)),
                  const((D, LANES)), const((LANES, D)), const((D, 2 * LANES)), const((2 * LANES, D)),
                  const((4, D))],
        out_specs=[out_big] * 7 + [hm_spec],
        out_shape=[big_shape] * 7 + [hm_shape],
        scratch_shapes=[pltpu.VMEM((1, D), F32)],
        compiler_params=_cp(("arbitrary", "arbitrary")),
        name="rwkv_pre",
    )(x, sc, sh, prev, w["mu"], w["w_rkv"], w["w1"], w["w2"], w["a1"], w["a2"], w["g1"], w["g2"], w["vec"])


def _wkv_chunk_kernel(r_ref, lw_ref, k_ref, v_ref, a_ref, b_ref, y_ref, zf_ref, z_ref, *, n_pairs, group):
    c = pl.program_id(0)

    @pl.when(c == 0)
    def _():
        z_ref[...] = jnp.zeros_like(z_ref)

    L = CHUNK
    row_l = lax.broadcasted_iota(I32, (L, L), 0)
    col_l = lax.broadcasted_iota(I32, (L, L), 1)
    tri_incl = (row_l >= col_l).astype(BF16)
    lane = lax.broadcasted_iota(I32, (L, LANES), 1)
    head0 = lane < HD
    row = lax.broadcasted_iota(I32, (LANES, LANES), 0)
    col = lax.broadcasted_iota(I32, (LANES, LANES), 1)
    strict = row > col
    incl = row >= col
    eye = (row == col).astype(F32)

    def expand(x):
        return jnp.concatenate([jnp.where(head0, x, 0.0), jnp.where(head0, 0.0, x)], axis=0)

    def group_body(gi, carry):
        ids = [gi * group + j for j in range(group)]
        bp = [(i // PAIRS, i % PAIRS) for i in ids]
        G = range(group)
        r = [r_ref[b_i, p_i] for b_i, p_i in bp]
        lw = [lw_ref[b_i, p_i] for b_i, p_i in bp]
        k = [k_ref[b_i, p_i] for b_i, p_i in bp]
        v = [v_ref[b_i, p_i] for b_i, p_i in bp]
        a = [a_ref[b_i, p_i] for b_i, p_i in bp]
        b = [b_ref[b_i, p_i] for b_i, p_i in bp]
        z0 = [z_ref[i] for i in ids]
        cum = [_split_dot_left(tri_incl, lw[j]) for j in G]
        cum_l = [cum[j][L - 1:L, :] for j in G]
        inv = [jnp.exp(-cum[j]) for j in G]
        tail = [jnp.exp(cum_l[j] - cum[j]) for j in G]
        a_e = [expand(a[j] * jnp.exp(cum[j] - lw[j])) for j in G]
        r_e = [expand(r[j] * jnp.exp(cum[j])) for j in G]
        b_e = [expand(b[j] * inv[j]) for j in G]
        k_e = [expand(k[j] * inv[j]) for j in G]
        bd_e = [expand(b[j] * tail[j]) for j in G]
        kd_e = [expand(k[j] * tail[j]) for j in G]
        v_e = [expand(v[j]) for j in G]
        gm = [_bdot_nt(jnp.concatenate([a_e[j], r_e[j]], axis=0), jnp.concatenate([b_e[j], k_e[j]], axis=0))
              for j in G]
        m_ab = [jnp.where(strict, gm[j][:LANES, :LANES], 0.0) for j in G]
        m_ak = [jnp.where(strict, gm[j][:LANES, LANES:], 0.0) for j in G]
        m_rb = [jnp.where(incl, gm[j][LANES:, :LANES], 0.0) for j in G]
        m_rk = [jnp.where(incl, gm[j][LANES:, LANES:], 0.0) for j in G]
        tinv = [eye + m_ab[j] for j in G]
        pw = [_bdot(m_ab[j], m_ab[j]) for j in G]
        for _ in range(4):
            both = [_bdot(pw[j], jnp.concatenate([pw[j], tinv[j]], axis=1)) for j in G]
            pw = [both[j][:, :LANES] for j in G]
            tinv = [tinv[j] + both[j][:, LANES:] for j in G]
        tinv = [tinv[j] + _bdot(pw[j], tinv[j]) for j in G]
        rhs = [_bdot(a_e[j], z0[j]) + _bdot(m_ak[j], v_e[j]) for j in G]
        u = [_bdot(tinv[j], rhs[j]) for j in G]
        zuv = [jnp.concatenate([z0[j], u[j], v_e[j]], axis=0) for j in G]
        decay = [eye * jnp.exp(cum_l[j]) for j in G]
        for j in G:
            z_ref[ids[j]] = _bdot_tn(jnp.concatenate([decay[j], bd_e[j], kd_e[j]], axis=0), zuv[j])
            y_e = _bdot(jnp.concatenate([r_e[j], m_rb[j], m_rk[j]], axis=1), zuv[j])
            y_ref[bp[j][0], bp[j][1]] = y_e[:L] + y_e[L:]
        return carry

    lax.fori_loop(0, n_pairs // group, group_body, 0)

    @pl.when(c == pl.num_programs(0) - 1)
    def _():
        zf_ref[...] = z_ref[...]


def _split_dot_left(m, x):
    hi = x.astype(BF16)
    lo = (x - hi.astype(F32)).astype(BF16)
    return (jnp.dot(m, hi, preferred_element_type=F32) + jnp.dot(m, lo, preferred_element_type=F32))


def _wkv_chunked(r, lw, k, v, a, b):
    bsz, _, t, _ = r.shape
    n_pairs = bsz * PAIRS
    blk = pl.BlockSpec((bsz, PAIRS, CHUNK, LANES), lambda c: (0, 0, c, 0))
    return pl.pallas_call(
        functools.partial(_wkv_chunk_kernel, n_pairs=n_pairs, group=WKV_GROUP),
        grid=(t // CHUNK,),
        in_specs=[blk] * 6,
        out_specs=[blk, pl.BlockSpec((n_pairs, LANES, LANES), lambda c: (0, 0, 0))],
        out_shape=[jax.ShapeDtypeStruct(r.shape, F32), jax.ShapeDtypeStruct((n_pairs, LANES, LANES), F32)],
        scratch_shapes=[pltpu.VMEM((n_pairs, LANES, LANES), F32)],
        compiler_params=_cp(("arbitrary",)),
        name="wkv_chunked",
    )(r, lw, k, v, a, b)


def _wkv_step_kernel(s_ref, r_ref, lw_ref, k_ref, v_ref, a_ref, b_ref, so_ref, y_ref):
    w = jnp.exp(lw_ref[...])
    a, b, k, r = a_ref[...], b_ref[...], k_ref[...], r_ref[...]

    def body(i, c):
        s = s_ref[i]
        sa = jnp.sum(s * a, axis=0, keepdims=True)
        s_new = s * w + sa * b + v_ref[pl.ds(i, 1), :] * k
        so_ref[i] = s_new
        y_ref[pl.ds(i, 1), :] = jnp.sum(s_new * r, axis=0, keepdims=True)
        return c

    lax.fori_loop(0, HD, body, 0, unroll=4)


def _wkv_step(state_t, r, lw, k, v, a, b):
    bsz = state_t.shape[-1]
    big = pl.BlockSpec((None, HD, HD, bsz), lambda h: (h, 0, 0, 0))
    vec = pl.BlockSpec((HD, bsz), lambda h: (h, 0))
    return pl.pallas_call(
        _wkv_step_kernel,
        grid=(HEADS,),
        in_specs=[big] + [vec] * 6,
        out_specs=[big, vec],
        out_shape=[jax.ShapeDtypeStruct(state_t.shape, F32), jax.ShapeDtypeStruct((D, bsz), F32)],
        compiler_params=_cp(("arbitrary",)),
        name="wkv_step",
    )(state_t, r, lw, k, v, a, b)


def _rwkv_post_kernel(y_ref, r_ref, k_ref, v_ref, g_ref, x_ref, gt_ref, vec_ref, wo_ref, o_ref, *, pairs):
    load = _from_pairs if pairs else (lambda ref: ref[...])
    y, r, k, v, g = (load(ref) for ref in (y_ref, r_ref, k_ref, v_ref, g_ref))
    r_k, lnx_g, lnx_b, ln_g, ln_b = (vec_ref[i:i + 1, :] for i in range(5))
    ones_bd = _head_ones()
    ym = _segsum(y, ones_bd) * (1.0 / HD)
    yc = y - ym
    yv = _segsum(yc * yc, ones_bd) * (1.0 / HD)
    yn = yc * lax.rsqrt(yv + GN_EPS) * lnx_g + lnx_b
    bonus = _segsum(r * k * r_k, ones_bd) * v
    mix = _bdot((yn + bonus) * g, wo_ref[...])
    o_ref[...] = _layer_norm(DN_ALPHA * x_ref[...] + gt_ref[...] * mix, ln_g, ln_b)


def _rwkv_post(y, r, k, v, g, x, gt, vec, w_o, *, pairs, tm):
    bsz, t, _ = x.shape
    row = pl.BlockSpec((None, tm, D), lambda b, i: (b, i, 0))
    big = pl.BlockSpec((None, PAIRS, tm, LANES), lambda b, i: (b, 0, i, 0)) if pairs else row
    mod = pl.BlockSpec((None, 1, D), lambda b, i: (b, 0, 0)) if pairs else row
    return pl.pallas_call(
        functools.partial(_rwkv_post_kernel, pairs=pairs),
        grid=(bsz, t // tm),
        in_specs=[big] * 5 + [row, mod, pl.BlockSpec((5, D), lambda b, i: (0, 0)),
                              pl.BlockSpec((D, D), lambda b, i: (0, 0))],
        out_specs=row,
        out_shape=jax.ShapeDtypeStruct(x.shape, F32),
        compiler_params=_cp(("arbitrary", "arbitrary")),
        name="rwkv_post",
    )(y, r, k, v, g, x, gt, vec, w_o)


def _top2(v):
    io = lax.broadcasted_iota(I32, v.shape, 0)
    m1 = jnp.max(v, axis=0, keepdims=True)
    i1 = jnp.min(jnp.where(v == m1, io, EXPERTS_PER_GROUP), axis=0, keepdims=True)
    v2 = jnp.where(io == i1, -jnp.inf, v)
    m2 = jnp.max(v2, axis=0, keepdims=True)
    i2 = jnp.min(jnp.where(v2 == m2, io, EXPERTS_PER_GROUP), axis=0, keepdims=True)
    return m1 + m2, i1, i2


def _moe_route_kernel(x_ref, sc_ref, sh_ref, rw_ref, rb_ref, h_ref, idx_ref, gate_ref, gmat_ref, cnt_ref,
                      *, tm):
    h = x_ref[...] * (1.0 + sc_ref[...]) + sh_ref[...]
    h_ref[...] = h
    aff = _sigmoid(_bdot_nt(rw_ref[...], h))
    sel = aff + rb_ref[...]
    best = gi = i1 = i2 = None
    for g in range(N_EXPERT_GROUPS):
        sc, j1, j2 = _top2(sel[g * EXPERTS_PER_GROUP:(g + 1) * EXPERTS_PER_GROUP, :])
        if g == 0:
            best, gi, i1, i2 = sc, jnp.zeros_like(j1), j1, j2
        else:
            upd = sc > best
            best = jnp.where(upd, sc, best)
            gi = jnp.where(upd, g, gi)
            i1 = jnp.where(upd, j1, i1)
            i2 = jnp.where(upd, j2, i2)
    e0 = gi * EXPERTS_PER_GROUP + i1
    e1 = gi * EXPERTS_PER_GROUP + i2
    io = lax.broadcasted_iota(I32, (N_EXPERTS, tm), 0)
    oh0 = io == e0
    oh1 = io == e1
    a0 = jnp.sum(jnp.where(oh0, aff, 0.0), axis=0, keepdims=True)
    a1 = jnp.sum(jnp.where(oh1, aff, 0.0), axis=0, keepdims=True)
    den = a0 + a1
    g0 = a0 / den
    g1 = a1 / den
    oh = jnp.where(oh0 | oh1, 1.0, 0.0)
    tr = lax.broadcasted_iota(I32, (tm, tm), 0)
    tc = lax.broadcasted_iota(I32, (tm, tm), 1)
    before = jnp.dot(oh.astype(BF16), (tr < tc).astype(BF16), preferred_element_type=F32)
    rank0 = jnp.sum(jnp.where(oh0, before, 0.0), axis=0, keepdims=True)
    rank1 = jnp.sum(jnp.where(oh1, before, 0.0), axis=0, keepdims=True)
    zi = jnp.zeros((4, tm), I32)
    idx_ref[...] = jnp.concatenate([e0, e1, rank0.astype(I32), rank1.astype(I32), zi], axis=0)
    gate_ref[...] = jnp.concatenate([g0, g1, jnp.zeros((6, tm), F32)], axis=0)
    gmat_ref[...] = jnp.where(oh0, g0, 0.0) + jnp.where(oh1, g1, 0.0)
    cnt_ref[...] = jnp.broadcast_to(jnp.sum(oh, axis=1, keepdims=True), cnt_ref.shape)


def _moe_route(x, sc, sh, router_w_t, router_b, *, per_b_mod, rows_out, tm):
    bsz, t, _ = x.shape
    n = bsz * t
    nt = t // tm
    row = pl.BlockSpec((None, tm, D), lambda b, i: (b, i, 0))
    mod = pl.BlockSpec((None, 1, D), lambda b, i: (b, 0, 0)) if per_b_mod else row
    if rows_out:
        h_spec = pl.BlockSpec((tm, D), lambda b, i: (b * nt + i, 0))
        h_shape = jax.ShapeDtypeStruct((n, D), F32)
    else:
        h_spec, h_shape = row, jax.ShapeDtypeStruct(x.shape, F32)
    tok = lambda rows: pl.BlockSpec((rows, tm), lambda b, i: (0, b * nt + i))
    return pl.pallas_call(
        functools.partial(_moe_route_kernel, tm=tm),
        grid=(bsz, nt),
        in_specs=[row, mod, mod, pl.BlockSpec((N_EXPERTS, D), lambda b, i: (0, 0)),
                  pl.BlockSpec((N_EXPERTS, 1), lambda b, i: (0, 0))],
        out_specs=[h_spec, tok(8), tok(8), tok(N_EXPERTS),
                   pl.BlockSpec((None, N_EXPERTS, LANES), lambda b, i: (b * nt + i, 0, 0))],
        out_shape=[h_shape, jax.ShapeDtypeStruct((8, n), I32), jax.ShapeDtypeStruct((8, n), F32),
                   jax.ShapeDtypeStruct((N_EXPERTS, n), F32),
                   jax.ShapeDtypeStruct((bsz * nt, N_EXPERTS, LANES), F32)],
        compiler_params=_cp(("arbitrary", "arbitrary")),
        name="moe_route",
    )(x, sc, sh, router_w_t, router_b)


def _row_copy(src, src_row, dst, dst_row, sem):
    return pltpu.make_async_copy(src.at[pl.ds(src_row, 1), :], dst.at[pl.ds(dst_row, 1), :], sem)


def _wait_rows(src, dst, rows, sem):
    pltpu.make_async_copy(src.at[pl.ds(0, rows), :], dst.at[pl.ds(0, rows), :], sem).wait()


def _moe_dispatch_kernel(c8_ref, lo_ref, rs_ref, zs_ref, h_ref, idx_ref, tab_ref, xs_ref, slot_ref,
                         buf, zbuf, sem, *, tm):
    step = pl.program_id(0)
    last = pl.num_programs(0) - 1
    s = step % 2

    def runs(tile, sl, start):
        for e in range(N_EXPERTS):
            j = tile * N_EXPERTS + e

            @pl.when(c8_ref[j] > 0)
            def _():
                size = pl.multiple_of(c8_ref[j], 8)
                cp = pltpu.make_async_copy(buf.at[sl, pl.ds(pl.multiple_of(lo_ref[j], 8), size), :],
                                           xs_ref.at[pl.ds(pl.multiple_of(rs_ref[j], 8), size), :], sem.at[sl])
                if start:
                    cp.start()
                else:
                    cp.wait()

    @pl.when(step == 0)
    def _():
        zbuf[...] = jnp.zeros_like(zbuf)

        def zero_block(first_slot):
            return pltpu.make_async_copy(zbuf, xs_ref.at[pl.ds(pl.multiple_of(first_slot, 8), MOE_BLK), :],
                                         sem.at[2])

        fresh = [zs_ref[e] != zs_ref[e - 1] for e in range(1, N_EXPERTS)]
        zero_block(zs_ref[0]).start()
        for e, f in enumerate(fresh, start=1):
            pl.when(f)(lambda e=e: zero_block(zs_ref[e]).start())
        zero_block(zs_ref[0]).wait()
        for e, f in enumerate(fresh, start=1):
            pl.when(f)(lambda e=e: zero_block(zs_ref[e]).wait())

        used, n_blocks = zs_ref[N_EXPERTS], xs_ref.shape[0] // MOE_BLK

        def tail_start(j, c):
            zero_block(j * MOE_BLK).start()
            return c

        def tail_wait(j, c):
            zero_block(j * MOE_BLK).wait()
            return c

        lax.fori_loop(used, n_blocks, tail_start, 0)
        lax.fori_loop(used, n_blocks, tail_wait, 0)

    @pl.when(step >= 2)
    def _():
        runs(step - 2, s, start=False)

    e0, e1, l0, l1 = (idx_ref[r:r + 1, :] for r in range(4))
    expert = lax.broadcasted_iota(I32, (N_EXPERTS, tm), 0)

    def lookup(col, e):
        return jnp.sum(jnp.where(expert == e, col, 0.0), axis=0, keepdims=True).astype(I32)

    lo_col = tab_ref[:, 0:1].astype(F32)
    rs_col = tab_ref[:, 1:2].astype(F32)
    pos0 = lookup(lo_col, e0) + l0
    pos1 = lookup(lo_col, e1) + l1
    slot_ref[...] = jnp.concatenate([lookup(rs_col, e0) + l0, lookup(rs_col, e1) + l1,
                                     jnp.zeros((6, tm), I32)], axis=0)
    row = lax.broadcasted_iota(I32, (buf.shape[1], tm), 0)
    perm = ((row == pos0) | (row == pos1)).astype(BF16)
    buf[s] = jnp.dot(perm, h_ref[...].astype(BF16), preferred_element_type=F32)
    runs(step, s, start=True)

    @pl.when(step == last)
    def _():
        @pl.when(step >= 1)
        def _():
            runs(step - 1, 1 - s, start=False)
        runs(step, s, start=False)


def _moe_dispatch(h_rows, idx, c8, loff, run_start, zero_start, n_slots, tm):
    n = h_rows.shape[0]
    tiles = n // tm
    buf_rows = 2 * tm + N_EXPERTS * 8
    tab = jnp.pad(jnp.stack([loff, run_start], axis=-1), ((0, 0), (0, 0), (0, LANES - 2)))
    flat = lambda a: a.reshape(-1).astype(I32)
    return pl.pallas_call(
        functools.partial(_moe_dispatch_kernel, tm=tm),
        grid_spec=pltpu.PrefetchScalarGridSpec(
            num_scalar_prefetch=4, grid=(tiles,),
            in_specs=[pl.BlockSpec((tm, D), lambda i, *_: (i, 0)),
                      pl.BlockSpec((8, tm), lambda i, *_: (0, i)),
                      pl.BlockSpec((None, N_EXPERTS, LANES), lambda i, *_: (i, 0, 0))],
            out_specs=[pl.BlockSpec(memory_space=pl.ANY), pl.BlockSpec((8, tm), lambda i, *_: (0, i))],
            scratch_shapes=[pltpu.VMEM((2, buf_rows, D), F32), pltpu.VMEM((MOE_BLK, D), F32),
                            pltpu.SemaphoreType.DMA((3,))]),
        out_shape=[jax.ShapeDtypeStruct((n_slots, D), F32), jax.ShapeDtypeStruct((8, n), I32)],
        compiler_params=_cp(("arbitrary",)),
        name="moe_dispatch",
    )(flat(c8), flat(loff), flat(run_start), zero_start, h_rows, idx, tab)


def _moe_expert_kernel(be_ref, used_ref, nxt_ref, xs_ref, win_hbm, wout_hbm, ys_ref,
                       win_f, wout_f, win_bf, wout_bf, run_ref, sem, *, layer):
    i = pl.program_id(0)
    used = used_ref[0]

    def weight_copies(e, s):
        return (pltpu.make_async_copy(win_hbm.at[layer, e], win_f.at[s], sem.at[s, 0]),
                pltpu.make_async_copy(wout_hbm.at[layer, e], wout_f.at[s], sem.at[s, 1]))

    @pl.when(i == 0)
    def _():
        run_ref[0] = 0
        for cp in weight_copies(be_ref[0], 0):
            cp.start()

    fresh = (i == 0) | (be_ref[i] != be_ref[jnp.maximum(i - 1, 0)])

    @pl.when(fresh & (i < used))
    def _():
        run = run_ref[0] + jnp.where(i > 0, 1, 0)
        run_ref[0] = run
        s = run % 2
        for cp in weight_copies(be_ref[i], s):
            cp.wait()
        win_bf[...] = win_f[s].astype(BF16)
        wout_bf[...] = wout_f[s].astype(BF16)

        @pl.when(nxt_ref[i] >= 0)
        def _():
            for cp in weight_copies(nxt_ref[i], 1 - s):
                cp.start()

    @pl.when(i < used)
    def _():
        hmid = jnp.dot(xs_ref[...].astype(BF16), win_bf[...], preferred_element_type=F32)
        gt = hmid[:, :D_EXPERT]
        up = hmid[:, D_EXPERT:]
        act = gt * _sigmoid(gt) * up
        ys_ref[...] = jnp.dot(act.astype(BF16), wout_bf[...], preferred_element_type=F32)

    @pl.when(i >= used_ref[0])
    def _():
        ys_ref[...] = jnp.zeros_like(ys_ref)


def _moe_experts_sorted(xs, blk_exp, n_used, w_in, w_out, layer):
    n_slots = xs.shape[0]
    n_blocks = n_slots // MOE_BLK
    blk = jnp.arange(n_blocks, dtype=I32)
    later = (blk[None, :] > blk[:, None]) & (blk_exp[None, :] != blk_exp[:, None]) & (blk[None, :] < n_used[0])
    first_later = jnp.min(jnp.where(later, blk[None, :], n_blocks), axis=1)
    nxt = jnp.where(first_later < n_blocks, blk_exp[jnp.minimum(first_later, n_blocks - 1)], -1).astype(I32)
    x_map = lambda i, be, u, nx: (jnp.minimum(i, u[0] - 1), 0)
    return pl.pallas_call(
        functools.partial(_moe_expert_kernel, layer=layer),
        grid_spec=pltpu.PrefetchScalarGridSpec(
            num_scalar_prefetch=3, grid=(n_blocks,),
            in_specs=[pl.BlockSpec((MOE_BLK, D), x_map),
                      pl.BlockSpec(memory_space=pl.ANY), pl.BlockSpec(memory_space=pl.ANY)],
            out_specs=pl.BlockSpec((MOE_BLK, D), lambda i, be, u, nx: (i, 0)),
            scratch_shapes=[pltpu.VMEM((2, D, 2 * D_EXPERT), F32), pltpu.VMEM((2, D_EXPERT, D), F32),
                            pltpu.VMEM((D, 2 * D_EXPERT), BF16), pltpu.VMEM((D_EXPERT, D), BF16),
                            pltpu.SMEM((1,), I32), pltpu.SemaphoreType.DMA((2, 2))]),
        out_shape=jax.ShapeDtypeStruct((n_slots, D), F32),
        compiler_params=_cp(("arbitrary",)),
        name="moe_experts_sorted",
    )(blk_exp, n_used, nxt, xs, w_in, w_out)


def _moe_combine_kernel(slot_ref, ys_ref, gate_ref, x_ref, gt_ref, vec_ref, o_ref, ya, yb, sem, *, tm, n):
    step = pl.program_id(0)
    slot = step % 2

    def gather(s, sl):
        def issue(i, c):
            t = s * tm + i
            _row_copy(ys_ref, slot_ref[t], ya.at[sl], i, sem.at[sl, 0]).start(priority=0)
            _row_copy(ys_ref, slot_ref[n + t], yb.at[sl], i, sem.at[sl, 1]).start(priority=1)
            return c
        lax.fori_loop(0, tm, issue, 0, unroll=DMA_UNROLL)

    @pl.when(step == 0)
    def _():
        gather(0, 0)

    @pl.when(step + 1 < pl.num_programs(0))
    def _():
        gather(step + 1, 1 - slot)

    _wait_rows(ys_ref, ya.at[slot], tm, sem.at[slot, 0])
    _wait_rows(ys_ref, yb.at[slot], tm, sem.at[slot, 1])
    ff = ya[slot] * gate_ref[:, 0:1] + yb[slot] * gate_ref[:, 1:2]
    o_ref[...] = _layer_norm(DN_ALPHA * x_ref[...] + gt_ref[...] * ff, vec_ref[0:1, :], vec_ref[1:2, :])


def _moe_combine(ys, slots, gates, x, gt, vec, tm):
    bsz, t, _ = x.shape
    nt = t // tm
    return pl.pallas_call(
        functools.partial(_moe_combine_kernel, tm=tm, n=bsz * t),
        grid_spec=pltpu.PrefetchScalarGridSpec(
            num_scalar_prefetch=1, grid=(bsz * nt,),
            in_specs=[pl.BlockSpec(memory_space=pl.ANY),
                      pl.BlockSpec((tm, 2), lambda i, sl: (i, 0)),
                      pl.BlockSpec((None, tm, D), lambda i, sl: (i // nt, i % nt, 0)),
                      pl.BlockSpec((None, 1, D), lambda i, sl: (i // nt, 0, 0)),
                      pl.BlockSpec((2, D), lambda i, sl: (0, 0))],
            out_specs=pl.BlockSpec((None, tm, D), lambda i, sl: (i // nt, i % nt, 0)),
            scratch_shapes=[pltpu.VMEM((2, tm, D), F32), pltpu.VMEM((2, tm, D), F32),
                            pltpu.SemaphoreType.DMA((2, 2))]),
        out_shape=jax.ShapeDtypeStruct(x.shape, F32),
        compiler_params=_cp(("arbitrary",)),
        name="moe_combine",
    )(slots, ys, gates, x, gt, vec)


def _moe_sorted(x, sc, sh, gt, vec, w, layer, tm=256):
    bsz, t, _ = x.shape
    n = bsz * t
    tiles = n // tm
    n_blocks = -(-(2 * n + tiles * N_EXPERTS * 7 + N_EXPERTS * (MOE_BLK - 1)) // MOE_BLK)
    h_rows, idx, gate, _, cnt = _moe_route(x, sc, sh, w["router_w_t"], w["router_b"], per_b_mod=True,
                                           rows_out=True, tm=tm)
    c8 = (cnt[:, :, 0].astype(I32) + 7) // 8 * 8
    padded = (jnp.sum(c8, axis=0) + MOE_BLK - 1) // MOE_BLK * MOE_BLK
    pad_end = jnp.cumsum(padded)
    run_start = (pad_end - padded)[None, :] + jnp.cumsum(c8, axis=0) - c8
    loff = jnp.cumsum(c8, axis=1) - c8
    blk_start = jnp.arange(n_blocks, dtype=I32) * MOE_BLK
    blk_exp = jnp.minimum(jnp.sum((pad_end[None, :] <= blk_start[:, None]).astype(I32), axis=1), N_EXPERTS - 1)
    n_used = (pad_end[-1:] // MOE_BLK).astype(I32)
    zero_start = jnp.concatenate([jnp.maximum(pad_end - MOE_BLK, 0), n_used])
    xs, slot_rows = _moe_dispatch(h_rows, idx, c8, loff, run_start, zero_start, n_blocks * MOE_BLK, tm)
    slots = jnp.concatenate([slot_rows[0], slot_rows[1]])
    ys = _moe_experts_sorted(xs, blk_exp, n_used, w["moe_w_in"], w["moe_w_out"], layer)
    return _moe_combine(ys, slots, gate[:2].T, x, gt, vec, tm)


def _moe_dense_kernel(h_ref, gm_ref, win_ref, wout_ref, x_ref, gt_ref, vec_ref, o_ref, acc_ref):
    e = pl.program_id(0)

    @pl.when(e == 0)
    def _():
        acc_ref[...] = jnp.zeros_like(acc_ref)

    lane = lax.broadcasted_iota(I32, gm_ref.shape, 1)
    gcol = jnp.sum(jnp.where(lane == e, gm_ref[...], 0.0), axis=1, keepdims=True)
    hmid = _bdot(h_ref[...], win_ref[...])
    gt = hmid[:, :D_EXPERT]
    up = hmid[:, D_EXPERT:]
    y = _bdot(gt * _sigmoid(gt) * up, wout_ref[...])
    acc_ref[...] = acc_ref[...] + jnp.where(gcol != 0.0, gcol * y, 0.0)

    @pl.when(e == N_EXPERTS - 1)
    def _():
        o_ref[...] = _layer_norm(DN_ALPHA * x_ref[...] + gt_ref[...] * acc_ref[...], vec_ref[0:1, :],
                                 vec_ref[1:2, :])


def _moe_dense(x, sc, sh, gt, vec, w, layer):
    _, m, _ = x.shape
    h, _, _, gmat, _ = _moe_route(x, sc, sh, w["router_w_t"], w["router_b"], per_b_mod=False, rows_out=False,
                                  tm=m)
    full = pl.BlockSpec((m, D), lambda e: (0, 0))
    out = pl.pallas_call(
        _moe_dense_kernel,
        grid=(N_EXPERTS,),
        in_specs=[full, pl.BlockSpec((m, N_EXPERTS), lambda e: (0, 0)),
                  pl.BlockSpec((None, None, D, 2 * D_EXPERT), lambda e: (layer, e, 0, 0)),
                  pl.BlockSpec((None, None, D_EXPERT, D), lambda e: (layer, e, 0, 0)),
                  full, full, pl.BlockSpec((2, D), lambda e: (0, 0))],
        out_specs=full,
        out_shape=jax.ShapeDtypeStruct((m, D), F32),
        scratch_shapes=[pltpu.VMEM((m, D), F32)],
        compiler_params=_cp(("arbitrary",)),
        name="moe_dense",
    )(h[0], gmat.T, w["moe_w_in"], w["moe_w_out"], x[0], gt[0], vec)
    return out[None]


def _rotary(x, cos_t, sa_t, sb_t):
    reps = x.shape[1] // LANES
    tile = lambda t: t if reps == 1 else jnp.concatenate([t] * reps, axis=1)
    n = x.shape[1]
    half = ROT_DIM // 2
    return x * tile(cos_t) + pltpu.roll(x, n - half, axis=1) * tile(sa_t) + pltpu.roll(x, half, axis=1) * tile(sb_t)


def _class_rows(c, rows, dil):
    return pl.ds(c, rows, stride=dil) if dil > 1 else pl.ds(0, rows)


def _class_major_perm(tm, dil, inverse=False):
    cm_row = lax.broadcasted_iota(I32, (tm, tm), 1 if inverse else 0)
    tok_row = lax.broadcasted_iota(I32, (tm, tm), 0 if inverse else 1)
    rows = tm // dil
    return ((cm_row // rows == tok_row % dil) & (cm_row % rows == tok_row // dil)).astype(BF16)


def _attn_pre_kernel(x_ref, sc_ref, sh_ref, ksc_ref, ksh_ref, rot_ref, wq_ref, wkv_ref, *refs, dils, tm):
    x = x_ref[...]
    cos_t, sa_t, sb_t = rot_ref[0], rot_ref[1], rot_ref[2]
    q = _rotary(_bdot(x * (1.0 + sc_ref[...]) + sh_ref[...], wq_ref[...]), cos_t, sa_t, sb_t) * HD ** -0.5
    kv = _bdot(x * (1.0 + ksc_ref[...]) + ksh_ref[...], wkv_ref[...])
    k = _rotary(kv[:, :KVW], cos_t, sa_t, sb_t)
    v = kv[:, KVW:]
    k_ref, v_ref = refs[0], refs[1]
    k_ref[...] = k
    v_ref[...] = v
    if dils is None:
        refs[2][...] = q
        return
    ng = len(dils)
    q_refs, kb_refs, vb_refs = refs[2:2 + ng], refs[2 + ng:2 + 2 * ng], refs[2 + 2 * ng:2 + 3 * ng]
    qb, kb, vb = q.astype(BF16), k.astype(BF16), v.astype(BF16)
    for g, dil in enumerate(dils):
        rows = tm // dil
        blk = jnp.concatenate([qb[:, g * D:(g + 1) * D], kb, vb], axis=1)
        if dil > 1:
            blk = jnp.dot(_class_major_perm(tm, dil), blk, preferred_element_type=F32).astype(BF16)
        for c in range(dil):
            rs = slice(c * rows, (c + 1) * rows)
            q_refs[g][c] = blk[rs, :D]
            kb_refs[g][c] = blk[rs, D:D + KVW]
            vb_refs[g][c] = blk[rs, D + KVW:]


def _attn_pre(x, sc, sh, ksc, ksh, rot, w_q, w_kv, *, per_b_mod, tm, dils=None):
    bsz, t, _ = x.shape
    row = lambda wd: pl.BlockSpec((None, tm, wd), lambda b, i: (b, i, 0))
    mod = pl.BlockSpec((None, 1, D), lambda b, i: (b, 0, 0)) if per_b_mod else row(D)
    nq = w_q.shape[1]
    out_specs = [row(KVW), row(KVW)]
    out_shape = [jax.ShapeDtypeStruct((bsz, t, KVW), F32)] * 2
    scratch = []
    if dils is None:
        out_specs.append(row(nq))
        out_shape.append(jax.ShapeDtypeStruct((bsz, t, nq), F32))
    else:
        for wd in (D, KVW, KVW):
            for dil in dils:
                out_specs.append(pl.BlockSpec((None, dil, tm // dil, wd), lambda b, i: (b, 0, i, 0)))
                out_shape.append(jax.ShapeDtypeStruct((bsz, dil, t // dil, wd), BF16))
    outs = pl.pallas_call(
        functools.partial(_attn_pre_kernel, dils=dils, tm=tm),
        grid=(bsz, t // tm),
        in_specs=[row(D), mod, mod, mod, mod, pl.BlockSpec((3, tm, LANES), lambda b, i: (0, i, 0)),
                  pl.BlockSpec((D, nq), lambda b, i: (0, 0)), pl.BlockSpec((D, 2 * KVW), lambda b, i: (0, 0))],
        out_specs=out_specs,
        out_shape=out_shape,
        scratch_shapes=scratch,
        compiler_params=_cp(("arbitrary", "arbitrary")),
        name="attn_pre",
    )(x, sc, sh, ksc, ksh, rot, w_q, w_kv)
    if dils is None:
        return outs
    ng = len(dils)
    return outs[0], outs[1], outs[2:2 + ng], outs[2 + ng:2 + 2 * ng], outs[2 + 2 * ng:]


def _attn_band_kernel(q_ref, ka_ref, kb_ref, va_ref, vb_ref, o_ref, lse_ref):
    i = pl.program_id(2)
    bq = ATT_WIN
    kcat = jnp.concatenate([ka_ref[...], kb_ref[...]], axis=0)
    vcat = jnp.concatenate([va_ref[...], vb_ref[...]], axis=0)
    qs = jnp.concatenate([q_ref[:, rep * KVW:(rep + 1) * KVW] for rep in range(REP)], axis=0)
    qrow = lax.broadcasted_iota(I32, (REP * bq, 2 * bq), 0) % bq
    kcol = lax.broadcasted_iota(I32, (REP * bq, 2 * bq), 1)
    valid = (kcol >= qrow) & (kcol <= qrow + bq) & ((i > 0) | (kcol >= bq))
    lane = lax.broadcasted_iota(I32, (REP * bq, KVW), 1) // HD
    kv_lane = lax.broadcasted_iota(I32, (2 * bq, KVW), 1) // HD
    lse_lane = lax.broadcasted_iota(I32, (bq, LANES), 1)
    lse = jnp.zeros((bq, LANES), F32)
    linv = jnp.ones((REP * bq, KVW), F32)
    ps = []
    for h in range(KV_HEADS):
        kh = jnp.where(kv_lane == h, kcat, jnp.zeros_like(kcat))
        s = lax.dot_general(qs, kh, (((1,), (1,)), ((), ())), preferred_element_type=F32)
        s = jnp.where(valid, s, -jnp.inf)
        m = jnp.max(s, axis=-1, keepdims=True)
        p = jnp.exp(s - m)
        l = jnp.sum(p, axis=-1, keepdims=True)
        ps.append(p.astype(BF16))
        linv = jnp.where(lane == h, 1.0 / l, linv)
        lse_h = m + jnp.log(l)
        for rep in range(REP):
            lse = jnp.where(lse_lane == rep * KV_HEADS + h, lse_h[rep * bq:(rep + 1) * bq], lse)
    vstack = jnp.concatenate([jnp.where(kv_lane == h, vcat, jnp.zeros_like(vcat)) for h in range(KV_HEADS)],
                             axis=0)
    o = (jnp.dot(jnp.concatenate(ps, axis=1), vstack, preferred_element_type=F32) * linv).astype(BF16)
    for rep in range(REP):
        o_ref[:, rep * KVW:(rep + 1) * KVW] = o[rep * bq:(rep + 1) * bq]
    lse_ref[...] = lse


def _attn_band(q, kb, vb, dil):
    bsz, _, tc, _ = q.shape
    nb = tc // ATT_WIN
    blk = lambda wd: pl.BlockSpec((None, None, ATT_WIN, wd), lambda b, c, i: (b, c, i, 0))
    prev = pl.BlockSpec((None, None, ATT_WIN, KVW), lambda b, c, i: (b, c, jnp.maximum(i - 1, 0), 0))
    return pl.pallas_call(
        _attn_band_kernel,
        grid=(bsz, dil, nb),
        in_specs=[blk(D), prev, blk(KVW), prev, blk(KVW)],
        out_specs=[blk(D), blk(LANES)],
        out_shape=[jax.ShapeDtypeStruct((bsz, dil, tc, D), BF16), jax.ShapeDtypeStruct((bsz, dil, tc, LANES), F32)],
        compiler_params=_cp(("arbitrary", "arbitrary", "arbitrary")),
        name=f"attn_band_d{dil}",
    )(q, kb, kb, vb, vb)


def _attn_step_kernel(q_ref, kn_ref, vn_ref, kt_ref, vt_ref, o_ref, lse_ref, *, wbuf):
    nq = len(DILATED_GROUPS) * REP
    row = lax.broadcasted_iota(I32, (nq, wbuf), 0)
    pos = lax.broadcasted_iota(I32, (nq, wbuf), 1)
    valid = None
    for g, (_, dil) in enumerate(DILATED_GROUPS):
        ok = (row // REP == g) & (pos >= wbuf - ATT_WIN * dil) & (pos % dil == 0)
        valid = ok if valid is None else valid | ok
    rnd = lambda t: t.astype(BF16).astype(F32)
    lse_lane = lax.broadcasted_iota(I32, (nq, LANES), 1)
    lse = jnp.zeros((nq, LANES), F32)
    for h in range(KV_HEADS):
        hs = slice(h * HD, (h + 1) * HD)
        qh = q_ref[:, hs]
        s = jnp.where(valid, _bdot(qh, kt_ref[h]), -1e30)
        s_n = jnp.sum(rnd(qh) * rnd(kn_ref[:, hs]), axis=-1, keepdims=True)
        m = jnp.maximum(jnp.max(s, axis=-1, keepdims=True), s_n)
        p = jnp.exp(s - m)
        p_n = jnp.exp(s_n - m)
        l = jnp.sum(p, axis=-1, keepdims=True) + p_n
        o_ref[:, hs] = (_bdot_nt(p, vt_ref[h]) + rnd(p_n) * rnd(vn_ref[:, hs])) / l
        lse = jnp.where(lse_lane == h, m + jnp.log(l), lse)
    lse_ref[...] = lse


def _attn_step(q, k_new, v_new, cache_k, cache_v):
    bsz = q.shape[0]
    wbuf = cache_k.shape[1]
    ng = len(DILATED_GROUPS)
    assert all(wbuf % dil == 0 and wbuf >= ATT_WIN * dil for _, dil in DILATED_GROUPS)
    nq = ng * REP
    cache_spec = pl.BlockSpec((None, KV_HEADS, HD, wbuf), lambda b: (b, 0, 0, 0))
    o, lse = pl.pallas_call(
        functools.partial(_attn_step_kernel, wbuf=wbuf),
        grid=(bsz,),
        in_specs=[pl.BlockSpec((None, nq, KVW), lambda b: (b, 0, 0)),
                  pl.BlockSpec((None, 1, KVW), lambda b: (b, 0, 0)),
                  pl.BlockSpec((None, 1, KVW), lambda b: (b, 0, 0)), cache_spec, cache_spec],
        out_specs=[pl.BlockSpec((None, nq, KVW), lambda b: (b, 0, 0)),
                   pl.BlockSpec((None, nq, LANES), lambda b: (b, 0, 0))],
        out_shape=[jax.ShapeDtypeStruct((bsz, nq, KVW), F32), jax.ShapeDtypeStruct((bsz, nq, LANES), F32)],
        compiler_params=_cp(("arbitrary",)),
        name="attn_step",
    )(q.reshape(bsz, nq, KVW), k_new.reshape(bsz, 1, KVW), v_new.reshape(bsz, 1, KVW),
      cache_k.transpose(0, 2, 3, 1), cache_v.transpose(0, 2, 3, 1))
    o = o.reshape(bsz, ng, D).transpose(1, 0, 2)
    lse = lse[:, :, :KV_HEADS].reshape(bsz, ng, REP * KV_HEADS).transpose(1, 0, 2)
    return o, jnp.pad(lse, ((0, 0), (0, 0), (0, LANES - REP * KV_HEADS)))


def _attn_post_kernel(o0_ref, o1_ref, o2_ref, l0_ref, l1_ref, l2_ref, x_ref, gt_ref, vec_ref, wo_ref, out_ref,
                      *scratch, dils, tm):
    o_refs = [o0_ref, o1_ref, o2_ref]
    l_refs = [l0_ref, l1_ref, l2_ref]
    if dils is not None:
        for g, dil in enumerate(dils):
            if dil == 1:
                o_refs[g], l_refs[g] = o_refs[g].at[0], l_refs[g].at[0]
                continue
            sl = scratch[g]
            for c in range(dil):
                sl[_class_rows(c, tm // dil, dil), :] = l_refs[g][c]
            l_refs[g] = sl
            o_cm = jnp.concatenate([o_refs[g][c] for c in range(dil)], axis=0)
            o_refs[g] = jnp.dot(_class_major_perm(tm, dil, inverse=True), o_cm, preferred_element_type=F32)
    o0, o1, o2 = (r if isinstance(r, jax.Array) else r[...] for r in o_refs)
    l0, l1, l2 = (r[...] for r in l_refs)
    m = jnp.maximum(jnp.maximum(l0, l1), l2)
    e0, e1, e2 = jnp.exp(l0 - m), jnp.exp(l1 - m), jnp.exp(l2 - m)
    den = e0 + e1 + e2
    r = lax.broadcasted_iota(I32, (LANES, D), 0)
    c = lax.broadcasted_iota(I32, (LANES, D), 1) // HD
    spread = (r == c).astype(BF16)
    mix = (_split_dot(e0 / den, spread) * o0 + _split_dot(e1 / den, spread) * o1
           + _split_dot(e2 / den, spread) * o2)
    out = _bdot(mix, wo_ref[...])
    out_ref[...] = _layer_norm(DN_ALPHA * x_ref[...] + gt_ref[...] * out, vec_ref[0:1, :], vec_ref[1:2, :])


def _attn_post(os, lses, x, gt, vec, w_o, *, per_b_mod, tm, dils=None):
    bsz, t, _ = x.shape
    row = lambda wd: pl.BlockSpec((None, tm, wd), lambda b, i: (b, i, 0))
    mod = pl.BlockSpec((None, 1, D), lambda b, i: (b, 0, 0)) if per_b_mod else row(D)
    if dils is None:
        o_specs, l_specs, scratch = [row(D)] * 3, [row(LANES)] * 3, []
    else:
        cls = lambda dil, wd: pl.BlockSpec((None, dil, tm // dil, wd), lambda b, i: (b, 0, i, 0))
        o_specs = [cls(dil, D) for dil in dils]
        l_specs = [cls(dil, LANES) for dil in dils]
        scratch = [pltpu.VMEM((tm, LANES), F32) for _ in dils]
    return pl.pallas_call(
        functools.partial(_attn_post_kernel, dils=dils, tm=tm),
        grid=(bsz, t // tm),
        in_specs=o_specs + l_specs + [row(D), mod, pl.BlockSpec((2, D), lambda b, i: (0, 0)),
                                      pl.BlockSpec((D, D), lambda b, i: (0, 0))],
        out_specs=row(D),
        out_shape=jax.ShapeDtypeStruct(x.shape, F32),
        scratch_shapes=scratch,
        compiler_params=_cp(("arbitrary", "arbitrary")),
        name="attn_post",
    )(*os, *lses, x, gt, vec, w_o)


def _rot_tables(pos):
    half = ROT_DIM // 2
    inv = ROPE_THETA ** (-jnp.arange(half, dtype=F32) * 2.0 / ROT_DIM)
    ang = pos.astype(F32)[:, None] * inv[None, :]
    cos, sin = jnp.cos(ang), jnp.sin(ang)
    t = pos.shape[0]
    one = jnp.ones((t, HD - ROT_DIM), F32)
    zero = jnp.zeros((t, HD - ROT_DIM), F32)
    zh = jnp.zeros((t, half), F32)
    cos_t = jnp.concatenate([cos, cos, one], axis=1)
    sa_t = jnp.concatenate([-sin, zh, zero], axis=1)
    sb_t = jnp.concatenate([zh, sin, zero], axis=1)
    return jnp.stack([jnp.tile(x, (1, 2)) for x in (cos_t, sa_t, sb_t)])


def _prep_weights(p):
    pad_c = lambda m, n: jnp.pad(m, ((0, 0), (0, n - m.shape[1])))
    pad_r = lambda m, n: jnp.pad(m, ((0, n - m.shape[0]), (0, 0)))
    w = {}
    w["mu"] = p["rwkv_mu"][0]
    w["w_rkv"] = p["rwkv_w_rkv"][0].astype(BF16)
    w["w1"] = pad_c(p["rwkv_w1"][0], LANES).astype(BF16)
    w["w2"] = pad_r(p["rwkv_w2"][0], LANES).astype(BF16)
    w["a1"] = pad_c(p["rwkv_a1"][0], LANES).astype(BF16)
    w["a2"] = pad_r(p["rwkv_a2"][0], LANES).astype(BF16)
    w["g1"] = pad_c(p["rwkv_g1"][0], 2 * LANES).astype(BF16)
    w["g2"] = pad_r(p["rwkv_g2"][0], 2 * LANES).astype(BF16)
    w["vec"] = jnp.stack([p["rwkv_w0"][0], p["rwkv_a0"][0], p["rwkv_k_k"][0], p["rwkv_k_a"][0]])
    w["post_vec"] = jnp.stack([p["rwkv_r_k"][0].reshape(D), p["rwkv_lnx_g"][0], p["rwkv_lnx_b"][0],
                               p["ln_g"][0, 0], p["ln_b"][0, 0]])
    w["rwkv_w_o"] = p["rwkv_w_o"][0].astype(BF16)
    wq = p["w_q"][0].reshape(D, 3, KV_HEADS, REP, HD).transpose(0, 1, 3, 2, 4).reshape(D, 3 * D)
    w["w_q"] = wq.astype(BF16)
    w["w_kv"] = p["w_kv"].astype(BF16)
    wo = p["w_o_attn"][0].reshape(KV_HEADS, REP, HD, D).transpose(1, 0, 2, 3).reshape(D, D)
    w["w_o_attn"] = wo.astype(BF16)
    w["router_w_t"] = p["router_w"].T.astype(BF16)
    w["router_b"] = p["router_b"].reshape(N_EXPERTS, 1)
    w["moe_w_in"] = p["moe_w_in"]
    w["moe_w_out"] = p["moe_w_out"]
    w["ln"] = [[jnp.stack([p["ln_g"][l, i], p["ln_b"][l, i]]) for i in range(2)] for l in range(DEPTH)]
    return w


def _modulations(c_prompt, c_sample, p):
    nb = c_prompt.shape[0]
    c = jnp.concatenate([c_prompt, c_sample], axis=0)
    pad = (-c.shape[0]) % 8
    c = jnp.pad(c, ((0, pad), (0, 0)))
    m3 = _ada_linear(c, p["ada_w"].reshape(2 * DEPTH, D, 3 * D), p["ada_b"].reshape(2 * DEPTH, 3 * D))
    m2 = _ada_linear(c, p["kv_ada_w"][None], p["kv_ada_b"][None])[0]
    n_all = nb + c_sample.shape[0]

    def split(m, parts, lo, hi):
        return [m[lo:hi, i * D:(i + 1) * D] for i in range(parts)]

    out = {}
    for name, lo, hi in (("prompt", 0, nb), ("sample", nb, n_all)):
        out[name] = {"ada": [[split(m3[2 * l + i], 3, lo, hi) for i in range(2)] for l in range(DEPTH)],
                     "kv": split(m2, 2, lo, hi)}
    return out


def _trunk_prompt(x, mods, w):
    bsz, t, _ = x.shape
    per_b = lambda m: m[:, None, :]
    sh, sc, gt = (per_b(m) for m in mods["ada"][0][0])
    zeros = jnp.zeros((bsz, 1, D), F32)
    r, lw, k, v, a, b, g, last = _rwkv_pre(x, sc, sh, zeros, w, seq_mode=True, tm=256)
    y, zf = _wkv_chunked(r, lw, k, v, a, b)
    x = _rwkv_post(y, r, k, v, g, x, gt, w["post_vec"], w["rwkv_w_o"], pairs=True, tm=256)
    zf = zf.reshape(bsz, PAIRS, 2, HD, 2, HD)
    wkv = jnp.stack([zf[:, :, 0, :, 0, :], zf[:, :, 1, :, 1, :]], axis=2)
    wkv = wkv.reshape(bsz, HEADS, HD, HD).transpose(0, 1, 3, 2)
    sh2, sc2, gt2 = (per_b(m) for m in mods["ada"][0][1])
    x = _moe_sorted(x, sc2, sh2, gt2, w["ln"][0][1], w, 0)

    sh, sc, gt = (per_b(m) for m in mods["ada"][1][0])
    ksh, ksc = (per_b(m) for m in mods["kv"])
    rot = _rot_tables(jnp.arange(t, dtype=I32))
    dils = tuple(dil for _, dil in DILATED_GROUPS)
    k_new, v_new, qs, kbs, vbs = _attn_pre(x, sc, sh, ksc, ksh, rot, w["w_q"], w["w_kv"], per_b_mod=True,
                                           tm=256, dils=dils)
    os, lses = zip(*[_attn_band(qs[g], kbs[g], vbs[g], dil) for g, dil in enumerate(dils)])
    x = _attn_post(os, lses, x, gt, w["ln"][1][0], w["w_o_attn"], per_b_mod=True, tm=256, dils=dils)
    sh2, sc2, gt2 = (per_b(m) for m in mods["ada"][1][1])
    x = _moe_sorted(x, sc2, sh2, gt2, w["ln"][1][1], w, 1)
    return x, wkv[None], last.reshape(1, bsz, D), k_new, v_new


def _trunk_sample(x, mods, state_wkv, state_shift, cache_k, cache_v, w):
    bsz = x.shape[0]
    xs = x.reshape(1, bsz, D)
    row = lambda m: m[None]
    sh, sc, gt = (row(m) for m in mods["ada"][0][0])
    r, lw, k, v, a, b, g, hm = _rwkv_pre(xs, sc, sh, state_shift[0][None], w, seq_mode=False, tm=bsz)
    s_new, y = _wkv_step(state_wkv[0].transpose(1, 2, 3, 0), *(t[0].T for t in (r, lw, k, v, a, b)))
    s_new = s_new.transpose(3, 0, 1, 2)
    xs = _rwkv_post(y.T[None], r, k, v, g, xs, gt, w["post_vec"], w["rwkv_w_o"], pairs=False, tm=bsz)
    sh2, sc2, gt2 = (row(m) for m in mods["ada"][0][1])
    xs = _moe_dense(xs, sc2, sh2, gt2, w["ln"][0][1], w, 0)

    sh, sc, gt = (row(m) for m in mods["ada"][1][0])
    ksh, ksc = (row(m) for m in mods["kv"])
    rot = _rot_tables(jnp.full((bsz,), PAST_LEN, I32))
    k_new, v_new, q = _attn_pre(xs, sc, sh, ksc, ksh, rot, w["w_q"], w["w_kv"], per_b_mod=False, tm=bsz)
    o, lse = _attn_step(q[0], k_new[0], v_new[0], cache_k, cache_v)
    xs = _attn_post([o[i][None] for i in range(3)], [lse[i][None] for i in range(3)], xs, gt, w["ln"][1][0],
                    w["w_o_attn"], per_b_mod=False, tm=bsz)
    sh2, sc2, gt2 = (row(m) for m in mods["ada"][1][1])
    xs = _moe_dense(xs, sc2, sh2, gt2, w["ln"][1][1], w, 1)
    return (xs.reshape(bsz, 1, D), s_new[None], hm, k_new.reshape(bsz, 1, KV_HEADS, HD),
            v_new.reshape(bsz, 1, KV_HEADS, HD))


def kernel(x_prompt, x_sample, state_wkv, state_shift, cache_k, cache_v, c_prompt, c_sample, ada_w, ada_b, ln_g, ln_b, rwkv_mu, rwkv_w_rkv, rwkv_w0, rwkv_w1, rwkv_w2, rwkv_a0, rwkv_a1, rwkv_a2, rwkv_g1, rwkv_g2, rwkv_k_k, rwkv_k_a, rwkv_r_k, rwkv_lnx_g, rwkv_lnx_b, rwkv_w_o, w_q, w_kv, kv_ada_w, kv_ada_b, w_o_attn, router_w, router_b, moe_w_in, moe_w_out):
    p = {"ada_w": ada_w, "ada_b": ada_b, "ln_g": ln_g, "ln_b": ln_b, "rwkv_mu": rwkv_mu, "rwkv_w_rkv": rwkv_w_rkv,
         "rwkv_w0": rwkv_w0, "rwkv_w1": rwkv_w1, "rwkv_w2": rwkv_w2, "rwkv_a0": rwkv_a0, "rwkv_a1": rwkv_a1,
         "rwkv_a2": rwkv_a2, "rwkv_g1": rwkv_g1, "rwkv_g2": rwkv_g2, "rwkv_k_k": rwkv_k_k, "rwkv_k_a": rwkv_k_a,
         "rwkv_r_k": rwkv_r_k, "rwkv_lnx_g": rwkv_lnx_g, "rwkv_lnx_b": rwkv_lnx_b, "rwkv_w_o": rwkv_w_o,
         "w_q": w_q, "w_kv": w_kv, "kv_ada_w": kv_ada_w, "kv_ada_b": kv_ada_b, "w_o_attn": w_o_attn,
         "router_w": router_w, "router_b": router_b, "moe_w_in": moe_w_in, "moe_w_out": moe_w_out}
    w = _prep_weights(p)
    mods = _modulations(c_prompt, c_sample, p)
    bp, tp, _ = x_prompt.shape
    y_p, wkv_p, shift_p, k_p, v_p = _trunk_prompt(x_prompt, mods["prompt"], w)
    keep = min(PAST_LEN, tp)
    k_p = k_p[:, tp - keep:].reshape(bp, keep, KV_HEADS, HD)
    v_p = v_p[:, tp - keep:].reshape(bp, keep, KV_HEADS, HD)
    y_s, wkv_s, shift_s, k_s, v_s = _trunk_sample(x_sample, mods["sample"], state_wkv, state_shift, cache_k,
                                                  cache_v, w)
    return (y_p, y_s, wkv_p, shift_p, k_p, v_p, wkv_s, shift_s, k_s, v_s)
```

```python
import functools
import math

import jax
import jax.numpy as jnp
from jax import lax
from jax.experimental import pallas as pl
from jax.experimental.pallas import tpu as pltpu

F32 = jnp.float32
BF16 = jnp.bfloat16
I32 = jnp.int32

D = 1024
HEADS = 16
HD = 64
LANES = 128
PAIRS = D // LANES
CHUNK = 64
WKV_GROUP = 16
N_EXPERTS = 32
EXPERTS_PER_GROUP = 8
N_EXPERT_GROUPS = 4
D_EXPERT = 512
MOE_BLK = 256
KV_HEADS = 4
REP = 4
KVW = KV_HEADS * HD
DILATED_GROUPS = ((128, 1), (512, 4), (2048, 16))
ATT_WIN = 128
PAST_LEN = 2048
ROT_DIM = 16
ROPE_THETA = 500000.0
DEPTH = 2
DN_ALPHA = (2 * DEPTH) ** 0.25
LN_EPS = 1e-5
GN_EPS = 64e-5
VMEM_LIMIT = 56 * 1024 * 1024


def _cp(sem):
    return pltpu.CompilerParams(dimension_semantics=sem, vmem_limit_bytes=VMEM_LIMIT)


def _bdot(a, b):
    return jnp.dot(a.astype(BF16), b.astype(BF16), preferred_element_type=F32)


def _bdot_nt(a, b):
    return lax.dot_general(a.astype(BF16), b.astype(BF16), (((1,), (1,)), ((), ())),
                           preferred_element_type=F32)


def _bdot_tn(a, b):
    return lax.dot_general(a.astype(BF16), b.astype(BF16), (((0,), (0,)), ((), ())),
                           preferred_element_type=F32)


def _split_dot(x, m):
    hi = x.astype(BF16)
    lo = (x - hi.astype(F32)).astype(BF16)
    return jnp.dot(jnp.concatenate([hi, lo], axis=1), jnp.concatenate([m, m], axis=0),
                   preferred_element_type=F32)


def _head_ones():
    r = lax.broadcasted_iota(I32, (LANES, LANES), 0) // HD
    c = lax.broadcasted_iota(I32, (LANES, LANES), 1) // HD
    return (r == c).astype(BF16)


def _segsum(x, ones_bd):
    cols = [_split_dot(x[:, c * LANES:(c + 1) * LANES], ones_bd) for c in range(x.shape[1] // LANES)]
    return cols[0] if len(cols) == 1 else jnp.concatenate(cols, axis=1)


def _layer_norm(x, g, b):
    mu = jnp.mean(x, axis=-1, keepdims=True)
    xc = x - mu
    var = jnp.mean(xc * xc, axis=-1, keepdims=True)
    return xc * lax.rsqrt(var + LN_EPS) * g + b


def _sigmoid(x):
    return 1.0 / (1.0 + jnp.exp(-x))


def _to_pairs(ref, val):
    for p in range(PAIRS):
        ref[p] = val[:, p * LANES:(p + 1) * LANES]


def _from_pairs(ref):
    return jnp.concatenate([ref[p] for p in range(PAIRS)], axis=1)


def _ada_kernel(c_ref, w_ref, b_ref, o_ref):
    c = c_ref[...]
    o_ref[...] = _bdot(c * _sigmoid(c), w_ref[...]) + b_ref[...]


def _ada_linear(c, w, b, tn=512):
    s, _, n = w.shape
    m = c.shape[0]
    return pl.pallas_call(
        _ada_kernel,
        grid=(s, n // tn),
        in_specs=[pl.BlockSpec((m, D), lambda i, j: (0, 0)),
                  pl.BlockSpec((None, D, tn), lambda i, j: (i, 0, j)),
                  pl.BlockSpec((None, 1, tn), lambda i, j: (i, 0, j))],
        out_specs=pl.BlockSpec((None, m, tn), lambda i, j: (i, 0, j)),
        out_shape=jax.ShapeDtypeStruct((s, m, n), F32),
        compiler_params=_cp(("arbitrary", "arbitrary")),
        name="ada_linear",
    )(c, w, b.reshape(s, 1, n))


def _rwkv_pre_kernel(x_ref, sc_ref, sh_ref, prev_ref, mu_ref, wrkv_ref, w1_ref, w2_ref, a1_ref, a2_ref,
                     g1_ref, g2_ref, vec_ref,
                     r_ref, lw_ref, k_ref, v_ref, a_ref, b_ref, g_ref, hm_ref, carry_ref, *, seq_mode, tm):
    x = x_ref[...]
    hm = x * (1.0 + sc_ref[...]) + sh_ref[...]
    if seq_mode:
        @pl.when(pl.program_id(1) == 0)
        def _():
            carry_ref[...] = prev_ref[...]
        row = lax.broadcasted_iota(I32, hm.shape, 0)
        hprev = jnp.where(row == 0, carry_ref[...], pltpu.roll(hm, 1, axis=0))
        carry_ref[...] = hm[tm - 1:tm, :]
        hm_ref[...] = hm[tm - 1:tm, :]
    else:
        hprev = prev_ref[...]
        hm_ref[...] = hm
    xx = hprev - hm

    def mix(i):
        return hm + xx * mu_ref[i:i + 1, :]

    w0, a0, k_k, k_a = (vec_ref[i:i + 1, :] for i in range(4))
    r = _bdot(mix(0), wrkv_ref[0])
    k = _bdot(mix(2), wrkv_ref[1])
    v = _bdot(mix(3), wrkv_ref[2])
    wl = w0 + _bdot(jnp.tanh(_bdot(mix(1), w1_ref[...])), w2_ref[...])
    lw = -math.exp(-0.5) * _sigmoid(wl)
    a_lr = _sigmoid(a0 + _bdot(_bdot(mix(4), a1_ref[...]), a2_ref[...]))
    g = _bdot(_sigmoid(_bdot(mix(5), g1_ref[...])), g2_ref[...])
    kk = k * k_k
    kk = kk * jnp.minimum(lax.rsqrt(_segsum(kk * kk, _head_ones())), 1e12)
    kmod = k * (1.0 + (a_lr - 1.0) * k_a)
    outs = ((r_ref, r), (lw_ref, lw), (k_ref, kmod), (v_ref, v), (a_ref, -kk), (b_ref, kk * a_lr), (g_ref, g))
    for ref, val in outs:
        if seq_mode:
            _to_pairs(ref, val)
        else:
            ref[...] = val


def _rwkv_pre(x, sc, sh, prev, w, *, seq_mode, tm):
    bsz, t, _ = x.shape
    grid = (bsz, t // tm)
    row = pl.BlockSpec((None, tm, D), lambda b, i: (b, i, 0))
    per_b = pl.BlockSpec((None, 1, D), lambda b, i: (b, 0, 0))
    mod = per_b if seq_mode else row

    def const(shape):
        return pl.BlockSpec(shape, lambda b, i: (0,) * len(shape))

    if seq_mode:
        out_big = pl.BlockSpec((None, PAIRS, tm, LANES), lambda b, i: (b, 0, i, 0))
        big_shape = jax.ShapeDtypeStruct((bsz, PAIRS, t, LANES), F32)
        hm_spec, hm_shape = per_b, jax.ShapeDtypeStruct((bsz, 1, D), F32)
    else:
        out_big, big_shape = row, jax.ShapeDtypeStruct((bsz, t, D), F32)
        hm_spec, hm_shape = row, jax.ShapeDtypeStruct((bsz, t, D), F32)
    return pl.pallas_call(
        functools.partial(_rwkv_pre_kernel, seq_mode=seq_mode, tm=tm),
        grid=grid,
        in_specs=[row, mod, mod, mod, const((6, D)), const((3, D, D)), const((D, LANES)), const((LANES, D)),
                  const((D, LANES)), const((LANES, D)), const((D, 2 * LANES)), const((2 * LANES, D)),
                  const((4, D))],
        out_specs=[out_big] * 7 + [hm_spec],
        out_shape=[big_shape] * 7 + [hm_shape],
        scratch_shapes=[pltpu.VMEM((1, D), F32)],
        compiler_params=_cp(("arbitrary", "arbitrary")),
        name="rwkv_pre",
    )(x, sc, sh, prev, w["mu"], w["w_rkv"], w["w1"], w["w2"], w["a1"], w["a2"], w["g1"], w["g2"], w["vec"])


def _wkv_chunk_kernel(r_ref, lw_ref, k_ref, v_ref, a_ref, b_ref, y_ref, zf_ref, z_ref, *, n_pairs, group):
    c = pl.program_id(0)

    @pl.when(c == 0)
    def _():
        z_ref[...] = jnp.zeros_like(z_ref)

    L = CHUNK
    row_l = lax.broadcasted_iota(I32, (L, L), 0)
    col_l = lax.broadcasted_iota(I32, (L, L), 1)
    tri_incl = (row_l >= col_l).astype(BF16)
    lane = lax.broadcasted_iota(I32, (L, LANES), 1)
    head0 = lane < HD
    row = lax.broadcasted_iota(I32, (LANES, LANES), 0)
    col = lax.broadcasted_iota(I32, (LANES, LANES), 1)
    strict = row > col
    incl = row >= col
    eye = (row == col).astype(F32)

    def expand(x):
        return jnp.concatenate([jnp.where(head0, x, 0.0), jnp.where(head0, 0.0, x)], axis=0)

    def group_body(gi, carry):
        ids = [gi * group + j for j in range(group)]
        bp = [(i // PAIRS, i % PAIRS) for i in ids]
        G = range(group)
        r = [r_ref[b_i, p_i] for b_i, p_i in bp]
        lw = [lw_ref[b_i, p_i] for b_i, p_i in bp]
        k = [k_ref[b_i, p_i] for b_i, p_i in bp]
        v = [v_ref[b_i, p_i] for b_i, p_i in bp]
        a = [a_ref[b_i, p_i] for b_i, p_i in bp]
        b = [b_ref[b_i, p_i] for b_i, p_i in bp]
        z0 = [z_ref[i] for i in ids]
        cum = [_split_dot_left(tri_incl, lw[j]) for j in G]
        cum_l = [cum[j][L - 1:L, :] for j in G]
        inv = [jnp.exp(-cum[j]) for j in G]
        tail = [jnp.exp(cum_l[j] - cum[j]) for j in G]
        a_e = [expand(a[j] * jnp.exp(cum[j] - lw[j])) for j in G]
        r_e = [expand(r[j] * jnp.exp(cum[j])) for j in G]
        b_e = [expand(b[j] * inv[j]) for j in G]
        k_e = [expand(k[j] * inv[j]) for j in G]
        bd_e = [expand(b[j] * tail[j]) for j in G]
        kd_e = [expand(k[j] * tail[j]) for j in G]
        v_e = [expand(v[j]) for j in G]
        gm = [_bdot_nt(jnp.concatenate([a_e[j], r_e[j]], axis=0), jnp.concatenate([b_e[j], k_e[j]], axis=0))
              for j in G]
        m_ab = [jnp.where(strict, gm[j][:LANES, :LANES], 0.0) for j in G]
        m_ak = [jnp.where(strict, gm[j][:LANES, LANES:], 0.0) for j in G]
        m_rb = [jnp.where(incl, gm[j][LANES:, :LANES], 0.0) for j in G]
        m_rk = [jnp.where(incl, gm[j][LANES:, LANES:], 0.0) for j in G]
        tinv = [eye + m_ab[j] for j in G]
        pw = [_bdot(m_ab[j], m_ab[j]) for j in G]
        for _ in range(4):
            both = [_bdot(pw[j], jnp.concatenate([pw[j], tinv[j]], axis=1)) for j in G]
            pw = [both[j][:, :LANES] for j in G]
            tinv = [tinv[j] + both[j][:, LANES:] for j in G]
        tinv = [tinv[j] + _bdot(pw[j], tinv[j]) for j in G]
        rhs = [_bdot(a_e[j], z0[j]) + _bdot(m_ak[j], v_e[j]) for j in G]
        u = [_bdot(tinv[j], rhs[j]) for j in G]
        zuv = [jnp.concatenate([z0[j], u[j], v_e[j]], axis=0) for j in G]
        decay = [eye * jnp.exp(cum_l[j]) for j in G]
        for j in G:
            z_ref[ids[j]] = _bdot_tn(jnp.concatenate([decay[j], bd_e[j], kd_e[j]], axis=0), zuv[j])
            y_e = _bdot(jnp.concatenate([r_e[j], m_rb[j], m_rk[j]], axis=1), zuv[j])
            y_ref[bp[j][0], bp[j][1]] = y_e[:L] + y_e[L:]
        return carry

    lax.fori_loop(0, n_pairs // group, group_body, 0)

    @pl.when(c == pl.num_programs(0) - 1)
    def _():
        zf_ref[...] = z_ref[...]


def _split_dot_left(m, x):
    hi = x.astype(BF16)
    lo = (x - hi.astype(F32)).astype(BF16)
    return (jnp.dot(m, hi, preferred_element_type=F32) + jnp.dot(m, lo, preferred_element_type=F32))


def _wkv_chunked(r, lw, k, v, a, b):
    bsz, _, t, _ = r.shape
    n_pairs = bsz * PAIRS
    blk = pl.BlockSpec((bsz, PAIRS, CHUNK, LANES), lambda c: (0, 0, c, 0))
    return pl.pallas_call(
        functools.partial(_wkv_chunk_kernel, n_pairs=n_pairs, group=WKV_GROUP),
        grid=(t // CHUNK,),
        in_specs=[blk] * 6,
        out_specs=[blk, pl.BlockSpec((n_pairs, LANES, LANES), lambda c: (0, 0, 0))],
        out_shape=[jax.ShapeDtypeStruct(r.shape, F32), jax.ShapeDtypeStruct((n_pairs, LANES, LANES), F32)],
        scratch_shapes=[pltpu.VMEM((n_pairs, LANES, LANES), F32)],
        compiler_params=_cp(("arbitrary",)),
        name="wkv_chunked",
    )(r, lw, k, v, a, b)


def _wkv_step_kernel(s_ref, r_ref, lw_ref, k_ref, v_ref, a_ref, b_ref, so_ref, y_ref):
    w = jnp.exp(lw_ref[...])
    a, b, k, r = a_ref[...], b_ref[...], k_ref[...], r_ref[...]

    def body(i, c):
        s = s_ref[i]
        sa = jnp.sum(s * a, axis=0, keepdims=True)
        s_new = s * w + sa * b + v_ref[pl.ds(i, 1), :] * k
        so_ref[i] = s_new
        y_ref[pl.ds(i, 1), :] = jnp.sum(s_new * r, axis=0, keepdims=True)
        return c

    lax.fori_loop(0, HD, body, 0, unroll=4)


def _wkv_step(state_t, r, lw, k, v, a, b):
    bsz = state_t.shape[-1]
    big = pl.BlockSpec((None, HD, HD, bsz), lambda h: (h, 0, 0, 0))
    vec = pl.BlockSpec((HD, bsz), lambda h: (h, 0))
    return pl.pallas_call(
        _wkv_step_kernel,
        grid=(HEADS,),
        in_specs=[big] + [vec] * 6,
        out_specs=[big, vec],
        out_shape=[jax.ShapeDtypeStruct(state_t.shape, F32), jax.ShapeDtypeStruct((D, bsz), F32)],
        compiler_params=_cp(("arbitrary",)),
        name="wkv_step",
    )(state_t, r, lw, k, v, a, b)


def _rwkv_post_kernel(y_ref, r_ref, k_ref, v_ref, g_ref, x_ref, gt_ref, vec_ref, wo_ref, o_ref, *, pairs):
    load = _from_pairs if pairs else (lambda ref: ref[...])
    y, r, k, v, g = (load(ref) for ref in (y_ref, r_ref, k_ref, v_ref, g_ref))
    r_k, lnx_g, lnx_b, ln_g, ln_b = (vec_ref[i:i + 1, :] for i in range(5))
    ones_bd = _head_ones()
    ym = _segsum(y, ones_bd) * (1.0 / HD)
    yc = y - ym
    yv = _segsum(yc * yc, ones_bd) * (1.0 / HD)
    yn = yc * lax.rsqrt(yv + GN_EPS) * lnx_g + lnx_b
    bonus = _segsum(r * k * r_k, ones_bd) * v
    mix = _bdot((yn + bonus) * g, wo_ref[...])
    o_ref[...] = _layer_norm(DN_ALPHA * x_ref[...] + gt_ref[...] * mix, ln_g, ln_b)


def _rwkv_post(y, r, k, v, g, x, gt, vec, w_o, *, pairs, tm):
    bsz, t, _ = x.shape
    row = pl.BlockSpec((None, tm, D), lambda b, i: (b, i, 0))
    big = pl.BlockSpec((None, PAIRS, tm, LANES), lambda b, i: (b, 0, i, 0)) if pairs else row
    mod = pl.BlockSpec((None, 1, D), lambda b, i: (b, 0, 0)) if pairs else row
    return pl.pallas_call(
        functools.partial(_rwkv_post_kernel, pairs=pairs),
        grid=(bsz, t // tm),
        in_specs=[big] * 5 + [row, mod, pl.BlockSpec((5, D), lambda b, i: (0, 0)),
                              pl.BlockSpec((D, D), lambda b, i: (0, 0))],
        out_specs=row,
        out_shape=jax.ShapeDtypeStruct(x.shape, F32),
        compiler_params=_cp(("arbitrary", "arbitrary")),
        name="rwkv_post",
    )(y, r, k, v, g, x, gt, vec, w_o)


def _top2(v):
    io = lax.broadcasted_iota(I32, v.shape, 0)
    m1 = jnp.max(v, axis=0, keepdims=True)
    i1 = jnp.min(jnp.where(v == m1, io, EXPERTS_PER_GROUP), axis=0, keepdims=True)
    v2 = jnp.where(io == i1, -jnp.inf, v)
    m2 = jnp.max(v2, axis=0, keepdims=True)
    i2 = jnp.min(jnp.where(v2 == m2, io, EXPERTS_PER_GROUP), axis=0, keepdims=True)
    return m1 + m2, i1, i2


def _moe_route_kernel(x_ref, sc_ref, sh_ref, rw_ref, rb_ref, h_ref, idx_ref, gate_ref, gmat_ref, cnt_ref,
                      *, tm):
    h = x_ref[...] * (1.0 + sc_ref[...]) + sh_ref[...]
    h_ref[...] = h
    aff = _sigmoid(_bdot_nt(rw_ref[...], h))
    sel = aff + rb_ref[...]
    best = gi = i1 = i2 = None
    for g in range(N_EXPERT_GROUPS):
        sc, j1, j2 = _top2(sel[g * EXPERTS_PER_GROUP:(g + 1) * EXPERTS_PER_GROUP, :])
        if g == 0:
            best, gi, i1, i2 = sc, jnp.zeros_like(j1), j1, j2
        else:
            upd = sc > best
            best = jnp.where(upd, sc, best)
            gi = jnp.where(upd, g, gi)
            i1 = jnp.where(upd, j1, i1)
            i2 = jnp.where(upd, j2, i2)
    e0 = gi * EXPERTS_PER_GROUP + i1
    e1 = gi * EXPERTS_PER_GROUP + i2
    io = lax.broadcasted_iota(I32, (N_EXPERTS, tm), 0)
    oh0 = io == e0
    oh1 = io == e1
    a0 = jnp.sum(jnp.where(oh0, aff, 0.0), axis=0, keepdims=True)
    a1 = jnp.sum(jnp.where(oh1, aff, 0.0), axis=0, keepdims=True)
    den = a0 + a1
    g0 = a0 / den
    g1 = a1 / den
    oh = jnp.where(oh0 | oh1, 1.0, 0.0)
    tr = lax.broadcasted_iota(I32, (tm, tm), 0)
    tc = lax.broadcasted_iota(I32, (tm, tm), 1)
    before = jnp.dot(oh.astype(BF16), (tr < tc).astype(BF16), preferred_element_type=F32)
    rank0 = jnp.sum(jnp.where(oh0, before, 0.0), axis=0, keepdims=True)
    rank1 = jnp.sum(jnp.where(oh1, before, 0.0), axis=0, keepdims=True)
    zi = jnp.zeros((4, tm), I32)
    idx_ref[...] = jnp.concatenate([e0, e1, rank0.astype(I32), rank1.astype(I32), zi], axis=0)
    gate_ref[...] = jnp.concatenate([g0, g1, jnp.zeros((6, tm), F32)], axis=0)
    gmat_ref[...] = jnp.where(oh0, g0, 0.0) + jnp.where(oh1, g1, 0.0)
    cnt_ref[...] = jnp.broadcast_to(jnp.sum(oh, axis=1, keepdims=True), cnt_ref.shape)


def _moe_route(x, sc, sh, router_w_t, router_b, *, per_b_mod, rows_out, tm):
    bsz, t, _ = x.shape
    n = bsz * t
    nt = t // tm
    row = pl.BlockSpec((None, tm, D), lambda b, i: (b, i, 0))
    mod = pl.BlockSpec((None, 1, D), lambda b, i: (b, 0, 0)) if per_b_mod else row
    if rows_out:
        h_spec = pl.BlockSpec((tm, D), lambda b, i: (b * nt + i, 0))
        h_shape = jax.ShapeDtypeStruct((n, D), F32)
    else:
        h_spec, h_shape = row, jax.ShapeDtypeStruct(x.shape, F32)
    tok = lambda rows: pl.BlockSpec((rows, tm), lambda b, i: (0, b * nt + i))
    return pl.pallas_call(
        functools.partial(_moe_route_kernel, tm=tm),
        grid=(bsz, nt),
        in_specs=[row, mod, mod, pl.BlockSpec((N_EXPERTS, D), lambda b, i: (0, 0)),
                  pl.BlockSpec((N_EXPERTS, 1), lambda b, i: (0, 0))],
        out_specs=[h_spec, tok(8), tok(8), tok(N_EXPERTS),
                   pl.BlockSpec((None, N_EXPERTS, LANES), lambda b, i: (b * nt + i, 0, 0))],
        out_shape=[h_shape, jax.ShapeDtypeStruct((8, n), I32), jax.ShapeDtypeStruct((8, n), F32),
                   jax.ShapeDtypeStruct((N_EXPERTS, n), F32),
                   jax.ShapeDtypeStruct((bsz * nt, N_EXPERTS, LANES), F32)],
        compiler_params=_cp(("arbitrary", "arbitrary")),
        name="moe_route",
    )(x, sc, sh, router_w_t, router_b)


def _moe_dispatch_kernel(c8_ref, lo_ref, rs_ref, zs_ref, h_ref, idx_ref, tab_ref, xs_ref,
                         buf, zbuf, sem, *, tm):
    step = pl.program_id(0)
    last = pl.num_programs(0) - 1
    s = step % 2

    def runs(tile, sl, start):
        for e in range(N_EXPERTS):
            j = tile * N_EXPERTS + e

            @pl.when(c8_ref[j] > 0)
            def _():
                size = pl.multiple_of(c8_ref[j], 8)
                cp = pltpu.make_async_copy(buf.at[sl, pl.ds(pl.multiple_of(lo_ref[j], 8), size), :],
                                           xs_ref.at[pl.ds(pl.multiple_of(rs_ref[j], 8), size), :], sem.at[sl])
                if start:
                    cp.start()
                else:
                    cp.wait()

    @pl.when(step == 0)
    def _():
        zbuf[...] = jnp.zeros_like(zbuf)

        def zero_block(first_slot):
            return pltpu.make_async_copy(zbuf, xs_ref.at[pl.ds(pl.multiple_of(first_slot, 8), MOE_BLK), :],
                                         sem.at[2])

        fresh = [zs_ref[e] != zs_ref[e - 1] for e in range(1, N_EXPERTS)]
        zero_block(zs_ref[0]).start()
        for e, f in enumerate(fresh, start=1):
            pl.when(f)(lambda e=e: zero_block(zs_ref[e]).start())
        zero_block(zs_ref[0]).wait()
        for e, f in enumerate(fresh, start=1):
            pl.when(f)(lambda e=e: zero_block(zs_ref[e]).wait())

        used, n_blocks = zs_ref[N_EXPERTS], xs_ref.shape[0] // MOE_BLK

        def tail_start(j, c):
            zero_block(j * MOE_BLK).start()
            return c

        def tail_wait(j, c):
            zero_block(j * MOE_BLK).wait()
            return c

        lax.fori_loop(used, n_blocks, tail_start, 0)
        lax.fori_loop(used, n_blocks, tail_wait, 0)

    @pl.when(step >= 2)
    def _():
        runs(step - 2, s, start=False)

    e0, e1, l0, l1 = (idx_ref[r:r + 1, :] for r in range(4))
    expert = lax.broadcasted_iota(I32, (N_EXPERTS, tm), 0)

    def lookup(col, e):
        return jnp.sum(jnp.where(expert == e, col, 0.0), axis=0, keepdims=True).astype(I32)

    lo_col = tab_ref[:, 0:1].astype(F32)
    pos0 = lookup(lo_col, e0) + l0
    pos1 = lookup(lo_col, e1) + l1
    row = lax.broadcasted_iota(I32, (buf.shape[1], tm), 0)
    perm = ((row == pos0) | (row == pos1)).astype(BF16)
    buf[s] = jnp.dot(perm, h_ref[...].astype(BF16), preferred_element_type=F32)
    runs(step, s, start=True)

    @pl.when(step == last)
    def _():
        @pl.when(step >= 1)
        def _():
            runs(step - 1, 1 - s, start=False)
        runs(step, s, start=False)


def _moe_dispatch(h_rows, idx, c8, loff, run_start, zero_start, n_slots, tm):
    n = h_rows.shape[0]
    tiles = n // tm
    buf_rows = 2 * tm + N_EXPERTS * 8
    tab = jnp.pad(loff[:, :, None], ((0, 0), (0, 0), (0, LANES - 1)))
    flat = lambda a: a.reshape(-1).astype(I32)
    return pl.pallas_call(
        functools.partial(_moe_dispatch_kernel, tm=tm),
        grid_spec=pltpu.PrefetchScalarGridSpec(
            num_scalar_prefetch=4, grid=(tiles,),
            in_specs=[pl.BlockSpec((tm, D), lambda i, *_: (i, 0)),
                      pl.BlockSpec((8, tm), lambda i, *_: (0, i)),
                      pl.BlockSpec((None, N_EXPERTS, LANES), lambda i, *_: (i, 0, 0))],
            out_specs=pl.BlockSpec(memory_space=pl.ANY),
            scratch_shapes=[pltpu.VMEM((2, buf_rows, D), F32), pltpu.VMEM((MOE_BLK, D), F32),
                            pltpu.SemaphoreType.DMA((3,))]),
        out_shape=jax.ShapeDtypeStruct((n_slots, D), F32),
        compiler_params=_cp(("arbitrary",)),
        name="moe_dispatch",
    )(flat(c8), flat(loff), flat(run_start), zero_start, h_rows, idx, tab)


def _moe_expert_kernel(be_ref, used_ref, nxt_ref, xs_ref, win_hbm, wout_hbm, ys_ref,
                       win_f, wout_f, win_bf, wout_bf, run_ref, sem, *, layer):
    i = pl.program_id(0)
    used = used_ref[0]

    def weight_copies(e, s):
        return (pltpu.make_async_copy(win_hbm.at[layer, e], win_f.at[s], sem.at[s, 0]),
                pltpu.make_async_copy(wout_hbm.at[layer, e], wout_f.at[s], sem.at[s, 1]))

    @pl.when(i == 0)
    def _():
        run_ref[0] = 0
        for cp in weight_copies(be_ref[0], 0):
            cp.start()

    fresh = (i == 0) | (be_ref[i] != be_ref[jnp.maximum(i - 1, 0)])

    @pl.when(fresh & (i < used))
    def _():
        run = run_ref[0] + jnp.where(i > 0, 1, 0)
        run_ref[0] = run
        s = run % 2
        for cp in weight_copies(be_ref[i], s):
            cp.wait()
        win_bf[...] = win_f[s].astype(BF16)
        wout_bf[...] = wout_f[s].astype(BF16)

        @pl.when(nxt_ref[i] >= 0)
        def _():
            for cp in weight_copies(nxt_ref[i], 1 - s):
                cp.start()

    @pl.when(i < used)
    def _():
        hmid = jnp.dot(xs_ref[...].astype(BF16), win_bf[...], preferred_element_type=F32)
        gt = hmid[:, :D_EXPERT]
        up = hmid[:, D_EXPERT:]
        act = gt * _sigmoid(gt) * up
        ys_ref[...] = jnp.dot(act.astype(BF16), wout_bf[...], preferred_element_type=F32)

    @pl.when(i >= used_ref[0])
    def _():
        ys_ref[...] = jnp.zeros_like(ys_ref)


def _moe_experts_sorted(xs, blk_exp, n_used, w_in, w_out, layer):
    n_slots = xs.shape[0]
    n_blocks = n_slots // MOE_BLK
    blk = jnp.arange(n_blocks, dtype=I32)
    later = (blk[None, :] > blk[:, None]) & (blk_exp[None, :] != blk_exp[:, None]) & (blk[None, :] < n_used[0])
    first_later = jnp.min(jnp.where(later, blk[None, :], n_blocks), axis=1)
    nxt = jnp.where(first_later < n_blocks, blk_exp[jnp.minimum(first_later, n_blocks - 1)], -1).astype(I32)
    x_map = lambda i, be, u, nx: (jnp.minimum(i, u[0] - 1), 0)
    return pl.pallas_call(
        functools.partial(_moe_expert_kernel, layer=layer),
        grid_spec=pltpu.PrefetchScalarGridSpec(
            num_scalar_prefetch=3, grid=(n_blocks,),
            in_specs=[pl.BlockSpec((MOE_BLK, D), x_map),
                      pl.BlockSpec(memory_space=pl.ANY), pl.BlockSpec(memory_space=pl.ANY)],
            out_specs=pl.BlockSpec((MOE_BLK, D), lambda i, be, u, nx: (i, 0)),
            scratch_shapes=[pltpu.VMEM((2, D, 2 * D_EXPERT), F32), pltpu.VMEM((2, D_EXPERT, D), F32),
                            pltpu.VMEM((D, 2 * D_EXPERT), BF16), pltpu.VMEM((D_EXPERT, D), BF16),
                            pltpu.SMEM((1,), I32), pltpu.SemaphoreType.DMA((2, 2))]),
        out_shape=jax.ShapeDtypeStruct((n_slots, D), F32),
        compiler_params=_cp(("arbitrary",)),
        name="moe_experts_sorted",
    )(blk_exp, n_used, nxt, xs, w_in, w_out)


def _moe_combine_kernel(c8_ref, lo_ref, rs_ref, ys_ref, idx_ref, tab_ref, gate_ref, x_ref, gt_ref, vec_ref,
                        o_ref, buf, sem, *, tm):
    step = pl.program_id(0)
    s = step % 2

    def runs(tile, sl, start):
        for e in range(N_EXPERTS):
            j = tile * N_EXPERTS + e

            @pl.when(c8_ref[j] > 0)
            def _():
                size = pl.multiple_of(c8_ref[j], 8)
                cp = pltpu.make_async_copy(ys_ref.at[pl.ds(pl.multiple_of(rs_ref[j], 8), size), :],
                                           buf.at[sl, pl.ds(pl.multiple_of(lo_ref[j], 8), size), :], sem.at[sl])
                if start:
                    cp.start()
                else:
                    cp.wait()

    @pl.when(step == 0)
    def _():
        buf[...] = jnp.zeros_like(buf)
        runs(0, 0, start=True)

    @pl.when(step + 1 < pl.num_programs(0))
    def _():
        runs(step + 1, 1 - s, start=True)

    runs(step, s, start=False)
    e0, e1, l0, l1 = (idx_ref[r:r + 1, :] for r in range(4))
    expert = lax.broadcasted_iota(I32, (N_EXPERTS, tm), 0)
    lo_col = tab_ref[:, 0:1].astype(F32)

    def position(e, l):
        return jnp.sum(jnp.where(expert == e, lo_col, 0.0), axis=0, keepdims=True).astype(I32) + l

    row = lax.broadcasted_iota(I32, (buf.shape[1], tm), 0)
    y_sorted = buf[s].astype(BF16)
    ya = _bdot_tn((row == position(e0, l0)).astype(BF16), y_sorted)
    yb = _bdot_tn((row == position(e1, l1)).astype(BF16), y_sorted)
    ff = ya * gate_ref[:, 0:1] + yb * gate_ref[:, 1:2]
    o_ref[...] = _layer_norm(DN_ALPHA * x_ref[...] + gt_ref[...] * ff, vec_ref[0:1, :], vec_ref[1:2, :])


def _moe_combine(ys, idx, c8, loff, run_start, gates, x, gt, vec, tm):
    bsz, t, _ = x.shape
    nt = t // tm
    buf_rows = 2 * tm + N_EXPERTS * 8
    tab = jnp.pad(loff[:, :, None], ((0, 0), (0, 0), (0, LANES - 1)))
    flat = lambda a: a.reshape(-1).astype(I32)
    return pl.pallas_call(
        functools.partial(_moe_combine_kernel, tm=tm),
        grid_spec=pltpu.PrefetchScalarGridSpec(
            num_scalar_prefetch=3, grid=(bsz * nt,),
            in_specs=[pl.BlockSpec(memory_space=pl.ANY),
                      pl.BlockSpec((8, tm), lambda i, *_: (0, i)),
                      pl.BlockSpec((None, N_EXPERTS, LANES), lambda i, *_: (i, 0, 0)),
                      pl.BlockSpec((tm, 2), lambda i, *_: (i, 0)),
                      pl.BlockSpec((None, tm, D), lambda i, *_: (i // nt, i % nt, 0)),
                      pl.BlockSpec((None, 1, D), lambda i, *_: (i // nt, 0, 0)),
                      pl.BlockSpec((2, D), lambda i, *_: (0, 0))],
            out_specs=pl.BlockSpec((None, tm, D), lambda i, *_: (i // nt, i % nt, 0)),
            scratch_shapes=[pltpu.VMEM((2, buf_rows, D), F32), pltpu.SemaphoreType.DMA((2,))]),
        out_shape=jax.ShapeDtypeStruct(x.shape, F32),
        compiler_params=_cp(("arbitrary",)),
        name="moe_combine",
    )(flat(c8), flat(loff), flat(run_start), ys, idx, tab, gates, x, gt, vec)


def _moe_sorted(x, sc, sh, gt, vec, w, layer, tm=256):
    bsz, t, _ = x.shape
    n = bsz * t
    tiles = n // tm
    n_blocks = -(-(2 * n + tiles * N_EXPERTS * 7 + N_EXPERTS * (MOE_BLK - 1)) // MOE_BLK)
    h_rows, idx, gate, _, cnt = _moe_route(x, sc, sh, w["router_w_t"], w["router_b"], per_b_mod=True,
                                           rows_out=True, tm=tm)
    c8 = (cnt[:, :, 0].astype(I32) + 7) // 8 * 8
    padded = (jnp.sum(c8, axis=0) + MOE_BLK - 1) // MOE_BLK * MOE_BLK
    pad_end = jnp.cumsum(padded)
    run_start = (pad_end - padded)[None, :] + jnp.cumsum(c8, axis=0) - c8
    loff = jnp.cumsum(c8, axis=1) - c8
    blk_start = jnp.arange(n_blocks, dtype=I32) * MOE_BLK
    blk_exp = jnp.minimum(jnp.sum((pad_end[None, :] <= blk_start[:, None]).astype(I32), axis=1), N_EXPERTS - 1)
    n_used = (pad_end[-1:] // MOE_BLK).astype(I32)
    zero_start = jnp.concatenate([jnp.maximum(pad_end - MOE_BLK, 0), n_used])
    xs = _moe_dispatch(h_rows, idx, c8, loff, run_start, zero_start, n_blocks * MOE_BLK, tm)
    ys = _moe_experts_sorted(xs, blk_exp, n_used, w["moe_w_in"], w["moe_w_out"], layer)
    return _moe_combine(ys, idx, c8, loff, run_start, gate[:2].T, x, gt, vec, tm)


def _moe_dense_kernel(h_ref, gm_ref, win_ref, wout_ref, x_ref, gt_ref, vec_ref, o_ref, acc_ref):
    e = pl.program_id(0)

    @pl.when(e == 0)
    def _():
        acc_ref[...] = jnp.zeros_like(acc_ref)

    lane = lax.broadcasted_iota(I32, gm_ref.shape, 1)
    gcol = jnp.sum(jnp.where(lane == e, gm_ref[...], 0.0), axis=1, keepdims=True)
    hmid = _bdot(h_ref[...], win_ref[...])
    gt = hmid[:, :D_EXPERT]
    up = hmid[:, D_EXPERT:]
    y = _bdot(gt * _sigmoid(gt) * up, wout_ref[...])
    acc_ref[...] = acc_ref[...] + jnp.where(gcol != 0.0, gcol * y, 0.0)

    @pl.when(e == N_EXPERTS - 1)
    def _():
        o_ref[...] = _layer_norm(DN_ALPHA * x_ref[...] + gt_ref[...] * acc_ref[...], vec_ref[0:1, :],
                                 vec_ref[1:2, :])


def _moe_dense(x, sc, sh, gt, vec, w, layer):
    _, m, _ = x.shape
    h, _, _, gmat, _ = _moe_route(x, sc, sh, w["router_w_t"], w["router_b"], per_b_mod=False, rows_out=False,
                                  tm=m)
    full = pl.BlockSpec((m, D), lambda e: (0, 0))
    out = pl.pallas_call(
        _moe_dense_kernel,
        grid=(N_EXPERTS,),
        in_specs=[full, pl.BlockSpec((m, N_EXPERTS), lambda e: (0, 0)),
                  pl.BlockSpec((None, None, D, 2 * D_EXPERT), lambda e: (layer, e, 0, 0)),
                  pl.BlockSpec((None, None, D_EXPERT, D), lambda e: (layer, e, 0, 0)),
                  full, full, pl.BlockSpec((2, D), lambda e: (0, 0))],
        out_specs=full,
        out_shape=jax.ShapeDtypeStruct((m, D), F32),
        scratch_shapes=[pltpu.VMEM((m, D), F32)],
        compiler_params=_cp(("arbitrary",)),
        name="moe_dense",
    )(h[0], gmat.T, w["moe_w_in"], w["moe_w_out"], x[0], gt[0], vec)
    return out[None]


def _rotary(x, cos_t, sa_t, sb_t):
    reps = x.shape[1] // LANES
    tile = lambda t: t if reps == 1 else jnp.concatenate([t] * reps, axis=1)
    n = x.shape[1]
    half = ROT_DIM // 2
    return x * tile(cos_t) + pltpu.roll(x, n - half, axis=1) * tile(sa_t) + pltpu.roll(x, half, axis=1) * tile(sb_t)


def _class_rows(c, rows, dil):
    return pl.ds(c, rows, stride=dil) if dil > 1 else pl.ds(0, rows)


def _class_major_perm(tm, dil, inverse=False):
    cm_row = lax.broadcasted_iota(I32, (tm, tm), 1 if inverse else 0)
    tok_row = lax.broadcasted_iota(I32, (tm, tm), 0 if inverse else 1)
    rows = tm // dil
    return ((cm_row // rows == tok_row % dil) & (cm_row % rows == tok_row // dil)).astype(BF16)


def _attn_pre_kernel(x_ref, sc_ref, sh_ref, ksc_ref, ksh_ref, rot_ref, wq_ref, wkv_ref, *refs, dils, tm):
    x = x_ref[...]
    cos_t, sa_t, sb_t = rot_ref[0], rot_ref[1], rot_ref[2]
    q = _rotary(_bdot(x * (1.0 + sc_ref[...]) + sh_ref[...], wq_ref[...]), cos_t, sa_t, sb_t) * HD ** -0.5
    kv = _bdot(x * (1.0 + ksc_ref[...]) + ksh_ref[...], wkv_ref[...])
    k = _rotary(kv[:, :KVW], cos_t, sa_t, sb_t)
    v = kv[:, KVW:]
    k_ref, v_ref = refs[0], refs[1]
    k_ref[...] = k
    v_ref[...] = v
    if dils is None:
        refs[2][...] = q
        return
    ng = len(dils)
    q_refs, kb_refs, vb_refs = refs[2:2 + ng], refs[2 + ng:2 + 2 * ng], refs[2 + 2 * ng:2 + 3 * ng]
    qb, kb, vb = q.astype(BF16), k.astype(BF16), v.astype(BF16)
    for g, dil in enumerate(dils):
        rows = tm // dil
        blk = jnp.concatenate([qb[:, g * D:(g + 1) * D], kb, vb], axis=1)
        if dil > 1:
            blk = jnp.dot(_class_major_perm(tm, dil), blk, preferred_element_type=F32).astype(BF16)
        for c in range(dil):
            rs = slice(c * rows, (c + 1) * rows)
            q_refs[g][c] = blk[rs, :D]
            kb_refs[g][c] = blk[rs, D:D + KVW]
            vb_refs[g][c] = blk[rs, D + KVW:]


def _attn_pre(x, sc, sh, ksc, ksh, rot, w_q, w_kv, *, per_b_mod, tm, dils=None):
    bsz, t, _ = x.shape
    row = lambda wd: pl.BlockSpec((None, tm, wd), lambda b, i: (b, i, 0))
    mod = pl.BlockSpec((None, 1, D), lambda b, i: (b, 0, 0)) if per_b_mod else row(D)
    nq = w_q.shape[1]
    out_specs = [row(KVW), row(KVW)]
    out_shape = [jax.ShapeDtypeStruct((bsz, t, KVW), F32)] * 2
    scratch = []
    if dils is None:
        out_specs.append(row(nq))
        out_shape.append(jax.ShapeDtypeStruct((bsz, t, nq), F32))
    else:
        for wd in (D, KVW, KVW):
            for dil in dils:
                out_specs.append(pl.BlockSpec((None, dil, tm // dil, wd), lambda b, i: (b, 0, i, 0)))
                out_shape.append(jax.ShapeDtypeStruct((bsz, dil, t // dil, wd), BF16))
    outs = pl.pallas_call(
        functools.partial(_attn_pre_kernel, dils=dils, tm=tm),
        grid=(bsz, t // tm),
        in_specs=[row(D), mod, mod, mod, mod, pl.BlockSpec((3, tm, LANES), lambda b, i: (0, i, 0)),
                  pl.BlockSpec((D, nq), lambda b, i: (0, 0)), pl.BlockSpec((D, 2 * KVW), lambda b, i: (0, 0))],
        out_specs=out_specs,
        out_shape=out_shape,
        scratch_shapes=scratch,
        compiler_params=_cp(("arbitrary", "arbitrary")),
        name="attn_pre",
    )(x, sc, sh, ksc, ksh, rot, w_q, w_kv)
    if dils is None:
        return outs
    ng = len(dils)
    return outs[0], outs[1], outs[2:2 + ng], outs[2 + ng:2 + 2 * ng], outs[2 + 2 * ng:]


def _attn_band_kernel(q_ref, ka_ref, kb_ref, va_ref, vb_ref, o_ref, lse_ref):
    i = pl.program_id(2)
    bq = ATT_WIN
    kcat = jnp.concatenate([ka_ref[...], kb_ref[...]], axis=0)
    vcat = jnp.concatenate([va_ref[...], vb_ref[...]], axis=0)
    qs = jnp.concatenate([q_ref[:, rep * KVW:(rep + 1) * KVW] for rep in range(REP)], axis=0)
    qrow = lax.broadcasted_iota(I32, (REP * bq, 2 * bq), 0) % bq
    kcol = lax.broadcasted_iota(I32, (REP * bq, 2 * bq), 1)
    valid = (kcol >= qrow) & (kcol <= qrow + bq) & ((i > 0) | (kcol >= bq))
    lane = lax.broadcasted_iota(I32, (REP * bq, KVW), 1) // HD
    kv_lane = lax.broadcasted_iota(I32, (2 * bq, KVW), 1) // HD
    lse_lane = lax.broadcasted_iota(I32, (bq, LANES), 1)
    lse = jnp.zeros((bq, LANES), F32)
    linv = jnp.ones((REP * bq, KVW), F32)
    ps = []
    for h in range(KV_HEADS):
        kh = jnp.where(kv_lane == h, kcat, jnp.zeros_like(kcat))
        s = lax.dot_general(qs, kh, (((1,), (1,)), ((), ())), preferred_element_type=F32)
        s = jnp.where(valid, s, -jnp.inf)
        m = jnp.max(s, axis=-1, keepdims=True)
        p = jnp.exp(s - m)
        l = jnp.sum(p, axis=-1, keepdims=True)
        ps.append(p.astype(BF16))
        linv = jnp.where(lane == h, 1.0 / l, linv)
        lse_h = m + jnp.log(l)
        for rep in range(REP):
            lse = jnp.where(lse_lane == rep * KV_HEADS + h, lse_h[rep * bq:(rep + 1) * bq], lse)
    vstack = jnp.concatenate([jnp.where(kv_lane == h, vcat, jnp.zeros_like(vcat)) for h in range(KV_HEADS)],
                             axis=0)
    o = (jnp.dot(jnp.concatenate(ps, axis=1), vstack, preferred_element_type=F32) * linv).astype(BF16)
    for rep in range(REP):
        o_ref[:, rep * KVW:(rep + 1) * KVW] = o[rep * bq:(rep + 1) * bq]
    lse_ref[...] = lse


def _attn_band(q, kb, vb, dil):
    bsz, _, tc, _ = q.shape
    nb = tc // ATT_WIN
    blk = lambda wd: pl.BlockSpec((None, None, ATT_WIN, wd), lambda b, c, i: (b, c, i, 0))
    prev = pl.BlockSpec((None, None, ATT_WIN, KVW), lambda b, c, i: (b, c, jnp.maximum(i - 1, 0), 0))
    return pl.pallas_call(
        _attn_band_kernel,
        grid=(bsz, dil, nb),
        in_specs=[blk(D), prev, blk(KVW), prev, blk(KVW)],
        out_specs=[blk(D), blk(LANES)],
        out_shape=[jax.ShapeDtypeStruct((bsz, dil, tc, D), BF16), jax.ShapeDtypeStruct((bsz, dil, tc, LANES), F32)],
        compiler_params=_cp(("arbitrary", "arbitrary", "arbitrary")),
        name=f"attn_band_d{dil}",
    )(q, kb, kb, vb, vb)


def _attn_step_kernel(q_ref, kn_ref, vn_ref, kt_ref, vt_ref, o_ref, lse_ref, *, wbuf):
    nq = len(DILATED_GROUPS) * REP
    row = lax.broadcasted_iota(I32, (nq, wbuf), 0)
    pos = lax.broadcasted_iota(I32, (nq, wbuf), 1)
    valid = None
    for g, (_, dil) in enumerate(DILATED_GROUPS):
        ok = (row // REP == g) & (pos >= wbuf - ATT_WIN * dil) & (pos % dil == 0)
        valid = ok if valid is None else valid | ok
    rnd = lambda t: t.astype(BF16).astype(F32)
    lse_lane = lax.broadcasted_iota(I32, (nq, LANES), 1)
    lse = jnp.zeros((nq, LANES), F32)
    for h in range(KV_HEADS):
        hs = slice(h * HD, (h + 1) * HD)
        qh = q_ref[:, hs]
        s = jnp.where(valid, _bdot(qh, kt_ref[h]), -1e30)
        s_n = jnp.sum(rnd(qh) * rnd(kn_ref[:, hs]), axis=-1, keepdims=True)
        m = jnp.maximum(jnp.max(s, axis=-1, keepdims=True), s_n)
        p = jnp.exp(s - m)
        p_n = jnp.exp(s_n - m)
        l = jnp.sum(p, axis=-1, keepdims=True) + p_n
        o_ref[:, hs] = (_bdot_nt(p, vt_ref[h]) + rnd(p_n) * rnd(vn_ref[:, hs])) / l
        lse = jnp.where(lse_lane == h, m + jnp.log(l), lse)
    lse_ref[...] = lse


def _attn_step(q, k_new, v_new, cache_k, cache_v):
    bsz = q.shape[0]
    wbuf = cache_k.shape[1]
    ng = len(DILATED_GROUPS)
    assert all(wbuf % dil == 0 and wbuf >= ATT_WIN * dil for _, dil in DILATED_GROUPS)
    nq = ng * REP
    cache_spec = pl.BlockSpec((None, KV_HEADS, HD, wbuf), lambda b: (b, 0, 0, 0))
    o, lse = pl.pallas_call(
        functools.partial(_attn_step_kernel, wbuf=wbuf),
        grid=(bsz,),
        in_specs=[pl.BlockSpec((None, nq, KVW), lambda b: (b, 0, 0)),
                  pl.BlockSpec((None, 1, KVW), lambda b: (b, 0, 0)),
                  pl.BlockSpec((None, 1, KVW), lambda b: (b, 0, 0)), cache_spec, cache_spec],
        out_specs=[pl.BlockSpec((None, nq, KVW), lambda b: (b, 0, 0)),
                   pl.BlockSpec((None, nq, LANES), lambda b: (b, 0, 0))],
        out_shape=[jax.ShapeDtypeStruct((bsz, nq, KVW), F32), jax.ShapeDtypeStruct((bsz, nq, LANES), F32)],
        compiler_params=_cp(("arbitrary",)),
        name="attn_step",
    )(q.reshape(bsz, nq, KVW), k_new.reshape(bsz, 1, KVW), v_new.reshape(bsz, 1, KVW),
      cache_k.transpose(0, 2, 3, 1), cache_v.transpose(0, 2, 3, 1))
    o = o.reshape(bsz, ng, D).transpose(1, 0, 2)
    lse = lse[:, :, :KV_HEADS].reshape(bsz, ng, REP * KV_HEADS).transpose(1, 0, 2)
    return o, jnp.pad(lse, ((0, 0), (0, 0), (0, LANES - REP * KV_HEADS)))


def _attn_post_kernel(o0_ref, o1_ref, o2_ref, l0_ref, l1_ref, l2_ref, x_ref, gt_ref, vec_ref, wo_ref, out_ref,
                      *scratch, dils, tm):
    o_refs = [o0_ref, o1_ref, o2_ref]
    l_refs = [l0_ref, l1_ref, l2_ref]
    if dils is not None:
        for g, dil in enumerate(dils):
            if dil == 1:
                o_refs[g], l_refs[g] = o_refs[g].at[0], l_refs[g].at[0]
                continue
            sl = scratch[g]
            for c in range(dil):
                sl[_class_rows(c, tm // dil, dil), :] = l_refs[g][c]
            l_refs[g] = sl
            o_cm = jnp.concatenate([o_refs[g][c] for c in range(dil)], axis=0)
            o_refs[g] = jnp.dot(_class_major_perm(tm, dil, inverse=True), o_cm, preferred_element_type=F32)
    o0, o1, o2 = (r if isinstance(r, jax.Array) else r[...] for r in o_refs)
    l0, l1, l2 = (r[...] for r in l_refs)
    m = jnp.maximum(jnp.maximum(l0, l1), l2)
    e0, e1, e2 = jnp.exp(l0 - m), jnp.exp(l1 - m), jnp.exp(l2 - m)
    den = e0 + e1 + e2
    r = lax.broadcasted_iota(I32, (LANES, D), 0)
    c = lax.broadcasted_iota(I32, (LANES, D), 1) // HD
    spread = (r == c).astype(BF16)
    mix = (_split_dot(e0 / den, spread) * o0 + _split_dot(e1 / den, spread) * o1
           + _split_dot(e2 / den, spread) * o2)
    out = _bdot(mix, wo_ref[...])
    out_ref[...] = _layer_norm(DN_ALPHA * x_ref[...] + gt_ref[...] * out, vec_ref[0:1, :], vec_ref[1:2, :])


def _attn_post(os, lses, x, gt, vec, w_o, *, per_b_mod, tm, dils=None):
    bsz, t, _ = x.shape
    row = lambda wd: pl.BlockSpec((None, tm, wd), lambda b, i: (b, i, 0))
    mod = pl.BlockSpec((None, 1, D), lambda b, i: (b, 0, 0)) if per_b_mod else row(D)
    if dils is None:
        o_specs, l_specs, scratch = [row(D)] * 3, [row(LANES)] * 3, []
    else:
        cls = lambda dil, wd: pl.BlockSpec((None, dil, tm // dil, wd), lambda b, i: (b, 0, i, 0))
        o_specs = [cls(dil, D) for dil in dils]
        l_specs = [cls(dil, LANES) for dil in dils]
        scratch = [pltpu.VMEM((tm, LANES), F32) for _ in dils]
    return pl.pallas_call(
        functools.partial(_attn_post_kernel, dils=dils, tm=tm),
        grid=(bsz, t // tm),
        in_specs=o_specs + l_specs + [row(D), mod, pl.BlockSpec((2, D), lambda b, i: (0, 0)),
                                      pl.BlockSpec((D, D), lambda b, i: (0, 0))],
        out_specs=row(D),
        out_shape=jax.ShapeDtypeStruct(x.shape, F32),
        scratch_shapes=scratch,
        compiler_params=_cp(("arbitrary", "arbitrary")),
        name="attn_post",
    )(*os, *lses, x, gt, vec, w_o)


def _rot_tables(pos):
    half = ROT_DIM // 2
    inv = ROPE_THETA ** (-jnp.arange(half, dtype=F32) * 2.0 / ROT_DIM)
    ang = pos.astype(F32)[:, None] * inv[None, :]
    cos, sin = jnp.cos(ang), jnp.sin(ang)
    t = pos.shape[0]
    one = jnp.ones((t, HD - ROT_DIM), F32)
    zero = jnp.zeros((t, HD - ROT_DIM), F32)
    zh = jnp.zeros((t, half), F32)
    cos_t = jnp.concatenate([cos, cos, one], axis=1)
    sa_t = jnp.concatenate([-sin, zh, zero], axis=1)
    sb_t = jnp.concatenate([zh, sin, zero], axis=1)
    return jnp.stack([jnp.tile(x, (1, 2)) for x in (cos_t, sa_t, sb_t)])


def _prep_weights(p):
    pad_c = lambda m, n: jnp.pad(m, ((0, 0), (0, n - m.shape[1])))
    pad_r = lambda m, n: jnp.pad(m, ((0, n - m.shape[0]), (0, 0)))
    w = {}
    w["mu"] = p["rwkv_mu"][0]
    w["w_rkv"] = p["rwkv_w_rkv"][0].astype(BF16)
    w["w1"] = pad_c(p["rwkv_w1"][0], LANES).astype(BF16)
    w["w2"] = pad_r(p["rwkv_w2"][0], LANES).astype(BF16)
    w["a1"] = pad_c(p["rwkv_a1"][0], LANES).astype(BF16)
    w["a2"] = pad_r(p["rwkv_a2"][0], LANES).astype(BF16)
    w["g1"] = pad_c(p["rwkv_g1"][0], 2 * LANES).astype(BF16)
    w["g2"] = pad_r(p["rwkv_g2"][0], 2 * LANES).astype(BF16)
    w["vec"] = jnp.stack([p["rwkv_w0"][0], p["rwkv_a0"][0], p["rwkv_k_k"][0], p["rwkv_k_a"][0]])
    w["post_vec"] = jnp.stack([p["rwkv_r_k"][0].reshape(D), p["rwkv_lnx_g"][0], p["rwkv_lnx_b"][0],
                               p["ln_g"][0, 0], p["ln_b"][0, 0]])
    w["rwkv_w_o"] = p["rwkv_w_o"][0].astype(BF16)
    wq = p["w_q"][0].reshape(D, 3, KV_HEADS, REP, HD).transpose(0, 1, 3, 2, 4).reshape(D, 3 * D)
    w["w_q"] = wq.astype(BF16)
    w["w_kv"] = p["w_kv"].astype(BF16)
    wo = p["w_o_attn"][0].reshape(KV_HEADS, REP, HD, D).transpose(1, 0, 2, 3).reshape(D, D)
    w["w_o_attn"] = wo.astype(BF16)
    w["router_w_t"] = p["router_w"].T.astype(BF16)
    w["router_b"] = p["router_b"].reshape(N_EXPERTS, 1)
    w["moe_w_in"] = p["moe_w_in"]
    w["moe_w_out"] = p["moe_w_out"]
    w["ln"] = [[jnp.stack([p["ln_g"][l, i], p["ln_b"][l, i]]) for i in range(2)] for l in range(DEPTH)]
    return w


def _modulations(c_prompt, c_sample, p):
    nb = c_prompt.shape[0]
    c = jnp.concatenate([c_prompt, c_sample], axis=0)
    pad = (-c.shape[0]) % 8
    c = jnp.pad(c, ((0, pad), (0, 0)))
    m3 = _ada_linear(c, p["ada_w"].reshape(2 * DEPTH, D, 3 * D), p["ada_b"].reshape(2 * DEPTH, 3 * D))
    m2 = _ada_linear(c, p["kv_ada_w"][None], p["kv_ada_b"][None])[0]
    n_all = nb + c_sample.shape[0]

    def split(m, parts, lo, hi):
        return [m[lo:hi, i * D:(i + 1) * D] for i in range(parts)]

    out = {}
    for name, lo, hi in (("prompt", 0, nb), ("sample", nb, n_all)):
        out[name] = {"ada": [[split(m3[2 * l + i], 3, lo, hi) for i in range(2)] for l in range(DEPTH)],
                     "kv": split(m2, 2, lo, hi)}
    return out


def _trunk_prompt(x, mods, w):
    bsz, t, _ = x.shape
    per_b = lambda m: m[:, None, :]
    sh, sc, gt = (per_b(m) for m in mods["ada"][0][0])
    zeros = jnp.zeros((bsz, 1, D), F32)
    r, lw, k, v, a, b, g, last = _rwkv_pre(x, sc, sh, zeros, w, seq_mode=True, tm=256)
    y, zf = _wkv_chunked(r, lw, k, v, a, b)
    x = _rwkv_post(y, r, k, v, g, x, gt, w["post_vec"], w["rwkv_w_o"], pairs=True, tm=256)
    zf = zf.reshape(bsz, PAIRS, 2, HD, 2, HD)
    wkv = jnp.stack([zf[:, :, 0, :, 0, :], zf[:, :, 1, :, 1, :]], axis=2)
    wkv = wkv.reshape(bsz, HEADS, HD, HD).transpose(0, 1, 3, 2)
    sh2, sc2, gt2 = (per_b(m) for m in mods["ada"][0][1])
    x = _moe_sorted(x, sc2, sh2, gt2, w["ln"][0][1], w, 0)

    sh, sc, gt = (per_b(m) for m in mods["ada"][1][0])
    ksh, ksc = (per_b(m) for m in mods["kv"])
    rot = _rot_tables(jnp.arange(t, dtype=I32))
    dils = tuple(dil for _, dil in DILATED_GROUPS)
    k_new, v_new, qs, kbs, vbs = _attn_pre(x, sc, sh, ksc, ksh, rot, w["w_q"], w["w_kv"], per_b_mod=True,
                                           tm=256, dils=dils)
    os, lses = zip(*[_attn_band(qs[g], kbs[g], vbs[g], dil) for g, dil in enumerate(dils)])
    x = _attn_post(os, lses, x, gt, w["ln"][1][0], w["w_o_attn"], per_b_mod=True, tm=256, dils=dils)
    sh2, sc2, gt2 = (per_b(m) for m in mods["ada"][1][1])
    x = _moe_sorted(x, sc2, sh2, gt2, w["ln"][1][1], w, 1)
    return x, wkv[None], last.reshape(1, bsz, D), k_new, v_new


def _trunk_sample(x, mods, state_wkv, state_shift, cache_k, cache_v, w):
    bsz = x.shape[0]
    xs = x.reshape(1, bsz, D)
    row = lambda m: m[None]
    sh, sc, gt = (row(m) for m in mods["ada"][0][0])
    r, lw, k, v, a, b, g, hm = _rwkv_pre(xs, sc, sh, state_shift[0][None], w, seq_mode=False, tm=bsz)
    s_new, y = _wkv_step(state_wkv[0].transpose(1, 2, 3, 0), *(t[0].T for t in (r, lw, k, v, a, b)))
    s_new = s_new.transpose(3, 0, 1, 2)
    xs = _rwkv_post(y.T[None], r, k, v, g, xs, gt, w["post_vec"], w["rwkv_w_o"], pairs=False, tm=bsz)
    sh2, sc2, gt2 = (row(m) for m in mods["ada"][0][1])
    xs = _moe_dense(xs, sc2, sh2, gt2, w["ln"][0][1], w, 0)

    sh, sc, gt = (row(m) for m in mods["ada"][1][0])
    ksh, ksc = (row(m) for m in mods["kv"])
    rot = _rot_tables(jnp.full((bsz,), PAST_LEN, I32))
    k_new, v_new, q = _attn_pre(xs, sc, sh, ksc, ksh, rot, w["w_q"], w["w_kv"], per_b_mod=False, tm=bsz)
    o, lse = _attn_step(q[0], k_new[0], v_new[0], cache_k, cache_v)
    xs = _attn_post([o[i][None] for i in range(3)], [lse[i][None] for i in range(3)], xs, gt, w["ln"][1][0],
                    w["w_o_attn"], per_b_mod=False, tm=bsz)
    sh2, sc2, gt2 = (row(m) for m in mods["ada"][1][1])
    xs = _moe_dense(xs, sc2, sh2, gt2, w["ln"][1][1], w, 1)
    return (xs.reshape(bsz, 1, D), s_new[None], hm, k_new.reshape(bsz, 1, KV_HEADS, HD),
            v_new.reshape(bsz, 1, KV_HEADS, HD))


def kernel(x_prompt, x_sample, state_wkv, state_shift, cache_k, cache_v, c_prompt, c_sample, ada_w, ada_b, ln_g, ln_b, rwkv_mu, rwkv_w_rkv, rwkv_w0, rwkv_w1, rwkv_w2, rwkv_a0, rwkv_a1, rwkv_a2, rwkv_g1, rwkv_g2, rwkv_k_k, rwkv_k_a, rwkv_r_k, rwkv_lnx_g, rwkv_lnx_b, rwkv_w_o, w_q, w_kv, kv_ada_w, kv_ada_b, w_o_attn, router_w, router_b, moe_w_in, moe_w_out):
    p = {"ada_w": ada_w, "ada_b": ada_b, "ln_g": ln_g, "ln_b": ln_b, "rwkv_mu": rwkv_mu, "rwkv_w_rkv": rwkv_w_rkv,
         "rwkv_w0": rwkv_w0, "rwkv_w1": rwkv_w1, "rwkv_w2": rwkv_w2, "rwkv_a0": rwkv_a0, "rwkv_a1": rwkv_a1,
         "rwkv_a2": rwkv_a2, "rwkv_g1": rwkv_g1, "rwkv_g2": rwkv_g2, "rwkv_k_k": rwkv_k_k, "rwkv_k_a": rwkv_k_a,
         "rwkv_r_k": rwkv_r_k, "rwkv_lnx_g": rwkv_lnx_g, "rwkv_lnx_b": rwkv_lnx_b, "rwkv_w_o": rwkv_w_o,
         "w_q": w_q, "w_kv": w_kv, "kv_ada_w": kv_ada_w, "kv_ada_b": kv_ada_b, "w_o_attn": w_o_attn,
         "router_w": router_w, "router_b": router_b, "moe_w_in": moe_w_in, "moe_w_out": moe_w_out}
    w = _prep_weights(p)
    mods = _modulations(c_prompt, c_sample, p)
    bp, tp, _ = x_prompt.shape
    y_p, wkv_p, shift_p, k_p, v_p = _trunk_prompt(x_prompt, mods["prompt"], w)
    keep = min(PAST_LEN, tp)
    k_p = k_p[:, tp - keep:].reshape(bp, keep, KV_HEADS, HD)
    v_p = v_p[:, tp - keep:].reshape(bp, keep, KV_HEADS, HD)
    y_s, wkv_s, shift_s, k_s, v_s = _trunk_sample(x_sample, mods["sample"], state_wkv, state_shift, cache_k,
                                                  cache_v, w)
    return (y_p, y_s, wkv_p, shift_p, k_p, v_p, wkv_s, shift_s, k_s, v_s)
```

```python
import functools
import math

import jax
import jax.numpy as jnp
from jax import lax
from jax.experimental import pallas as pl
from jax.experimental.pallas import tpu as pltpu

F32 = jnp.float32
BF16 = jnp.bfloat16
I32 = jnp.int32
U32 = jnp.uint32

D = 1024
HEADS = 16
HD = 64
LANES = 128
PAIRS = D // LANES
CHUNK = 64
WKV_GROUP = 16
N_EXPERTS = 32
EXPERTS_PER_GROUP = 8
N_EXPERT_GROUPS = 4
D_EXPERT = 512
MOE_BLK = 256
KV_HEADS = 4
REP = 4
KVW = KV_HEADS * HD
DILATED_GROUPS = ((128, 1), (512, 4), (2048, 16))
ATT_WIN = 128
PAST_LEN = 2048
ROT_DIM = 16
ROPE_THETA = 500000.0
DEPTH = 2
DN_ALPHA = (2 * DEPTH) ** 0.25
LN_EPS = 1e-5
GN_EPS = 64e-5
VMEM_LIMIT = 56 * 1024 * 1024


def _cp(sem):
    return pltpu.CompilerParams(dimension_semantics=sem, vmem_limit_bytes=VMEM_LIMIT)


def _bdot(a, b):
    return jnp.dot(a.astype(BF16), b.astype(BF16), preferred_element_type=F32)


def _bdot_nt(a, b):
    return lax.dot_general(a.astype(BF16), b.astype(BF16), (((1,), (1,)), ((), ())),
                           preferred_element_type=F32)


def _bdot_tn(a, b):
    return lax.dot_general(a.astype(BF16), b.astype(BF16), (((0,), (0,)), ((), ())),
                           preferred_element_type=F32)


def _split_dot(x, m):
    hi = x.astype(BF16)
    lo = (x - hi.astype(F32)).astype(BF16)
    return jnp.dot(jnp.concatenate([hi, lo], axis=1), jnp.concatenate([m, m], axis=0),
                   preferred_element_type=F32)


def _pack_pairs(x):
    k = x.shape[1] // 2
    hi = lax.bitcast_convert_type(x[:, :k].astype(BF16).astype(F32), U32)
    lo = lax.bitcast_convert_type(x[:, k:].astype(BF16).astype(F32), U32)
    return hi | (lo >> 16)


def _unpack_pairs(w):
    hi = lax.bitcast_convert_type(w & jnp.uint32(0xFFFF0000), F32)
    lo = lax.bitcast_convert_type(w << 16, F32)
    return jnp.concatenate([hi, lo], axis=1).astype(BF16)


def _head_ones():
    r = lax.broadcasted_iota(I32, (LANES, LANES), 0) // HD
    c = lax.broadcasted_iota(I32, (LANES, LANES), 1) // HD
    return (r == c).astype(BF16)


def _segsum(x, ones_bd):
    cols = [_split_dot(x[:, c * LANES:(c + 1) * LANES], ones_bd) for c in range(x.shape[1] // LANES)]
    return cols[0] if len(cols) == 1 else jnp.concatenate(cols, axis=1)


def _layer_norm(x, g, b):
    mu = jnp.mean(x, axis=-1, keepdims=True)
    xc = x - mu
    var = jnp.mean(xc * xc, axis=-1, keepdims=True)
    return xc * lax.rsqrt(var + LN_EPS) * g + b


def _sigmoid(x):
    return 1.0 / (1.0 + jnp.exp(-x))


def _to_pairs(ref, val):
    for p in range(PAIRS):
        ref[p] = val[:, p * LANES:(p + 1) * LANES]


def _from_pairs(ref):
    return jnp.concatenate([ref[p] for p in range(PAIRS)], axis=1)


def _ada_kernel(c_ref, w_ref, b_ref, o_ref):
    c = c_ref[...]
    o_ref[...] = _bdot(c * _sigmoid(c), w_ref[...]) + b_ref[...]


def _ada_linear(c, w, b, tn=512):
    s, _, n = w.shape
    m = c.shape[0]
    return pl.pallas_call(
        _ada_kernel,
        grid=(s, n // tn),
        in_specs=[pl.BlockSpec((m, D), lambda i, j: (0, 0)),
                  pl.BlockSpec((None, D, tn), lambda i, j: (i, 0, j)),
                  pl.BlockSpec((None, 1, tn), lambda i, j: (i, 0, j))],
        out_specs=pl.BlockSpec((None, m, tn), lambda i, j: (i, 0, j)),
        out_shape=jax.ShapeDtypeStruct((s, m, n), F32),
        compiler_params=_cp(("arbitrary", "arbitrary")),
        name="ada_linear",
    )(c, w, b.reshape(s, 1, n))


def _rwkv_pre_kernel(x_ref, sc_ref, sh_ref, prev_ref, mu_ref, wrkv_ref, w1_ref, w2_ref, a1_ref, a2_ref,
                     g1_ref, g2_ref, vec_ref,
                     r_ref, lw_ref, k_ref, v_ref, a_ref, b_ref, g_ref, hm_ref, carry_ref, *, seq_mode, tm):
    x = x_ref[...]
    hm = x * (1.0 + sc_ref[...]) + sh_ref[...]
    if seq_mode:
        @pl.when(pl.program_id(1) == 0)
        def _():
            carry_ref[...] = prev_ref[...]
        row = lax.broadcasted_iota(I32, hm.shape, 0)
        hprev = jnp.where(row == 0, carry_ref[...], pltpu.roll(hm, 1, axis=0))
        carry_ref[...] = hm[tm - 1:tm, :]
        hm_ref[...] = hm[tm - 1:tm, :]
    else:
        hprev = prev_ref[...]
        hm_ref[...] = hm
    xx = hprev - hm

    def mix(i):
        return hm + xx * mu_ref[i:i + 1, :]

    w0, a0, k_k, k_a = (vec_ref[i:i + 1, :] for i in range(4))
    r = _bdot(mix(0), wrkv_ref[0])
    k = _bdot(mix(2), wrkv_ref[1])
    v = _bdot(mix(3), wrkv_ref[2])
    wl = w0 + _bdot(jnp.tanh(_bdot(mix(1), w1_ref[...])), w2_ref[...])
    lw = -math.exp(-0.5) * _sigmoid(wl)
    a_lr = _sigmoid(a0 + _bdot(_bdot(mix(4), a1_ref[...]), a2_ref[...]))
    g = _bdot(_sigmoid(_bdot(mix(5), g1_ref[...])), g2_ref[...])
    kk = k * k_k
    kk = kk * jnp.minimum(lax.rsqrt(_segsum(kk * kk, _head_ones())), 1e12)
    kmod = k * (1.0 + (a_lr - 1.0) * k_a)
    outs = ((r_ref, r), (lw_ref, lw), (k_ref, kmod), (v_ref, v), (a_ref, -kk), (b_ref, kk * a_lr), (g_ref, g))
    for ref, val in outs:
        if seq_mode:
            _to_pairs(ref, val)
        else:
            ref[...] = val


def _rwkv_pre(x, sc, sh, prev, w, *, seq_mode, tm):
    bsz, t, _ = x.shape
    grid = (bsz, t // tm)
    row = pl.BlockSpec((None, tm, D), lambda b, i: (b, i, 0))
    per_b = pl.BlockSpec((None, 1, D), lambda b, i: (b, 0, 0))
    mod = per_b if seq_mode else row

    def const(shape):
        return pl.BlockSpec(shape, lambda b, i: (0,) * len(shape))

    if seq_mode:
        out_big = pl.BlockSpec((None, PAIRS, tm, LANES), lambda b, i: (b, 0, i, 0))
        big_shape = jax.ShapeDtypeStruct((bsz, PAIRS, t, LANES), F32)
        hm_spec, hm_shape = per_b, jax.ShapeDtypeStruct((bsz, 1, D), F32)
    else:
        out_big, big_shape = row, jax.ShapeDtypeStruct((bsz, t, D), F32)
        hm_spec, hm_shape = row, jax.ShapeDtypeStruct((bsz, t, D), F32)
    return pl.pallas_call(
        functools.partial(_rwkv_pre_kernel, seq_mode=seq_mode, tm=tm),
        grid=grid,
        in_specs=[row, mod, mod, mod, const((6, D)), const((3, D, D)), const((D, LANES)), const((LANES, D)),
                  const((D, LANES)), const((LANES, D)), const((D, 2 * LANES)), const((2 * LANES, D)),
                  const((4, D))],
        out_specs=[out_big] * 7 + [hm_spec],
        out_shape=[big_shape] * 7 + [hm_shape],
        scratch_shapes=[pltpu.VMEM((1, D), F32)],
        compiler_params=_cp(("arbitrary", "arbitrary")),
        name="rwkv_pre",
    )(x, sc, sh, prev, w["mu"], w["w_rkv"], w["w1"], w["w2"], w["a1"], w["a2"], w["g1"], w["g2"], w["vec"])


def _wkv_chunk_kernel(r_ref, lw_ref, k_ref, v_ref, a_ref, b_ref, y_ref, zf_ref, z_ref, *, n_pairs, group):
    c = pl.program_id(0)

    @pl.when(c == 0)
    def _():
        z_ref[...] = jnp.zeros_like(z_ref)

    L = CHUNK
    row_l = lax.broadcasted_iota(I32, (L, L), 0)
    col_l = lax.broadcasted_iota(I32, (L, L), 1)
    tri_incl = (row_l >= col_l).astype(BF16)
    lane = lax.broadcasted_iota(I32, (L, LANES), 1)
    head0 = lane < HD
    row = lax.broadcasted_iota(I32, (LANES, LANES), 0)
    col = lax.broadcasted_iota(I32, (LANES, LANES), 1)
    strict = row > col
    incl = row >= col
    eye = (row == col).astype(F32)

    def expand(x):
        return jnp.concatenate([jnp.where(head0, x, 0.0), jnp.where(head0, 0.0, x)], axis=0)

    def group_body(gi, carry):
        ids = [gi * group + j for j in range(group)]
        bp = [(i // PAIRS, i % PAIRS) for i in ids]
        G = range(group)
        r = [r_ref[b_i, p_i] for b_i, p_i in bp]
        lw = [lw_ref[b_i, p_i] for b_i, p_i in bp]
        k = [k_ref[b_i, p_i] for b_i, p_i in bp]
        v = [v_ref[b_i, p_i] for b_i, p_i in bp]
        a = [a_ref[b_i, p_i] for b_i, p_i in bp]
        b = [b_ref[b_i, p_i] for b_i, p_i in bp]
        z0 = [z_ref[i] for i in ids]
        cum = [_split_dot_left(tri_incl, lw[j]) for j in G]
        cum_l = [cum[j][L - 1:L, :] for j in G]
        inv = [jnp.exp(-cum[j]) for j in G]
        tail = [jnp.exp(cum_l[j] - cum[j]) for j in G]
        a_e = [expand(a[j] * jnp.exp(cum[j] - lw[j])) for j in G]
        r_e = [expand(r[j] * jnp.exp(cum[j])) for j in G]
        b_e = [expand(b[j] * inv[j]) for j in G]
        k_e = [expand(k[j] * inv[j]) for j in G]
        bd_e = [expand(b[j] * tail[j]) for j in G]
        kd_e = [expand(k[j] * tail[j]) for j in G]
        v_e = [expand(v[j]) for j in G]
        gm = [_bdot_nt(jnp.concatenate([a_e[j], r_e[j]], axis=0), jnp.concatenate([b_e[j], k_e[j]], axis=0))
              for j in G]
        m_ab = [jnp.where(strict, gm[j][:LANES, :LANES], 0.0) for j in G]
        m_ak = [jnp.where(strict, gm[j][:LANES, LANES:], 0.0) for j in G]
        m_rb = [jnp.where(incl, gm[j][LANES:, :LANES], 0.0) for j in G]
        m_rk = [jnp.where(incl, gm[j][LANES:, LANES:], 0.0) for j in G]
        tinv = [eye + m_ab[j] for j in G]
        pw = [_bdot(m_ab[j], m_ab[j]) for j in G]
        for _ in range(4):
            both = [_bdot(pw[j], jnp.concatenate([pw[j], tinv[j]], axis=1)) for j in G]
            pw = [both[j][:, :LANES] for j in G]
            tinv = [tinv[j] + both[j][:, LANES:] for j in G]
        tinv = [tinv[j] + _bdot(pw[j], tinv[j]) for j in G]
        rhs = [_bdot(a_e[j], z0[j]) + _bdot(m_ak[j], v_e[j]) for j in G]
        u = [_bdot(tinv[j], rhs[j]) for j in G]
        zuv = [jnp.concatenate([z0[j], u[j], v_e[j]], axis=0) for j in G]
        decay = [eye * jnp.exp(cum_l[j]) for j in G]
        for j in G:
            z_ref[ids[j]] = _bdot_tn(jnp.concatenate([decay[j], bd_e[j], kd_e[j]], axis=0), zuv[j])
            y_e = _bdot(jnp.concatenate([r_e[j], m_rb[j], m_rk[j]], axis=1), zuv[j])
            y_ref[bp[j][0], bp[j][1]] = y_e[:L] + y_e[L:]
        return carry

    lax.fori_loop(0, n_pairs // group, group_body, 0)

    @pl.when(c == pl.num_programs(0) - 1)
    def _():
        zf_ref[...] = z_ref[...]


def _split_dot_left(m, x):
    hi = x.astype(BF16)
    lo = (x - hi.astype(F32)).astype(BF16)
    return (jnp.dot(m, hi, preferred_element_type=F32) + jnp.dot(m, lo, preferred_element_type=F32))


def _wkv_chunked(r, lw, k, v, a, b):
    bsz, _, t, _ = r.shape
    n_pairs = bsz * PAIRS
    blk = pl.BlockSpec((bsz, PAIRS, CHUNK, LANES), lambda c: (0, 0, c, 0))
    return pl.pallas_call(
        functools.partial(_wkv_chunk_kernel, n_pairs=n_pairs, group=WKV_GROUP),
        grid=(t // CHUNK,),
        in_specs=[blk] * 6,
        out_specs=[blk, pl.BlockSpec((n_pairs, LANES, LANES), lambda c: (0, 0, 0))],
        out_shape=[jax.ShapeDtypeStruct(r.shape, F32), jax.ShapeDtypeStruct((n_pairs, LANES, LANES), F32)],
        scratch_shapes=[pltpu.VMEM((n_pairs, LANES, LANES), F32)],
        compiler_params=_cp(("arbitrary",)),
        name="wkv_chunked",
    )(r, lw, k, v, a, b)


def _wkv_step_kernel(s_ref, r_ref, lw_ref, k_ref, v_ref, a_ref, b_ref, so_ref, y_ref):
    w = jnp.exp(lw_ref[...])
    a, b, k, r = a_ref[...], b_ref[...], k_ref[...], r_ref[...]

    def body(i, c):
        s = s_ref[i]
        sa = jnp.sum(s * a, axis=0, keepdims=True)
        s_new = s * w + sa * b + v_ref[pl.ds(i, 1), :] * k
        so_ref[i] = s_new
        y_ref[pl.ds(i, 1), :] = jnp.sum(s_new * r, axis=0, keepdims=True)
        return c

    lax.fori_loop(0, HD, body, 0, unroll=4)


def _wkv_step(state_t, r, lw, k, v, a, b):
    bsz = state_t.shape[-1]
    big = pl.BlockSpec((None, HD, HD, bsz), lambda h: (h, 0, 0, 0))
    vec = pl.BlockSpec((HD, bsz), lambda h: (h, 0))
    return pl.pallas_call(
        _wkv_step_kernel,
        grid=(HEADS,),
        in_specs=[big] + [vec] * 6,
        out_specs=[big, vec],
        out_shape=[jax.ShapeDtypeStruct(state_t.shape, F32), jax.ShapeDtypeStruct((D, bsz), F32)],
        compiler_params=_cp(("arbitrary",)),
        name="wkv_step",
    )(state_t, r, lw, k, v, a, b)


def _rwkv_post_kernel(y_ref, r_ref, k_ref, v_ref, g_ref, x_ref, gt_ref, vec_ref, wo_ref, o_ref, *, pairs):
    load = _from_pairs if pairs else (lambda ref: ref[...])
    y, r, k, v, g = (load(ref) for ref in (y_ref, r_ref, k_ref, v_ref, g_ref))
    r_k, lnx_g, lnx_b, ln_g, ln_b = (vec_ref[i:i + 1, :] for i in range(5))
    ones_bd = _head_ones()
    ym = _segsum(y, ones_bd) * (1.0 / HD)
    yc = y - ym
    yv = _segsum(yc * yc, ones_bd) * (1.0 / HD)
    yn = yc * lax.rsqrt(yv + GN_EPS) * lnx_g + lnx_b
    bonus = _segsum(r * k * r_k, ones_bd) * v
    mix = _bdot((yn + bonus) * g, wo_ref[...])
    o_ref[...] = _layer_norm(DN_ALPHA * x_ref[...] + gt_ref[...] * mix, ln_g, ln_b)


def _rwkv_post(y, r, k, v, g, x, gt, vec, w_o, *, pairs, tm):
    bsz, t, _ = x.shape
    row = pl.BlockSpec((None, tm, D), lambda b, i: (b, i, 0))
    big = pl.BlockSpec((None, PAIRS, tm, LANES), lambda b, i: (b, 0, i, 0)) if pairs else row
    mod = pl.BlockSpec((None, 1, D), lambda b, i: (b, 0, 0)) if pairs else row
    return pl.pallas_call(
        functools.partial(_rwkv_post_kernel, pairs=pairs),
        grid=(bsz, t // tm),
        in_specs=[big] * 5 + [row, mod, pl.BlockSpec((5, D), lambda b, i: (0, 0)),
                              pl.BlockSpec((D, D), lambda b, i: (0, 0))],
        out_specs=row,
        out_shape=jax.ShapeDtypeStruct(x.shape, F32),
        compiler_params=_cp(("arbitrary", "arbitrary")),
        name="rwkv_post",
    )(y, r, k, v, g, x, gt, vec, w_o)


def _top2(v):
    io = lax.broadcasted_iota(I32, v.shape, 0)
    m1 = jnp.max(v, axis=0, keepdims=True)
    i1 = jnp.min(jnp.where(v == m1, io, EXPERTS_PER_GROUP), axis=0, keepdims=True)
    v2 = jnp.where(io == i1, -jnp.inf, v)
    m2 = jnp.max(v2, axis=0, keepdims=True)
    i2 = jnp.min(jnp.where(v2 == m2, io, EXPERTS_PER_GROUP), axis=0, keepdims=True)
    return m1 + m2, i1, i2


def _moe_route_kernel(x_ref, sc_ref, sh_ref, rw_ref, rb_ref, h_ref, idx_ref, gate_ref, gmat_ref, cnt_ref,
                      *, tm):
    h = x_ref[...] * (1.0 + sc_ref[...]) + sh_ref[...]
    h_ref[...] = h
    aff = _sigmoid(_bdot_nt(rw_ref[...], h))
    sel = aff + rb_ref[...]
    best = gi = i1 = i2 = None
    for g in range(N_EXPERT_GROUPS):
        sc, j1, j2 = _top2(sel[g * EXPERTS_PER_GROUP:(g + 1) * EXPERTS_PER_GROUP, :])
        if g == 0:
            best, gi, i1, i2 = sc, jnp.zeros_like(j1), j1, j2
        else:
            upd = sc > best
            best = jnp.where(upd, sc, best)
            gi = jnp.where(upd, g, gi)
            i1 = jnp.where(upd, j1, i1)
            i2 = jnp.where(upd, j2, i2)
    e0 = gi * EXPERTS_PER_GROUP + i1
    e1 = gi * EXPERTS_PER_GROUP + i2
    io = lax.broadcasted_iota(I32, (N_EXPERTS, tm), 0)
    oh0 = io == e0
    oh1 = io == e1
    a0 = jnp.sum(jnp.where(oh0, aff, 0.0), axis=0, keepdims=True)
    a1 = jnp.sum(jnp.where(oh1, aff, 0.0), axis=0, keepdims=True)
    den = a0 + a1
    g0 = a0 / den
    g1 = a1 / den
    oh = jnp.where(oh0 | oh1, 1.0, 0.0)
    tr = lax.broadcasted_iota(I32, (tm, tm), 0)
    tc = lax.broadcasted_iota(I32, (tm, tm), 1)
    before = jnp.dot(oh.astype(BF16), (tr < tc).astype(BF16), preferred_element_type=F32)
    rank0 = jnp.sum(jnp.where(oh0, before, 0.0), axis=0, keepdims=True)
    rank1 = jnp.sum(jnp.where(oh1, before, 0.0), axis=0, keepdims=True)
    zi = jnp.zeros((4, tm), I32)
    idx_ref[...] = jnp.concatenate([e0, e1, rank0.astype(I32), rank1.astype(I32), zi], axis=0)
    gate_ref[...] = jnp.concatenate([g0, g1, jnp.zeros((6, tm), F32)], axis=0)
    gmat_ref[...] = jnp.where(oh0, g0, 0.0) + jnp.where(oh1, g1, 0.0)
    cnt_ref[...] = jnp.broadcast_to(jnp.sum(oh, axis=1, keepdims=True), cnt_ref.shape)


def _moe_route(x, sc, sh, router_w_t, router_b, *, per_b_mod, rows_out, tm):
    bsz, t, _ = x.shape
    n = bsz * t
    nt = t // tm
    row = pl.BlockSpec((None, tm, D), lambda b, i: (b, i, 0))
    mod = pl.BlockSpec((None, 1, D), lambda b, i: (b, 0, 0)) if per_b_mod else row
    if rows_out:
        h_spec = pl.BlockSpec((tm, D), lambda b, i: (b * nt + i, 0))
        h_shape = jax.ShapeDtypeStruct((n, D), F32)
    else:
        h_spec, h_shape = row, jax.ShapeDtypeStruct(x.shape, F32)
    tok = lambda rows: pl.BlockSpec((rows, tm), lambda b, i: (0, b * nt + i))
    return pl.pallas_call(
        functools.partial(_moe_route_kernel, tm=tm),
        grid=(bsz, nt),
        in_specs=[row, mod, mod, pl.BlockSpec((N_EXPERTS, D), lambda b, i: (0, 0)),
                  pl.BlockSpec((N_EXPERTS, 1), lambda b, i: (0, 0))],
        out_specs=[h_spec, tok(8), tok(8), tok(N_EXPERTS),
                   pl.BlockSpec((None, N_EXPERTS, LANES), lambda b, i: (b * nt + i, 0, 0))],
        out_shape=[h_shape, jax.ShapeDtypeStruct((8, n), I32), jax.ShapeDtypeStruct((8, n), F32),
                   jax.ShapeDtypeStruct((N_EXPERTS, n), F32),
                   jax.ShapeDtypeStruct((bsz * nt, N_EXPERTS, LANES), F32)],
        compiler_params=_cp(("arbitrary", "arbitrary")),
        name="moe_route",
    )(x, sc, sh, router_w_t, router_b)


def _moe_dispatch_kernel(c8_ref, lo_ref, rs_ref, zs_ref, h_ref, idx_ref, tab_ref, xs_ref,
                         buf, zbuf, sem, *, tm):
    step = pl.program_id(0)
    last = pl.num_programs(0) - 1
    s = step % 2

    def runs(tile, sl, start):
        for e in range(N_EXPERTS):
            j = tile * N_EXPERTS + e

            @pl.when(c8_ref[j] > 0)
            def _():
                size = pl.multiple_of(c8_ref[j], 8)
                cp = pltpu.make_async_copy(buf.at[sl, pl.ds(pl.multiple_of(lo_ref[j], 8), size), :],
                                           xs_ref.at[pl.ds(pl.multiple_of(rs_ref[j], 8), size), :], sem.at[sl])
                if start:
                    cp.start()
                else:
                    cp.wait()

    @pl.when(step == 0)
    def _():
        zbuf[...] = jnp.zeros_like(zbuf)

        def zero_block(first_slot):
            return pltpu.make_async_copy(zbuf, xs_ref.at[pl.ds(pl.multiple_of(first_slot, 8), MOE_BLK), :],
                                         sem.at[2])

        fresh = [zs_ref[e] != zs_ref[e - 1] for e in range(1, N_EXPERTS)]
        zero_block(zs_ref[0]).start()
        for e, f in enumerate(fresh, start=1):
            pl.when(f)(lambda e=e: zero_block(zs_ref[e]).start())
        zero_block(zs_ref[0]).wait()
        for e, f in enumerate(fresh, start=1):
            pl.when(f)(lambda e=e: zero_block(zs_ref[e]).wait())

        used, n_blocks = zs_ref[N_EXPERTS], xs_ref.shape[0] // MOE_BLK

        def tail_start(j, c):
            zero_block(j * MOE_BLK).start()
            return c

        def tail_wait(j, c):
            zero_block(j * MOE_BLK).wait()
            return c

        lax.fori_loop(used, n_blocks, tail_start, 0)
        lax.fori_loop(used, n_blocks, tail_wait, 0)

    @pl.when(step >= 2)
    def _():
        runs(step - 2, s, start=False)

    e0, e1, l0, l1 = (idx_ref[r:r + 1, :] for r in range(4))
    expert = lax.broadcasted_iota(I32, (N_EXPERTS, tm), 0)

    def lookup(col, e):
        return jnp.sum(jnp.where(expert == e, col, 0.0), axis=0, keepdims=True).astype(I32)

    lo_col = tab_ref[:, 0:1].astype(F32)
    pos0 = lookup(lo_col, e0) + l0
    pos1 = lookup(lo_col, e1) + l1
    row = lax.broadcasted_iota(I32, (buf.shape[1], tm), 0)
    perm = ((row == pos0) | (row == pos1)).astype(BF16)
    buf[s] = _pack_pairs(jnp.dot(perm, h_ref[...].astype(BF16), preferred_element_type=F32))
    runs(step, s, start=True)

    @pl.when(step == last)
    def _():
        @pl.when(step >= 1)
        def _():
            runs(step - 1, 1 - s, start=False)
        runs(step, s, start=False)


def _moe_dispatch(h_rows, idx, c8, loff, run_start, zero_start, n_slots, tm):
    n = h_rows.shape[0]
    tiles = n // tm
    buf_rows = 2 * tm + N_EXPERTS * 8
    tab = jnp.pad(loff[:, :, None], ((0, 0), (0, 0), (0, LANES - 1)))
    flat = lambda a: a.reshape(-1).astype(I32)
    return pl.pallas_call(
        functools.partial(_moe_dispatch_kernel, tm=tm),
        grid_spec=pltpu.PrefetchScalarGridSpec(
            num_scalar_prefetch=4, grid=(tiles,),
            in_specs=[pl.BlockSpec((tm, D), lambda i, *_: (i, 0)),
                      pl.BlockSpec((8, tm), lambda i, *_: (0, i)),
                      pl.BlockSpec((None, N_EXPERTS, LANES), lambda i, *_: (i, 0, 0))],
            out_specs=pl.BlockSpec(memory_space=pl.ANY),
            scratch_shapes=[pltpu.VMEM((2, buf_rows, D // 2), U32), pltpu.VMEM((MOE_BLK, D // 2), U32),
                            pltpu.SemaphoreType.DMA((3,))]),
        out_shape=jax.ShapeDtypeStruct((n_slots, D // 2), U32),
        compiler_params=_cp(("arbitrary",)),
        name="moe_dispatch",
    )(flat(c8), flat(loff), flat(run_start), zero_start, h_rows, idx, tab)


def _moe_expert_kernel(be_ref, used_ref, nxt_ref, xs_ref, win_hbm, wout_hbm, ys_ref,
                       win_f, wout_f, win_bf, wout_bf, run_ref, sem, *, layer):
    i = pl.program_id(0)
    used = used_ref[0]

    def weight_copies(e, s):
        return (pltpu.make_async_copy(win_hbm.at[layer, e], win_f.at[s], sem.at[s, 0]),
                pltpu.make_async_copy(wout_hbm.at[layer, e], wout_f.at[s], sem.at[s, 1]))

    @pl.when(i == 0)
    def _():
        run_ref[0] = 0
        for cp in weight_copies(be_ref[0], 0):
            cp.start()

    fresh = (i == 0) | (be_ref[i] != be_ref[jnp.maximum(i - 1, 0)])

    @pl.when(fresh & (i < used))
    def _():
        run = run_ref[0] + jnp.where(i > 0, 1, 0)
        run_ref[0] = run
        s = run % 2
        for cp in weight_copies(be_ref[i], s):
            cp.wait()
        win_bf[...] = win_f[s].astype(BF16)
        wout_bf[...] = wout_f[s].astype(BF16)

        @pl.when(nxt_ref[i] >= 0)
        def _():
            for cp in weight_copies(nxt_ref[i], 1 - s):
                cp.start()

    @pl.when(i < used)
    def _():
        hmid = jnp.dot(_unpack_pairs(xs_ref[...]), win_bf[...], preferred_element_type=F32)
        gt = hmid[:, :D_EXPERT]
        up = hmid[:, D_EXPERT:]
        act = gt * _sigmoid(gt) * up
        ys_ref[...] = _pack_pairs(jnp.dot(act.astype(BF16), wout_bf[...], preferred_element_type=F32))

    @pl.when(i >= used_ref[0])
    def _():
        ys_ref[...] = jnp.zeros_like(ys_ref)


def _moe_experts_sorted(xs, blk_exp, n_used, w_in, w_out, layer):
    n_slots = xs.shape[0]
    n_blocks = n_slots // MOE_BLK
    blk = jnp.arange(n_blocks, dtype=I32)
    later = (blk[None, :] > blk[:, None]) & (blk_exp[None, :] != blk_exp[:, None]) & (blk[None, :] < n_used[0])
    first_later = jnp.min(jnp.where(later, blk[None, :], n_blocks), axis=1)
    nxt = jnp.where(first_later < n_blocks, blk_exp[jnp.minimum(first_later, n_blocks - 1)], -1).astype(I32)
    x_map = lambda i, be, u, nx: (jnp.minimum(i, u[0] - 1), 0)
    return pl.pallas_call(
        functools.partial(_moe_expert_kernel, layer=layer),
        grid_spec=pltpu.PrefetchScalarGridSpec(
            num_scalar_prefetch=3, grid=(n_blocks,),
            in_specs=[pl.BlockSpec((MOE_BLK, D // 2), x_map),
                      pl.BlockSpec(memory_space=pl.ANY), pl.BlockSpec(memory_space=pl.ANY)],
            out_specs=pl.BlockSpec((MOE_BLK, D // 2), lambda i, be, u, nx: (i, 0)),
            scratch_shapes=[pltpu.VMEM((2, D, 2 * D_EXPERT), F32), pltpu.VMEM((2, D_EXPERT, D), F32),
                            pltpu.VMEM((D, 2 * D_EXPERT), BF16), pltpu.VMEM((D_EXPERT, D), BF16),
                            pltpu.SMEM((1,), I32), pltpu.SemaphoreType.DMA((2, 2))]),
        out_shape=jax.ShapeDtypeStruct((n_slots, D // 2), U32),
        compiler_params=_cp(("arbitrary",)),
        name="moe_experts_sorted",
    )(blk_exp, n_used, nxt, xs, w_in, w_out)


def _moe_combine_kernel(c8_ref, lo_ref, rs_ref, ys_ref, idx_ref, tab_ref, gate_ref, x_ref, gt_ref, vec_ref,
                        o_ref, buf, sem, *, tm):
    step = pl.program_id(0)
    s = step % 2

    def runs(tile, sl, start):
        for e in range(N_EXPERTS):
            j = tile * N_EXPERTS + e

            @pl.when(c8_ref[j] > 0)
            def _():
                size = pl.multiple_of(c8_ref[j], 8)
                cp = pltpu.make_async_copy(ys_ref.at[pl.ds(pl.multiple_of(rs_ref[j], 8), size), :],
                                           buf.at[sl, pl.ds(pl.multiple_of(lo_ref[j], 8), size), :], sem.at[sl])
                if start:
                    cp.start()
                else:
                    cp.wait()

    @pl.when(step == 0)
    def _():
        buf[...] = jnp.zeros_like(buf)
        runs(0, 0, start=True)

    @pl.when(step + 1 < pl.num_programs(0))
    def _():
        runs(step + 1, 1 - s, start=True)

    runs(step, s, start=False)
    e0, e1, l0, l1 = (idx_ref[r:r + 1, :] for r in range(4))
    expert = lax.broadcasted_iota(I32, (N_EXPERTS, tm), 0)
    lo_col = tab_ref[:, 0:1].astype(F32)

    def position(e, l):
        return jnp.sum(jnp.where(expert == e, lo_col, 0.0), axis=0, keepdims=True).astype(I32) + l

    row = lax.broadcasted_iota(I32, (buf.shape[1], tm), 0)
    y_sorted = _unpack_pairs(buf[s])
    ya = _bdot_tn((row == position(e0, l0)).astype(BF16), y_sorted)
    yb = _bdot_tn((row == position(e1, l1)).astype(BF16), y_sorted)
    ff = ya * gate_ref[:, 0:1] + yb * gate_ref[:, 1:2]
    o_ref[...] = _layer_norm(DN_ALPHA * x_ref[...] + gt_ref[...] * ff, vec_ref[0:1, :], vec_ref[1:2, :])


def _moe_combine(ys, idx, c8, loff, run_start, gates, x, gt, vec, tm):
    bsz, t, _ = x.shape
    nt = t // tm
    buf_rows = 2 * tm + N_EXPERTS * 8
    tab = jnp.pad(loff[:, :, None], ((0, 0), (0, 0), (0, LANES - 1)))
    flat = lambda a: a.reshape(-1).astype(I32)
    return pl.pallas_call(
        functools.partial(_moe_combine_kernel, tm=tm),
        grid_spec=pltpu.PrefetchScalarGridSpec(
            num_scalar_prefetch=3, grid=(bsz * nt,),
            in_specs=[pl.BlockSpec(memory_space=pl.ANY),
                      pl.BlockSpec((8, tm), lambda i, *_: (0, i)),
                      pl.BlockSpec((None, N_EXPERTS, LANES), lambda i, *_: (i, 0, 0)),
                      pl.BlockSpec((tm, 2), lambda i, *_: (i, 0)),
                      pl.BlockSpec((None, tm, D), lambda i, *_: (i // nt, i % nt, 0)),
                      pl.BlockSpec((None, 1, D), lambda i, *_: (i // nt, 0, 0)),
                      pl.BlockSpec((2, D), lambda i, *_: (0, 0))],
            out_specs=pl.BlockSpec((None, tm, D), lambda i, *_: (i // nt, i % nt, 0)),
            scratch_shapes=[pltpu.VMEM((2, buf_rows, D // 2), U32), pltpu.SemaphoreType.DMA((2,))]),
        out_shape=jax.ShapeDtypeStruct(x.shape, F32),
        compiler_params=_cp(("arbitrary",)),
        name="moe_combine",
    )(flat(c8), flat(loff), flat(run_start), ys, idx, tab, gates, x, gt, vec)


def _moe_sorted(x, sc, sh, gt, vec, w, layer, tm=256):
    bsz, t, _ = x.shape
    n = bsz * t
    tiles = n // tm
    n_blocks = -(-(2 * n + tiles * N_EXPERTS * 7 + N_EXPERTS * (MOE_BLK - 1)) // MOE_BLK)
    h_rows, idx, gate, _, cnt = _moe_route(x, sc, sh, w["router_w_t"], w["router_b"], per_b_mod=True,
                                           rows_out=True, tm=tm)
    c8 = (cnt[:, :, 0].astype(I32) + 7) // 8 * 8
    padded = (jnp.sum(c8, axis=0) + MOE_BLK - 1) // MOE_BLK * MOE_BLK
    pad_end = jnp.cumsum(padded)
    run_start = (pad_end - padded)[None, :] + jnp.cumsum(c8, axis=0) - c8
    loff = jnp.cumsum(c8, axis=1) - c8
    blk_start = jnp.arange(n_blocks, dtype=I32) * MOE_BLK
    blk_exp = jnp.minimum(jnp.sum((pad_end[None, :] <= blk_start[:, None]).astype(I32), axis=1), N_EXPERTS - 1)
    n_used = (pad_end[-1:] // MOE_BLK).astype(I32)
    zero_start = jnp.concatenate([jnp.maximum(pad_end - MOE_BLK, 0), n_used])
    xs = _moe_dispatch(h_rows, idx, c8, loff, run_start, zero_start, n_blocks * MOE_BLK, tm)
    ys = _moe_experts_sorted(xs, blk_exp, n_used, w["moe_w_in"], w["moe_w_out"], layer)
    return _moe_combine(ys, idx, c8, loff, run_start, gate[:2].T, x, gt, vec, tm)


def _moe_dense_kernel(h_ref, gm_ref, win_ref, wout_ref, x_ref, gt_ref, vec_ref, o_ref, acc_ref):
    e = pl.program_id(0)

    @pl.when(e == 0)
    def _():
        acc_ref[...] = jnp.zeros_like(acc_ref)

    lane = lax.broadcasted_iota(I32, gm_ref.shape, 1)
    gcol = jnp.sum(jnp.where(lane == e, gm_ref[...], 0.0), axis=1, keepdims=True)
    hmid = _bdot(h_ref[...], win_ref[...])
    gt = hmid[:, :D_EXPERT]
    up = hmid[:, D_EXPERT:]
    y = _bdot(gt * _sigmoid(gt) * up, wout_ref[...])
    acc_ref[...] = acc_ref[...] + jnp.where(gcol != 0.0, gcol * y, 0.0)

    @pl.when(e == N_EXPERTS - 1)
    def _():
        o_ref[...] = _layer_norm(DN_ALPHA * x_ref[...] + gt_ref[...] * acc_ref[...], vec_ref[0:1, :],
                                 vec_ref[1:2, :])


def _moe_dense(x, sc, sh, gt, vec, w, layer):
    _, m, _ = x.shape
    h, _, _, gmat, _ = _moe_route(x, sc, sh, w["router_w_t"], w["router_b"], per_b_mod=False, rows_out=False,
                                  tm=m)
    full = pl.BlockSpec((m, D), lambda e: (0, 0))
    out = pl.pallas_call(
        _moe_dense_kernel,
        grid=(N_EXPERTS,),
        in_specs=[full, pl.BlockSpec((m, N_EXPERTS), lambda e: (0, 0)),
                  pl.BlockSpec((None, None, D, 2 * D_EXPERT), lambda e: (layer, e, 0, 0)),
                  pl.BlockSpec((None, None, D_EXPERT, D), lambda e: (layer, e, 0, 0)),
                  full, full, pl.BlockSpec((2, D), lambda e: (0, 0))],
        out_specs=full,
        out_shape=jax.ShapeDtypeStruct((m, D), F32),
        scratch_shapes=[pltpu.VMEM((m, D), F32)],
        compiler_params=_cp(("arbitrary",)),
        name="moe_dense",
    )(h[0], gmat.T, w["moe_w_in"], w["moe_w_out"], x[0], gt[0], vec)
    return out[None]


def _rotary(x, cos_t, sa_t, sb_t):
    reps = x.shape[1] // LANES
    tile = lambda t: t if reps == 1 else jnp.concatenate([t] * reps, axis=1)
    n = x.shape[1]
    half = ROT_DIM // 2
    return x * tile(cos_t) + pltpu.roll(x, n - half, axis=1) * tile(sa_t) + pltpu.roll(x, half, axis=1) * tile(sb_t)


def _class_rows(c, rows, dil):
    return pl.ds(c, rows, stride=dil) if dil > 1 else pl.ds(0, rows)


def _class_major_perm(tm, dil, inverse=False):
    cm_row = lax.broadcasted_iota(I32, (tm, tm), 1 if inverse else 0)
    tok_row = lax.broadcasted_iota(I32, (tm, tm), 0 if inverse else 1)
    rows = tm // dil
    return ((cm_row // rows == tok_row % dil) & (cm_row % rows == tok_row // dil)).astype(BF16)


def _attn_pre_kernel(x_ref, sc_ref, sh_ref, ksc_ref, ksh_ref, rot_ref, wq_ref, wkv_ref, *refs, dils, tm):
    x = x_ref[...]
    cos_t, sa_t, sb_t = rot_ref[0], rot_ref[1], rot_ref[2]
    q = _rotary(_bdot(x * (1.0 + sc_ref[...]) + sh_ref[...], wq_ref[...]), cos_t, sa_t, sb_t) * HD ** -0.5
    kv = _bdot(x * (1.0 + ksc_ref[...]) + ksh_ref[...], wkv_ref[...])
    k = _rotary(kv[:, :KVW], cos_t, sa_t, sb_t)
    v = kv[:, KVW:]
    k_ref, v_ref = refs[0], refs[1]
    k_ref[...] = k
    v_ref[...] = v
    if dils is None:
        refs[2][...] = q
        return
    ng = len(dils)
    q_refs, kb_refs, vb_refs = refs[2:2 + ng], refs[2 + ng:2 + 2 * ng], refs[2 + 2 * ng:2 + 3 * ng]
    qb, kb, vb = q.astype(BF16), k.astype(BF16), v.astype(BF16)
    for g, dil in enumerate(dils):
        rows = tm // dil
        blk = jnp.concatenate([qb[:, g * D:(g + 1) * D], kb, vb], axis=1)
        if dil > 1:
            blk = jnp.dot(_class_major_perm(tm, dil), blk, preferred_element_type=F32).astype(BF16)
        for c in range(dil):
            rs = slice(c * rows, (c + 1) * rows)
            q_refs[g][c] = blk[rs, :D]
            kb_refs[g][c] = blk[rs, D:D + KVW]
            vb_refs[g][c] = blk[rs, D + KVW:]


def _attn_pre(x, sc, sh, ksc, ksh, rot, w_q, w_kv, *, per_b_mod, tm, dils=None):
    bsz, t, _ = x.shape
    row = lambda wd: pl.BlockSpec((None, tm, wd), lambda b, i: (b, i, 0))
    mod = pl.BlockSpec((None, 1, D), lambda b, i: (b, 0, 0)) if per_b_mod else row(D)
    nq = w_q.shape[1]
    out_specs = [row(KVW), row(KVW)]
    out_shape = [jax.ShapeDtypeStruct((bsz, t, KVW), F32)] * 2
    scratch = []
    if dils is None:
        out_specs.append(row(nq))
        out_shape.append(jax.ShapeDtypeStruct((bsz, t, nq), F32))
    else:
        for wd in (D, KVW, KVW):
            for dil in dils:
                out_specs.append(pl.BlockSpec((None, dil, tm // dil, wd), lambda b, i: (b, 0, i, 0)))
                out_shape.append(jax.ShapeDtypeStruct((bsz, dil, t // dil, wd), BF16))
    outs = pl.pallas_call(
        functools.partial(_attn_pre_kernel, dils=dils, tm=tm),
        grid=(bsz, t // tm),
        in_specs=[row(D), mod, mod, mod, mod, pl.BlockSpec((3, tm, LANES), lambda b, i: (0, i, 0)),
                  pl.BlockSpec((D, nq), lambda b, i: (0, 0)), pl.BlockSpec((D, 2 * KVW), lambda b, i: (0, 0))],
        out_specs=out_specs,
        out_shape=out_shape,
        scratch_shapes=scratch,
        compiler_params=_cp(("arbitrary", "arbitrary")),
        name="attn_pre",
    )(x, sc, sh, ksc, ksh, rot, w_q, w_kv)
    if dils is None:
        return outs
    ng = len(dils)
    return outs[0], outs[1], outs[2:2 + ng], outs[2 + ng:2 + 2 * ng], outs[2 + 2 * ng:]


def _attn_band_kernel(q_ref, ka_ref, kb_ref, va_ref, vb_ref, o_ref, lse_ref):
    i = pl.program_id(2)
    bq = ATT_WIN
    kcat = jnp.concatenate([ka_ref[...], kb_ref[...]], axis=0)
    vcat = jnp.concatenate([va_ref[...], vb_ref[...]], axis=0)
    qs = jnp.concatenate([q_ref[:, rep * KVW:(rep + 1) * KVW] for rep in range(REP)], axis=0)
    qrow = lax.broadcasted_iota(I32, (REP * bq, 2 * bq), 0) % bq
    kcol = lax.broadcasted_iota(I32, (REP * bq, 2 * bq), 1)
    valid = (kcol >= qrow) & (kcol <= qrow + bq) & ((i > 0) | (kcol >= bq))
    lane = lax.broadcasted_iota(I32, (REP * bq, KVW), 1) // HD
    kv_lane = lax.broadcasted_iota(I32, (2 * bq, KVW), 1) // HD
    lse_lane = lax.broadcasted_iota(I32, (bq, LANES), 1)
    lse = jnp.zeros((bq, LANES), F32)
    linv = jnp.ones((REP * bq, KVW), F32)
    ps = []
    for h in range(KV_HEADS):
        kh = jnp.where(kv_lane == h, kcat, jnp.zeros_like(kcat))
        s = lax.dot_general(qs, kh, (((1,), (1,)), ((), ())), preferred_element_type=F32)
        s = jnp.where(valid, s, -jnp.inf)
        m = jnp.max(s, axis=-1, keepdims=True)
        p = jnp.exp(s - m)
        l = jnp.sum(p, axis=-1, keepdims=True)
        ps.append(p.astype(BF16))
        linv = jnp.where(lane == h, 1.0 / l, linv)
        lse_h = m + jnp.log(l)
        for rep in range(REP):
            lse = jnp.where(lse_lane == rep * KV_HEADS + h, lse_h[rep * bq:(rep + 1) * bq], lse)
    vstack = jnp.concatenate([jnp.where(kv_lane == h, vcat, jnp.zeros_like(vcat)) for h in range(KV_HEADS)],
                             axis=0)
    o = (jnp.dot(jnp.concatenate(ps, axis=1), vstack, preferred_element_type=F32) * linv).astype(BF16)
    for rep in range(REP):
        o_ref[:, rep * KVW:(rep + 1) * KVW] = o[rep * bq:(rep + 1) * bq]
    lse_ref[...] = lse


def _attn_band(q, kb, vb, dil):
    bsz, _, tc, _ = q.shape
    nb = tc // ATT_WIN
    blk = lambda wd: pl.BlockSpec((None, None, ATT_WIN, wd), lambda b, c, i: (b, c, i, 0))
    prev = pl.BlockSpec((None, None, ATT_WIN, KVW), lambda b, c, i: (b, c, jnp.maximum(i - 1, 0), 0))
    return pl.pallas_call(
        _attn_band_kernel,
        grid=(bsz, dil, nb),
        in_specs=[blk(D), prev, blk(KVW), prev, blk(KVW)],
        out_specs=[blk(D), blk(LANES)],
        out_shape=[jax.ShapeDtypeStruct((bsz, dil, tc, D), BF16), jax.ShapeDtypeStruct((bsz, dil, tc, LANES), F32)],
        compiler_params=_cp(("arbitrary", "arbitrary", "arbitrary")),
        name=f"attn_band_d{dil}",
    )(q, kb, kb, vb, vb)


def _attn_step_kernel(q_ref, kn_ref, vn_ref, kt_ref, vt_ref, o_ref, lse_ref, *, wbuf):
    nq = len(DILATED_GROUPS) * REP
    row = lax.broadcasted_iota(I32, (nq, wbuf), 0)
    pos = lax.broadcasted_iota(I32, (nq, wbuf), 1)
    valid = None
    for g, (_, dil) in enumerate(DILATED_GROUPS):
        ok = (row // REP == g) & (pos >= wbuf - ATT_WIN * dil) & (pos % dil == 0)
        valid = ok if valid is None else valid | ok
    rnd = lambda t: t.astype(BF16).astype(F32)
    lse_lane = lax.broadcasted_iota(I32, (nq, LANES), 1)
    lse = jnp.zeros((nq, LANES), F32)
    for h in range(KV_HEADS):
        hs = slice(h * HD, (h + 1) * HD)
        qh = q_ref[:, hs]
        s = jnp.where(valid, _bdot(qh, kt_ref[h]), -1e30)
        s_n = jnp.sum(rnd(qh) * rnd(kn_ref[:, hs]), axis=-1, keepdims=True)
        m = jnp.maximum(jnp.max(s, axis=-1, keepdims=True), s_n)
        p = jnp.exp(s - m)
        p_n = jnp.exp(s_n - m)
        l = jnp.sum(p, axis=-1, keepdims=True) + p_n
        o_ref[:, hs] = (_bdot_nt(p, vt_ref[h]) + rnd(p_n) * rnd(vn_ref[:, hs])) / l
        lse = jnp.where(lse_lane == h, m + jnp.log(l), lse)
    lse_ref[...] = lse


def _attn_step(q, k_new, v_new, cache_k, cache_v):
    bsz = q.shape[0]
    wbuf = cache_k.shape[1]
    ng = len(DILATED_GROUPS)
    assert all(wbuf % dil == 0 and wbuf >= ATT_WIN * dil for _, dil in DILATED_GROUPS)
    nq = ng * REP
    cache_spec = pl.BlockSpec((None, KV_HEADS, HD, wbuf), lambda b: (b, 0, 0, 0))
    o, lse = pl.pallas_call(
        functools.partial(_attn_step_kernel, wbuf=wbuf),
        grid=(bsz,),
        in_specs=[pl.BlockSpec((None, nq, KVW), lambda b: (b, 0, 0)),
                  pl.BlockSpec((None, 1, KVW), lambda b: (b, 0, 0)),
                  pl.BlockSpec((None, 1, KVW), lambda b: (b, 0, 0)), cache_spec, cache_spec],
        out_specs=[pl.BlockSpec((None, nq, KVW), lambda b: (b, 0, 0)),
                   pl.BlockSpec((None, nq, LANES), lambda b: (b, 0, 0))],
        out_shape=[jax.ShapeDtypeStruct((bsz, nq, KVW), F32), jax.ShapeDtypeStruct((bsz, nq, LANES), F32)],
        compiler_params=_cp(("arbitrary",)),
        name="attn_step",
    )(q.reshape(bsz, nq, KVW), k_new.reshape(bsz, 1, KVW), v_new.reshape(bsz, 1, KVW),
      cache_k.transpose(0, 2, 3, 1), cache_v.transpose(0, 2, 3, 1))
    o = o.reshape(bsz, ng, D).transpose(1, 0, 2)
    lse = lse[:, :, :KV_HEADS].reshape(bsz, ng, REP * KV_HEADS).transpose(1, 0, 2)
    return o, jnp.pad(lse, ((0, 0), (0, 0), (0, LANES - REP * KV_HEADS)))


def _attn_post_kernel(o0_ref, o1_ref, o2_ref, l0_ref, l1_ref, l2_ref, x_ref, gt_ref, vec_ref, wo_ref, out_ref,
                      *scratch, dils, tm):
    o_refs = [o0_ref, o1_ref, o2_ref]
    l_refs = [l0_ref, l1_ref, l2_ref]
    if dils is not None:
        for g, dil in enumerate(dils):
            if dil == 1:
                o_refs[g], l_refs[g] = o_refs[g].at[0], l_refs[g].at[0]
                continue
            sl = scratch[g]
            for c in range(dil):
                sl[_class_rows(c, tm // dil, dil), :] = l_refs[g][c]
            l_refs[g] = sl
            o_cm = jnp.concatenate([o_refs[g][c] for c in range(dil)], axis=0)
            o_refs[g] = jnp.dot(_class_major_perm(tm, dil, inverse=True), o_cm, preferred_element_type=F32)
    o0, o1, o2 = (r if isinstance(r, jax.Array) else r[...] for r in o_refs)
    l0, l1, l2 = (r[...] for r in l_refs)
    m = jnp.maximum(jnp.maximum(l0, l1), l2)
    e0, e1, e2 = jnp.exp(l0 - m), jnp.exp(l1 - m), jnp.exp(l2 - m)
    den = e0 + e1 + e2
    r = lax.broadcasted_iota(I32, (LANES, D), 0)
    c = lax.broadcasted_iota(I32, (LANES, D), 1) // HD
    spread = (r == c).astype(BF16)
    mix = (_split_dot(e0 / den, spread) * o0 + _split_dot(e1 / den, spread) * o1
           + _split_dot(e2 / den, spread) * o2)
    out = _bdot(mix, wo_ref[...])
    out_ref[...] = _layer_norm(DN_ALPHA * x_ref[...] + gt_ref[...] * out, vec_ref[0:1, :], vec_ref[1:2, :])


def _attn_post(os, lses, x, gt, vec, w_o, *, per_b_mod, tm, dils=None):
    bsz, t, _ = x.shape
    row = lambda wd: pl.BlockSpec((None, tm, wd), lambda b, i: (b, i, 0))
    mod = pl.BlockSpec((None, 1, D), lambda b, i: (b, 0, 0)) if per_b_mod else row(D)
    if dils is None:
        o_specs, l_specs, scratch = [row(D)] * 3, [row(LANES)] * 3, []
    else:
        cls = lambda dil, wd: pl.BlockSpec((None, dil, tm // dil, wd), lambda b, i: (b, 0, i, 0))
        o_specs = [cls(dil, D) for dil in dils]
        l_specs = [cls(dil, LANES) for dil in dils]
        scratch = [pltpu.VMEM((tm, LANES), F32) for _ in dils]
    return pl.pallas_call(
        functools.partial(_attn_post_kernel, dils=dils, tm=tm),
        grid=(bsz, t // tm),
        in_specs=o_specs + l_specs + [row(D), mod, pl.BlockSpec((2, D), lambda b, i: (0, 0)),
                                      pl.BlockSpec((D, D), lambda b, i: (0, 0))],
        out_specs=row(D),
        out_shape=jax.ShapeDtypeStruct(x.shape, F32),
        scratch_shapes=scratch,
        compiler_params=_cp(("arbitrary", "arbitrary")),
        name="attn_post",
    )(*os, *lses, x, gt, vec, w_o)


def _rot_tables(pos):
    half = ROT_DIM // 2
    inv = ROPE_THETA ** (-jnp.arange(half, dtype=F32) * 2.0 / ROT_DIM)
    ang = pos.astype(F32)[:, None] * inv[None, :]
    cos, sin = jnp.cos(ang), jnp.sin(ang)
    t = pos.shape[0]
    one = jnp.ones((t, HD - ROT_DIM), F32)
    zero = jnp.zeros((t, HD - ROT_DIM), F32)
    zh = jnp.zeros((t, half), F32)
    cos_t = jnp.concatenate([cos, cos, one], axis=1)
    sa_t = jnp.concatenate([-sin, zh, zero], axis=1)
    sb_t = jnp.concatenate([zh, sin, zero], axis=1)
    return jnp.stack([jnp.tile(x, (1, 2)) for x in (cos_t, sa_t, sb_t)])


def _prep_weights(p):
    pad_c = lambda m, n: jnp.pad(m, ((0, 0), (0, n - m.shape[1])))
    pad_r = lambda m, n: jnp.pad(m, ((0, n - m.shape[0]), (0, 0)))
    w = {}
    w["mu"] = p["rwkv_mu"][0]
    w["w_rkv"] = p["rwkv_w_rkv"][0].astype(BF16)
    w["w1"] = pad_c(p["rwkv_w1"][0], LANES).astype(BF16)
    w["w2"] = pad_r(p["rwkv_w2"][0], LANES).astype(BF16)
    w["a1"] = pad_c(p["rwkv_a1"][0], LANES).astype(BF16)
    w["a2"] = pad_r(p["rwkv_a2"][0], LANES).astype(BF16)
    w["g1"] = pad_c(p["rwkv_g1"][0], 2 * LANES).astype(BF16)
    w["g2"] = pad_r(p["rwkv_g2"][0], 2 * LANES).astype(BF16)
    w["vec"] = jnp.stack([p["rwkv_w0"][0], p["rwkv_a0"][0], p["rwkv_k_k"][0], p["rwkv_k_a"][0]])
    w["post_vec"] = jnp.stack([p["rwkv_r_k"][0].reshape(D), p["rwkv_lnx_g"][0], p["rwkv_lnx_b"][0],
                               p["ln_g"][0, 0], p["ln_b"][0, 0]])
    w["rwkv_w_o"] = p["rwkv_w_o"][0].astype(BF16)
    wq = p["w_q"][0].reshape(D, 3, KV_HEADS, REP, HD).transpose(0, 1, 3, 2, 4).reshape(D, 3 * D)
    w["w_q"] = wq.astype(BF16)
    w["w_kv"] = p["w_kv"].astype(BF16)
    wo = p["w_o_attn"][0].reshape(KV_HEADS, REP, HD, D).transpose(1, 0, 2, 3).reshape(D, D)
    w["w_o_attn"] = wo.astype(BF16)
    w["router_w_t"] = p["router_w"].T.astype(BF16)
    w["router_b"] = p["router_b"].reshape(N_EXPERTS, 1)
    w["moe_w_in"] = p["moe_w_in"]
    w["moe_w_out"] = p["moe_w_out"]
    w["ln"] = [[jnp.stack([p["ln_g"][l, i], p["ln_b"][l, i]]) for i in range(2)] for l in range(DEPTH)]
    return w


def _modulations(c_prompt, c_sample, p):
    nb = c_prompt.shape[0]
    c = jnp.concatenate([c_prompt, c_sample], axis=0)
    pad = (-c.shape[0]) % 8
    c = jnp.pad(c, ((0, pad), (0, 0)))
    m3 = _ada_linear(c, p["ada_w"].reshape(2 * DEPTH, D, 3 * D), p["ada_b"].reshape(2 * DEPTH, 3 * D))
    m2 = _ada_linear(c, p["kv_ada_w"][None], p["kv_ada_b"][None])[0]
    n_all = nb + c_sample.shape[0]

    def split(m, parts, lo, hi):
        return [m[lo:hi, i * D:(i + 1) * D] for i in range(parts)]

    out = {}
    for name, lo, hi in (("prompt", 0, nb), ("sample", nb, n_all)):
        out[name] = {"ada": [[split(m3[2 * l + i], 3, lo, hi) for i in range(2)] for l in range(DEPTH)],
                     "kv": split(m2, 2, lo, hi)}
    return out


def _trunk_prompt(x, mods, w):
    bsz, t, _ = x.shape
    per_b = lambda m: m[:, None, :]
    sh, sc, gt = (per_b(m) for m in mods["ada"][0][0])
    zeros = jnp.zeros((bsz, 1, D), F32)
    r, lw, k, v, a, b, g, last = _rwkv_pre(x, sc, sh, zeros, w, seq_mode=True, tm=256)
    y, zf = _wkv_chunked(r, lw, k, v, a, b)
    x = _rwkv_post(y, r, k, v, g, x, gt, w["post_vec"], w["rwkv_w_o"], pairs=True, tm=256)
    zf = zf.reshape(bsz, PAIRS, 2, HD, 2, HD)
    wkv = jnp.stack([zf[:, :, 0, :, 0, :], zf[:, :, 1, :, 1, :]], axis=2)
    wkv = wkv.reshape(bsz, HEADS, HD, HD).transpose(0, 1, 3, 2)
    sh2, sc2, gt2 = (per_b(m) for m in mods["ada"][0][1])
    x = _moe_sorted(x, sc2, sh2, gt2, w["ln"][0][1], w, 0)

    sh, sc, gt = (per_b(m) for m in mods["ada"][1][0])
    ksh, ksc = (per_b(m) for m in mods["kv"])
    rot = _rot_tables(jnp.arange(t, dtype=I32))
    dils = tuple(dil for _, dil in DILATED_GROUPS)
    k_new, v_new, qs, kbs, vbs = _attn_pre(x, sc, sh, ksc, ksh, rot, w["w_q"], w["w_kv"], per_b_mod=True,
                                           tm=256, dils=dils)
    os, lses = zip(*[_attn_band(qs[g], kbs[g], vbs[g], dil) for g, dil in enumerate(dils)])
    x = _attn_post(os, lses, x, gt, w["ln"][1][0], w["w_o_attn"], per_b_mod=True, tm=256, dils=dils)
    sh2, sc2, gt2 = (per_b(m) for m in mods["ada"][1][1])
    x = _moe_sorted(x, sc2, sh2, gt2, w["ln"][1][1], w, 1)
    return x, wkv[None], last.reshape(1, bsz, D), k_new, v_new


def _trunk_sample(x, mods, state_wkv, state_shift, cache_k, cache_v, w):
    bsz = x.shape[0]
    xs = x.reshape(1, bsz, D)
    row = lambda m: m[None]
    sh, sc, gt = (row(m) for m in mods["ada"][0][0])
    r, lw, k, v, a, b, g, hm = _rwkv_pre(xs, sc, sh, state_shift[0][None], w, seq_mode=False, tm=bsz)
    s_new, y = _wkv_step(state_wkv[0].transpose(1, 2, 3, 0), *(t[0].T for t in (r, lw, k, v, a, b)))
    s_new = s_new.transpose(3, 0, 1, 2)
    xs = _rwkv_post(y.T[None], r, k, v, g, xs, gt, w["post_vec"], w["rwkv_w_o"], pairs=False, tm=bsz)
    sh2, sc2, gt2 = (row(m) for m in mods["ada"][0][1])
    xs = _moe_dense(xs, sc2, sh2, gt2, w["ln"][0][1], w, 0)

    sh, sc, gt = (row(m) for m in mods["ada"][1][0])
    ksh, ksc = (row(m) for m in mods["kv"])
    rot = _rot_tables(jnp.full((bsz,), PAST_LEN, I32))
    k_new, v_new, q = _attn_pre(xs, sc, sh, ksc, ksh, rot, w["w_q"], w["w_kv"], per_b_mod=False, tm=bsz)
    o, lse = _attn_step(q[0], k_new[0], v_new[0], cache_k, cache_v)
    xs = _attn_post([o[i][None] for i in range(3)], [lse[i][None] for i in range(3)], xs, gt, w["ln"][1][0],
                    w["w_o_attn"], per_b_mod=False, tm=bsz)
    sh2, sc2, gt2 = (row(m) for m in mods["ada"][1][1])
    xs = _moe_dense(xs, sc2, sh2, gt2, w["ln"][1][1], w, 1)
    return (xs.reshape(bsz, 1, D), s_new[None], hm, k_new.reshape(bsz, 1, KV_HEADS, HD),
            v_new.reshape(bsz, 1, KV_HEADS, HD))


def kernel(x_prompt, x_sample, state_wkv, state_shift, cache_k, cache_v, c_prompt, c_sample, ada_w, ada_b, ln_g, ln_b, rwkv_mu, rwkv_w_rkv, rwkv_w0, rwkv_w1, rwkv_w2, rwkv_a0, rwkv_a1, rwkv_a2, rwkv_g1, rwkv_g2, rwkv_k_k, rwkv_k_a, rwkv_r_k, rwkv_lnx_g, rwkv_lnx_b, rwkv_w_o, w_q, w_kv, kv_ada_w, kv_ada_b, w_o_attn, router_w, router_b, moe_w_in, moe_w_out):
    p = {"ada_w": ada_w, "ada_b": ada_b, "ln_g": ln_g, "ln_b": ln_b, "rwkv_mu": rwkv_mu, "rwkv_w_rkv": rwkv_w_rkv,
         "rwkv_w0": rwkv_w0, "rwkv_w1": rwkv_w1, "rwkv_w2": rwkv_w2, "rwkv_a0": rwkv_a0, "rwkv_a1": rwkv_a1,
         "rwkv_a2": rwkv_a2, "rwkv_g1": rwkv_g1, "rwkv_g2": rwkv_g2, "rwkv_k_k": rwkv_k_k, "rwkv_k_a": rwkv_k_a,
         "rwkv_r_k": rwkv_r_k, "rwkv_lnx_g": rwkv_lnx_g, "rwkv_lnx_b": rwkv_lnx_b, "rwkv_w_o": rwkv_w_o,
         "w_q": w_q, "w_kv": w_kv, "kv_ada_w": kv_ada_w, "kv_ada_b": kv_ada_b, "w_o_attn": w_o_attn,
         "router_w": router_w, "router_b": router_b, "moe_w_in": moe_w_in, "moe_w_out": moe_w_out}
    w = _prep_weights(p)
    mods = _modulations(c_prompt, c_sample, p)
    bp, tp, _ = x_prompt.shape
    y_p, wkv_p, shift_p, k_p, v_p = _trunk_prompt(x_prompt, mods["prompt"], w)
    keep = min(PAST_LEN, tp)
    k_p = k_p[:, tp - keep:].reshape(bp, keep, KV_HEADS, HD)
    v_p = v_p[:, tp - keep:].reshape(bp, keep, KV_HEADS, HD)
    y_s, wkv_s, shift_s, k_s, v_s = _trunk_sample(x_sample, mods["sample"], state_wkv, state_shift, cache_k,
                                                  cache_v, w)
    return (y_p, y_s, wkv_p, shift_p, k_p, v_p, wkv_s, shift_s, k_s, v_s)
```

```python
import functools
import math

import jax
import jax.numpy as jnp
from jax import lax
from jax.experimental import pallas as pl
from jax.experimental.pallas import tpu as pltpu

F32 = jnp.float32
BF16 = jnp.bfloat16
I32 = jnp.int32
U32 = jnp.uint32

D = 1024
HEADS = 16
HD = 64
LANES = 128
PAIRS = D // LANES
CHUNK = 64
WKV_GROUP = 16
N_EXPERTS = 32
EXPERTS_PER_GROUP = 8
N_EXPERT_GROUPS = 4
D_EXPERT = 512
MOE_BLK = 256
KV_HEADS = 4
REP = 4
KVW = KV_HEADS * HD
DILATED_GROUPS = ((128, 1), (512, 4), (2048, 16))
ATT_WIN = 128
PAST_LEN = 2048
ROT_DIM = 16
ROPE_THETA = 500000.0
DEPTH = 2
DN_ALPHA = (2 * DEPTH) ** 0.25
LN_EPS = 1e-5
GN_EPS = 64e-5
VMEM_LIMIT = 56 * 1024 * 1024


def _cp(sem):
    return pltpu.CompilerParams(dimension_semantics=sem, vmem_limit_bytes=VMEM_LIMIT)


def _bdot(a, b):
    return jnp.dot(a.astype(BF16), b.astype(BF16), preferred_element_type=F32)


def _bdot_nt(a, b):
    return lax.dot_general(a.astype(BF16), b.astype(BF16), (((1,), (1,)), ((), ())),
                           preferred_element_type=F32)


def _bdot_tn(a, b):
    return lax.dot_general(a.astype(BF16), b.astype(BF16), (((0,), (0,)), ((), ())),
                           preferred_element_type=F32)


def _split_dot(x, m):
    hi = x.astype(BF16)
    lo = (x - hi.astype(F32)).astype(BF16)
    return jnp.dot(jnp.concatenate([hi, lo], axis=1), jnp.concatenate([m, m], axis=0),
                   preferred_element_type=F32)


def _pack_pairs(x):
    k = x.shape[1] // 2
    hi = lax.bitcast_convert_type(x[:, :k].astype(BF16).astype(F32), U32)
    lo = lax.bitcast_convert_type(x[:, k:].astype(BF16).astype(F32), U32)
    return hi | (lo >> 16)


def _unpack_pairs(w):
    hi = lax.bitcast_convert_type(w & jnp.uint32(0xFFFF0000), F32)
    lo = lax.bitcast_convert_type(w << 16, F32)
    return jnp.concatenate([hi, lo], axis=1).astype(BF16)


def _head_ones():
    r = lax.broadcasted_iota(I32, (LANES, LANES), 0) // HD
    c = lax.broadcasted_iota(I32, (LANES, LANES), 1) // HD
    return (r == c).astype(BF16)


def _segsum(x, ones_bd):
    cols = [_split_dot(x[:, c * LANES:(c + 1) * LANES], ones_bd) for c in range(x.shape[1] // LANES)]
    return cols[0] if len(cols) == 1 else jnp.concatenate(cols, axis=1)


def _layer_norm(x, g, b):
    mu = jnp.mean(x, axis=-1, keepdims=True)
    xc = x - mu
    var = jnp.mean(xc * xc, axis=-1, keepdims=True)
    return xc * lax.rsqrt(var + LN_EPS) * g + b


def _sigmoid(x):
    return 1.0 / (1.0 + jnp.exp(-x))


def _to_pairs(ref, val):
    for p in range(PAIRS):
        ref[p] = val[:, p * LANES:(p + 1) * LANES]


def _from_pairs(ref):
    return jnp.concatenate([ref[p] for p in range(PAIRS)], axis=1)


def _ada_kernel(c_ref, w_ref, b_ref, o_ref):
    c = c_ref[...]
    o_ref[...] = _bdot(c * _sigmoid(c), w_ref[...]) + b_ref[...]


def _ada_linear(c, w, b, tn=512):
    s, _, n = w.shape
    m = c.shape[0]
    return pl.pallas_call(
        _ada_kernel,
        grid=(s, n // tn),
        in_specs=[pl.BlockSpec((m, D), lambda i, j: (0, 0)),
                  pl.BlockSpec((None, D, tn), lambda i, j: (i, 0, j)),
                  pl.BlockSpec((None, 1, tn), lambda i, j: (i, 0, j))],
        out_specs=pl.BlockSpec((None, m, tn), lambda i, j: (i, 0, j)),
        out_shape=jax.ShapeDtypeStruct((s, m, n), F32),
        compiler_params=_cp(("arbitrary", "arbitrary")),
        name="ada_linear",
    )(c, w, b.reshape(s, 1, n))


def _rwkv_pre_kernel(x_ref, sc_ref, sh_ref, prev_ref, mu_ref, wrkv_ref, w1_ref, w2_ref, a1_ref, a2_ref,
                     g1_ref, g2_ref, vec_ref,
                     r_ref, lw_ref, k_ref, v_ref, a_ref, b_ref, g_ref, hm_ref, carry_ref, *, seq_mode, tm):
    x = x_ref[...]
    hm = x * (1.0 + sc_ref[...]) + sh_ref[...]
    if seq_mode:
        @pl.when(pl.program_id(1) == 0)
        def _():
            carry_ref[...] = prev_ref[...]
        row = lax.broadcasted_iota(I32, hm.shape, 0)
        hprev = jnp.where(row == 0, carry_ref[...], pltpu.roll(hm, 1, axis=0))
        carry_ref[...] = hm[tm - 1:tm, :]
        hm_ref[...] = hm[tm - 1:tm, :]
    else:
        hprev = prev_ref[...]
        hm_ref[...] = hm
    xx = hprev - hm

    def mix(i):
        return hm + xx * mu_ref[i:i + 1, :]

    w0, a0, k_k, k_a = (vec_ref[i:i + 1, :] for i in range(4))
    r = _bdot(mix(0), wrkv_ref[0])
    k = _bdot(mix(2), wrkv_ref[1])
    v = _bdot(mix(3), wrkv_ref[2])
    wl = w0 + _bdot(jnp.tanh(_bdot(mix(1), w1_ref[...])), w2_ref[...])
    lw = -math.exp(-0.5) * _sigmoid(wl)
    a_lr = _sigmoid(a0 + _bdot(_bdot(mix(4), a1_ref[...]), a2_ref[...]))
    g = _bdot(_sigmoid(_bdot(mix(5), g1_ref[...])), g2_ref[...])
    kk = k * k_k
    kk = kk * jnp.minimum(lax.rsqrt(_segsum(kk * kk, _head_ones())), 1e12)
    kmod = k * (1.0 + (a_lr - 1.0) * k_a)
    outs = ((r_ref, r), (lw_ref, lw), (k_ref, kmod), (v_ref, v), (a_ref, -kk), (b_ref, kk * a_lr),
            (g_ref, g.astype(BF16)))
    for ref, val in outs:
        if seq_mode:
            _to_pairs(ref, val)
        else:
            ref[...] = val


def _rwkv_pre(x, sc, sh, prev, w, *, seq_mode, tm):
    bsz, t, _ = x.shape
    grid = (bsz, t // tm)
    row = pl.BlockSpec((None, tm, D), lambda b, i: (b, i, 0))
    per_b = pl.BlockSpec((None, 1, D), lambda b, i: (b, 0, 0))
    mod = per_b if seq_mode else row

    def const(shape):
        return pl.BlockSpec(shape, lambda b, i: (0,) * len(shape))

    if seq_mode:
        out_big = pl.BlockSpec((None, PAIRS, tm, LANES), lambda b, i: (b, 0, i, 0))
        big_shape = jax.ShapeDtypeStruct((bsz, PAIRS, t, LANES), F32)
        hm_spec, hm_shape = per_b, jax.ShapeDtypeStruct((bsz, 1, D), F32)
    else:
        out_big, big_shape = row, jax.ShapeDtypeStruct((bsz, t, D), F32)
        hm_spec, hm_shape = row, jax.ShapeDtypeStruct((bsz, t, D), F32)
    return pl.pallas_call(
        functools.partial(_rwkv_pre_kernel, seq_mode=seq_mode, tm=tm),
        grid=grid,
        in_specs=[row, mod, mod, mod, const((6, D)), const((3, D, D)), const((D, LANES)), const((LANES, D)),
                  const((D, LANES)), const((LANES, D)), const((D, 2 * LANES)), const((2 * LANES, D)),
                  const((4, D))],
        out_specs=[out_big] * 7 + [hm_spec],
        out_shape=[big_shape] * 6 + [jax.ShapeDtypeStruct(big_shape.shape, BF16), hm_shape],
        scratch_shapes=[pltpu.VMEM((1, D), F32)],
        compiler_params=_cp(("arbitrary", "arbitrary")),
        name="rwkv_pre",
    )(x, sc, sh, prev, w["mu"], w["w_rkv"], w["w1"], w["w2"], w["a1"], w["a2"], w["g1"], w["g2"], w["vec"])


def _wkv_chunk_kernel(r_ref, lw_ref, k_ref, v_ref, a_ref, b_ref, y_ref, zf_ref, z_ref, *, n_pairs, group):
    c = pl.program_id(0)

    @pl.when(c == 0)
    def _():
        z_ref[...] = jnp.zeros_like(z_ref)

    L = CHUNK
    row_l = lax.broadcasted_iota(I32, (L, L), 0)
    col_l = lax.broadcasted_iota(I32, (L, L), 1)
    tri_incl = (row_l >= col_l).astype(BF16)
    lane = lax.broadcasted_iota(I32, (L, LANES), 1)
    head0 = lane < HD
    row = lax.broadcasted_iota(I32, (LANES, LANES), 0)
    col = lax.broadcasted_iota(I32, (LANES, LANES), 1)
    strict = row > col
    incl = row >= col
    eye = (row == col).astype(F32)

    def expand(x):
        return jnp.concatenate([jnp.where(head0, x, 0.0), jnp.where(head0, 0.0, x)], axis=0)

    def group_body(gi, carry):
        ids = [gi * group + j for j in range(group)]
        bp = [(i // PAIRS, i % PAIRS) for i in ids]
        G = range(group)
        r = [r_ref[b_i, p_i] for b_i, p_i in bp]
        lw = [lw_ref[b_i, p_i] for b_i, p_i in bp]
        k = [k_ref[b_i, p_i] for b_i, p_i in bp]
        v = [v_ref[b_i, p_i] for b_i, p_i in bp]
        a = [a_ref[b_i, p_i] for b_i, p_i in bp]
        b = [b_ref[b_i, p_i] for b_i, p_i in bp]
        z0 = [z_ref[i] for i in ids]
        cum = [_split_dot_left(tri_incl, lw[j]) for j in G]
        cum_l = [cum[j][L - 1:L, :] for j in G]
        inv = [jnp.exp(-cum[j]) for j in G]
        tail = [jnp.exp(cum_l[j] - cum[j]) for j in G]
        a_e = [expand(a[j] * jnp.exp(cum[j] - lw[j])) for j in G]
        r_e = [expand(r[j] * jnp.exp(cum[j])) for j in G]
        b_e = [expand(b[j] * inv[j]) for j in G]
        k_e = [expand(k[j] * inv[j]) for j in G]
        bd_e = [expand(b[j] * tail[j]) for j in G]
        kd_e = [expand(k[j] * tail[j]) for j in G]
        v_e = [expand(v[j]) for j in G]
        gm = [_bdot_nt(jnp.concatenate([a_e[j], r_e[j]], axis=0), jnp.concatenate([b_e[j], k_e[j]], axis=0))
              for j in G]
        m_ab = [jnp.where(strict, gm[j][:LANES, :LANES], 0.0) for j in G]
        m_ak = [jnp.where(strict, gm[j][:LANES, LANES:], 0.0) for j in G]
        m_rb = [jnp.where(incl, gm[j][LANES:, :LANES], 0.0) for j in G]
        m_rk = [jnp.where(incl, gm[j][LANES:, LANES:], 0.0) for j in G]
        tinv = [eye + m_ab[j] for j in G]
        pw = [_bdot(m_ab[j], m_ab[j]) for j in G]
        for _ in range(4):
            both = [_bdot(pw[j], jnp.concatenate([pw[j], tinv[j]], axis=1)) for j in G]
            pw = [both[j][:, :LANES] for j in G]
            tinv = [tinv[j] + both[j][:, LANES:] for j in G]
        tinv = [tinv[j] + _bdot(pw[j], tinv[j]) for j in G]
        rhs = [_bdot(a_e[j], z0[j]) + _bdot(m_ak[j], v_e[j]) for j in G]
        u = [_bdot(tinv[j], rhs[j]) for j in G]
        zuv = [jnp.concatenate([z0[j], u[j], v_e[j]], axis=0) for j in G]
        decay = [eye * jnp.exp(cum_l[j]) for j in G]
        for j in G:
            z_ref[ids[j]] = _bdot_tn(jnp.concatenate([decay[j], bd_e[j], kd_e[j]], axis=0), zuv[j])
            y_e = _bdot(jnp.concatenate([r_e[j], m_rb[j], m_rk[j]], axis=1), zuv[j])
            y_ref[bp[j][0], bp[j][1]] = y_e[:L] + y_e[L:]
        return carry

    lax.fori_loop(0, n_pairs // group, group_body, 0)

    @pl.when(c == pl.num_programs(0) - 1)
    def _():
        zf_ref[...] = z_ref[...]


def _split_dot_left(m, x):
    hi = x.astype(BF16)
    lo = (x - hi.astype(F32)).astype(BF16)
    return (jnp.dot(m, hi, preferred_element_type=F32) + jnp.dot(m, lo, preferred_element_type=F32))


def _wkv_chunked(r, lw, k, v, a, b):
    bsz, _, t, _ = r.shape
    n_pairs = bsz * PAIRS
    blk = pl.BlockSpec((bsz, PAIRS, CHUNK, LANES), lambda c: (0, 0, c, 0))
    return pl.pallas_call(
        functools.partial(_wkv_chunk_kernel, n_pairs=n_pairs, group=WKV_GROUP),
        grid=(t // CHUNK,),
        in_specs=[blk] * 6,
        out_specs=[blk, pl.BlockSpec((n_pairs, LANES, LANES), lambda c: (0, 0, 0))],
        out_shape=[jax.ShapeDtypeStruct(r.shape, F32), jax.ShapeDtypeStruct((n_pairs, LANES, LANES), F32)],
        scratch_shapes=[pltpu.VMEM((n_pairs, LANES, LANES), F32)],
        compiler_params=_cp(("arbitrary",)),
        name="wkv_chunked",
    )(r, lw, k, v, a, b)


def _wkv_step_kernel(s_ref, r_ref, lw_ref, k_ref, v_ref, a_ref, b_ref, so_ref, y_ref):
    w = jnp.exp(lw_ref[...])
    a, b, k, r = a_ref[...], b_ref[...], k_ref[...], r_ref[...]

    def body(i, c):
        s = s_ref[i]
        sa = jnp.sum(s * a, axis=0, keepdims=True)
        s_new = s * w + sa * b + v_ref[pl.ds(i, 1), :] * k
        so_ref[i] = s_new
        y_ref[pl.ds(i, 1), :] = jnp.sum(s_new * r, axis=0, keepdims=True)
        return c

    lax.fori_loop(0, HD, body, 0, unroll=4)


def _wkv_step(state_t, r, lw, k, v, a, b):
    bsz = state_t.shape[-1]
    big = pl.BlockSpec((None, HD, HD, bsz), lambda h: (h, 0, 0, 0))
    vec = pl.BlockSpec((HD, bsz), lambda h: (h, 0))
    return pl.pallas_call(
        _wkv_step_kernel,
        grid=(HEADS,),
        in_specs=[big] + [vec] * 6,
        out_specs=[big, vec],
        out_shape=[jax.ShapeDtypeStruct(state_t.shape, F32), jax.ShapeDtypeStruct((D, bsz), F32)],
        compiler_params=_cp(("arbitrary",)),
        name="wkv_step",
    )(state_t, r, lw, k, v, a, b)


def _rwkv_post_kernel(y_ref, r_ref, k_ref, v_ref, g_ref, x_ref, gt_ref, vec_ref, wo_ref, o_ref, *, pairs):
    load = _from_pairs if pairs else (lambda ref: ref[...])
    y, r, k, v, g = (load(ref) for ref in (y_ref, r_ref, k_ref, v_ref, g_ref))
    r_k, lnx_g, lnx_b, ln_g, ln_b = (vec_ref[i:i + 1, :] for i in range(5))
    ones_bd = _head_ones()
    ym = _segsum(y, ones_bd) * (1.0 / HD)
    yc = y - ym
    yv = _segsum(yc * yc, ones_bd) * (1.0 / HD)
    yn = yc * lax.rsqrt(yv + GN_EPS) * lnx_g + lnx_b
    bonus = _segsum(r * k * r_k, ones_bd) * v
    mix = _bdot((yn + bonus) * g, wo_ref[...])
    o_ref[...] = _layer_norm(DN_ALPHA * x_ref[...] + gt_ref[...] * mix, ln_g, ln_b)


def _rwkv_post(y, r, k, v, g, x, gt, vec, w_o, *, pairs, tm):
    bsz, t, _ = x.shape
    row = pl.BlockSpec((None, tm, D), lambda b, i: (b, i, 0))
    big = pl.BlockSpec((None, PAIRS, tm, LANES), lambda b, i: (b, 0, i, 0)) if pairs else row
    mod = pl.BlockSpec((None, 1, D), lambda b, i: (b, 0, 0)) if pairs else row
    return pl.pallas_call(
        functools.partial(_rwkv_post_kernel, pairs=pairs),
        grid=(bsz, t // tm),
        in_specs=[big] * 5 + [row, mod, pl.BlockSpec((5, D), lambda b, i: (0, 0)),
                              pl.BlockSpec((D, D), lambda b, i: (0, 0))],
        out_specs=row,
        out_shape=jax.ShapeDtypeStruct(x.shape, F32),
        compiler_params=_cp(("arbitrary", "arbitrary")),
        name="rwkv_post",
    )(y, r, k, v, g, x, gt, vec, w_o)


def _top2(v):
    io = lax.broadcasted_iota(I32, v.shape, 0)
    m1 = jnp.max(v, axis=0, keepdims=True)
    i1 = jnp.min(jnp.where(v == m1, io, EXPERTS_PER_GROUP), axis=0, keepdims=True)
    v2 = jnp.where(io == i1, -jnp.inf, v)
    m2 = jnp.max(v2, axis=0, keepdims=True)
    i2 = jnp.min(jnp.where(v2 == m2, io, EXPERTS_PER_GROUP), axis=0, keepdims=True)
    return m1 + m2, i1, i2


def _moe_route_kernel(x_ref, sc_ref, sh_ref, rw_ref, rb_ref, h_ref, idx_ref, gate_ref, gmat_ref, cnt_ref,
                      *, tm):
    h = x_ref[...] * (1.0 + sc_ref[...]) + sh_ref[...]
    h_ref[...] = h.astype(h_ref.dtype)
    aff = _sigmoid(_bdot_nt(rw_ref[...], h))
    sel = aff + rb_ref[...]
    best = gi = i1 = i2 = None
    for g in range(N_EXPERT_GROUPS):
        sc, j1, j2 = _top2(sel[g * EXPERTS_PER_GROUP:(g + 1) * EXPERTS_PER_GROUP, :])
        if g == 0:
            best, gi, i1, i2 = sc, jnp.zeros_like(j1), j1, j2
        else:
            upd = sc > best
            best = jnp.where(upd, sc, best)
            gi = jnp.where(upd, g, gi)
            i1 = jnp.where(upd, j1, i1)
            i2 = jnp.where(upd, j2, i2)
    e0 = gi * EXPERTS_PER_GROUP + i1
    e1 = gi * EXPERTS_PER_GROUP + i2
    io = lax.broadcasted_iota(I32, (N_EXPERTS, tm), 0)
    oh0 = io == e0
    oh1 = io == e1
    a0 = jnp.sum(jnp.where(oh0, aff, 0.0), axis=0, keepdims=True)
    a1 = jnp.sum(jnp.where(oh1, aff, 0.0), axis=0, keepdims=True)
    den = a0 + a1
    g0 = a0 / den
    g1 = a1 / den
    oh = jnp.where(oh0 | oh1, 1.0, 0.0)
    tr = lax.broadcasted_iota(I32, (tm, tm), 0)
    tc = lax.broadcasted_iota(I32, (tm, tm), 1)
    before = jnp.dot(oh.astype(BF16), (tr < tc).astype(BF16), preferred_element_type=F32)
    rank0 = jnp.sum(jnp.where(oh0, before, 0.0), axis=0, keepdims=True)
    rank1 = jnp.sum(jnp.where(oh1, before, 0.0), axis=0, keepdims=True)
    zi = jnp.zeros((4, tm), I32)
    idx_ref[...] = jnp.concatenate([e0, e1, rank0.astype(I32), rank1.astype(I32), zi], axis=0)
    gate_ref[...] = jnp.concatenate([g0, g1, jnp.zeros((6, tm), F32)], axis=0)
    gmat_ref[...] = jnp.where(oh0, g0, 0.0) + jnp.where(oh1, g1, 0.0)
    cnt_ref[...] = jnp.broadcast_to(jnp.sum(oh, axis=1, keepdims=True), cnt_ref.shape)


def _moe_route(x, sc, sh, router_w_t, router_b, *, per_b_mod, rows_out, tm):
    bsz, t, _ = x.shape
    n = bsz * t
    nt = t // tm
    row = pl.BlockSpec((None, tm, D), lambda b, i: (b, i, 0))
    mod = pl.BlockSpec((None, 1, D), lambda b, i: (b, 0, 0)) if per_b_mod else row
    if rows_out:
        h_spec = pl.BlockSpec((tm, D), lambda b, i: (b * nt + i, 0))
        h_shape = jax.ShapeDtypeStruct((n, D), BF16)
    else:
        h_spec, h_shape = row, jax.ShapeDtypeStruct(x.shape, F32)
    tok = lambda rows: pl.BlockSpec((rows, tm), lambda b, i: (0, b * nt + i))
    return pl.pallas_call(
        functools.partial(_moe_route_kernel, tm=tm),
        grid=(bsz, nt),
        in_specs=[row, mod, mod, pl.BlockSpec((N_EXPERTS, D), lambda b, i: (0, 0)),
                  pl.BlockSpec((N_EXPERTS, 1), lambda b, i: (0, 0))],
        out_specs=[h_spec, tok(8), tok(8), tok(N_EXPERTS),
                   pl.BlockSpec((None, N_EXPERTS, LANES), lambda b, i: (b * nt + i, 0, 0))],
        out_shape=[h_shape, jax.ShapeDtypeStruct((8, n), I32), jax.ShapeDtypeStruct((8, n), F32),
                   jax.ShapeDtypeStruct((N_EXPERTS, n), F32),
                   jax.ShapeDtypeStruct((bsz * nt, N_EXPERTS, LANES), F32)],
        compiler_params=_cp(("arbitrary", "arbitrary")),
        name="moe_route",
    )(x, sc, sh, router_w_t, router_b)


def _moe_dispatch_kernel(c8_ref, lo_ref, rs_ref, zs_ref, h_ref, idx_ref, tab_ref, xs_ref,
                         buf, zbuf, sem, *, tm):
    step = pl.program_id(0)
    last = pl.num_programs(0) - 1
    s = step % 2

    def runs(tile, sl, start):
        for e in range(N_EXPERTS):
            j = tile * N_EXPERTS + e

            @pl.when(c8_ref[j] > 0)
            def _():
                size = pl.multiple_of(c8_ref[j], 8)
                cp = pltpu.make_async_copy(buf.at[sl, pl.ds(pl.multiple_of(lo_ref[j], 8), size), :],
                                           xs_ref.at[pl.ds(pl.multiple_of(rs_ref[j], 8), size), :], sem.at[sl])
                if start:
                    cp.start()
                else:
                    cp.wait()

    @pl.when(step == 0)
    def _():
        zbuf[...] = jnp.zeros_like(zbuf)

        def zero_block(first_slot):
            return pltpu.make_async_copy(zbuf, xs_ref.at[pl.ds(pl.multiple_of(first_slot, 8), MOE_BLK), :],
                                         sem.at[2])

        fresh = [zs_ref[e] != zs_ref[e - 1] for e in range(1, N_EXPERTS)]
        zero_block(zs_ref[0]).start()
        for e, f in enumerate(fresh, start=1):
            pl.when(f)(lambda e=e: zero_block(zs_ref[e]).start())
        zero_block(zs_ref[0]).wait()
        for e, f in enumerate(fresh, start=1):
            pl.when(f)(lambda e=e: zero_block(zs_ref[e]).wait())

        used, n_blocks = zs_ref[N_EXPERTS], xs_ref.shape[0] // MOE_BLK

        def tail_start(j, c):
            zero_block(j * MOE_BLK).start()
            return c

        def tail_wait(j, c):
            zero_block(j * MOE_BLK).wait()
            return c

        lax.fori_loop(used, n_blocks, tail_start, 0)
        lax.fori_loop(used, n_blocks, tail_wait, 0)

    @pl.when(step >= 2)
    def _():
        runs(step - 2, s, start=False)

    e0, e1, l0, l1 = (idx_ref[r:r + 1, :] for r in range(4))
    expert = lax.broadcasted_iota(I32, (N_EXPERTS, tm), 0)

    def lookup(col, e):
        return jnp.sum(jnp.where(expert == e, col, 0.0), axis=0, keepdims=True).astype(I32)

    lo_col = tab_ref[:, 0:1].astype(F32)
    pos0 = lookup(lo_col, e0) + l0
    pos1 = lookup(lo_col, e1) + l1
    row = lax.broadcasted_iota(I32, (buf.shape[1], tm), 0)
    perm = ((row == pos0) | (row == pos1)).astype(BF16)
    buf[s] = _pack_pairs(jnp.dot(perm, h_ref[...].astype(BF16), preferred_element_type=F32))
    runs(step, s, start=True)

    @pl.when(step == last)
    def _():
        @pl.when(step >= 1)
        def _():
            runs(step - 1, 1 - s, start=False)
        runs(step, s, start=False)


def _moe_dispatch(h_rows, idx, c8, loff, run_start, zero_start, n_slots, tm):
    n = h_rows.shape[0]
    tiles = n // tm
    buf_rows = 2 * tm + N_EXPERTS * 8
    tab = jnp.pad(loff[:, :, None], ((0, 0), (0, 0), (0, LANES - 1)))
    flat = lambda a: a.reshape(-1).astype(I32)
    return pl.pallas_call(
        functools.partial(_moe_dispatch_kernel, tm=tm),
        grid_spec=pltpu.PrefetchScalarGridSpec(
            num_scalar_prefetch=4, grid=(tiles,),
            in_specs=[pl.BlockSpec((tm, D), lambda i, *_: (i, 0)),
                      pl.BlockSpec((8, tm), lambda i, *_: (0, i)),
                      pl.BlockSpec((None, N_EXPERTS, LANES), lambda i, *_: (i, 0, 0))],
            out_specs=pl.BlockSpec(memory_space=pl.ANY),
            scratch_shapes=[pltpu.VMEM((2, buf_rows, D // 2), U32), pltpu.VMEM((MOE_BLK, D // 2), U32),
                            pltpu.SemaphoreType.DMA((3,))]),
        out_shape=jax.ShapeDtypeStruct((n_slots, D // 2), U32),
        compiler_params=_cp(("arbitrary",)),
        name="moe_dispatch",
    )(flat(c8), flat(loff), flat(run_start), zero_start, h_rows, idx, tab)


def _moe_expert_kernel(be_ref, used_ref, nxt_ref, xs_ref, win_hbm, wout_hbm, ys_ref,
                       win_f, wout_f, win_bf, wout_bf, run_ref, sem, *, layer):
    i = pl.program_id(0)
    used = used_ref[0]

    def weight_copies(e, s):
        return (pltpu.make_async_copy(win_hbm.at[layer, e], win_f.at[s], sem.at[s, 0]),
                pltpu.make_async_copy(wout_hbm.at[layer, e], wout_f.at[s], sem.at[s, 1]))

    @pl.when(i == 0)
    def _():
        run_ref[0] = 0
        for cp in weight_copies(be_ref[0], 0):
            cp.start()

    fresh = (i == 0) | (be_ref[i] != be_ref[jnp.maximum(i - 1, 0)])

    @pl.when(fresh & (i < used))
    def _():
        run = run_ref[0] + jnp.where(i > 0, 1, 0)
        run_ref[0] = run
        s = run % 2
        for cp in weight_copies(be_ref[i], s):
            cp.wait()
        win_bf[...] = win_f[s].astype(BF16)
        wout_bf[...] = wout_f[s].astype(BF16)

        @pl.when(nxt_ref[i] >= 0)
        def _():
            for cp in weight_copies(nxt_ref[i], 1 - s):
                cp.start()

    @pl.when(i < used)
    def _():
        hmid = jnp.dot(_unpack_pairs(xs_ref[...]), win_bf[...], preferred_element_type=F32)
        gt = hmid[:, :D_EXPERT]
        up = hmid[:, D_EXPERT:]
        act = gt * _sigmoid(gt) * up
        ys_ref[...] = _pack_pairs(jnp.dot(act.astype(BF16), wout_bf[...], preferred_element_type=F32))

    @pl.when(i >= used_ref[0])
    def _():
        ys_ref[...] = jnp.zeros_like(ys_ref)


def _moe_experts_sorted(xs, blk_exp, n_used, w_in, w_out, layer):
    n_slots = xs.shape[0]
    n_blocks = n_slots // MOE_BLK
    blk = jnp.arange(n_blocks, dtype=I32)
    later = (blk[None, :] > blk[:, None]) & (blk_exp[None, :] != blk_exp[:, None]) & (blk[None, :] < n_used[0])
    first_later = jnp.min(jnp.where(later, blk[None, :], n_blocks), axis=1)
    nxt = jnp.where(first_later < n_blocks, blk_exp[jnp.minimum(first_later, n_blocks - 1)], -1).astype(I32)
    x_map = lambda i, be, u, nx: (jnp.minimum(i, u[0] - 1), 0)
    return pl.pallas_call(
        functools.partial(_moe_expert_kernel, layer=layer),
        grid_spec=pltpu.PrefetchScalarGridSpec(
            num_scalar_prefetch=3, grid=(n_blocks,),
            in_specs=[pl.BlockSpec((MOE_BLK, D // 2), x_map),
                      pl.BlockSpec(memory_space=pl.ANY), pl.BlockSpec(memory_space=pl.ANY)],
            out_specs=pl.BlockSpec((MOE_BLK, D // 2), lambda i, be, u, nx: (i, 0)),
            scratch_shapes=[pltpu.VMEM((2, D, 2 * D_EXPERT), F32), pltpu.VMEM((2, D_EXPERT, D), F32),
                            pltpu.VMEM((D, 2 * D_EXPERT), BF16), pltpu.VMEM((D_EXPERT, D), BF16),
                            pltpu.SMEM((1,), I32), pltpu.SemaphoreType.DMA((2, 2))]),
        out_shape=jax.ShapeDtypeStruct((n_slots, D // 2), U32),
        compiler_params=_cp(("arbitrary",)),
        name="moe_experts_sorted",
    )(blk_exp, n_used, nxt, xs, w_in, w_out)


def _moe_combine_kernel(c8_ref, lo_ref, rs_ref, ys_ref, idx_ref, tab_ref, gate_ref, x_ref, gt_ref, vec_ref,
                        o_ref, buf, sem, *, tm):
    step = pl.program_id(0)
    s = step % 2

    def runs(tile, sl, start):
        for e in range(N_EXPERTS):
            j = tile * N_EXPERTS + e

            @pl.when(c8_ref[j] > 0)
            def _():
                size = pl.multiple_of(c8_ref[j], 8)
                cp = pltpu.make_async_copy(ys_ref.at[pl.ds(pl.multiple_of(rs_ref[j], 8), size), :],
                                           buf.at[sl, pl.ds(pl.multiple_of(lo_ref[j], 8), size), :], sem.at[sl])
                if start:
                    cp.start()
                else:
                    cp.wait()

    @pl.when(step == 0)
    def _():
        buf[...] = jnp.zeros_like(buf)
        runs(0, 0, start=True)

    @pl.when(step + 1 < pl.num_programs(0))
    def _():
        runs(step + 1, 1 - s, start=True)

    runs(step, s, start=False)
    e0, e1, l0, l1 = (idx_ref[r:r + 1, :] for r in range(4))
    expert = lax.broadcasted_iota(I32, (N_EXPERTS, tm), 0)
    lo_col = tab_ref[:, 0:1].astype(F32)

    def position(e, l):
        return jnp.sum(jnp.where(expert == e, lo_col, 0.0), axis=0, keepdims=True).astype(I32) + l

    row = lax.broadcasted_iota(I32, (buf.shape[1], tm), 0)
    y_sorted = _unpack_pairs(buf[s])
    ya = _bdot_tn((row == position(e0, l0)).astype(BF16), y_sorted)
    yb = _bdot_tn((row == position(e1, l1)).astype(BF16), y_sorted)
    ff = ya * gate_ref[:, 0:1] + yb * gate_ref[:, 1:2]
    o_ref[...] = _layer_norm(DN_ALPHA * x_ref[...] + gt_ref[...] * ff, vec_ref[0:1, :], vec_ref[1:2, :])


def _moe_combine(ys, idx, c8, loff, run_start, gates, x, gt, vec, tm):
    bsz, t, _ = x.shape
    nt = t // tm
    buf_rows = 2 * tm + N_EXPERTS * 8
    tab = jnp.pad(loff[:, :, None], ((0, 0), (0, 0), (0, LANES - 1)))
    flat = lambda a: a.reshape(-1).astype(I32)
    return pl.pallas_call(
        functools.partial(_moe_combine_kernel, tm=tm),
        grid_spec=pltpu.PrefetchScalarGridSpec(
            num_scalar_prefetch=3, grid=(bsz * nt,),
            in_specs=[pl.BlockSpec(memory_space=pl.ANY),
                      pl.BlockSpec((8, tm), lambda i, *_: (0, i)),
                      pl.BlockSpec((None, N_EXPERTS, LANES), lambda i, *_: (i, 0, 0)),
                      pl.BlockSpec((tm, 2), lambda i, *_: (i, 0)),
                      pl.BlockSpec((None, tm, D), lambda i, *_: (i // nt, i % nt, 0)),
                      pl.BlockSpec((None, 1, D), lambda i, *_: (i // nt, 0, 0)),
                      pl.BlockSpec((2, D), lambda i, *_: (0, 0))],
            out_specs=pl.BlockSpec((None, tm, D), lambda i, *_: (i // nt, i % nt, 0)),
            scratch_shapes=[pltpu.VMEM((2, buf_rows, D // 2), U32), pltpu.SemaphoreType.DMA((2,))]),
        out_shape=jax.ShapeDtypeStruct(x.shape, F32),
        compiler_params=_cp(("arbitrary",)),
        name="moe_combine",
    )(flat(c8), flat(loff), flat(run_start), ys, idx, tab, gates, x, gt, vec)


def _moe_sorted(x, sc, sh, gt, vec, w, layer, tm=256):
    bsz, t, _ = x.shape
    n = bsz * t
    tiles = n // tm
    n_blocks = -(-(2 * n + tiles * N_EXPERTS * 7 + N_EXPERTS * (MOE_BLK - 1)) // MOE_BLK)
    h_rows, idx, gate, _, cnt = _moe_route(x, sc, sh, w["router_w_t"], w["router_b"], per_b_mod=True,
                                           rows_out=True, tm=tm)
    c8 = (cnt[:, :, 0].astype(I32) + 7) // 8 * 8
    padded = (jnp.sum(c8, axis=0) + MOE_BLK - 1) // MOE_BLK * MOE_BLK
    pad_end = jnp.cumsum(padded)
    run_start = (pad_end - padded)[None, :] + jnp.cumsum(c8, axis=0) - c8
    loff = jnp.cumsum(c8, axis=1) - c8
    blk_start = jnp.arange(n_blocks, dtype=I32) * MOE_BLK
    blk_exp = jnp.minimum(jnp.sum((pad_end[None, :] <= blk_start[:, None]).astype(I32), axis=1), N_EXPERTS - 1)
    n_used = (pad_end[-1:] // MOE_BLK).astype(I32)
    zero_start = jnp.concatenate([jnp.maximum(pad_end - MOE_BLK, 0), n_used])
    xs = _moe_dispatch(h_rows, idx, c8, loff, run_start, zero_start, n_blocks * MOE_BLK, tm)
    ys = _moe_experts_sorted(xs, blk_exp, n_used, w["moe_w_in"], w["moe_w_out"], layer)
    return _moe_combine(ys, idx, c8, loff, run_start, gate[:2].T, x, gt, vec, tm)


def _moe_dense_kernel(h_ref, gm_ref, win_ref, wout_ref, x_ref, gt_ref, vec_ref, o_ref, acc_ref):
    e = pl.program_id(0)

    @pl.when(e == 0)
    def _():
        acc_ref[...] = jnp.zeros_like(acc_ref)

    lane = lax.broadcasted_iota(I32, gm_ref.shape, 1)
    gcol = jnp.sum(jnp.where(lane == e, gm_ref[...], 0.0), axis=1, keepdims=True)
    hmid = _bdot(h_ref[...], win_ref[...])
    gt = hmid[:, :D_EXPERT]
    up = hmid[:, D_EXPERT:]
    y = _bdot(gt * _sigmoid(gt) * up, wout_ref[...])
    acc_ref[...] = acc_ref[...] + jnp.where(gcol != 0.0, gcol * y, 0.0)

    @pl.when(e == N_EXPERTS - 1)
    def _():
        o_ref[...] = _layer_norm(DN_ALPHA * x_ref[...] + gt_ref[...] * acc_ref[...], vec_ref[0:1, :],
                                 vec_ref[1:2, :])


def _moe_dense(x, sc, sh, gt, vec, w, layer):
    _, m, _ = x.shape
    h, _, _, gmat, _ = _moe_route(x, sc, sh, w["router_w_t"], w["router_b"], per_b_mod=False, rows_out=False,
                                  tm=m)
    full = pl.BlockSpec((m, D), lambda e: (0, 0))
    out = pl.pallas_call(
        _moe_dense_kernel,
        grid=(N_EXPERTS,),
        in_specs=[full, pl.BlockSpec((m, N_EXPERTS), lambda e: (0, 0)),
                  pl.BlockSpec((None, None, D, 2 * D_EXPERT), lambda e: (layer, e, 0, 0)),
                  pl.BlockSpec((None, None, D_EXPERT, D), lambda e: (layer, e, 0, 0)),
                  full, full, pl.BlockSpec((2, D), lambda e: (0, 0))],
        out_specs=full,
        out_shape=jax.ShapeDtypeStruct((m, D), F32),
        scratch_shapes=[pltpu.VMEM((m, D), F32)],
        compiler_params=_cp(("arbitrary",)),
        name="moe_dense",
    )(h[0], gmat.T, w["moe_w_in"], w["moe_w_out"], x[0], gt[0], vec)
    return out[None]


def _rotary(x, cos_t, sa_t, sb_t):
    reps = x.shape[1] // LANES
    tile = lambda t: t if reps == 1 else jnp.concatenate([t] * reps, axis=1)
    n = x.shape[1]
    half = ROT_DIM // 2
    return x * tile(cos_t) + pltpu.roll(x, n - half, axis=1) * tile(sa_t) + pltpu.roll(x, half, axis=1) * tile(sb_t)


def _class_rows(c, rows, dil):
    return pl.ds(c, rows, stride=dil) if dil > 1 else pl.ds(0, rows)


def _class_major_perm(tm, dil, inverse=False):
    cm_row = lax.broadcasted_iota(I32, (tm, tm), 1 if inverse else 0)
    tok_row = lax.broadcasted_iota(I32, (tm, tm), 0 if inverse else 1)
    rows = tm // dil
    return ((cm_row // rows == tok_row % dil) & (cm_row % rows == tok_row // dil)).astype(BF16)


def _attn_pre_kernel(x_ref, sc_ref, sh_ref, ksc_ref, ksh_ref, rot_ref, wq_ref, wkv_ref, *refs, dils, tm):
    x = x_ref[...]
    cos_t, sa_t, sb_t = rot_ref[0], rot_ref[1], rot_ref[2]
    q = _rotary(_bdot(x * (1.0 + sc_ref[...]) + sh_ref[...], wq_ref[...]), cos_t, sa_t, sb_t) * HD ** -0.5
    kv = _bdot(x * (1.0 + ksc_ref[...]) + ksh_ref[...], wkv_ref[...])
    k = _rotary(kv[:, :KVW], cos_t, sa_t, sb_t)
    v = kv[:, KVW:]
    k_ref, v_ref = refs[0], refs[1]
    k_ref[...] = k
    v_ref[...] = v
    if dils is None:
        refs[2][...] = q
        return
    ng = len(dils)
    q_refs, kb_refs, vb_refs = refs[2:2 + ng], refs[2 + ng:2 + 2 * ng], refs[2 + 2 * ng:2 + 3 * ng]
    qb, kb, vb = q.astype(BF16), k.astype(BF16), v.astype(BF16)
    for g, dil in enumerate(dils):
        rows = tm // dil
        blk = jnp.concatenate([qb[:, g * D:(g + 1) * D], kb, vb], axis=1)
        if dil > 1:
            blk = jnp.dot(_class_major_perm(tm, dil), blk, preferred_element_type=F32).astype(BF16)
        for c in range(dil):
            rs = slice(c * rows, (c + 1) * rows)
            q_refs[g][c] = blk[rs, :D]
            kb_refs[g][c] = blk[rs, D:D + KVW]
            vb_refs[g][c] = blk[rs, D + KVW:]


def _attn_pre(x, sc, sh, ksc, ksh, rot, w_q, w_kv, *, per_b_mod, tm, dils=None):
    bsz, t, _ = x.shape
    row = lambda wd: pl.BlockSpec((None, tm, wd), lambda b, i: (b, i, 0))
    mod = pl.BlockSpec((None, 1, D), lambda b, i: (b, 0, 0)) if per_b_mod else row(D)
    nq = w_q.shape[1]
    out_specs = [row(KVW), row(KVW)]
    out_shape = [jax.ShapeDtypeStruct((bsz, t, KVW), F32)] * 2
    scratch = []
    if dils is None:
        out_specs.append(row(nq))
        out_shape.append(jax.ShapeDtypeStruct((bsz, t, nq), F32))
    else:
        for wd in (D, KVW, KVW):
            for dil in dils:
                out_specs.append(pl.BlockSpec((None, dil, tm // dil, wd), lambda b, i: (b, 0, i, 0)))
                out_shape.append(jax.ShapeDtypeStruct((bsz, dil, t // dil, wd), BF16))
    outs = pl.pallas_call(
        functools.partial(_attn_pre_kernel, dils=dils, tm=tm),
        grid=(bsz, t // tm),
        in_specs=[row(D), mod, mod, mod, mod, pl.BlockSpec((3, tm, LANES), lambda b, i: (0, i, 0)),
                  pl.BlockSpec((D, nq), lambda b, i: (0, 0)), pl.BlockSpec((D, 2 * KVW), lambda b, i: (0, 0))],
        out_specs=out_specs,
        out_shape=out_shape,
        scratch_shapes=scratch,
        compiler_params=_cp(("arbitrary", "arbitrary")),
        name="attn_pre",
    )(x, sc, sh, ksc, ksh, rot, w_q, w_kv)
    if dils is None:
        return outs
    ng = len(dils)
    return outs[0], outs[1], outs[2:2 + ng], outs[2 + ng:2 + 2 * ng], outs[2 + 2 * ng:]


def _attn_band_kernel(q_ref, ka_ref, kb_ref, va_ref, vb_ref, o_ref, lse_ref):
    i = pl.program_id(2)
    bq = ATT_WIN
    kcat = jnp.concatenate([ka_ref[...], kb_ref[...]], axis=0)
    vcat = jnp.concatenate([va_ref[...], vb_ref[...]], axis=0)
    qs = jnp.concatenate([q_ref[:, rep * KVW:(rep + 1) * KVW] for rep in range(REP)], axis=0)
    qrow = lax.broadcasted_iota(I32, (REP * bq, 2 * bq), 0) % bq
    kcol = lax.broadcasted_iota(I32, (REP * bq, 2 * bq), 1)
    valid = (kcol >= qrow) & (kcol <= qrow + bq) & ((i > 0) | (kcol >= bq))
    lane = lax.broadcasted_iota(I32, (REP * bq, KVW), 1) // HD
    kv_lane = lax.broadcasted_iota(I32, (2 * bq, KVW), 1) // HD
    lse_lane = lax.broadcasted_iota(I32, (bq, LANES), 1)
    lse = jnp.zeros((bq, LANES), F32)
    linv = jnp.ones((REP * bq, KVW), F32)
    ps = []
    for h in range(KV_HEADS):
        kh = jnp.where(kv_lane == h, kcat, jnp.zeros_like(kcat))
        s = lax.dot_general(qs, kh, (((1,), (1,)), ((), ())), preferred_element_type=F32)
        s = jnp.where(valid, s, -jnp.inf)
        m = jnp.max(s, axis=-1, keepdims=True)
        p = jnp.exp(s - m)
        l = jnp.sum(p, axis=-1, keepdims=True)
        ps.append(p.astype(BF16))
        linv = jnp.where(lane == h, 1.0 / l, linv)
        lse_h = m + jnp.log(l)
        for rep in range(REP):
            lse = jnp.where(lse_lane == rep * KV_HEADS + h, lse_h[rep * bq:(rep + 1) * bq], lse)
    vstack = jnp.concatenate([jnp.where(kv_lane == h, vcat, jnp.zeros_like(vcat)) for h in range(KV_HEADS)],
                             axis=0)
    o = (jnp.dot(jnp.concatenate(ps, axis=1), vstack, preferred_element_type=F32) * linv).astype(BF16)
    for rep in range(REP):
        o_ref[:, rep * KVW:(rep + 1) * KVW] = o[rep * bq:(rep + 1) * bq]
    lse_ref[...] = lse


def _attn_band(q, kb, vb, dil):
    bsz, _, tc, _ = q.shape
    nb = tc // ATT_WIN
    blk = lambda wd: pl.BlockSpec((None, None, ATT_WIN, wd), lambda b, c, i: (b, c, i, 0))
    prev = pl.BlockSpec((None, None, ATT_WIN, KVW), lambda b, c, i: (b, c, jnp.maximum(i - 1, 0), 0))
    return pl.pallas_call(
        _attn_band_kernel,
        grid=(bsz, dil, nb),
        in_specs=[blk(D), prev, blk(KVW), prev, blk(KVW)],
        out_specs=[blk(D), blk(LANES)],
        out_shape=[jax.ShapeDtypeStruct((bsz, dil, tc, D), BF16), jax.ShapeDtypeStruct((bsz, dil, tc, LANES), F32)],
        compiler_params=_cp(("arbitrary", "arbitrary", "arbitrary")),
        name=f"attn_band_d{dil}",
    )(q, kb, kb, vb, vb)


def _attn_step_kernel(q_ref, kn_ref, vn_ref, kt_ref, vt_ref, o_ref, lse_ref, *, wbuf):
    nq = len(DILATED_GROUPS) * REP
    row = lax.broadcasted_iota(I32, (nq, wbuf), 0)
    pos = lax.broadcasted_iota(I32, (nq, wbuf), 1)
    valid = None
    for g, (_, dil) in enumerate(DILATED_GROUPS):
        ok = (row // REP == g) & (pos >= wbuf - ATT_WIN * dil) & (pos % dil == 0)
        valid = ok if valid is None else valid | ok
    rnd = lambda t: t.astype(BF16).astype(F32)
    lse_lane = lax.broadcasted_iota(I32, (nq, LANES), 1)
    lse = jnp.zeros((nq, LANES), F32)
    for h in range(KV_HEADS):
        hs = slice(h * HD, (h + 1) * HD)
        qh = q_ref[:, hs]
        s = jnp.where(valid, _bdot(qh, kt_ref[h]), -1e30)
        s_n = jnp.sum(rnd(qh) * rnd(kn_ref[:, hs]), axis=-1, keepdims=True)
        m = jnp.maximum(jnp.max(s, axis=-1, keepdims=True), s_n)
        p = jnp.exp(s - m)
        p_n = jnp.exp(s_n - m)
        l = jnp.sum(p, axis=-1, keepdims=True) + p_n
        o_ref[:, hs] = (_bdot_nt(p, vt_ref[h]) + rnd(p_n) * rnd(vn_ref[:, hs])) / l
        lse = jnp.where(lse_lane == h, m + jnp.log(l), lse)
    lse_ref[...] = lse


def _attn_step(q, k_new, v_new, cache_k, cache_v):
    bsz = q.shape[0]
    wbuf = cache_k.shape[1]
    ng = len(DILATED_GROUPS)
    assert all(wbuf % dil == 0 and wbuf >= ATT_WIN * dil for _, dil in DILATED_GROUPS)
    nq = ng * REP
    cache_spec = pl.BlockSpec((None, KV_HEADS, HD, wbuf), lambda b: (b, 0, 0, 0))
    o, lse = pl.pallas_call(
        functools.partial(_attn_step_kernel, wbuf=wbuf),
        grid=(bsz,),
        in_specs=[pl.BlockSpec((None, nq, KVW), lambda b: (b, 0, 0)),
                  pl.BlockSpec((None, 1, KVW), lambda b: (b, 0, 0)),
                  pl.BlockSpec((None, 1, KVW), lambda b: (b, 0, 0)), cache_spec, cache_spec],
        out_specs=[pl.BlockSpec((None, nq, KVW), lambda b: (b, 0, 0)),
                   pl.BlockSpec((None, nq, LANES), lambda b: (b, 0, 0))],
        out_shape=[jax.ShapeDtypeStruct((bsz, nq, KVW), F32), jax.ShapeDtypeStruct((bsz, nq, LANES), F32)],
        compiler_params=_cp(("arbitrary",)),
        name="attn_step",
    )(q.reshape(bsz, nq, KVW), k_new.reshape(bsz, 1, KVW), v_new.reshape(bsz, 1, KVW),
      cache_k.transpose(0, 2, 3, 1), cache_v.transpose(0, 2, 3, 1))
    o = o.reshape(bsz, ng, D).transpose(1, 0, 2)
    lse = lse[:, :, :KV_HEADS].reshape(bsz, ng, REP * KV_HEADS).transpose(1, 0, 2)
    return o, jnp.pad(lse, ((0, 0), (0, 0), (0, LANES - REP * KV_HEADS)))


def _attn_post_kernel(o0_ref, o1_ref, o2_ref, l0_ref, l1_ref, l2_ref, x_ref, gt_ref, vec_ref, wo_ref, out_ref,
                      *scratch, dils, tm):
    o_refs = [o0_ref, o1_ref, o2_ref]
    l_refs = [l0_ref, l1_ref, l2_ref]
    if dils is not None:
        for g, dil in enumerate(dils):
            if dil == 1:
                o_refs[g], l_refs[g] = o_refs[g].at[0], l_refs[g].at[0]
                continue
            sl = scratch[g]
            for c in range(dil):
                sl[_class_rows(c, tm // dil, dil), :] = l_refs[g][c]
            l_refs[g] = sl
            o_cm = jnp.concatenate([o_refs[g][c] for c in range(dil)], axis=0)
            o_refs[g] = jnp.dot(_class_major_perm(tm, dil, inverse=True), o_cm, preferred_element_type=F32)
    o0, o1, o2 = (r if isinstance(r, jax.Array) else r[...] for r in o_refs)
    l0, l1, l2 = (r[...] for r in l_refs)
    m = jnp.maximum(jnp.maximum(l0, l1), l2)
    e0, e1, e2 = jnp.exp(l0 - m), jnp.exp(l1 - m), jnp.exp(l2 - m)
    den = e0 + e1 + e2
    r = lax.broadcasted_iota(I32, (LANES, D), 0)
    c = lax.broadcasted_iota(I32, (LANES, D), 1) // HD
    spread = (r == c).astype(BF16)
    mix = (_split_dot(e0 / den, spread) * o0 + _split_dot(e1 / den, spread) * o1
           + _split_dot(e2 / den, spread) * o2)
    out = _bdot(mix, wo_ref[...])
    out_ref[...] = _layer_norm(DN_ALPHA * x_ref[...] + gt_ref[...] * out, vec_ref[0:1, :], vec_ref[1:2, :])


def _attn_post(os, lses, x, gt, vec, w_o, *, per_b_mod, tm, dils=None):
    bsz, t, _ = x.shape
    row = lambda wd: pl.BlockSpec((None, tm, wd), lambda b, i: (b, i, 0))
    mod = pl.BlockSpec((None, 1, D), lambda b, i: (b, 0, 0)) if per_b_mod else row(D)
    if dils is None:
        o_specs, l_specs, scratch = [row(D)] * 3, [row(LANES)] * 3, []
    else:
        cls = lambda dil, wd: pl.BlockSpec((None, dil, tm // dil, wd), lambda b, i: (b, 0, i, 0))
        o_specs = [cls(dil, D) for dil in dils]
        l_specs = [cls(dil, LANES) for dil in dils]
        scratch = [pltpu.VMEM((tm, LANES), F32) for _ in dils]
    return pl.pallas_call(
        functools.partial(_attn_post_kernel, dils=dils, tm=tm),
        grid=(bsz, t // tm),
        in_specs=o_specs + l_specs + [row(D), mod, pl.BlockSpec((2, D), lambda b, i: (0, 0)),
                                      pl.BlockSpec((D, D), lambda b, i: (0, 0))],
        out_specs=row(D),
        out_shape=jax.ShapeDtypeStruct(x.shape, F32),
        scratch_shapes=scratch,
        compiler_params=_cp(("arbitrary", "arbitrary")),
        name="attn_post",
    )(*os, *lses, x, gt, vec, w_o)


def _rot_tables(pos):
    half = ROT_DIM // 2
    inv = ROPE_THETA ** (-jnp.arange(half, dtype=F32) * 2.0 / ROT_DIM)
    ang = pos.astype(F32)[:, None] * inv[None, :]
    cos, sin = jnp.cos(ang), jnp.sin(ang)
    t = pos.shape[0]
    one = jnp.ones((t, HD - ROT_DIM), F32)
    zero = jnp.zeros((t, HD - ROT_DIM), F32)
    zh = jnp.zeros((t, half), F32)
    cos_t = jnp.concatenate([cos, cos, one], axis=1)
    sa_t = jnp.concatenate([-sin, zh, zero], axis=1)
    sb_t = jnp.concatenate([zh, sin, zero], axis=1)
    return jnp.stack([jnp.tile(x, (1, 2)) for x in (cos_t, sa_t, sb_t)])


def _prep_weights(p):
    pad_c = lambda m, n: jnp.pad(m, ((0, 0), (0, n - m.shape[1])))
    pad_r = lambda m, n: jnp.pad(m, ((0, n - m.shape[0]), (0, 0)))
    w = {}
    w["mu"] = p["rwkv_mu"][0]
    w["w_rkv"] = p["rwkv_w_rkv"][0].astype(BF16)
    w["w1"] = pad_c(p["rwkv_w1"][0], LANES).astype(BF16)
    w["w2"] = pad_r(p["rwkv_w2"][0], LANES).astype(BF16)
    w["a1"] = pad_c(p["rwkv_a1"][0], LANES).astype(BF16)
    w["a2"] = pad_r(p["rwkv_a2"][0], LANES).astype(BF16)
    w["g1"] = pad_c(p["rwkv_g1"][0], 2 * LANES).astype(BF16)
    w["g2"] = pad_r(p["rwkv_g2"][0], 2 * LANES).astype(BF16)
    w["vec"] = jnp.stack([p["rwkv_w0"][0], p["rwkv_a0"][0], p["rwkv_k_k"][0], p["rwkv_k_a"][0]])
    w["post_vec"] = jnp.stack([p["rwkv_r_k"][0].reshape(D), p["rwkv_lnx_g"][0], p["rwkv_lnx_b"][0],
                               p["ln_g"][0, 0], p["ln_b"][0, 0]])
    w["rwkv_w_o"] = p["rwkv_w_o"][0].astype(BF16)
    wq = p["w_q"][0].reshape(D, 3, KV_HEADS, REP, HD).transpose(0, 1, 3, 2, 4).reshape(D, 3 * D)
    w["w_q"] = wq.astype(BF16)
    w["w_kv"] = p["w_kv"].astype(BF16)
    wo = p["w_o_attn"][0].reshape(KV_HEADS, REP, HD, D).transpose(1, 0, 2, 3).reshape(D, D)
    w["w_o_attn"] = wo.astype(BF16)
    w["router_w_t"] = p["router_w"].T.astype(BF16)
    w["router_b"] = p["router_b"].reshape(N_EXPERTS, 1)
    w["moe_w_in"] = p["moe_w_in"]
    w["moe_w_out"] = p["moe_w_out"]
    w["ln"] = [[jnp.stack([p["ln_g"][l, i], p["ln_b"][l, i]]) for i in range(2)] for l in range(DEPTH)]
    return w


def _modulations(c_prompt, c_sample, p):
    nb = c_prompt.shape[0]
    c = jnp.concatenate([c_prompt, c_sample], axis=0)
    pad = (-c.shape[0]) % 8
    c = jnp.pad(c, ((0, pad), (0, 0)))
    m3 = _ada_linear(c, p["ada_w"].reshape(2 * DEPTH, D, 3 * D), p["ada_b"].reshape(2 * DEPTH, 3 * D))
    m2 = _ada_linear(c, p["kv_ada_w"][None], p["kv_ada_b"][None])[0]
    n_all = nb + c_sample.shape[0]

    def split(m, parts, lo, hi):
        return [m[lo:hi, i * D:(i + 1) * D] for i in range(parts)]

    out = {}
    for name, lo, hi in (("prompt", 0, nb), ("sample", nb, n_all)):
        out[name] = {"ada": [[split(m3[2 * l + i], 3, lo, hi) for i in range(2)] for l in range(DEPTH)],
                     "kv": split(m2, 2, lo, hi)}
    return out


def _trunk_prompt(x, mods, w):
    bsz, t, _ = x.shape
    per_b = lambda m: m[:, None, :]
    sh, sc, gt = (per_b(m) for m in mods["ada"][0][0])
    zeros = jnp.zeros((bsz, 1, D), F32)
    r, lw, k, v, a, b, g, last = _rwkv_pre(x, sc, sh, zeros, w, seq_mode=True, tm=256)
    y, zf = _wkv_chunked(r, lw, k, v, a, b)
    x = _rwkv_post(y, r, k, v, g, x, gt, w["post_vec"], w["rwkv_w_o"], pairs=True, tm=256)
    zf = zf.reshape(bsz, PAIRS, 2, HD, 2, HD)
    wkv = jnp.stack([zf[:, :, 0, :, 0, :], zf[:, :, 1, :, 1, :]], axis=2)
    wkv = wkv.reshape(bsz, HEADS, HD, HD).transpose(0, 1, 3, 2)
    sh2, sc2, gt2 = (per_b(m) for m in mods["ada"][0][1])
    x = _moe_sorted(x, sc2, sh2, gt2, w["ln"][0][1], w, 0)

    sh, sc, gt = (per_b(m) for m in mods["ada"][1][0])
    ksh, ksc = (per_b(m) for m in mods["kv"])
    rot = _rot_tables(jnp.arange(t, dtype=I32))
    dils = tuple(dil for _, dil in DILATED_GROUPS)
    k_new, v_new, qs, kbs, vbs = _attn_pre(x, sc, sh, ksc, ksh, rot, w["w_q"], w["w_kv"], per_b_mod=True,
                                           tm=256, dils=dils)
    os, lses = zip(*[_attn_band(qs[g], kbs[g], vbs[g], dil) for g, dil in enumerate(dils)])
    x = _attn_post(os, lses, x, gt, w["ln"][1][0], w["w_o_attn"], per_b_mod=True, tm=256, dils=dils)
    sh2, sc2, gt2 = (per_b(m) for m in mods["ada"][1][1])
    x = _moe_sorted(x, sc2, sh2, gt2, w["ln"][1][1], w, 1)
    return x, wkv[None], last.reshape(1, bsz, D), k_new, v_new


def _trunk_sample(x, mods, state_wkv, state_shift, cache_k, cache_v, w):
    bsz = x.shape[0]
    xs = x.reshape(1, bsz, D)
    row = lambda m: m[None]
    sh, sc, gt = (row(m) for m in mods["ada"][0][0])
    r, lw, k, v, a, b, g, hm = _rwkv_pre(xs, sc, sh, state_shift[0][None], w, seq_mode=False, tm=bsz)
    s_new, y = _wkv_step(state_wkv[0].transpose(1, 2, 3, 0), *(t[0].T for t in (r, lw, k, v, a, b)))
    s_new = s_new.transpose(3, 0, 1, 2)
    xs = _rwkv_post(y.T[None], r, k, v, g, xs, gt, w["post_vec"], w["rwkv_w_o"], pairs=False, tm=bsz)
    sh2, sc2, gt2 = (row(m) for m in mods["ada"][0][1])
    xs = _moe_dense(xs, sc2, sh2, gt2, w["ln"][0][1], w, 0)

    sh, sc, gt = (row(m) for m in mods["ada"][1][0])
    ksh, ksc = (row(m) for m in mods["kv"])
    rot = _rot_tables(jnp.full((bsz,), PAST_LEN, I32))
    k_new, v_new, q = _attn_pre(xs, sc, sh, ksc, ksh, rot, w["w_q"], w["w_kv"], per_b_mod=False, tm=bsz)
    o, lse = _attn_step(q[0], k_new[0], v_new[0], cache_k, cache_v)
    xs = _attn_post([o[i][None] for i in range(3)], [lse[i][None] for i in range(3)], xs, gt, w["ln"][1][0],
                    w["w_o_attn"], per_b_mod=False, tm=bsz)
    sh2, sc2, gt2 = (row(m) for m in mods["ada"][1][1])
    xs = _moe_dense(xs, sc2, sh2, gt2, w["ln"][1][1], w, 1)
    return (xs.reshape(bsz, 1, D), s_new[None], hm, k_new.reshape(bsz, 1, KV_HEADS, HD),
            v_new.reshape(bsz, 1, KV_HEADS, HD))


def kernel(x_prompt, x_sample, state_wkv, state_shift, cache_k, cache_v, c_prompt, c_sample, ada_w, ada_b, ln_g, ln_b, rwkv_mu, rwkv_w_rkv, rwkv_w0, rwkv_w1, rwkv_w2, rwkv_a0, rwkv_a1, rwkv_a2, rwkv_g1, rwkv_g2, rwkv_k_k, rwkv_k_a, rwkv_r_k, rwkv_lnx_g, rwkv_lnx_b, rwkv_w_o, w_q, w_kv, kv_ada_w, kv_ada_b, w_o_attn, router_w, router_b, moe_w_in, moe_w_out):
    p = {"ada_w": ada_w, "ada_b": ada_b, "ln_g": ln_g, "ln_b": ln_b, "rwkv_mu": rwkv_mu, "rwkv_w_rkv": rwkv_w_rkv,
         "rwkv_w0": rwkv_w0, "rwkv_w1": rwkv_w1, "rwkv_w2": rwkv_w2, "rwkv_a0": rwkv_a0, "rwkv_a1": rwkv_a1,
         "rwkv_a2": rwkv_a2, "rwkv_g1": rwkv_g1, "rwkv_g2": rwkv_g2, "rwkv_k_k": rwkv_k_k, "rwkv_k_a": rwkv_k_a,
         "rwkv_r_k": rwkv_r_k, "rwkv_lnx_g": rwkv_lnx_g, "rwkv_lnx_b": rwkv_lnx_b, "rwkv_w_o": rwkv_w_o,
         "w_q": w_q, "w_kv": w_kv, "kv_ada_w": kv_ada_w, "kv_ada_b": kv_ada_b, "w_o_attn": w_o_attn,
         "router_w": router_w, "router_b": router_b, "moe_w_in": moe_w_in, "moe_w_out": moe_w_out}
    w = _prep_weights(p)
    mods = _modulations(c_prompt, c_sample, p)
    bp, tp, _ = x_prompt.shape
    y_p, wkv_p, shift_p, k_p, v_p = _trunk_prompt(x_prompt, mods["prompt"], w)
    keep = min(PAST_LEN, tp)
    k_p = k_p[:, tp - keep:].reshape(bp, keep, KV_HEADS, HD)
    v_p = v_p[:, tp - keep:].reshape(bp, keep, KV_HEADS, HD)
    y_s, wkv_s, shift_s, k_s, v_s = _trunk_sample(x_sample, mods["sample"], state_wkv, state_shift, cache_k,
                                                  cache_v, w)
    return (y_p, y_s, wkv_p, shift_p, k_p, v_p, wkv_s, shift_s, k_s, v_s)
```

```python
import functools
import math

import jax
import jax.numpy as jnp
from jax import lax
from jax.experimental import pallas as pl
from jax.experimental.pallas import tpu as pltpu

F32 = jnp.float32
BF16 = jnp.bfloat16
I32 = jnp.int32
U32 = jnp.uint32

D = 1024
HEADS = 16
HD = 64
LANES = 128
PAIRS = D // LANES
CHUNK = 64
WKV_GROUP = 16
N_EXPERTS = 32
EXPERTS_PER_GROUP = 8
N_EXPERT_GROUPS = 4
D_EXPERT = 512
MOE_BLK = 256
KV_HEADS = 4
REP = 4
KVW = KV_HEADS * HD
DILATED_GROUPS = ((128, 1), (512, 4), (2048, 16))
ATT_WIN = 128
PAST_LEN = 2048
ROT_DIM = 16
ROPE_THETA = 500000.0
DEPTH = 2
DN_ALPHA = (2 * DEPTH) ** 0.25
LN_EPS = 1e-5
GN_EPS = 64e-5
VMEM_LIMIT = 56 * 1024 * 1024


def _cp(sem):
    return pltpu.CompilerParams(dimension_semantics=sem, vmem_limit_bytes=VMEM_LIMIT)


def _bdot(a, b):
    return jnp.dot(a.astype(BF16), b.astype(BF16), preferred_element_type=F32)


def _bdot_nt(a, b):
    return lax.dot_general(a.astype(BF16), b.astype(BF16), (((1,), (1,)), ((), ())),
                           preferred_element_type=F32)


def _bdot_tn(a, b):
    return lax.dot_general(a.astype(BF16), b.astype(BF16), (((0,), (0,)), ((), ())),
                           preferred_element_type=F32)


def _split_dot(x, m):
    hi = x.astype(BF16)
    lo = (x - hi.astype(F32)).astype(BF16)
    return jnp.dot(jnp.concatenate([hi, lo], axis=1), jnp.concatenate([m, m], axis=0),
                   preferred_element_type=F32)


def _pack_pairs(x):
    k = x.shape[1] // 2
    hi = lax.bitcast_convert_type(x[:, :k].astype(BF16).astype(F32), U32)
    lo = lax.bitcast_convert_type(x[:, k:].astype(BF16).astype(F32), U32)
    return hi | (lo >> 16)


def _unpack_pairs(w):
    hi = lax.bitcast_convert_type(w & jnp.uint32(0xFFFF0000), F32)
    lo = lax.bitcast_convert_type(w << 16, F32)
    return jnp.concatenate([hi, lo], axis=1).astype(BF16)


def _head_ones():
    r = lax.broadcasted_iota(I32, (LANES, LANES), 0) // HD
    c = lax.broadcasted_iota(I32, (LANES, LANES), 1) // HD
    return (r == c).astype(BF16)


def _segsum(x, ones_bd):
    cols = [_split_dot(x[:, c * LANES:(c + 1) * LANES], ones_bd) for c in range(x.shape[1] // LANES)]
    return cols[0] if len(cols) == 1 else jnp.concatenate(cols, axis=1)


def _layer_norm(x, g, b):
    mu = jnp.mean(x, axis=-1, keepdims=True)
    xc = x - mu
    var = jnp.mean(xc * xc, axis=-1, keepdims=True)
    return xc * lax.rsqrt(var + LN_EPS) * g + b


def _sigmoid(x):
    return 1.0 / (1.0 + jnp.exp(-x))


def _to_pairs(ref, val):
    for p in range(PAIRS):
        ref[p] = val[:, p * LANES:(p + 1) * LANES]


def _from_pairs(ref):
    return jnp.concatenate([ref[p] for p in range(PAIRS)], axis=1)


def _ada_kernel(c_ref, w_ref, b_ref, o_ref):
    c = c_ref[...]
    o_ref[...] = _bdot(c * _sigmoid(c), w_ref[...]) + b_ref[...]


def _ada_linear(c, w, b, tn=512):
    s, _, n = w.shape
    m = c.shape[0]
    return pl.pallas_call(
        _ada_kernel,
        grid=(s, n // tn),
        in_specs=[pl.BlockSpec((m, D), lambda i, j: (0, 0)),
                  pl.BlockSpec((None, D, tn), lambda i, j: (i, 0, j)),
                  pl.BlockSpec((None, 1, tn), lambda i, j: (i, 0, j))],
        out_specs=pl.BlockSpec((None, m, tn), lambda i, j: (i, 0, j)),
        out_shape=jax.ShapeDtypeStruct((s, m, n), F32),
        compiler_params=_cp(("arbitrary", "arbitrary")),
        name="ada_linear",
    )(c, w, b.reshape(s, 1, n))


def _rwkv_pre_kernel(x_ref, sc_ref, sh_ref, prev_ref, mu_ref, wrkv_ref, w1_ref, w2_ref, a1_ref, a2_ref,
                     g1_ref, g2_ref, vec_ref,
                     r_ref, lw_ref, k_ref, v_ref, a_ref, b_ref, g_ref, bonus_ref, hm_ref, carry_ref,
                     *, seq_mode, tm):
    x = x_ref[...]
    hm = x * (1.0 + sc_ref[...]) + sh_ref[...]
    if seq_mode:
        @pl.when(pl.program_id(1) == 0)
        def _():
            carry_ref[...] = prev_ref[...]
        row = lax.broadcasted_iota(I32, hm.shape, 0)
        hprev = jnp.where(row == 0, carry_ref[...], pltpu.roll(hm, 1, axis=0))
        carry_ref[...] = hm[tm - 1:tm, :]
        hm_ref[...] = hm[tm - 1:tm, :]
    else:
        hprev = prev_ref[...]
        hm_ref[...] = hm
    xx = hprev - hm

    def mix(i):
        return hm + xx * mu_ref[i:i + 1, :]

    w0, a0, k_k, k_a, r_k = (vec_ref[i:i + 1, :] for i in range(5))
    r = _bdot(mix(0), wrkv_ref[0])
    k = _bdot(mix(2), wrkv_ref[1])
    v = _bdot(mix(3), wrkv_ref[2])
    wl = w0 + _bdot(jnp.tanh(_bdot(mix(1), w1_ref[...])), w2_ref[...])
    lw = -math.exp(-0.5) * _sigmoid(wl)
    a_lr = _sigmoid(a0 + _bdot(_bdot(mix(4), a1_ref[...]), a2_ref[...]))
    g = _bdot(_sigmoid(_bdot(mix(5), g1_ref[...])), g2_ref[...])
    kk = k * k_k
    kk = kk * jnp.minimum(lax.rsqrt(_segsum(kk * kk, _head_ones())), 1e12)
    kmod = k * (1.0 + (a_lr - 1.0) * k_a)
    bonus = _segsum(r * kmod * r_k, _head_ones()) * v
    outs = ((r_ref, r), (lw_ref, lw), (k_ref, kmod), (v_ref, v), (a_ref, -kk), (b_ref, kk * a_lr),
            (g_ref, g.astype(BF16)), (bonus_ref, bonus))
    for ref, val in outs:
        if seq_mode:
            _to_pairs(ref, val)
        else:
            ref[...] = val


def _rwkv_pre(x, sc, sh, prev, w, *, seq_mode, tm):
    bsz, t, _ = x.shape
    grid = (bsz, t // tm)
    row = pl.BlockSpec((None, tm, D), lambda b, i: (b, i, 0))
    per_b = pl.BlockSpec((None, 1, D), lambda b, i: (b, 0, 0))
    mod = per_b if seq_mode else row

    def const(shape):
        return pl.BlockSpec(shape, lambda b, i: (0,) * len(shape))

    if seq_mode:
        out_big = pl.BlockSpec((None, PAIRS, tm, LANES), lambda b, i: (b, 0, i, 0))
        big_shape = jax.ShapeDtypeStruct((bsz, PAIRS, t, LANES), F32)
        hm_spec, hm_shape = per_b, jax.ShapeDtypeStruct((bsz, 1, D), F32)
    else:
        out_big, big_shape = row, jax.ShapeDtypeStruct((bsz, t, D), F32)
        hm_spec, hm_shape = row, jax.ShapeDtypeStruct((bsz, t, D), F32)
    return pl.pallas_call(
        functools.partial(_rwkv_pre_kernel, seq_mode=seq_mode, tm=tm),
        grid=grid,
        in_specs=[row, mod, mod, mod, const((6, D)), const((3, D, D)), const((D, LANES)), const((LANES, D)),
                  const((D, LANES)), const((LANES, D)), const((D, 2 * LANES)), const((2 * LANES, D)),
                  const((5, D))],
        out_specs=[out_big] * 8 + [hm_spec],
        out_shape=[big_shape] * 6 + [jax.ShapeDtypeStruct(big_shape.shape, BF16), big_shape, hm_shape],
        scratch_shapes=[pltpu.VMEM((1, D), F32)],
        compiler_params=_cp(("arbitrary", "arbitrary")),
        name="rwkv_pre",
    )(x, sc, sh, prev, w["mu"], w["w_rkv"], w["w1"], w["w2"], w["a1"], w["a2"], w["g1"], w["g2"], w["vec"])


def _wkv_chunk_kernel(r_ref, lw_ref, k_ref, v_ref, a_ref, b_ref, y_ref, zf_ref, z_ref, *, n_pairs, group):
    c = pl.program_id(0)

    @pl.when(c == 0)
    def _():
        z_ref[...] = jnp.zeros_like(z_ref)

    L = CHUNK
    row_l = lax.broadcasted_iota(I32, (L, L), 0)
    col_l = lax.broadcasted_iota(I32, (L, L), 1)
    tri_incl = (row_l >= col_l).astype(BF16)
    lane = lax.broadcasted_iota(I32, (L, LANES), 1)
    head0 = lane < HD
    row = lax.broadcasted_iota(I32, (LANES, LANES), 0)
    col = lax.broadcasted_iota(I32, (LANES, LANES), 1)
    strict = row > col
    incl = row >= col
    eye = (row == col).astype(F32)

    def expand(x):
        return jnp.concatenate([jnp.where(head0, x, 0.0), jnp.where(head0, 0.0, x)], axis=0)

    def group_body(gi, carry):
        ids = [gi * group + j for j in range(group)]
        bp = [(i // PAIRS, i % PAIRS) for i in ids]
        G = range(group)
        r = [r_ref[b_i, p_i] for b_i, p_i in bp]
        lw = [lw_ref[b_i, p_i] for b_i, p_i in bp]
        k = [k_ref[b_i, p_i] for b_i, p_i in bp]
        v = [v_ref[b_i, p_i] for b_i, p_i in bp]
        a = [a_ref[b_i, p_i] for b_i, p_i in bp]
        b = [b_ref[b_i, p_i] for b_i, p_i in bp]
        z0 = [z_ref[i] for i in ids]
        cum = [_split_dot_left(tri_incl, lw[j]) for j in G]
        cum_l = [cum[j][L - 1:L, :] for j in G]
        inv = [jnp.exp(-cum[j]) for j in G]
        tail = [jnp.exp(cum_l[j] - cum[j]) for j in G]
        a_e = [expand(a[j] * jnp.exp(cum[j] - lw[j])) for j in G]
        r_e = [expand(r[j] * jnp.exp(cum[j])) for j in G]
        b_e = [expand(b[j] * inv[j]) for j in G]
        k_e = [expand(k[j] * inv[j]) for j in G]
        bd_e = [expand(b[j] * tail[j]) for j in G]
        kd_e = [expand(k[j] * tail[j]) for j in G]
        v_e = [expand(v[j]) for j in G]
        gm = [_bdot_nt(jnp.concatenate([a_e[j], r_e[j]], axis=0), jnp.concatenate([b_e[j], k_e[j]], axis=0))
              for j in G]
        m_ab = [jnp.where(strict, gm[j][:LANES, :LANES], 0.0) for j in G]
        m_ak = [jnp.where(strict, gm[j][:LANES, LANES:], 0.0) for j in G]
        m_rb = [jnp.where(incl, gm[j][LANES:, :LANES], 0.0) for j in G]
        m_rk = [jnp.where(incl, gm[j][LANES:, LANES:], 0.0) for j in G]
        tinv = [eye + m_ab[j] for j in G]
        pw = [_bdot(m_ab[j], m_ab[j]) for j in G]
        for _ in range(4):
            both = [_bdot(pw[j], jnp.concatenate([pw[j], tinv[j]], axis=1)) for j in G]
            pw = [both[j][:, :LANES] for j in G]
            tinv = [tinv[j] + both[j][:, LANES:] for j in G]
        tinv = [tinv[j] + _bdot(pw[j], tinv[j]) for j in G]
        rhs = [_bdot(a_e[j], z0[j]) + _bdot(m_ak[j], v_e[j]) for j in G]
        u = [_bdot(tinv[j], rhs[j]) for j in G]
        zuv = [jnp.concatenate([z0[j], u[j], v_e[j]], axis=0) for j in G]
        decay = [eye * jnp.exp(cum_l[j]) for j in G]
        for j in G:
            z_ref[ids[j]] = _bdot_tn(jnp.concatenate([decay[j], bd_e[j], kd_e[j]], axis=0), zuv[j])
            y_e = _bdot(jnp.concatenate([r_e[j], m_rb[j], m_rk[j]], axis=1), zuv[j])
            y_ref[bp[j][0], bp[j][1]] = y_e[:L] + y_e[L:]
        return carry

    lax.fori_loop(0, n_pairs // group, group_body, 0)

    @pl.when(c == pl.num_programs(0) - 1)
    def _():
        zf_ref[...] = z_ref[...]


def _split_dot_left(m, x):
    hi = x.astype(BF16)
    lo = (x - hi.astype(F32)).astype(BF16)
    return (jnp.dot(m, hi, preferred_element_type=F32) + jnp.dot(m, lo, preferred_element_type=F32))


def _wkv_chunked(r, lw, k, v, a, b):
    bsz, _, t, _ = r.shape
    n_pairs = bsz * PAIRS
    blk = pl.BlockSpec((bsz, PAIRS, CHUNK, LANES), lambda c: (0, 0, c, 0))
    return pl.pallas_call(
        functools.partial(_wkv_chunk_kernel, n_pairs=n_pairs, group=WKV_GROUP),
        grid=(t // CHUNK,),
        in_specs=[blk] * 6,
        out_specs=[blk, pl.BlockSpec((n_pairs, LANES, LANES), lambda c: (0, 0, 0))],
        out_shape=[jax.ShapeDtypeStruct(r.shape, F32), jax.ShapeDtypeStruct((n_pairs, LANES, LANES), F32)],
        scratch_shapes=[pltpu.VMEM((n_pairs, LANES, LANES), F32)],
        compiler_params=_cp(("arbitrary",)),
        name="wkv_chunked",
    )(r, lw, k, v, a, b)


def _wkv_step_kernel(s_ref, r_ref, lw_ref, k_ref, v_ref, a_ref, b_ref, so_ref, y_ref):
    w = jnp.exp(lw_ref[...])
    a, b, k, r = a_ref[...], b_ref[...], k_ref[...], r_ref[...]

    def body(i, c):
        s = s_ref[i]
        sa = jnp.sum(s * a, axis=0, keepdims=True)
        s_new = s * w + sa * b + v_ref[pl.ds(i, 1), :] * k
        so_ref[i] = s_new
        y_ref[pl.ds(i, 1), :] = jnp.sum(s_new * r, axis=0, keepdims=True)
        return c

    lax.fori_loop(0, HD, body, 0, unroll=4)


def _wkv_step(state_t, r, lw, k, v, a, b):
    bsz = state_t.shape[-1]
    big = pl.BlockSpec((None, HD, HD, bsz), lambda h: (h, 0, 0, 0))
    vec = pl.BlockSpec((HD, bsz), lambda h: (h, 0))
    return pl.pallas_call(
        _wkv_step_kernel,
        grid=(HEADS,),
        in_specs=[big] + [vec] * 6,
        out_specs=[big, vec],
        out_shape=[jax.ShapeDtypeStruct(state_t.shape, F32), jax.ShapeDtypeStruct((D, bsz), F32)],
        compiler_params=_cp(("arbitrary",)),
        name="wkv_step",
    )(state_t, r, lw, k, v, a, b)


def _rwkv_post_kernel(y_ref, bonus_ref, g_ref, x_ref, gt_ref, vec_ref, wo_ref, o_ref, *, pairs):
    load = _from_pairs if pairs else (lambda ref: ref[...])
    y, bonus, g = (load(ref) for ref in (y_ref, bonus_ref, g_ref))
    lnx_g, lnx_b, ln_g, ln_b = (vec_ref[i:i + 1, :] for i in range(4))
    ones_bd = _head_ones()
    ym = _segsum(y, ones_bd) * (1.0 / HD)
    yc = y - ym
    yv = _segsum(yc * yc, ones_bd) * (1.0 / HD)
    yn = yc * lax.rsqrt(yv + GN_EPS) * lnx_g + lnx_b
    mix = _bdot((yn + bonus) * g, wo_ref[...])
    o_ref[...] = _layer_norm(DN_ALPHA * x_ref[...] + gt_ref[...] * mix, ln_g, ln_b)


def _rwkv_post(y, bonus, g, x, gt, vec, w_o, *, pairs, tm):
    bsz, t, _ = x.shape
    row = pl.BlockSpec((None, tm, D), lambda b, i: (b, i, 0))
    big = pl.BlockSpec((None, PAIRS, tm, LANES), lambda b, i: (b, 0, i, 0)) if pairs else row
    mod = pl.BlockSpec((None, 1, D), lambda b, i: (b, 0, 0)) if pairs else row
    return pl.pallas_call(
        functools.partial(_rwkv_post_kernel, pairs=pairs),
        grid=(bsz, t // tm),
        in_specs=[big] * 3 + [row, mod, pl.BlockSpec((4, D), lambda b, i: (0, 0)),
                              pl.BlockSpec((D, D), lambda b, i: (0, 0))],
        out_specs=row,
        out_shape=jax.ShapeDtypeStruct(x.shape, F32),
        compiler_params=_cp(("arbitrary", "arbitrary")),
        name="rwkv_post",
    )(y, bonus, g, x, gt, vec, w_o)


def _top2(v):
    io = lax.broadcasted_iota(I32, v.shape, 0)
    m1 = jnp.max(v, axis=0, keepdims=True)
    i1 = jnp.min(jnp.where(v == m1, io, EXPERTS_PER_GROUP), axis=0, keepdims=True)
    v2 = jnp.where(io == i1, -jnp.inf, v)
    m2 = jnp.max(v2, axis=0, keepdims=True)
    i2 = jnp.min(jnp.where(v2 == m2, io, EXPERTS_PER_GROUP), axis=0, keepdims=True)
    return m1 + m2, i1, i2


def _moe_route_kernel(x_ref, sc_ref, sh_ref, rw_ref, rb_ref, h_ref, idx_ref, gate_ref, gmat_ref, cnt_ref,
                      *, tm):
    h = x_ref[...] * (1.0 + sc_ref[...]) + sh_ref[...]
    h_ref[...] = h.astype(h_ref.dtype)
    aff = _sigmoid(_bdot_nt(rw_ref[...], h))
    sel = aff + rb_ref[...]
    best = gi = i1 = i2 = None
    for g in range(N_EXPERT_GROUPS):
        sc, j1, j2 = _top2(sel[g * EXPERTS_PER_GROUP:(g + 1) * EXPERTS_PER_GROUP, :])
        if g == 0:
            best, gi, i1, i2 = sc, jnp.zeros_like(j1), j1, j2
        else:
            upd = sc > best
            best = jnp.where(upd, sc, best)
            gi = jnp.where(upd, g, gi)
            i1 = jnp.where(upd, j1, i1)
            i2 = jnp.where(upd, j2, i2)
    e0 = gi * EXPERTS_PER_GROUP + i1
    e1 = gi * EXPERTS_PER_GROUP + i2
    io = lax.broadcasted_iota(I32, (N_EXPERTS, tm), 0)
    oh0 = io == e0
    oh1 = io == e1
    a0 = jnp.sum(jnp.where(oh0, aff, 0.0), axis=0, keepdims=True)
    a1 = jnp.sum(jnp.where(oh1, aff, 0.0), axis=0, keepdims=True)
    den = a0 + a1
    g0 = a0 / den
    g1 = a1 / den
    oh = jnp.where(oh0 | oh1, 1.0, 0.0)
    tr = lax.broadcasted_iota(I32, (tm, tm), 0)
    tc = lax.broadcasted_iota(I32, (tm, tm), 1)
    before = jnp.dot(oh.astype(BF16), (tr < tc).astype(BF16), preferred_element_type=F32)
    rank0 = jnp.sum(jnp.where(oh0, before, 0.0), axis=0, keepdims=True)
    rank1 = jnp.sum(jnp.where(oh1, before, 0.0), axis=0, keepdims=True)
    zi = jnp.zeros((4, tm), I32)
    idx_ref[...] = jnp.concatenate([e0, e1, rank0.astype(I32), rank1.astype(I32), zi], axis=0)
    gate_ref[...] = jnp.concatenate([g0, g1, jnp.zeros((6, tm), F32)], axis=0)
    gmat_ref[...] = jnp.where(oh0, g0, 0.0) + jnp.where(oh1, g1, 0.0)
    cnt_ref[...] = jnp.broadcast_to(jnp.sum(oh, axis=1, keepdims=True), cnt_ref.shape)


def _moe_route(x, sc, sh, router_w_t, router_b, *, per_b_mod, rows_out, tm):
    bsz, t, _ = x.shape
    n = bsz * t
    nt = t // tm
    row = pl.BlockSpec((None, tm, D), lambda b, i: (b, i, 0))
    mod = pl.BlockSpec((None, 1, D), lambda b, i: (b, 0, 0)) if per_b_mod else row
    if rows_out:
        h_spec = pl.BlockSpec((tm, D), lambda b, i: (b * nt + i, 0))
        h_shape = jax.ShapeDtypeStruct((n, D), BF16)
    else:
        h_spec, h_shape = row, jax.ShapeDtypeStruct(x.shape, F32)
    tok = lambda rows: pl.BlockSpec((rows, tm), lambda b, i: (0, b * nt + i))
    return pl.pallas_call(
        functools.partial(_moe_route_kernel, tm=tm),
        grid=(bsz, nt),
        in_specs=[row, mod, mod, pl.BlockSpec((N_EXPERTS, D), lambda b, i: (0, 0)),
                  pl.BlockSpec((N_EXPERTS, 1), lambda b, i: (0, 0))],
        out_specs=[h_spec, tok(8), tok(8), tok(N_EXPERTS),
                   pl.BlockSpec((None, N_EXPERTS, LANES), lambda b, i: (b * nt + i, 0, 0))],
        out_shape=[h_shape, jax.ShapeDtypeStruct((8, n), I32), jax.ShapeDtypeStruct((8, n), F32),
                   jax.ShapeDtypeStruct((N_EXPERTS, n), F32),
                   jax.ShapeDtypeStruct((bsz * nt, N_EXPERTS, LANES), F32)],
        compiler_params=_cp(("arbitrary", "arbitrary")),
        name="moe_route",
    )(x, sc, sh, router_w_t, router_b)


def _moe_dispatch_kernel(c8_ref, lo_ref, rs_ref, zs_ref, h_ref, idx_ref, tab_ref, xs_ref,
                         buf, zbuf, sem, *, tm):
    step = pl.program_id(0)
    last = pl.num_programs(0) - 1
    s = step % 2

    def runs(tile, sl, start):
        for e in range(N_EXPERTS):
            j = tile * N_EXPERTS + e

            @pl.when(c8_ref[j] > 0)
            def _():
                size = pl.multiple_of(c8_ref[j], 8)
                cp = pltpu.make_async_copy(buf.at[sl, pl.ds(pl.multiple_of(lo_ref[j], 8), size), :],
                                           xs_ref.at[pl.ds(pl.multiple_of(rs_ref[j], 8), size), :], sem.at[sl])
                if start:
                    cp.start()
                else:
                    cp.wait()

    @pl.when(step == 0)
    def _():
        zbuf[...] = jnp.zeros_like(zbuf)

        def zero_block(first_slot):
            return pltpu.make_async_copy(zbuf, xs_ref.at[pl.ds(pl.multiple_of(first_slot, 8), MOE_BLK), :],
                                         sem.at[2])

        fresh = [zs_ref[e] != zs_ref[e - 1] for e in range(1, N_EXPERTS)]
        zero_block(zs_ref[0]).start()
        for e, f in enumerate(fresh, start=1):
            pl.when(f)(lambda e=e: zero_block(zs_ref[e]).start())
        zero_block(zs_ref[0]).wait()
        for e, f in enumerate(fresh, start=1):
            pl.when(f)(lambda e=e: zero_block(zs_ref[e]).wait())

        used, n_blocks = zs_ref[N_EXPERTS], xs_ref.shape[0] // MOE_BLK

        def tail_start(j, c):
            zero_block(j * MOE_BLK).start()
            return c

        def tail_wait(j, c):
            zero_block(j * MOE_BLK).wait()
            return c

        lax.fori_loop(used, n_blocks, tail_start, 0)
        lax.fori_loop(used, n_blocks, tail_wait, 0)

    @pl.when(step >= 2)
    def _():
        runs(step - 2, s, start=False)

    e0, e1, l0, l1 = (idx_ref[r:r + 1, :] for r in range(4))
    expert = lax.broadcasted_iota(I32, (N_EXPERTS, tm), 0)

    def lookup(col, e):
        return jnp.sum(jnp.where(expert == e, col, 0.0), axis=0, keepdims=True).astype(I32)

    lo_col = tab_ref[:, 0:1].astype(F32)
    pos0 = lookup(lo_col, e0) + l0
    pos1 = lookup(lo_col, e1) + l1
    row = lax.broadcasted_iota(I32, (buf.shape[1], tm), 0)
    perm = ((row == pos0) | (row == pos1)).astype(BF16)
    buf[s] = _pack_pairs(jnp.dot(perm, h_ref[...].astype(BF16), preferred_element_type=F32))
    runs(step, s, start=True)

    @pl.when(step == last)
    def _():
        @pl.when(step >= 1)
        def _():
            runs(step - 1, 1 - s, start=False)
        runs(step, s, start=False)


def _moe_dispatch(h_rows, idx, c8, loff, run_start, zero_start, n_slots, tm):
    n = h_rows.shape[0]
    tiles = n // tm
    buf_rows = 2 * tm + N_EXPERTS * 8
    tab = jnp.pad(loff[:, :, None], ((0, 0), (0, 0), (0, LANES - 1)))
    flat = lambda a: a.reshape(-1).astype(I32)
    return pl.pallas_call(
        functools.partial(_moe_dispatch_kernel, tm=tm),
        grid_spec=pltpu.PrefetchScalarGridSpec(
            num_scalar_prefetch=4, grid=(tiles,),
            in_specs=[pl.BlockSpec((tm, D), lambda i, *_: (i, 0)),
                      pl.BlockSpec((8, tm), lambda i, *_: (0, i)),
                      pl.BlockSpec((None, N_EXPERTS, LANES), lambda i, *_: (i, 0, 0))],
            out_specs=pl.BlockSpec(memory_space=pl.ANY),
            scratch_shapes=[pltpu.VMEM((2, buf_rows, D // 2), U32), pltpu.VMEM((MOE_BLK, D // 2), U32),
                            pltpu.SemaphoreType.DMA((3,))]),
        out_shape=jax.ShapeDtypeStruct((n_slots, D // 2), U32),
        compiler_params=_cp(("arbitrary",)),
        name="moe_dispatch",
    )(flat(c8), flat(loff), flat(run_start), zero_start, h_rows, idx, tab)


def _moe_expert_kernel(be_ref, used_ref, nxt_ref, xs_ref, win_hbm, wout_hbm, ys_ref,
                       win_f, wout_f, win_bf, wout_bf, run_ref, sem, *, layer):
    i = pl.program_id(0)
    used = used_ref[0]

    def weight_copies(e, s):
        return (pltpu.make_async_copy(win_hbm.at[layer, e], win_f.at[s], sem.at[s, 0]),
                pltpu.make_async_copy(wout_hbm.at[layer, e], wout_f.at[s], sem.at[s, 1]))

    @pl.when(i == 0)
    def _():
        run_ref[0] = 0
        for cp in weight_copies(be_ref[0], 0):
            cp.start()

    fresh = (i == 0) | (be_ref[i] != be_ref[jnp.maximum(i - 1, 0)])

    @pl.when(fresh & (i < used))
    def _():
        run = run_ref[0] + jnp.where(i > 0, 1, 0)
        run_ref[0] = run
        s = run % 2
        for cp in weight_copies(be_ref[i], s):
            cp.wait()
        win_bf[...] = win_f[s].astype(BF16)
        wout_bf[...] = wout_f[s].astype(BF16)

        @pl.when(nxt_ref[i] >= 0)
        def _():
            for cp in weight_copies(nxt_ref[i], 1 - s):
                cp.start()

    @pl.when(i < used)
    def _():
        hmid = jnp.dot(_unpack_pairs(xs_ref[...]), win_bf[...], preferred_element_type=F32)
        gt = hmid[:, :D_EXPERT]
        up = hmid[:, D_EXPERT:]
        act = gt * _sigmoid(gt) * up
        ys_ref[...] = _pack_pairs(jnp.dot(act.astype(BF16), wout_bf[...], preferred_element_type=F32))

    @pl.when(i >= used_ref[0])
    def _():
        ys_ref[...] = jnp.zeros_like(ys_ref)


def _moe_experts_sorted(xs, blk_exp, n_used, w_in, w_out, layer):
    n_slots = xs.shape[0]
    n_blocks = n_slots // MOE_BLK
    blk = jnp.arange(n_blocks, dtype=I32)
    later = (blk[None, :] > blk[:, None]) & (blk_exp[None, :] != blk_exp[:, None]) & (blk[None, :] < n_used[0])
    first_later = jnp.min(jnp.where(later, blk[None, :], n_blocks), axis=1)
    nxt = jnp.where(first_later < n_blocks, blk_exp[jnp.minimum(first_later, n_blocks - 1)], -1).astype(I32)
    x_map = lambda i, be, u, nx: (jnp.minimum(i, u[0] - 1), 0)
    return pl.pallas_call(
        functools.partial(_moe_expert_kernel, layer=layer),
        grid_spec=pltpu.PrefetchScalarGridSpec(
            num_scalar_prefetch=3, grid=(n_blocks,),
            in_specs=[pl.BlockSpec((MOE_BLK, D // 2), x_map),
                      pl.BlockSpec(memory_space=pl.ANY), pl.BlockSpec(memory_space=pl.ANY)],
            out_specs=pl.BlockSpec((MOE_BLK, D // 2), lambda i, be, u, nx: (i, 0)),
            scratch_shapes=[pltpu.VMEM((2, D, 2 * D_EXPERT), F32), pltpu.VMEM((2, D_EXPERT, D), F32),
                            pltpu.VMEM((D, 2 * D_EXPERT), BF16), pltpu.VMEM((D_EXPERT, D), BF16),
                            pltpu.SMEM((1,), I32), pltpu.SemaphoreType.DMA((2, 2))]),
        out_shape=jax.ShapeDtypeStruct((n_slots, D // 2), U32),
        compiler_params=_cp(("arbitrary",)),
        name="moe_experts_sorted",
    )(blk_exp, n_used, nxt, xs, w_in, w_out)


def _moe_combine_kernel(c8_ref, lo_ref, rs_ref, ys_ref, idx_ref, tab_ref, gate_ref, x_ref, gt_ref, vec_ref,
                        o_ref, buf, sem, *, tm):
    step = pl.program_id(0)
    s = step % 2

    def runs(tile, sl, start):
        for e in range(N_EXPERTS):
            j = tile * N_EXPERTS + e

            @pl.when(c8_ref[j] > 0)
            def _():
                size = pl.multiple_of(c8_ref[j], 8)
                cp = pltpu.make_async_copy(ys_ref.at[pl.ds(pl.multiple_of(rs_ref[j], 8), size), :],
                                           buf.at[sl, pl.ds(pl.multiple_of(lo_ref[j], 8), size), :], sem.at[sl])
                if start:
                    cp.start()
                else:
                    cp.wait()

    @pl.when(step == 0)
    def _():
        buf[...] = jnp.zeros_like(buf)
        runs(0, 0, start=True)

    @pl.when(step + 1 < pl.num_programs(0))
    def _():
        runs(step + 1, 1 - s, start=True)

    runs(step, s, start=False)
    e0, e1, l0, l1 = (idx_ref[r:r + 1, :] for r in range(4))
    expert = lax.broadcasted_iota(I32, (N_EXPERTS, tm), 0)
    lo_col = tab_ref[:, 0:1].astype(F32)

    def position(e, l):
        return jnp.sum(jnp.where(expert == e, lo_col, 0.0), axis=0, keepdims=True).astype(I32) + l

    row = lax.broadcasted_iota(I32, (buf.shape[1], tm), 0)
    y_sorted = _unpack_pairs(buf[s])
    ya = _bdot_tn((row == position(e0, l0)).astype(BF16), y_sorted)
    yb = _bdot_tn((row == position(e1, l1)).astype(BF16), y_sorted)
    ff = ya * gate_ref[:, 0:1] + yb * gate_ref[:, 1:2]
    o_ref[...] = _layer_norm(DN_ALPHA * x_ref[...] + gt_ref[...] * ff, vec_ref[0:1, :], vec_ref[1:2, :])


def _moe_combine(ys, idx, c8, loff, run_start, gates, x, gt, vec, tm):
    bsz, t, _ = x.shape
    nt = t // tm
    buf_rows = 2 * tm + N_EXPERTS * 8
    tab = jnp.pad(loff[:, :, None], ((0, 0), (0, 0), (0, LANES - 1)))
    flat = lambda a: a.reshape(-1).astype(I32)
    return pl.pallas_call(
        functools.partial(_moe_combine_kernel, tm=tm),
        grid_spec=pltpu.PrefetchScalarGridSpec(
            num_scalar_prefetch=3, grid=(bsz * nt,),
            in_specs=[pl.BlockSpec(memory_space=pl.ANY),
                      pl.BlockSpec((8, tm), lambda i, *_: (0, i)),
                      pl.BlockSpec((None, N_EXPERTS, LANES), lambda i, *_: (i, 0, 0)),
                      pl.BlockSpec((tm, 2), lambda i, *_: (i, 0)),
                      pl.BlockSpec((None, tm, D), lambda i, *_: (i // nt, i % nt, 0)),
                      pl.BlockSpec((None, 1, D), lambda i, *_: (i // nt, 0, 0)),
                      pl.BlockSpec((2, D), lambda i, *_: (0, 0))],
            out_specs=pl.BlockSpec((None, tm, D), lambda i, *_: (i // nt, i % nt, 0)),
            scratch_shapes=[pltpu.VMEM((2, buf_rows, D // 2), U32), pltpu.SemaphoreType.DMA((2,))]),
        out_shape=jax.ShapeDtypeStruct(x.shape, F32),
        compiler_params=_cp(("arbitrary",)),
        name="moe_combine",
    )(flat(c8), flat(loff), flat(run_start), ys, idx, tab, gates, x, gt, vec)


def _moe_sorted(x, sc, sh, gt, vec, w, layer, tm=256):
    bsz, t, _ = x.shape
    n = bsz * t
    tiles = n // tm
    n_blocks = -(-(2 * n + tiles * N_EXPERTS * 7 + N_EXPERTS * (MOE_BLK - 1)) // MOE_BLK)
    h_rows, idx, gate, _, cnt = _moe_route(x, sc, sh, w["router_w_t"], w["router_b"], per_b_mod=True,
                                           rows_out=True, tm=tm)
    c8 = (cnt[:, :, 0].astype(I32) + 7) // 8 * 8
    padded = (jnp.sum(c8, axis=0) + MOE_BLK - 1) // MOE_BLK * MOE_BLK
    pad_end = jnp.cumsum(padded)
    run_start = (pad_end - padded)[None, :] + jnp.cumsum(c8, axis=0) - c8
    loff = jnp.cumsum(c8, axis=1) - c8
    blk_start = jnp.arange(n_blocks, dtype=I32) * MOE_BLK
    blk_exp = jnp.minimum(jnp.sum((pad_end[None, :] <= blk_start[:, None]).astype(I32), axis=1), N_EXPERTS - 1)
    n_used = (pad_end[-1:] // MOE_BLK).astype(I32)
    zero_start = jnp.concatenate([jnp.maximum(pad_end - MOE_BLK, 0), n_used])
    xs = _moe_dispatch(h_rows, idx, c8, loff, run_start, zero_start, n_blocks * MOE_BLK, tm)
    ys = _moe_experts_sorted(xs, blk_exp, n_used, w["moe_w_in"], w["moe_w_out"], layer)
    return _moe_combine(ys, idx, c8, loff, run_start, gate[:2].T, x, gt, vec, tm)


def _moe_dense_kernel(h_ref, gm_ref, win_ref, wout_ref, x_ref, gt_ref, vec_ref, o_ref, acc_ref):
    e = pl.program_id(0)

    @pl.when(e == 0)
    def _():
        acc_ref[...] = jnp.zeros_like(acc_ref)

    lane = lax.broadcasted_iota(I32, gm_ref.shape, 1)
    gcol = jnp.sum(jnp.where(lane == e, gm_ref[...], 0.0), axis=1, keepdims=True)
    hmid = _bdot(h_ref[...], win_ref[...])
    gt = hmid[:, :D_EXPERT]
    up = hmid[:, D_EXPERT:]
    y = _bdot(gt * _sigmoid(gt) * up, wout_ref[...])
    acc_ref[...] = acc_ref[...] + jnp.where(gcol != 0.0, gcol * y, 0.0)

    @pl.when(e == N_EXPERTS - 1)
    def _():
        o_ref[...] = _layer_norm(DN_ALPHA * x_ref[...] + gt_ref[...] * acc_ref[...], vec_ref[0:1, :],
                                 vec_ref[1:2, :])


def _moe_dense(x, sc, sh, gt, vec, w, layer):
    _, m, _ = x.shape
    h, _, _, gmat, _ = _moe_route(x, sc, sh, w["router_w_t"], w["router_b"], per_b_mod=False, rows_out=False,
                                  tm=m)
    full = pl.BlockSpec((m, D), lambda e: (0, 0))
    out = pl.pallas_call(
        _moe_dense_kernel,
        grid=(N_EXPERTS,),
        in_specs=[full, pl.BlockSpec((m, N_EXPERTS), lambda e: (0, 0)),
                  pl.BlockSpec((None, None, D, 2 * D_EXPERT), lambda e: (layer, e, 0, 0)),
                  pl.BlockSpec((None, None, D_EXPERT, D), lambda e: (layer, e, 0, 0)),
                  full, full, pl.BlockSpec((2, D), lambda e: (0, 0))],
        out_specs=full,
        out_shape=jax.ShapeDtypeStruct((m, D), F32),
        scratch_shapes=[pltpu.VMEM((m, D), F32)],
        compiler_params=_cp(("arbitrary",)),
        name="moe_dense",
    )(h[0], gmat.T, w["moe_w_in"], w["moe_w_out"], x[0], gt[0], vec)
    return out[None]


def _rotary(x, cos_t, sa_t, sb_t):
    reps = x.shape[1] // LANES
    tile = lambda t: t if reps == 1 else jnp.concatenate([t] * reps, axis=1)
    n = x.shape[1]
    half = ROT_DIM // 2
    return x * tile(cos_t) + pltpu.roll(x, n - half, axis=1) * tile(sa_t) + pltpu.roll(x, half, axis=1) * tile(sb_t)


def _class_rows(c, rows, dil):
    return pl.ds(c, rows, stride=dil) if dil > 1 else pl.ds(0, rows)


def _class_major_perm(tm, dil, inverse=False):
    cm_row = lax.broadcasted_iota(I32, (tm, tm), 1 if inverse else 0)
    tok_row = lax.broadcasted_iota(I32, (tm, tm), 0 if inverse else 1)
    rows = tm // dil
    return ((cm_row // rows == tok_row % dil) & (cm_row % rows == tok_row // dil)).astype(BF16)


def _attn_pre_kernel(x_ref, sc_ref, sh_ref, ksc_ref, ksh_ref, rot_ref, wq_ref, wkv_ref, *refs, dils, tm):
    x = x_ref[...]
    cos_t, sa_t, sb_t = rot_ref[0], rot_ref[1], rot_ref[2]
    q = _rotary(_bdot(x * (1.0 + sc_ref[...]) + sh_ref[...], wq_ref[...]), cos_t, sa_t, sb_t) * HD ** -0.5
    kv = _bdot(x * (1.0 + ksc_ref[...]) + ksh_ref[...], wkv_ref[...])
    k = _rotary(kv[:, :KVW], cos_t, sa_t, sb_t)
    v = kv[:, KVW:]
    k_ref, v_ref = refs[0], refs[1]
    k_ref[...] = k
    v_ref[...] = v
    if dils is None:
        refs[2][...] = q
        return
    ng = len(dils)
    q_refs, kb_refs, vb_refs = refs[2:2 + ng], refs[2 + ng:2 + 2 * ng], refs[2 + 2 * ng:2 + 3 * ng]
    qb, kb, vb = q.astype(BF16), k.astype(BF16), v.astype(BF16)
    for g, dil in enumerate(dils):
        rows = tm // dil
        blk = jnp.concatenate([qb[:, g * D:(g + 1) * D], kb, vb], axis=1)
        if dil > 1:
            blk = jnp.dot(_class_major_perm(tm, dil), blk, preferred_element_type=F32).astype(BF16)
        for c in range(dil):
            rs = slice(c * rows, (c + 1) * rows)
            q_refs[g][c] = blk[rs, :D]
            kb_refs[g][c] = blk[rs, D:D + KVW]
            vb_refs[g][c] = blk[rs, D + KVW:]


def _attn_pre(x, sc, sh, ksc, ksh, rot, w_q, w_kv, *, per_b_mod, tm, dils=None):
    bsz, t, _ = x.shape
    row = lambda wd: pl.BlockSpec((None, tm, wd), lambda b, i: (b, i, 0))
    mod = pl.BlockSpec((None, 1, D), lambda b, i: (b, 0, 0)) if per_b_mod else row(D)
    nq = w_q.shape[1]
    out_specs = [row(KVW), row(KVW)]
    out_shape = [jax.ShapeDtypeStruct((bsz, t, KVW), F32)] * 2
    scratch = []
    if dils is None:
        out_specs.append(row(nq))
        out_shape.append(jax.ShapeDtypeStruct((bsz, t, nq), F32))
    else:
        for wd in (D, KVW, KVW):
            for dil in dils:
                out_specs.append(pl.BlockSpec((None, dil, tm // dil, wd), lambda b, i: (b, 0, i, 0)))
                out_shape.append(jax.ShapeDtypeStruct((bsz, dil, t // dil, wd), BF16))
    outs = pl.pallas_call(
        functools.partial(_attn_pre_kernel, dils=dils, tm=tm),
        grid=(bsz, t // tm),
        in_specs=[row(D), mod, mod, mod, mod, pl.BlockSpec((3, tm, LANES), lambda b, i: (0, i, 0)),
                  pl.BlockSpec((D, nq), lambda b, i: (0, 0)), pl.BlockSpec((D, 2 * KVW), lambda b, i: (0, 0))],
        out_specs=out_specs,
        out_shape=out_shape,
        scratch_shapes=scratch,
        compiler_params=_cp(("arbitrary", "arbitrary")),
        name="attn_pre",
    )(x, sc, sh, ksc, ksh, rot, w_q, w_kv)
    if dils is None:
        return outs
    ng = len(dils)
    return outs[0], outs[1], outs[2:2 + ng], outs[2 + ng:2 + 2 * ng], outs[2 + 2 * ng:]


def _attn_band_kernel(q_ref, ka_ref, kb_ref, va_ref, vb_ref, o_ref, lse_ref):
    i = pl.program_id(2)
    bq = ATT_WIN
    kcat = jnp.concatenate([ka_ref[...], kb_ref[...]], axis=0)
    vcat = jnp.concatenate([va_ref[...], vb_ref[...]], axis=0)
    qs = jnp.concatenate([q_ref[:, rep * KVW:(rep + 1) * KVW] for rep in range(REP)], axis=0)
    qrow = lax.broadcasted_iota(I32, (REP * bq, 2 * bq), 0) % bq
    kcol = lax.broadcasted_iota(I32, (REP * bq, 2 * bq), 1)
    valid = (kcol >= qrow) & (kcol <= qrow + bq) & ((i > 0) | (kcol >= bq))
    lane = lax.broadcasted_iota(I32, (REP * bq, KVW), 1) // HD
    kv_lane = lax.broadcasted_iota(I32, (2 * bq, KVW), 1) // HD
    lse_lane = lax.broadcasted_iota(I32, (bq, LANES), 1)
    lse = jnp.zeros((bq, LANES), F32)
    linv = jnp.ones((REP * bq, KVW), F32)
    ps = []
    for h in range(KV_HEADS):
        kh = jnp.where(kv_lane == h, kcat, jnp.zeros_like(kcat))
        s = lax.dot_general(qs, kh, (((1,), (1,)), ((), ())), preferred_element_type=F32)
        s = jnp.where(valid, s, -jnp.inf)
        m = jnp.max(s, axis=-1, keepdims=True)
        p = jnp.exp(s - m)
        l = jnp.sum(p, axis=-1, keepdims=True)
        ps.append(p.astype(BF16))
        linv = jnp.where(lane == h, 1.0 / l, linv)
        lse_h = m + jnp.log(l)
        for rep in range(REP):
            lse = jnp.where(lse_lane == rep * KV_HEADS + h, lse_h[rep * bq:(rep + 1) * bq], lse)
    vstack = jnp.concatenate([jnp.where(kv_lane == h, vcat, jnp.zeros_like(vcat)) for h in range(KV_HEADS)],
                             axis=0)
    o = (jnp.dot(jnp.concatenate(ps, axis=1), vstack, preferred_element_type=F32) * linv).astype(BF16)
    for rep in range(REP):
        o_ref[:, rep * KVW:(rep + 1) * KVW] = o[rep * bq:(rep + 1) * bq]
    lse_ref[...] = lse


def _attn_band(q, kb, vb, dil):
    bsz, _, tc, _ = q.shape
    nb = tc // ATT_WIN
    blk = lambda wd: pl.BlockSpec((None, None, ATT_WIN, wd), lambda b, c, i: (b, c, i, 0))
    prev = pl.BlockSpec((None, None, ATT_WIN, KVW), lambda b, c, i: (b, c, jnp.maximum(i - 1, 0), 0))
    return pl.pallas_call(
        _attn_band_kernel,
        grid=(bsz, dil, nb),
        in_specs=[blk(D), prev, blk(KVW), prev, blk(KVW)],
        out_specs=[blk(D), blk(LANES)],
        out_shape=[jax.ShapeDtypeStruct((bsz, dil, tc, D), BF16), jax.ShapeDtypeStruct((bsz, dil, tc, LANES), F32)],
        compiler_params=_cp(("arbitrary", "arbitrary", "arbitrary")),
        name=f"attn_band_d{dil}",
    )(q, kb, kb, vb, vb)


def _attn_step_kernel(q_ref, kn_ref, vn_ref, kt_ref, vt_ref, o_ref, lse_ref, *, wbuf):
    nq = len(DILATED_GROUPS) * REP
    row = lax.broadcasted_iota(I32, (nq, wbuf), 0)
    pos = lax.broadcasted_iota(I32, (nq, wbuf), 1)
    valid = None
    for g, (_, dil) in enumerate(DILATED_GROUPS):
        ok = (row // REP == g) & (pos >= wbuf - ATT_WIN * dil) & (pos % dil == 0)
        valid = ok if valid is None else valid | ok
    rnd = lambda t: t.astype(BF16).astype(F32)
    lse_lane = lax.broadcasted_iota(I32, (nq, LANES), 1)
    lse = jnp.zeros((nq, LANES), F32)
    for h in range(KV_HEADS):
        hs = slice(h * HD, (h + 1) * HD)
        qh = q_ref[:, hs]
        s = jnp.where(valid, _bdot(qh, kt_ref[h]), -1e30)
        s_n = jnp.sum(rnd(qh) * rnd(kn_ref[:, hs]), axis=-1, keepdims=True)
        m = jnp.maximum(jnp.max(s, axis=-1, keepdims=True), s_n)
        p = jnp.exp(s - m)
        p_n = jnp.exp(s_n - m)
        l = jnp.sum(p, axis=-1, keepdims=True) + p_n
        o_ref[:, hs] = (_bdot_nt(p, vt_ref[h]) + rnd(p_n) * rnd(vn_ref[:, hs])) / l
        lse = jnp.where(lse_lane == h, m + jnp.log(l), lse)
    lse_ref[...] = lse


def _attn_step(q, k_new, v_new, cache_k, cache_v):
    bsz = q.shape[0]
    wbuf = cache_k.shape[1]
    ng = len(DILATED_GROUPS)
    assert all(wbuf % dil == 0 and wbuf >= ATT_WIN * dil for _, dil in DILATED_GROUPS)
    nq = ng * REP
    cache_spec = pl.BlockSpec((None, KV_HEADS, HD, wbuf), lambda b: (b, 0, 0, 0))
    o, lse = pl.pallas_call(
        functools.partial(_attn_step_kernel, wbuf=wbuf),
        grid=(bsz,),
        in_specs=[pl.BlockSpec((None, nq, KVW), lambda b: (b, 0, 0)),
                  pl.BlockSpec((None, 1, KVW), lambda b: (b, 0, 0)),
                  pl.BlockSpec((None, 1, KVW), lambda b: (b, 0, 0)), cache_spec, cache_spec],
        out_specs=[pl.BlockSpec((None, nq, KVW), lambda b: (b, 0, 0)),
                   pl.BlockSpec((None, nq, LANES), lambda b: (b, 0, 0))],
        out_shape=[jax.ShapeDtypeStruct((bsz, nq, KVW), F32), jax.ShapeDtypeStruct((bsz, nq, LANES), F32)],
        compiler_params=_cp(("arbitrary",)),
        name="attn_step",
    )(q.reshape(bsz, nq, KVW), k_new.reshape(bsz, 1, KVW), v_new.reshape(bsz, 1, KVW),
      cache_k.transpose(0, 2, 3, 1), cache_v.transpose(0, 2, 3, 1))
    o = o.reshape(bsz, ng, D).transpose(1, 0, 2)
    lse = lse[:, :, :KV_HEADS].reshape(bsz, ng, REP * KV_HEADS).transpose(1, 0, 2)
    return o, jnp.pad(lse, ((0, 0), (0, 0), (0, LANES - REP * KV_HEADS)))


def _attn_post_kernel(o0_ref, o1_ref, o2_ref, l0_ref, l1_ref, l2_ref, x_ref, gt_ref, vec_ref, wo_ref, out_ref,
                      *scratch, dils, tm):
    o_refs = [o0_ref, o1_ref, o2_ref]
    l_refs = [l0_ref, l1_ref, l2_ref]
    if dils is not None:
        for g, dil in enumerate(dils):
            if dil == 1:
                o_refs[g], l_refs[g] = o_refs[g].at[0], l_refs[g].at[0]
                continue
            sl = scratch[g]
            for c in range(dil):
                sl[_class_rows(c, tm // dil, dil), :] = l_refs[g][c]
            l_refs[g] = sl
            o_cm = jnp.concatenate([o_refs[g][c] for c in range(dil)], axis=0)
            o_refs[g] = jnp.dot(_class_major_perm(tm, dil, inverse=True), o_cm, preferred_element_type=F32)
    o0, o1, o2 = (r if isinstance(r, jax.Array) else r[...] for r in o_refs)
    l0, l1, l2 = (r[...] for r in l_refs)
    m = jnp.maximum(jnp.maximum(l0, l1), l2)
    e0, e1, e2 = jnp.exp(l0 - m), jnp.exp(l1 - m), jnp.exp(l2 - m)
    den = e0 + e1 + e2
    r = lax.broadcasted_iota(I32, (LANES, D), 0)
    c = lax.broadcasted_iota(I32, (LANES, D), 1) // HD
    spread = (r == c).astype(BF16)
    mix = (_split_dot(e0 / den, spread) * o0 + _split_dot(e1 / den, spread) * o1
           + _split_dot(e2 / den, spread) * o2)
    out = _bdot(mix, wo_ref[...])
    out_ref[...] = _layer_norm(DN_ALPHA * x_ref[...] + gt_ref[...] * out, vec_ref[0:1, :], vec_ref[1:2, :])


def _attn_post(os, lses, x, gt, vec, w_o, *, per_b_mod, tm, dils=None):
    bsz, t, _ = x.shape
    row = lambda wd: pl.BlockSpec((None, tm, wd), lambda b, i: (b, i, 0))
    mod = pl.BlockSpec((None, 1, D), lambda b, i: (b, 0, 0)) if per_b_mod else row(D)
    if dils is None:
        o_specs, l_specs, scratch = [row(D)] * 3, [row(LANES)] * 3, []
    else:
        cls = lambda dil, wd: pl.BlockSpec((None, dil, tm // dil, wd), lambda b, i: (b, 0, i, 0))
        o_specs = [cls(dil, D) for dil in dils]
        l_specs = [cls(dil, LANES) for dil in dils]
        scratch = [pltpu.VMEM((tm, LANES), F32) for _ in dils]
    return pl.pallas_call(
        functools.partial(_attn_post_kernel, dils=dils, tm=tm),
        grid=(bsz, t // tm),
        in_specs=o_specs + l_specs + [row(D), mod, pl.BlockSpec((2, D), lambda b, i: (0, 0)),
                                      pl.BlockSpec((D, D), lambda b, i: (0, 0))],
        out_specs=row(D),
        out_shape=jax.ShapeDtypeStruct(x.shape, F32),
        scratch_shapes=scratch,
        compiler_params=_cp(("arbitrary", "arbitrary")),
        name="attn_post",
    )(*os, *lses, x, gt, vec, w_o)


def _rot_tables(pos):
    half = ROT_DIM // 2
    inv = ROPE_THETA ** (-jnp.arange(half, dtype=F32) * 2.0 / ROT_DIM)
    ang = pos.astype(F32)[:, None] * inv[None, :]
    cos, sin = jnp.cos(ang), jnp.sin(ang)
    t = pos.shape[0]
    one = jnp.ones((t, HD - ROT_DIM), F32)
    zero = jnp.zeros((t, HD - ROT_DIM), F32)
    zh = jnp.zeros((t, half), F32)
    cos_t = jnp.concatenate([cos, cos, one], axis=1)
    sa_t = jnp.concatenate([-sin, zh, zero], axis=1)
    sb_t = jnp.concatenate([zh, sin, zero], axis=1)
    return jnp.stack([jnp.tile(x, (1, 2)) for x in (cos_t, sa_t, sb_t)])


def _prep_weights(p):
    pad_c = lambda m, n: jnp.pad(m, ((0, 0), (0, n - m.shape[1])))
    pad_r = lambda m, n: jnp.pad(m, ((0, n - m.shape[0]), (0, 0)))
    w = {}
    w["mu"] = p["rwkv_mu"][0]
    w["w_rkv"] = p["rwkv_w_rkv"][0].astype(BF16)
    w["w1"] = pad_c(p["rwkv_w1"][0], LANES).astype(BF16)
    w["w2"] = pad_r(p["rwkv_w2"][0], LANES).astype(BF16)
    w["a1"] = pad_c(p["rwkv_a1"][0], LANES).astype(BF16)
    w["a2"] = pad_r(p["rwkv_a2"][0], LANES).astype(BF16)
    w["g1"] = pad_c(p["rwkv_g1"][0], 2 * LANES).astype(BF16)
    w["g2"] = pad_r(p["rwkv_g2"][0], 2 * LANES).astype(BF16)
    w["vec"] = jnp.stack([p["rwkv_w0"][0], p["rwkv_a0"][0], p["rwkv_k_k"][0], p["rwkv_k_a"][0],
                          p["rwkv_r_k"][0].reshape(D)])
    w["post_vec"] = jnp.stack([p["rwkv_lnx_g"][0], p["rwkv_lnx_b"][0], p["ln_g"][0, 0], p["ln_b"][0, 0]])
    w["rwkv_w_o"] = p["rwkv_w_o"][0].astype(BF16)
    wq = p["w_q"][0].reshape(D, 3, KV_HEADS, REP, HD).transpose(0, 1, 3, 2, 4).reshape(D, 3 * D)
    w["w_q"] = wq.astype(BF16)
    w["w_kv"] = p["w_kv"].astype(BF16)
    wo = p["w_o_attn"][0].reshape(KV_HEADS, REP, HD, D).transpose(1, 0, 2, 3).reshape(D, D)
    w["w_o_attn"] = wo.astype(BF16)
    w["router_w_t"] = p["router_w"].T.astype(BF16)
    w["router_b"] = p["router_b"].reshape(N_EXPERTS, 1)
    w["moe_w_in"] = p["moe_w_in"]
    w["moe_w_out"] = p["moe_w_out"]
    w["ln"] = [[jnp.stack([p["ln_g"][l, i], p["ln_b"][l, i]]) for i in range(2)] for l in range(DEPTH)]
    return w


def _modulations(c_prompt, c_sample, p):
    nb = c_prompt.shape[0]
    c = jnp.concatenate([c_prompt, c_sample], axis=0)
    pad = (-c.shape[0]) % 8
    c = jnp.pad(c, ((0, pad), (0, 0)))
    m3 = _ada_linear(c, p["ada_w"].reshape(2 * DEPTH, D, 3 * D), p["ada_b"].reshape(2 * DEPTH, 3 * D))
    m2 = _ada_linear(c, p["kv_ada_w"][None], p["kv_ada_b"][None])[0]
    n_all = nb + c_sample.shape[0]

    def split(m, parts, lo, hi):
        return [m[lo:hi, i * D:(i + 1) * D] for i in range(parts)]

    out = {}
    for name, lo, hi in (("prompt", 0, nb), ("sample", nb, n_all)):
        out[name] = {"ada": [[split(m3[2 * l + i], 3, lo, hi) for i in range(2)] for l in range(DEPTH)],
                     "kv": split(m2, 2, lo, hi)}
    return out


def _trunk_prompt(x, mods, w):
    bsz, t, _ = x.shape
    per_b = lambda m: m[:, None, :]
    sh, sc, gt = (per_b(m) for m in mods["ada"][0][0])
    zeros = jnp.zeros((bsz, 1, D), F32)
    r, lw, k, v, a, b, g, bonus, last = _rwkv_pre(x, sc, sh, zeros, w, seq_mode=True, tm=256)
    y, zf = _wkv_chunked(r, lw, k, v, a, b)
    x = _rwkv_post(y, bonus, g, x, gt, w["post_vec"], w["rwkv_w_o"], pairs=True, tm=256)
    zf = zf.reshape(bsz, PAIRS, 2, HD, 2, HD)
    wkv = jnp.stack([zf[:, :, 0, :, 0, :], zf[:, :, 1, :, 1, :]], axis=2)
    wkv = wkv.reshape(bsz, HEADS, HD, HD).transpose(0, 1, 3, 2)
    sh2, sc2, gt2 = (per_b(m) for m in mods["ada"][0][1])
    x = _moe_sorted(x, sc2, sh2, gt2, w["ln"][0][1], w, 0)

    sh, sc, gt = (per_b(m) for m in mods["ada"][1][0])
    ksh, ksc = (per_b(m) for m in mods["kv"])
    rot = _rot_tables(jnp.arange(t, dtype=I32))
    dils = tuple(dil for _, dil in DILATED_GROUPS)
    k_new, v_new, qs, kbs, vbs = _attn_pre(x, sc, sh, ksc, ksh, rot, w["w_q"], w["w_kv"], per_b_mod=True,
                                           tm=256, dils=dils)
    os, lses = zip(*[_attn_band(qs[g], kbs[g], vbs[g], dil) for g, dil in enumerate(dils)])
    x = _attn_post(os, lses, x, gt, w["ln"][1][0], w["w_o_attn"], per_b_mod=True, tm=256, dils=dils)
    sh2, sc2, gt2 = (per_b(m) for m in mods["ada"][1][1])
    x = _moe_sorted(x, sc2, sh2, gt2, w["ln"][1][1], w, 1)
    return x, wkv[None], last.reshape(1, bsz, D), k_new, v_new


def _trunk_sample(x, mods, state_wkv, state_shift, cache_k, cache_v, w):
    bsz = x.shape[0]
    xs = x.reshape(1, bsz, D)
    row = lambda m: m[None]
    sh, sc, gt = (row(m) for m in mods["ada"][0][0])
    r, lw, k, v, a, b, g, bonus, hm = _rwkv_pre(xs, sc, sh, state_shift[0][None], w, seq_mode=False, tm=bsz)
    s_new, y = _wkv_step(state_wkv[0].transpose(1, 2, 3, 0), *(t[0].T for t in (r, lw, k, v, a, b)))
    s_new = s_new.transpose(3, 0, 1, 2)
    xs = _rwkv_post(y.T[None], bonus, g, xs, gt, w["post_vec"], w["rwkv_w_o"], pairs=False, tm=bsz)
    sh2, sc2, gt2 = (row(m) for m in mods["ada"][0][1])
    xs = _moe_dense(xs, sc2, sh2, gt2, w["ln"][0][1], w, 0)

    sh, sc, gt = (row(m) for m in mods["ada"][1][0])
    ksh, ksc = (row(m) for m in mods["kv"])
    rot = _rot_tables(jnp.full((bsz,), PAST_LEN, I32))
    k_new, v_new, q = _attn_pre(xs, sc, sh, ksc, ksh, rot, w["w_q"], w["w_kv"], per_b_mod=False, tm=bsz)
    o, lse = _attn_step(q[0], k_new[0], v_new[0], cache_k, cache_v)
    xs = _attn_post([o[i][None] for i in range(3)], [lse[i][None] for i in range(3)], xs, gt, w["ln"][1][0],
                    w["w_o_attn"], per_b_mod=False, tm=bsz)
    sh2, sc2, gt2 = (row(m) for m in mods["ada"][1][1])
    xs = _moe_dense(xs, sc2, sh2, gt2, w["ln"][1][1], w, 1)
    return (xs.reshape(bsz, 1, D), s_new[None], hm, k_new.reshape(bsz, 1, KV_HEADS, HD),
            v_new.reshape(bsz, 1, KV_HEADS, HD))


def kernel(x_prompt, x_sample, state_wkv, state_shift, cache_k, cache_v, c_prompt, c_sample, ada_w, ada_b, ln_g, ln_b, rwkv_mu, rwkv_w_rkv, rwkv_w0, rwkv_w1, rwkv_w2, rwkv_a0, rwkv_a1, rwkv_a2, rwkv_g1, rwkv_g2, rwkv_k_k, rwkv_k_a, rwkv_r_k, rwkv_lnx_g, rwkv_lnx_b, rwkv_w_o, w_q, w_kv, kv_ada_w, kv_ada_b, w_o_attn, router_w, router_b, moe_w_in, moe_w_out):
    p = {"ada_w": ada_w, "ada_b": ada_b, "ln_g": ln_g, "ln_b": ln_b, "rwkv_mu": rwkv_mu, "rwkv_w_rkv": rwkv_w_rkv,
         "rwkv_w0": rwkv_w0, "rwkv_w1": rwkv_w1, "rwkv_w2": rwkv_w2, "rwkv_a0": rwkv_a0, "rwkv_a1": rwkv_a1,
         "rwkv_a2": rwkv_a2, "rwkv_g1": rwkv_g1, "rwkv_g2": rwkv_g2, "rwkv_k_k": rwkv_k_k, "rwkv_k_a": rwkv_k_a,
         "rwkv_r_k": rwkv_r_k, "rwkv_lnx_g": rwkv_lnx_g, "rwkv_lnx_b": rwkv_lnx_b, "rwkv_w_o": rwkv_w_o,
         "w_q": w_q, "w_kv": w_kv, "kv_ada_w": kv_ada_w, "kv_ada_b": kv_ada_b, "w_o_attn": w_o_attn,
         "router_w": router_w, "router_b": router_b, "moe_w_in": moe_w_in, "moe_w_out": moe_w_out}
    w = _prep_weights(p)
    mods = _modulations(c_prompt, c_sample, p)
    bp, tp, _ = x_prompt.shape
    y_p, wkv_p, shift_p, k_p, v_p = _trunk_prompt(x_prompt, mods["prompt"], w)
    keep = min(PAST_LEN, tp)
    k_p = k_p[:, tp - keep:].reshape(bp, keep, KV_HEADS, HD)
    v_p = v_p[:, tp - keep:].reshape(bp, keep, KV_HEADS, HD)
    y_s, wkv_s, shift_s, k_s, v_s = _trunk_sample(x_sample, mods["sample"], state_wkv, state_shift, cache_k,
                                                  cache_v, w)
    return (y_p, y_s, wkv_p, shift_p, k_p, v_p, wkv_s, shift_s, k_s, v_s)
```
